```python
import jax, jax.numpy as jnp
from jax import lax
import numpy as np

D_MODEL = 1024
BATCH = 8
SEQ = 8192
DEPTH = 2

HEAD_DIM = 64
A_GROUPS = ((128, 1), (512, 4), (2048, 16))
N_GROUPS = len(A_GROUPS)
A_WIDTH = D_MODEL // 2
A_HEADS = A_WIDTH // HEAD_DIM
B_WIDTH = D_MODEL // 2
SC_WIDTH = 3
POOL_SIZES = (2, 4, 8, 16)
C_WIDTH = D_MODEL // 2
C_GROUP = C_WIDTH // len(POOL_SIZES)
D_WIDTH = D_MODEL // 2
D_CONV = 31
GATE_EVEN = A_WIDTH + B_WIDTH
GATE_ODD = C_WIDTH + D_WIDTH
EVEN_IN = 3 * N_GROUPS * A_WIDTH + 3 * B_WIDTH + GATE_EVEN
ODD_IN = C_WIDTH + 2 * D_WIDTH + GATE_ODD
ROT_DIM = HEAD_DIM // 4
ROPE_THETA = 500000.0
QBLK = 128
EPS = 1e-6
NEG = -1e30
N_EVEN = (DEPTH + 1) // 2
N_ODD = DEPTH // 2

kernel_name = "hybrid_dilated_attn_shortconv_pool_conformer"


def rms_norm(t, w):
    tf = t.astype(jnp.float32)
    tf = tf * lax.rsqrt(jnp.mean(tf * tf, axis=-1, keepdims=True) + EPS)
    return (tf * w.astype(jnp.float32)).astype(t.dtype)


def layer_norm(t, w, b):
    tf = t.astype(jnp.float32)
    mu = jnp.mean(tf, axis=-1, keepdims=True)
    var = jnp.mean(jnp.square(tf - mu), axis=-1, keepdims=True)
    y = (tf - mu) * lax.rsqrt(var + EPS)
    return (y * w.astype(jnp.float32) + b.astype(jnp.float32)).astype(t.dtype)


def rope_tables(positions):
    half = ROT_DIM // 2
    inv_freq = ROPE_THETA ** (-jnp.arange(half, dtype=jnp.float32) / half)
    ang = positions.astype(jnp.float32)[..., None] * inv_freq
    return jnp.cos(ang)[:, :, None, None, :], jnp.sin(ang)[:, :, None, None, :]


def apply_rope(t, cos, sin):
    half = ROT_DIM // 2
    t1 = t[..., :half].astype(jnp.float32)
    t2 = t[..., half:ROT_DIM].astype(jnp.float32)
    r1 = (t1 * cos - t2 * sin).astype(t.dtype)
    r2 = (t2 * cos + t1 * sin).astype(t.dtype)
    return jnp.concatenate([r1, r2, t[..., ROT_DIM:]], axis=-1)


def causal_dwconv(t, w):
    K, C = w.shape
    return lax.conv_general_dilated(
        t, w[:, None, :].astype(t.dtype), window_strides=(1,), padding=[(K - 1, 0)],
        dimension_numbers=('NWC', 'WIO', 'NWC'), feature_group_count=C)


def dilated_attention(q, k, v, window, dilation):
    Bsz, S, H, Dh = q.shape
    steps = window // dilation
    span = dilation * QBLK
    L = -(-S // span) * span
    n = L // dilation
    nb = n // QBLK

    def to_streams(t):
        t = jnp.pad(t, ((0, 0), (0, L - S), (0, 0), (0, 0)))
        t = t.reshape(Bsz, n, dilation, H, Dh).transpose(0, 2, 3, 1, 4)
        return t.reshape(Bsz, dilation, H, nb, QBLK, Dh)

    def with_prev(t):
        prev = jnp.pad(t, ((0, 0), (0, 0), (0, 0), (1, 0), (0, 0), (0, 0)))[:, :, :, :-1]
        return jnp.concatenate([prev, t], axis=-2)

    qb = to_streams(q)
    kc = with_prev(to_streams(k))
    vc = with_prev(to_streams(v))

    s = jnp.einsum('brhnqc,brhnkc->brhnqk', qb, kc).astype(jnp.float32) * (Dh ** -0.5)
    qi = jnp.arange(QBLK)[:, None] + QBLK
    kj = jnp.arange(2 * QBLK)[None, :]
    dist = qi - kj
    band = (dist >= 0) & (dist <= steps)
    blk = jnp.arange(nb)[:, None, None]
    mask = band[None] & ((blk > 0) | (kj[None] >= QBLK))
    s = jnp.where(mask, s, NEG)
    m = jnp.max(s, axis=-1, keepdims=True)
    p = jnp.exp(s - m)
    den = jnp.sum(p, axis=-1, keepdims=True)
    o = jnp.einsum('brhnqk,brhnkc->brhnqc', (p / den).astype(v.dtype), vc)
    lse = (m + jnp.log(den))[..., 0]

    o = o.reshape(Bsz, dilation, H, n, Dh).transpose(0, 3, 1, 2, 4).reshape(Bsz, L, H, Dh)[:, :S]
    lse = lse.reshape(Bsz, dilation, H, n).transpose(0, 3, 1, 2).reshape(Bsz, L, H)[:, :S]
    return o, lse


def causal_pool_minus_self(u):
    S = u.shape[1]
    cs = jnp.cumsum(u.astype(jnp.float32), axis=1)
    t = jnp.arange(S)
    outs = []
    for g, p in enumerate(POOL_SIZES):
        c = cs[:, :, g]
        prev = jnp.pad(c, ((0, 0), (p, 0), (0, 0)))[:, :S]
        cnt = jnp.minimum(t + 1, p).astype(jnp.float32)[None, :, None]
        outs.append((c - prev) / cnt - u[:, :, g].astype(jnp.float32))
    return jnp.stack(outs, axis=2).astype(u.dtype)


def even_layer(x, cos, sin, norm_w, w_in, q_norm_w, k_norm_w, conv_w, w_out):
    Bsz, S, _ = x.shape
    h = rms_norm(x, norm_w)
    proj = h @ w_in
    nA = N_GROUPS * A_WIDTH
    cuts = np.cumsum([nA, nA, nA, B_WIDTH, B_WIDTH, B_WIDTH]).tolist()
    q, k, v, bg, cg, hb, z = jnp.split(proj, cuts, axis=-1)
    shp = (Bsz, S, N_GROUPS, A_HEADS, HEAD_DIM)
    q = apply_rope(rms_norm(q.reshape(shp), q_norm_w), cos, sin)
    k = apply_rope(rms_norm(k.reshape(shp), k_norm_w), cos, sin)
    v = v.reshape(shp)
    outs, lses = [], []
    for g, (win, dil) in enumerate(A_GROUPS):
        o, l = dilated_attention(q[:, :, g], k[:, :, g], v[:, :, g], win, dil)
        outs.append(o)
        lses.append(l)
    wts = jax.nn.softmax(jnp.stack(lses, axis=0), axis=0)
    o_a = jnp.sum(wts[..., None] * jnp.stack(outs, axis=0).astype(jnp.float32), axis=0)
    o_a = o_a.astype(x.dtype).reshape(Bsz, S, A_WIDTH)
    y_b = bg * causal_dwconv(cg * hb, conv_w)
    u = jnp.concatenate([o_a, y_b], axis=-1) * jax.nn.silu(z)
    return x + u @ w_out


def odd_layer(x, norm_w, w_in, pool_w, pool_scale, dconv_w, dconv_b, ln_w, ln_b, w_out):
    Bsz, S, _ = x.shape
    h = rms_norm(x, norm_w)
    proj = h @ w_in
    cuts = np.cumsum([C_WIDTH, D_WIDTH, D_WIDTH]).tolist()
    uc, da, dg, z = jnp.split(proj, cuts, axis=-1)
    pooled = causal_pool_minus_self(uc.reshape(Bsz, S, len(POOL_SIZES), C_GROUP))
    y_c = jnp.einsum('bsgc,gcd->bsgd', pooled, pool_w).reshape(Bsz, S, C_WIDTH) * pool_scale
    gl = da * jax.nn.sigmoid(dg)
    c = causal_dwconv(gl, dconv_w) + dconv_b
    y_d = jax.nn.silu(layer_norm(c, ln_w, ln_b))
    u = jnp.concatenate([y_c, y_d], axis=-1) * jax.nn.silu(z)
    return x + u @ w_out


def _fwd_setup_inputs(seed: int = 0) -> dict:
    key = jax.random.key(seed)
    ks = jax.random.split(key, 20)
    f32 = jnp.float32
    nrm = lambda k, shape, scale: jax.random.normal(k, shape, f32) * scale
    x = jax.random.normal(ks[0], (BATCH, SEQ, D_MODEL), f32)
    offset = jax.random.randint(ks[1], (BATCH, 1), 0, 4096, dtype=jnp.int32)
    positions = offset + jnp.arange(SEQ, dtype=jnp.int32)[None, :]
    return {
        "x": x,
        "positions": positions,
        "e_norm_w": 1.0 + nrm(ks[2], (N_EVEN, D_MODEL), 0.02),
        "e_w_in": nrm(ks[3], (N_EVEN, D_MODEL, EVEN_IN), D_MODEL ** -0.5),
        "e_q_norm_w": 1.0 + nrm(ks[4], (N_EVEN, HEAD_DIM), 0.02),
        "e_k_norm_w": 1.0 + nrm(ks[5], (N_EVEN, HEAD_DIM), 0.02),
        "e_conv_w": nrm(ks[6], (N_EVEN, SC_WIDTH, B_WIDTH), SC_WIDTH ** -0.5),
        "e_w_out": nrm(ks[7], (N_EVEN, GATE_EVEN, D_MODEL), GATE_EVEN ** -0.5),
        "o_norm_w": 1.0 + nrm(ks[8], (N_ODD, D_MODEL), 0.02),
        "o_w_in": nrm(ks[9], (N_ODD, D_MODEL, ODD_IN), D_MODEL ** -0.5),
        "o_pool_w": nrm(ks[10], (N_ODD, len(POOL_SIZES), C_GROUP, C_GROUP), C_GROUP ** -0.5),
        "o_pool_scale": 1.0 + nrm(ks[11], (N_ODD, C_WIDTH), 0.02),
        "o_dconv_w": nrm(ks[12], (N_ODD, D_CONV, D_WIDTH), D_CONV ** -0.5),
        "o_dconv_b": nrm(ks[13], (N_ODD, D_WIDTH), 0.01),
        "o_ln_w": 1.0 + nrm(ks[14], (N_ODD, D_WIDTH), 0.02),
        "o_ln_b": nrm(ks[15], (N_ODD, D_WIDTH), 0.01),
        "o_w_out": nrm(ks[16], (N_ODD, GATE_ODD, D_MODEL), GATE_ODD ** -0.5),
    }


def _fwd_reference(x, positions, e_norm_w, e_w_in, e_q_norm_w, e_k_norm_w, e_conv_w, e_w_out,
              o_norm_w, o_w_in, o_pool_w, o_pool_scale, o_dconv_w, o_dconv_b, o_ln_w, o_ln_b,
              o_w_out):
    cos, sin = rope_tables(positions)
    for i in range(DEPTH):
        j = i // 2
        if i % 2 == 0:
            x = even_layer(x, cos, sin, e_norm_w[j], e_w_in[j], e_q_norm_w[j], e_k_norm_w[j],
                           e_conv_w[j], e_w_out[j])
        else:
            x = odd_layer(x, o_norm_w[j], o_w_in[j], o_pool_w[j], o_pool_scale[j], o_dconv_w[j],
                          o_dconv_b[j], o_ln_w[j], o_ln_b[j], o_w_out[j])
    return x


import jax as _jax
import jax.numpy as _jnp

TWIN_FORMAT = 'train_step'
FWD_PARAMS = ['x', 'positions', 'e_norm_w', 'e_w_in', 'e_q_norm_w', 'e_k_norm_w', 'e_conv_w', 'e_w_out', 'o_norm_w', 'o_w_in', 'o_pool_w', 'o_pool_scale', 'o_dconv_w', 'o_dconv_b', 'o_ln_w', 'o_ln_b', 'o_w_out']
TWIN_WEIGHTS = ['e_norm_w', 'e_w_in', 'e_q_norm_w', 'e_k_norm_w', 'e_conv_w', 'e_w_out', 'o_norm_w', 'o_w_in', 'o_pool_w', 'o_pool_scale', 'o_dconv_w', 'o_dconv_b', 'o_ln_w', 'o_ln_b', 'o_w_out']
TWIN_DIFF_INPUT = 'x'
TWIN_INPUTS = ['x', 'positions', 'e_norm_w', 'e_w_in', 'e_q_norm_w', 'e_k_norm_w', 'e_conv_w', 'e_w_out', 'o_norm_w', 'o_w_in', 'o_pool_w', 'o_pool_scale', 'o_dconv_w', 'o_dconv_b', 'o_ln_w', 'o_ln_b', 'o_w_out', 'loss_target', 'm_e_norm_w', 'm_e_w_in', 'm_e_q_norm_w', 'm_e_k_norm_w', 'm_e_conv_w', 'm_e_w_out', 'm_o_norm_w', 'm_o_w_in', 'm_o_pool_w', 'm_o_pool_scale', 'm_o_dconv_w', 'm_o_dconv_b', 'm_o_ln_w', 'm_o_ln_b', 'm_o_w_out', 'v_e_norm_w', 'v_e_w_in', 'v_e_q_norm_w', 'v_e_k_norm_w', 'v_e_conv_w', 'v_e_w_out', 'v_o_norm_w', 'v_o_w_in', 'v_o_pool_w', 'v_o_pool_scale', 'v_o_dconv_w', 'v_o_dconv_b', 'v_o_ln_w', 'v_o_ln_b', 'v_o_w_out']
TWIN_OUTPUTS = ['loss', 'grad_x', 'grad_e_norm_w', 'grad_e_w_in', 'grad_e_q_norm_w', 'grad_e_k_norm_w', 'grad_e_conv_w', 'grad_e_w_out', 'grad_o_norm_w', 'grad_o_w_in', 'grad_o_pool_w', 'grad_o_pool_scale', 'grad_o_dconv_w', 'grad_o_dconv_b', 'grad_o_ln_w', 'grad_o_ln_b', 'grad_o_w_out', 'delta_e_norm_w', 'delta_e_w_in', 'delta_e_q_norm_w', 'delta_e_k_norm_w', 'delta_e_conv_w', 'delta_e_w_out', 'delta_o_norm_w', 'delta_o_w_in', 'delta_o_pool_w', 'delta_o_pool_scale', 'delta_o_dconv_w', 'delta_o_dconv_b', 'delta_o_ln_w', 'delta_o_ln_b', 'delta_o_w_out', 'new_m_e_norm_w', 'new_m_e_w_in', 'new_m_e_q_norm_w', 'new_m_e_k_norm_w', 'new_m_e_conv_w', 'new_m_e_w_out', 'new_m_o_norm_w', 'new_m_o_w_in', 'new_m_o_pool_w', 'new_m_o_pool_scale', 'new_m_o_dconv_w', 'new_m_o_dconv_b', 'new_m_o_ln_w', 'new_m_o_ln_b', 'new_m_o_w_out', 'new_v_e_norm_w', 'new_v_e_w_in', 'new_v_e_q_norm_w', 'new_v_e_k_norm_w', 'new_v_e_conv_w', 'new_v_e_w_out', 'new_v_o_norm_w', 'new_v_o_w_in', 'new_v_o_pool_w', 'new_v_o_pool_scale', 'new_v_o_dconv_w', 'new_v_o_dconv_b', 'new_v_o_ln_w', 'new_v_o_ln_b', 'new_v_o_w_out']
TWIN_LEAF_KINDS = {'loss': 'loss', 'grad_x': 'grad_x', 'grad_e_norm_w': 'grad_w', 'grad_e_w_in': 'grad_w', 'grad_e_q_norm_w': 'grad_w', 'grad_e_k_norm_w': 'grad_w', 'grad_e_conv_w': 'grad_w', 'grad_e_w_out': 'grad_w', 'grad_o_norm_w': 'grad_w', 'grad_o_w_in': 'grad_w', 'grad_o_pool_w': 'grad_w', 'grad_o_pool_scale': 'grad_w', 'grad_o_dconv_w': 'grad_w', 'grad_o_dconv_b': 'grad_w', 'grad_o_ln_w': 'grad_w', 'grad_o_ln_b': 'grad_w', 'grad_o_w_out': 'grad_w', 'delta_e_norm_w': 'delta_w', 'delta_e_w_in': 'delta_w', 'delta_e_q_norm_w': 'delta_w', 'delta_e_k_norm_w': 'delta_w', 'delta_e_conv_w': 'delta_w', 'delta_e_w_out': 'delta_w', 'delta_o_norm_w': 'delta_w', 'delta_o_w_in': 'delta_w', 'delta_o_pool_w': 'delta_w', 'delta_o_pool_scale': 'delta_w', 'delta_o_dconv_w': 'delta_w', 'delta_o_dconv_b': 'delta_w', 'delta_o_ln_w': 'delta_w', 'delta_o_ln_b': 'delta_w', 'delta_o_w_out': 'delta_w', 'new_m_e_norm_w': 'new_m', 'new_m_e_w_in': 'new_m', 'new_m_e_q_norm_w': 'new_m', 'new_m_e_k_norm_w': 'new_m', 'new_m_e_conv_w': 'new_m', 'new_m_e_w_out': 'new_m', 'new_m_o_norm_w': 'new_m', 'new_m_o_w_in': 'new_m', 'new_m_o_pool_w': 'new_m', 'new_m_o_pool_scale': 'new_m', 'new_m_o_dconv_w': 'new_m', 'new_m_o_dconv_b': 'new_m', 'new_m_o_ln_w': 'new_m', 'new_m_o_ln_b': 'new_m', 'new_m_o_w_out': 'new_m', 'new_v_e_norm_w': 'new_v', 'new_v_e_w_in': 'new_v', 'new_v_e_q_norm_w': 'new_v', 'new_v_e_k_norm_w': 'new_v', 'new_v_e_conv_w': 'new_v', 'new_v_e_w_out': 'new_v', 'new_v_o_norm_w': 'new_v', 'new_v_o_w_in': 'new_v', 'new_v_o_pool_w': 'new_v', 'new_v_o_pool_scale': 'new_v', 'new_v_o_dconv_w': 'new_v', 'new_v_o_dconv_b': 'new_v', 'new_v_o_ln_w': 'new_v', 'new_v_o_ln_b': 'new_v', 'new_v_o_w_out': 'new_v'}


def _forward(args):
    return _fwd_reference(*[args[k] for k in FWD_PARAMS])


def _output_shape():
    out = _jax.eval_shape(lambda: _forward(_fwd_setup_inputs(0)))
    return out.shape, out.dtype

N_MICROBATCH = 1
ADAM_LR = 0.001
ADAM_B1 = 0.9
ADAM_B2 = 0.999
ADAM_EPS = 1e-08
ADAM_WD = 0.01
ADAM_STEP = 10
PER_EXAMPLE_BATCH_AXIS = {'x': 0, 'positions': 0, 'loss_target': 0}
SHARED_INPUTS = []
_WEIGHT_DTYPES = {'e_norm_w': _jnp.float32, 'e_w_in': _jnp.float32, 'e_q_norm_w': _jnp.float32, 'e_k_norm_w': _jnp.float32, 'e_conv_w': _jnp.float32, 'e_w_out': _jnp.float32, 'o_norm_w': _jnp.float32, 'o_w_in': _jnp.float32, 'o_pool_w': _jnp.float32, 'o_pool_scale': _jnp.float32, 'o_dconv_w': _jnp.float32, 'o_dconv_b': _jnp.float32, 'o_ln_w': _jnp.float32, 'o_ln_b': _jnp.float32, 'o_w_out': _jnp.float32}
MOMENT_SCALE = {'e_norm_w': 4.940981e+01, 'e_w_in': 6.673005e-01, 'e_q_norm_w': 6.625133e-01, 'e_k_norm_w': 6.449917e-01, 'e_conv_w': 1.250353e+01, 'e_w_out': 4.139017e-01, 'o_norm_w': 2.374201e+01, 'o_w_in': 3.281546e-01, 'o_pool_w': 1.091258e+00, 'o_pool_scale': 1.782088e+01, 'o_dconv_w': 2.159408e-01, 'o_dconv_b': 1.384975e+00, 'o_ln_w': 9.738736e+00, 'o_ln_b': 6.291983e+00, 'o_w_out': 3.129973e-01}


def _to_microbatches(a, axis):
    t = _jnp.moveaxis(a, axis, 0)
    t = t.reshape((N_MICROBATCH, t.shape[0] // N_MICROBATCH) + t.shape[1:])
    return _jnp.moveaxis(t, 1, axis + 1)


def setup_inputs(seed: int = 0) -> dict:
    inp = _fwd_setup_inputs(seed)
    key = _jax.random.fold_in(_jax.random.key(seed), 7919)
    shape, _ = _output_shape()
    out = dict(inp)
    out["loss_target"] = _jax.random.normal(_jax.random.fold_in(key, 0), shape, _jnp.float32)
    for i, name in enumerate(TWIN_WEIGHTS):
        w = inp[name].astype(_jnp.float32)
        if MOMENT_SCALE is None:
            s = _jnp.sqrt(_jnp.mean(_jnp.square(w)) + 1e-30)
        else:
            s = MOMENT_SCALE[name]
        km, kv = _jax.random.split(_jax.random.fold_in(key, i + 1))
        out[name] = w
        out["m_" + name] = s * _jax.random.normal(km, w.shape, _jnp.float32)
        out["v_" + name] = (s * s) * _jax.random.uniform(kv, w.shape, _jnp.float32, 0.5, 1.5)
    if N_MICROBATCH > 1:
        for name, axis in PER_EXAMPLE_BATCH_AXIS.items():
            out[name] = _to_microbatches(out[name], axis)
    return {'x': out['x'], 'positions': out['positions'], 'e_norm_w': out['e_norm_w'], 'e_w_in': out['e_w_in'], 'e_q_norm_w': out['e_q_norm_w'], 'e_k_norm_w': out['e_k_norm_w'], 'e_conv_w': out['e_conv_w'], 'e_w_out': out['e_w_out'], 'o_norm_w': out['o_norm_w'], 'o_w_in': out['o_w_in'], 'o_pool_w': out['o_pool_w'], 'o_pool_scale': out['o_pool_scale'], 'o_dconv_w': out['o_dconv_w'], 'o_dconv_b': out['o_dconv_b'], 'o_ln_w': out['o_ln_w'], 'o_ln_b': out['o_ln_b'], 'o_w_out': out['o_w_out'], 'loss_target': out['loss_target'], 'm_e_norm_w': out['m_e_norm_w'], 'm_e_w_in': out['m_e_w_in'], 'm_e_q_norm_w': out['m_e_q_norm_w'], 'm_e_k_norm_w': out['m_e_k_norm_w'], 'm_e_conv_w': out['m_e_conv_w'], 'm_e_w_out': out['m_e_w_out'], 'm_o_norm_w': out['m_o_norm_w'], 'm_o_w_in': out['m_o_w_in'], 'm_o_pool_w': out['m_o_pool_w'], 'm_o_pool_scale': out['m_o_pool_scale'], 'm_o_dconv_w': out['m_o_dconv_w'], 'm_o_dconv_b': out['m_o_dconv_b'], 'm_o_ln_w': out['m_o_ln_w'], 'm_o_ln_b': out['m_o_ln_b'], 'm_o_w_out': out['m_o_w_out'], 'v_e_norm_w': out['v_e_norm_w'], 'v_e_w_in': out['v_e_w_in'], 'v_e_q_norm_w': out['v_e_q_norm_w'], 'v_e_k_norm_w': out['v_e_k_norm_w'], 'v_e_conv_w': out['v_e_conv_w'], 'v_e_w_out': out['v_e_w_out'], 'v_o_norm_w': out['v_o_norm_w'], 'v_o_w_in': out['v_o_w_in'], 'v_o_pool_w': out['v_o_pool_w'], 'v_o_pool_scale': out['v_o_pool_scale'], 'v_o_dconv_w': out['v_o_dconv_w'], 'v_o_dconv_b': out['v_o_dconv_b'], 'v_o_ln_w': out['v_o_ln_w'], 'v_o_ln_b': out['v_o_ln_b'], 'v_o_w_out': out['v_o_w_out']}


def _loss(weights, diff, rest, loss_target):
    with _jax.named_scope("forward"):
        args = {**rest, TWIN_DIFF_INPUT: diff, **{k: w.astype(_WEIGHT_DTYPES[k]) for k, w in weights.items()}}
        y = _forward(args)
    with _jax.named_scope("loss_head"):
        err = _jnp.square(y.astype(_jnp.float32) - loss_target)
        return 0.5 * _jnp.sum(_jnp.mean(err, axis=-1)) if err.ndim else 0.5 * err


def _adamw(w, g, m, v):
    m = ADAM_B1 * m + (1.0 - ADAM_B1) * g
    v = ADAM_B2 * v + (1.0 - ADAM_B2) * _jnp.square(g)
    m_hat = m / (1.0 - ADAM_B1 ** ADAM_STEP)
    v_hat = v / (1.0 - ADAM_B2 ** ADAM_STEP)
    delta = -ADAM_LR * (m_hat / (_jnp.sqrt(v_hat) + ADAM_EPS) + ADAM_WD * w)
    return delta, m, v


def reference(x, positions, e_norm_w, e_w_in, e_q_norm_w, e_k_norm_w, e_conv_w, e_w_out, o_norm_w, o_w_in, o_pool_w, o_pool_scale, o_dconv_w, o_dconv_b, o_ln_w, o_ln_b, o_w_out, loss_target, m_e_norm_w, m_e_w_in, m_e_q_norm_w, m_e_k_norm_w, m_e_conv_w, m_e_w_out, m_o_norm_w, m_o_w_in, m_o_pool_w, m_o_pool_scale, m_o_dconv_w, m_o_dconv_b, m_o_ln_w, m_o_ln_b, m_o_w_out, v_e_norm_w, v_e_w_in, v_e_q_norm_w, v_e_k_norm_w, v_e_conv_w, v_e_w_out, v_o_norm_w, v_o_w_in, v_o_pool_w, v_o_pool_scale, v_o_dconv_w, v_o_dconv_b, v_o_ln_w, v_o_ln_b, v_o_w_out):
    given = dict(x=x, positions=positions, e_norm_w=e_norm_w, e_w_in=e_w_in, e_q_norm_w=e_q_norm_w, e_k_norm_w=e_k_norm_w, e_conv_w=e_conv_w, e_w_out=e_w_out, o_norm_w=o_norm_w, o_w_in=o_w_in, o_pool_w=o_pool_w, o_pool_scale=o_pool_scale, o_dconv_w=o_dconv_w, o_dconv_b=o_dconv_b, o_ln_w=o_ln_w, o_ln_b=o_ln_b, o_w_out=o_w_out, loss_target=loss_target, m_e_norm_w=m_e_norm_w, m_e_w_in=m_e_w_in, m_e_q_norm_w=m_e_q_norm_w, m_e_k_norm_w=m_e_k_norm_w, m_e_conv_w=m_e_conv_w, m_e_w_out=m_e_w_out, m_o_norm_w=m_o_norm_w, m_o_w_in=m_o_w_in, m_o_pool_w=m_o_pool_w, m_o_pool_scale=m_o_pool_scale, m_o_dconv_w=m_o_dconv_w, m_o_dconv_b=m_o_dconv_b, m_o_ln_w=m_o_ln_w, m_o_ln_b=m_o_ln_b, m_o_w_out=m_o_w_out, v_e_norm_w=v_e_norm_w, v_e_w_in=v_e_w_in, v_e_q_norm_w=v_e_q_norm_w, v_e_k_norm_w=v_e_k_norm_w, v_e_conv_w=v_e_conv_w, v_e_w_out=v_e_w_out, v_o_norm_w=v_o_norm_w, v_o_w_in=v_o_w_in, v_o_pool_w=v_o_pool_w, v_o_pool_scale=v_o_pool_scale, v_o_dconv_w=v_o_dconv_w, v_o_dconv_b=v_o_dconv_b, v_o_ln_w=v_o_ln_w, v_o_ln_b=v_o_ln_b, v_o_w_out=v_o_w_out)
    weights = {n: given[n] for n in TWIN_WEIGHTS}
    shared = {n: given[n] for n in SHARED_INPUTS}
    per_example = {n: given[n] for n in ['x', 'positions']}
    grad_fn = _jax.value_and_grad(_loss, argnums=(0, 1))

    def one_microbatch(ex, loss_target):
        ex = dict(ex)
        diff = ex.pop(TWIN_DIFF_INPUT)
        return grad_fn(weights, diff, {**shared, **ex}, loss_target)

    if N_MICROBATCH == 1:
        loss, (grad_w, grad_x) = one_microbatch(per_example, given["loss_target"])
    else:
        def body(carry, xs):
            loss_sum, grad_sum = carry
            l_k, (gw_k, gx_k) = one_microbatch(xs[0], xs[1])
            with _jax.named_scope("update"):
                return (loss_sum + l_k, _jax.tree.map(_jnp.add, grad_sum, gw_k)), gx_k

        init = (_jnp.zeros((), _jnp.float32), _jax.tree.map(_jnp.zeros_like, weights))
        (loss, grad_w), grad_x = _jax.lax.scan(body, init, (per_example, given["loss_target"]))
    with _jax.named_scope("update"):
        delta_w, new_m, new_v = {}, {}, {}
        for n in TWIN_WEIGHTS:
            delta_w[n], new_m[n], new_v[n] = _adamw(weights[n], grad_w[n], given["m_" + n], given["v_" + n])
    return (loss, grad_x, *[grad_w[n] for n in TWIN_WEIGHTS], *[delta_w[n] for n in TWIN_WEIGHTS],
            *[new_m[n] for n in TWIN_WEIGHTS], *[new_v[n] for n in TWIN_WEIGHTS])
```

```python
import functools

import numpy as np
import jax
import jax.numpy as jnp
from jax import lax
from jax.experimental import pallas as pl
from jax.experimental.pallas import tpu as pltpu

f32 = jnp.float32
bf16 = jnp.bfloat16

D_MODEL = 1024
HEAD_DIM = 64
N_GROUPS = 3
DILATIONS = (1, 4, 16)
QBLK = 128
A_WIDTH = 512
EVEN_IN = 7168
ODD_IN = 2560
POOL_SIZES = (2, 4, 8, 16)
D_CONV = 31
SC_WIDTH = 3
ROT_HALF = 8
ROPE_THETA = 500000.0
EPS = 1e-6
NEG = -1e30
SCALE = HEAD_DIM ** -0.5
N_DEV = 8
LANES = 128
VMEM_LIMIT = 48 * 1024 * 1024

ADAM_LR = 0.001
ADAM_B1 = 0.9
ADAM_B2 = 0.999
ADAM_EPS = 1e-08
ADAM_WD = 0.01
ADAM_STEP = 10

E_Q, E_K, E_V, E_BG, E_CG, E_HB, E_Z = 0, 1536, 3072, 4608, 5120, 5632, 6144
O_UC, O_DA, O_DG, O_Z = 0, 512, 1024, 1536


def _cp(sem):
    return pltpu.CompilerParams(dimension_semantics=sem, vmem_limit_bytes=VMEM_LIMIT)


def _sigmoid(z):
    return 1.0 / (1.0 + jnp.exp(-z))


def _tile(n, pref):
    t = pref
    while n % t:
        t //= 2
    return t


def _mm_nn(a, b, *, name, out_dtype=f32, res=None, tn=1024):
    M, K = a.shape
    tm = _tile(M, 1024)
    if b.ndim == 3:
        tn = b.shape[2]
        N = b.shape[0] * tn
        b_spec = pl.BlockSpec((None, K, tn), lambda i, j: (j, 0, 0))
    else:
        N = b.shape[1]
        tn = _tile(N, tn)
        b_spec = pl.BlockSpec((K, tn), lambda i, j: (0, j))

    def body(*refs):
        if res is None:
            a_ref, b_ref, o_ref = refs
        else:
            a_ref, b_ref, r_ref, o_ref = refs
        acc = jnp.dot(a_ref[...], b_ref[...], preferred_element_type=f32)
        if res is not None:
            acc = acc + r_ref[...]
        o_ref[...] = acc.astype(out_dtype)

    in_specs = [pl.BlockSpec((tm, K), lambda i, j: (i, 0)), b_spec]
    args = [a, b]
    if res is not None:
        in_specs.append(pl.BlockSpec((tm, tn), lambda i, j: (i, j)))
        args.append(res)
    return pl.pallas_call(
        body, name=name, grid=(M // tm, N // tn), in_specs=in_specs,
        out_specs=pl.BlockSpec((tm, tn), lambda i, j: (i, j)),
        out_shape=jax.ShapeDtypeStruct((M, N), out_dtype),
        compiler_params=_cp(("parallel", "parallel")),
    )(*args)


def _mm_nt(a, b, *, name, out_dtype=f32):
    M, K = a.shape
    tm = _tile(M, 1024)
    if b.ndim == 3:
        nk, N, tk = b.shape
        b_spec = pl.BlockSpec((None, N, tk), lambda i, k: (k, 0, 0))
    else:
        N = b.shape[0]
        tk = _tile(K, 1024) if K % 1024 == 0 else _tile(K, 512)
        nk = K // tk
        b_spec = pl.BlockSpec((N, tk), lambda i, k: (0, k))

    def body(a_ref, b_ref, o_ref, acc_ref):
        k = pl.program_id(1)
        part = lax.dot_general(a_ref[...], b_ref[...], (((1,), (1,)), ((), ())), preferred_element_type=f32)

        @pl.when(k == 0)
        def _():
            acc_ref[...] = part

        @pl.when(k > 0)
        def _():
            acc_ref[...] += part

        @pl.when(k == nk - 1)
        def _():
            o_ref[...] = acc_ref[...].astype(out_dtype)

    return pl.pallas_call(
        body, name=name, grid=(M // tm, nk),
        in_specs=[pl.BlockSpec((tm, tk), lambda i, k: (i, k)), b_spec],
        out_specs=pl.BlockSpec((tm, N), lambda i, k: (i, 0)),
        out_shape=jax.ShapeDtypeStruct((M, N), out_dtype),
        scratch_shapes=[pltpu.VMEM((tm, N), f32)],
        compiler_params=_cp(("parallel", "arbitrary")),
    )(a, b)


def _mm_tn(a, b, *, name, out_dtype=f32, tn=512, chunks=None):
    S, Ka = a.shape
    N = b.shape[1]
    ts = _tile(S, 1024)
    ns = S // ts
    if chunks:
        tn = N // chunks
        out_spec = pl.BlockSpec((None, Ka, tn), lambda j, s: (j, 0, 0))
        out_shape = jax.ShapeDtypeStruct((chunks, Ka, tn), out_dtype)
    else:
        tn = _tile(N, tn)
        out_spec = pl.BlockSpec((Ka, tn), lambda j, s: (0, j))
        out_shape = jax.ShapeDtypeStruct((Ka, N), out_dtype)

    def body(a_ref, b_ref, o_ref, acc_ref):
        s = pl.program_id(1)
        part = lax.dot_general(a_ref[...], b_ref[...], (((0,), (0,)), ((), ())), preferred_element_type=f32)

        @pl.when(s == 0)
        def _():
            acc_ref[...] = part

        @pl.when(s > 0)
        def _():
            acc_ref[...] += part

        @pl.when(s == ns - 1)
        def _():
            o_ref[...] = acc_ref[...].astype(out_dtype)

    return pl.pallas_call(
        body, name=name, grid=(N // tn, ns),
        in_specs=[pl.BlockSpec((ts, Ka), lambda j, s: (s, 0)), pl.BlockSpec((ts, tn), lambda j, s: (s, j))],
        out_specs=out_spec, out_shape=out_shape,
        scratch_shapes=[pltpu.VMEM((Ka, tn), f32)],
        compiler_params=_cp(("parallel", "arbitrary")),
    )(a, b)


def _mm_out_loss(u, w, x_res, target, *, name):
    M, K = u.shape
    N = w.shape[1]
    tm = _tile(M, 512)
    nm = M // tm

    def body(u_ref, w_ref, x_ref, t_ref, dy_ref, dyb_ref, loss_ref, acc_ref):
        i = pl.program_id(0)
        y = jnp.dot(u_ref[...], w_ref[...], preferred_element_type=f32) + x_ref[...]
        err = y - t_ref[...]
        dy = err * (1.0 / N)
        dy_ref[...] = dy
        dyb_ref[...] = dy.astype(bf16)
        part = jnp.sum(err * err, axis=0, keepdims=True)

        @pl.when(i == 0)
        def _():
            acc_ref[...] = part

        @pl.when(i > 0)
        def _():
            acc_ref[...] += part

        @pl.when(i == nm - 1)
        def _():
            tot = jnp.sum(acc_ref[...], axis=1, keepdims=True)
            loss_ref[...] = jnp.broadcast_to(tot * (0.5 / N), (8, LANES))

    return pl.pallas_call(
        body, name=name, grid=(nm,),
        in_specs=[pl.BlockSpec((tm, K), lambda i: (i, 0)), pl.BlockSpec((K, N), lambda i: (0, 0)),
                  pl.BlockSpec((tm, N), lambda i: (i, 0)), pl.BlockSpec((tm, N), lambda i: (i, 0))],
        out_specs=[pl.BlockSpec((tm, N), lambda i: (i, 0)), pl.BlockSpec((tm, N), lambda i: (i, 0)),
                   pl.BlockSpec((8, LANES), lambda i: (0, 0))],
        out_shape=[jax.ShapeDtypeStruct((M, N), f32), jax.ShapeDtypeStruct((M, N), bf16),
                   jax.ShapeDtypeStruct((8, LANES), f32)],
        scratch_shapes=[pltpu.VMEM((1, N), f32)],
        compiler_params=_cp(("arbitrary",)),
    )(u, w, x_res, target)


def _rms_fwd(x, w, *, name):
    S, Dm = x.shape
    tm = _tile(S, 1024)

    def body(x_ref, w_ref, h_ref):
        xv = x_ref[...]
        r = lax.rsqrt(jnp.mean(xv * xv, axis=-1, keepdims=True) + EPS)
        h_ref[...] = (xv * r * w_ref[...]).astype(bf16)

    return pl.pallas_call(
        body, name=name, grid=(S // tm,),
        in_specs=[pl.BlockSpec((tm, Dm), lambda i: (i, 0)), pl.BlockSpec((1, Dm), lambda i: (0, 0))],
        out_specs=pl.BlockSpec((tm, Dm), lambda i: (i, 0)),
        out_shape=jax.ShapeDtypeStruct((S, Dm), bf16),
        compiler_params=_cp(("parallel",)),
    )(x, w)


def _rms_bwd(x, w, dh, res, *, name):
    S, Dm = x.shape
    tm = _tile(S, 512)

    def body(x_ref, w_ref, dh_ref, res_ref, dx_ref, dxb_ref, gw_ref):
        i = pl.program_id(0)
        xv = x_ref[...]
        r = lax.rsqrt(jnp.mean(xv * xv, axis=-1, keepdims=True) + EPS)
        xn = xv * r
        dh_v = dh_ref[...]
        dxn = dh_v * w_ref[...]
        dx = r * (dxn - xn * jnp.mean(dxn * xn, axis=-1, keepdims=True)) + res_ref[...]
        dx_ref[...] = dx
        dxb_ref[...] = dx.astype(bf16)
        part = jnp.sum(dh_v * xn, axis=0, keepdims=True)

        @pl.when(i == 0)
        def _():
            gw_ref[...] = part

        @pl.when(i > 0)
        def _():
            gw_ref[...] += part

    dx, dxb, gw = pl.pallas_call(
        body, name=name, grid=(S // tm,),
        in_specs=[pl.BlockSpec((tm, Dm), lambda i: (i, 0)), pl.BlockSpec((1, Dm), lambda i: (0, 0)),
                  pl.BlockSpec((tm, Dm), lambda i: (i, 0)), pl.BlockSpec((tm, Dm), lambda i: (i, 0))],
        out_specs=[pl.BlockSpec((tm, Dm), lambda i: (i, 0)), pl.BlockSpec((tm, Dm), lambda i: (i, 0)),
                   pl.BlockSpec((1, Dm), lambda i: (0, 0))],
        out_shape=[jax.ShapeDtypeStruct((S, Dm), f32), jax.ShapeDtypeStruct((S, Dm), bf16),
                   jax.ShapeDtypeStruct((1, Dm), f32)],
        compiler_params=_cp(("arbitrary",)),
    )(x, w, dh, res)
    return dx, dxb, gw


_INV_FREQ = [float(v) for v in (np.float32(ROPE_THETA) ** (-np.arange(ROT_HALF, dtype=np.float32) / np.float32(ROT_HALF))).astype(np.float32)]


def _rope_tables(pos_col):
    S = pos_col.shape[0]
    tm = _tile(S, 1024)

    def body(p_ref, c_ref, s1_ref, s2_ref):
        lane = lax.broadcasted_iota(jnp.int32, (tm, LANES), 1)
        lm = lane % HEAD_DIM
        fi = lm % ROT_HALF
        inv = jnp.zeros((tm, LANES), f32)
        for k in range(ROT_HALF):
            inv = jnp.where(fi == k, _INV_FREQ[k], inv)
        ang = p_ref[...].astype(f32) * inv
        cs = jnp.cos(ang)
        sn = jnp.sin(ang)
        c_ref[...] = jnp.where(lm < 2 * ROT_HALF, cs, 1.0)
        s1_ref[...] = jnp.where((lm >= ROT_HALF) & (lm < 2 * ROT_HALF), sn, 0.0)
        s2_ref[...] = jnp.where(lm < ROT_HALF, -sn, 0.0)

    spec = pl.BlockSpec((tm, LANES), lambda i: (i, 0))
    return pl.pallas_call(
        body, name="rope_tables", grid=(S // tm,),
        in_specs=[pl.BlockSpec((tm, 1), lambda i: (i, 0))],
        out_specs=[spec, spec, spec],
        out_shape=[jax.ShapeDtypeStruct((S, LANES), f32)] * 3,
        compiler_params=_cp(("parallel",)),
    )(pos_col)


def _head_mean(v, m):
    hi = v.astype(bf16)
    r1 = v - hi.astype(f32)
    mid = r1.astype(bf16)
    lo = (r1 - mid.astype(f32)).astype(bf16)
    return (jnp.dot(hi, m, preferred_element_type=f32) + jnp.dot(mid, m, preferred_element_type=f32)
            + jnp.dot(lo, m, preferred_element_type=f32))


def _head_mean_matrix():
    i = np.arange(LANES)
    return jnp.asarray(((i[:, None] // HEAD_DIM) == (i[None, :] // HEAD_DIM)).astype(np.float32) / HEAD_DIM, dtype=bf16)


def _qk_prep(proj, tabs, nw, hm):
    S = proj.shape[0]
    tm = _tile(S, 512)

    def body(x_ref, c_ref, s1_ref, s2_ref, nw_ref, m_ref, o_ref):
        cb = pl.program_id(1)
        w = jnp.where(cb >= 3, nw_ref[1:2, :], nw_ref[0:1, :])
        c, s1, s2, m = c_ref[...], s1_ref[...], s2_ref[...], m_ref[...]
        for p in range(4):
            t = x_ref[:, p * LANES:(p + 1) * LANES]
            r = lax.rsqrt(_head_mean(t * t, m) + EPS)
            that = t * r * w
            o_ref[:, p * LANES:(p + 1) * LANES] = (
                that * c + pltpu.roll(that, ROT_HALF, axis=1) * s1 + pltpu.roll(that, LANES - ROT_HALF, axis=1) * s2)

    tab = pl.BlockSpec((tm, LANES), lambda i, j: (i, 0))
    return pl.pallas_call(
        body, name="qk_prep", grid=(S // tm, 6),
        in_specs=[pl.BlockSpec((tm, 512), lambda i, j: (i, j)), tab, tab, tab,
                  pl.BlockSpec((2, LANES), lambda i, j: (0, 0)), pl.BlockSpec((LANES, LANES), lambda i, j: (0, 0))],
        out_specs=pl.BlockSpec((tm, 512), lambda i, j: (i, j)),
        out_shape=jax.ShapeDtypeStruct((S, 3072), f32),
        compiler_params=_cp(("parallel", "parallel")),
    )(proj, *tabs, nw, hm)


def _rows(r, d):
    return pl.ds(r, QBLK, stride=d) if d > 1 else slice(None)


def _band_mask(j):
    qi = lax.broadcasted_iota(jnp.int32, (QBLK, 2 * QBLK), 0) + QBLK
    kj = lax.broadcasted_iota(jnp.int32, (QBLK, 2 * QBLK), 1)
    dist = qi - kj
    first_key = jnp.where(j > 0, 0, QBLK)
    return (dist >= 0) & (dist <= QBLK) & (kj >= first_key)


def _attn_fwd(qk, proj, g, *, name):
    S = qk.shape[0]
    d = DILATIONS[g]
    span = QBLK * d
    ns = S // span

    def body(q_ref, kp_ref, kc_ref, vp_ref, vc_ref, o_ref, lse_ref):
        j = pl.program_id(0)
        mask = _band_mask(j)
        lo = lax.broadcasted_iota(jnp.int32, (QBLK, LANES), 1) < HEAD_DIM
        for r in range(d):
            rows = _rows(r, d)
            qb = q_ref[rows, :].astype(bf16)
            kcat = jnp.concatenate([kp_ref[rows, :], kc_ref[rows, :]], axis=0).astype(bf16)
            vcat = jnp.concatenate([vp_ref[rows, :], vc_ref[rows, :]], axis=0).astype(bf16)
            outs = []
            for hh in range(2):
                sel = lo if hh == 0 else jnp.logical_not(lo)
                qm = jnp.where(sel, qb, jnp.zeros_like(qb))
                s = lax.dot_general(qm, kcat, (((1,), (1,)), ((), ())), preferred_element_type=f32) * SCALE
                s = jnp.where(mask, s, NEG)
                mx = jnp.max(s, axis=-1, keepdims=True)
                pexp = jnp.exp(s - mx)
                den = jnp.sum(pexp, axis=-1, keepdims=True)
                pn = (pexp * (1.0 / den)).astype(bf16)
                outs.append(jnp.dot(pn, vcat, preferred_element_type=f32))
                lse_ref.at[hh][rows, :] = jnp.broadcast_to(mx + jnp.log(den), (QBLK, LANES))
            o_ref[rows, :] = jnp.where(lo, outs[0], outs[1])

    def prev(j):
        return jnp.maximum(j - 1, 0)

    blk = (span, LANES)
    return pl.pallas_call(
        body, name=name, grid=(ns, 4),
        in_specs=[pl.BlockSpec(blk, lambda j, p: (j, 4 * g + p)),
                  pl.BlockSpec(blk, lambda j, p: (prev(j), 12 + 4 * g + p)),
                  pl.BlockSpec(blk, lambda j, p: (j, 12 + 4 * g + p)),
                  pl.BlockSpec(blk, lambda j, p: (prev(j), E_V // LANES + 4 * g + p)),
                  pl.BlockSpec(blk, lambda j, p: (j, E_V // LANES + 4 * g + p))],
        out_specs=[pl.BlockSpec(blk, lambda j, p: (j, p)), pl.BlockSpec((2, span, LANES), lambda j, p: (p, j, 0))],
        out_shape=[jax.ShapeDtypeStruct((S, A_WIDTH), f32), jax.ShapeDtypeStruct((8, S, LANES), f32)],
        compiler_params=_cp(("parallel", "parallel")),
    )(qk, qk, qk, proj, proj)


def _attn_bwd(qk, proj, do_a, lt, dsum, g, *, name):
    S = qk.shape[0]
    d = DILATIONS[g]
    span = QBLK * d
    ns = S // span

    def body(q_ref, kp_ref, kc_ref, vp_ref, vc_ref, do_ref, lt_ref, ds_ref, dq_ref, dk_ref, dv_ref, ck_ref, cv_ref):
        j = pl.program_id(1)

        @pl.when(j == 0)
        def _():
            ck_ref[...] = jnp.zeros_like(ck_ref)
            cv_ref[...] = jnp.zeros_like(cv_ref)

        @pl.when(j < ns)
        def _():
            mask = _band_mask(j)
            lo = lax.broadcasted_iota(jnp.int32, (QBLK, LANES), 1) < HEAD_DIM
            for r in range(d):
                rows = _rows(r, d)
                qb = q_ref[rows, :].astype(bf16)
                dob = do_ref[rows, :].astype(bf16)
                kcat = jnp.concatenate([kp_ref[rows, :], kc_ref[rows, :]], axis=0).astype(bf16)
                vcat = jnp.concatenate([vp_ref[rows, :], vc_ref[rows, :]], axis=0).astype(bf16)
                dq = jnp.zeros((QBLK, LANES), f32)
                dk = jnp.zeros((2 * QBLK, LANES), f32)
                dv = jnp.zeros((2 * QBLK, LANES), f32)
                for hh in range(2):
                    sel = lo if hh == 0 else jnp.logical_not(lo)
                    qm = jnp.where(sel, qb, jnp.zeros_like(qb))
                    dom = jnp.where(sel, dob, jnp.zeros_like(dob))
                    s = lax.dot_general(qm, kcat, (((1,), (1,)), ((), ())), preferred_element_type=f32) * SCALE
                    s = jnp.where(mask, s, NEG)
                    lt_h = lt_ref.at[hh][rows, :]
                    ds_h = ds_ref.at[hh][rows, :]
                    prob = jnp.exp(s - jnp.concatenate([lt_h, lt_h], axis=1))
                    dp = lax.dot_general(dom, vcat, (((1,), (1,)), ((), ())), preferred_element_type=f32)
                    dsb = (prob * (dp - jnp.concatenate([ds_h, ds_h], axis=1)) * SCALE).astype(bf16)
                    pb = prob.astype(bf16)
                    dq = dq + jnp.where(sel, jnp.dot(dsb, kcat, preferred_element_type=f32), 0.0)
                    dk = dk + lax.dot_general(dsb, qm, (((0,), (0,)), ((), ())), preferred_element_type=f32)
                    dv = dv + lax.dot_general(pb, dom, (((0,), (0,)), ((), ())), preferred_element_type=f32)
                dq_ref[rows, :] = dq
                dk_ref[rows, :] = ck_ref[rows, :] + dk[:QBLK]
                dv_ref[rows, :] = cv_ref[rows, :] + dv[:QBLK]
                ck_ref[rows, :] = dk[QBLK:]
                cv_ref[rows, :] = dv[QBLK:]

        @pl.when(j == ns)
        def _():
            dk_ref[...] = ck_ref[...]
            dv_ref[...] = cv_ref[...]

    def cur(j):
        return jnp.minimum(j, ns - 1)

    def prev(j):
        return jnp.maximum(jnp.minimum(j, ns - 1) - 1, 0)

    def out_prev(j):
        return jnp.maximum(j - 1, 0)

    blk = (span, LANES)
    blk2 = (2, span, LANES)
    return pl.pallas_call(
        body, name=name, grid=(4, ns + 1),
        in_specs=[pl.BlockSpec(blk, lambda p, j: (cur(j), 4 * g + p)),
                  pl.BlockSpec(blk, lambda p, j: (prev(j), 12 + 4 * g + p)),
                  pl.BlockSpec(blk, lambda p, j: (cur(j), 12 + 4 * g + p)),
                  pl.BlockSpec(blk, lambda p, j: (prev(j), E_V // LANES + 4 * g + p)),
                  pl.BlockSpec(blk, lambda p, j: (cur(j), E_V // LANES + 4 * g + p)),
                  pl.BlockSpec(blk, lambda p, j: (cur(j), p)),
                  pl.BlockSpec(blk2, lambda p, j: (p, cur(j), 0)),
                  pl.BlockSpec(blk2, lambda p, j: (p, cur(j), 0))],
        out_specs=[pl.BlockSpec(blk, lambda p, j: (cur(j), p)),
                   pl.BlockSpec(blk, lambda p, j: (out_prev(j), p)),
                   pl.BlockSpec(blk, lambda p, j: (out_prev(j), p))],
        out_shape=[jax.ShapeDtypeStruct((S, A_WIDTH), f32)] * 3,
        scratch_shapes=[pltpu.VMEM(blk, f32), pltpu.VMEM(blk, f32)],
        compiler_params=_cp(("parallel", "arbitrary")),
    )(qk, qk, qk, proj, proj, do_a, lt, dsum)


def _prev_halo(tm, h, col):
    return pl.BlockSpec((h, 512), lambda i: (jnp.maximum(i * (tm // h) - 1, 0), col))


def _next_halo(tm, h, col, S):
    return pl.BlockSpec((h, 512), lambda i: (jnp.minimum((i + 1) * (tm // h), S // h - 1), col))


def _mix0_fwd(o_g, lse_g, proj, conv_w):
    S = proj.shape[0]
    tm = _tile(S, 256)

    def body(o0, o1, o2, l0, l1, l2, bg_ref, cg_ref, hb_ref, z_ref, cgh_ref, hbh_ref, w_ref,
             u_ref, oa_ref, lt_ref, tbuf):
        i = pl.program_id(0)
        lo = lax.broadcasted_iota(jnp.int32, (tm, LANES), 1) < HEAD_DIM
        ls = [l0[...], l1[...], l2[...]]
        mx = jnp.maximum(jnp.maximum(ls[0], ls[1]), ls[2])
        es = [jnp.exp(l - mx) for l in ls]
        tot = es[0] + es[1] + es[2]
        lt_ref[...] = mx + jnp.log(tot)
        inv = 1.0 / tot
        ws = [e * inv for e in es]
        z = z_ref[...]
        sz = z * _sigmoid(z)
        for p in range(4):
            acc = jnp.zeros((tm, LANES), f32)
            for gi, o_ref in enumerate((o0, o1, o2)):
                wp = jnp.where(lo, ws[gi][2 * p], ws[gi][2 * p + 1])
                acc = acc + wp * o_ref[:, p * LANES:(p + 1) * LANES]
            oa_ref[:, p * LANES:(p + 1) * LANES] = acc
            u_ref[:, p * LANES:(p + 1) * LANES] = (acc * sz[:, p * LANES:(p + 1) * LANES]).astype(bf16)
        t = cg_ref[...] * hb_ref[...]
        tbuf[0:8, :] = jnp.where(i > 0, cgh_ref[...] * hbh_ref[...], 0.0)
        tbuf[8:, :] = t
        cv = w_ref[2:3, :] * t + w_ref[1:2, :] * tbuf[pl.ds(7, tm), :] + w_ref[0:1, :] * tbuf[pl.ds(6, tm), :]
        u_ref[:, A_WIDTH:] = (bg_ref[...] * cv * sz[:, A_WIDTH:]).astype(bf16)

    row = lambda w, c: pl.BlockSpec((tm, w), lambda i: (i, c))
    stat = pl.BlockSpec((8, tm, LANES), lambda i: (0, i, 0))
    return pl.pallas_call(
        body, name="mix0_fwd", grid=(S // tm,),
        in_specs=[row(512, 0)] * 3 + [stat] * 3
        + [row(512, E_BG // 512), row(512, E_CG // 512), row(512, E_HB // 512), row(1024, E_Z // 1024),
           _prev_halo(tm, 8, E_CG // 512), _prev_halo(tm, 8, E_HB // 512), pl.BlockSpec((SC_WIDTH, 512), lambda i: (0, 0))],
        out_specs=[row(1024, 0), row(512, 0), stat],
        out_shape=[jax.ShapeDtypeStruct((S, D_MODEL), bf16), jax.ShapeDtypeStruct((S, A_WIDTH), f32),
                   jax.ShapeDtypeStruct((8, S, LANES), f32)],
        scratch_shapes=[pltpu.VMEM((tm + 8, 512), f32)],
        compiler_params=_cp(("parallel",)),
    )(*o_g, *lse_g, proj, proj, proj, proj, proj, proj, conv_w)


def _dsilu(z, sg):
    return sg * (1.0 + z * (1.0 - sg))


def _mix0_bwd_a(du, proj, o_a, conv_w):
    S = proj.shape[0]
    tm = _tile(S, 256)

    def body(du_ref, bg_ref, cg_ref, hb_ref, z_ref, cgh_ref, hbh_ref, oa_ref, w_ref,
             dz_ref, doa_ref, ds_ref, dbg_ref, dcv_ref, tbuf):
        i = pl.program_id(0)
        lo = lax.broadcasted_iota(jnp.int32, (tm, LANES), 1) < HEAD_DIM
        z = z_ref[...]
        sg = _sigmoid(z)
        sz = z * sg
        dsz = _dsilu(z, sg)
        du_v = du_ref[...]
        t = cg_ref[...] * hb_ref[...]
        tbuf[0:8, :] = jnp.where(i > 0, cgh_ref[...] * hbh_ref[...], 0.0)
        tbuf[8:, :] = t
        cv = w_ref[2:3, :] * t + w_ref[1:2, :] * tbuf[pl.ds(7, tm), :] + w_ref[0:1, :] * tbuf[pl.ds(6, tm), :]
        bg = bg_ref[...]
        oa = oa_ref[...]
        dz_ref[:, :A_WIDTH] = (du_v[:, :A_WIDTH] * oa * dsz[:, :A_WIDTH]).astype(bf16)
        dz_ref[:, A_WIDTH:] = (du_v[:, A_WIDTH:] * (bg * cv) * dsz[:, A_WIDTH:]).astype(bf16)
        doa = du_v[:, :A_WIDTH] * sz[:, :A_WIDTH]
        dyb = du_v[:, A_WIDTH:] * sz[:, A_WIDTH:]
        doa_ref[...] = doa
        dbg_ref[...] = (dyb * cv).astype(bf16)
        dcv_ref[...] = dyb * bg
        prod = doa * oa
        for p in range(4):
            pp = prod[:, p * LANES:(p + 1) * LANES]
            sa = jnp.sum(jnp.where(lo, pp, 0.0), axis=-1, keepdims=True)
            sb = jnp.sum(jnp.where(lo, 0.0, pp), axis=-1, keepdims=True)
            ds_ref[2 * p] = jnp.broadcast_to(sa, (tm, LANES))
            ds_ref[2 * p + 1] = jnp.broadcast_to(sb, (tm, LANES))

    row = lambda w, c: pl.BlockSpec((tm, w), lambda i: (i, c))
    return pl.pallas_call(
        body, name="mix0_bwd_a", grid=(S // tm,),
        in_specs=[row(1024, 0), row(512, E_BG // 512), row(512, E_CG // 512), row(512, E_HB // 512), row(1024, E_Z // 1024),
                  _prev_halo(tm, 8, E_CG // 512), _prev_halo(tm, 8, E_HB // 512), row(512, 0),
                  pl.BlockSpec((SC_WIDTH, 512), lambda i: (0, 0))],
        out_specs=[row(1024, 0), row(512, 0), pl.BlockSpec((8, tm, LANES), lambda i: (0, i, 0)), row(512, 0), row(512, 0)],
        out_shape=[jax.ShapeDtypeStruct((S, D_MODEL), bf16), jax.ShapeDtypeStruct((S, A_WIDTH), f32),
                   jax.ShapeDtypeStruct((8, S, LANES), f32), jax.ShapeDtypeStruct((S, 512), bf16),
                   jax.ShapeDtypeStruct((S, 512), f32)],
        scratch_shapes=[pltpu.VMEM((tm + 8, 512), f32)],
        compiler_params=_cp(("parallel",)),
    )(du, proj, proj, proj, proj, proj, proj, o_a, conv_w)


def _mix0_bwd_b(dcv, proj, conv_w):
    S = proj.shape[0]
    tm = _tile(S, 256)
    nt = S // tm

    def body(dcv_ref, dcvn_ref, cg_ref, hb_ref, cgh_ref, hbh_ref, w_ref, dcg_ref, dhb_ref, gw_ref, tbuf, dbuf):
        i = pl.program_id(0)
        cg = cg_ref[...]
        hb = hb_ref[...]
        t = cg * hb
        tbuf[0:8, :] = jnp.where(i > 0, cgh_ref[...] * hbh_ref[...], 0.0)
        tbuf[8:, :] = t
        dcv_v = dcv_ref[...]
        dbuf[0:tm, :] = dcv_v
        dbuf[tm:, :] = jnp.where(i < nt - 1, dcvn_ref[...], 0.0)
        dt = w_ref[2:3, :] * dcv_v + w_ref[1:2, :] * dbuf[pl.ds(1, tm), :] + w_ref[0:1, :] * dbuf[pl.ds(2, tm), :]
        dcg_ref[...] = (dt * hb).astype(bf16)
        dhb_ref[...] = (dt * cg).astype(bf16)
        g2 = jnp.sum(dcv_v * t, axis=0, keepdims=True)
        g1 = jnp.sum(dcv_v * tbuf[pl.ds(7, tm), :], axis=0, keepdims=True)
        g0 = jnp.sum(dcv_v * tbuf[pl.ds(6, tm), :], axis=0, keepdims=True)
        part = jnp.concatenate([g0, g1, g2, jnp.zeros((5, 512), f32)], axis=0)

        @pl.when(i == 0)
        def _():
            gw_ref[...] = part

        @pl.when(i > 0)
        def _():
            gw_ref[...] += part

    row = lambda w, c: pl.BlockSpec((tm, w), lambda i: (i, c))
    return pl.pallas_call(
        body, name="mix0_bwd_b", grid=(nt,),
        in_specs=[row(512, 0), _next_halo(tm, 8, 0, S), row(512, E_CG // 512), row(512, E_HB // 512),
                  _prev_halo(tm, 8, E_CG // 512), _prev_halo(tm, 8, E_HB // 512),
                  pl.BlockSpec((SC_WIDTH, 512), lambda i: (0, 0))],
        out_specs=[row(512, 0), row(512, 0), pl.BlockSpec((8, 512), lambda i: (0, 0))],
        out_shape=[jax.ShapeDtypeStruct((S, 512), bf16), jax.ShapeDtypeStruct((S, 512), bf16),
                   jax.ShapeDtypeStruct((8, 512), f32)],
        scratch_shapes=[pltpu.VMEM((tm + 8, 512), f32), pltpu.VMEM((tm + 8, 512), f32)],
        compiler_params=_cp(("arbitrary",)),
    )(dcv, dcv, proj, proj, proj, proj, conv_w)


def _qk_bwd(dq_g, dk_g, dv_g, proj, tabs, nw, hm, dbg, dcg, dhb, dz):
    S = proj.shape[0]
    tm = _tile(S, 256)

    def body(*refs):
        d_refs = refs[0:6]
        dv_refs = refs[6:9]
        x_ref, c_ref, s1_ref, s2_ref, nw_ref, m_ref, dbg_ref, dcg_ref, dhb_ref, dz_ref, o_ref, gw_ref = refs[9:]
        i = pl.program_id(0)
        c, s1, s2, m = c_ref[...], s1_ref[...], s2_ref[...], m_ref[...]
        accs = []
        for kind in range(2):
            w = nw_ref[kind:kind + 1, :]
            acc = jnp.zeros((1, LANES), f32)
            for gi in range(N_GROUPS):
                for p in range(4):
                    col = kind * 1536 + gi * 512 + p * LANES
                    dout = d_refs[kind * 3 + gi][:, p * LANES:(p + 1) * LANES]
                    t = x_ref[:, col:col + LANES]
                    dthat = (dout * c + pltpu.roll(dout * s1, LANES - ROT_HALF, axis=1)
                             + pltpu.roll(dout * s2, ROT_HALF, axis=1))
                    r = lax.rsqrt(_head_mean(t * t, m) + EPS)
                    tn = t * r
                    acc = acc + jnp.sum(dthat * tn, axis=0, keepdims=True)
                    dtn = dthat * w
                    o_ref[:, col:col + LANES] = (r * (dtn - tn * _head_mean(dtn * tn, m))).astype(bf16)
            accs.append(acc + pltpu.roll(acc, HEAD_DIM, axis=1))
        for gi in range(N_GROUPS):
            o_ref[:, E_V + gi * 512:E_V + (gi + 1) * 512] = dv_refs[gi][...].astype(bf16)
        o_ref[:, E_BG:E_CG] = dbg_ref[...]
        o_ref[:, E_CG:E_HB] = dcg_ref[...]
        o_ref[:, E_HB:E_Z] = dhb_ref[...]
        o_ref[:, E_Z:] = dz_ref[...]
        part = jnp.concatenate([accs[0], accs[1], jnp.zeros((6, LANES), f32)], axis=0)

        @pl.when(i == 0)
        def _():
            gw_ref[...] = part

        @pl.when(i > 0)
        def _():
            gw_ref[...] += part

    row = lambda w, c: pl.BlockSpec((tm, w), lambda i: (i, c))
    tab = row(LANES, 0)
    return pl.pallas_call(
        body, name="qk_bwd", grid=(S // tm,),
        in_specs=[row(512, 0)] * 9 + [row(3072, 0), tab, tab, tab, pl.BlockSpec((2, LANES), lambda i: (0, 0)),
                                      pl.BlockSpec((LANES, LANES), lambda i: (0, 0)),
                                      row(512, 0), row(512, 0), row(512, 0), row(1024, 0)],
        out_specs=[row(EVEN_IN, 0), pl.BlockSpec((8, LANES), lambda i: (0, 0))],
        out_shape=[jax.ShapeDtypeStruct((S, EVEN_IN), bf16), jax.ShapeDtypeStruct((8, LANES), f32)],
        compiler_params=_cp(("arbitrary",)),
    )(*dq_g, *dk_g, *dv_g, proj, *tabs, nw, hm, dbg, dcg, dhb, dz)


def _inv_count(i, tm, p):
    rowg = lax.broadcasted_iota(jnp.int32, (tm, 1), 0) + i * tm
    return 1.0 / jnp.minimum(rowg + 1, p).astype(f32)


def _layer_norm_stats(c):
    mu = jnp.mean(c, axis=-1, keepdims=True)
    cen = c - mu
    rstd = lax.rsqrt(jnp.mean(cen * cen, axis=-1, keepdims=True) + EPS)
    return cen * rstd, rstd


def _fill_pool_buf(i, ubuf, uc_ref, uch_ref):
    ubuf[0:16, :] = jnp.where(i > 0, uch_ref[...], 0.0)
    ubuf[16:, :] = uc_ref[...]


def _pooled(i, tm, ubuf, gi):
    p = POOL_SIZES[gi]
    cols = slice(gi * LANES, (gi + 1) * LANES)
    acc = ubuf[pl.ds(16, tm), cols]
    cur = acc
    for jj in range(1, p):
        acc = acc + ubuf[pl.ds(16 - jj, tm), cols]
    return acc * _inv_count(i, tm, p) - cur


def _fill_glu_buf(i, gbuf, da_ref, dg_ref, dah_ref, dgh_ref):
    gbuf[0:32, :] = jnp.where(i > 0, dah_ref[...] * _sigmoid(dgh_ref[...]), 0.0)
    gbuf[32:, :] = da_ref[...] * _sigmoid(dg_ref[...])


def _mix1_fwd(proj, pool_w, pool_scale, dconv_w, dconv_b, ln_w, ln_b):
    S = proj.shape[0]
    tm = _tile(S, 256)

    def body(uc_ref, uch_ref, da_ref, dg_ref, dah_ref, dgh_ref, za_ref, zb_ref, pw_ref, ps_ref, cw_ref, cb_ref,
             lw_ref, lb_ref, u_ref, c_ref, mc_ref, ubuf, gbuf):
        i = pl.program_id(0)
        _fill_pool_buf(i, ubuf, uc_ref, uch_ref)
        za = za_ref[...]
        for gi in range(4):
            cols = slice(gi * LANES, (gi + 1) * LANES)
            mc = jnp.dot(_pooled(i, tm, ubuf, gi).astype(bf16), pw_ref[gi], preferred_element_type=f32)
            mc_ref[:, cols] = mc
            zg = za[:, cols]
            u_ref[:, cols] = (mc * ps_ref[:, cols] * (zg * _sigmoid(zg))).astype(bf16)
        _fill_glu_buf(i, gbuf, da_ref, dg_ref, dah_ref, dgh_ref)
        c = jnp.zeros((tm, 512), f32) + cb_ref[...]
        for k in range(D_CONV):
            c = c + cw_ref[k:k + 1, :] * gbuf[pl.ds(32 - (D_CONV - 1) + k, tm), :]
        c_ref[...] = c
        yhat, _ = _layer_norm_stats(c)
        l = yhat * lw_ref[...] + lb_ref[...]
        zb = zb_ref[...]
        u_ref[:, 512:] = (l * _sigmoid(l) * (zb * _sigmoid(zb))).astype(bf16)

    row = lambda w, c: pl.BlockSpec((tm, w), lambda i: (i, c))
    vec = pl.BlockSpec((1, 512), lambda i: (0, 0))
    return pl.pallas_call(
        body, name="mix1_fwd", grid=(S // tm,),
        in_specs=[row(512, 0), _prev_halo(tm, 16, 0), row(512, 1), row(512, 2), _prev_halo(tm, 32, 1), _prev_halo(tm, 32, 2),
                  row(512, 3), row(512, 4), pl.BlockSpec((4, LANES, LANES), lambda i: (0, 0, 0)), vec,
                  pl.BlockSpec((D_CONV, 512), lambda i: (0, 0)), vec, vec, vec],
        out_specs=[row(1024, 0), row(512, 0), row(512, 0)],
        out_shape=[jax.ShapeDtypeStruct((S, D_MODEL), bf16), jax.ShapeDtypeStruct((S, 512), f32),
                   jax.ShapeDtypeStruct((S, 512), f32)],
        scratch_shapes=[pltpu.VMEM((tm + 16, 512), f32), pltpu.VMEM((tm + 32, 512), f32)],
        compiler_params=_cp(("parallel",)),
    )(proj, proj, proj, proj, proj, proj, proj, proj, pool_w, pool_scale, dconv_w, dconv_b, ln_w, ln_b)


def _mix1_bwd_a(du, proj, c, mc, pool_w, pool_scale, ln_w, ln_b):
    S = proj.shape[0]
    tm = _tile(S, 256)

    def body(du_ref, za_ref, zb_ref, c_ref, mc_ref, pw_ref, ps_ref, lw_ref, lb_ref,
             dz_ref, dc_ref, dpl_ref, dmc_ref, acc_ref):
        i = pl.program_id(0)
        du_v = du_ref[...]
        ps = ps_ref[...]
        za = za_ref[...]
        sga = _sigmoid(za)
        mcv = mc_ref[...]
        dz_ref[:, :512] = (du_v[:, :512] * (mcv * ps) * _dsilu(za, sga)).astype(bf16)
        dyc = du_v[:, :512] * (za * sga)
        g_ps = jnp.sum(dyc * mcv, axis=0, keepdims=True)
        dmc = (dyc * ps).astype(bf16)
        dmc_ref[...] = dmc
        for gi in range(4):
            cols = slice(gi * LANES, (gi + 1) * LANES)
            dpl_ref[:, cols] = lax.dot_general(dmc[:, cols], pw_ref[gi], (((1,), (1,)), ((), ())), preferred_element_type=f32)
        yhat, rstd = _layer_norm_stats(c_ref[...])
        lw = lw_ref[...]
        l = yhat * lw + lb_ref[...]
        sgl = _sigmoid(l)
        zb = zb_ref[...]
        sgb = _sigmoid(zb)
        dz_ref[:, 512:] = (du_v[:, 512:] * (l * sgl) * _dsilu(zb, sgb)).astype(bf16)
        dl = du_v[:, 512:] * (zb * sgb) * _dsilu(l, sgl)
        g_lb = jnp.sum(dl, axis=0, keepdims=True)
        g_lw = jnp.sum(dl * yhat, axis=0, keepdims=True)
        dyh = dl * lw
        dc = rstd * (dyh - jnp.mean(dyh, axis=-1, keepdims=True) - yhat * jnp.mean(dyh * yhat, axis=-1, keepdims=True))
        dc_ref[...] = dc
        g_db = jnp.sum(dc, axis=0, keepdims=True)
        part = jnp.concatenate([g_ps, g_lw, g_lb, g_db, jnp.zeros((4, 512), f32)], axis=0)

        @pl.when(i == 0)
        def _():
            acc_ref[...] = part

        @pl.when(i > 0)
        def _():
            acc_ref[...] += part

    row = lambda w, c_: pl.BlockSpec((tm, w), lambda i: (i, c_))
    vec = pl.BlockSpec((1, 512), lambda i: (0, 0))
    return pl.pallas_call(
        body, name="mix1_bwd_a", grid=(S // tm,),
        in_specs=[row(1024, 0), row(512, 3), row(512, 4), row(512, 0), row(512, 0),
                  pl.BlockSpec((4, LANES, LANES), lambda i: (0, 0, 0)), vec, vec, vec],
        out_specs=[row(1024, 0), row(512, 0), row(512, 0), row(512, 0), pl.BlockSpec((8, 512), lambda i: (0, 0))],
        out_shape=[jax.ShapeDtypeStruct((S, D_MODEL), bf16), jax.ShapeDtypeStruct((S, 512), f32),
                   jax.ShapeDtypeStruct((S, 512), f32), jax.ShapeDtypeStruct((S, 512), bf16),
                   jax.ShapeDtypeStruct((8, 512), f32)],
        compiler_params=_cp(("arbitrary",)),
    )(du, proj, proj, c, mc, pool_w, pool_scale, ln_w, ln_b)


def _mix1_bwd_b(dc, dpl, dmc, dz, proj, dconv_w):
    S = proj.shape[0]
    tm = _tile(S, 256)
    nt = S // tm

    def body(dc_ref, dcn_ref, dpl_ref, dpn_ref, dmc_ref, dz_ref, uc_ref, uch_ref, da_ref, dg_ref, dah_ref, dgh_ref,
             cw_ref, o_ref, gcw_ref, gpw_ref, ubuf, gbuf, dcbuf, dpbuf):
        i = pl.program_id(0)
        last = i == nt - 1
        _fill_pool_buf(i, ubuf, uc_ref, uch_ref)
        _fill_glu_buf(i, gbuf, da_ref, dg_ref, dah_ref, dgh_ref)
        dc_v = dc_ref[...]
        dcbuf[0:tm, :] = dc_v
        dcbuf[tm:, :] = jnp.where(last, 0.0, dcn_ref[...])
        dpl_v = dpl_ref[...]
        for gi in range(4):
            p = POOL_SIZES[gi]
            cols = slice(gi * LANES, (gi + 1) * LANES)
            dpbuf[0:tm, cols] = dpl_v[:, cols] * _inv_count(i, tm, p)
            dpbuf[tm:, cols] = jnp.where(last, 0.0, dpn_ref[:, cols] * (1.0 / p))
        gpw = []
        for gi in range(4):
            p = POOL_SIZES[gi]
            cols = slice(gi * LANES, (gi + 1) * LANES)
            acc = -dpl_v[:, cols]
            for jj in range(p):
                acc = acc + dpbuf[pl.ds(jj, tm), cols]
            o_ref[:, cols] = acc.astype(bf16)
            pooled = _pooled(i, tm, ubuf, gi).astype(bf16)
            gpw.append(lax.dot_general(pooled, dmc_ref[:, cols], (((0,), (0,)), ((), ())), preferred_element_type=f32))
        dgl = jnp.zeros((tm, 512), f32)
        gcw = []
        for k in range(D_CONV):
            lag = D_CONV - 1 - k
            dgl = dgl + cw_ref[k:k + 1, :] * dcbuf[pl.ds(lag, tm), :]
            gcw.append(jnp.sum(dc_v * gbuf[pl.ds(32 - lag, tm), :], axis=0, keepdims=True))
        gcw.append(jnp.zeros((1, 512), f32))
        da = da_ref[...]
        sg = _sigmoid(dg_ref[...])
        o_ref[:, O_DA:O_DG] = (dgl * sg).astype(bf16)
        o_ref[:, O_DG:O_Z] = (dgl * da * sg * (1.0 - sg)).astype(bf16)
        o_ref[:, O_Z:] = dz_ref[...]
        gcw_part = jnp.concatenate(gcw, axis=0)

        @pl.when(i == 0)
        def _():
            gcw_ref[...] = gcw_part
            for gi in range(4):
                gpw_ref[gi] = gpw[gi]

        @pl.when(i > 0)
        def _():
            gcw_ref[...] += gcw_part
            for gi in range(4):
                gpw_ref[gi] += gpw[gi]

    row = lambda w, c_: pl.BlockSpec((tm, w), lambda i: (i, c_))
    return pl.pallas_call(
        body, name="mix1_bwd_b", grid=(nt,),
        in_specs=[row(512, 0), _next_halo(tm, 32, 0, S), row(512, 0), _next_halo(tm, 16, 0, S), row(512, 0), row(1024, 0),
                  row(512, 0), _prev_halo(tm, 16, 0), row(512, 1), row(512, 2), _prev_halo(tm, 32, 1), _prev_halo(tm, 32, 2),
                  pl.BlockSpec((D_CONV, 512), lambda i: (0, 0))],
        out_specs=[row(ODD_IN, 0), pl.BlockSpec((32, 512), lambda i: (0, 0)),
                   pl.BlockSpec((4, LANES, LANES), lambda i: (0, 0, 0))],
        out_shape=[jax.ShapeDtypeStruct((S, ODD_IN), bf16), jax.ShapeDtypeStruct((32, 512), f32),
                   jax.ShapeDtypeStruct((4, LANES, LANES), f32)],
        scratch_shapes=[pltpu.VMEM((tm + 16, 512), f32), pltpu.VMEM((tm + 32, 512), f32),
                        pltpu.VMEM((tm + 32, 512), f32), pltpu.VMEM((tm + 16, 512), f32)],
        compiler_params=_cp(("arbitrary",)),
    )(dc, dc, dpl, dpl, dmc, dz, proj, proj, proj, proj, proj, proj, dconv_w)


def _local_step(x, pos_col, target, e_norm_w, e_w_in, e_q_norm_w, e_k_norm_w, e_conv_w, e_w_out,
                o_norm_w, o_w_in, o_pool_w, o_pool_scale, o_dconv_w, o_dconv_b, o_ln_w, o_ln_b, o_w_out):
    hm = _head_mean_matrix()
    nw = jnp.concatenate([jnp.tile(e_q_norm_w, (1, 2)), jnp.tile(e_k_norm_w, (1, 2))], axis=0)
    tabs = _rope_tables(pos_col)
    pool_wb = o_pool_w.astype(bf16)

    h0 = _rms_fwd(x, e_norm_w, name="rms0_fwd")
    proj0 = _mm_nn(h0, e_w_in, name="in_proj0")
    qk = _qk_prep(proj0, tabs, nw, hm)
    o_g, lse_g = [], []
    for g in range(N_GROUPS):
        o, l = _attn_fwd(qk, proj0, g, name=f"attn_fwd{g}")
        o_g.append(o)
        lse_g.append(l)
    u0, o_a, lt = _mix0_fwd(o_g, lse_g, proj0, e_conv_w)
    x1 = _mm_nn(u0, e_w_out, res=x, name="out_proj0")
    h1 = _rms_fwd(x1, o_norm_w, name="rms1_fwd")
    proj1 = _mm_nn(h1, o_w_in, name="in_proj1", tn=512)
    u1, c1, mc1 = _mix1_fwd(proj1, pool_wb, o_pool_scale, o_dconv_w, o_dconv_b, o_ln_w, o_ln_b)
    dy, dyb, loss = _mm_out_loss(u1, o_w_out, x1, target, name="out_proj1_loss")
    g_o_w_out = _mm_tn(u1, dyb, name="g_w_out1", out_dtype=bf16)
    du1 = _mm_nt(dyb, o_w_out, name="d_u1")
    dz1, dc1, dpl1, dmc1, sums1 = _mix1_bwd_a(du1, proj1, c1, mc1, pool_wb, o_pool_scale, o_ln_w, o_ln_b)
    dproj1, g_dconv_w, g_pool_w = _mix1_bwd_b(dc1, dpl1, dmc1, dz1, proj1, o_dconv_w)
    g_o_w_in = _mm_tn(h1, dproj1, name="g_w_in1", out_dtype=bf16)
    dh1 = _mm_nt(dproj1, o_w_in, name="d_h1")
    d1, d1b, g_o_norm = _rms_bwd(x1, o_norm_w, dh1, dy, name="rms1_bwd")
    g_e_w_out = _mm_tn(u0, d1b, name="g_w_out0", out_dtype=bf16)
    du0 = _mm_nt(d1b, e_w_out, name="d_u0")
    dz0, do_a, dsum, dbg, dcv = _mix0_bwd_a(du0, proj0, o_a, e_conv_w)
    dcg, dhb, g_conv_w = _mix0_bwd_b(dcv, proj0, e_conv_w)
    dq_g, dk_g, dv_g = [], [], []
    for g in range(N_GROUPS):
        dq, dk, dv = _attn_bwd(qk, proj0, do_a, lt, dsum, g, name=f"attn_bwd{g}")
        dq_g.append(dq)
        dk_g.append(dk)
        dv_g.append(dv)
    dproj0, g_qk_norm = _qk_bwd(dq_g, dk_g, dv_g, proj0, tabs, nw, hm, dbg, dcg, dhb, dz0)
    g_e_w_in = _mm_tn(h0, dproj0, name="g_w_in0", out_dtype=bf16, chunks=N_DEV)
    dh0 = _mm_nt(dproj0, e_w_in, name="d_h0")
    grad_x, _, g_e_norm = _rms_bwd(x, e_norm_w, dh0, d1, name="rms0_bwd")

    grads = dict(
        e_norm_w=g_e_norm, e_w_in=g_e_w_in,
        e_q_norm_w=g_qk_norm[0:1, :HEAD_DIM], e_k_norm_w=g_qk_norm[1:2, :HEAD_DIM],
        e_conv_w=g_conv_w[:SC_WIDTH], e_w_out=g_e_w_out,
        o_norm_w=g_o_norm, o_w_in=g_o_w_in, o_pool_w=g_pool_w,
        o_pool_scale=sums1[0:1], o_dconv_w=g_dconv_w[:D_CONV], o_dconv_b=sums1[3:4],
        o_ln_w=sums1[1:2], o_ln_b=sums1[2:3], o_w_out=g_o_w_out)
    return loss, grad_x, grads


_MESH_ID = pl.DeviceIdType.MESH
_HBM = pl.BlockSpec(memory_space=pl.ANY)


def _place():
    x, y, c = lax.axis_index("x"), lax.axis_index("y"), lax.axis_index("c")
    return x, y, c


def _all_gather(arrs, *, name):
    n = len(arrs)

    def body(*refs):
        ins, outs = refs[:n], refs[n:2 * n]
        send_sems, recv_sems, local_sems = refs[2 * n:]
        x, y, c = _place()
        me, sibling = (x, y, c), (x, y, 1 - c)
        chips = [(1 - x, y), (x, 1 - y), (1 - x, 1 - y)]

        def slot(t, px, py, pc):
            return outs[t].at[4 * px + 2 * py + pc]

        def copy(t, k, block, to, src=None):
            dst = slot(t, *block)
            return pltpu.make_async_remote_copy(
                src_ref=dst if src is None else src, dst_ref=dst,
                send_sem=send_sems.at[7 * t + k], recv_sem=recv_sems.at[7 * t + k],
                device_id=to, device_id_type=_MESH_ID)

        mine = [pltpu.make_async_copy(ins[t], slot(t, *me), local_sems.at[t]) for t in range(n)]
        for cp in mine:
            cp.start()
        first = []
        for t in range(n):
            first.append(copy(t, 0, me, sibling, src=ins[t]))
            first += [copy(t, 1 + j, me, (*chip, c), src=ins[t]) for j, chip in enumerate(chips)]
        for cp in first:
            cp.start()
        passed = []
        for j, chip in enumerate(chips):
            for t in range(n):
                copy(t, 1 + j, (*chip, c), me).wait_recv()
                fwd = copy(t, 4 + j, (*chip, c), sibling)
                fwd.start()
                passed.append(fwd)
        for t in range(n):
            copy(t, 0, sibling, me).wait_recv()
            for j, chip in enumerate(chips):
                copy(t, 4 + j, (*chip, 1 - c), me).wait_recv()
        for cp in first + passed:
            cp.wait_send()
        for cp in mine:
            cp.wait()

    return pl.pallas_call(
        body, name=name,
        in_specs=[_HBM] * n, out_specs=[_HBM] * n,
        out_shape=[jax.ShapeDtypeStruct((N_DEV, *a.shape), a.dtype) for a in arrs],
        scratch_shapes=[pltpu.SemaphoreType.DMA((7 * n,)), pltpu.SemaphoreType.DMA((7 * n,)),
                        pltpu.SemaphoreType.DMA((n,))],
    )(*arrs)


def _exchange(chunked, whole, *, name):
    arrs = list(chunked) + list(whole)
    n, nc = len(arrs), len(chunked)

    def body(*refs):
        ins, outs = refs[:n], refs[n:2 * n]
        send_sems, recv_sems, local_sems = refs[2 * n:]
        x, y, c = _place()
        me_i = 4 * x + 2 * y + c

        def src(t, dev_i):
            return ins[t].at[dev_i] if t < nc else ins[t]

        mine = [pltpu.make_async_copy(src(t, me_i), outs[t].at[me_i], local_sems.at[t]) for t in range(n)]
        for cp in mine:
            cp.start()
        sends = []
        for m in range(1, N_DEV):
            px = 1 - x if m & 4 else x
            py = 1 - y if m & 2 else y
            pc = 1 - c if m & 1 else c
            peer_i = 4 * px + 2 * py + pc
            for t in range(n):
                sends.append((t, m, peer_i, pltpu.make_async_remote_copy(
                    src_ref=src(t, peer_i), dst_ref=outs[t].at[me_i],
                    send_sem=send_sems.at[7 * t + m - 1], recv_sem=recv_sems.at[7 * t + m - 1],
                    device_id=(px, py, pc), device_id_type=_MESH_ID)))
        for _, _, _, cp in sends:
            cp.start()
        for t, m, peer_i, cp in sends:
            pltpu.make_async_remote_copy(
                src_ref=src(t, peer_i), dst_ref=outs[t].at[peer_i],
                send_sem=send_sems.at[7 * t + m - 1], recv_sem=recv_sems.at[7 * t + m - 1],
                device_id=(x, y, c), device_id_type=_MESH_ID).wait_recv()
        for _, _, _, cp in sends:
            cp.wait_send()
        for cp in mine:
            cp.wait()

    out_shape = [jax.ShapeDtypeStruct(a.shape, a.dtype) for a in chunked]
    out_shape += [jax.ShapeDtypeStruct((N_DEV, *a.shape), a.dtype) for a in whole]
    return pl.pallas_call(
        body, name=name,
        in_specs=[_HBM] * n, out_specs=[_HBM] * n, out_shape=out_shape,
        scratch_shapes=[pltpu.SemaphoreType.DMA((7 * n,)), pltpu.SemaphoreType.DMA((7 * n,)),
                        pltpu.SemaphoreType.DMA((n,))],
    )(*arrs)


def _adamw(w, g, m, v):
    m2 = ADAM_B1 * m + (1.0 - ADAM_B1) * g
    v2 = ADAM_B2 * v + (1.0 - ADAM_B2) * (g * g)
    m_hat = m2 / (1.0 - ADAM_B1 ** ADAM_STEP)
    v_hat = v2 / (1.0 - ADAM_B2 ** ADAM_STEP)
    delta = -ADAM_LR * (m_hat / (jnp.sqrt(v_hat) + ADAM_EPS) + ADAM_WD * w)
    return delta, m2, v2


def _sum_adamw(parts, w, m, v, *, name):
    R, C = w.shape
    tr = _tile(R, 256)

    def body(p_ref, w_ref, m_ref, v_ref, g_ref, d_ref, nm_ref, nv_ref):
        g = p_ref[0].astype(f32)
        for i in range(1, N_DEV):
            g = g + p_ref[i].astype(f32)
        g_ref[...] = g
        d_ref[...], nm_ref[...], nv_ref[...] = _adamw(w_ref[...], g, m_ref[...], v_ref[...])

    spec = pl.BlockSpec((tr, C), lambda i: (i, 0))
    return pl.pallas_call(
        body, name=name, grid=(R // tr,),
        in_specs=[pl.BlockSpec((N_DEV, tr, C), lambda i: (0, i, 0)), spec, spec, spec],
        out_specs=[spec] * 4, out_shape=[jax.ShapeDtypeStruct((R, C), f32)] * 4,
        compiler_params=_cp(("parallel",)),
    )(parts, w, m, v)


def _sum_parts(parts, *, name):
    _, R, C = parts.shape

    def body(p_ref, o_ref):
        g = p_ref[0]
        for i in range(1, N_DEV):
            g = g + p_ref[i]
        o_ref[...] = g

    return pl.pallas_call(body, name=name, out_shape=jax.ShapeDtypeStruct((R, C), f32),
                          compiler_params=pltpu.CompilerParams(vmem_limit_bytes=VMEM_LIMIT))(parts)


def _adamw_small(ws, gs, ms, vs):
    n = len(ws)

    def body(*refs):
        w_r, g_r, m_r, v_r = refs[:n], refs[n:2 * n], refs[2 * n:3 * n], refs[3 * n:4 * n]
        d_r, nm_r, nv_r = refs[4 * n:5 * n], refs[5 * n:6 * n], refs[6 * n:7 * n]
        for t in range(n):
            d_r[t][...], nm_r[t][...], nv_r[t][...] = _adamw(w_r[t][...], g_r[t][...], m_r[t][...], v_r[t][...])

    shapes = [jax.ShapeDtypeStruct(w.shape, f32) for w in ws]
    outs = pl.pallas_call(body, name="adamw_small", out_shape=shapes * 3)(*ws, *gs, *ms, *vs)
    return outs[:n], outs[n:2 * n], outs[2 * n:]


_WEIGHTS = ["e_norm_w", "e_w_in", "e_q_norm_w", "e_k_norm_w", "e_conv_w", "e_w_out", "o_norm_w", "o_w_in", "o_pool_w",
            "o_pool_scale", "o_dconv_w", "o_dconv_b", "o_ln_w", "o_ln_b", "o_w_out"]
_BIG = ["e_w_in", "e_w_out", "o_w_in", "o_w_out"]
_SMALL_SHARDED = ["e_conv_w", "o_norm_w", "o_pool_scale", "o_dconv_w", "o_dconv_b", "o_ln_w", "o_ln_b"]
_SMALL_ALL = ["e_norm_w", "e_q_norm_w", "e_k_norm_w", "e_conv_w", "o_norm_w", "o_pool_w", "o_pool_scale", "o_dconv_w",
              "o_dconv_b", "o_ln_w", "o_ln_b"]


def _pack_rows(pieces):
    rows, offs, r0 = [], [], 0
    for p in pieces:
        flat = p.reshape(-1)
        nr = -(-flat.shape[0] // (8 * LANES)) * 8
        rows.append(jnp.pad(flat, (0, nr * LANES - flat.shape[0])).reshape(nr, LANES))
        offs.append((r0, nr))
        r0 += nr
    return jnp.concatenate(rows, axis=0), offs


def _unpack_rows(buf, off, shape):
    r0, nr = off
    size = int(np.prod(shape))
    return buf[..., r0:r0 + nr, :].reshape(*buf.shape[:-2], nr * LANES)[..., :size].reshape(*buf.shape[:-2], *shape)


def kernel(x, positions, e_norm_w, e_w_in, e_q_norm_w, e_k_norm_w, e_conv_w, e_w_out, o_norm_w, o_w_in, o_pool_w, o_pool_scale, o_dconv_w, o_dconv_b, o_ln_w, o_ln_b, o_w_out, loss_target, m_e_norm_w, m_e_w_in, m_e_q_norm_w, m_e_k_norm_w, m_e_conv_w, m_e_w_out, m_o_norm_w, m_o_w_in, m_o_pool_w, m_o_pool_scale, m_o_dconv_w, m_o_dconv_b, m_o_ln_w, m_o_ln_b, m_o_w_out, v_e_norm_w, v_e_w_in, v_e_q_norm_w, v_e_k_norm_w, v_e_conv_w, v_e_w_out, v_o_norm_w, v_o_w_in, v_o_pool_w, v_o_pool_scale, v_o_dconv_w, v_o_dconv_b, v_o_ln_w, v_o_ln_b, v_o_w_out):
    w = dict(e_norm_w=e_norm_w, e_w_in=e_w_in, e_q_norm_w=e_q_norm_w, e_k_norm_w=e_k_norm_w, e_conv_w=e_conv_w,
             e_w_out=e_w_out, o_norm_w=o_norm_w, o_w_in=o_w_in, o_pool_w=o_pool_w, o_pool_scale=o_pool_scale,
             o_dconv_w=o_dconv_w, o_dconv_b=o_dconv_b, o_ln_w=o_ln_w, o_ln_b=o_ln_b, o_w_out=o_w_out)
    m = dict(e_norm_w=m_e_norm_w, e_w_in=m_e_w_in, e_q_norm_w=m_e_q_norm_w, e_k_norm_w=m_e_k_norm_w, e_conv_w=m_e_conv_w,
             e_w_out=m_e_w_out, o_norm_w=m_o_norm_w, o_w_in=m_o_w_in, o_pool_w=m_o_pool_w, o_pool_scale=m_o_pool_scale,
             o_dconv_w=m_o_dconv_w, o_dconv_b=m_o_dconv_b, o_ln_w=m_o_ln_w, o_ln_b=m_o_ln_b, o_w_out=m_o_w_out)
    v = dict(e_norm_w=v_e_norm_w, e_w_in=v_e_w_in, e_q_norm_w=v_e_q_norm_w, e_k_norm_w=v_e_k_norm_w, e_conv_w=v_e_conv_w,
             e_w_out=v_e_w_out, o_norm_w=v_o_norm_w, o_w_in=v_o_w_in, o_pool_w=v_o_pool_w, o_pool_scale=v_o_pool_scale,
             o_dconv_w=v_o_dconv_w, o_dconv_b=v_o_dconv_b, o_ln_w=v_o_ln_w, o_ln_b=v_o_ln_b, o_w_out=v_o_w_out)
    S = x.shape[1]
    me = 4 * lax.axis_index("x") + 2 * lax.axis_index("y") + lax.axis_index("c")

    small_local, small_offs = _pack_rows([w[n_] for n_ in _SMALL_SHARDED])
    g_e_in, g_e_out, g_o_in, g_o_out, g_small = _all_gather(
        [w["e_w_in"][0].astype(bf16), w["e_w_out"][0].astype(bf16), w["o_w_in"][0].astype(bf16),
         w["o_w_out"][0].astype(bf16), small_local], name="gather_weights")
    full = {}
    for n_, off in zip(_SMALL_SHARDED, small_offs):
        shard = _unpack_rows(g_small, off, w[n_].shape[1:])
        full[n_] = jnp.moveaxis(shard, 0, -2).reshape(*shard.shape[1:-1], N_DEV * shard.shape[-1])
    o_w_in_full = jnp.moveaxis(g_o_in, 0, 1).reshape(D_MODEL, ODD_IN)

    loss_blk, grad_x, g = _local_step(
        x[0], positions.reshape(S, 1), loss_target[0], w["e_norm_w"], g_e_in, w["e_q_norm_w"], w["e_k_norm_w"],
        full["e_conv_w"], g_e_out.reshape(D_MODEL, D_MODEL), full["o_norm_w"].reshape(1, D_MODEL), o_w_in_full,
        w["o_pool_w"][0], full["o_pool_scale"].reshape(1, 512), full["o_dconv_w"], full["o_dconv_b"].reshape(1, 512),
        full["o_ln_w"].reshape(1, 512), full["o_ln_b"].reshape(1, 512), g_o_out.reshape(D_MODEL, D_MODEL))
    loss = lax.psum(loss_blk[0, 0], ("x", "y", "c"))

    small_grads, sg_offs = _pack_rows([g[n_] for n_ in _SMALL_ALL])
    r_e_in, r_e_out, r_o_in, r_o_out, r_small = _exchange(
        [g["e_w_in"], g["e_w_out"].reshape(N_DEV, D_MODEL // N_DEV, D_MODEL),
         jnp.moveaxis(g["o_w_in"].reshape(D_MODEL, N_DEV, ODD_IN // N_DEV), 1, 0),
         g["o_w_out"].reshape(N_DEV, D_MODEL // N_DEV, D_MODEL)],
        [small_grads], name="exchange_grads")

    out_g, out_d, out_m, out_v = {}, {}, {}, {}
    for n_, parts in zip(_BIG, (r_e_in, r_e_out, r_o_in, r_o_out)):
        res = _sum_adamw(parts, w[n_][0], m[n_][0], v[n_][0], name="adamw_" + n_)
        out_g[n_], out_d[n_], out_m[n_], out_v[n_] = [r[None] for r in res]
    small_sum = _sum_parts(r_small, name="sum_small_grads")
    gs = []
    for n_, off in zip(_SMALL_ALL, sg_offs):
        gfull = _unpack_rows(small_sum, off, g[n_].shape)
        if n_ in _SMALL_SHARDED:
            width = w[n_].shape[-1]
            gfull = lax.dynamic_slice_in_dim(gfull, me * width, width, axis=gfull.ndim - 1)
        gs.append(gfull.reshape(w[n_].shape))
    ds, nms, nvs = _adamw_small([w[n_] for n_ in _SMALL_ALL], gs, [m[n_] for n_ in _SMALL_ALL], [v[n_] for n_ in _SMALL_ALL])
    for n_, g_, d_, nm_, nv_ in zip(_SMALL_ALL, gs, ds, nms, nvs):
        out_g[n_], out_d[n_], out_m[n_], out_v[n_] = g_, d_, nm_, nv_

    return (loss, grad_x[None], *[out_g[n_] for n_ in _WEIGHTS], *[out_d[n_] for n_ in _WEIGHTS],
            *[out_m[n_] for n_ in _WEIGHTS], *[out_v[n_] for n_ in _WEIGHTS])
```

```python
import functools

import numpy as np
import jax
import jax.numpy as jnp
from jax import lax
from jax.experimental import pallas as pl
from jax.experimental.pallas import tpu as pltpu

f32 = jnp.float32
bf16 = jnp.bfloat16

D_MODEL = 1024
HEAD_DIM = 64
N_GROUPS = 3
DILATIONS = (1, 4, 16)
QBLK = 128
A_WIDTH = 512
EVEN_IN = 7168
ODD_IN = 2560
POOL_SIZES = (2, 4, 8, 16)
D_CONV = 31
SC_WIDTH = 3
ROT_HALF = 8
ROPE_THETA = 500000.0
EPS = 1e-6
NEG = -1e30
SCALE = HEAD_DIM ** -0.5
N_DEV = 8
LANES = 128
VMEM_LIMIT = 48 * 1024 * 1024

ADAM_LR = 0.001
ADAM_B1 = 0.9
ADAM_B2 = 0.999
ADAM_EPS = 1e-08
ADAM_WD = 0.01
ADAM_STEP = 10

E_Q, E_K, E_V, E_BG, E_CG, E_HB, E_Z = 0, 1536, 3072, 4608, 5120, 5632, 6144
O_UC, O_DA, O_DG, O_Z = 0, 512, 1024, 1536


def _cp(sem):
    return pltpu.CompilerParams(dimension_semantics=sem, vmem_limit_bytes=VMEM_LIMIT)


_HBM_ANY = pl.BlockSpec(memory_space=pl.ANY)


def _sigmoid(z):
    return 1.0 / (1.0 + jnp.exp(-z))


def _tile(n, pref):
    t = pref
    while n % t:
        t //= 2
    return t


def _mm_nn(a, b, *, name, out_dtype=f32, res=None, tn=1024):
    M, K = a.shape
    tm = _tile(M, 1024)
    if b.ndim == 3:
        tn = b.shape[2]
        N = b.shape[0] * tn
        b_spec = pl.BlockSpec((None, K, tn), lambda i, j: (j, 0, 0))
    else:
        N = b.shape[1]
        tn = _tile(N, tn)
        b_spec = pl.BlockSpec((K, tn), lambda i, j: (0, j))

    def body(*refs):
        if res is None:
            a_ref, b_ref, o_ref = refs
        else:
            a_ref, b_ref, r_ref, o_ref = refs
        acc = jnp.dot(a_ref[...], b_ref[...], preferred_element_type=f32)
        if res is not None:
            acc = acc + r_ref[...]
        o_ref[...] = acc.astype(out_dtype)

    in_specs = [pl.BlockSpec((tm, K), lambda i, j: (i, 0)), b_spec]
    args = [a, b]
    if res is not None:
        in_specs.append(pl.BlockSpec((tm, tn), lambda i, j: (i, j)))
        args.append(res)
    return pl.pallas_call(
        body, name=name, grid=(M // tm, N // tn), in_specs=in_specs,
        out_specs=pl.BlockSpec((tm, tn), lambda i, j: (i, j)),
        out_shape=jax.ShapeDtypeStruct((M, N), out_dtype),
        compiler_params=_cp(("parallel", "parallel")),
    )(*args)


def _mm_nt(a, b, *, name, out_dtype=f32):
    M, K = a.shape
    tm = _tile(M, 1024)
    if b.ndim == 3:
        nk, N, tk = b.shape
        b_spec = pl.BlockSpec((None, N, tk), lambda i, k: (k, 0, 0))
    else:
        N = b.shape[0]
        tk = _tile(K, 1024) if K % 1024 == 0 else _tile(K, 512)
        nk = K // tk
        b_spec = pl.BlockSpec((N, tk), lambda i, k: (0, k))

    def body(a_ref, b_ref, o_ref, acc_ref):
        k = pl.program_id(1)
        part = lax.dot_general(a_ref[...], b_ref[...], (((1,), (1,)), ((), ())), preferred_element_type=f32)

        @pl.when(k == 0)
        def _():
            acc_ref[...] = part

        @pl.when(k > 0)
        def _():
            acc_ref[...] += part

        @pl.when(k == nk - 1)
        def _():
            o_ref[...] = acc_ref[...].astype(out_dtype)

    return pl.pallas_call(
        body, name=name, grid=(M // tm, nk),
        in_specs=[pl.BlockSpec((tm, tk), lambda i, k: (i, k)), b_spec],
        out_specs=pl.BlockSpec((tm, N), lambda i, k: (i, 0)),
        out_shape=jax.ShapeDtypeStruct((M, N), out_dtype),
        scratch_shapes=[pltpu.VMEM((tm, N), f32)],
        compiler_params=_cp(("parallel", "arbitrary")),
    )(a, b)


def _mm_tn(a, b, *, name, out_dtype=f32, tn=512, chunks=None):
    S, Ka = a.shape
    N = b.shape[1]
    ts = _tile(S, 1024)
    ns = S // ts
    if chunks:
        tn = N // chunks
        out_spec = pl.BlockSpec((None, Ka, tn), lambda j, s: (j, 0, 0))
        out_shape = jax.ShapeDtypeStruct((chunks, Ka, tn), out_dtype)
    else:
        tn = _tile(N, tn)
        out_spec = pl.BlockSpec((Ka, tn), lambda j, s: (0, j))
        out_shape = jax.ShapeDtypeStruct((Ka, N), out_dtype)

    def body(a_ref, b_ref, o_ref, acc_ref):
        s = pl.program_id(1)
        part = lax.dot_general(a_ref[...], b_ref[...], (((0,), (0,)), ((), ())), preferred_element_type=f32)

        @pl.when(s == 0)
        def _():
            acc_ref[...] = part

        @pl.when(s > 0)
        def _():
            acc_ref[...] += part

        @pl.when(s == ns - 1)
        def _():
            o_ref[...] = acc_ref[...].astype(out_dtype)

    return pl.pallas_call(
        body, name=name, grid=(N // tn, ns),
        in_specs=[pl.BlockSpec((ts, Ka), lambda j, s: (s, 0)), pl.BlockSpec((ts, tn), lambda j, s: (s, j))],
        out_specs=out_spec, out_shape=out_shape,
        scratch_shapes=[pltpu.VMEM((Ka, tn), f32)],
        compiler_params=_cp(("parallel", "arbitrary")),
    )(a, b)


def _mm_out_loss(u, w, x_res, target, *, name):
    M, K = u.shape
    N = w.shape[1]
    tm = _tile(M, 512)
    nm = M // tm

    def body(u_ref, w_ref, x_ref, t_ref, dy_ref, dyb_ref, loss_ref, acc_ref):
        i = pl.program_id(0)
        y = jnp.dot(u_ref[...], w_ref[...], preferred_element_type=f32) + x_ref[...]
        err = y - t_ref[...]
        dy = err * (1.0 / N)
        dy_ref[...] = dy
        dyb_ref[...] = dy.astype(bf16)
        part = jnp.sum(err * err, axis=0, keepdims=True)

        @pl.when(i == 0)
        def _():
            acc_ref[...] = part

        @pl.when(i > 0)
        def _():
            acc_ref[...] += part

        @pl.when(i == nm - 1)
        def _():
            tot = jnp.sum(acc_ref[...], axis=1, keepdims=True)
            loss_ref[...] = jnp.broadcast_to(tot * (0.5 / N), (8, LANES))

    return pl.pallas_call(
        body, name=name, grid=(nm,),
        in_specs=[pl.BlockSpec((tm, K), lambda i: (i, 0)), pl.BlockSpec((K, N), lambda i: (0, 0)),
                  pl.BlockSpec((tm, N), lambda i: (i, 0)), pl.BlockSpec((tm, N), lambda i: (i, 0))],
        out_specs=[pl.BlockSpec((tm, N), lambda i: (i, 0)), pl.BlockSpec((tm, N), lambda i: (i, 0)),
                   pl.BlockSpec((8, LANES), lambda i: (0, 0))],
        out_shape=[jax.ShapeDtypeStruct((M, N), f32), jax.ShapeDtypeStruct((M, N), bf16),
                   jax.ShapeDtypeStruct((8, LANES), f32)],
        scratch_shapes=[pltpu.VMEM((1, N), f32)],
        compiler_params=_cp(("arbitrary",)),
    )(u, w, x_res, target)


def _rms_fwd(x, w, *, name):
    S, Dm = x.shape
    tm = _tile(S, 1024)

    def body(x_ref, w_ref, h_ref):
        xv = x_ref[...]
        r = lax.rsqrt(jnp.mean(xv * xv, axis=-1, keepdims=True) + EPS)
        h_ref[...] = (xv * r * w_ref[...]).astype(bf16)

    return pl.pallas_call(
        body, name=name, grid=(S // tm,),
        in_specs=[pl.BlockSpec((tm, Dm), lambda i: (i, 0)), pl.BlockSpec((1, Dm), lambda i: (0, 0))],
        out_specs=pl.BlockSpec((tm, Dm), lambda i: (i, 0)),
        out_shape=jax.ShapeDtypeStruct((S, Dm), bf16),
        compiler_params=_cp(("parallel",)),
    )(x, w)


def _rms_bwd(x, w, dh, res, *, name):
    S, Dm = x.shape
    tm = _tile(S, 512)

    def body(x_ref, w_ref, dh_ref, res_ref, dx_ref, dxb_ref, gw_ref):
        i = pl.program_id(0)
        xv = x_ref[...]
        r = lax.rsqrt(jnp.mean(xv * xv, axis=-1, keepdims=True) + EPS)
        xn = xv * r
        dh_v = dh_ref[...]
        dxn = dh_v * w_ref[...]
        dx = r * (dxn - xn * jnp.mean(dxn * xn, axis=-1, keepdims=True)) + res_ref[...]
        dx_ref[...] = dx
        dxb_ref[...] = dx.astype(bf16)
        part = jnp.sum(dh_v * xn, axis=0, keepdims=True)

        @pl.when(i == 0)
        def _():
            gw_ref[...] = part

        @pl.when(i > 0)
        def _():
            gw_ref[...] += part

    dx, dxb, gw = pl.pallas_call(
        body, name=name, grid=(S // tm,),
        in_specs=[pl.BlockSpec((tm, Dm), lambda i: (i, 0)), pl.BlockSpec((1, Dm), lambda i: (0, 0)),
                  pl.BlockSpec((tm, Dm), lambda i: (i, 0)), pl.BlockSpec((tm, Dm), lambda i: (i, 0))],
        out_specs=[pl.BlockSpec((tm, Dm), lambda i: (i, 0)), pl.BlockSpec((tm, Dm), lambda i: (i, 0)),
                   pl.BlockSpec((1, Dm), lambda i: (0, 0))],
        out_shape=[jax.ShapeDtypeStruct((S, Dm), f32), jax.ShapeDtypeStruct((S, Dm), bf16),
                   jax.ShapeDtypeStruct((1, Dm), f32)],
        compiler_params=_cp(("arbitrary",)),
    )(x, w, dh, res)
    return dx, dxb, gw


_INV_FREQ = [float(v) for v in (np.float32(ROPE_THETA) ** (-np.arange(ROT_HALF, dtype=np.float32) / np.float32(ROT_HALF))).astype(np.float32)]


def _rope_tables(pos_col):
    S = pos_col.shape[0]
    tm = _tile(S, 1024)

    def body(p_ref, c_ref, s1_ref, s2_ref):
        lane = lax.broadcasted_iota(jnp.int32, (tm, LANES), 1)
        lm = lane % HEAD_DIM
        fi = lm % ROT_HALF
        inv = jnp.zeros((tm, LANES), f32)
        for k in range(ROT_HALF):
            inv = jnp.where(fi == k, _INV_FREQ[k], inv)
        ang = p_ref[...].astype(f32) * inv
        cs = jnp.cos(ang)
        sn = jnp.sin(ang)
        c_ref[...] = jnp.where(lm < 2 * ROT_HALF, cs, 1.0)
        s1_ref[...] = jnp.where((lm >= ROT_HALF) & (lm < 2 * ROT_HALF), sn, 0.0)
        s2_ref[...] = jnp.where(lm < ROT_HALF, -sn, 0.0)

    spec = pl.BlockSpec((tm, LANES), lambda i: (i, 0))
    return pl.pallas_call(
        body, name="rope_tables", grid=(S // tm,),
        in_specs=[pl.BlockSpec((tm, 1), lambda i: (i, 0))],
        out_specs=[spec, spec, spec],
        out_shape=[jax.ShapeDtypeStruct((S, LANES), f32)] * 3,
        compiler_params=_cp(("parallel",)),
    )(pos_col)


def _head_mean(v, m):
    hi = v.astype(bf16)
    r1 = v - hi.astype(f32)
    mid = r1.astype(bf16)
    lo = (r1 - mid.astype(f32)).astype(bf16)
    return (jnp.dot(hi, m, preferred_element_type=f32) + jnp.dot(mid, m, preferred_element_type=f32)
            + jnp.dot(lo, m, preferred_element_type=f32))


def _head_mean_matrix():
    i = np.arange(LANES)
    return jnp.asarray(((i[:, None] // HEAD_DIM) == (i[None, :] // HEAD_DIM)).astype(np.float32) / HEAD_DIM, dtype=bf16)


def _qk_prep(proj, tabs, nw, hm):
    S = proj.shape[0]
    tm = _tile(S, 512)

    def body(x_ref, c_ref, s1_ref, s2_ref, nw_ref, m_ref, o_ref):
        cb = pl.program_id(1)
        w = jnp.where(cb >= 3, nw_ref[1:2, :], nw_ref[0:1, :])
        c, s1, s2, m = c_ref[...], s1_ref[...], s2_ref[...], m_ref[...]
        for p in range(4):
            t = x_ref[:, p * LANES:(p + 1) * LANES]
            r = lax.rsqrt(_head_mean(t * t, m) + EPS)
            that = t * r * w
            o_ref[:, p * LANES:(p + 1) * LANES] = (
                that * c + pltpu.roll(that, ROT_HALF, axis=1) * s1 + pltpu.roll(that, LANES - ROT_HALF, axis=1) * s2)

    tab = pl.BlockSpec((tm, LANES), lambda i, j: (i, 0))
    return pl.pallas_call(
        body, name="qk_prep", grid=(S // tm, 6),
        in_specs=[pl.BlockSpec((tm, 512), lambda i, j: (i, j)), tab, tab, tab,
                  pl.BlockSpec((2, LANES), lambda i, j: (0, 0)), pl.BlockSpec((LANES, LANES), lambda i, j: (0, 0))],
        out_specs=pl.BlockSpec((tm, 512), lambda i, j: (i, j)),
        out_shape=jax.ShapeDtypeStruct((S, 3072), f32),
        compiler_params=_cp(("parallel", "parallel")),
    )(proj, *tabs, nw, hm)


def _key_geometry(nparts):
    qr = QBLK // nparts
    rho = lax.broadcasted_iota(jnp.int32, (QBLK, 2 * QBLK), 0)
    kap = lax.broadcasted_iota(jnp.int32, (QBLK, 2 * QBLK), 1)
    n_q = QBLK + nparts * (rho % qr) + rho // qr
    tt = kap % (2 * qr)
    n_k = nparts * tt + kap // (2 * qr)
    dist = n_q - n_k
    return (dist >= 0) & (dist <= QBLK), (tt < qr).astype(jnp.int32)


def _attn_block_fwd(qb, kcat, vcat, mask, lo):
    outs, lses = [], []
    for hh in range(2):
        sel = lo if hh == 0 else jnp.logical_not(lo)
        qm = jnp.where(sel, qb, jnp.zeros_like(qb))
        s = lax.dot_general(qm, kcat, (((1,), (1,)), ((), ())), preferred_element_type=f32) * SCALE
        s = jnp.where(mask, s, NEG)
        mx = jnp.max(s, axis=-1, keepdims=True)
        pexp = jnp.exp(s - mx)
        den = jnp.sum(pexp, axis=-1, keepdims=True)
        pn = (pexp * (1.0 / den)).astype(bf16)
        outs.append(jnp.dot(pn, vcat, preferred_element_type=f32))
        lses.append(jnp.broadcast_to(mx + jnp.log(den), (QBLK, LANES)))
    return jnp.where(lo, outs[0], outs[1]), jnp.where(lo, lses[0], lses[1])


def _attn_block_bwd(qb, dob, kcat, vcat, lt, ds, mask, lo):
    lt_sw = pltpu.roll(lt, HEAD_DIM, axis=1)
    ds_sw = pltpu.roll(ds, HEAD_DIM, axis=1)
    dq = jnp.zeros((QBLK, LANES), f32)
    dk = jnp.zeros((2 * QBLK, LANES), f32)
    dv = jnp.zeros((2 * QBLK, LANES), f32)
    for hh in range(2):
        sel = lo if hh == 0 else jnp.logical_not(lo)
        qm = jnp.where(sel, qb, jnp.zeros_like(qb))
        dom = jnp.where(sel, dob, jnp.zeros_like(dob))
        s = lax.dot_general(qm, kcat, (((1,), (1,)), ((), ())), preferred_element_type=f32) * SCALE
        s = jnp.where(mask, s, NEG)
        lt_h = jnp.where(sel, lt, lt_sw)
        ds_h = jnp.where(sel, ds, ds_sw)
        prob = jnp.exp(s - jnp.concatenate([lt_h, lt_h], axis=1))
        dp = lax.dot_general(dom, vcat, (((1,), (1,)), ((), ())), preferred_element_type=f32)
        dsb = (prob * (dp - jnp.concatenate([ds_h, ds_h], axis=1)) * SCALE).astype(bf16)
        pb = prob.astype(bf16)
        dq = dq + jnp.where(sel, jnp.dot(dsb, kcat, preferred_element_type=f32), 0.0)
        dk = dk + lax.dot_general(dsb, qm, (((0,), (0,)), ((), ())), preferred_element_type=f32)
        dv = dv + lax.dot_general(pb, dom, (((0,), (0,)), ((), ())), preferred_element_type=f32)
    return dq, dk, dv


ATT_ROWS = 2048


def _attn_fwd_local(qk, proj):
    S = qk.shape[0]
    tr = _tile(S, ATT_ROWS)
    lw = 2 * LANES
    nb = tr // QBLK

    def body(q_ref, k_ref, kh_ref, v_ref, vh_ref, o_ref, lse_ref, kbuf, vbuf):
        j = pl.program_id(0)
        kbuf[0:QBLK, :] = jnp.where(j > 0, kh_ref[...], 0.0)
        kbuf[QBLK:, :] = k_ref[...]
        vbuf[0:QBLK, :] = jnp.where(j > 0, vh_ref[...], 0.0)
        vbuf[QBLK:, :] = v_ref[...]
        band, is_prev = _key_geometry(1)
        lo = lax.broadcasted_iota(jnp.int32, (QBLK, LANES), 1) < HEAD_DIM

        def blk(c, carry):
            r0 = pl.multiple_of(c * QBLK, QBLK)
            first = jnp.where((c == 0) & (j == 0), 1, 0)
            mask = band & (is_prev * first == 0)
            for pp in range(lw // LANES):
                lanes = slice(pp * LANES, (pp + 1) * LANES)
                o, lse = _attn_block_fwd(q_ref[pl.ds(r0, QBLK), lanes].astype(bf16),
                                         kbuf[pl.ds(r0, 2 * QBLK), lanes].astype(bf16),
                                         vbuf[pl.ds(r0, 2 * QBLK), lanes].astype(bf16), mask, lo)
                o_ref[pl.ds(r0, QBLK), lanes] = o
                lse_ref[pl.ds(r0, QBLK), lanes] = lse
            return carry

        lax.fori_loop(0, nb, blk, 0)

    def halo(col):
        return pl.BlockSpec((QBLK, lw), lambda j, l: (jnp.maximum(j * nb - 1, 0), col + l))

    def tile(col):
        return pl.BlockSpec((tr, lw), lambda j, l: (j, col + l))

    return pl.pallas_call(
        body, name="attn_fwd0", grid=(S // tr, A_WIDTH // lw),
        in_specs=[tile(E_Q // lw), tile(E_K // lw), halo(E_K // lw), tile(E_V // lw), halo(E_V // lw)],
        out_specs=[tile(0), tile(0)],
        out_shape=[jax.ShapeDtypeStruct((S, A_WIDTH), f32)] * 2,
        scratch_shapes=[pltpu.VMEM((QBLK + tr, lw), f32)] * 2,
        compiler_params=_cp(("parallel", "parallel")),
    )(qk, qk, qk, proj, proj)


def _attn_bwd_local(qk, proj, do_a, lt, dsum):
    S = qk.shape[0]
    tr = _tile(S, ATT_ROWS)
    lw = LANES
    nb = tr // QBLK
    nt = S // tr

    def body(q_ref, qn_ref, do_ref, don_ref, lt_ref, ltn_ref, ds_ref, dsn_ref, k_ref, kh_ref, v_ref, vh_ref,
             dq_ref, dk_ref, dv_ref, kbuf, vbuf, dkbuf, dvbuf):
        j = pl.program_id(0)
        zeros = jnp.zeros((QBLK, lw), f32)
        kbuf[0:QBLK, :] = jnp.where(j > 0, kh_ref[...], 0.0)
        kbuf[pl.ds(QBLK, tr), :] = k_ref[...]
        kbuf[pl.ds(QBLK + tr, QBLK), :] = zeros
        vbuf[0:QBLK, :] = jnp.where(j > 0, vh_ref[...], 0.0)
        vbuf[pl.ds(QBLK, tr), :] = v_ref[...]
        vbuf[pl.ds(QBLK + tr, QBLK), :] = zeros
        dkbuf[...] = jnp.zeros_like(dkbuf)
        dvbuf[...] = jnp.zeros_like(dvbuf)
        band, is_prev = _key_geometry(1)
        lo = lax.broadcasted_iota(jnp.int32, (QBLK, LANES), 1) < HEAD_DIM

        def blk(c, carry):
            r0 = pl.multiple_of(c * QBLK, QBLK)
            first = jnp.where((c == 0) & (j == 0), 1, 0)
            mask = band & (is_prev * first == 0)
            dq, dk, dv = _attn_block_bwd(
                q_ref[pl.ds(r0, QBLK), :].astype(bf16), do_ref[pl.ds(r0, QBLK), :].astype(bf16),
                kbuf[pl.ds(r0, 2 * QBLK), :].astype(bf16), vbuf[pl.ds(r0, 2 * QBLK), :].astype(bf16),
                lt_ref[pl.ds(r0, QBLK), :], ds_ref[pl.ds(r0, QBLK), :], mask, lo)
            dq_ref[pl.ds(r0, QBLK), :] = dq
            dkbuf[pl.ds(r0, 2 * QBLK), :] += dk
            dvbuf[pl.ds(r0, 2 * QBLK), :] += dv
            return carry

        lax.fori_loop(0, nb, blk, 0)

        @pl.when(j < nt - 1)
        def _():
            mask = band & (is_prev == 1)
            _, dk, dv = _attn_block_bwd(
                qn_ref[...].astype(bf16), don_ref[...].astype(bf16),
                kbuf[pl.ds(tr, 2 * QBLK), :].astype(bf16), vbuf[pl.ds(tr, 2 * QBLK), :].astype(bf16),
                ltn_ref[...], dsn_ref[...], mask, lo)
            dkbuf[pl.ds(tr, 2 * QBLK), :] += dk
            dvbuf[pl.ds(tr, 2 * QBLK), :] += dv

        dk_ref[...] = dkbuf[pl.ds(QBLK, tr), :]
        dv_ref[...] = dvbuf[pl.ds(QBLK, tr), :]

    def prev_halo(col):
        return pl.BlockSpec((QBLK, lw), lambda j, l: (jnp.maximum(j * nb - 1, 0), col + l))

    def next_halo(col):
        return pl.BlockSpec((QBLK, lw), lambda j, l: (jnp.minimum((j + 1) * nb, S // QBLK - 1), col + l))

    def tile(col):
        return pl.BlockSpec((tr, lw), lambda j, l: (j, col + l))

    return pl.pallas_call(
        body, name="attn_bwd0", grid=(nt, A_WIDTH // lw),
        in_specs=[tile(E_Q // lw), next_halo(E_Q // lw), tile(0), next_halo(0), tile(0), next_halo(0), tile(0), next_halo(0),
                  tile(E_K // lw), prev_halo(E_K // lw), tile(E_V // lw), prev_halo(E_V // lw)],
        out_specs=[tile(0)] * 3,
        out_shape=[jax.ShapeDtypeStruct((S, A_WIDTH), f32)] * 3,
        scratch_shapes=[pltpu.VMEM((tr + 2 * QBLK, lw), f32)] * 4,
        compiler_params=_cp(("parallel", "parallel")),
    )(qk, qk, do_a, do_a, lt, lt, dsum, dsum, qk, qk, proj, proj)


def _stream_view(a, d):
    S, W = a.shape
    return a.reshape(S // 8, 8, W) if d == 4 else a.reshape(S // 16, 2, 8, W)


def _stream_ref(ref, d, r, part, col, lw):
    n = ref.shape[0]
    if d == 4:
        return ref.at[pl.ds(0, n), r + 4 * part, pl.ds(col, lw)]
    return ref.at[pl.ds(0, n), r // 8, r % 8, pl.ds(col, lw)]


def _stream_geometry(S, d):
    nparts = 2 if d == 4 else 1
    rows = S // (d * nparts)
    return nparts, rows, QBLK // nparts


def _attn_fwd_dil(qk, proj, g, *, name):
    S = qk.shape[0]
    d = DILATIONS[g]
    nparts, rows, qr = _stream_geometry(S, d)
    nb = rows // qr
    lw = 2 * LANES if d == 4 else 4 * LANES
    nlg = A_WIDTH // lw
    nitems = d * nlg
    ins = ((0, E_Q + A_WIDTH * g, 0), (0, E_K + A_WIDTH * g, qr), (1, E_V + A_WIDTH * g, qr))

    def body(qk_hbm, pj_hbm, o_hbm, l_hbm, qbuf, kbuf, vbuf, obuf, lbuf, in_sems, out_sems):
        i = pl.program_id(0)
        slot = i % 2
        hbm_in = (qk_hbm, pj_hbm)
        bufs_in = (qbuf, kbuf, vbuf)

        def in_copies(item, sl):
            r, lg = item // nlg, item % nlg
            cps = []
            for a in range(nparts):
                for t, (src, col, pad) in enumerate(ins):
                    cps.append(pltpu.make_async_copy(
                        _stream_ref(hbm_in[src], d, r, a, pl.multiple_of(col + lw * lg, LANES), lw),
                        bufs_in[t].at[sl, a, pl.ds(pad, rows), :], in_sems.at[sl, 3 * a + t]))
            return cps

        def out_copies(item, sl):
            r, lg = item // nlg, item % nlg
            cps = []
            for a in range(nparts):
                for t, (buf, dst) in enumerate(((obuf, o_hbm), (lbuf, l_hbm))):
                    cps.append(pltpu.make_async_copy(
                        buf.at[sl, a], _stream_ref(dst, d, r, a, pl.multiple_of(lw * lg, LANES), lw),
                        out_sems.at[sl, 2 * a + t]))
            return cps

        @pl.when(i == 0)
        def _():
            for sl in range(2):
                for a in range(nparts):
                    kbuf[sl, a, 0:qr, :] = jnp.zeros((qr, lw), f32)
                    vbuf[sl, a, 0:qr, :] = jnp.zeros((qr, lw), f32)
            for cp in in_copies(0, 0):
                cp.start()

        @pl.when(i + 1 < nitems)
        def _():
            for cp in in_copies(i + 1, 1 - slot):
                cp.start()

        for cp in in_copies(i, slot):
            cp.wait()

        @pl.when(i >= 2)
        def _():
            for cp in out_copies(i - 2, slot):
                cp.wait()

        band, is_prev = _key_geometry(nparts)
        lo = lax.broadcasted_iota(jnp.int32, (QBLK, LANES), 1) < HEAD_DIM

        def blk(c, carry):
            r0 = pl.multiple_of(c * qr, qr)
            mask = band & (is_prev * jnp.where(c == 0, 1, 0) == 0)
            for pp in range(lw // LANES):
                lanes = slice(pp * LANES, (pp + 1) * LANES)
                qb = jnp.concatenate([qbuf[slot, a, pl.ds(r0, qr), lanes] for a in range(nparts)], axis=0).astype(bf16)
                kcat = jnp.concatenate([kbuf[slot, a, pl.ds(r0, 2 * qr), lanes] for a in range(nparts)], axis=0).astype(bf16)
                vcat = jnp.concatenate([vbuf[slot, a, pl.ds(r0, 2 * qr), lanes] for a in range(nparts)], axis=0).astype(bf16)
                o, lse = _attn_block_fwd(qb, kcat, vcat, mask, lo)
                for a in range(nparts):
                    obuf[slot, a, pl.ds(r0, qr), lanes] = o[a * qr:(a + 1) * qr]
                    lbuf[slot, a, pl.ds(r0, qr), lanes] = lse[a * qr:(a + 1) * qr]
            return carry

        lax.fori_loop(0, nb, blk, 0)

        for cp in out_copies(i, slot):
            cp.start()

        @pl.when(i == nitems - 1)
        def _():
            for cp in out_copies(i - 1, 1 - slot) + out_copies(i, slot):
                cp.wait()

    vshape = (S // 8, 8, A_WIDTH) if d == 4 else (S // 16, 2, 8, A_WIDTH)
    o, lse = pl.pallas_call(
        body, name=name, grid=(nitems,),
        in_specs=[_HBM_ANY, _HBM_ANY], out_specs=[_HBM_ANY, _HBM_ANY],
        out_shape=[jax.ShapeDtypeStruct(vshape, f32)] * 2,
        scratch_shapes=[pltpu.VMEM((2, nparts, rows, lw), f32), pltpu.VMEM((2, nparts, qr + rows, lw), f32),
                        pltpu.VMEM((2, nparts, qr + rows, lw), f32), pltpu.VMEM((2, nparts, rows, lw), f32),
                        pltpu.VMEM((2, nparts, rows, lw), f32),
                        pltpu.SemaphoreType.DMA((2, 3 * nparts)), pltpu.SemaphoreType.DMA((2, 2 * nparts))],
        compiler_params=_cp(("arbitrary",)),
    )(_stream_view(qk, d), _stream_view(proj, d))
    return o.reshape(S, A_WIDTH), lse.reshape(S, A_WIDTH)


def _attn_bwd_dil(qk, proj, do_a, lt, dsum, g, *, name):
    S = qk.shape[0]
    d = DILATIONS[g]
    nparts, rows, qr = _stream_geometry(S, d)
    nb = rows // qr
    lw = LANES if d == 4 else 2 * LANES
    nlg = A_WIDTH // lw
    nitems = d * nlg
    ins = ((0, E_Q + A_WIDTH * g, 0), (2, 0, 0), (3, 0, 0), (4, 0, 0), (0, E_K + A_WIDTH * g, qr), (1, E_V + A_WIDTH * g, qr))
    n_in = len(ins)

    def body(qk_hbm, pj_hbm, do_hbm, lt_hbm, ds_hbm, dq_hbm, dk_hbm, dv_hbm,
             qbuf, dobuf, ltbuf, dsbuf, kbuf, vbuf, dqbuf, dkbuf, dvbuf, in_sems, out_sems):
        i = pl.program_id(0)
        slot = i % 2
        hbm_in = (qk_hbm, pj_hbm, do_hbm, lt_hbm, ds_hbm)
        bufs_in = (qbuf, dobuf, ltbuf, dsbuf, kbuf, vbuf)

        def in_copies(item, sl):
            r, lg = item // nlg, item % nlg
            cps = []
            for a in range(nparts):
                for t, (src, col, pad) in enumerate(ins):
                    cps.append(pltpu.make_async_copy(
                        _stream_ref(hbm_in[src], d, r, a, pl.multiple_of(col + lw * lg, LANES), lw),
                        bufs_in[t].at[sl, a, pl.ds(pad, rows), :], in_sems.at[sl, n_in * a + t]))
            return cps

        def out_copies(item, sl):
            r, lg = item // nlg, item % nlg
            cps = []
            for a in range(nparts):
                for t, (buf, dst, pad) in enumerate(((dqbuf, dq_hbm, 0), (dkbuf, dk_hbm, qr), (dvbuf, dv_hbm, qr))):
                    cps.append(pltpu.make_async_copy(
                        buf.at[sl, a, pl.ds(pad, rows), :],
                        _stream_ref(dst, d, r, a, pl.multiple_of(lw * lg, LANES), lw), out_sems.at[sl, 3 * a + t]))
            return cps

        @pl.when(i == 0)
        def _():
            for sl in range(2):
                for a in range(nparts):
                    kbuf[sl, a, 0:qr, :] = jnp.zeros((qr, lw), f32)
                    vbuf[sl, a, 0:qr, :] = jnp.zeros((qr, lw), f32)
            for cp in in_copies(0, 0):
                cp.start()

        @pl.when(i + 1 < nitems)
        def _():
            for cp in in_copies(i + 1, 1 - slot):
                cp.start()

        for cp in in_copies(i, slot):
            cp.wait()

        @pl.when(i >= 2)
        def _():
            for cp in out_copies(i - 2, slot):
                cp.wait()

        for a in range(nparts):
            dkbuf[slot, a] = jnp.zeros((qr + rows, lw), f32)
            dvbuf[slot, a] = jnp.zeros((qr + rows, lw), f32)
        band, is_prev = _key_geometry(nparts)
        lo = lax.broadcasted_iota(jnp.int32, (QBLK, LANES), 1) < HEAD_DIM

        def blk(c, carry):
            r0 = pl.multiple_of(c * qr, qr)
            mask = band & (is_prev * jnp.where(c == 0, 1, 0) == 0)

            def rows_of(buf, n, lanes):
                return jnp.concatenate([buf[slot, a, pl.ds(r0, n), lanes] for a in range(nparts)], axis=0)

            for pp in range(lw // LANES):
                lanes = slice(pp * LANES, (pp + 1) * LANES)
                dq, dk, dv = _attn_block_bwd(
                    rows_of(qbuf, qr, lanes).astype(bf16), rows_of(dobuf, qr, lanes).astype(bf16),
                    rows_of(kbuf, 2 * qr, lanes).astype(bf16), rows_of(vbuf, 2 * qr, lanes).astype(bf16),
                    rows_of(ltbuf, qr, lanes), rows_of(dsbuf, qr, lanes), mask, lo)
                for a in range(nparts):
                    dqbuf[slot, a, pl.ds(r0, qr), lanes] = dq[a * qr:(a + 1) * qr]
                    dkbuf[slot, a, pl.ds(r0, 2 * qr), lanes] += dk[2 * a * qr:2 * (a + 1) * qr]
                    dvbuf[slot, a, pl.ds(r0, 2 * qr), lanes] += dv[2 * a * qr:2 * (a + 1) * qr]
            return carry

        lax.fori_loop(0, nb, blk, 0)

        for cp in out_copies(i, slot):
            cp.start()

        @pl.when(i == nitems - 1)
        def _():
            for cp in out_copies(i - 1, 1 - slot) + out_copies(i, slot):
                cp.wait()

    vshape = (S // 8, 8, A_WIDTH) if d == 4 else (S // 16, 2, 8, A_WIDTH)
    plain = pltpu.VMEM((2, nparts, rows, lw), f32)
    padded = pltpu.VMEM((2, nparts, qr + rows, lw), f32)
    outs = pl.pallas_call(
        body, name=name, grid=(nitems,),
        in_specs=[_HBM_ANY] * 5, out_specs=[_HBM_ANY] * 3,
        out_shape=[jax.ShapeDtypeStruct(vshape, f32)] * 3,
        scratch_shapes=[plain, plain, plain, plain, padded, padded, plain, padded, padded,
                        pltpu.SemaphoreType.DMA((2, n_in * nparts)), pltpu.SemaphoreType.DMA((2, 3 * nparts))],
        compiler_params=_cp(("arbitrary",)),
    )(*[_stream_view(a, d) for a in (qk, proj, do_a, lt, dsum)])
    return [o.reshape(S, A_WIDTH) for o in outs]


def _prev_halo(tm, h, col):
    return pl.BlockSpec((h, 512), lambda i: (jnp.maximum(i * (tm // h) - 1, 0), col))


def _next_halo(tm, h, col, S):
    return pl.BlockSpec((h, 512), lambda i: (jnp.minimum((i + 1) * (tm // h), S // h - 1), col))


def _mix0_fwd(o_g, lse_g, proj, conv_w):
    S = proj.shape[0]
    tm = _tile(S, 256)

    def body(o0, o1, o2, l0, l1, l2, bg_ref, cg_ref, hb_ref, z_ref, cgh_ref, hbh_ref, w_ref,
             u_ref, oa_ref, lt_ref, tbuf):
        i = pl.program_id(0)
        ls = [l0[...], l1[...], l2[...]]
        mx = jnp.maximum(jnp.maximum(ls[0], ls[1]), ls[2])
        es = [jnp.exp(l - mx) for l in ls]
        tot = es[0] + es[1] + es[2]
        lt_ref[...] = mx + jnp.log(tot)
        inv = 1.0 / tot
        z = z_ref[...]
        sz = z * _sigmoid(z)
        oa = (es[0] * inv) * o0[...] + (es[1] * inv) * o1[...] + (es[2] * inv) * o2[...]
        oa_ref[...] = oa
        u_ref[:, :A_WIDTH] = (oa * sz[:, :A_WIDTH]).astype(bf16)
        t = cg_ref[...] * hb_ref[...]
        tbuf[0:8, :] = jnp.where(i > 0, cgh_ref[...] * hbh_ref[...], 0.0)
        tbuf[8:, :] = t
        cv = w_ref[2:3, :] * t + w_ref[1:2, :] * tbuf[pl.ds(7, tm), :] + w_ref[0:1, :] * tbuf[pl.ds(6, tm), :]
        u_ref[:, A_WIDTH:] = (bg_ref[...] * cv * sz[:, A_WIDTH:]).astype(bf16)

    row = lambda w, c: pl.BlockSpec((tm, w), lambda i: (i, c))
    return pl.pallas_call(
        body, name="mix0_fwd", grid=(S // tm,),
        in_specs=[row(512, 0)] * 6
        + [row(512, E_BG // 512), row(512, E_CG // 512), row(512, E_HB // 512), row(1024, E_Z // 1024),
           _prev_halo(tm, 8, E_CG // 512), _prev_halo(tm, 8, E_HB // 512), pl.BlockSpec((SC_WIDTH, 512), lambda i: (0, 0))],
        out_specs=[row(1024, 0), row(512, 0), row(512, 0)],
        out_shape=[jax.ShapeDtypeStruct((S, D_MODEL), bf16), jax.ShapeDtypeStruct((S, A_WIDTH), f32),
                   jax.ShapeDtypeStruct((S, A_WIDTH), f32)],
        scratch_shapes=[pltpu.VMEM((tm + 8, 512), f32)],
        compiler_params=_cp(("parallel",)),
    )(*o_g, *lse_g, proj, proj, proj, proj, proj, proj, conv_w)


def _dsilu(z, sg):
    return sg * (1.0 + z * (1.0 - sg))


def _mix0_bwd_a(du, proj, o_a, conv_w):
    S = proj.shape[0]
    tm = _tile(S, 256)

    def body(du_ref, bg_ref, cg_ref, hb_ref, z_ref, cgh_ref, hbh_ref, oa_ref, w_ref,
             dz_ref, doa_ref, ds_ref, dbg_ref, dcv_ref, tbuf):
        i = pl.program_id(0)
        lo = lax.broadcasted_iota(jnp.int32, (tm, LANES), 1) < HEAD_DIM
        z = z_ref[...]
        sg = _sigmoid(z)
        sz = z * sg
        dsz = _dsilu(z, sg)
        du_v = du_ref[...]
        t = cg_ref[...] * hb_ref[...]
        tbuf[0:8, :] = jnp.where(i > 0, cgh_ref[...] * hbh_ref[...], 0.0)
        tbuf[8:, :] = t
        cv = w_ref[2:3, :] * t + w_ref[1:2, :] * tbuf[pl.ds(7, tm), :] + w_ref[0:1, :] * tbuf[pl.ds(6, tm), :]
        bg = bg_ref[...]
        oa = oa_ref[...]
        dz_ref[:, :A_WIDTH] = (du_v[:, :A_WIDTH] * oa * dsz[:, :A_WIDTH]).astype(bf16)
        dz_ref[:, A_WIDTH:] = (du_v[:, A_WIDTH:] * (bg * cv) * dsz[:, A_WIDTH:]).astype(bf16)
        doa = du_v[:, :A_WIDTH] * sz[:, :A_WIDTH]
        dyb = du_v[:, A_WIDTH:] * sz[:, A_WIDTH:]
        doa_ref[...] = doa
        dbg_ref[...] = (dyb * cv).astype(bf16)
        dcv_ref[...] = dyb * bg
        prod = doa * oa
        for p in range(4):
            pp = prod[:, p * LANES:(p + 1) * LANES]
            sa = jnp.sum(jnp.where(lo, pp, 0.0), axis=-1, keepdims=True)
            sb = jnp.sum(jnp.where(lo, 0.0, pp), axis=-1, keepdims=True)
            ds_ref[:, p * LANES:(p + 1) * LANES] = jnp.where(lo, sa, sb)

    row = lambda w, c: pl.BlockSpec((tm, w), lambda i: (i, c))
    return pl.pallas_call(
        body, name="mix0_bwd_a", grid=(S // tm,),
        in_specs=[row(1024, 0), row(512, E_BG // 512), row(512, E_CG // 512), row(512, E_HB // 512), row(1024, E_Z // 1024),
                  _prev_halo(tm, 8, E_CG // 512), _prev_halo(tm, 8, E_HB // 512), row(512, 0),
                  pl.BlockSpec((SC_WIDTH, 512), lambda i: (0, 0))],
        out_specs=[row(1024, 0), row(512, 0), row(512, 0), row(512, 0), row(512, 0)],
        out_shape=[jax.ShapeDtypeStruct((S, D_MODEL), bf16), jax.ShapeDtypeStruct((S, A_WIDTH), f32),
                   jax.ShapeDtypeStruct((S, A_WIDTH), f32), jax.ShapeDtypeStruct((S, 512), bf16),
                   jax.ShapeDtypeStruct((S, 512), f32)],
        scratch_shapes=[pltpu.VMEM((tm + 8, 512), f32)],
        compiler_params=_cp(("parallel",)),
    )(du, proj, proj, proj, proj, proj, proj, o_a, conv_w)


def _mix0_bwd_b(dcv, proj, conv_w):
    S = proj.shape[0]
    tm = _tile(S, 256)
    nt = S // tm

    def body(dcv_ref, dcvn_ref, cg_ref, hb_ref, cgh_ref, hbh_ref, w_ref, dcg_ref, dhb_ref, gw_ref, tbuf, dbuf):
        i = pl.program_id(0)
        cg = cg_ref[...]
        hb = hb_ref[...]
        t = cg * hb
        tbuf[0:8, :] = jnp.where(i > 0, cgh_ref[...] * hbh_ref[...], 0.0)
        tbuf[8:, :] = t
        dcv_v = dcv_ref[...]
        dbuf[0:tm, :] = dcv_v
        dbuf[tm:, :] = jnp.where(i < nt - 1, dcvn_ref[...], 0.0)
        dt = w_ref[2:3, :] * dcv_v + w_ref[1:2, :] * dbuf[pl.ds(1, tm), :] + w_ref[0:1, :] * dbuf[pl.ds(2, tm), :]
        dcg_ref[...] = (dt * hb).astype(bf16)
        dhb_ref[...] = (dt * cg).astype(bf16)
        g2 = jnp.sum(dcv_v * t, axis=0, keepdims=True)
        g1 = jnp.sum(dcv_v * tbuf[pl.ds(7, tm), :], axis=0, keepdims=True)
        g0 = jnp.sum(dcv_v * tbuf[pl.ds(6, tm), :], axis=0, keepdims=True)
        part = jnp.concatenate([g0, g1, g2, jnp.zeros((5, 512), f32)], axis=0)

        @pl.when(i == 0)
        def _():
            gw_ref[...] = part

        @pl.when(i > 0)
        def _():
            gw_ref[...] += part

    row = lambda w, c: pl.BlockSpec((tm, w), lambda i: (i, c))
    return pl.pallas_call(
        body, name="mix0_bwd_b", grid=(nt,),
        in_specs=[row(512, 0), _next_halo(tm, 8, 0, S), row(512, E_CG // 512), row(512, E_HB // 512),
                  _prev_halo(tm, 8, E_CG // 512), _prev_halo(tm, 8, E_HB // 512),
                  pl.BlockSpec((SC_WIDTH, 512), lambda i: (0, 0))],
        out_specs=[row(512, 0), row(512, 0), pl.BlockSpec((8, 512), lambda i: (0, 0))],
        out_shape=[jax.ShapeDtypeStruct((S, 512), bf16), jax.ShapeDtypeStruct((S, 512), bf16),
                   jax.ShapeDtypeStruct((8, 512), f32)],
        scratch_shapes=[pltpu.VMEM((tm + 8, 512), f32), pltpu.VMEM((tm + 8, 512), f32)],
        compiler_params=_cp(("arbitrary",)),
    )(dcv, dcv, proj, proj, proj, proj, conv_w)


def _qk_bwd(dq_g, dk_g, dv_g, proj, tabs, nw, hm, dbg, dcg, dhb, dz):
    S = proj.shape[0]
    tm = _tile(S, 256)

    def body(*refs):
        d_refs = refs[0:6]
        dv_refs = refs[6:9]
        x_ref, c_ref, s1_ref, s2_ref, nw_ref, m_ref, dbg_ref, dcg_ref, dhb_ref, dz_ref, o_ref, gw_ref = refs[9:]
        i = pl.program_id(0)
        c, s1, s2, m = c_ref[...], s1_ref[...], s2_ref[...], m_ref[...]
        accs = []
        for kind in range(2):
            w = nw_ref[kind:kind + 1, :]
            acc = jnp.zeros((1, LANES), f32)
            for gi in range(N_GROUPS):
                for p in range(4):
                    col = kind * 1536 + gi * 512 + p * LANES
                    dout = d_refs[kind * 3 + gi][:, p * LANES:(p + 1) * LANES]
                    t = x_ref[:, col:col + LANES]
                    dthat = (dout * c + pltpu.roll(dout * s1, LANES - ROT_HALF, axis=1)
                             + pltpu.roll(dout * s2, ROT_HALF, axis=1))
                    r = lax.rsqrt(_head_mean(t * t, m) + EPS)
                    tn = t * r
                    acc = acc + jnp.sum(dthat * tn, axis=0, keepdims=True)
                    dtn = dthat * w
                    o_ref[:, col:col + LANES] = (r * (dtn - tn * _head_mean(dtn * tn, m))).astype(bf16)
            accs.append(acc + pltpu.roll(acc, HEAD_DIM, axis=1))
        for gi in range(N_GROUPS):
            o_ref[:, E_V + gi * 512:E_V + (gi + 1) * 512] = dv_refs[gi][...].astype(bf16)
        o_ref[:, E_BG:E_CG] = dbg_ref[...]
        o_ref[:, E_CG:E_HB] = dcg_ref[...]
        o_ref[:, E_HB:E_Z] = dhb_ref[...]
        o_ref[:, E_Z:] = dz_ref[...]
        part = jnp.concatenate([accs[0], accs[1], jnp.zeros((6, LANES), f32)], axis=0)

        @pl.when(i == 0)
        def _():
            gw_ref[...] = part

        @pl.when(i > 0)
        def _():
            gw_ref[...] += part

    row = lambda w, c: pl.BlockSpec((tm, w), lambda i: (i, c))
    tab = row(LANES, 0)
    return pl.pallas_call(
        body, name="qk_bwd", grid=(S // tm,),
        in_specs=[row(512, 0)] * 9 + [row(3072, 0), tab, tab, tab, pl.BlockSpec((2, LANES), lambda i: (0, 0)),
                                      pl.BlockSpec((LANES, LANES), lambda i: (0, 0)),
                                      row(512, 0), row(512, 0), row(512, 0), row(1024, 0)],
        out_specs=[row(EVEN_IN, 0), pl.BlockSpec((8, LANES), lambda i: (0, 0))],
        out_shape=[jax.ShapeDtypeStruct((S, EVEN_IN), bf16), jax.ShapeDtypeStruct((8, LANES), f32)],
        compiler_params=_cp(("arbitrary",)),
    )(*dq_g, *dk_g, *dv_g, proj, *tabs, nw, hm, dbg, dcg, dhb, dz)


def _inv_count(i, tm, p):
    rowg = lax.broadcasted_iota(jnp.int32, (tm, 1), 0) + i * tm
    return 1.0 / jnp.minimum(rowg + 1, p).astype(f32)


def _layer_norm_stats(c):
    mu = jnp.mean(c, axis=-1, keepdims=True)
    cen = c - mu
    rstd = lax.rsqrt(jnp.mean(cen * cen, axis=-1, keepdims=True) + EPS)
    return cen * rstd, rstd


def _fill_pool_buf(i, ubuf, uc_ref, uch_ref):
    ubuf[0:16, :] = jnp.where(i > 0, uch_ref[...], 0.0)
    ubuf[16:, :] = uc_ref[...]


def _pooled(i, tm, ubuf, gi):
    p = POOL_SIZES[gi]
    cols = slice(gi * LANES, (gi + 1) * LANES)
    acc = ubuf[pl.ds(16, tm), cols]
    cur = acc
    for jj in range(1, p):
        acc = acc + ubuf[pl.ds(16 - jj, tm), cols]
    return acc * _inv_count(i, tm, p) - cur


def _fill_glu_buf(i, gbuf, da_ref, dg_ref, dah_ref, dgh_ref):
    gbuf[0:32, :] = jnp.where(i > 0, dah_ref[...] * _sigmoid(dgh_ref[...]), 0.0)
    gbuf[32:, :] = da_ref[...] * _sigmoid(dg_ref[...])


def _mix1_fwd(proj, pool_w, pool_scale, dconv_w, dconv_b, ln_w, ln_b):
    S = proj.shape[0]
    tm = _tile(S, 256)

    def body(uc_ref, uch_ref, da_ref, dg_ref, dah_ref, dgh_ref, za_ref, zb_ref, pw_ref, ps_ref, cw_ref, cb_ref,
             lw_ref, lb_ref, u_ref, c_ref, mc_ref, ubuf, gbuf):
        i = pl.program_id(0)
        _fill_pool_buf(i, ubuf, uc_ref, uch_ref)
        za = za_ref[...]
        for gi in range(4):
            cols = slice(gi * LANES, (gi + 1) * LANES)
            mc = jnp.dot(_pooled(i, tm, ubuf, gi).astype(bf16), pw_ref[gi], preferred_element_type=f32)
            mc_ref[:, cols] = mc
            zg = za[:, cols]
            u_ref[:, cols] = (mc * ps_ref[:, cols] * (zg * _sigmoid(zg))).astype(bf16)
        _fill_glu_buf(i, gbuf, da_ref, dg_ref, dah_ref, dgh_ref)
        c = jnp.zeros((tm, 512), f32) + cb_ref[...]
        for k in range(D_CONV):
            c = c + cw_ref[k:k + 1, :] * gbuf[pl.ds(32 - (D_CONV - 1) + k, tm), :]
        c_ref[...] = c
        yhat, _ = _layer_norm_stats(c)
        l = yhat * lw_ref[...] + lb_ref[...]
        zb = zb_ref[...]
        u_ref[:, 512:] = (l * _sigmoid(l) * (zb * _sigmoid(zb))).astype(bf16)

    row = lambda w, c: pl.BlockSpec((tm, w), lambda i: (i, c))
    vec = pl.BlockSpec((1, 512), lambda i: (0, 0))
    return pl.pallas_call(
        body, name="mix1_fwd", grid=(S // tm,),
        in_specs=[row(512, 0), _prev_halo(tm, 16, 0), row(512, 1), row(512, 2), _prev_halo(tm, 32, 1), _prev_halo(tm, 32, 2),
                  row(512, 3), row(512, 4), pl.BlockSpec((4, LANES, LANES), lambda i: (0, 0, 0)), vec,
                  pl.BlockSpec((D_CONV, 512), lambda i: (0, 0)), vec, vec, vec],
        out_specs=[row(1024, 0), row(512, 0), row(512, 0)],
        out_shape=[jax.ShapeDtypeStruct((S, D_MODEL), bf16), jax.ShapeDtypeStruct((S, 512), f32),
                   jax.ShapeDtypeStruct((S, 512), f32)],
        scratch_shapes=[pltpu.VMEM((tm + 16, 512), f32), pltpu.VMEM((tm + 32, 512), f32)],
        compiler_params=_cp(("parallel",)),
    )(proj, proj, proj, proj, proj, proj, proj, proj, pool_w, pool_scale, dconv_w, dconv_b, ln_w, ln_b)


def _mix1_bwd_a(du, proj, c, mc, pool_w, pool_scale, ln_w, ln_b):
    S = proj.shape[0]
    tm = _tile(S, 256)

    def body(du_ref, za_ref, zb_ref, c_ref, mc_ref, pw_ref, ps_ref, lw_ref, lb_ref,
             dz_ref, dc_ref, dpl_ref, dmc_ref, acc_ref):
        i = pl.program_id(0)
        du_v = du_ref[...]
        ps = ps_ref[...]
        za = za_ref[...]
        sga = _sigmoid(za)
        mcv = mc_ref[...]
        dz_ref[:, :512] = (du_v[:, :512] * (mcv * ps) * _dsilu(za, sga)).astype(bf16)
        dyc = du_v[:, :512] * (za * sga)
        g_ps = jnp.sum(dyc * mcv, axis=0, keepdims=True)
        dmc = (dyc * ps).astype(bf16)
        dmc_ref[...] = dmc
        for gi in range(4):
            cols = slice(gi * LANES, (gi + 1) * LANES)
            dpl_ref[:, cols] = lax.dot_general(dmc[:, cols], pw_ref[gi], (((1,), (1,)), ((), ())), preferred_element_type=f32)
        yhat, rstd = _layer_norm_stats(c_ref[...])
        lw = lw_ref[...]
        l = yhat * lw + lb_ref[...]
        sgl = _sigmoid(l)
        zb = zb_ref[...]
        sgb = _sigmoid(zb)
        dz_ref[:, 512:] = (du_v[:, 512:] * (l * sgl) * _dsilu(zb, sgb)).astype(bf16)
        dl = du_v[:, 512:] * (zb * sgb) * _dsilu(l, sgl)
        g_lb = jnp.sum(dl, axis=0, keepdims=True)
        g_lw = jnp.sum(dl * yhat, axis=0, keepdims=True)
        dyh = dl * lw
        dc = rstd * (dyh - jnp.mean(dyh, axis=-1, keepdims=True) - yhat * jnp.mean(dyh * yhat, axis=-1, keepdims=True))
        dc_ref[...] = dc
        g_db = jnp.sum(dc, axis=0, keepdims=True)
        part = jnp.concatenate([g_ps, g_lw, g_lb, g_db, jnp.zeros((4, 512), f32)], axis=0)

        @pl.when(i == 0)
        def _():
            acc_ref[...] = part

        @pl.when(i > 0)
        def _():
            acc_ref[...] += part

    row = lambda w, c_: pl.BlockSpec((tm, w), lambda i: (i, c_))
    vec = pl.BlockSpec((1, 512), lambda i: (0, 0))
    return pl.pallas_call(
        body, name="mix1_bwd_a", grid=(S // tm,),
        in_specs=[row(1024, 0), row(512, 3), row(512, 4), row(512, 0), row(512, 0),
                  pl.BlockSpec((4, LANES, LANES), lambda i: (0, 0, 0)), vec, vec, vec],
        out_specs=[row(1024, 0), row(512, 0), row(512, 0), row(512, 0), pl.BlockSpec((8, 512), lambda i: (0, 0))],
        out_shape=[jax.ShapeDtypeStruct((S, D_MODEL), bf16), jax.ShapeDtypeStruct((S, 512), f32),
                   jax.ShapeDtypeStruct((S, 512), f32), jax.ShapeDtypeStruct((S, 512), bf16),
                   jax.ShapeDtypeStruct((8, 512), f32)],
        compiler_params=_cp(("arbitrary",)),
    )(du, proj, proj, c, mc, pool_w, pool_scale, ln_w, ln_b)


def _mix1_bwd_b(dc, dpl, dmc, dz, proj, dconv_w):
    S = proj.shape[0]
    tm = _tile(S, 256)
    nt = S // tm

    def body(dc_ref, dcn_ref, dpl_ref, dpn_ref, dmc_ref, dz_ref, uc_ref, uch_ref, da_ref, dg_ref, dah_ref, dgh_ref,
             cw_ref, o_ref, gcw_ref, gpw_ref, ubuf, gbuf, dcbuf, dpbuf):
        i = pl.program_id(0)
        last = i == nt - 1
        _fill_pool_buf(i, ubuf, uc_ref, uch_ref)
        _fill_glu_buf(i, gbuf, da_ref, dg_ref, dah_ref, dgh_ref)
        dc_v = dc_ref[...]
        dcbuf[0:tm, :] = dc_v
        dcbuf[tm:, :] = jnp.where(last, 0.0, dcn_ref[...])
        dpl_v = dpl_ref[...]
        for gi in range(4):
            p = POOL_SIZES[gi]
            cols = slice(gi * LANES, (gi + 1) * LANES)
            dpbuf[0:tm, cols] = dpl_v[:, cols] * _inv_count(i, tm, p)
            dpbuf[tm:, cols] = jnp.where(last, 0.0, dpn_ref[:, cols] * (1.0 / p))
        gpw = []
        for gi in range(4):
            p = POOL_SIZES[gi]
            cols = slice(gi * LANES, (gi + 1) * LANES)
            acc = -dpl_v[:, cols]
            for jj in range(p):
                acc = acc + dpbuf[pl.ds(jj, tm), cols]
            o_ref[:, cols] = acc.astype(bf16)
            pooled = _pooled(i, tm, ubuf, gi).astype(bf16)
            gpw.append(lax.dot_general(pooled, dmc_ref[:, cols], (((0,), (0,)), ((), ())), preferred_element_type=f32))
        dgl = jnp.zeros((tm, 512), f32)
        gcw = []
        for k in range(D_CONV):
            lag = D_CONV - 1 - k
            dgl = dgl + cw_ref[k:k + 1, :] * dcbuf[pl.ds(lag, tm), :]
            gcw.append(jnp.sum(dc_v * gbuf[pl.ds(32 - lag, tm), :], axis=0, keepdims=True))
        gcw.append(jnp.zeros((1, 512), f32))
        da = da_ref[...]
        sg = _sigmoid(dg_ref[...])
        o_ref[:, O_DA:O_DG] = (dgl * sg).astype(bf16)
        o_ref[:, O_DG:O_Z] = (dgl * da * sg * (1.0 - sg)).astype(bf16)
        o_ref[:, O_Z:] = dz_ref[...]
        gcw_part = jnp.concatenate(gcw, axis=0)

        @pl.when(i == 0)
        def _():
            gcw_ref[...] = gcw_part
            for gi in range(4):
                gpw_ref[gi] = gpw[gi]

        @pl.when(i > 0)
        def _():
            gcw_ref[...] += gcw_part
            for gi in range(4):
                gpw_ref[gi] += gpw[gi]

    row = lambda w, c_: pl.BlockSpec((tm, w), lambda i: (i, c_))
    return pl.pallas_call(
        body, name="mix1_bwd_b", grid=(nt,),
        in_specs=[row(512, 0), _next_halo(tm, 32, 0, S), row(512, 0), _next_halo(tm, 16, 0, S), row(512, 0), row(1024, 0),
                  row(512, 0), _prev_halo(tm, 16, 0), row(512, 1), row(512, 2), _prev_halo(tm, 32, 1), _prev_halo(tm, 32, 2),
                  pl.BlockSpec((D_CONV, 512), lambda i: (0, 0))],
        out_specs=[row(ODD_IN, 0), pl.BlockSpec((32, 512), lambda i: (0, 0)),
                   pl.BlockSpec((4, LANES, LANES), lambda i: (0, 0, 0))],
        out_shape=[jax.ShapeDtypeStruct((S, ODD_IN), bf16), jax.ShapeDtypeStruct((32, 512), f32),
                   jax.ShapeDtypeStruct((4, LANES, LANES), f32)],
        scratch_shapes=[pltpu.VMEM((tm + 16, 512), f32), pltpu.VMEM((tm + 32, 512), f32),
                        pltpu.VMEM((tm + 32, 512), f32), pltpu.VMEM((tm + 16, 512), f32)],
        compiler_params=_cp(("arbitrary",)),
    )(dc, dc, dpl, dpl, dmc, dz, proj, proj, proj, proj, proj, proj, dconv_w)


def _local_step(x, pos_col, target, e_norm_w, e_w_in, e_q_norm_w, e_k_norm_w, e_conv_w, e_w_out,
                o_norm_w, o_w_in, o_pool_w, o_pool_scale, o_dconv_w, o_dconv_b, o_ln_w, o_ln_b, o_w_out):
    hm = _head_mean_matrix()
    nw = jnp.concatenate([jnp.tile(e_q_norm_w, (1, 2)), jnp.tile(e_k_norm_w, (1, 2))], axis=0)
    tabs = _rope_tables(pos_col)
    pool_wb = o_pool_w.astype(bf16)

    h0 = _rms_fwd(x, e_norm_w, name="rms0_fwd")
    proj0 = _mm_nn(h0, e_w_in, name="in_proj0")
    qk = _qk_prep(proj0, tabs, nw, hm)
    o_g, lse_g = [], []
    for g in range(N_GROUPS):
        o, l = _attn_fwd_local(qk, proj0) if g == 0 else _attn_fwd_dil(qk, proj0, g, name=f"attn_fwd{g}")
        o_g.append(o)
        lse_g.append(l)
    u0, o_a, lt = _mix0_fwd(o_g, lse_g, proj0, e_conv_w)
    x1 = _mm_nn(u0, e_w_out, res=x, name="out_proj0")
    h1 = _rms_fwd(x1, o_norm_w, name="rms1_fwd")
    proj1 = _mm_nn(h1, o_w_in, name="in_proj1", tn=512)
    u1, c1, mc1 = _mix1_fwd(proj1, pool_wb, o_pool_scale, o_dconv_w, o_dconv_b, o_ln_w, o_ln_b)
    dy, dyb, loss = _mm_out_loss(u1, o_w_out, x1, target, name="out_proj1_loss")
    g_o_w_out = _mm_tn(u1, dyb, name="g_w_out1", out_dtype=bf16)
    du1 = _mm_nt(dyb, o_w_out, name="d_u1")
    dz1, dc1, dpl1, dmc1, sums1 = _mix1_bwd_a(du1, proj1, c1, mc1, pool_wb, o_pool_scale, o_ln_w, o_ln_b)
    dproj1, g_dconv_w, g_pool_w = _mix1_bwd_b(dc1, dpl1, dmc1, dz1, proj1, o_dconv_w)
    g_o_w_in = _mm_tn(h1, dproj1, name="g_w_in1", out_dtype=bf16)
    dh1 = _mm_nt(dproj1, o_w_in, name="d_h1")
    d1, d1b, g_o_norm = _rms_bwd(x1, o_norm_w, dh1, dy, name="rms1_bwd")
    g_e_w_out = _mm_tn(u0, d1b, name="g_w_out0", out_dtype=bf16)
    du0 = _mm_nt(d1b, e_w_out, name="d_u0")
    dz0, do_a, dsum, dbg, dcv = _mix0_bwd_a(du0, proj0, o_a, e_conv_w)
    dcg, dhb, g_conv_w = _mix0_bwd_b(dcv, proj0, e_conv_w)
    dq_g, dk_g, dv_g = [], [], []
    for g in range(N_GROUPS):
        if g == 0:
            dq, dk, dv = _attn_bwd_local(qk, proj0, do_a, lt, dsum)
        else:
            dq, dk, dv = _attn_bwd_dil(qk, proj0, do_a, lt, dsum, g, name=f"attn_bwd{g}")
        dq_g.append(dq)
        dk_g.append(dk)
        dv_g.append(dv)
    dproj0, g_qk_norm = _qk_bwd(dq_g, dk_g, dv_g, proj0, tabs, nw, hm, dbg, dcg, dhb, dz0)
    g_e_w_in = _mm_tn(h0, dproj0, name="g_w_in0", out_dtype=bf16, chunks=N_DEV)
    dh0 = _mm_nt(dproj0, e_w_in, name="d_h0")
    grad_x, _, g_e_norm = _rms_bwd(x, e_norm_w, dh0, d1, name="rms0_bwd")

    grads = dict(
        e_norm_w=g_e_norm, e_w_in=g_e_w_in,
        e_q_norm_w=g_qk_norm[0:1, :HEAD_DIM], e_k_norm_w=g_qk_norm[1:2, :HEAD_DIM],
        e_conv_w=g_conv_w[:SC_WIDTH], e_w_out=g_e_w_out,
        o_norm_w=g_o_norm, o_w_in=g_o_w_in, o_pool_w=g_pool_w,
        o_pool_scale=sums1[0:1], o_dconv_w=g_dconv_w[:D_CONV], o_dconv_b=sums1[3:4],
        o_ln_w=sums1[1:2], o_ln_b=sums1[2:3], o_w_out=g_o_w_out)
    return loss, grad_x, grads


_MESH_ID = pl.DeviceIdType.MESH
_HBM = pl.BlockSpec(memory_space=pl.ANY)


def _place():
    x, y, c = lax.axis_index("x"), lax.axis_index("y"), lax.axis_index("c")
    return x, y, c


def _all_gather(arrs, *, name):
    n = len(arrs)

    def body(*refs):
        ins, outs = refs[:n], refs[n:2 * n]
        send_sems, recv_sems, local_sems = refs[2 * n:]
        x, y, c = _place()
        me, sibling = (x, y, c), (x, y, 1 - c)
        chips = [(1 - x, y), (x, 1 - y), (1 - x, 1 - y)]

        def slot(t, px, py, pc):
            return outs[t].at[4 * px + 2 * py + pc]

        def copy(t, k, block, to, src=None):
            dst = slot(t, *block)
            return pltpu.make_async_remote_copy(
                src_ref=dst if src is None else src, dst_ref=dst,
                send_sem=send_sems.at[7 * t + k], recv_sem=recv_sems.at[7 * t + k],
                device_id=to, device_id_type=_MESH_ID)

        mine = [pltpu.make_async_copy(ins[t], slot(t, *me), local_sems.at[t]) for t in range(n)]
        for cp in mine:
            cp.start()
        first = []
        for t in range(n):
            first.append(copy(t, 0, me, sibling, src=ins[t]))
            first += [copy(t, 1 + j, me, (*chip, c), src=ins[t]) for j, chip in enumerate(chips)]
        for cp in first:
            cp.start()
        passed = []
        for j, chip in enumerate(chips):
            for t in range(n):
                copy(t, 1 + j, (*chip, c), me).wait_recv()
                fwd = copy(t, 4 + j, (*chip, c), sibling)
                fwd.start()
                passed.append(fwd)
        for t in range(n):
            copy(t, 0, sibling, me).wait_recv()
            for j, chip in enumerate(chips):
                copy(t, 4 + j, (*chip, 1 - c), me).wait_recv()
        for cp in first + passed:
            cp.wait_send()
        for cp in mine:
            cp.wait()

    return pl.pallas_call(
        body, name=name,
        in_specs=[_HBM] * n, out_specs=[_HBM] * n,
        out_shape=[jax.ShapeDtypeStruct((N_DEV, *a.shape), a.dtype) for a in arrs],
        scratch_shapes=[pltpu.SemaphoreType.DMA((7 * n,)), pltpu.SemaphoreType.DMA((7 * n,)),
                        pltpu.SemaphoreType.DMA((n,))],
    )(*arrs)


def _exchange(chunked, whole, *, name):
    arrs = list(chunked) + list(whole)
    n, nc = len(arrs), len(chunked)

    def body(*refs):
        ins, outs = refs[:n], refs[n:2 * n]
        send_sems, recv_sems, local_sems = refs[2 * n:]
        x, y, c = _place()
        me_i = 4 * x + 2 * y + c

        def src(t, dev_i):
            return ins[t].at[dev_i] if t < nc else ins[t]

        mine = [pltpu.make_async_copy(src(t, me_i), outs[t].at[me_i], local_sems.at[t]) for t in range(n)]
        for cp in mine:
            cp.start()
        sends = []
        for m in range(1, N_DEV):
            px = 1 - x if m & 4 else x
            py = 1 - y if m & 2 else y
            pc = 1 - c if m & 1 else c
            peer_i = 4 * px + 2 * py + pc
            for t in range(n):
                sends.append((t, m, peer_i, pltpu.make_async_remote_copy(
                    src_ref=src(t, peer_i), dst_ref=outs[t].at[me_i],
                    send_sem=send_sems.at[7 * t + m - 1], recv_sem=recv_sems.at[7 * t + m - 1],
                    device_id=(px, py, pc), device_id_type=_MESH_ID)))
        for _, _, _, cp in sends:
            cp.start()
        for t, m, peer_i, cp in sends:
            pltpu.make_async_remote_copy(
                src_ref=src(t, peer_i), dst_ref=outs[t].at[peer_i],
                send_sem=send_sems.at[7 * t + m - 1], recv_sem=recv_sems.at[7 * t + m - 1],
                device_id=(x, y, c), device_id_type=_MESH_ID).wait_recv()
        for _, _, _, cp in sends:
            cp.wait_send()
        for cp in mine:
            cp.wait()

    out_shape = [jax.ShapeDtypeStruct(a.shape, a.dtype) for a in chunked]
    out_shape += [jax.ShapeDtypeStruct((N_DEV, *a.shape), a.dtype) for a in whole]
    return pl.pallas_call(
        body, name=name,
        in_specs=[_HBM] * n, out_specs=[_HBM] * n, out_shape=out_shape,
        scratch_shapes=[pltpu.SemaphoreType.DMA((7 * n,)), pltpu.SemaphoreType.DMA((7 * n,)),
                        pltpu.SemaphoreType.DMA((n,))],
    )(*arrs)


def _adamw(w, g, m, v):
    m2 = ADAM_B1 * m + (1.0 - ADAM_B1) * g
    v2 = ADAM_B2 * v + (1.0 - ADAM_B2) * (g * g)
    m_hat = m2 / (1.0 - ADAM_B1 ** ADAM_STEP)
    v_hat = v2 / (1.0 - ADAM_B2 ** ADAM_STEP)
    delta = -ADAM_LR * (m_hat / (jnp.sqrt(v_hat) + ADAM_EPS) + ADAM_WD * w)
    return delta, m2, v2


def _sum_adamw(parts, w, m, v, *, name):
    R, C = w.shape
    tr = _tile(R, 256)

    def body(p_ref, w_ref, m_ref, v_ref, g_ref, d_ref, nm_ref, nv_ref):
        g = p_ref[0].astype(f32)
        for i in range(1, N_DEV):
            g = g + p_ref[i].astype(f32)
        g_ref[...] = g
        d_ref[...], nm_ref[...], nv_ref[...] = _adamw(w_ref[...], g, m_ref[...], v_ref[...])

    spec = pl.BlockSpec((tr, C), lambda i: (i, 0))
    return pl.pallas_call(
        body, name=name, grid=(R // tr,),
        in_specs=[pl.BlockSpec((N_DEV, tr, C), lambda i: (0, i, 0)), spec, spec, spec],
        out_specs=[spec] * 4, out_shape=[jax.ShapeDtypeStruct((R, C), f32)] * 4,
        compiler_params=_cp(("parallel",)),
    )(parts, w, m, v)


def _sum_parts(parts, *, name):
    _, R, C = parts.shape

    def body(p_ref, o_ref):
        g = p_ref[0]
        for i in range(1, N_DEV):
            g = g + p_ref[i]
        o_ref[...] = g

    return pl.pallas_call(body, name=name, out_shape=jax.ShapeDtypeStruct((R, C), f32),
                          compiler_params=pltpu.CompilerParams(vmem_limit_bytes=VMEM_LIMIT))(parts)


def _adamw_small(ws, gs, ms, vs):
    n = len(ws)

    def body(*refs):
        w_r, g_r, m_r, v_r = refs[:n], refs[n:2 * n], refs[2 * n:3 * n], refs[3 * n:4 * n]
        d_r, nm_r, nv_r = refs[4 * n:5 * n], refs[5 * n:6 * n], refs[6 * n:7 * n]
        for t in range(n):
            d_r[t][...], nm_r[t][...], nv_r[t][...] = _adamw(w_r[t][...], g_r[t][...], m_r[t][...], v_r[t][...])

    shapes = [jax.ShapeDtypeStruct(w.shape, f32) for w in ws]
    outs = pl.pallas_call(body, name="adamw_small", out_shape=shapes * 3)(*ws, *gs, *ms, *vs)
    return outs[:n], outs[n:2 * n], outs[2 * n:]


_WEIGHTS = ["e_norm_w", "e_w_in", "e_q_norm_w", "e_k_norm_w", "e_conv_w", "e_w_out", "o_norm_w", "o_w_in", "o_pool_w",
            "o_pool_scale", "o_dconv_w", "o_dconv_b", "o_ln_w", "o_ln_b", "o_w_out"]
_BIG = ["e_w_in", "e_w_out", "o_w_in", "o_w_out"]
_SMALL_SHARDED = ["e_conv_w", "o_norm_w", "o_pool_scale", "o_dconv_w", "o_dconv_b", "o_ln_w", "o_ln_b"]
_SMALL_ALL = ["e_norm_w", "e_q_norm_w", "e_k_norm_w", "e_conv_w", "o_norm_w", "o_pool_w", "o_pool_scale", "o_dconv_w",
              "o_dconv_b", "o_ln_w", "o_ln_b"]


def _pack_rows(pieces):
    rows, offs, r0 = [], [], 0
    for p in pieces:
        flat = p.reshape(-1)
        nr = -(-flat.shape[0] // (8 * LANES)) * 8
        rows.append(jnp.pad(flat, (0, nr * LANES - flat.shape[0])).reshape(nr, LANES))
        offs.append((r0, nr))
        r0 += nr
    return jnp.concatenate(rows, axis=0), offs


def _unpack_rows(buf, off, shape):
    r0, nr = off
    size = int(np.prod(shape))
    return buf[..., r0:r0 + nr, :].reshape(*buf.shape[:-2], nr * LANES)[..., :size].reshape(*buf.shape[:-2], *shape)


def kernel(x, positions, e_norm_w, e_w_in, e_q_norm_w, e_k_norm_w, e_conv_w, e_w_out, o_norm_w, o_w_in, o_pool_w, o_pool_scale, o_dconv_w, o_dconv_b, o_ln_w, o_ln_b, o_w_out, loss_target, m_e_norm_w, m_e_w_in, m_e_q_norm_w, m_e_k_norm_w, m_e_conv_w, m_e_w_out, m_o_norm_w, m_o_w_in, m_o_pool_w, m_o_pool_scale, m_o_dconv_w, m_o_dconv_b, m_o_ln_w, m_o_ln_b, m_o_w_out, v_e_norm_w, v_e_w_in, v_e_q_norm_w, v_e_k_norm_w, v_e_conv_w, v_e_w_out, v_o_norm_w, v_o_w_in, v_o_pool_w, v_o_pool_scale, v_o_dconv_w, v_o_dconv_b, v_o_ln_w, v_o_ln_b, v_o_w_out):
    w = dict(e_norm_w=e_norm_w, e_w_in=e_w_in, e_q_norm_w=e_q_norm_w, e_k_norm_w=e_k_norm_w, e_conv_w=e_conv_w,
             e_w_out=e_w_out, o_norm_w=o_norm_w, o_w_in=o_w_in, o_pool_w=o_pool_w, o_pool_scale=o_pool_scale,
             o_dconv_w=o_dconv_w, o_dconv_b=o_dconv_b, o_ln_w=o_ln_w, o_ln_b=o_ln_b, o_w_out=o_w_out)
    m = dict(e_norm_w=m_e_norm_w, e_w_in=m_e_w_in, e_q_norm_w=m_e_q_norm_w, e_k_norm_w=m_e_k_norm_w, e_conv_w=m_e_conv_w,
             e_w_out=m_e_w_out, o_norm_w=m_o_norm_w, o_w_in=m_o_w_in, o_pool_w=m_o_pool_w, o_pool_scale=m_o_pool_scale,
             o_dconv_w=m_o_dconv_w, o_dconv_b=m_o_dconv_b, o_ln_w=m_o_ln_w, o_ln_b=m_o_ln_b, o_w_out=m_o_w_out)
    v = dict(e_norm_w=v_e_norm_w, e_w_in=v_e_w_in, e_q_norm_w=v_e_q_norm_w, e_k_norm_w=v_e_k_norm_w, e_conv_w=v_e_conv_w,
             e_w_out=v_e_w_out, o_norm_w=v_o_norm_w, o_w_in=v_o_w_in, o_pool_w=v_o_pool_w, o_pool_scale=v_o_pool_scale,
             o_dconv_w=v_o_dconv_w, o_dconv_b=v_o_dconv_b, o_ln_w=v_o_ln_w, o_ln_b=v_o_ln_b, o_w_out=v_o_w_out)
    S = x.shape[1]
    me = 4 * lax.axis_index("x") + 2 * lax.axis_index("y") + lax.axis_index("c")

    small_local, small_offs = _pack_rows([w[n_] for n_ in _SMALL_SHARDED])
    g_e_in, g_e_out, g_o_in, g_o_out, g_small = _all_gather(
        [w["e_w_in"][0].astype(bf16), w["e_w_out"][0].astype(bf16), w["o_w_in"][0].astype(bf16),
         w["o_w_out"][0].astype(bf16), small_local], name="gather_weights")
    full = {}
    for n_, off in zip(_SMALL_SHARDED, small_offs):
        shard = _unpack_rows(g_small, off, w[n_].shape[1:])
        full[n_] = jnp.moveaxis(shard, 0, -2).reshape(*shard.shape[1:-1], N_DEV * shard.shape[-1])
    o_w_in_full = jnp.moveaxis(g_o_in, 0, 1).reshape(D_MODEL, ODD_IN)

    loss_blk, grad_x, g = _local_step(
        x[0], positions.reshape(S, 1), loss_target[0], w["e_norm_w"], g_e_in, w["e_q_norm_w"], w["e_k_norm_w"],
        full["e_conv_w"], g_e_out.reshape(D_MODEL, D_MODEL), full["o_norm_w"].reshape(1, D_MODEL), o_w_in_full,
        w["o_pool_w"][0], full["o_pool_scale"].reshape(1, 512), full["o_dconv_w"], full["o_dconv_b"].reshape(1, 512),
        full["o_ln_w"].reshape(1, 512), full["o_ln_b"].reshape(1, 512), g_o_out.reshape(D_MODEL, D_MODEL))
    loss = lax.psum(loss_blk[0, 0], ("x", "y", "c"))

    small_grads, sg_offs = _pack_rows([g[n_] for n_ in _SMALL_ALL])
    r_e_in, r_e_out, r_o_in, r_o_out, r_small = _exchange(
        [g["e_w_in"], g["e_w_out"].reshape(N_DEV, D_MODEL // N_DEV, D_MODEL),
         jnp.moveaxis(g["o_w_in"].reshape(D_MODEL, N_DEV, ODD_IN // N_DEV), 1, 0),
         g["o_w_out"].reshape(N_DEV, D_MODEL // N_DEV, D_MODEL)],
        [small_grads], name="exchange_grads")

    out_g, out_d, out_m, out_v = {}, {}, {}, {}
    for n_, parts in zip(_BIG, (r_e_in, r_e_out, r_o_in, r_o_out)):
        res = _sum_adamw(parts, w[n_][0], m[n_][0], v[n_][0], name="adamw_" + n_)
        out_g[n_], out_d[n_], out_m[n_], out_v[n_] = [r[None] for r in res]
    small_sum = _sum_parts(r_small, name="sum_small_grads")
    gs = []
    for n_, off in zip(_SMALL_ALL, sg_offs):
        gfull = _unpack_rows(small_sum, off, g[n_].shape)
        if n_ in _SMALL_SHARDED:
            width = w[n_].shape[-1]
            gfull = lax.dynamic_slice_in_dim(gfull, me * width, width, axis=gfull.ndim - 1)
        gs.append(gfull.reshape(w[n_].shape))
    ds, nms, nvs = _adamw_small([w[n_] for n_ in _SMALL_ALL], gs, [m[n_] for n_ in _SMALL_ALL], [v[n_] for n_ in _SMALL_ALL])
    for n_, g_, d_, nm_, nv_ in zip(_SMALL_ALL, gs, ds, nms, nvs):
        out_g[n_], out_d[n_], out_m[n_], out_v[n_] = g_, d_, nm_, nv_

    return (loss, grad_x[None], *[out_g[n_] for n_ in _WEIGHTS], *[out_d[n_] for n_ in _WEIGHTS],
            *[out_m[n_] for n_ in _WEIGHTS], *[out_v[n_] for n_ in _WEIGHTS])
```

```python
import functools

import numpy as np
import jax
import jax.numpy as jnp
from jax import lax
from jax.experimental import pallas as pl
from jax.experimental.pallas import tpu as pltpu

f32 = jnp.float32
bf16 = jnp.bfloat16

D_MODEL = 1024
HEAD_DIM = 64
N_GROUPS = 3
DILATIONS = (1, 4, 16)
QBLK = 128
A_WIDTH = 512
EVEN_IN = 7168
ODD_IN = 2560
POOL_SIZES = (2, 4, 8, 16)
D_CONV = 31
SC_WIDTH = 3
ROT_HALF = 8
ROPE_THETA = 500000.0
EPS = 1e-6
NEG = -1e30
SCALE = HEAD_DIM ** -0.5
N_DEV = 8
LANES = 128
VMEM_LIMIT = 48 * 1024 * 1024

ADAM_LR = 0.001
ADAM_B1 = 0.9
ADAM_B2 = 0.999
ADAM_EPS = 1e-08
ADAM_WD = 0.01
ADAM_STEP = 10

E_Q, E_K, E_V, E_BG, E_CG, E_HB, E_Z = 0, 1536, 3072, 4608, 5120, 5632, 6144
O_UC, O_DA, O_DG, O_Z = 0, 512, 1024, 1536


def _cp(sem):
    return pltpu.CompilerParams(dimension_semantics=sem, vmem_limit_bytes=VMEM_LIMIT)


_HBM_ANY = pl.BlockSpec(memory_space=pl.ANY)


def _sigmoid(z):
    return 1.0 / (1.0 + jnp.exp(-z))


def _tile(n, pref):
    t = pref
    while n % t:
        t //= 2
    return t


def _mm_nn(a, b, *, name, out_dtype=f32, res=None, tn=1024):
    M, K = a.shape
    tm = _tile(M, 1024)
    if b.ndim == 3:
        tn = b.shape[2]
        N = b.shape[0] * tn
        b_spec = pl.BlockSpec((None, K, tn), lambda i, j: (j, 0, 0))
    else:
        N = b.shape[1]
        tn = _tile(N, tn)
        b_spec = pl.BlockSpec((K, tn), lambda i, j: (0, j))

    def body(*refs):
        if res is None:
            a_ref, b_ref, o_ref = refs
        else:
            a_ref, b_ref, r_ref, o_ref = refs
        acc = jnp.dot(a_ref[...], b_ref[...], preferred_element_type=f32)
        if res is not None:
            acc = acc + r_ref[...]
        o_ref[...] = acc.astype(out_dtype)

    in_specs = [pl.BlockSpec((tm, K), lambda i, j: (i, 0)), b_spec]
    args = [a, b]
    if res is not None:
        in_specs.append(pl.BlockSpec((tm, tn), lambda i, j: (i, j)))
        args.append(res)
    return pl.pallas_call(
        body, name=name, grid=(M // tm, N // tn), in_specs=in_specs,
        out_specs=pl.BlockSpec((tm, tn), lambda i, j: (i, j)),
        out_shape=jax.ShapeDtypeStruct((M, N), out_dtype),
        compiler_params=_cp(("parallel", "parallel")),
    )(*args)


def _mm_nt(a, b, *, name, out_dtype=f32):
    M, K = a.shape
    tm = _tile(M, 1024)
    if b.ndim == 3:
        nk, N, tk = b.shape
        b_spec = pl.BlockSpec((None, N, tk), lambda i, k: (k, 0, 0))
    else:
        N = b.shape[0]
        tk = _tile(K, 1024) if K % 1024 == 0 else _tile(K, 512)
        nk = K // tk
        b_spec = pl.BlockSpec((N, tk), lambda i, k: (0, k))

    def body(a_ref, b_ref, o_ref, acc_ref):
        k = pl.program_id(1)
        part = lax.dot_general(a_ref[...], b_ref[...], (((1,), (1,)), ((), ())), preferred_element_type=f32)

        @pl.when(k == 0)
        def _():
            acc_ref[...] = part

        @pl.when(k > 0)
        def _():
            acc_ref[...] += part

        @pl.when(k == nk - 1)
        def _():
            o_ref[...] = acc_ref[...].astype(out_dtype)

    return pl.pallas_call(
        body, name=name, grid=(M // tm, nk),
        in_specs=[pl.BlockSpec((tm, tk), lambda i, k: (i, k)), b_spec],
        out_specs=pl.BlockSpec((tm, N), lambda i, k: (i, 0)),
        out_shape=jax.ShapeDtypeStruct((M, N), out_dtype),
        scratch_shapes=[pltpu.VMEM((tm, N), f32)],
        compiler_params=_cp(("parallel", "arbitrary")),
    )(a, b)


def _mm_tn(a, b, *, name, out_dtype=f32, tn=512, chunks=None):
    S, Ka = a.shape
    N = b.shape[1]
    ts = _tile(S, 1024)
    ns = S // ts
    if chunks:
        tn = N // chunks
        out_spec = pl.BlockSpec((None, Ka, tn), lambda j, s: (j, 0, 0))
        out_shape = jax.ShapeDtypeStruct((chunks, Ka, tn), out_dtype)
    else:
        tn = _tile(N, tn)
        out_spec = pl.BlockSpec((Ka, tn), lambda j, s: (0, j))
        out_shape = jax.ShapeDtypeStruct((Ka, N), out_dtype)

    def body(a_ref, b_ref, o_ref, acc_ref):
        s = pl.program_id(1)
        part = lax.dot_general(a_ref[...], b_ref[...], (((0,), (0,)), ((), ())), preferred_element_type=f32)

        @pl.when(s == 0)
        def _():
            acc_ref[...] = part

        @pl.when(s > 0)
        def _():
            acc_ref[...] += part

        @pl.when(s == ns - 1)
        def _():
            o_ref[...] = acc_ref[...].astype(out_dtype)

    return pl.pallas_call(
        body, name=name, grid=(N // tn, ns),
        in_specs=[pl.BlockSpec((ts, Ka), lambda j, s: (s, 0)), pl.BlockSpec((ts, tn), lambda j, s: (s, j))],
        out_specs=out_spec, out_shape=out_shape,
        scratch_shapes=[pltpu.VMEM((Ka, tn), f32)],
        compiler_params=_cp(("parallel", "arbitrary")),
    )(a, b)


def _mm_out_loss(u, w, x_res, target, *, name):
    M, K = u.shape
    N = w.shape[1]
    tm = _tile(M, 512)
    nm = M // tm

    def body(u_ref, w_ref, x_ref, t_ref, dy_ref, dyb_ref, loss_ref, acc_ref):
        i = pl.program_id(0)
        y = jnp.dot(u_ref[...], w_ref[...], preferred_element_type=f32) + x_ref[...]
        err = y - t_ref[...]
        dy = err * (1.0 / N)
        dy_ref[...] = dy
        dyb_ref[...] = dy.astype(bf16)
        part = jnp.sum(err * err, axis=0, keepdims=True)

        @pl.when(i == 0)
        def _():
            acc_ref[...] = part

        @pl.when(i > 0)
        def _():
            acc_ref[...] += part

        @pl.when(i == nm - 1)
        def _():
            tot = jnp.sum(acc_ref[...], axis=1, keepdims=True)
            loss_ref[...] = jnp.broadcast_to(tot * (0.5 / N), (8, LANES))

    return pl.pallas_call(
        body, name=name, grid=(nm,),
        in_specs=[pl.BlockSpec((tm, K), lambda i: (i, 0)), pl.BlockSpec((K, N), lambda i: (0, 0)),
                  pl.BlockSpec((tm, N), lambda i: (i, 0)), pl.BlockSpec((tm, N), lambda i: (i, 0))],
        out_specs=[pl.BlockSpec((tm, N), lambda i: (i, 0)), pl.BlockSpec((tm, N), lambda i: (i, 0)),
                   pl.BlockSpec((8, LANES), lambda i: (0, 0))],
        out_shape=[jax.ShapeDtypeStruct((M, N), f32), jax.ShapeDtypeStruct((M, N), bf16),
                   jax.ShapeDtypeStruct((8, LANES), f32)],
        scratch_shapes=[pltpu.VMEM((1, N), f32)],
        compiler_params=_cp(("arbitrary",)),
    )(u, w, x_res, target)


def _rms_fwd(x, w, *, name):
    S, Dm = x.shape
    tm = _tile(S, 1024)

    def body(x_ref, w_ref, h_ref):
        xv = x_ref[...]
        r = lax.rsqrt(jnp.mean(xv * xv, axis=-1, keepdims=True) + EPS)
        h_ref[...] = (xv * r * w_ref[...]).astype(bf16)

    return pl.pallas_call(
        body, name=name, grid=(S // tm,),
        in_specs=[pl.BlockSpec((tm, Dm), lambda i: (i, 0)), pl.BlockSpec((1, Dm), lambda i: (0, 0))],
        out_specs=pl.BlockSpec((tm, Dm), lambda i: (i, 0)),
        out_shape=jax.ShapeDtypeStruct((S, Dm), bf16),
        compiler_params=_cp(("parallel",)),
    )(x, w)


def _rms_bwd(x, w, dh, res, *, name):
    S, Dm = x.shape
    tm = _tile(S, 512)

    def body(x_ref, w_ref, dh_ref, res_ref, dx_ref, dxb_ref, gw_ref):
        i = pl.program_id(0)
        xv = x_ref[...]
        r = lax.rsqrt(jnp.mean(xv * xv, axis=-1, keepdims=True) + EPS)
        xn = xv * r
        dh_v = dh_ref[...]
        dxn = dh_v * w_ref[...]
        dx = r * (dxn - xn * jnp.mean(dxn * xn, axis=-1, keepdims=True)) + res_ref[...]
        dx_ref[...] = dx
        dxb_ref[...] = dx.astype(bf16)
        part = jnp.sum(dh_v * xn, axis=0, keepdims=True)

        @pl.when(i == 0)
        def _():
            gw_ref[...] = part

        @pl.when(i > 0)
        def _():
            gw_ref[...] += part

    dx, dxb, gw = pl.pallas_call(
        body, name=name, grid=(S // tm,),
        in_specs=[pl.BlockSpec((tm, Dm), lambda i: (i, 0)), pl.BlockSpec((1, Dm), lambda i: (0, 0)),
                  pl.BlockSpec((tm, Dm), lambda i: (i, 0)), pl.BlockSpec((tm, Dm), lambda i: (i, 0))],
        out_specs=[pl.BlockSpec((tm, Dm), lambda i: (i, 0)), pl.BlockSpec((tm, Dm), lambda i: (i, 0)),
                   pl.BlockSpec((1, Dm), lambda i: (0, 0))],
        out_shape=[jax.ShapeDtypeStruct((S, Dm), f32), jax.ShapeDtypeStruct((S, Dm), bf16),
                   jax.ShapeDtypeStruct((1, Dm), f32)],
        compiler_params=_cp(("arbitrary",)),
    )(x, w, dh, res)
    return dx, dxb, gw


_INV_FREQ = [float(v) for v in (np.float32(ROPE_THETA) ** (-np.arange(ROT_HALF, dtype=np.float32) / np.float32(ROT_HALF))).astype(np.float32)]


def _rope_tables(pos_col):
    S = pos_col.shape[0]
    tm = _tile(S, 1024)

    def body(p_ref, c_ref, s1_ref, s2_ref):
        lane = lax.broadcasted_iota(jnp.int32, (tm, LANES), 1)
        lm = lane % HEAD_DIM
        fi = lm % ROT_HALF
        inv = jnp.zeros((tm, LANES), f32)
        for k in range(ROT_HALF):
            inv = jnp.where(fi == k, _INV_FREQ[k], inv)
        ang = p_ref[...].astype(f32) * inv
        cs = jnp.cos(ang)
        sn = jnp.sin(ang)
        c_ref[...] = jnp.where(lm < 2 * ROT_HALF, cs, 1.0)
        s1_ref[...] = jnp.where((lm >= ROT_HALF) & (lm < 2 * ROT_HALF), sn, 0.0)
        s2_ref[...] = jnp.where(lm < ROT_HALF, -sn, 0.0)

    spec = pl.BlockSpec((tm, LANES), lambda i: (i, 0))
    return pl.pallas_call(
        body, name="rope_tables", grid=(S // tm,),
        in_specs=[pl.BlockSpec((tm, 1), lambda i: (i, 0))],
        out_specs=[spec, spec, spec],
        out_shape=[jax.ShapeDtypeStruct((S, LANES), f32)] * 3,
        compiler_params=_cp(("parallel",)),
    )(pos_col)


def _head_mean(v, m):
    hi = v.astype(bf16)
    lo = (v - hi.astype(f32)).astype(bf16)
    return jnp.dot(hi, m, preferred_element_type=f32) + jnp.dot(lo, m, preferred_element_type=f32)


def _head_mean_matrix():
    i = np.arange(LANES)
    return jnp.asarray(((i[:, None] // HEAD_DIM) == (i[None, :] // HEAD_DIM)).astype(np.float32) / HEAD_DIM, dtype=bf16)


def _qk_prep(proj, tabs, nw, hm):
    S = proj.shape[0]
    tm = _tile(S, 512)

    def body(x_ref, c_ref, s1_ref, s2_ref, nw_ref, m_ref, o_ref):
        cb = pl.program_id(1)
        w = jnp.where(cb >= 3, nw_ref[1:2, :], nw_ref[0:1, :])
        c, s1, s2, m = c_ref[...], s1_ref[...], s2_ref[...], m_ref[...]
        for p in range(4):
            t = x_ref[:, p * LANES:(p + 1) * LANES]
            r = lax.rsqrt(_head_mean(t * t, m) + EPS)
            that = t * r * w
            o_ref[:, p * LANES:(p + 1) * LANES] = (
                that * c + pltpu.roll(that, ROT_HALF, axis=1) * s1 + pltpu.roll(that, LANES - ROT_HALF, axis=1) * s2)

    tab = pl.BlockSpec((tm, LANES), lambda i, j: (i, 0))
    return pl.pallas_call(
        body, name="qk_prep", grid=(S // tm, 6),
        in_specs=[pl.BlockSpec((tm, 512), lambda i, j: (i, j)), tab, tab, tab,
                  pl.BlockSpec((2, LANES), lambda i, j: (0, 0)), pl.BlockSpec((LANES, LANES), lambda i, j: (0, 0))],
        out_specs=pl.BlockSpec((tm, 512), lambda i, j: (i, j)),
        out_shape=jax.ShapeDtypeStruct((S, 3072), f32),
        compiler_params=_cp(("parallel", "parallel")),
    )(proj, *tabs, nw, hm)


def _key_geometry(nparts):
    qr = QBLK // nparts
    rho = lax.broadcasted_iota(jnp.int32, (2 * QBLK, 2 * QBLK), 0) % QBLK
    kap = lax.broadcasted_iota(jnp.int32, (2 * QBLK, 2 * QBLK), 1)
    n_q = QBLK + nparts * (rho % qr) + rho // qr
    tt = kap % (2 * qr)
    n_k = nparts * tt + kap // (2 * qr)
    dist = n_q - n_k
    return (dist >= 0) & (dist <= QBLK), (tt < qr).astype(jnp.int32)


def _stack_heads(t, lo):
    zero = jnp.zeros_like(t)
    return jnp.concatenate([jnp.where(lo, t, zero), jnp.where(lo, zero, t)], axis=0)


def _attn_block_fwd(qb, kcat, vcat, mask, lo):
    s = lax.dot_general(_stack_heads(qb, lo), kcat, (((1,), (1,)), ((), ())), preferred_element_type=f32) * SCALE
    s = jnp.where(mask, s, NEG)
    mx = jnp.max(s, axis=-1, keepdims=True)
    pexp = jnp.exp(s - mx)
    den = jnp.sum(pexp, axis=-1, keepdims=True)
    pn = (pexp * (1.0 / den)).astype(bf16)
    o2 = jnp.dot(pn, vcat, preferred_element_type=f32)
    lse2 = jnp.broadcast_to(mx + jnp.log(den), (2 * QBLK, LANES))
    return jnp.where(lo, o2[:QBLK], o2[QBLK:]), jnp.where(lo, lse2[:QBLK], lse2[QBLK:])


def _attn_block_bwd(qb, dob, kcat, vcat, lt, ds, mask, lo):
    lt_sw = pltpu.roll(lt, HEAD_DIM, axis=1)
    ds_sw = pltpu.roll(ds, HEAD_DIM, axis=1)
    lt2 = jnp.concatenate([jnp.where(lo, lt, lt_sw), jnp.where(lo, lt_sw, lt)], axis=0)
    ds2 = jnp.concatenate([jnp.where(lo, ds, ds_sw), jnp.where(lo, ds_sw, ds)], axis=0)
    q2 = _stack_heads(qb, lo)
    do2 = _stack_heads(dob, lo)
    s = lax.dot_general(q2, kcat, (((1,), (1,)), ((), ())), preferred_element_type=f32) * SCALE
    s = jnp.where(mask, s, NEG)
    prob = jnp.exp(s - jnp.concatenate([lt2, lt2], axis=1))
    dp = lax.dot_general(do2, vcat, (((1,), (1,)), ((), ())), preferred_element_type=f32)
    dsb = (prob * (dp - jnp.concatenate([ds2, ds2], axis=1)) * SCALE).astype(bf16)
    dq2 = jnp.dot(dsb, kcat, preferred_element_type=f32)
    dk = lax.dot_general(dsb, q2, (((0,), (0,)), ((), ())), preferred_element_type=f32)
    dv = lax.dot_general(prob.astype(bf16), do2, (((0,), (0,)), ((), ())), preferred_element_type=f32)
    return jnp.where(lo, dq2[:QBLK], dq2[QBLK:]), dk, dv


ATT_ROWS = 1024


def _attn_fwd_local(qk, proj):
    S = qk.shape[0]
    tr = _tile(S, ATT_ROWS)
    lw = 4 * LANES
    nb = tr // QBLK

    def body(q_ref, k_ref, kh_ref, v_ref, vh_ref, o_ref, lse_ref, kbuf, vbuf):
        j = pl.program_id(0)
        kbuf[0:QBLK, :] = jnp.where(j > 0, kh_ref[...], 0.0)
        kbuf[QBLK:, :] = k_ref[...]
        vbuf[0:QBLK, :] = jnp.where(j > 0, vh_ref[...], 0.0)
        vbuf[QBLK:, :] = v_ref[...]
        band, is_prev = _key_geometry(1)
        lo = lax.broadcasted_iota(jnp.int32, (QBLK, LANES), 1) < HEAD_DIM

        def blk(c, carry):
            r0 = pl.multiple_of(c * QBLK, QBLK)
            first = jnp.where((c == 0) & (j == 0), 1, 0)
            mask = band & (is_prev * first == 0)
            for pp in range(lw // LANES):
                lanes = slice(pp * LANES, (pp + 1) * LANES)
                o, lse = _attn_block_fwd(q_ref[pl.ds(r0, QBLK), lanes].astype(bf16),
                                         kbuf[pl.ds(r0, 2 * QBLK), lanes].astype(bf16),
                                         vbuf[pl.ds(r0, 2 * QBLK), lanes].astype(bf16), mask, lo)
                o_ref[pl.ds(r0, QBLK), lanes] = o
                lse_ref[pl.ds(r0, QBLK), lanes] = lse
            return carry

        lax.fori_loop(0, nb, blk, 0)

    def halo(col):
        return pl.BlockSpec((QBLK, lw), lambda j, l: (jnp.maximum(j * nb - 1, 0), col + l))

    def tile(col):
        return pl.BlockSpec((tr, lw), lambda j, l: (j, col + l))

    return pl.pallas_call(
        body, name="attn_fwd0", grid=(S // tr, A_WIDTH // lw),
        in_specs=[tile(E_Q // lw), tile(E_K // lw), halo(E_K // lw), tile(E_V // lw), halo(E_V // lw)],
        out_specs=[tile(0), tile(0)],
        out_shape=[jax.ShapeDtypeStruct((S, A_WIDTH), f32)] * 2,
        scratch_shapes=[pltpu.VMEM((QBLK + tr, lw), f32)] * 2,
        compiler_params=_cp(("parallel", "parallel")),
    )(qk, qk, qk, proj, proj)


def _attn_bwd_local(qk, proj, do_a, lt, dsum):
    S = qk.shape[0]
    tr = _tile(S, ATT_ROWS)
    lw = 2 * LANES
    nb = tr // QBLK
    nt = S // tr

    def body(q_ref, qn_ref, do_ref, don_ref, lt_ref, ltn_ref, ds_ref, dsn_ref, k_ref, kh_ref, v_ref, vh_ref,
             dq_ref, dk_ref, dv_ref, kbuf, vbuf, dkbuf, dvbuf):
        j = pl.program_id(0)
        zeros = jnp.zeros((QBLK, lw), f32)
        kbuf[0:QBLK, :] = jnp.where(j > 0, kh_ref[...], 0.0)
        kbuf[pl.ds(QBLK, tr), :] = k_ref[...]
        kbuf[pl.ds(QBLK + tr, QBLK), :] = zeros
        vbuf[0:QBLK, :] = jnp.where(j > 0, vh_ref[...], 0.0)
        vbuf[pl.ds(QBLK, tr), :] = v_ref[...]
        vbuf[pl.ds(QBLK + tr, QBLK), :] = zeros
        dkbuf[...] = jnp.zeros_like(dkbuf)
        dvbuf[...] = jnp.zeros_like(dvbuf)
        band, is_prev = _key_geometry(1)
        lo = lax.broadcasted_iota(jnp.int32, (QBLK, LANES), 1) < HEAD_DIM

        def blk(c, carry):
            r0 = pl.multiple_of(c * QBLK, QBLK)
            first = jnp.where((c == 0) & (j == 0), 1, 0)
            mask = band & (is_prev * first == 0)
            for pp in range(lw // LANES):
                lanes = slice(pp * LANES, (pp + 1) * LANES)
                dq, dk, dv = _attn_block_bwd(
                    q_ref[pl.ds(r0, QBLK), lanes].astype(bf16), do_ref[pl.ds(r0, QBLK), lanes].astype(bf16),
                    kbuf[pl.ds(r0, 2 * QBLK), lanes].astype(bf16), vbuf[pl.ds(r0, 2 * QBLK), lanes].astype(bf16),
                    lt_ref[pl.ds(r0, QBLK), lanes], ds_ref[pl.ds(r0, QBLK), lanes], mask, lo)
                dq_ref[pl.ds(r0, QBLK), lanes] = dq
                dkbuf[pl.ds(r0, 2 * QBLK), lanes] += dk
                dvbuf[pl.ds(r0, 2 * QBLK), lanes] += dv
            return carry

        lax.fori_loop(0, nb, blk, 0)

        @pl.when(j < nt - 1)
        def _():
            mask = band & (is_prev == 1)
            for pp in range(lw // LANES):
                lanes = slice(pp * LANES, (pp + 1) * LANES)
                _, dk, dv = _attn_block_bwd(
                    qn_ref[:, lanes].astype(bf16), don_ref[:, lanes].astype(bf16),
                    kbuf[pl.ds(tr, 2 * QBLK), lanes].astype(bf16), vbuf[pl.ds(tr, 2 * QBLK), lanes].astype(bf16),
                    ltn_ref[:, lanes], dsn_ref[:, lanes], mask, lo)
                dkbuf[pl.ds(tr, 2 * QBLK), lanes] += dk
                dvbuf[pl.ds(tr, 2 * QBLK), lanes] += dv

        dk_ref[...] = dkbuf[pl.ds(QBLK, tr), :]
        dv_ref[...] = dvbuf[pl.ds(QBLK, tr), :]

    def prev_halo(col):
        return pl.BlockSpec((QBLK, lw), lambda j, l: (jnp.maximum(j * nb - 1, 0), col + l))

    def next_halo(col):
        return pl.BlockSpec((QBLK, lw), lambda j, l: (jnp.minimum((j + 1) * nb, S // QBLK - 1), col + l))

    def tile(col):
        return pl.BlockSpec((tr, lw), lambda j, l: (j, col + l))

    return pl.pallas_call(
        body, name="attn_bwd0", grid=(nt, A_WIDTH // lw),
        in_specs=[tile(E_Q // lw), next_halo(E_Q // lw), tile(0), next_halo(0), tile(0), next_halo(0), tile(0), next_halo(0),
                  tile(E_K // lw), prev_halo(E_K // lw), tile(E_V // lw), prev_halo(E_V // lw)],
        out_specs=[tile(0)] * 3,
        out_shape=[jax.ShapeDtypeStruct((S, A_WIDTH), f32)] * 3,
        scratch_shapes=[pltpu.VMEM((tr + 2 * QBLK, lw), f32)] * 4,
        compiler_params=_cp(("parallel", "parallel")),
    )(qk, qk, do_a, do_a, lt, lt, dsum, dsum, qk, qk, proj, proj)


def _stream_view(a, d):
    S, W = a.shape
    return a.reshape(S // 8, 8, W) if d == 4 else a.reshape(S // 16, 2, 8, W)


def _stream_ref(ref, d, r, part, col, lw):
    n = ref.shape[0]
    if d == 4:
        return ref.at[pl.ds(0, n), r + 4 * part, pl.ds(col, lw)]
    return ref.at[pl.ds(0, n), r // 8, r % 8, pl.ds(col, lw)]


def _stream_geometry(S, d):
    nparts = 2 if d == 4 else 1
    rows = S // (d * nparts)
    return nparts, rows, QBLK // nparts


def _attn_fwd_dil(qk, proj, g, *, name):
    S = qk.shape[0]
    d = DILATIONS[g]
    nparts, rows, qr = _stream_geometry(S, d)
    nb = rows // qr
    lw = 2 * LANES if d == 4 else 4 * LANES
    nlg = A_WIDTH // lw
    nitems = d * nlg
    ins = ((0, E_Q + A_WIDTH * g, 0), (0, E_K + A_WIDTH * g, qr), (1, E_V + A_WIDTH * g, qr))

    def body(qk_hbm, pj_hbm, o_hbm, l_hbm, qbuf, kbuf, vbuf, obuf, lbuf, in_sems, out_sems):
        i = pl.program_id(0)
        slot = i % 2
        hbm_in = (qk_hbm, pj_hbm)
        bufs_in = (qbuf, kbuf, vbuf)

        def in_copies(item, sl):
            r, lg = item // nlg, item % nlg
            cps = []
            for a in range(nparts):
                for t, (src, col, pad) in enumerate(ins):
                    cps.append(pltpu.make_async_copy(
                        _stream_ref(hbm_in[src], d, r, a, pl.multiple_of(col + lw * lg, LANES), lw),
                        bufs_in[t].at[sl, a, pl.ds(pad, rows), :], in_sems.at[sl, 3 * a + t]))
            return cps

        def out_copies(item, sl):
            r, lg = item // nlg, item % nlg
            cps = []
            for a in range(nparts):
                for t, (buf, dst) in enumerate(((obuf, o_hbm), (lbuf, l_hbm))):
                    cps.append(pltpu.make_async_copy(
                        buf.at[sl, a], _stream_ref(dst, d, r, a, pl.multiple_of(lw * lg, LANES), lw),
                        out_sems.at[sl, 2 * a + t]))
            return cps

        @pl.when(i == 0)
        def _():
            for sl in range(2):
                for a in range(nparts):
                    kbuf[sl, a, 0:qr, :] = jnp.zeros((qr, lw), f32)
                    vbuf[sl, a, 0:qr, :] = jnp.zeros((qr, lw), f32)
            for cp in in_copies(0, 0):
                cp.start()

        @pl.when(i + 1 < nitems)
        def _():
            for cp in in_copies(i + 1, 1 - slot):
                cp.start()

        for cp in in_copies(i, slot):
            cp.wait()

        @pl.when(i >= 2)
        def _():
            for cp in out_copies(i - 2, slot):
                cp.wait()

        band, is_prev = _key_geometry(nparts)
        lo = lax.broadcasted_iota(jnp.int32, (QBLK, LANES), 1) < HEAD_DIM

        def blk(c, carry):
            r0 = pl.multiple_of(c * qr, qr)
            mask = band & (is_prev * jnp.where(c == 0, 1, 0) == 0)
            for pp in range(lw // LANES):
                lanes = slice(pp * LANES, (pp + 1) * LANES)
                qb = jnp.concatenate([qbuf[slot, a, pl.ds(r0, qr), lanes] for a in range(nparts)], axis=0).astype(bf16)
                kcat = jnp.concatenate([kbuf[slot, a, pl.ds(r0, 2 * qr), lanes] for a in range(nparts)], axis=0).astype(bf16)
                vcat = jnp.concatenate([vbuf[slot, a, pl.ds(r0, 2 * qr), lanes] for a in range(nparts)], axis=0).astype(bf16)
                o, lse = _attn_block_fwd(qb, kcat, vcat, mask, lo)
                for a in range(nparts):
                    obuf[slot, a, pl.ds(r0, qr), lanes] = o[a * qr:(a + 1) * qr]
                    lbuf[slot, a, pl.ds(r0, qr), lanes] = lse[a * qr:(a + 1) * qr]
            return carry

        lax.fori_loop(0, nb, blk, 0)

        for cp in out_copies(i, slot):
            cp.start()

        @pl.when(i == nitems - 1)
        def _():
            for cp in out_copies(i - 1, 1 - slot) + out_copies(i, slot):
                cp.wait()

    vshape = (S // 8, 8, A_WIDTH) if d == 4 else (S // 16, 2, 8, A_WIDTH)
    o, lse = pl.pallas_call(
        body, name=name, grid=(nitems,),
        in_specs=[_HBM_ANY, _HBM_ANY], out_specs=[_HBM_ANY, _HBM_ANY],
        out_shape=[jax.ShapeDtypeStruct(vshape, f32)] * 2,
        scratch_shapes=[pltpu.VMEM((2, nparts, rows, lw), f32), pltpu.VMEM((2, nparts, qr + rows, lw), f32),
                        pltpu.VMEM((2, nparts, qr + rows, lw), f32), pltpu.VMEM((2, nparts, rows, lw), f32),
                        pltpu.VMEM((2, nparts, rows, lw), f32),
                        pltpu.SemaphoreType.DMA((2, 3 * nparts)), pltpu.SemaphoreType.DMA((2, 2 * nparts))],
        compiler_params=_cp(("arbitrary",)),
    )(_stream_view(qk, d), _stream_view(proj, d))
    return o.reshape(S, A_WIDTH), lse.reshape(S, A_WIDTH)


def _attn_bwd_dil(qk, proj, do_a, lt, dsum, g, *, name):
    S = qk.shape[0]
    d = DILATIONS[g]
    nparts, rows, qr = _stream_geometry(S, d)
    nb = rows // qr
    lw = LANES if d == 4 else 4 * LANES
    nlg = A_WIDTH // lw
    nitems = d * nlg
    ins = ((0, E_Q + A_WIDTH * g, 0), (2, 0, 0), (3, 0, 0), (4, 0, 0), (0, E_K + A_WIDTH * g, qr), (1, E_V + A_WIDTH * g, qr))
    n_in = len(ins)

    def body(qk_hbm, pj_hbm, do_hbm, lt_hbm, ds_hbm, dq_hbm, dk_hbm, dv_hbm,
             qbuf, dobuf, ltbuf, dsbuf, kbuf, vbuf, dqbuf, dkbuf, dvbuf, in_sems, out_sems):
        i = pl.program_id(0)
        slot = i % 2
        hbm_in = (qk_hbm, pj_hbm, do_hbm, lt_hbm, ds_hbm)
        bufs_in = (qbuf, dobuf, ltbuf, dsbuf, kbuf, vbuf)

        def in_copies(item, sl):
            r, lg = item // nlg, item % nlg
            cps = []
            for a in range(nparts):
                for t, (src, col, pad) in enumerate(ins):
                    cps.append(pltpu.make_async_copy(
                        _stream_ref(hbm_in[src], d, r, a, pl.multiple_of(col + lw * lg, LANES), lw),
                        bufs_in[t].at[sl, a, pl.ds(pad, rows), :], in_sems.at[sl, n_in * a + t]))
            return cps

        def out_copies(item, sl):
            r, lg = item // nlg, item % nlg
            cps = []
            for a in range(nparts):
                for t, (buf, dst, pad) in enumerate(((dqbuf, dq_hbm, 0), (dkbuf, dk_hbm, qr), (dvbuf, dv_hbm, qr))):
                    cps.append(pltpu.make_async_copy(
                        buf.at[sl, a, pl.ds(pad, rows), :],
                        _stream_ref(dst, d, r, a, pl.multiple_of(lw * lg, LANES), lw), out_sems.at[sl, 3 * a + t]))
            return cps

        @pl.when(i == 0)
        def _():
            for sl in range(2):
                for a in range(nparts):
                    kbuf[sl, a, 0:qr, :] = jnp.zeros((qr, lw), f32)
                    vbuf[sl, a, 0:qr, :] = jnp.zeros((qr, lw), f32)
            for cp in in_copies(0, 0):
                cp.start()

        @pl.when(i + 1 < nitems)
        def _():
            for cp in in_copies(i + 1, 1 - slot):
                cp.start()

        for cp in in_copies(i, slot):
            cp.wait()

        @pl.when(i >= 2)
        def _():
            for cp in out_copies(i - 2, slot):
                cp.wait()

        for a in range(nparts):
            dkbuf[slot, a] = jnp.zeros((qr + rows, lw), f32)
            dvbuf[slot, a] = jnp.zeros((qr + rows, lw), f32)
        band, is_prev = _key_geometry(nparts)
        lo = lax.broadcasted_iota(jnp.int32, (QBLK, LANES), 1) < HEAD_DIM

        def blk(c, carry):
            r0 = pl.multiple_of(c * qr, qr)
            mask = band & (is_prev * jnp.where(c == 0, 1, 0) == 0)

            def rows_of(buf, n, lanes):
                return jnp.concatenate([buf[slot, a, pl.ds(r0, n), lanes] for a in range(nparts)], axis=0)

            for pp in range(lw // LANES):
                lanes = slice(pp * LANES, (pp + 1) * LANES)
                dq, dk, dv = _attn_block_bwd(
                    rows_of(qbuf, qr, lanes).astype(bf16), rows_of(dobuf, qr, lanes).astype(bf16),
                    rows_of(kbuf, 2 * qr, lanes).astype(bf16), rows_of(vbuf, 2 * qr, lanes).astype(bf16),
                    rows_of(ltbuf, qr, lanes), rows_of(dsbuf, qr, lanes), mask, lo)
                for a in range(nparts):
                    dqbuf[slot, a, pl.ds(r0, qr), lanes] = dq[a * qr:(a + 1) * qr]
                    dkbuf[slot, a, pl.ds(r0, 2 * qr), lanes] += dk[2 * a * qr:2 * (a + 1) * qr]
                    dvbuf[slot, a, pl.ds(r0, 2 * qr), lanes] += dv[2 * a * qr:2 * (a + 1) * qr]
            return carry

        lax.fori_loop(0, nb, blk, 0)

        for cp in out_copies(i, slot):
            cp.start()

        @pl.when(i == nitems - 1)
        def _():
            for cp in out_copies(i - 1, 1 - slot) + out_copies(i, slot):
                cp.wait()

    vshape = (S // 8, 8, A_WIDTH) if d == 4 else (S // 16, 2, 8, A_WIDTH)
    plain = pltpu.VMEM((2, nparts, rows, lw), f32)
    padded = pltpu.VMEM((2, nparts, qr + rows, lw), f32)
    outs = pl.pallas_call(
        body, name=name, grid=(nitems,),
        in_specs=[_HBM_ANY] * 5, out_specs=[_HBM_ANY] * 3,
        out_shape=[jax.ShapeDtypeStruct(vshape, f32)] * 3,
        scratch_shapes=[plain, plain, plain, plain, padded, padded, plain, padded, padded,
                        pltpu.SemaphoreType.DMA((2, n_in * nparts)), pltpu.SemaphoreType.DMA((2, 3 * nparts))],
        compiler_params=_cp(("arbitrary",)),
    )(*[_stream_view(a, d) for a in (qk, proj, do_a, lt, dsum)])
    return [o.reshape(S, A_WIDTH) for o in outs]


def _prev_halo(tm, h, col):
    return pl.BlockSpec((h, 512), lambda i: (jnp.maximum(i * (tm // h) - 1, 0), col))


def _next_halo(tm, h, col, S):
    return pl.BlockSpec((h, 512), lambda i: (jnp.minimum((i + 1) * (tm // h), S // h - 1), col))


def _mix0_fwd(o_g, lse_g, proj, conv_w):
    S = proj.shape[0]
    tm = _tile(S, 256)

    def body(o0, o1, o2, l0, l1, l2, bg_ref, cg_ref, hb_ref, z_ref, cgh_ref, hbh_ref, w_ref,
             u_ref, oa_ref, lt_ref, tbuf):
        i = pl.program_id(0)
        ls = [l0[...], l1[...], l2[...]]
        mx = jnp.maximum(jnp.maximum(ls[0], ls[1]), ls[2])
        es = [jnp.exp(l - mx) for l in ls]
        tot = es[0] + es[1] + es[2]
        lt_ref[...] = mx + jnp.log(tot)
        inv = 1.0 / tot
        z = z_ref[...]
        sz = z * _sigmoid(z)
        oa = (es[0] * inv) * o0[...] + (es[1] * inv) * o1[...] + (es[2] * inv) * o2[...]
        oa_ref[...] = oa
        u_ref[:, :A_WIDTH] = (oa * sz[:, :A_WIDTH]).astype(bf16)
        t = cg_ref[...] * hb_ref[...]
        tbuf[0:8, :] = jnp.where(i > 0, cgh_ref[...] * hbh_ref[...], 0.0)
        tbuf[8:, :] = t
        cv = w_ref[2:3, :] * t + w_ref[1:2, :] * tbuf[pl.ds(7, tm), :] + w_ref[0:1, :] * tbuf[pl.ds(6, tm), :]
        u_ref[:, A_WIDTH:] = (bg_ref[...] * cv * sz[:, A_WIDTH:]).astype(bf16)

    row = lambda w, c: pl.BlockSpec((tm, w), lambda i: (i, c))
    return pl.pallas_call(
        body, name="mix0_fwd", grid=(S // tm,),
        in_specs=[row(512, 0)] * 6
        + [row(512, E_BG // 512), row(512, E_CG // 512), row(512, E_HB // 512), row(1024, E_Z // 1024),
           _prev_halo(tm, 8, E_CG // 512), _prev_halo(tm, 8, E_HB // 512), pl.BlockSpec((SC_WIDTH, 512), lambda i: (0, 0))],
        out_specs=[row(1024, 0), row(512, 0), row(512, 0)],
        out_shape=[jax.ShapeDtypeStruct((S, D_MODEL), bf16), jax.ShapeDtypeStruct((S, A_WIDTH), f32),
                   jax.ShapeDtypeStruct((S, A_WIDTH), f32)],
        scratch_shapes=[pltpu.VMEM((tm + 8, 512), f32)],
        compiler_params=_cp(("parallel",)),
    )(*o_g, *lse_g, proj, proj, proj, proj, proj, proj, conv_w)


def _dsilu(z, sg):
    return sg * (1.0 + z * (1.0 - sg))


def _mix0_bwd_a(du, proj, o_a, conv_w):
    S = proj.shape[0]
    tm = _tile(S, 256)

    def body(du_ref, bg_ref, cg_ref, hb_ref, z_ref, cgh_ref, hbh_ref, oa_ref, w_ref,
             dz_ref, doa_ref, ds_ref, dbg_ref, dcv_ref, tbuf):
        i = pl.program_id(0)
        lo = lax.broadcasted_iota(jnp.int32, (tm, LANES), 1) < HEAD_DIM
        z = z_ref[...]
        sg = _sigmoid(z)
        sz = z * sg
        dsz = _dsilu(z, sg)
        du_v = du_ref[...]
        t = cg_ref[...] * hb_ref[...]
        tbuf[0:8, :] = jnp.where(i > 0, cgh_ref[...] * hbh_ref[...], 0.0)
        tbuf[8:, :] = t
        cv = w_ref[2:3, :] * t + w_ref[1:2, :] * tbuf[pl.ds(7, tm), :] + w_ref[0:1, :] * tbuf[pl.ds(6, tm), :]
        bg = bg_ref[...]
        oa = oa_ref[...]
        dz_ref[:, :A_WIDTH] = (du_v[:, :A_WIDTH] * oa * dsz[:, :A_WIDTH]).astype(bf16)
        dz_ref[:, A_WIDTH:] = (du_v[:, A_WIDTH:] * (bg * cv) * dsz[:, A_WIDTH:]).astype(bf16)
        doa = du_v[:, :A_WIDTH] * sz[:, :A_WIDTH]
        dyb = du_v[:, A_WIDTH:] * sz[:, A_WIDTH:]
        doa_ref[...] = doa
        dbg_ref[...] = (dyb * cv).astype(bf16)
        dcv_ref[...] = dyb * bg
        prod = doa * oa
        for p in range(4):
            pp = prod[:, p * LANES:(p + 1) * LANES]
            sa = jnp.sum(jnp.where(lo, pp, 0.0), axis=-1, keepdims=True)
            sb = jnp.sum(jnp.where(lo, 0.0, pp), axis=-1, keepdims=True)
            ds_ref[:, p * LANES:(p + 1) * LANES] = jnp.where(lo, sa, sb)

    row = lambda w, c: pl.BlockSpec((tm, w), lambda i: (i, c))
    return pl.pallas_call(
        body, name="mix0_bwd_a", grid=(S // tm,),
        in_specs=[row(1024, 0), row(512, E_BG // 512), row(512, E_CG // 512), row(512, E_HB // 512), row(1024, E_Z // 1024),
                  _prev_halo(tm, 8, E_CG // 512), _prev_halo(tm, 8, E_HB // 512), row(512, 0),
                  pl.BlockSpec((SC_WIDTH, 512), lambda i: (0, 0))],
        out_specs=[row(1024, 0), row(512, 0), row(512, 0), row(512, 0), row(512, 0)],
        out_shape=[jax.ShapeDtypeStruct((S, D_MODEL), bf16), jax.ShapeDtypeStruct((S, A_WIDTH), f32),
                   jax.ShapeDtypeStruct((S, A_WIDTH), f32), jax.ShapeDtypeStruct((S, 512), bf16),
                   jax.ShapeDtypeStruct((S, 512), f32)],
        scratch_shapes=[pltpu.VMEM((tm + 8, 512), f32)],
        compiler_params=_cp(("parallel",)),
    )(du, proj, proj, proj, proj, proj, proj, o_a, conv_w)


def _mix0_bwd_b(dcv, proj, conv_w):
    S = proj.shape[0]
    tm = _tile(S, 256)
    nt = S // tm

    def body(dcv_ref, dcvn_ref, cg_ref, hb_ref, cgh_ref, hbh_ref, w_ref, dcg_ref, dhb_ref, gw_ref, tbuf, dbuf):
        i = pl.program_id(0)
        cg = cg_ref[...]
        hb = hb_ref[...]
        t = cg * hb
        tbuf[0:8, :] = jnp.where(i > 0, cgh_ref[...] * hbh_ref[...], 0.0)
        tbuf[8:, :] = t
        dcv_v = dcv_ref[...]
        dbuf[0:tm, :] = dcv_v
        dbuf[tm:, :] = jnp.where(i < nt - 1, dcvn_ref[...], 0.0)
        dt = w_ref[2:3, :] * dcv_v + w_ref[1:2, :] * dbuf[pl.ds(1, tm), :] + w_ref[0:1, :] * dbuf[pl.ds(2, tm), :]
        dcg_ref[...] = (dt * hb).astype(bf16)
        dhb_ref[...] = (dt * cg).astype(bf16)
        g2 = jnp.sum(dcv_v * t, axis=0, keepdims=True)
        g1 = jnp.sum(dcv_v * tbuf[pl.ds(7, tm), :], axis=0, keepdims=True)
        g0 = jnp.sum(dcv_v * tbuf[pl.ds(6, tm), :], axis=0, keepdims=True)
        part = jnp.concatenate([g0, g1, g2, jnp.zeros((5, 512), f32)], axis=0)

        @pl.when(i == 0)
        def _():
            gw_ref[...] = part

        @pl.when(i > 0)
        def _():
            gw_ref[...] += part

    row = lambda w, c: pl.BlockSpec((tm, w), lambda i: (i, c))
    return pl.pallas_call(
        body, name="mix0_bwd_b", grid=(nt,),
        in_specs=[row(512, 0), _next_halo(tm, 8, 0, S), row(512, E_CG // 512), row(512, E_HB // 512),
                  _prev_halo(tm, 8, E_CG // 512), _prev_halo(tm, 8, E_HB // 512),
                  pl.BlockSpec((SC_WIDTH, 512), lambda i: (0, 0))],
        out_specs=[row(512, 0), row(512, 0), pl.BlockSpec((8, 512), lambda i: (0, 0))],
        out_shape=[jax.ShapeDtypeStruct((S, 512), bf16), jax.ShapeDtypeStruct((S, 512), bf16),
                   jax.ShapeDtypeStruct((8, 512), f32)],
        scratch_shapes=[pltpu.VMEM((tm + 8, 512), f32), pltpu.VMEM((tm + 8, 512), f32)],
        compiler_params=_cp(("arbitrary",)),
    )(dcv, dcv, proj, proj, proj, proj, conv_w)


def _qk_bwd(dq_g, dk_g, dv_g, proj, tabs, nw, hm, dbg, dcg, dhb, dz):
    S = proj.shape[0]
    tm = _tile(S, 256)

    def body(*refs):
        d_refs = refs[0:6]
        dv_refs = refs[6:9]
        x_ref, c_ref, s1_ref, s2_ref, nw_ref, m_ref, dbg_ref, dcg_ref, dhb_ref, dz_ref, o_ref, gw_ref = refs[9:]
        i = pl.program_id(0)
        c, s1, s2, m = c_ref[...], s1_ref[...], s2_ref[...], m_ref[...]
        accs = []
        for kind in range(2):
            w = nw_ref[kind:kind + 1, :]
            acc = jnp.zeros((1, LANES), f32)
            for gi in range(N_GROUPS):
                for p in range(4):
                    col = kind * 1536 + gi * 512 + p * LANES
                    dout = d_refs[kind * 3 + gi][:, p * LANES:(p + 1) * LANES]
                    t = x_ref[:, col:col + LANES]
                    dthat = (dout * c + pltpu.roll(dout * s1, LANES - ROT_HALF, axis=1)
                             + pltpu.roll(dout * s2, ROT_HALF, axis=1))
                    r = lax.rsqrt(_head_mean(t * t, m) + EPS)
                    tn = t * r
                    acc = acc + jnp.sum(dthat * tn, axis=0, keepdims=True)
                    dtn = dthat * w
                    o_ref[:, col:col + LANES] = (r * (dtn - tn * _head_mean(dtn * tn, m))).astype(bf16)
            accs.append(acc + pltpu.roll(acc, HEAD_DIM, axis=1))
        for gi in range(N_GROUPS):
            o_ref[:, E_V + gi * 512:E_V + (gi + 1) * 512] = dv_refs[gi][...].astype(bf16)
        o_ref[:, E_BG:E_CG] = dbg_ref[...]
        o_ref[:, E_CG:E_HB] = dcg_ref[...]
        o_ref[:, E_HB:E_Z] = dhb_ref[...]
        o_ref[:, E_Z:] = dz_ref[...]
        part = jnp.concatenate([accs[0], accs[1], jnp.zeros((6, LANES), f32)], axis=0)

        @pl.when(i == 0)
        def _():
            gw_ref[...] = part

        @pl.when(i > 0)
        def _():
            gw_ref[...] += part

    row = lambda w, c: pl.BlockSpec((tm, w), lambda i: (i, c))
    tab = row(LANES, 0)
    return pl.pallas_call(
        body, name="qk_bwd", grid=(S // tm,),
        in_specs=[row(512, 0)] * 9 + [row(3072, 0), tab, tab, tab, pl.BlockSpec((2, LANES), lambda i: (0, 0)),
                                      pl.BlockSpec((LANES, LANES), lambda i: (0, 0)),
                                      row(512, 0), row(512, 0), row(512, 0), row(1024, 0)],
        out_specs=[row(EVEN_IN, 0), pl.BlockSpec((8, LANES), lambda i: (0, 0))],
        out_shape=[jax.ShapeDtypeStruct((S, EVEN_IN), bf16), jax.ShapeDtypeStruct((8, LANES), f32)],
        compiler_params=_cp(("arbitrary",)),
    )(*dq_g, *dk_g, *dv_g, proj, *tabs, nw, hm, dbg, dcg, dhb, dz)


def _inv_count(i, tm, p):
    rowg = lax.broadcasted_iota(jnp.int32, (tm, 1), 0) + i * tm
    return 1.0 / jnp.minimum(rowg + 1, p).astype(f32)


def _layer_norm_stats(c):
    mu = jnp.mean(c, axis=-1, keepdims=True)
    cen = c - mu
    rstd = lax.rsqrt(jnp.mean(cen * cen, axis=-1, keepdims=True) + EPS)
    return cen * rstd, rstd


def _fill_pool_buf(i, ubuf, uc_ref, uch_ref):
    ubuf[0:16, :] = jnp.where(i > 0, uch_ref[...], 0.0)
    ubuf[16:, :] = uc_ref[...]


def _pooled(i, tm, ubuf, gi):
    p = POOL_SIZES[gi]
    cols = slice(gi * LANES, (gi + 1) * LANES)
    acc = ubuf[pl.ds(16, tm), cols]
    cur = acc
    for jj in range(1, p):
        acc = acc + ubuf[pl.ds(16 - jj, tm), cols]
    return acc * _inv_count(i, tm, p) - cur


def _fill_glu_buf(i, gbuf, da_ref, dg_ref, dah_ref, dgh_ref):
    gbuf[0:32, :] = jnp.where(i > 0, dah_ref[...] * _sigmoid(dgh_ref[...]), 0.0)
    gbuf[32:, :] = da_ref[...] * _sigmoid(dg_ref[...])


def _shift_copies(buf, sh, tm):
    for b in range(1, 8):
        sh[b - 1] = buf[pl.ds(b, tm + 24), :]


def _window(buf, sh, off, tm):
    b = off % 8
    if b == 0:
        return buf[pl.ds(off, tm), :]
    return sh[b - 1, pl.ds(off - b, tm), :]


def _mix1_fwd(proj, pool_w, pool_scale, dconv_w, dconv_b, ln_w, ln_b):
    S = proj.shape[0]
    tm = _tile(S, 256)

    def body(uc_ref, uch_ref, da_ref, dg_ref, dah_ref, dgh_ref, za_ref, zb_ref, pw_ref, ps_ref, cw_ref, cb_ref,
             lw_ref, lb_ref, u_ref, c_ref, mc_ref, ubuf, gbuf, gsh):
        i = pl.program_id(0)
        _fill_pool_buf(i, ubuf, uc_ref, uch_ref)
        za = za_ref[...]
        for gi in range(4):
            cols = slice(gi * LANES, (gi + 1) * LANES)
            mc = jnp.dot(_pooled(i, tm, ubuf, gi).astype(bf16), pw_ref[gi], preferred_element_type=f32)
            mc_ref[:, cols] = mc
            zg = za[:, cols]
            u_ref[:, cols] = (mc * ps_ref[:, cols] * (zg * _sigmoid(zg))).astype(bf16)
        _fill_glu_buf(i, gbuf, da_ref, dg_ref, dah_ref, dgh_ref)
        _shift_copies(gbuf, gsh, tm)
        c = jnp.zeros((tm, 512), f32) + cb_ref[...]
        for k in range(D_CONV):
            c = c + cw_ref[k:k + 1, :] * _window(gbuf, gsh, 32 - (D_CONV - 1) + k, tm)
        c_ref[...] = c
        yhat, _ = _layer_norm_stats(c)
        l = yhat * lw_ref[...] + lb_ref[...]
        zb = zb_ref[...]
        u_ref[:, 512:] = (l * _sigmoid(l) * (zb * _sigmoid(zb))).astype(bf16)

    row = lambda w, c: pl.BlockSpec((tm, w), lambda i: (i, c))
    vec = pl.BlockSpec((1, 512), lambda i: (0, 0))
    return pl.pallas_call(
        body, name="mix1_fwd", grid=(S // tm,),
        in_specs=[row(512, 0), _prev_halo(tm, 16, 0), row(512, 1), row(512, 2), _prev_halo(tm, 32, 1), _prev_halo(tm, 32, 2),
                  row(512, 3), row(512, 4), pl.BlockSpec((4, LANES, LANES), lambda i: (0, 0, 0)), vec,
                  pl.BlockSpec((D_CONV, 512), lambda i: (0, 0)), vec, vec, vec],
        out_specs=[row(1024, 0), row(512, 0), row(512, 0)],
        out_shape=[jax.ShapeDtypeStruct((S, D_MODEL), bf16), jax.ShapeDtypeStruct((S, 512), f32),
                   jax.ShapeDtypeStruct((S, 512), f32)],
        scratch_shapes=[pltpu.VMEM((tm + 16, 512), f32), pltpu.VMEM((tm + 32, 512), f32),
                        pltpu.VMEM((7, tm + 24, 512), f32)],
        compiler_params=_cp(("parallel",)),
    )(proj, proj, proj, proj, proj, proj, proj, proj, pool_w, pool_scale, dconv_w, dconv_b, ln_w, ln_b)


def _mix1_bwd_a(du, proj, c, mc, pool_w, pool_scale, ln_w, ln_b):
    S = proj.shape[0]
    tm = _tile(S, 256)

    def body(du_ref, za_ref, zb_ref, c_ref, mc_ref, pw_ref, ps_ref, lw_ref, lb_ref,
             dz_ref, dc_ref, dpl_ref, dmc_ref, acc_ref):
        i = pl.program_id(0)
        du_v = du_ref[...]
        ps = ps_ref[...]
        za = za_ref[...]
        sga = _sigmoid(za)
        mcv = mc_ref[...]
        dz_ref[:, :512] = (du_v[:, :512] * (mcv * ps) * _dsilu(za, sga)).astype(bf16)
        dyc = du_v[:, :512] * (za * sga)
        g_ps = jnp.sum(dyc * mcv, axis=0, keepdims=True)
        dmc = (dyc * ps).astype(bf16)
        dmc_ref[...] = dmc
        for gi in range(4):
            cols = slice(gi * LANES, (gi + 1) * LANES)
            dpl_ref[:, cols] = lax.dot_general(dmc[:, cols], pw_ref[gi], (((1,), (1,)), ((), ())), preferred_element_type=f32)
        yhat, rstd = _layer_norm_stats(c_ref[...])
        lw = lw_ref[...]
        l = yhat * lw + lb_ref[...]
        sgl = _sigmoid(l)
        zb = zb_ref[...]
        sgb = _sigmoid(zb)
        dz_ref[:, 512:] = (du_v[:, 512:] * (l * sgl) * _dsilu(zb, sgb)).astype(bf16)
        dl = du_v[:, 512:] * (zb * sgb) * _dsilu(l, sgl)
        g_lb = jnp.sum(dl, axis=0, keepdims=True)
        g_lw = jnp.sum(dl * yhat, axis=0, keepdims=True)
        dyh = dl * lw
        dc = rstd * (dyh - jnp.mean(dyh, axis=-1, keepdims=True) - yhat * jnp.mean(dyh * yhat, axis=-1, keepdims=True))
        dc_ref[...] = dc
        g_db = jnp.sum(dc, axis=0, keepdims=True)
        part = jnp.concatenate([g_ps, g_lw, g_lb, g_db, jnp.zeros((4, 512), f32)], axis=0)

        @pl.when(i == 0)
        def _():
            acc_ref[...] = part

        @pl.when(i > 0)
        def _():
            acc_ref[...] += part

    row = lambda w, c_: pl.BlockSpec((tm, w), lambda i: (i, c_))
    vec = pl.BlockSpec((1, 512), lambda i: (0, 0))
    return pl.pallas_call(
        body, name="mix1_bwd_a", grid=(S // tm,),
        in_specs=[row(1024, 0), row(512, 3), row(512, 4), row(512, 0), row(512, 0),
                  pl.BlockSpec((4, LANES, LANES), lambda i: (0, 0, 0)), vec, vec, vec],
        out_specs=[row(1024, 0), row(512, 0), row(512, 0), row(512, 0), pl.BlockSpec((8, 512), lambda i: (0, 0))],
        out_shape=[jax.ShapeDtypeStruct((S, D_MODEL), bf16), jax.ShapeDtypeStruct((S, 512), f32),
                   jax.ShapeDtypeStruct((S, 512), f32), jax.ShapeDtypeStruct((S, 512), bf16),
                   jax.ShapeDtypeStruct((8, 512), f32)],
        compiler_params=_cp(("arbitrary",)),
    )(du, proj, proj, c, mc, pool_w, pool_scale, ln_w, ln_b)


def _mix1_bwd_b(dc, dpl, dmc, dz, proj, dconv_w):
    S = proj.shape[0]
    tm = _tile(S, 256)
    nt = S // tm

    def body(dc_ref, dcn_ref, dpl_ref, dpn_ref, dmc_ref, dz_ref, uc_ref, uch_ref, da_ref, dg_ref,
             cw_ref, o_ref, gcw_ref, gpw_ref, ubuf, dcbuf, dpbuf, dcsh):
        i = pl.program_id(0)
        last = i == nt - 1
        _fill_pool_buf(i, ubuf, uc_ref, uch_ref)
        dcbuf[0:tm, :] = dc_ref[...]
        dcbuf[tm:, :] = jnp.where(last, 0.0, dcn_ref[...])
        _shift_copies(dcbuf, dcsh, tm)
        dpl_v = dpl_ref[...]
        for gi in range(4):
            p = POOL_SIZES[gi]
            cols = slice(gi * LANES, (gi + 1) * LANES)
            dpbuf[0:tm, cols] = dpl_v[:, cols] * _inv_count(i, tm, p)
            dpbuf[tm:, cols] = jnp.where(last, 0.0, dpn_ref[:, cols] * (1.0 / p))
        gpw = []
        for gi in range(4):
            p = POOL_SIZES[gi]
            cols = slice(gi * LANES, (gi + 1) * LANES)
            acc = -dpl_v[:, cols]
            for jj in range(p):
                acc = acc + dpbuf[pl.ds(jj, tm), cols]
            o_ref[:, cols] = acc.astype(bf16)
            pooled = _pooled(i, tm, ubuf, gi).astype(bf16)
            gpw.append(lax.dot_general(pooled, dmc_ref[:, cols], (((0,), (0,)), ((), ())), preferred_element_type=f32))
        da = da_ref[...]
        sg = _sigmoid(dg_ref[...])
        gl = da * sg
        dgl = jnp.zeros((tm, 512), f32)
        gcw = []
        for k in range(D_CONV):
            win = _window(dcbuf, dcsh, D_CONV - 1 - k, tm)
            dgl = dgl + cw_ref[k:k + 1, :] * win
            gcw.append(jnp.sum(gl * win, axis=0, keepdims=True))
        gcw.append(jnp.zeros((1, 512), f32))
        o_ref[:, O_DA:O_DG] = (dgl * sg).astype(bf16)
        o_ref[:, O_DG:O_Z] = (dgl * da * sg * (1.0 - sg)).astype(bf16)
        o_ref[:, O_Z:] = dz_ref[...]
        gcw_part = jnp.concatenate(gcw, axis=0)

        @pl.when(i == 0)
        def _():
            gcw_ref[...] = gcw_part
            for gi in range(4):
                gpw_ref[gi] = gpw[gi]

        @pl.when(i > 0)
        def _():
            gcw_ref[...] += gcw_part
            for gi in range(4):
                gpw_ref[gi] += gpw[gi]

    row = lambda w, c_: pl.BlockSpec((tm, w), lambda i: (i, c_))
    return pl.pallas_call(
        body, name="mix1_bwd_b", grid=(nt,),
        in_specs=[row(512, 0), _next_halo(tm, 32, 0, S), row(512, 0), _next_halo(tm, 16, 0, S), row(512, 0), row(1024, 0),
                  row(512, 0), _prev_halo(tm, 16, 0), row(512, 1), row(512, 2),
                  pl.BlockSpec((D_CONV, 512), lambda i: (0, 0))],
        out_specs=[row(ODD_IN, 0), pl.BlockSpec((32, 512), lambda i: (0, 0)),
                   pl.BlockSpec((4, LANES, LANES), lambda i: (0, 0, 0))],
        out_shape=[jax.ShapeDtypeStruct((S, ODD_IN), bf16), jax.ShapeDtypeStruct((32, 512), f32),
                   jax.ShapeDtypeStruct((4, LANES, LANES), f32)],
        scratch_shapes=[pltpu.VMEM((tm + 16, 512), f32), pltpu.VMEM((tm + 32, 512), f32),
                        pltpu.VMEM((tm + 16, 512), f32), pltpu.VMEM((7, tm + 24, 512), f32)],
        compiler_params=_cp(("arbitrary",)),
    )(dc, dc, dpl, dpl, dmc, dz, proj, proj, proj, proj, dconv_w)


def _local_step(x, pos_col, target, e_norm_w, e_w_in, e_q_norm_w, e_k_norm_w, e_conv_w, e_w_out,
                o_norm_w, o_w_in, o_pool_w, o_pool_scale, o_dconv_w, o_dconv_b, o_ln_w, o_ln_b, o_w_out):
    hm = _head_mean_matrix()
    nw = jnp.concatenate([jnp.tile(e_q_norm_w, (1, 2)), jnp.tile(e_k_norm_w, (1, 2))], axis=0)
    tabs = _rope_tables(pos_col)
    pool_wb = o_pool_w.astype(bf16)

    h0 = _rms_fwd(x, e_norm_w, name="rms0_fwd")
    proj0 = _mm_nn(h0, e_w_in, name="in_proj0")
    qk = _qk_prep(proj0, tabs, nw, hm)
    o_g, lse_g = [], []
    for g in range(N_GROUPS):
        o, l = _attn_fwd_local(qk, proj0) if g == 0 else _attn_fwd_dil(qk, proj0, g, name=f"attn_fwd{g}")
        o_g.append(o)
        lse_g.append(l)
    u0, o_a, lt = _mix0_fwd(o_g, lse_g, proj0, e_conv_w)
    x1 = _mm_nn(u0, e_w_out, res=x, name="out_proj0")
    h1 = _rms_fwd(x1, o_norm_w, name="rms1_fwd")
    proj1 = _mm_nn(h1, o_w_in, name="in_proj1", tn=512)
    u1, c1, mc1 = _mix1_fwd(proj1, pool_wb, o_pool_scale, o_dconv_w, o_dconv_b, o_ln_w, o_ln_b)
    dy, dyb, loss = _mm_out_loss(u1, o_w_out, x1, target, name="out_proj1_loss")
    g_o_w_out = _mm_tn(u1, dyb, name="g_w_out1", out_dtype=bf16)
    du1 = _mm_nt(dyb, o_w_out, name="d_u1")
    dz1, dc1, dpl1, dmc1, sums1 = _mix1_bwd_a(du1, proj1, c1, mc1, pool_wb, o_pool_scale, o_ln_w, o_ln_b)
    dproj1, g_dconv_w, g_pool_w = _mix1_bwd_b(dc1, dpl1, dmc1, dz1, proj1, o_dconv_w)
    g_o_w_in = _mm_tn(h1, dproj1, name="g_w_in1", out_dtype=bf16)
    dh1 = _mm_nt(dproj1, o_w_in, name="d_h1")
    d1, d1b, g_o_norm = _rms_bwd(x1, o_norm_w, dh1, dy, name="rms1_bwd")
    g_e_w_out = _mm_tn(u0, d1b, name="g_w_out0", out_dtype=bf16)
    du0 = _mm_nt(d1b, e_w_out, name="d_u0")
    dz0, do_a, dsum, dbg, dcv = _mix0_bwd_a(du0, proj0, o_a, e_conv_w)
    dcg, dhb, g_conv_w = _mix0_bwd_b(dcv, proj0, e_conv_w)
    dq_g, dk_g, dv_g = [], [], []
    for g in range(N_GROUPS):
        if g == 0:
            dq, dk, dv = _attn_bwd_local(qk, proj0, do_a, lt, dsum)
        else:
            dq, dk, dv = _attn_bwd_dil(qk, proj0, do_a, lt, dsum, g, name=f"attn_bwd{g}")
        dq_g.append(dq)
        dk_g.append(dk)
        dv_g.append(dv)
    dproj0, g_qk_norm = _qk_bwd(dq_g, dk_g, dv_g, proj0, tabs, nw, hm, dbg, dcg, dhb, dz0)
    g_e_w_in = _mm_tn(h0, dproj0, name="g_w_in0", out_dtype=bf16, chunks=N_DEV)
    dh0 = _mm_nt(dproj0, e_w_in, name="d_h0")
    grad_x, _, g_e_norm = _rms_bwd(x, e_norm_w, dh0, d1, name="rms0_bwd")

    grads = dict(
        e_norm_w=g_e_norm, e_w_in=g_e_w_in,
        e_q_norm_w=g_qk_norm[0:1, :HEAD_DIM], e_k_norm_w=g_qk_norm[1:2, :HEAD_DIM],
        e_conv_w=g_conv_w[:SC_WIDTH], e_w_out=g_e_w_out,
        o_norm_w=g_o_norm, o_w_in=g_o_w_in, o_pool_w=g_pool_w,
        o_pool_scale=sums1[0:1], o_dconv_w=g_dconv_w[:D_CONV], o_dconv_b=sums1[3:4],
        o_ln_w=sums1[1:2], o_ln_b=sums1[2:3], o_w_out=g_o_w_out)
    return loss, grad_x, grads


_MESH_ID = pl.DeviceIdType.MESH
_HBM = pl.BlockSpec(memory_space=pl.ANY)


def _place():
    x, y, c = lax.axis_index("x"), lax.axis_index("y"), lax.axis_index("c")
    return x, y, c


def _all_gather(arrs, *, name):
    n = len(arrs)

    def body(*refs):
        ins, outs = refs[:n], refs[n:2 * n]
        send_sems, recv_sems, local_sems = refs[2 * n:]
        x, y, c = _place()
        me, sibling = (x, y, c), (x, y, 1 - c)
        chips = [(1 - x, y), (x, 1 - y), (1 - x, 1 - y)]

        def slot(t, px, py, pc):
            return outs[t].at[4 * px + 2 * py + pc]

        def copy(t, k, block, to, src=None):
            dst = slot(t, *block)
            return pltpu.make_async_remote_copy(
                src_ref=dst if src is None else src, dst_ref=dst,
                send_sem=send_sems.at[7 * t + k], recv_sem=recv_sems.at[7 * t + k],
                device_id=to, device_id_type=_MESH_ID)

        mine = [pltpu.make_async_copy(ins[t], slot(t, *me), local_sems.at[t]) for t in range(n)]
        for cp in mine:
            cp.start()
        first = []
        for t in range(n):
            first.append(copy(t, 0, me, sibling, src=ins[t]))
            first += [copy(t, 1 + j, me, (*chip, c), src=ins[t]) for j, chip in enumerate(chips)]
        for cp in first:
            cp.start()
        passed = []
        for j, chip in enumerate(chips):
            for t in range(n):
                copy(t, 1 + j, (*chip, c), me).wait_recv()
                fwd = copy(t, 4 + j, (*chip, c), sibling)
                fwd.start()
                passed.append(fwd)
        for t in range(n):
            copy(t, 0, sibling, me).wait_recv()
            for j, chip in enumerate(chips):
                copy(t, 4 + j, (*chip, 1 - c), me).wait_recv()
        for cp in first + passed:
            cp.wait_send()
        for cp in mine:
            cp.wait()

    return pl.pallas_call(
        body, name=name,
        in_specs=[_HBM] * n, out_specs=[_HBM] * n,
        out_shape=[jax.ShapeDtypeStruct((N_DEV, *a.shape), a.dtype) for a in arrs],
        scratch_shapes=[pltpu.SemaphoreType.DMA((7 * n,)), pltpu.SemaphoreType.DMA((7 * n,)),
                        pltpu.SemaphoreType.DMA((n,))],
    )(*arrs)


def _exchange(chunked, whole, *, name):
    arrs = list(chunked) + list(whole)
    n, nc = len(arrs), len(chunked)

    def body(*refs):
        ins, outs = refs[:n], refs[n:2 * n]
        send_sems, recv_sems, local_sems = refs[2 * n:]
        x, y, c = _place()
        me_i = 4 * x + 2 * y + c

        def src(t, dev_i):
            return ins[t].at[dev_i] if t < nc else ins[t]

        mine = [pltpu.make_async_copy(src(t, me_i), outs[t].at[me_i], local_sems.at[t]) for t in range(n)]
        for cp in mine:
            cp.start()
        sends = []
        for m in range(1, N_DEV):
            px = 1 - x if m & 4 else x
            py = 1 - y if m & 2 else y
            pc = 1 - c if m & 1 else c
            peer_i = 4 * px + 2 * py + pc
            for t in range(n):
                sends.append((t, m, peer_i, pltpu.make_async_remote_copy(
                    src_ref=src(t, peer_i), dst_ref=outs[t].at[me_i],
                    send_sem=send_sems.at[7 * t + m - 1], recv_sem=recv_sems.at[7 * t + m - 1],
                    device_id=(px, py, pc), device_id_type=_MESH_ID)))
        for _, _, _, cp in sends:
            cp.start()
        for t, m, peer_i, cp in sends:
            pltpu.make_async_remote_copy(
                src_ref=src(t, peer_i), dst_ref=outs[t].at[peer_i],
                send_sem=send_sems.at[7 * t + m - 1], recv_sem=recv_sems.at[7 * t + m - 1],
                device_id=(x, y, c), device_id_type=_MESH_ID).wait_recv()
        for _, _, _, cp in sends:
            cp.wait_send()
        for cp in mine:
            cp.wait()

    out_shape = [jax.ShapeDtypeStruct(a.shape, a.dtype) for a in chunked]
    out_shape += [jax.ShapeDtypeStruct((N_DEV, *a.shape), a.dtype) for a in whole]
    return pl.pallas_call(
        body, name=name,
        in_specs=[_HBM] * n, out_specs=[_HBM] * n, out_shape=out_shape,
        scratch_shapes=[pltpu.SemaphoreType.DMA((7 * n,)), pltpu.SemaphoreType.DMA((7 * n,)),
                        pltpu.SemaphoreType.DMA((n,))],
    )(*arrs)


def _adamw(w, g, m, v):
    m2 = ADAM_B1 * m + (1.0 - ADAM_B1) * g
    v2 = ADAM_B2 * v + (1.0 - ADAM_B2) * (g * g)
    m_hat = m2 / (1.0 - ADAM_B1 ** ADAM_STEP)
    v_hat = v2 / (1.0 - ADAM_B2 ** ADAM_STEP)
    delta = -ADAM_LR * (m_hat / (jnp.sqrt(v_hat) + ADAM_EPS) + ADAM_WD * w)
    return delta, m2, v2


def _sum_adamw(parts, w, m, v, *, name):
    R, C = w.shape
    tr = _tile(R, 256)

    def body(p_ref, w_ref, m_ref, v_ref, g_ref, d_ref, nm_ref, nv_ref):
        g = p_ref[0].astype(f32)
        for i in range(1, N_DEV):
            g = g + p_ref[i].astype(f32)
        g_ref[...] = g
        d_ref[...], nm_ref[...], nv_ref[...] = _adamw(w_ref[...], g, m_ref[...], v_ref[...])

    spec = pl.BlockSpec((tr, C), lambda i: (i, 0))
    return pl.pallas_call(
        body, name=name, grid=(R // tr,),
        in_specs=[pl.BlockSpec((N_DEV, tr, C), lambda i: (0, i, 0)), spec, spec, spec],
        out_specs=[spec] * 4, out_shape=[jax.ShapeDtypeStruct((R, C), f32)] * 4,
        compiler_params=_cp(("parallel",)),
    )(parts, w, m, v)


def _sum_parts(parts, *, name):
    _, R, C = parts.shape

    def body(p_ref, o_ref):
        g = p_ref[0]
        for i in range(1, N_DEV):
            g = g + p_ref[i]
        o_ref[...] = g

    return pl.pallas_call(body, name=name, out_shape=jax.ShapeDtypeStruct((R, C), f32),
                          compiler_params=pltpu.CompilerParams(vmem_limit_bytes=VMEM_LIMIT))(parts)


def _adamw_small(ws, gs, ms, vs):
    n = len(ws)

    def body(*refs):
        w_r, g_r, m_r, v_r = refs[:n], refs[n:2 * n], refs[2 * n:3 * n], refs[3 * n:4 * n]
        d_r, nm_r, nv_r = refs[4 * n:5 * n], refs[5 * n:6 * n], refs[6 * n:7 * n]
        for t in range(n):
            d_r[t][...], nm_r[t][...], nv_r[t][...] = _adamw(w_r[t][...], g_r[t][...], m_r[t][...], v_r[t][...])

    shapes = [jax.ShapeDtypeStruct(w.shape, f32) for w in ws]
    outs = pl.pallas_call(body, name="adamw_small", out_shape=shapes * 3)(*ws, *gs, *ms, *vs)
    return outs[:n], outs[n:2 * n], outs[2 * n:]


_WEIGHTS = ["e_norm_w", "e_w_in", "e_q_norm_w", "e_k_norm_w", "e_conv_w", "e_w_out", "o_norm_w", "o_w_in", "o_pool_w",
            "o_pool_scale", "o_dconv_w", "o_dconv_b", "o_ln_w", "o_ln_b", "o_w_out"]
_BIG = ["e_w_in", "e_w_out", "o_w_in", "o_w_out"]
_SMALL_SHARDED = ["e_conv_w", "o_norm_w", "o_pool_scale", "o_dconv_w", "o_dconv_b", "o_ln_w", "o_ln_b"]
_SMALL_ALL = ["e_norm_w", "e_q_norm_w", "e_k_norm_w", "e_conv_w", "o_norm_w", "o_pool_w", "o_pool_scale", "o_dconv_w",
              "o_dconv_b", "o_ln_w", "o_ln_b"]


def _pack_rows(pieces):
    rows, offs, r0 = [], [], 0
    for p in pieces:
        flat = p.reshape(-1)
        nr = -(-flat.shape[0] // (8 * LANES)) * 8
        rows.append(jnp.pad(flat, (0, nr * LANES - flat.shape[0])).reshape(nr, LANES))
        offs.append((r0, nr))
        r0 += nr
    return jnp.concatenate(rows, axis=0), offs


def _unpack_rows(buf, off, shape):
    r0, nr = off
    size = int(np.prod(shape))
    return buf[..., r0:r0 + nr, :].reshape(*buf.shape[:-2], nr * LANES)[..., :size].reshape(*buf.shape[:-2], *shape)


def kernel(x, positions, e_norm_w, e_w_in, e_q_norm_w, e_k_norm_w, e_conv_w, e_w_out, o_norm_w, o_w_in, o_pool_w, o_pool_scale, o_dconv_w, o_dconv_b, o_ln_w, o_ln_b, o_w_out, loss_target, m_e_norm_w, m_e_w_in, m_e_q_norm_w, m_e_k_norm_w, m_e_conv_w, m_e_w_out, m_o_norm_w, m_o_w_in, m_o_pool_w, m_o_pool_scale, m_o_dconv_w, m_o_dconv_b, m_o_ln_w, m_o_ln_b, m_o_w_out, v_e_norm_w, v_e_w_in, v_e_q_norm_w, v_e_k_norm_w, v_e_conv_w, v_e_w_out, v_o_norm_w, v_o_w_in, v_o_pool_w, v_o_pool_scale, v_o_dconv_w, v_o_dconv_b, v_o_ln_w, v_o_ln_b, v_o_w_out):
    w = dict(e_norm_w=e_norm_w, e_w_in=e_w_in, e_q_norm_w=e_q_norm_w, e_k_norm_w=e_k_norm_w, e_conv_w=e_conv_w,
             e_w_out=e_w_out, o_norm_w=o_norm_w, o_w_in=o_w_in, o_pool_w=o_pool_w, o_pool_scale=o_pool_scale,
             o_dconv_w=o_dconv_w, o_dconv_b=o_dconv_b, o_ln_w=o_ln_w, o_ln_b=o_ln_b, o_w_out=o_w_out)
    m = dict(e_norm_w=m_e_norm_w, e_w_in=m_e_w_in, e_q_norm_w=m_e_q_norm_w, e_k_norm_w=m_e_k_norm_w, e_conv_w=m_e_conv_w,
             e_w_out=m_e_w_out, o_norm_w=m_o_norm_w, o_w_in=m_o_w_in, o_pool_w=m_o_pool_w, o_pool_scale=m_o_pool_scale,
             o_dconv_w=m_o_dconv_w, o_dconv_b=m_o_dconv_b, o_ln_w=m_o_ln_w, o_ln_b=m_o_ln_b, o_w_out=m_o_w_out)
    v = dict(e_norm_w=v_e_norm_w, e_w_in=v_e_w_in, e_q_norm_w=v_e_q_norm_w, e_k_norm_w=v_e_k_norm_w, e_conv_w=v_e_conv_w,
             e_w_out=v_e_w_out, o_norm_w=v_o_norm_w, o_w_in=v_o_w_in, o_pool_w=v_o_pool_w, o_pool_scale=v_o_pool_scale,
             o_dconv_w=v_o_dconv_w, o_dconv_b=v_o_dconv_b, o_ln_w=v_o_ln_w, o_ln_b=v_o_ln_b, o_w_out=v_o_w_out)
    S = x.shape[1]
    me = 4 * lax.axis_index("x") + 2 * lax.axis_index("y") + lax.axis_index("c")

    small_local, small_offs = _pack_rows([w[n_] for n_ in _SMALL_SHARDED])
    g_e_in, g_e_out, g_o_in, g_o_out, g_small = _all_gather(
        [w["e_w_in"][0].astype(bf16), w["e_w_out"][0].astype(bf16), w["o_w_in"][0].astype(bf16),
         w["o_w_out"][0].astype(bf16), small_local], name="gather_weights")
    full = {}
    for n_, off in zip(_SMALL_SHARDED, small_offs):
        shard = _unpack_rows(g_small, off, w[n_].shape[1:])
        full[n_] = jnp.moveaxis(shard, 0, -2).reshape(*shard.shape[1:-1], N_DEV * shard.shape[-1])
    o_w_in_full = jnp.moveaxis(g_o_in, 0, 1).reshape(D_MODEL, ODD_IN)

    loss_blk, grad_x, g = _local_step(
        x[0], positions.reshape(S, 1), loss_target[0], w["e_norm_w"], g_e_in, w["e_q_norm_w"], w["e_k_norm_w"],
        full["e_conv_w"], g_e_out.reshape(D_MODEL, D_MODEL), full["o_norm_w"].reshape(1, D_MODEL), o_w_in_full,
        w["o_pool_w"][0], full["o_pool_scale"].reshape(1, 512), full["o_dconv_w"], full["o_dconv_b"].reshape(1, 512),
        full["o_ln_w"].reshape(1, 512), full["o_ln_b"].reshape(1, 512), g_o_out.reshape(D_MODEL, D_MODEL))
    loss = lax.psum(loss_blk[0, 0], ("x", "y", "c"))

    small_grads, sg_offs = _pack_rows([g[n_] for n_ in _SMALL_ALL])
    r_e_in, r_e_out, r_o_in, r_o_out, r_small = _exchange(
        [g["e_w_in"], g["e_w_out"].reshape(N_DEV, D_MODEL // N_DEV, D_MODEL),
         jnp.moveaxis(g["o_w_in"].reshape(D_MODEL, N_DEV, ODD_IN // N_DEV), 1, 0),
         g["o_w_out"].reshape(N_DEV, D_MODEL // N_DEV, D_MODEL)],
        [small_grads], name="exchange_grads")

    out_g, out_d, out_m, out_v = {}, {}, {}, {}
    for n_, parts in zip(_BIG, (r_e_in, r_e_out, r_o_in, r_o_out)):
        res = _sum_adamw(parts, w[n_][0], m[n_][0], v[n_][0], name="adamw_" + n_)
        out_g[n_], out_d[n_], out_m[n_], out_v[n_] = [r[None] for r in res]
    small_sum = _sum_parts(r_small, name="sum_small_grads")
    gs = []
    for n_, off in zip(_SMALL_ALL, sg_offs):
        gfull = _unpack_rows(small_sum, off, g[n_].shape)
        if n_ in _SMALL_SHARDED:
            width = w[n_].shape[-1]
            gfull = lax.dynamic_slice_in_dim(gfull, me * width, width, axis=gfull.ndim - 1)
        gs.append(gfull.reshape(w[n_].shape))
    ds, nms, nvs = _adamw_small([w[n_] for n_ in _SMALL_ALL], gs, [m[n_] for n_ in _SMALL_ALL], [v[n_] for n_ in _SMALL_ALL])
    for n_, g_, d_, nm_, nv_ in zip(_SMALL_ALL, gs, ds, nms, nvs):
        out_g[n_], out_d[n_], out_m[n_], out_v[n_] = g_, d_, nm_, nv_

    return (loss, grad_x[None], *[out_g[n_] for n_ in _WEIGHTS], *[out_d[n_] for n_ in _WEIGHTS],
            *[out_m[n_] for n_ in _WEIGHTS], *[out_v[n_] for n_ in _WEIGHTS])
```

```python
import functools

import numpy as np
import jax
import jax.numpy as jnp
from jax import lax
from jax.experimental import pallas as pl
from jax.experimental.pallas import tpu as pltpu

f32 = jnp.float32
bf16 = jnp.bfloat16

D_MODEL = 1024
HEAD_DIM = 64
N_GROUPS = 3
DILATIONS = (1, 4, 16)
QBLK = 128
A_WIDTH = 512
EVEN_IN = 7168
ODD_IN = 2560
POOL_SIZES = (2, 4, 8, 16)
D_CONV = 31
SC_WIDTH = 3
ROT_HALF = 8
ROPE_THETA = 500000.0
EPS = 1e-6
NEG = -1e30
SCALE = HEAD_DIM ** -0.5
N_DEV = 8
LANES = 128
VMEM_LIMIT = 48 * 1024 * 1024

ADAM_LR = 0.001
ADAM_B1 = 0.9
ADAM_B2 = 0.999
ADAM_EPS = 1e-08
ADAM_WD = 0.01
ADAM_STEP = 10

E_Q, E_K, E_V, E_BG, E_CG, E_HB, E_Z = 0, 1536, 3072, 4608, 5120, 5632, 6144
O_UC, O_DA, O_DG, O_Z = 0, 512, 1024, 1536


def _cp(sem):
    return pltpu.CompilerParams(dimension_semantics=sem, vmem_limit_bytes=VMEM_LIMIT)


_HBM_ANY = pl.BlockSpec(memory_space=pl.ANY)


def _sigmoid(z):
    return 1.0 / (1.0 + jnp.exp(-z))


def _tile(n, pref):
    t = pref
    while n % t:
        t //= 2
    return t


def _place():
    return lax.axis_index("x"), lax.axis_index("y"), lax.axis_index("c")


def _exchange_plan(ins, outs, send_sems, recv_sems, local_sems, nc):
    n = len(ins)
    x, y, c = _place()
    me_i = 4 * x + 2 * y + c

    def src(t, dev_i):
        return ins[t].at[dev_i] if t < nc else ins[t]

    def copies(arriving):
        cps = []
        for m in range(1, N_DEV):
            px = 1 - x if m & 4 else x
            py = 1 - y if m & 2 else y
            pc = 1 - c if m & 1 else c
            peer_i = 4 * px + 2 * py + pc
            for t in range(n):
                cps.append(pltpu.make_async_remote_copy(
                    src_ref=src(t, peer_i), dst_ref=outs[t].at[peer_i if arriving else me_i],
                    send_sem=send_sems.at[7 * t + m - 1], recv_sem=recv_sems.at[7 * t + m - 1],
                    device_id=(x, y, c) if arriving else (px, py, pc), device_id_type=pl.DeviceIdType.MESH))
        return cps

    def mine():
        return [pltpu.make_async_copy(src(t, me_i), outs[t].at[me_i], local_sems.at[t]) for t in range(n)]

    def start():
        for cp in mine() + copies(False):
            cp.start()

    def wait():
        for cp in copies(True):
            cp.wait_recv()
        for cp in copies(False):
            cp.wait_send()
        for cp in mine():
            cp.wait()

    return start, wait


def _exchange_sems(n):
    return [pltpu.SemaphoreType.DMA((7 * n,)), pltpu.SemaphoreType.DMA((7 * n,)), pltpu.SemaphoreType.DMA((n,))]


def _exchange_out_shapes(chunked, whole):
    return ([jax.ShapeDtypeStruct(a.shape, a.dtype) for a in chunked]
            + [jax.ShapeDtypeStruct((N_DEV, *a.shape), a.dtype) for a in whole])


def _grid_call(body, *, name, grid, in_specs, out_specs, out_shape, scratch_shapes, sem, args, fuse=None):
    if fuse is None:
        return pl.pallas_call(body, name=name, grid=grid, in_specs=in_specs, out_specs=out_specs, out_shape=out_shape,
                              scratch_shapes=scratch_shapes, compiler_params=_cp(sem))(*args)
    chunked, whole = fuse
    ex = list(chunked) + list(whole)
    n, n_in, n_out, n_sc = len(ex), len(in_specs), len(out_specs), len(scratch_shapes)

    def fused(*refs):
        ins, ex_in = refs[:n_in], refs[n_in:n_in + n]
        outs, ex_out = refs[n_in + n:n_in + n + n_out], refs[n_in + n + n_out:n_in + 2 * n + n_out]
        scratch = refs[n_in + 2 * n + n_out:n_in + 2 * n + n_out + n_sc]
        start, wait = _exchange_plan(ex_in, ex_out, *refs[-3:], len(chunked))
        first = functools.reduce(jnp.logical_and, [pl.program_id(a) == 0 for a in range(len(grid))])
        last = functools.reduce(jnp.logical_and, [pl.program_id(a) == g - 1 for a, g in enumerate(grid)])
        pl.when(first)(start)
        body(*ins, *outs, *scratch)
        pl.when(last)(wait)

    res = pl.pallas_call(
        fused, name=name, grid=grid, in_specs=list(in_specs) + [_HBM_ANY] * n,
        out_specs=list(out_specs) + [_HBM_ANY] * n, out_shape=list(out_shape) + _exchange_out_shapes(chunked, whole),
        scratch_shapes=list(scratch_shapes) + _exchange_sems(n),
        compiler_params=_cp(("arbitrary",) * len(grid)))(*args, *ex)
    return res[:n_out], res[n_out:]


def _mm_nn(a, b, *, name, out_dtype=f32, res=None, tn=1024, fuse=None):
    M, K = a.shape
    tm = _tile(M, 1024)
    if b.ndim == 3:
        tn = b.shape[2]
        N = b.shape[0] * tn
        b_spec = pl.BlockSpec((None, K, tn), lambda i, j: (j, 0, 0))
    else:
        N = b.shape[1]
        tn = _tile(N, tn)
        b_spec = pl.BlockSpec((K, tn), lambda i, j: (0, j))

    def body(*refs):
        if res is None:
            a_ref, b_ref, o_ref = refs
        else:
            a_ref, b_ref, r_ref, o_ref = refs
        acc = jnp.dot(a_ref[...], b_ref[...], preferred_element_type=f32)
        if res is not None:
            acc = acc + r_ref[...]
        o_ref[...] = acc.astype(out_dtype)

    in_specs = [pl.BlockSpec((tm, K), lambda i, j: (i, 0)), b_spec]
    args = [a, b]
    if res is not None:
        in_specs.append(pl.BlockSpec((tm, tn), lambda i, j: (i, j)))
        args.append(res)
    out = _grid_call(
        body, name=name, grid=(M // tm, N // tn), in_specs=in_specs,
        out_specs=[pl.BlockSpec((tm, tn), lambda i, j: (i, j))],
        out_shape=[jax.ShapeDtypeStruct((M, N), out_dtype)], scratch_shapes=[],
        sem=("parallel", "parallel"), args=args, fuse=fuse)
    return out[0] if fuse is None else (out[0][0], out[1])


def _mm_nt(a, b, *, name, out_dtype=f32, fuse=None):
    M, K = a.shape
    tm = _tile(M, 1024)
    if b.ndim == 3:
        nk, N, tk = b.shape
        b_spec = pl.BlockSpec((None, N, tk), lambda i, k: (k, 0, 0))
    else:
        N = b.shape[0]
        tk = _tile(K, 1024) if K % 1024 == 0 else _tile(K, 512)
        nk = K // tk
        b_spec = pl.BlockSpec((N, tk), lambda i, k: (0, k))

    def body(a_ref, b_ref, o_ref, acc_ref):
        k = pl.program_id(1)
        part = lax.dot_general(a_ref[...], b_ref[...], (((1,), (1,)), ((), ())), preferred_element_type=f32)

        @pl.when(k == 0)
        def _():
            acc_ref[...] = part

        @pl.when(k > 0)
        def _():
            acc_ref[...] += part

        @pl.when(k == nk - 1)
        def _():
            o_ref[...] = acc_ref[...].astype(out_dtype)

    out = _grid_call(
        body, name=name, grid=(M // tm, nk),
        in_specs=[pl.BlockSpec((tm, tk), lambda i, k: (i, k)), b_spec],
        out_specs=[pl.BlockSpec((tm, N), lambda i, k: (i, 0))],
        out_shape=[jax.ShapeDtypeStruct((M, N), out_dtype)],
        scratch_shapes=[pltpu.VMEM((tm, N), f32)],
        sem=("parallel", "arbitrary"), args=[a, b], fuse=fuse)
    return out[0] if fuse is None else (out[0][0], out[1])


def _mm_tn(a, b, *, name, out_dtype=f32, tn=512, chunks=None):
    S, Ka = a.shape
    N = b.shape[1]
    ts = _tile(S, 1024)
    ns = S // ts
    if chunks:
        tn = N // chunks
        out_spec = pl.BlockSpec((None, Ka, tn), lambda j, s: (j, 0, 0))
        out_shape = jax.ShapeDtypeStruct((chunks, Ka, tn), out_dtype)
    else:
        tn = _tile(N, tn)
        out_spec = pl.BlockSpec((Ka, tn), lambda j, s: (0, j))
        out_shape = jax.ShapeDtypeStruct((Ka, N), out_dtype)

    def body(a_ref, b_ref, o_ref, acc_ref):
        s = pl.program_id(1)
        part = lax.dot_general(a_ref[...], b_ref[...], (((0,), (0,)), ((), ())), preferred_element_type=f32)

        @pl.when(s == 0)
        def _():
            acc_ref[...] = part

        @pl.when(s > 0)
        def _():
            acc_ref[...] += part

        @pl.when(s == ns - 1)
        def _():
            o_ref[...] = acc_ref[...].astype(out_dtype)

    return pl.pallas_call(
        body, name=name, grid=(N // tn, ns),
        in_specs=[pl.BlockSpec((ts, Ka), lambda j, s: (s, 0)), pl.BlockSpec((ts, tn), lambda j, s: (s, j))],
        out_specs=out_spec, out_shape=out_shape,
        scratch_shapes=[pltpu.VMEM((Ka, tn), f32)],
        compiler_params=_cp(("parallel", "arbitrary")),
    )(a, b)


def _mm_out_loss(u, w, x_res, target, *, name):
    M, K = u.shape
    N = w.shape[1]
    tm = _tile(M, 512)
    nm = M // tm

    def body(u_ref, w_ref, x_ref, t_ref, dy_ref, dyb_ref, loss_ref, acc_ref):
        i = pl.program_id(0)
        y = jnp.dot(u_ref[...], w_ref[...], preferred_element_type=f32) + x_ref[...]
        err = y - t_ref[...]
        dy = err * (1.0 / N)
        dy_ref[...] = dy
        dyb_ref[...] = dy.astype(bf16)
        part = jnp.sum(err * err, axis=0, keepdims=True)

        @pl.when(i == 0)
        def _():
            acc_ref[...] = part

        @pl.when(i > 0)
        def _():
            acc_ref[...] += part

        @pl.when(i == nm - 1)
        def _():
            tot = jnp.sum(acc_ref[...], axis=1, keepdims=True)
            loss_ref[...] = jnp.broadcast_to(tot * (0.5 / N), (8, LANES))

    return pl.pallas_call(
        body, name=name, grid=(nm,),
        in_specs=[pl.BlockSpec((tm, K), lambda i: (i, 0)), pl.BlockSpec((K, N), lambda i: (0, 0)),
                  pl.BlockSpec((tm, N), lambda i: (i, 0)), pl.BlockSpec((tm, N), lambda i: (i, 0))],
        out_specs=[pl.BlockSpec((tm, N), lambda i: (i, 0)), pl.BlockSpec((tm, N), lambda i: (i, 0)),
                   pl.BlockSpec((8, LANES), lambda i: (0, 0))],
        out_shape=[jax.ShapeDtypeStruct((M, N), f32), jax.ShapeDtypeStruct((M, N), bf16),
                   jax.ShapeDtypeStruct((8, LANES), f32)],
        scratch_shapes=[pltpu.VMEM((1, N), f32)],
        compiler_params=_cp(("arbitrary",)),
    )(u, w, x_res, target)


def _rms_fwd(x, w, *, name):
    S, Dm = x.shape
    tm = _tile(S, 1024)

    def body(x_ref, w_ref, h_ref):
        xv = x_ref[...]
        r = lax.rsqrt(jnp.mean(xv * xv, axis=-1, keepdims=True) + EPS)
        h_ref[...] = (xv * r * w_ref[...]).astype(bf16)

    return pl.pallas_call(
        body, name=name, grid=(S // tm,),
        in_specs=[pl.BlockSpec((tm, Dm), lambda i: (i, 0)), pl.BlockSpec((1, Dm), lambda i: (0, 0))],
        out_specs=pl.BlockSpec((tm, Dm), lambda i: (i, 0)),
        out_shape=jax.ShapeDtypeStruct((S, Dm), bf16),
        compiler_params=_cp(("parallel",)),
    )(x, w)


def _rms_bwd(x, w, dh, res, *, name):
    S, Dm = x.shape
    tm = _tile(S, 512)

    def body(x_ref, w_ref, dh_ref, res_ref, dx_ref, dxb_ref, gw_ref):
        i = pl.program_id(0)
        xv = x_ref[...]
        r = lax.rsqrt(jnp.mean(xv * xv, axis=-1, keepdims=True) + EPS)
        xn = xv * r
        dh_v = dh_ref[...]
        dxn = dh_v * w_ref[...]
        dx = r * (dxn - xn * jnp.mean(dxn * xn, axis=-1, keepdims=True)) + res_ref[...]
        dx_ref[...] = dx
        dxb_ref[...] = dx.astype(bf16)
        part = jnp.sum(dh_v * xn, axis=0, keepdims=True)

        @pl.when(i == 0)
        def _():
            gw_ref[...] = part

        @pl.when(i > 0)
        def _():
            gw_ref[...] += part

    dx, dxb, gw = pl.pallas_call(
        body, name=name, grid=(S // tm,),
        in_specs=[pl.BlockSpec((tm, Dm), lambda i: (i, 0)), pl.BlockSpec((1, Dm), lambda i: (0, 0)),
                  pl.BlockSpec((tm, Dm), lambda i: (i, 0)), pl.BlockSpec((tm, Dm), lambda i: (i, 0))],
        out_specs=[pl.BlockSpec((tm, Dm), lambda i: (i, 0)), pl.BlockSpec((tm, Dm), lambda i: (i, 0)),
                   pl.BlockSpec((1, Dm), lambda i: (0, 0))],
        out_shape=[jax.ShapeDtypeStruct((S, Dm), f32), jax.ShapeDtypeStruct((S, Dm), bf16),
                   jax.ShapeDtypeStruct((1, Dm), f32)],
        compiler_params=_cp(("arbitrary",)),
    )(x, w, dh, res)
    return dx, dxb, gw


_INV_FREQ = [float(v) for v in (np.float32(ROPE_THETA) ** (-np.arange(ROT_HALF, dtype=np.float32) / np.float32(ROT_HALF))).astype(np.float32)]


def _rope_tables(pos_col):
    S = pos_col.shape[0]
    tm = _tile(S, 1024)

    def body(p_ref, c_ref, s1_ref, s2_ref):
        lane = lax.broadcasted_iota(jnp.int32, (tm, LANES), 1)
        lm = lane % HEAD_DIM
        fi = lm % ROT_HALF
        inv = jnp.zeros((tm, LANES), f32)
        for k in range(ROT_HALF):
            inv = jnp.where(fi == k, _INV_FREQ[k], inv)
        ang = p_ref[...].astype(f32) * inv
        cs = jnp.cos(ang)
        sn = jnp.sin(ang)
        c_ref[...] = jnp.where(lm < 2 * ROT_HALF, cs, 1.0)
        s1_ref[...] = jnp.where((lm >= ROT_HALF) & (lm < 2 * ROT_HALF), sn, 0.0)
        s2_ref[...] = jnp.where(lm < ROT_HALF, -sn, 0.0)

    spec = pl.BlockSpec((tm, LANES), lambda i: (i, 0))
    return pl.pallas_call(
        body, name="rope_tables", grid=(S // tm,),
        in_specs=[pl.BlockSpec((tm, 1), lambda i: (i, 0))],
        out_specs=[spec, spec, spec],
        out_shape=[jax.ShapeDtypeStruct((S, LANES), f32)] * 3,
        compiler_params=_cp(("parallel",)),
    )(pos_col)


def _head_mean(v, m):
    hi = v.astype(bf16)
    lo = (v - hi.astype(f32)).astype(bf16)
    return jnp.dot(hi, m, preferred_element_type=f32) + jnp.dot(lo, m, preferred_element_type=f32)


def _head_mean_matrix():
    i = np.arange(LANES)
    return jnp.asarray(((i[:, None] // HEAD_DIM) == (i[None, :] // HEAD_DIM)).astype(np.float32) / HEAD_DIM, dtype=bf16)


def _qk_prep(proj, tabs, nw, hm):
    S = proj.shape[0]
    tm = _tile(S, 512)

    def body(x_ref, c_ref, s1_ref, s2_ref, nw_ref, m_ref, o_ref):
        cb = pl.program_id(1)
        w = jnp.where(cb >= 3, nw_ref[1:2, :], nw_ref[0:1, :])
        c, s1, s2, m = c_ref[...], s1_ref[...], s2_ref[...], m_ref[...]
        for p in range(4):
            t = x_ref[:, p * LANES:(p + 1) * LANES]
            r = lax.rsqrt(_head_mean(t * t, m) + EPS)
            that = t * r * w
            o_ref[:, p * LANES:(p + 1) * LANES] = (
                that * c + pltpu.roll(that, ROT_HALF, axis=1) * s1 + pltpu.roll(that, LANES - ROT_HALF, axis=1) * s2)

    tab = pl.BlockSpec((tm, LANES), lambda i, j: (i, 0))
    return pl.pallas_call(
        body, name="qk_prep", grid=(S // tm, 6),
        in_specs=[pl.BlockSpec((tm, 512), lambda i, j: (i, j)), tab, tab, tab,
                  pl.BlockSpec((2, LANES), lambda i, j: (0, 0)), pl.BlockSpec((LANES, LANES), lambda i, j: (0, 0))],
        out_specs=pl.BlockSpec((tm, 512), lambda i, j: (i, j)),
        out_shape=jax.ShapeDtypeStruct((S, 3072), f32),
        compiler_params=_cp(("parallel", "parallel")),
    )(proj, *tabs, nw, hm)


def _key_geometry(nparts):
    qr = QBLK // nparts
    rho = lax.broadcasted_iota(jnp.int32, (2 * QBLK, 2 * QBLK), 0) % QBLK
    kap = lax.broadcasted_iota(jnp.int32, (2 * QBLK, 2 * QBLK), 1)
    n_q = QBLK + nparts * (rho % qr) + rho // qr
    tt = kap % (2 * qr)
    n_k = nparts * tt + kap // (2 * qr)
    dist = n_q - n_k
    return (dist >= 0) & (dist <= QBLK), (tt < qr).astype(jnp.int32)


def _stack_heads(t, lo):
    zero = jnp.zeros_like(t)
    return jnp.concatenate([jnp.where(lo, t, zero), jnp.where(lo, zero, t)], axis=0)


def _attn_block_fwd(qb, kcat, vcat, mask, lo):
    s = lax.dot_general(_stack_heads(qb, lo), kcat, (((1,), (1,)), ((), ())), preferred_element_type=f32) * SCALE
    s = jnp.where(mask, s, NEG)
    mx = jnp.max(s, axis=-1, keepdims=True)
    pexp = jnp.exp(s - mx)
    den = jnp.sum(pexp, axis=-1, keepdims=True)
    pn = (pexp * (1.0 / den)).astype(bf16)
    o2 = jnp.dot(pn, vcat, preferred_element_type=f32)
    lse2 = jnp.broadcast_to(mx + jnp.log(den), (2 * QBLK, LANES))
    return jnp.where(lo, o2[:QBLK], o2[QBLK:]), jnp.where(lo, lse2[:QBLK], lse2[QBLK:])


def _attn_block_bwd(qb, dob, kcat, vcat, lt, ds, mask, lo):
    lt_sw = pltpu.roll(lt, HEAD_DIM, axis=1)
    ds_sw = pltpu.roll(ds, HEAD_DIM, axis=1)
    lt2 = jnp.concatenate([jnp.where(lo, lt, lt_sw), jnp.where(lo, lt_sw, lt)], axis=0)
    ds2 = jnp.concatenate([jnp.where(lo, ds, ds_sw), jnp.where(lo, ds_sw, ds)], axis=0)
    q2 = _stack_heads(qb, lo)
    do2 = _stack_heads(dob, lo)
    s = lax.dot_general(q2, kcat, (((1,), (1,)), ((), ())), preferred_element_type=f32) * SCALE
    s = jnp.where(mask, s, NEG)
    prob = jnp.exp(s - jnp.concatenate([lt2, lt2], axis=1))
    dp = lax.dot_general(do2, vcat, (((1,), (1,)), ((), ())), preferred_element_type=f32)
    dsb = (prob * (dp - jnp.concatenate([ds2, ds2], axis=1)) * SCALE).astype(bf16)
    dq2 = jnp.dot(dsb, kcat, preferred_element_type=f32)
    dk = lax.dot_general(dsb, q2, (((0,), (0,)), ((), ())), preferred_element_type=f32)
    dv = lax.dot_general(prob.astype(bf16), do2, (((0,), (0,)), ((), ())), preferred_element_type=f32)
    return jnp.where(lo, dq2[:QBLK], dq2[QBLK:]), dk, dv


ATT_ROWS = 1024


def _attn_fwd_local(qk, proj):
    S = qk.shape[0]
    tr = _tile(S, ATT_ROWS)
    lw = 4 * LANES
    nb = tr // QBLK

    def body(q_ref, k_ref, kh_ref, v_ref, vh_ref, o_ref, lse_ref, kbuf, vbuf):
        j = pl.program_id(0)
        kbuf[0:QBLK, :] = jnp.where(j > 0, kh_ref[...], 0.0)
        kbuf[QBLK:, :] = k_ref[...]
        vbuf[0:QBLK, :] = jnp.where(j > 0, vh_ref[...], 0.0)
        vbuf[QBLK:, :] = v_ref[...]
        band, is_prev = _key_geometry(1)
        lo = lax.broadcasted_iota(jnp.int32, (QBLK, LANES), 1) < HEAD_DIM

        def blk(c, carry):
            r0 = pl.multiple_of(c * QBLK, QBLK)
            first = jnp.where((c == 0) & (j == 0), 1, 0)
            mask = band & (is_prev * first == 0)
            for pp in range(lw // LANES):
                lanes = slice(pp * LANES, (pp + 1) * LANES)
                o, lse = _attn_block_fwd(q_ref[pl.ds(r0, QBLK), lanes].astype(bf16),
                                         kbuf[pl.ds(r0, 2 * QBLK), lanes].astype(bf16),
                                         vbuf[pl.ds(r0, 2 * QBLK), lanes].astype(bf16), mask, lo)
                o_ref[pl.ds(r0, QBLK), lanes] = o
                lse_ref[pl.ds(r0, QBLK), lanes] = lse
            return carry

        lax.fori_loop(0, nb, blk, 0)

    def halo(col):
        return pl.BlockSpec((QBLK, lw), lambda j, l: (jnp.maximum(j * nb - 1, 0), col + l))

    def tile(col):
        return pl.BlockSpec((tr, lw), lambda j, l: (j, col + l))

    return pl.pallas_call(
        body, name="attn_fwd0", grid=(S // tr, A_WIDTH // lw),
        in_specs=[tile(E_Q // lw), tile(E_K // lw), halo(E_K // lw), tile(E_V // lw), halo(E_V // lw)],
        out_specs=[tile(0), tile(0)],
        out_shape=[jax.ShapeDtypeStruct((S, A_WIDTH), f32)] * 2,
        scratch_shapes=[pltpu.VMEM((QBLK + tr, lw), f32)] * 2,
        compiler_params=_cp(("parallel", "parallel")),
    )(qk, qk, qk, proj, proj)


def _attn_bwd_local(qk, proj, do_a, lt, dsum, fuse=None):
    S = qk.shape[0]
    tr = _tile(S, ATT_ROWS)
    lw = 2 * LANES
    nb = tr // QBLK
    nt = S // tr

    def body(q_ref, qn_ref, do_ref, don_ref, lt_ref, ltn_ref, ds_ref, dsn_ref, k_ref, kh_ref, v_ref, vh_ref,
             dq_ref, dk_ref, dv_ref, kbuf, vbuf, dkbuf, dvbuf):
        j = pl.program_id(0)
        zeros = jnp.zeros((QBLK, lw), f32)
        kbuf[0:QBLK, :] = jnp.where(j > 0, kh_ref[...], 0.0)
        kbuf[pl.ds(QBLK, tr), :] = k_ref[...]
        kbuf[pl.ds(QBLK + tr, QBLK), :] = zeros
        vbuf[0:QBLK, :] = jnp.where(j > 0, vh_ref[...], 0.0)
        vbuf[pl.ds(QBLK, tr), :] = v_ref[...]
        vbuf[pl.ds(QBLK + tr, QBLK), :] = zeros
        dkbuf[...] = jnp.zeros_like(dkbuf)
        dvbuf[...] = jnp.zeros_like(dvbuf)
        band, is_prev = _key_geometry(1)
        lo = lax.broadcasted_iota(jnp.int32, (QBLK, LANES), 1) < HEAD_DIM

        def blk(c, carry):
            r0 = pl.multiple_of(c * QBLK, QBLK)
            first = jnp.where((c == 0) & (j == 0), 1, 0)
            mask = band & (is_prev * first == 0)
            for pp in range(lw // LANES):
                lanes = slice(pp * LANES, (pp + 1) * LANES)
                dq, dk, dv = _attn_block_bwd(
                    q_ref[pl.ds(r0, QBLK), lanes].astype(bf16), do_ref[pl.ds(r0, QBLK), lanes].astype(bf16),
                    kbuf[pl.ds(r0, 2 * QBLK), lanes].astype(bf16), vbuf[pl.ds(r0, 2 * QBLK), lanes].astype(bf16),
                    lt_ref[pl.ds(r0, QBLK), lanes], ds_ref[pl.ds(r0, QBLK), lanes], mask, lo)
                dq_ref[pl.ds(r0, QBLK), lanes] = dq
                dkbuf[pl.ds(r0, 2 * QBLK), lanes] += dk
                dvbuf[pl.ds(r0, 2 * QBLK), lanes] += dv
            return carry

        lax.fori_loop(0, nb, blk, 0)

        @pl.when(j < nt - 1)
        def _():
            mask = band & (is_prev == 1)
            for pp in range(lw // LANES):
                lanes = slice(pp * LANES, (pp + 1) * LANES)
                _, dk, dv = _attn_block_bwd(
                    qn_ref[:, lanes].astype(bf16), don_ref[:, lanes].astype(bf16),
                    kbuf[pl.ds(tr, 2 * QBLK), lanes].astype(bf16), vbuf[pl.ds(tr, 2 * QBLK), lanes].astype(bf16),
                    ltn_ref[:, lanes], dsn_ref[:, lanes], mask, lo)
                dkbuf[pl.ds(tr, 2 * QBLK), lanes] += dk
                dvbuf[pl.ds(tr, 2 * QBLK), lanes] += dv

        dk_ref[...] = dkbuf[pl.ds(QBLK, tr), :]
        dv_ref[...] = dvbuf[pl.ds(QBLK, tr), :]

    def prev_halo(col):
        return pl.BlockSpec((QBLK, lw), lambda j, l: (jnp.maximum(j * nb - 1, 0), col + l))

    def next_halo(col):
        return pl.BlockSpec((QBLK, lw), lambda j, l: (jnp.minimum((j + 1) * nb, S // QBLK - 1), col + l))

    def tile(col):
        return pl.BlockSpec((tr, lw), lambda j, l: (j, col + l))

    return _grid_call(
        body, name="attn_bwd0", grid=(nt, A_WIDTH // lw),
        in_specs=[tile(E_Q // lw), next_halo(E_Q // lw), tile(0), next_halo(0), tile(0), next_halo(0), tile(0), next_halo(0),
                  tile(E_K // lw), prev_halo(E_K // lw), tile(E_V // lw), prev_halo(E_V // lw)],
        out_specs=[tile(0)] * 3,
        out_shape=[jax.ShapeDtypeStruct((S, A_WIDTH), f32)] * 3,
        scratch_shapes=[pltpu.VMEM((tr + 2 * QBLK, lw), f32)] * 4,
        sem=("parallel", "parallel"), args=[qk, qk, do_a, do_a, lt, lt, dsum, dsum, qk, qk, proj, proj], fuse=fuse)


def _stream_view(a, d):
    S, W = a.shape
    return a.reshape(S // 8, 8, W) if d == 4 else a.reshape(S // 16, 2, 8, W)


def _stream_ref(ref, d, r, part, col, lw):
    n = ref.shape[0]
    if d == 4:
        return ref.at[pl.ds(0, n), r + 4 * part, pl.ds(col, lw)]
    return ref.at[pl.ds(0, n), r // 8, r % 8, pl.ds(col, lw)]


def _stream_geometry(S, d):
    nparts = 2 if d == 4 else 1
    rows = S // (d * nparts)
    return nparts, rows, QBLK // nparts


def _attn_fwd_dil(qk, proj, g, *, name):
    S = qk.shape[0]
    d = DILATIONS[g]
    nparts, rows, qr = _stream_geometry(S, d)
    nb = rows // qr
    lw = 2 * LANES if d == 4 else 4 * LANES
    nlg = A_WIDTH // lw
    nitems = d * nlg
    ins = ((0, E_Q + A_WIDTH * g, 0), (0, E_K + A_WIDTH * g, qr), (1, E_V + A_WIDTH * g, qr))

    def body(qk_hbm, pj_hbm, o_hbm, l_hbm, qbuf, kbuf, vbuf, obuf, lbuf, in_sems, out_sems):
        i = pl.program_id(0)
        slot = i % 2
        hbm_in = (qk_hbm, pj_hbm)
        bufs_in = (qbuf, kbuf, vbuf)

        def in_copies(item, sl):
            r, lg = item // nlg, item % nlg
            cps = []
            for a in range(nparts):
                for t, (src, col, pad) in enumerate(ins):
                    cps.append(pltpu.make_async_copy(
                        _stream_ref(hbm_in[src], d, r, a, pl.multiple_of(col + lw * lg, LANES), lw),
                        bufs_in[t].at[sl, a, pl.ds(pad, rows), :], in_sems.at[sl, 3 * a + t]))
            return cps

        def out_copies(item, sl):
            r, lg = item // nlg, item % nlg
            cps = []
            for a in range(nparts):
                for t, (buf, dst) in enumerate(((obuf, o_hbm), (lbuf, l_hbm))):
                    cps.append(pltpu.make_async_copy(
                        buf.at[sl, a], _stream_ref(dst, d, r, a, pl.multiple_of(lw * lg, LANES), lw),
                        out_sems.at[sl, 2 * a + t]))
            return cps

        @pl.when(i == 0)
        def _():
            for sl in range(2):
                for a in range(nparts):
                    kbuf[sl, a, 0:qr, :] = jnp.zeros((qr, lw), f32)
                    vbuf[sl, a, 0:qr, :] = jnp.zeros((qr, lw), f32)
            for cp in in_copies(0, 0):
                cp.start()

        @pl.when(i + 1 < nitems)
        def _():
            for cp in in_copies(i + 1, 1 - slot):
                cp.start()

        for cp in in_copies(i, slot):
            cp.wait()

        @pl.when(i >= 2)
        def _():
            for cp in out_copies(i - 2, slot):
                cp.wait()

        band, is_prev = _key_geometry(nparts)
        lo = lax.broadcasted_iota(jnp.int32, (QBLK, LANES), 1) < HEAD_DIM

        def blk(c, carry):
            r0 = pl.multiple_of(c * qr, qr)
            mask = band & (is_prev * jnp.where(c == 0, 1, 0) == 0)
            for pp in range(lw // LANES):
                lanes = slice(pp * LANES, (pp + 1) * LANES)
                qb = jnp.concatenate([qbuf[slot, a, pl.ds(r0, qr), lanes] for a in range(nparts)], axis=0).astype(bf16)
                kcat = jnp.concatenate([kbuf[slot, a, pl.ds(r0, 2 * qr), lanes] for a in range(nparts)], axis=0).astype(bf16)
                vcat = jnp.concatenate([vbuf[slot, a, pl.ds(r0, 2 * qr), lanes] for a in range(nparts)], axis=0).astype(bf16)
                o, lse = _attn_block_fwd(qb, kcat, vcat, mask, lo)
                for a in range(nparts):
                    obuf[slot, a, pl.ds(r0, qr), lanes] = o[a * qr:(a + 1) * qr]
                    lbuf[slot, a, pl.ds(r0, qr), lanes] = lse[a * qr:(a + 1) * qr]
            return carry

        lax.fori_loop(0, nb, blk, 0)

        for cp in out_copies(i, slot):
            cp.start()

        @pl.when(i == nitems - 1)
        def _():
            for cp in out_copies(i - 1, 1 - slot) + out_copies(i, slot):
                cp.wait()

    vshape = (S // 8, 8, A_WIDTH) if d == 4 else (S // 16, 2, 8, A_WIDTH)
    o, lse = pl.pallas_call(
        body, name=name, grid=(nitems,),
        in_specs=[_HBM_ANY, _HBM_ANY], out_specs=[_HBM_ANY, _HBM_ANY],
        out_shape=[jax.ShapeDtypeStruct(vshape, f32)] * 2,
        scratch_shapes=[pltpu.VMEM((2, nparts, rows, lw), f32), pltpu.VMEM((2, nparts, qr + rows, lw), f32),
                        pltpu.VMEM((2, nparts, qr + rows, lw), f32), pltpu.VMEM((2, nparts, rows, lw), f32),
                        pltpu.VMEM((2, nparts, rows, lw), f32),
                        pltpu.SemaphoreType.DMA((2, 3 * nparts)), pltpu.SemaphoreType.DMA((2, 2 * nparts))],
        compiler_params=_cp(("arbitrary",)),
    )(_stream_view(qk, d), _stream_view(proj, d))
    return o.reshape(S, A_WIDTH), lse.reshape(S, A_WIDTH)


def _attn_bwd_dil(qk, proj, do_a, lt, dsum, g, *, name):
    S = qk.shape[0]
    d = DILATIONS[g]
    nparts, rows, qr = _stream_geometry(S, d)
    nb = rows // qr
    lw = LANES if d == 4 else 4 * LANES
    nlg = A_WIDTH // lw
    nitems = d * nlg
    ins = ((0, E_Q + A_WIDTH * g, 0), (2, 0, 0), (3, 0, 0), (4, 0, 0), (0, E_K + A_WIDTH * g, qr), (1, E_V + A_WIDTH * g, qr))
    n_in = len(ins)

    def body(qk_hbm, pj_hbm, do_hbm, lt_hbm, ds_hbm, dq_hbm, dk_hbm, dv_hbm,
             qbuf, dobuf, ltbuf, dsbuf, kbuf, vbuf, dqbuf, dkbuf, dvbuf, in_sems, out_sems):
        i = pl.program_id(0)
        slot = i % 2
        hbm_in = (qk_hbm, pj_hbm, do_hbm, lt_hbm, ds_hbm)
        bufs_in = (qbuf, dobuf, ltbuf, dsbuf, kbuf, vbuf)

        def in_copies(item, sl):
            r, lg = item // nlg, item % nlg
            cps = []
            for a in range(nparts):
                for t, (src, col, pad) in enumerate(ins):
                    cps.append(pltpu.make_async_copy(
                        _stream_ref(hbm_in[src], d, r, a, pl.multiple_of(col + lw * lg, LANES), lw),
                        bufs_in[t].at[sl, a, pl.ds(pad, rows), :], in_sems.at[sl, n_in * a + t]))
            return cps

        def out_copies(item, sl):
            r, lg = item // nlg, item % nlg
            cps = []
            for a in range(nparts):
                for t, (buf, dst, pad) in enumerate(((dqbuf, dq_hbm, 0), (dkbuf, dk_hbm, qr), (dvbuf, dv_hbm, qr))):
                    cps.append(pltpu.make_async_copy(
                        buf.at[sl, a, pl.ds(pad, rows), :],
                        _stream_ref(dst, d, r, a, pl.multiple_of(lw * lg, LANES), lw), out_sems.at[sl, 3 * a + t]))
            return cps

        @pl.when(i == 0)
        def _():
            for sl in range(2):
                for a in range(nparts):
                    kbuf[sl, a, 0:qr, :] = jnp.zeros((qr, lw), f32)
                    vbuf[sl, a, 0:qr, :] = jnp.zeros((qr, lw), f32)
            for cp in in_copies(0, 0):
                cp.start()

        @pl.when(i + 1 < nitems)
        def _():
            for cp in in_copies(i + 1, 1 - slot):
                cp.start()

        for cp in in_copies(i, slot):
            cp.wait()

        @pl.when(i >= 2)
        def _():
            for cp in out_copies(i - 2, slot):
                cp.wait()

        for a in range(nparts):
            dkbuf[slot, a] = jnp.zeros((qr + rows, lw), f32)
            dvbuf[slot, a] = jnp.zeros((qr + rows, lw), f32)
        band, is_prev = _key_geometry(nparts)
        lo = lax.broadcasted_iota(jnp.int32, (QBLK, LANES), 1) < HEAD_DIM

        def blk(c, carry):
            r0 = pl.multiple_of(c * qr, qr)
            mask = band & (is_prev * jnp.where(c == 0, 1, 0) == 0)

            def rows_of(buf, n, lanes):
                return jnp.concatenate([buf[slot, a, pl.ds(r0, n), lanes] for a in range(nparts)], axis=0)

            for pp in range(lw // LANES):
                lanes = slice(pp * LANES, (pp + 1) * LANES)
                dq, dk, dv = _attn_block_bwd(
                    rows_of(qbuf, qr, lanes).astype(bf16), rows_of(dobuf, qr, lanes).astype(bf16),
                    rows_of(kbuf, 2 * qr, lanes).astype(bf16), rows_of(vbuf, 2 * qr, lanes).astype(bf16),
                    rows_of(ltbuf, qr, lanes), rows_of(dsbuf, qr, lanes), mask, lo)
                for a in range(nparts):
                    dqbuf[slot, a, pl.ds(r0, qr), lanes] = dq[a * qr:(a + 1) * qr]
                    dkbuf[slot, a, pl.ds(r0, 2 * qr), lanes] += dk[2 * a * qr:2 * (a + 1) * qr]
                    dvbuf[slot, a, pl.ds(r0, 2 * qr), lanes] += dv[2 * a * qr:2 * (a + 1) * qr]
            return carry

        lax.fori_loop(0, nb, blk, 0)

        for cp in out_copies(i, slot):
            cp.start()

        @pl.when(i == nitems - 1)
        def _():
            for cp in out_copies(i - 1, 1 - slot) + out_copies(i, slot):
                cp.wait()

    vshape = (S // 8, 8, A_WIDTH) if d == 4 else (S // 16, 2, 8, A_WIDTH)
    plain = pltpu.VMEM((2, nparts, rows, lw), f32)
    padded = pltpu.VMEM((2, nparts, qr + rows, lw), f32)
    outs = pl.pallas_call(
        body, name=name, grid=(nitems,),
        in_specs=[_HBM_ANY] * 5, out_specs=[_HBM_ANY] * 3,
        out_shape=[jax.ShapeDtypeStruct(vshape, f32)] * 3,
        scratch_shapes=[plain, plain, plain, plain, padded, padded, plain, padded, padded,
                        pltpu.SemaphoreType.DMA((2, n_in * nparts)), pltpu.SemaphoreType.DMA((2, 3 * nparts))],
        compiler_params=_cp(("arbitrary",)),
    )(*[_stream_view(a, d) for a in (qk, proj, do_a, lt, dsum)])
    return [o.reshape(S, A_WIDTH) for o in outs]


def _prev_halo(tm, h, col):
    return pl.BlockSpec((h, 512), lambda i: (jnp.maximum(i * (tm // h) - 1, 0), col))


def _next_halo(tm, h, col, S):
    return pl.BlockSpec((h, 512), lambda i: (jnp.minimum((i + 1) * (tm // h), S // h - 1), col))


def _mix0_fwd(o_g, lse_g, proj, conv_w):
    S = proj.shape[0]
    tm = _tile(S, 256)

    def body(o0, o1, o2, l0, l1, l2, bg_ref, cg_ref, hb_ref, z_ref, cgh_ref, hbh_ref, w_ref,
             u_ref, oa_ref, lt_ref, tbuf):
        i = pl.program_id(0)
        ls = [l0[...], l1[...], l2[...]]
        mx = jnp.maximum(jnp.maximum(ls[0], ls[1]), ls[2])
        es = [jnp.exp(l - mx) for l in ls]
        tot = es[0] + es[1] + es[2]
        lt_ref[...] = mx + jnp.log(tot)
        inv = 1.0 / tot
        z = z_ref[...]
        sz = z * _sigmoid(z)
        oa = (es[0] * inv) * o0[...] + (es[1] * inv) * o1[...] + (es[2] * inv) * o2[...]
        oa_ref[...] = oa
        u_ref[:, :A_WIDTH] = (oa * sz[:, :A_WIDTH]).astype(bf16)
        t = cg_ref[...] * hb_ref[...]
        tbuf[0:8, :] = jnp.where(i > 0, cgh_ref[...] * hbh_ref[...], 0.0)
        tbuf[8:, :] = t
        cv = w_ref[2:3, :] * t + w_ref[1:2, :] * tbuf[pl.ds(7, tm), :] + w_ref[0:1, :] * tbuf[pl.ds(6, tm), :]
        u_ref[:, A_WIDTH:] = (bg_ref[...] * cv * sz[:, A_WIDTH:]).astype(bf16)

    row = lambda w, c: pl.BlockSpec((tm, w), lambda i: (i, c))
    return pl.pallas_call(
        body, name="mix0_fwd", grid=(S // tm,),
        in_specs=[row(512, 0)] * 6
        + [row(512, E_BG // 512), row(512, E_CG // 512), row(512, E_HB // 512), row(1024, E_Z // 1024),
           _prev_halo(tm, 8, E_CG // 512), _prev_halo(tm, 8, E_HB // 512), pl.BlockSpec((SC_WIDTH, 512), lambda i: (0, 0))],
        out_specs=[row(1024, 0), row(512, 0), row(512, 0)],
        out_shape=[jax.ShapeDtypeStruct((S, D_MODEL), bf16), jax.ShapeDtypeStruct((S, A_WIDTH), f32),
                   jax.ShapeDtypeStruct((S, A_WIDTH), f32)],
        scratch_shapes=[pltpu.VMEM((tm + 8, 512), f32)],
        compiler_params=_cp(("parallel",)),
    )(*o_g, *lse_g, proj, proj, proj, proj, proj, proj, conv_w)


def _dsilu(z, sg):
    return sg * (1.0 + z * (1.0 - sg))


def _mix0_bwd_a(du, proj, o_a, conv_w):
    S = proj.shape[0]
    tm = _tile(S, 256)

    def body(du_ref, bg_ref, cg_ref, hb_ref, z_ref, cgh_ref, hbh_ref, oa_ref, w_ref,
             dz_ref, doa_ref, ds_ref, dbg_ref, dcv_ref, tbuf):
        i = pl.program_id(0)
        lo = lax.broadcasted_iota(jnp.int32, (tm, LANES), 1) < HEAD_DIM
        z = z_ref[...]
        sg = _sigmoid(z)
        sz = z * sg
        dsz = _dsilu(z, sg)
        du_v = du_ref[...]
        t = cg_ref[...] * hb_ref[...]
        tbuf[0:8, :] = jnp.where(i > 0, cgh_ref[...] * hbh_ref[...], 0.0)
        tbuf[8:, :] = t
        cv = w_ref[2:3, :] * t + w_ref[1:2, :] * tbuf[pl.ds(7, tm), :] + w_ref[0:1, :] * tbuf[pl.ds(6, tm), :]
        bg = bg_ref[...]
        oa = oa_ref[...]
        dz_ref[:, :A_WIDTH] = (du_v[:, :A_WIDTH] * oa * dsz[:, :A_WIDTH]).astype(bf16)
        dz_ref[:, A_WIDTH:] = (du_v[:, A_WIDTH:] * (bg * cv) * dsz[:, A_WIDTH:]).astype(bf16)
        doa = du_v[:, :A_WIDTH] * sz[:, :A_WIDTH]
        dyb = du_v[:, A_WIDTH:] * sz[:, A_WIDTH:]
        doa_ref[...] = doa
        dbg_ref[...] = (dyb * cv).astype(bf16)
        dcv_ref[...] = dyb * bg
        prod = doa * oa
        for p in range(4):
            pp = prod[:, p * LANES:(p + 1) * LANES]
            sa = jnp.sum(jnp.where(lo, pp, 0.0), axis=-1, keepdims=True)
            sb = jnp.sum(jnp.where(lo, 0.0, pp), axis=-1, keepdims=True)
            ds_ref[:, p * LANES:(p + 1) * LANES] = jnp.where(lo, sa, sb)

    row = lambda w, c: pl.BlockSpec((tm, w), lambda i: (i, c))
    return pl.pallas_call(
        body, name="mix0_bwd_a", grid=(S // tm,),
        in_specs=[row(1024, 0), row(512, E_BG // 512), row(512, E_CG // 512), row(512, E_HB // 512), row(1024, E_Z // 1024),
                  _prev_halo(tm, 8, E_CG // 512), _prev_halo(tm, 8, E_HB // 512), row(512, 0),
                  pl.BlockSpec((SC_WIDTH, 512), lambda i: (0, 0))],
        out_specs=[row(1024, 0), row(512, 0), row(512, 0), row(512, 0), row(512, 0)],
        out_shape=[jax.ShapeDtypeStruct((S, D_MODEL), bf16), jax.ShapeDtypeStruct((S, A_WIDTH), f32),
                   jax.ShapeDtypeStruct((S, A_WIDTH), f32), jax.ShapeDtypeStruct((S, 512), bf16),
                   jax.ShapeDtypeStruct((S, 512), f32)],
        scratch_shapes=[pltpu.VMEM((tm + 8, 512), f32)],
        compiler_params=_cp(("parallel",)),
    )(du, proj, proj, proj, proj, proj, proj, o_a, conv_w)


def _mix0_bwd_b(dcv, proj, conv_w):
    S = proj.shape[0]
    tm = _tile(S, 256)
    nt = S // tm

    def body(dcv_ref, dcvn_ref, cg_ref, hb_ref, cgh_ref, hbh_ref, w_ref, dcg_ref, dhb_ref, gw_ref, tbuf, dbuf):
        i = pl.program_id(0)
        cg = cg_ref[...]
        hb = hb_ref[...]
        t = cg * hb
        tbuf[0:8, :] = jnp.where(i > 0, cgh_ref[...] * hbh_ref[...], 0.0)
        tbuf[8:, :] = t
        dcv_v = dcv_ref[...]
        dbuf[0:tm, :] = dcv_v
        dbuf[tm:, :] = jnp.where(i < nt - 1, dcvn_ref[...], 0.0)
        dt = w_ref[2:3, :] * dcv_v + w_ref[1:2, :] * dbuf[pl.ds(1, tm), :] + w_ref[0:1, :] * dbuf[pl.ds(2, tm), :]
        dcg_ref[...] = (dt * hb).astype(bf16)
        dhb_ref[...] = (dt * cg).astype(bf16)
        g2 = jnp.sum(dcv_v * t, axis=0, keepdims=True)
        g1 = jnp.sum(dcv_v * tbuf[pl.ds(7, tm), :], axis=0, keepdims=True)
        g0 = jnp.sum(dcv_v * tbuf[pl.ds(6, tm), :], axis=0, keepdims=True)
        part = jnp.concatenate([g0, g1, g2, jnp.zeros((5, 512), f32)], axis=0)

        @pl.when(i == 0)
        def _():
            gw_ref[...] = part

        @pl.when(i > 0)
        def _():
            gw_ref[...] += part

    row = lambda w, c: pl.BlockSpec((tm, w), lambda i: (i, c))
    return pl.pallas_call(
        body, name="mix0_bwd_b", grid=(nt,),
        in_specs=[row(512, 0), _next_halo(tm, 8, 0, S), row(512, E_CG // 512), row(512, E_HB // 512),
                  _prev_halo(tm, 8, E_CG // 512), _prev_halo(tm, 8, E_HB // 512),
                  pl.BlockSpec((SC_WIDTH, 512), lambda i: (0, 0))],
        out_specs=[row(512, 0), row(512, 0), pl.BlockSpec((8, 512), lambda i: (0, 0))],
        out_shape=[jax.ShapeDtypeStruct((S, 512), bf16), jax.ShapeDtypeStruct((S, 512), bf16),
                   jax.ShapeDtypeStruct((8, 512), f32)],
        scratch_shapes=[pltpu.VMEM((tm + 8, 512), f32), pltpu.VMEM((tm + 8, 512), f32)],
        compiler_params=_cp(("arbitrary",)),
    )(dcv, dcv, proj, proj, proj, proj, conv_w)


def _qk_bwd(dq_g, dk_g, dv_g, proj, tabs, nw, hm, dbg, dcg, dhb, dz):
    S = proj.shape[0]
    tm = _tile(S, 256)

    def body(*refs):
        d_refs = refs[0:6]
        dv_refs = refs[6:9]
        x_ref, c_ref, s1_ref, s2_ref, nw_ref, m_ref, dbg_ref, dcg_ref, dhb_ref, dz_ref, o_ref, gw_ref = refs[9:]
        i = pl.program_id(0)
        c, s1, s2, m = c_ref[...], s1_ref[...], s2_ref[...], m_ref[...]
        accs = []
        for kind in range(2):
            w = nw_ref[kind:kind + 1, :]
            acc = jnp.zeros((1, LANES), f32)
            for gi in range(N_GROUPS):
                for p in range(4):
                    col = kind * 1536 + gi * 512 + p * LANES
                    dout = d_refs[kind * 3 + gi][:, p * LANES:(p + 1) * LANES]
                    t = x_ref[:, col:col + LANES]
                    dthat = (dout * c + pltpu.roll(dout * s1, LANES - ROT_HALF, axis=1)
                             + pltpu.roll(dout * s2, ROT_HALF, axis=1))
                    r = lax.rsqrt(_head_mean(t * t, m) + EPS)
                    tn = t * r
                    acc = acc + jnp.sum(dthat * tn, axis=0, keepdims=True)
                    dtn = dthat * w
                    o_ref[:, col:col + LANES] = (r * (dtn - tn * _head_mean(dtn * tn, m))).astype(bf16)
            accs.append(acc + pltpu.roll(acc, HEAD_DIM, axis=1))
        for gi in range(N_GROUPS):
            o_ref[:, E_V + gi * 512:E_V + (gi + 1) * 512] = dv_refs[gi][...].astype(bf16)
        o_ref[:, E_BG:E_CG] = dbg_ref[...]
        o_ref[:, E_CG:E_HB] = dcg_ref[...]
        o_ref[:, E_HB:E_Z] = dhb_ref[...]
        o_ref[:, E_Z:] = dz_ref[...]
        part = jnp.concatenate([accs[0], accs[1], jnp.zeros((6, LANES), f32)], axis=0)

        @pl.when(i == 0)
        def _():
            gw_ref[...] = part

        @pl.when(i > 0)
        def _():
            gw_ref[...] += part

    row = lambda w, c: pl.BlockSpec((tm, w), lambda i: (i, c))
    tab = row(LANES, 0)
    return pl.pallas_call(
        body, name="qk_bwd", grid=(S // tm,),
        in_specs=[row(512, 0)] * 9 + [row(3072, 0), tab, tab, tab, pl.BlockSpec((2, LANES), lambda i: (0, 0)),
                                      pl.BlockSpec((LANES, LANES), lambda i: (0, 0)),
                                      row(512, 0), row(512, 0), row(512, 0), row(1024, 0)],
        out_specs=[row(EVEN_IN, 0), pl.BlockSpec((8, LANES), lambda i: (0, 0))],
        out_shape=[jax.ShapeDtypeStruct((S, EVEN_IN), bf16), jax.ShapeDtypeStruct((8, LANES), f32)],
        compiler_params=_cp(("arbitrary",)),
    )(*dq_g, *dk_g, *dv_g, proj, *tabs, nw, hm, dbg, dcg, dhb, dz)


def _inv_count(i, tm, p):
    rowg = lax.broadcasted_iota(jnp.int32, (tm, 1), 0) + i * tm
    return 1.0 / jnp.minimum(rowg + 1, p).astype(f32)


def _layer_norm_stats(c):
    mu = jnp.mean(c, axis=-1, keepdims=True)
    cen = c - mu
    rstd = lax.rsqrt(jnp.mean(cen * cen, axis=-1, keepdims=True) + EPS)
    return cen * rstd, rstd


def _fill_pool_buf(i, ubuf, uc_ref, uch_ref):
    ubuf[0:16, :] = jnp.where(i > 0, uch_ref[...], 0.0)
    ubuf[16:, :] = uc_ref[...]


def _pooled(i, tm, ubuf, gi):
    p = POOL_SIZES[gi]
    cols = slice(gi * LANES, (gi + 1) * LANES)
    acc = ubuf[pl.ds(16, tm), cols]
    cur = acc
    for jj in range(1, p):
        acc = acc + ubuf[pl.ds(16 - jj, tm), cols]
    return acc * _inv_count(i, tm, p) - cur


def _fill_glu_buf(i, gbuf, da_ref, dg_ref, dah_ref, dgh_ref):
    gbuf[0:32, :] = jnp.where(i > 0, dah_ref[...] * _sigmoid(dgh_ref[...]), 0.0)
    gbuf[32:, :] = da_ref[...] * _sigmoid(dg_ref[...])


def _shift_copies(buf, sh, tm):
    for b in range(1, 8):
        sh[b - 1] = buf[pl.ds(b, tm + 24), :]


def _window(buf, sh, off, tm):
    b = off % 8
    if b == 0:
        return buf[pl.ds(off, tm), :]
    return sh[b - 1, pl.ds(off - b, tm), :]


def _mix1_fwd(proj, pool_w, pool_scale, dconv_w, dconv_b, ln_w, ln_b):
    S = proj.shape[0]
    tm = _tile(S, 256)

    def body(uc_ref, uch_ref, da_ref, dg_ref, dah_ref, dgh_ref, za_ref, zb_ref, pw_ref, ps_ref, cw_ref, cb_ref,
             lw_ref, lb_ref, u_ref, c_ref, mc_ref, ubuf, gbuf, gsh):
        i = pl.program_id(0)
        _fill_pool_buf(i, ubuf, uc_ref, uch_ref)
        za = za_ref[...]
        for gi in range(4):
            cols = slice(gi * LANES, (gi + 1) * LANES)
            mc = jnp.dot(_pooled(i, tm, ubuf, gi).astype(bf16), pw_ref[gi], preferred_element_type=f32)
            mc_ref[:, cols] = mc
            zg = za[:, cols]
            u_ref[:, cols] = (mc * ps_ref[:, cols] * (zg * _sigmoid(zg))).astype(bf16)
        _fill_glu_buf(i, gbuf, da_ref, dg_ref, dah_ref, dgh_ref)
        _shift_copies(gbuf, gsh, tm)
        c = jnp.zeros((tm, 512), f32) + cb_ref[...]
        for k in range(D_CONV):
            c = c + cw_ref[k:k + 1, :] * _window(gbuf, gsh, 32 - (D_CONV - 1) + k, tm)
        c_ref[...] = c
        yhat, _ = _layer_norm_stats(c)
        l = yhat * lw_ref[...] + lb_ref[...]
        zb = zb_ref[...]
        u_ref[:, 512:] = (l * _sigmoid(l) * (zb * _sigmoid(zb))).astype(bf16)

    row = lambda w, c: pl.BlockSpec((tm, w), lambda i: (i, c))
    vec = pl.BlockSpec((1, 512), lambda i: (0, 0))
    return pl.pallas_call(
        body, name="mix1_fwd", grid=(S // tm,),
        in_specs=[row(512, 0), _prev_halo(tm, 16, 0), row(512, 1), row(512, 2), _prev_halo(tm, 32, 1), _prev_halo(tm, 32, 2),
                  row(512, 3), row(512, 4), pl.BlockSpec((4, LANES, LANES), lambda i: (0, 0, 0)), vec,
                  pl.BlockSpec((D_CONV, 512), lambda i: (0, 0)), vec, vec, vec],
        out_specs=[row(1024, 0), row(512, 0), row(512, 0)],
        out_shape=[jax.ShapeDtypeStruct((S, D_MODEL), bf16), jax.ShapeDtypeStruct((S, 512), f32),
                   jax.ShapeDtypeStruct((S, 512), f32)],
        scratch_shapes=[pltpu.VMEM((tm + 16, 512), f32), pltpu.VMEM((tm + 32, 512), f32),
                        pltpu.VMEM((7, tm + 24, 512), f32)],
        compiler_params=_cp(("parallel",)),
    )(proj, proj, proj, proj, proj, proj, proj, proj, pool_w, pool_scale, dconv_w, dconv_b, ln_w, ln_b)


def _mix1_bwd_a(du, proj, c, mc, pool_w, pool_scale, ln_w, ln_b):
    S = proj.shape[0]
    tm = _tile(S, 256)

    def body(du_ref, za_ref, zb_ref, c_ref, mc_ref, pw_ref, ps_ref, lw_ref, lb_ref,
             dz_ref, dc_ref, dpl_ref, dmc_ref, acc_ref):
        i = pl.program_id(0)
        du_v = du_ref[...]
        ps = ps_ref[...]
        za = za_ref[...]
        sga = _sigmoid(za)
        mcv = mc_ref[...]
        dz_ref[:, :512] = (du_v[:, :512] * (mcv * ps) * _dsilu(za, sga)).astype(bf16)
        dyc = du_v[:, :512] * (za * sga)
        g_ps = jnp.sum(dyc * mcv, axis=0, keepdims=True)
        dmc = (dyc * ps).astype(bf16)
        dmc_ref[...] = dmc
        for gi in range(4):
            cols = slice(gi * LANES, (gi + 1) * LANES)
            dpl_ref[:, cols] = lax.dot_general(dmc[:, cols], pw_ref[gi], (((1,), (1,)), ((), ())), preferred_element_type=f32)
        yhat, rstd = _layer_norm_stats(c_ref[...])
        lw = lw_ref[...]
        l = yhat * lw + lb_ref[...]
        sgl = _sigmoid(l)
        zb = zb_ref[...]
        sgb = _sigmoid(zb)
        dz_ref[:, 512:] = (du_v[:, 512:] * (l * sgl) * _dsilu(zb, sgb)).astype(bf16)
        dl = du_v[:, 512:] * (zb * sgb) * _dsilu(l, sgl)
        g_lb = jnp.sum(dl, axis=0, keepdims=True)
        g_lw = jnp.sum(dl * yhat, axis=0, keepdims=True)
        dyh = dl * lw
        dc = rstd * (dyh - jnp.mean(dyh, axis=-1, keepdims=True) - yhat * jnp.mean(dyh * yhat, axis=-1, keepdims=True))
        dc_ref[...] = dc
        g_db = jnp.sum(dc, axis=0, keepdims=True)
        part = jnp.concatenate([g_ps, g_lw, g_lb, g_db, jnp.zeros((4, 512), f32)], axis=0)

        @pl.when(i == 0)
        def _():
            acc_ref[...] = part

        @pl.when(i > 0)
        def _():
            acc_ref[...] += part

    row = lambda w, c_: pl.BlockSpec((tm, w), lambda i: (i, c_))
    vec = pl.BlockSpec((1, 512), lambda i: (0, 0))
    return pl.pallas_call(
        body, name="mix1_bwd_a", grid=(S // tm,),
        in_specs=[row(1024, 0), row(512, 3), row(512, 4), row(512, 0), row(512, 0),
                  pl.BlockSpec((4, LANES, LANES), lambda i: (0, 0, 0)), vec, vec, vec],
        out_specs=[row(1024, 0), row(512, 0), row(512, 0), row(512, 0), pl.BlockSpec((8, 512), lambda i: (0, 0))],
        out_shape=[jax.ShapeDtypeStruct((S, D_MODEL), bf16), jax.ShapeDtypeStruct((S, 512), f32),
                   jax.ShapeDtypeStruct((S, 512), f32), jax.ShapeDtypeStruct((S, 512), bf16),
                   jax.ShapeDtypeStruct((8, 512), f32)],
        compiler_params=_cp(("arbitrary",)),
    )(du, proj, proj, c, mc, pool_w, pool_scale, ln_w, ln_b)


def _mix1_bwd_b(dc, dpl, dmc, dz, proj, dconv_w):
    S = proj.shape[0]
    tm = _tile(S, 256)
    nt = S // tm

    def body(dc_ref, dcn_ref, dpl_ref, dpn_ref, dmc_ref, dz_ref, uc_ref, uch_ref, da_ref, dg_ref,
             cw_ref, o_ref, gcw_ref, gpw_ref, ubuf, dcbuf, dpbuf, dcsh):
        i = pl.program_id(0)
        last = i == nt - 1
        _fill_pool_buf(i, ubuf, uc_ref, uch_ref)
        dcbuf[0:tm, :] = dc_ref[...]
        dcbuf[tm:, :] = jnp.where(last, 0.0, dcn_ref[...])
        _shift_copies(dcbuf, dcsh, tm)
        dpl_v = dpl_ref[...]
        for gi in range(4):
            p = POOL_SIZES[gi]
            cols = slice(gi * LANES, (gi + 1) * LANES)
            dpbuf[0:tm, cols] = dpl_v[:, cols] * _inv_count(i, tm, p)
            dpbuf[tm:, cols] = jnp.where(last, 0.0, dpn_ref[:, cols] * (1.0 / p))
        gpw = []
        for gi in range(4):
            p = POOL_SIZES[gi]
            cols = slice(gi * LANES, (gi + 1) * LANES)
            acc = -dpl_v[:, cols]
            for jj in range(p):
                acc = acc + dpbuf[pl.ds(jj, tm), cols]
            o_ref[:, cols] = acc.astype(bf16)
            pooled = _pooled(i, tm, ubuf, gi).astype(bf16)
            gpw.append(lax.dot_general(pooled, dmc_ref[:, cols], (((0,), (0,)), ((), ())), preferred_element_type=f32))
        da = da_ref[...]
        sg = _sigmoid(dg_ref[...])
        gl = da * sg
        dgl = jnp.zeros((tm, 512), f32)
        gcw = []
        for k in range(D_CONV):
            win = _window(dcbuf, dcsh, D_CONV - 1 - k, tm)
            dgl = dgl + cw_ref[k:k + 1, :] * win
            gcw.append(jnp.sum(gl * win, axis=0, keepdims=True))
        gcw.append(jnp.zeros((1, 512), f32))
        o_ref[:, O_DA:O_DG] = (dgl * sg).astype(bf16)
        o_ref[:, O_DG:O_Z] = (dgl * da * sg * (1.0 - sg)).astype(bf16)
        o_ref[:, O_Z:] = dz_ref[...]
        gcw_part = jnp.concatenate(gcw, axis=0)

        @pl.when(i == 0)
        def _():
            gcw_ref[...] = gcw_part
            for gi in range(4):
                gpw_ref[gi] = gpw[gi]

        @pl.when(i > 0)
        def _():
            gcw_ref[...] += gcw_part
            for gi in range(4):
                gpw_ref[gi] += gpw[gi]

    row = lambda w, c_: pl.BlockSpec((tm, w), lambda i: (i, c_))
    return pl.pallas_call(
        body, name="mix1_bwd_b", grid=(nt,),
        in_specs=[row(512, 0), _next_halo(tm, 32, 0, S), row(512, 0), _next_halo(tm, 16, 0, S), row(512, 0), row(1024, 0),
                  row(512, 0), _prev_halo(tm, 16, 0), row(512, 1), row(512, 2),
                  pl.BlockSpec((D_CONV, 512), lambda i: (0, 0))],
        out_specs=[row(ODD_IN, 0), pl.BlockSpec((32, 512), lambda i: (0, 0)),
                   pl.BlockSpec((4, LANES, LANES), lambda i: (0, 0, 0))],
        out_shape=[jax.ShapeDtypeStruct((S, ODD_IN), bf16), jax.ShapeDtypeStruct((32, 512), f32),
                   jax.ShapeDtypeStruct((4, LANES, LANES), f32)],
        scratch_shapes=[pltpu.VMEM((tm + 16, 512), f32), pltpu.VMEM((tm + 32, 512), f32),
                        pltpu.VMEM((tm + 16, 512), f32), pltpu.VMEM((7, tm + 24, 512), f32)],
        compiler_params=_cp(("arbitrary",)),
    )(dc, dc, dpl, dpl, dmc, dz, proj, proj, proj, proj, dconv_w)


_SMALL_LATE = ["e_q_norm_w", "e_k_norm_w", "e_conv_w", "o_norm_w", "o_pool_w", "o_pool_scale", "o_dconv_w", "o_dconv_b",
               "o_ln_w", "o_ln_b"]


def _local_step(x, pos_col, target, w, dist=None):
    hm = _head_mean_matrix()
    nw = jnp.concatenate([jnp.tile(w["e_q_norm_w"], (1, 2)), jnp.tile(w["e_k_norm_w"], (1, 2))], axis=0)
    tabs = _rope_tables(pos_col)
    pool_wb = w["o_pool_w"].astype(bf16)
    e_norm_w, e_w_in = w["e_norm_w"], w["e_w_in"]

    h0 = _rms_fwd(x, e_norm_w, name="rms0_fwd")
    if dist is None:
        proj0 = _mm_nn(h0, e_w_in, name="in_proj0")
    else:
        proj0, gathered = _mm_nn(h0, e_w_in, name="in_proj0", fuse=([], dist[0]))
        w = {**w, **dist[1](gathered)}
    e_conv_w, e_w_out, o_norm_w, o_w_in, o_w_out = w["e_conv_w"], w["e_w_out"], w["o_norm_w"], w["o_w_in"], w["o_w_out"]
    o_pool_scale, o_dconv_w, o_dconv_b, o_ln_w, o_ln_b = (w[k] for k in ("o_pool_scale", "o_dconv_w", "o_dconv_b", "o_ln_w", "o_ln_b"))
    qk = _qk_prep(proj0, tabs, nw, hm)
    o_g, lse_g = [], []
    for g in range(N_GROUPS):
        o, l = _attn_fwd_local(qk, proj0) if g == 0 else _attn_fwd_dil(qk, proj0, g, name=f"attn_fwd{g}")
        o_g.append(o)
        lse_g.append(l)
    u0, o_a, lt = _mix0_fwd(o_g, lse_g, proj0, e_conv_w)
    x1 = _mm_nn(u0, e_w_out, res=x, name="out_proj0")
    h1 = _rms_fwd(x1, o_norm_w, name="rms1_fwd")
    proj1 = _mm_nn(h1, o_w_in, name="in_proj1", tn=512)
    u1, c1, mc1 = _mix1_fwd(proj1, pool_wb, o_pool_scale, o_dconv_w, o_dconv_b, o_ln_w, o_ln_b)
    dy, dyb, loss = _mm_out_loss(u1, o_w_out, x1, target, name="out_proj1_loss")
    g_o_w_out = _mm_tn(u1, dyb, name="g_w_out1", out_dtype=bf16)
    du1 = _mm_nt(dyb, o_w_out, name="d_u1")
    dz1, dc1, dpl1, dmc1, sums1 = _mix1_bwd_a(du1, proj1, c1, mc1, pool_wb, o_pool_scale, o_ln_w, o_ln_b)
    dproj1, g_dconv_w, g_pool_w = _mix1_bwd_b(dc1, dpl1, dmc1, dz1, proj1, o_dconv_w)
    g_o_w_in = _mm_tn(h1, dproj1, name="g_w_in1", out_dtype=bf16)
    dh1 = _mm_nt(dproj1, o_w_in, name="d_h1")
    d1, d1b, g_o_norm = _rms_bwd(x1, o_norm_w, dh1, dy, name="rms1_bwd")
    g_e_w_out = _mm_tn(u0, d1b, name="g_w_out0", out_dtype=bf16)
    du0 = _mm_nt(d1b, e_w_out, name="d_u0")
    dz0, do_a, dsum, dbg, dcv = _mix0_bwd_a(du0, proj0, o_a, e_conv_w)
    dcg, dhb, g_conv_w = _mix0_bwd_b(dcv, proj0, e_conv_w)
    fuse_a = None if dist is None else (
        [g_e_w_out.reshape(N_DEV, D_MODEL // N_DEV, D_MODEL),
         jnp.moveaxis(g_o_w_in.reshape(D_MODEL, N_DEV, ODD_IN // N_DEV), 1, 0),
         g_o_w_out.reshape(N_DEV, D_MODEL // N_DEV, D_MODEL)], [])
    dq_g, dk_g, dv_g = [], [], []
    for g in range(N_GROUPS):
        if g == 0:
            dqkv = _attn_bwd_local(qk, proj0, do_a, lt, dsum, fuse=fuse_a)
            if dist is not None:
                dqkv, recv_a = dqkv
            dq, dk, dv = dqkv
        else:
            dq, dk, dv = _attn_bwd_dil(qk, proj0, do_a, lt, dsum, g, name=f"attn_bwd{g}")
        dq_g.append(dq)
        dk_g.append(dk)
        dv_g.append(dv)
    dproj0, g_qk_norm = _qk_bwd(dq_g, dk_g, dv_g, proj0, tabs, nw, hm, dbg, dcg, dhb, dz0)
    g_e_w_in = _mm_tn(h0, dproj0, name="g_w_in0", out_dtype=bf16, chunks=N_DEV)
    grads = dict(
        e_w_in=g_e_w_in, e_q_norm_w=g_qk_norm[0:1, :HEAD_DIM], e_k_norm_w=g_qk_norm[1:2, :HEAD_DIM],
        e_conv_w=g_conv_w[:SC_WIDTH], e_w_out=g_e_w_out,
        o_norm_w=g_o_norm, o_w_in=g_o_w_in, o_pool_w=g_pool_w,
        o_pool_scale=sums1[0:1], o_dconv_w=g_dconv_w[:D_CONV], o_dconv_b=sums1[3:4],
        o_ln_w=sums1[1:2], o_ln_b=sums1[2:3], o_w_out=g_o_w_out)
    if dist is None:
        dh0 = _mm_nt(dproj0, e_w_in, name="d_h0")
        grad_x, _, grads["e_norm_w"] = _rms_bwd(x, e_norm_w, dh0, d1, name="rms0_bwd")
        return loss, grad_x, grads
    small_late, offs = _pack_rows([grads[n_] for n_ in _SMALL_LATE])
    dh0, recv_b = _mm_nt(dproj0, e_w_in, name="d_h0", fuse=([g_e_w_in], [small_late]))
    grad_x, _, g_e_norm = _rms_bwd(x, e_norm_w, dh0, d1, name="rms0_bwd")
    recv_c = _exchange([], [g_e_norm.reshape(8, LANES)], name="exchange_e_norm")
    recv = dict(e_w_out=recv_a[0], o_w_in=recv_a[1], o_w_out=recv_a[2], e_w_in=recv_b[0], small_late=recv_b[1],
                e_norm_w=recv_c[0])
    return loss, grad_x, recv, {n_: (off, grads[n_].shape) for n_, off in zip(_SMALL_LATE, offs)}


_MESH_ID = pl.DeviceIdType.MESH
_HBM = pl.BlockSpec(memory_space=pl.ANY)


def _all_gather(arrs, *, name):
    n = len(arrs)

    def body(*refs):
        ins, outs = refs[:n], refs[n:2 * n]
        send_sems, recv_sems, local_sems = refs[2 * n:]
        x, y, c = _place()
        me, sibling = (x, y, c), (x, y, 1 - c)
        chips = [(1 - x, y), (x, 1 - y), (1 - x, 1 - y)]

        def slot(t, px, py, pc):
            return outs[t].at[4 * px + 2 * py + pc]

        def copy(t, k, block, to, src=None):
            dst = slot(t, *block)
            return pltpu.make_async_remote_copy(
                src_ref=dst if src is None else src, dst_ref=dst,
                send_sem=send_sems.at[7 * t + k], recv_sem=recv_sems.at[7 * t + k],
                device_id=to, device_id_type=_MESH_ID)

        mine = [pltpu.make_async_copy(ins[t], slot(t, *me), local_sems.at[t]) for t in range(n)]
        for cp in mine:
            cp.start()
        first = []
        for t in range(n):
            first.append(copy(t, 0, me, sibling, src=ins[t]))
            first += [copy(t, 1 + j, me, (*chip, c), src=ins[t]) for j, chip in enumerate(chips)]
        for cp in first:
            cp.start()
        passed = []
        for j, chip in enumerate(chips):
            for t in range(n):
                copy(t, 1 + j, (*chip, c), me).wait_recv()
                fwd = copy(t, 4 + j, (*chip, c), sibling)
                fwd.start()
                passed.append(fwd)
        for t in range(n):
            copy(t, 0, sibling, me).wait_recv()
            for j, chip in enumerate(chips):
                copy(t, 4 + j, (*chip, 1 - c), me).wait_recv()
        for cp in first + passed:
            cp.wait_send()
        for cp in mine:
            cp.wait()

    return pl.pallas_call(
        body, name=name,
        in_specs=[_HBM] * n, out_specs=[_HBM] * n,
        out_shape=[jax.ShapeDtypeStruct((N_DEV, *a.shape), a.dtype) for a in arrs],
        scratch_shapes=[pltpu.SemaphoreType.DMA((7 * n,)), pltpu.SemaphoreType.DMA((7 * n,)),
                        pltpu.SemaphoreType.DMA((n,))],
    )(*arrs)


def _exchange(chunked, whole, *, name):
    arrs = list(chunked) + list(whole)
    n = len(arrs)

    def body(*refs):
        start, wait = _exchange_plan(refs[:n], refs[n:2 * n], *refs[2 * n:], len(chunked))
        start()
        wait()

    return pl.pallas_call(
        body, name=name, in_specs=[_HBM] * n, out_specs=[_HBM] * n,
        out_shape=_exchange_out_shapes(chunked, whole), scratch_shapes=_exchange_sems(n),
    )(*arrs)


def _adamw(w, g, m, v):
    m2 = ADAM_B1 * m + (1.0 - ADAM_B1) * g
    v2 = ADAM_B2 * v + (1.0 - ADAM_B2) * (g * g)
    m_hat = m2 / (1.0 - ADAM_B1 ** ADAM_STEP)
    v_hat = v2 / (1.0 - ADAM_B2 ** ADAM_STEP)
    delta = -ADAM_LR * (m_hat / (jnp.sqrt(v_hat) + ADAM_EPS) + ADAM_WD * w)
    return delta, m2, v2


def _sum_adamw(parts, w, m, v, *, name):
    R, C = w.shape
    tr = _tile(R, 256)

    def body(p_ref, w_ref, m_ref, v_ref, g_ref, d_ref, nm_ref, nv_ref):
        g = p_ref[0].astype(f32)
        for i in range(1, N_DEV):
            g = g + p_ref[i].astype(f32)
        g_ref[...] = g
        d_ref[...], nm_ref[...], nv_ref[...] = _adamw(w_ref[...], g, m_ref[...], v_ref[...])

    spec = pl.BlockSpec((tr, C), lambda i: (i, 0))
    return pl.pallas_call(
        body, name=name, grid=(R // tr,),
        in_specs=[pl.BlockSpec((N_DEV, tr, C), lambda i: (0, i, 0)), spec, spec, spec],
        out_specs=[spec] * 4, out_shape=[jax.ShapeDtypeStruct((R, C), f32)] * 4,
        compiler_params=_cp(("parallel",)),
    )(parts, w, m, v)


def _sum_parts(parts, *, name):
    _, R, C = parts.shape

    def body(p_ref, o_ref):
        g = p_ref[0]
        for i in range(1, N_DEV):
            g = g + p_ref[i]
        o_ref[...] = g

    return pl.pallas_call(body, name=name, out_shape=jax.ShapeDtypeStruct((R, C), f32),
                          compiler_params=pltpu.CompilerParams(vmem_limit_bytes=VMEM_LIMIT))(parts)


def _adamw_small(ws, gs, ms, vs):
    n = len(ws)

    def body(*refs):
        w_r, g_r, m_r, v_r = refs[:n], refs[n:2 * n], refs[2 * n:3 * n], refs[3 * n:4 * n]
        d_r, nm_r, nv_r = refs[4 * n:5 * n], refs[5 * n:6 * n], refs[6 * n:7 * n]
        for t in range(n):
            d_r[t][...], nm_r[t][...], nv_r[t][...] = _adamw(w_r[t][...], g_r[t][...], m_r[t][...], v_r[t][...])

    shapes = [jax.ShapeDtypeStruct(w.shape, f32) for w in ws]
    outs = pl.pallas_call(body, name="adamw_small", out_shape=shapes * 3)(*ws, *gs, *ms, *vs)
    return outs[:n], outs[n:2 * n], outs[2 * n:]


_WEIGHTS = ["e_norm_w", "e_w_in", "e_q_norm_w", "e_k_norm_w", "e_conv_w", "e_w_out", "o_norm_w", "o_w_in", "o_pool_w",
            "o_pool_scale", "o_dconv_w", "o_dconv_b", "o_ln_w", "o_ln_b", "o_w_out"]
_BIG = ["e_w_in", "e_w_out", "o_w_in", "o_w_out"]
_SMALL_SHARDED = ["e_conv_w", "o_norm_w", "o_pool_scale", "o_dconv_w", "o_dconv_b", "o_ln_w", "o_ln_b"]
_SMALL_ALL = ["e_norm_w", "e_q_norm_w", "e_k_norm_w", "e_conv_w", "o_norm_w", "o_pool_w", "o_pool_scale", "o_dconv_w",
              "o_dconv_b", "o_ln_w", "o_ln_b"]


def _pack_rows(pieces):
    rows, offs, r0 = [], [], 0
    for p in pieces:
        flat = p.reshape(-1)
        nr = -(-flat.shape[0] // (8 * LANES)) * 8
        rows.append(jnp.pad(flat, (0, nr * LANES - flat.shape[0])).reshape(nr, LANES))
        offs.append((r0, nr))
        r0 += nr
    return jnp.concatenate(rows, axis=0), offs


def _unpack_rows(buf, off, shape):
    r0, nr = off
    size = int(np.prod(shape))
    return buf[..., r0:r0 + nr, :].reshape(*buf.shape[:-2], nr * LANES)[..., :size].reshape(*buf.shape[:-2], *shape)


def kernel(x, positions, e_norm_w, e_w_in, e_q_norm_w, e_k_norm_w, e_conv_w, e_w_out, o_norm_w, o_w_in, o_pool_w, o_pool_scale, o_dconv_w, o_dconv_b, o_ln_w, o_ln_b, o_w_out, loss_target, m_e_norm_w, m_e_w_in, m_e_q_norm_w, m_e_k_norm_w, m_e_conv_w, m_e_w_out, m_o_norm_w, m_o_w_in, m_o_pool_w, m_o_pool_scale, m_o_dconv_w, m_o_dconv_b, m_o_ln_w, m_o_ln_b, m_o_w_out, v_e_norm_w, v_e_w_in, v_e_q_norm_w, v_e_k_norm_w, v_e_conv_w, v_e_w_out, v_o_norm_w, v_o_w_in, v_o_pool_w, v_o_pool_scale, v_o_dconv_w, v_o_dconv_b, v_o_ln_w, v_o_ln_b, v_o_w_out):
    w = dict(e_norm_w=e_norm_w, e_w_in=e_w_in, e_q_norm_w=e_q_norm_w, e_k_norm_w=e_k_norm_w, e_conv_w=e_conv_w,
             e_w_out=e_w_out, o_norm_w=o_norm_w, o_w_in=o_w_in, o_pool_w=o_pool_w, o_pool_scale=o_pool_scale,
             o_dconv_w=o_dconv_w, o_dconv_b=o_dconv_b, o_ln_w=o_ln_w, o_ln_b=o_ln_b, o_w_out=o_w_out)
    m = dict(e_norm_w=m_e_norm_w, e_w_in=m_e_w_in, e_q_norm_w=m_e_q_norm_w, e_k_norm_w=m_e_k_norm_w, e_conv_w=m_e_conv_w,
             e_w_out=m_e_w_out, o_norm_w=m_o_norm_w, o_w_in=m_o_w_in, o_pool_w=m_o_pool_w, o_pool_scale=m_o_pool_scale,
             o_dconv_w=m_o_dconv_w, o_dconv_b=m_o_dconv_b, o_ln_w=m_o_ln_w, o_ln_b=m_o_ln_b, o_w_out=m_o_w_out)
    v = dict(e_norm_w=v_e_norm_w, e_w_in=v_e_w_in, e_q_norm_w=v_e_q_norm_w, e_k_norm_w=v_e_k_norm_w, e_conv_w=v_e_conv_w,
             e_w_out=v_e_w_out, o_norm_w=v_o_norm_w, o_w_in=v_o_w_in, o_pool_w=v_o_pool_w, o_pool_scale=v_o_pool_scale,
             o_dconv_w=v_o_dconv_w, o_dconv_b=v_o_dconv_b, o_ln_w=v_o_ln_w, o_ln_b=v_o_ln_b, o_w_out=v_o_w_out)
    S = x.shape[1]
    me = 4 * lax.axis_index("x") + 2 * lax.axis_index("y") + lax.axis_index("c")

    small_local, small_offs = _pack_rows([w[n_] for n_ in _SMALL_SHARDED])
    g_e_in, = _all_gather([w["e_w_in"][0].astype(bf16)], name="gather_e_w_in")
    rest_local = [w["e_w_out"][0].astype(bf16), w["o_w_in"][0].astype(bf16), w["o_w_out"][0].astype(bf16), small_local]

    def unpack_rest(gathered):
        g_e_out, g_o_in, g_o_out, g_small = gathered
        full = {}
        for n_, off in zip(_SMALL_SHARDED, small_offs):
            shard = _unpack_rows(g_small, off, w[n_].shape[1:])
            full[n_] = jnp.moveaxis(shard, 0, -2).reshape(*shard.shape[1:-1], N_DEV * shard.shape[-1])
        return dict(
            e_conv_w=full["e_conv_w"], e_w_out=g_e_out.reshape(D_MODEL, D_MODEL), o_norm_w=full["o_norm_w"].reshape(1, D_MODEL),
            o_w_in=jnp.moveaxis(g_o_in, 0, 1).reshape(D_MODEL, ODD_IN), o_pool_scale=full["o_pool_scale"].reshape(1, 512),
            o_dconv_w=full["o_dconv_w"], o_dconv_b=full["o_dconv_b"].reshape(1, 512), o_ln_w=full["o_ln_w"].reshape(1, 512),
            o_ln_b=full["o_ln_b"].reshape(1, 512), o_w_out=g_o_out.reshape(D_MODEL, D_MODEL))

    loss_blk, grad_x, recv, small_where = _local_step(
        x[0], positions.reshape(S, 1), loss_target[0],
        dict(e_norm_w=w["e_norm_w"], e_w_in=g_e_in, e_q_norm_w=w["e_q_norm_w"], e_k_norm_w=w["e_k_norm_w"],
             o_pool_w=w["o_pool_w"][0]),
        dist=(rest_local, unpack_rest))
    loss = lax.psum(loss_blk[0, 0], ("x", "y", "c"))

    out_g, out_d, out_m, out_v = {}, {}, {}, {}
    for n_ in _BIG:
        res = _sum_adamw(recv[n_], w[n_][0], m[n_][0], v[n_][0], name="adamw_" + n_)
        out_g[n_], out_d[n_], out_m[n_], out_v[n_] = [r[None] for r in res]
    small_sum = _sum_parts(recv["small_late"], name="sum_small_grads")
    e_norm_sum = _sum_parts(recv["e_norm_w"], name="sum_e_norm_grad")
    gs = []
    for n_ in _SMALL_ALL:
        if n_ == "e_norm_w":
            gs.append(e_norm_sum.reshape(w[n_].shape))
            continue
        off, shape = small_where[n_]
        gfull = _unpack_rows(small_sum, off, shape)
        if n_ in _SMALL_SHARDED:
            width = w[n_].shape[-1]
            gfull = lax.dynamic_slice_in_dim(gfull, me * width, width, axis=gfull.ndim - 1)
        gs.append(gfull.reshape(w[n_].shape))
    ds, nms, nvs = _adamw_small([w[n_] for n_ in _SMALL_ALL], gs, [m[n_] for n_ in _SMALL_ALL], [v[n_] for n_ in _SMALL_ALL])
    for n_, g_, d_, nm_, nv_ in zip(_SMALL_ALL, gs, ds, nms, nvs):
        out_g[n_], out_d[n_], out_m[n_], out_v[n_] = g_, d_, nm_, nv_

    return (loss, grad_x[None], *[out_g[n_] for n_ in _WEIGHTS], *[out_d[n_] for n_ in _WEIGHTS],
            *[out_m[n_] for n_ in _WEIGHTS], *[out_v[n_] for n_ in _WEIGHTS])
```

```python
import functools

import numpy as np
import jax
import jax.numpy as jnp
from jax import lax
from jax.experimental import pallas as pl
from jax.experimental.pallas import tpu as pltpu

f32 = jnp.float32
bf16 = jnp.bfloat16

D_MODEL = 1024
HEAD_DIM = 64
N_GROUPS = 3
DILATIONS = (1, 4, 16)
QBLK = 128
A_WIDTH = 512
EVEN_IN = 7168
ODD_IN = 2560
POOL_SIZES = (2, 4, 8, 16)
D_CONV = 31
SC_WIDTH = 3
ROT_HALF = 8
ROPE_THETA = 500000.0
EPS = 1e-6
NEG = -1e30
SCALE = HEAD_DIM ** -0.5
N_DEV = 8
LANES = 128
VMEM_LIMIT = 48 * 1024 * 1024

ADAM_LR = 0.001
ADAM_B1 = 0.9
ADAM_B2 = 0.999
ADAM_EPS = 1e-08
ADAM_WD = 0.01
ADAM_STEP = 10

E_Q, E_K, E_V, E_BG, E_CG, E_HB, E_Z = 0, 1536, 3072, 4608, 5120, 5632, 6144
O_UC, O_DA, O_DG, O_Z = 0, 512, 1024, 1536


def _cp(sem):
    return pltpu.CompilerParams(dimension_semantics=sem, vmem_limit_bytes=VMEM_LIMIT)


_HBM_ANY = pl.BlockSpec(memory_space=pl.ANY)


def _sigmoid(z):
    return 1.0 / (1.0 + jnp.exp(-z))


def _tile(n, pref):
    t = pref
    while n % t:
        t //= 2
    return t


def _place():
    return lax.axis_index("x"), lax.axis_index("y"), lax.axis_index("c")


def _exchange_plan(ins, outs, send_sems, recv_sems, local_sems, nc):
    n = len(ins)
    x, y, c = _place()
    me_i = 4 * x + 2 * y + c

    def src(t, dev_i):
        return ins[t].at[dev_i] if t < nc else ins[t]

    def copies(arriving):
        cps = []
        for m in range(1, N_DEV):
            px = 1 - x if m & 4 else x
            py = 1 - y if m & 2 else y
            pc = 1 - c if m & 1 else c
            peer_i = 4 * px + 2 * py + pc
            for t in range(n):
                cps.append(pltpu.make_async_remote_copy(
                    src_ref=src(t, peer_i), dst_ref=outs[t].at[peer_i if arriving else me_i],
                    send_sem=send_sems.at[7 * t + m - 1], recv_sem=recv_sems.at[7 * t + m - 1],
                    device_id=(x, y, c) if arriving else (px, py, pc), device_id_type=pl.DeviceIdType.MESH))
        return cps

    def mine():
        return [pltpu.make_async_copy(src(t, me_i), outs[t].at[me_i], local_sems.at[t]) for t in range(n)]

    def start():
        for cp in mine() + copies(False):
            cp.start()

    def wait():
        for cp in copies(True):
            cp.wait_recv()
        for cp in copies(False):
            cp.wait_send()
        for cp in mine():
            cp.wait()

    return start, wait


def _exchange_sems(n):
    return [pltpu.SemaphoreType.DMA((7 * n,)), pltpu.SemaphoreType.DMA((7 * n,)), pltpu.SemaphoreType.DMA((n,))]


def _exchange_out_shapes(chunked, whole):
    return ([jax.ShapeDtypeStruct(a.shape, a.dtype) for a in chunked]
            + [jax.ShapeDtypeStruct((N_DEV, *a.shape), a.dtype) for a in whole])


def _grid_call(body, *, name, grid, in_specs, out_specs, out_shape, scratch_shapes, sem, args, fuse=None):
    if fuse is None:
        return pl.pallas_call(body, name=name, grid=grid, in_specs=in_specs, out_specs=out_specs, out_shape=out_shape,
                              scratch_shapes=scratch_shapes, compiler_params=_cp(sem))(*args)
    chunked, whole = fuse
    ex = list(chunked) + list(whole)
    n, n_in, n_out, n_sc = len(ex), len(in_specs), len(out_specs), len(scratch_shapes)

    def fused(*refs):
        ins, ex_in = refs[:n_in], refs[n_in:n_in + n]
        outs, ex_out = refs[n_in + n:n_in + n + n_out], refs[n_in + n + n_out:n_in + 2 * n + n_out]
        scratch = refs[n_in + 2 * n + n_out:n_in + 2 * n + n_out + n_sc]
        start, wait = _exchange_plan(ex_in, ex_out, *refs[-3:], len(chunked))
        first = functools.reduce(jnp.logical_and, [pl.program_id(a) == 0 for a in range(len(grid))])
        last = functools.reduce(jnp.logical_and, [pl.program_id(a) == g - 1 for a, g in enumerate(grid)])
        pl.when(first)(start)
        body(*ins, *outs, *scratch)
        pl.when(last)(wait)

    res = pl.pallas_call(
        fused, name=name, grid=grid, in_specs=list(in_specs) + [_HBM_ANY] * n,
        out_specs=list(out_specs) + [_HBM_ANY] * n, out_shape=list(out_shape) + _exchange_out_shapes(chunked, whole),
        scratch_shapes=list(scratch_shapes) + _exchange_sems(n),
        compiler_params=_cp(("arbitrary",) * len(grid)))(*args, *ex)
    return res[:n_out], res[n_out:]


def _mm_nn(a, b, *, name, out_dtype=f32, res=None, tn=1024, fuse=None):
    M, K = a.shape
    tm = _tile(M, 1024)
    if b.ndim == 3:
        tn = b.shape[2]
        N = b.shape[0] * tn
        b_spec = pl.BlockSpec((None, K, tn), lambda i, j: (j, 0, 0))
    else:
        N = b.shape[1]
        tn = _tile(N, tn)
        b_spec = pl.BlockSpec((K, tn), lambda i, j: (0, j))

    def body(*refs):
        if res is None:
            a_ref, b_ref, o_ref = refs
        else:
            a_ref, b_ref, r_ref, o_ref = refs
        acc = jnp.dot(a_ref[...], b_ref[...], preferred_element_type=f32)
        if res is not None:
            acc = acc + r_ref[...]
        o_ref[...] = acc.astype(out_dtype)

    in_specs = [pl.BlockSpec((tm, K), lambda i, j: (i, 0)), b_spec]
    args = [a, b]
    if res is not None:
        in_specs.append(pl.BlockSpec((tm, tn), lambda i, j: (i, j)))
        args.append(res)
    out = _grid_call(
        body, name=name, grid=(M // tm, N // tn), in_specs=in_specs,
        out_specs=[pl.BlockSpec((tm, tn), lambda i, j: (i, j))],
        out_shape=[jax.ShapeDtypeStruct((M, N), out_dtype)], scratch_shapes=[],
        sem=("parallel", "parallel"), args=args, fuse=fuse)
    return out[0] if fuse is None else (out[0][0], out[1])


def _mm_nt(a, b, *, name, out_dtype=f32, fuse=None):
    M, K = a.shape
    tm = _tile(M, 1024)
    if b.ndim == 3:
        nk, N, tk = b.shape
        b_spec = pl.BlockSpec((None, N, tk), lambda i, k: (k, 0, 0))
    else:
        N = b.shape[0]
        tk = _tile(K, 1024) if K % 1024 == 0 else _tile(K, 512)
        nk = K // tk
        b_spec = pl.BlockSpec((N, tk), lambda i, k: (0, k))

    def body(a_ref, b_ref, o_ref, acc_ref):
        k = pl.program_id(1)
        part = lax.dot_general(a_ref[...], b_ref[...], (((1,), (1,)), ((), ())), preferred_element_type=f32)

        @pl.when(k == 0)
        def _():
            acc_ref[...] = part

        @pl.when(k > 0)
        def _():
            acc_ref[...] += part

        @pl.when(k == nk - 1)
        def _():
            o_ref[...] = acc_ref[...].astype(out_dtype)

    out = _grid_call(
        body, name=name, grid=(M // tm, nk),
        in_specs=[pl.BlockSpec((tm, tk), lambda i, k: (i, k)), b_spec],
        out_specs=[pl.BlockSpec((tm, N), lambda i, k: (i, 0))],
        out_shape=[jax.ShapeDtypeStruct((M, N), out_dtype)],
        scratch_shapes=[pltpu.VMEM((tm, N), f32)],
        sem=("parallel", "arbitrary"), args=[a, b], fuse=fuse)
    return out[0] if fuse is None else (out[0][0], out[1])


def _load_once(src_hbm, dst_vmem, sem):
    @pl.when(pl.program_id(0) == 0)
    def _():
        cp = pltpu.make_async_copy(src_hbm, dst_vmem, sem)
        cp.start()
        cp.wait()


def _mm_nn_resident(a, b, *, name, fuse=None):
    M, K = a.shape
    nch, _, tn = b.shape
    tm = _tile(M, 256)

    def body(a_ref, b_hbm, o_ref, bbuf, sem):
        _load_once(b_hbm, bbuf, sem)
        av = a_ref[...]
        for j in range(nch):
            o_ref[:, j * tn:(j + 1) * tn] = jnp.dot(av, bbuf[j], preferred_element_type=f32)

    out = _grid_call(
        body, name=name, grid=(M // tm,), in_specs=[pl.BlockSpec((tm, K), lambda i: (i, 0)), _HBM_ANY],
        out_specs=[pl.BlockSpec((tm, nch * tn), lambda i: (i, 0))],
        out_shape=[jax.ShapeDtypeStruct((M, nch * tn), f32)],
        scratch_shapes=[pltpu.VMEM(b.shape, b.dtype), pltpu.SemaphoreType.DMA],
        sem=("arbitrary",), args=[a, b], fuse=fuse)
    return out[0] if fuse is None else (out[0][0], out[1])


def _mm_nt_resident(a, b, *, name, fuse=None):
    M, K = a.shape
    nch, N, tk = b.shape
    tm = _tile(M, 512)

    def body(a_ref, b_hbm, o_ref, bbuf, sem):
        _load_once(b_hbm, bbuf, sem)
        acc = None
        for k in range(nch):
            part = lax.dot_general(a_ref[:, k * tk:(k + 1) * tk], bbuf[k], (((1,), (1,)), ((), ())),
                                   preferred_element_type=f32)
            acc = part if acc is None else acc + part
        o_ref[...] = acc

    out = _grid_call(
        body, name=name, grid=(M // tm,), in_specs=[pl.BlockSpec((tm, K), lambda i: (i, 0)), _HBM_ANY],
        out_specs=[pl.BlockSpec((tm, N), lambda i: (i, 0))],
        out_shape=[jax.ShapeDtypeStruct((M, N), f32)],
        scratch_shapes=[pltpu.VMEM(b.shape, b.dtype), pltpu.SemaphoreType.DMA],
        sem=("arbitrary",), args=[a, b], fuse=fuse)
    return out[0] if fuse is None else (out[0][0], out[1])


def _mm_tn(a, b, *, name, out_dtype=f32, tn=512, chunks=None):
    S, Ka = a.shape
    N = b.shape[1]
    ts = _tile(S, 2048)
    ns = S // ts
    if chunks:
        tn = N // chunks
        out_spec = pl.BlockSpec((None, Ka, tn), lambda j, s: (j, 0, 0))
        out_shape = jax.ShapeDtypeStruct((chunks, Ka, tn), out_dtype)
    else:
        tn = _tile(N, tn)
        out_spec = pl.BlockSpec((Ka, tn), lambda j, s: (0, j))
        out_shape = jax.ShapeDtypeStruct((Ka, N), out_dtype)

    def body(a_ref, b_ref, o_ref, acc_ref):
        s = pl.program_id(1)
        part = lax.dot_general(a_ref[...], b_ref[...], (((0,), (0,)), ((), ())), preferred_element_type=f32)

        @pl.when(s == 0)
        def _():
            acc_ref[...] = part

        @pl.when(s > 0)
        def _():
            acc_ref[...] += part

        @pl.when(s == ns - 1)
        def _():
            o_ref[...] = acc_ref[...].astype(out_dtype)

    return pl.pallas_call(
        body, name=name, grid=(N // tn, ns),
        in_specs=[pl.BlockSpec((ts, Ka), lambda j, s: (s, 0)), pl.BlockSpec((ts, tn), lambda j, s: (s, j))],
        out_specs=out_spec, out_shape=out_shape,
        scratch_shapes=[pltpu.VMEM((Ka, tn), f32)],
        compiler_params=_cp(("parallel", "arbitrary")),
    )(a, b)


def _mm_out_loss(u, w, x_res, target, *, name):
    M, K = u.shape
    N = w.shape[1]
    tm = _tile(M, 512)
    nm = M // tm

    def body(u_ref, w_ref, x_ref, t_ref, dy_ref, dyb_ref, loss_ref, acc_ref):
        i = pl.program_id(0)
        y = jnp.dot(u_ref[...], w_ref[...], preferred_element_type=f32) + x_ref[...]
        err = y - t_ref[...]
        dy = err * (1.0 / N)
        dy_ref[...] = dy
        dyb_ref[...] = dy.astype(bf16)
        part = jnp.sum(err * err, axis=0, keepdims=True)

        @pl.when(i == 0)
        def _():
            acc_ref[...] = part

        @pl.when(i > 0)
        def _():
            acc_ref[...] += part

        @pl.when(i == nm - 1)
        def _():
            tot = jnp.sum(acc_ref[...], axis=1, keepdims=True)
            loss_ref[...] = jnp.broadcast_to(tot * (0.5 / N), (8, LANES))

    return pl.pallas_call(
        body, name=name, grid=(nm,),
        in_specs=[pl.BlockSpec((tm, K), lambda i: (i, 0)), pl.BlockSpec((K, N), lambda i: (0, 0)),
                  pl.BlockSpec((tm, N), lambda i: (i, 0)), pl.BlockSpec((tm, N), lambda i: (i, 0))],
        out_specs=[pl.BlockSpec((tm, N), lambda i: (i, 0)), pl.BlockSpec((tm, N), lambda i: (i, 0)),
                   pl.BlockSpec((8, LANES), lambda i: (0, 0))],
        out_shape=[jax.ShapeDtypeStruct((M, N), f32), jax.ShapeDtypeStruct((M, N), bf16),
                   jax.ShapeDtypeStruct((8, LANES), f32)],
        scratch_shapes=[pltpu.VMEM((1, N), f32)],
        compiler_params=_cp(("arbitrary",)),
    )(u, w, x_res, target)


def _rms_fwd(x, w, *, name):
    S, Dm = x.shape
    tm = _tile(S, 1024)

    def body(x_ref, w_ref, h_ref):
        xv = x_ref[...]
        r = lax.rsqrt(jnp.mean(xv * xv, axis=-1, keepdims=True) + EPS)
        h_ref[...] = (xv * r * w_ref[...]).astype(bf16)

    return pl.pallas_call(
        body, name=name, grid=(S // tm,),
        in_specs=[pl.BlockSpec((tm, Dm), lambda i: (i, 0)), pl.BlockSpec((1, Dm), lambda i: (0, 0))],
        out_specs=pl.BlockSpec((tm, Dm), lambda i: (i, 0)),
        out_shape=jax.ShapeDtypeStruct((S, Dm), bf16),
        compiler_params=_cp(("parallel",)),
    )(x, w)


def _rms_bwd(x, w, dh, res, *, name):
    S, Dm = x.shape
    tm = _tile(S, 512)

    def body(x_ref, w_ref, dh_ref, res_ref, dx_ref, dxb_ref, gw_ref):
        i = pl.program_id(0)
        xv = x_ref[...]
        r = lax.rsqrt(jnp.mean(xv * xv, axis=-1, keepdims=True) + EPS)
        xn = xv * r
        dh_v = dh_ref[...]
        dxn = dh_v * w_ref[...]
        dx = r * (dxn - xn * jnp.mean(dxn * xn, axis=-1, keepdims=True)) + res_ref[...]
        dx_ref[...] = dx
        dxb_ref[...] = dx.astype(bf16)
        part = jnp.sum(dh_v * xn, axis=0, keepdims=True)

        @pl.when(i == 0)
        def _():
            gw_ref[...] = part

        @pl.when(i > 0)
        def _():
            gw_ref[...] += part

    dx, dxb, gw = pl.pallas_call(
        body, name=name, grid=(S // tm,),
        in_specs=[pl.BlockSpec((tm, Dm), lambda i: (i, 0)), pl.BlockSpec((1, Dm), lambda i: (0, 0)),
                  pl.BlockSpec((tm, Dm), lambda i: (i, 0)), pl.BlockSpec((tm, Dm), lambda i: (i, 0))],
        out_specs=[pl.BlockSpec((tm, Dm), lambda i: (i, 0)), pl.BlockSpec((tm, Dm), lambda i: (i, 0)),
                   pl.BlockSpec((1, Dm), lambda i: (0, 0))],
        out_shape=[jax.ShapeDtypeStruct((S, Dm), f32), jax.ShapeDtypeStruct((S, Dm), bf16),
                   jax.ShapeDtypeStruct((1, Dm), f32)],
        compiler_params=_cp(("arbitrary",)),
    )(x, w, dh, res)
    return dx, dxb, gw


_INV_FREQ = [float(v) for v in (np.float32(ROPE_THETA) ** (-np.arange(ROT_HALF, dtype=np.float32) / np.float32(ROT_HALF))).astype(np.float32)]


def _rope_tables(pos_col):
    S = pos_col.shape[0]
    tm = _tile(S, 1024)

    def body(p_ref, c_ref, s1_ref, s2_ref):
        lane = lax.broadcasted_iota(jnp.int32, (tm, LANES), 1)
        lm = lane % HEAD_DIM
        fi = lm % ROT_HALF
        inv = jnp.zeros((tm, LANES), f32)
        for k in range(ROT_HALF):
            inv = jnp.where(fi == k, _INV_FREQ[k], inv)
        ang = p_ref[...].astype(f32) * inv
        cs = jnp.cos(ang)
        sn = jnp.sin(ang)
        c_ref[...] = jnp.where(lm < 2 * ROT_HALF, cs, 1.0)
        s1_ref[...] = jnp.where((lm >= ROT_HALF) & (lm < 2 * ROT_HALF), sn, 0.0)
        s2_ref[...] = jnp.where(lm < ROT_HALF, -sn, 0.0)

    spec = pl.BlockSpec((tm, LANES), lambda i: (i, 0))
    return pl.pallas_call(
        body, name="rope_tables", grid=(S // tm,),
        in_specs=[pl.BlockSpec((tm, 1), lambda i: (i, 0))],
        out_specs=[spec, spec, spec],
        out_shape=[jax.ShapeDtypeStruct((S, LANES), f32)] * 3,
        compiler_params=_cp(("parallel",)),
    )(pos_col)


def _head_mean(v, m):
    hi = v.astype(bf16)
    lo = (v - hi.astype(f32)).astype(bf16)
    return jnp.dot(hi, m, preferred_element_type=f32) + jnp.dot(lo, m, preferred_element_type=f32)


def _head_mean_matrix():
    i = np.arange(LANES)
    return jnp.asarray(((i[:, None] // HEAD_DIM) == (i[None, :] // HEAD_DIM)).astype(np.float32) / HEAD_DIM, dtype=bf16)


def _qk_prep(proj, tabs, nw, hm):
    S = proj.shape[0]
    tm = _tile(S, 512)

    def body(x_ref, c_ref, s1_ref, s2_ref, nw_ref, m_ref, o_ref):
        cb = pl.program_id(1)
        w = jnp.where(cb >= 3, nw_ref[1:2, :], nw_ref[0:1, :])
        c, s1, s2, m = c_ref[...], s1_ref[...], s2_ref[...], m_ref[...]
        for p in range(4):
            t = x_ref[:, p * LANES:(p + 1) * LANES]
            r = lax.rsqrt(_head_mean(t * t, m) + EPS)
            that = t * r * w
            o_ref[:, p * LANES:(p + 1) * LANES] = (
                that * c + pltpu.roll(that, ROT_HALF, axis=1) * s1 + pltpu.roll(that, LANES - ROT_HALF, axis=1) * s2)

    tab = pl.BlockSpec((tm, LANES), lambda i, j: (i, 0))
    return pl.pallas_call(
        body, name="qk_prep", grid=(S // tm, 6),
        in_specs=[pl.BlockSpec((tm, 512), lambda i, j: (i, j)), tab, tab, tab,
                  pl.BlockSpec((2, LANES), lambda i, j: (0, 0)), pl.BlockSpec((LANES, LANES), lambda i, j: (0, 0))],
        out_specs=pl.BlockSpec((tm, 512), lambda i, j: (i, j)),
        out_shape=jax.ShapeDtypeStruct((S, 3072), f32),
        compiler_params=_cp(("parallel", "parallel")),
    )(proj, *tabs, nw, hm)


def _key_geometry(nparts):
    qr = QBLK // nparts
    rho = lax.broadcasted_iota(jnp.int32, (2 * QBLK, 2 * QBLK), 0) % QBLK
    kap = lax.broadcasted_iota(jnp.int32, (2 * QBLK, 2 * QBLK), 1)
    n_q = QBLK + nparts * (rho % qr) + rho // qr
    tt = kap % (2 * qr)
    n_k = nparts * tt + kap // (2 * qr)
    dist = n_q - n_k
    return (dist >= 0) & (dist <= QBLK), (tt < qr).astype(jnp.int32)


def _stack_heads(t, lo):
    zero = jnp.zeros_like(t)
    return jnp.concatenate([jnp.where(lo, t, zero), jnp.where(lo, zero, t)], axis=0)


def _attn_block_fwd(qb, kcat, vcat, mask, lo):
    s = lax.dot_general(_stack_heads(qb, lo), kcat, (((1,), (1,)), ((), ())), preferred_element_type=f32) * SCALE
    s = jnp.where(mask, s, NEG)
    mx = jnp.max(s, axis=-1, keepdims=True)
    pexp = jnp.exp(s - mx)
    den = jnp.sum(pexp, axis=-1, keepdims=True)
    pn = (pexp * (1.0 / den)).astype(bf16)
    o2 = jnp.dot(pn, vcat, preferred_element_type=f32)
    lse2 = jnp.broadcast_to(mx + jnp.log(den), (2 * QBLK, LANES))
    return jnp.where(lo, o2[:QBLK], o2[QBLK:]), jnp.where(lo, lse2[:QBLK], lse2[QBLK:])


def _attn_block_bwd(qb, dob, kcat, vcat, lt, ds, mask, lo):
    lt_sw = pltpu.roll(lt, HEAD_DIM, axis=1)
    ds_sw = pltpu.roll(ds, HEAD_DIM, axis=1)
    lt2 = jnp.concatenate([jnp.where(lo, lt, lt_sw), jnp.where(lo, lt_sw, lt)], axis=0)
    ds2 = jnp.concatenate([jnp.where(lo, ds, ds_sw), jnp.where(lo, ds_sw, ds)], axis=0)
    q2 = _stack_heads(qb, lo)
    do2 = _stack_heads(dob, lo)
    s = lax.dot_general(q2, kcat, (((1,), (1,)), ((), ())), preferred_element_type=f32) * SCALE
    s = jnp.where(mask, s, NEG)
    prob = jnp.exp(s - jnp.concatenate([lt2, lt2], axis=1))
    dp = lax.dot_general(do2, vcat, (((1,), (1,)), ((), ())), preferred_element_type=f32)
    dsb = (prob * (dp - jnp.concatenate([ds2, ds2], axis=1)) * SCALE).astype(bf16)
    dq2 = jnp.dot(dsb, kcat, preferred_element_type=f32)
    dk = lax.dot_general(dsb, q2, (((0,), (0,)), ((), ())), preferred_element_type=f32)
    dv = lax.dot_general(prob.astype(bf16), do2, (((0,), (0,)), ((), ())), preferred_element_type=f32)
    return jnp.where(lo, dq2[:QBLK], dq2[QBLK:]), dk, dv


ATT_ROWS = 1024


def _attn_fwd_local(qk, proj):
    S = qk.shape[0]
    tr = _tile(S, ATT_ROWS)
    lw = 4 * LANES
    nb = tr // QBLK

    def body(q_ref, k_ref, kh_ref, v_ref, vh_ref, o_ref, lse_ref, kbuf, vbuf):
        j = pl.program_id(0)
        kbuf[0:QBLK, :] = jnp.where(j > 0, kh_ref[...], 0.0)
        kbuf[QBLK:, :] = k_ref[...]
        vbuf[0:QBLK, :] = jnp.where(j > 0, vh_ref[...], 0.0)
        vbuf[QBLK:, :] = v_ref[...]
        band, is_prev = _key_geometry(1)
        lo = lax.broadcasted_iota(jnp.int32, (QBLK, LANES), 1) < HEAD_DIM

        def blk(c, carry):
            r0 = pl.multiple_of(c * QBLK, QBLK)
            first = jnp.where((c == 0) & (j == 0), 1, 0)
            mask = band & (is_prev * first == 0)
            for pp in range(lw // LANES):
                lanes = slice(pp * LANES, (pp + 1) * LANES)
                o, lse = _attn_block_fwd(q_ref[pl.ds(r0, QBLK), lanes].astype(bf16),
                                         kbuf[pl.ds(r0, 2 * QBLK), lanes].astype(bf16),
                                         vbuf[pl.ds(r0, 2 * QBLK), lanes].astype(bf16), mask, lo)
                o_ref[pl.ds(r0, QBLK), lanes] = o
                lse_ref[pl.ds(r0, QBLK), lanes] = lse
            return carry

        lax.fori_loop(0, nb, blk, 0)

    def halo(col):
        return pl.BlockSpec((QBLK, lw), lambda j, l: (jnp.maximum(j * nb - 1, 0), col + l))

    def tile(col):
        return pl.BlockSpec((tr, lw), lambda j, l: (j, col + l))

    return pl.pallas_call(
        body, name="attn_fwd0", grid=(S // tr, A_WIDTH // lw),
        in_specs=[tile(E_Q // lw), tile(E_K // lw), halo(E_K // lw), tile(E_V // lw), halo(E_V // lw)],
        out_specs=[tile(0), tile(0)],
        out_shape=[jax.ShapeDtypeStruct((S, A_WIDTH), f32)] * 2,
        scratch_shapes=[pltpu.VMEM((QBLK + tr, lw), f32)] * 2,
        compiler_params=_cp(("parallel", "parallel")),
    )(qk, qk, qk, proj, proj)


def _attn_bwd_local(qk, proj, do_a, lt, dsum, fuse=None):
    S = qk.shape[0]
    tr = _tile(S, ATT_ROWS)
    lw = 2 * LANES
    nb = tr // QBLK
    nt = S // tr

    def body(q_ref, qn_ref, do_ref, don_ref, lt_ref, ltn_ref, ds_ref, dsn_ref, k_ref, kh_ref, v_ref, vh_ref,
             dq_ref, dk_ref, dv_ref, kbuf, vbuf, dkbuf, dvbuf):
        j = pl.program_id(0)
        zeros = jnp.zeros((QBLK, lw), f32)
        kbuf[0:QBLK, :] = jnp.where(j > 0, kh_ref[...], 0.0)
        kbuf[pl.ds(QBLK, tr), :] = k_ref[...]
        kbuf[pl.ds(QBLK + tr, QBLK), :] = zeros
        vbuf[0:QBLK, :] = jnp.where(j > 0, vh_ref[...], 0.0)
        vbuf[pl.ds(QBLK, tr), :] = v_ref[...]
        vbuf[pl.ds(QBLK + tr, QBLK), :] = zeros
        dkbuf[...] = jnp.zeros_like(dkbuf)
        dvbuf[...] = jnp.zeros_like(dvbuf)
        band, is_prev = _key_geometry(1)
        lo = lax.broadcasted_iota(jnp.int32, (QBLK, LANES), 1) < HEAD_DIM

        def blk(c, carry):
            r0 = pl.multiple_of(c * QBLK, QBLK)
            first = jnp.where((c == 0) & (j == 0), 1, 0)
            mask = band & (is_prev * first == 0)
            for pp in range(lw // LANES):
                lanes = slice(pp * LANES, (pp + 1) * LANES)
                dq, dk, dv = _attn_block_bwd(
                    q_ref[pl.ds(r0, QBLK), lanes].astype(bf16), do_ref[pl.ds(r0, QBLK), lanes].astype(bf16),
                    kbuf[pl.ds(r0, 2 * QBLK), lanes].astype(bf16), vbuf[pl.ds(r0, 2 * QBLK), lanes].astype(bf16),
                    lt_ref[pl.ds(r0, QBLK), lanes], ds_ref[pl.ds(r0, QBLK), lanes], mask, lo)
                dq_ref[pl.ds(r0, QBLK), lanes] = dq
                dkbuf[pl.ds(r0, 2 * QBLK), lanes] += dk
                dvbuf[pl.ds(r0, 2 * QBLK), lanes] += dv
            return carry

        lax.fori_loop(0, nb, blk, 0)

        @pl.when(j < nt - 1)
        def _():
            mask = band & (is_prev == 1)
            for pp in range(lw // LANES):
                lanes = slice(pp * LANES, (pp + 1) * LANES)
                _, dk, dv = _attn_block_bwd(
                    qn_ref[:, lanes].astype(bf16), don_ref[:, lanes].astype(bf16),
                    kbuf[pl.ds(tr, 2 * QBLK), lanes].astype(bf16), vbuf[pl.ds(tr, 2 * QBLK), lanes].astype(bf16),
                    ltn_ref[:, lanes], dsn_ref[:, lanes], mask, lo)
                dkbuf[pl.ds(tr, 2 * QBLK), lanes] += dk
                dvbuf[pl.ds(tr, 2 * QBLK), lanes] += dv

        dk_ref[...] = dkbuf[pl.ds(QBLK, tr), :]
        dv_ref[...] = dvbuf[pl.ds(QBLK, tr), :]

    def prev_halo(col):
        return pl.BlockSpec((QBLK, lw), lambda j, l: (jnp.maximum(j * nb - 1, 0), col + l))

    def next_halo(col):
        return pl.BlockSpec((QBLK, lw), lambda j, l: (jnp.minimum((j + 1) * nb, S // QBLK - 1), col + l))

    def tile(col):
        return pl.BlockSpec((tr, lw), lambda j, l: (j, col + l))

    return _grid_call(
        body, name="attn_bwd0", grid=(nt, A_WIDTH // lw),
        in_specs=[tile(E_Q // lw), next_halo(E_Q // lw), tile(0), next_halo(0), tile(0), next_halo(0), tile(0), next_halo(0),
                  tile(E_K // lw), prev_halo(E_K // lw), tile(E_V // lw), prev_halo(E_V // lw)],
        out_specs=[tile(0)] * 3,
        out_shape=[jax.ShapeDtypeStruct((S, A_WIDTH), f32)] * 3,
        scratch_shapes=[pltpu.VMEM((tr + 2 * QBLK, lw), f32)] * 4,
        sem=("parallel", "parallel"), args=[qk, qk, do_a, do_a, lt, lt, dsum, dsum, qk, qk, proj, proj], fuse=fuse)


def _stream_view(a, d):
    S, W = a.shape
    return a.reshape(S // 8, 8, W) if d == 4 else a.reshape(S // 16, 2, 8, W)


def _stream_ref(ref, d, r, part, col, lw):
    n = ref.shape[0]
    if d == 4:
        return ref.at[pl.ds(0, n), r + 4 * part, pl.ds(col, lw)]
    return ref.at[pl.ds(0, n), r // 8, r % 8, pl.ds(col, lw)]


def _stream_geometry(S, d):
    nparts = 2 if d == 4 else 1
    rows = S // (d * nparts)
    return nparts, rows, QBLK // nparts


def _attn_fwd_dil(qk, proj, g, *, name):
    S = qk.shape[0]
    d = DILATIONS[g]
    nparts, rows, qr = _stream_geometry(S, d)
    nb = rows // qr
    lw = 2 * LANES if d == 4 else 4 * LANES
    nlg = A_WIDTH // lw
    nitems = d * nlg
    ins = ((0, E_Q + A_WIDTH * g, 0), (0, E_K + A_WIDTH * g, qr), (1, E_V + A_WIDTH * g, qr))

    def body(qk_hbm, pj_hbm, o_hbm, l_hbm, qbuf, kbuf, vbuf, obuf, lbuf, in_sems, out_sems):
        i = pl.program_id(0)
        slot = i % 2
        hbm_in = (qk_hbm, pj_hbm)
        bufs_in = (qbuf, kbuf, vbuf)

        def in_copies(item, sl):
            r, lg = item // nlg, item % nlg
            cps = []
            for a in range(nparts):
                for t, (src, col, pad) in enumerate(ins):
                    cps.append(pltpu.make_async_copy(
                        _stream_ref(hbm_in[src], d, r, a, pl.multiple_of(col + lw * lg, LANES), lw),
                        bufs_in[t].at[sl, a, pl.ds(pad, rows), :], in_sems.at[sl, 3 * a + t]))
            return cps

        def out_copies(item, sl):
            r, lg = item // nlg, item % nlg
            cps = []
            for a in range(nparts):
                for t, (buf, dst) in enumerate(((obuf, o_hbm), (lbuf, l_hbm))):
                    cps.append(pltpu.make_async_copy(
                        buf.at[sl, a], _stream_ref(dst, d, r, a, pl.multiple_of(lw * lg, LANES), lw),
                        out_sems.at[sl, 2 * a + t]))
            return cps

        @pl.when(i == 0)
        def _():
            for sl in range(2):
                for a in range(nparts):
                    kbuf[sl, a, 0:qr, :] = jnp.zeros((qr, lw), f32)
                    vbuf[sl, a, 0:qr, :] = jnp.zeros((qr, lw), f32)
            for cp in in_copies(0, 0):
                cp.start()

        @pl.when(i + 1 < nitems)
        def _():
            for cp in in_copies(i + 1, 1 - slot):
                cp.start()

        for cp in in_copies(i, slot):
            cp.wait()

        @pl.when(i >= 2)
        def _():
            for cp in out_copies(i - 2, slot):
                cp.wait()

        band, is_prev = _key_geometry(nparts)
        lo = lax.broadcasted_iota(jnp.int32, (QBLK, LANES), 1) < HEAD_DIM

        def blk(c, carry):
            r0 = pl.multiple_of(c * qr, qr)
            mask = band & (is_prev * jnp.where(c == 0, 1, 0) == 0)
            for pp in range(lw // LANES):
                lanes = slice(pp * LANES, (pp + 1) * LANES)
                qb = jnp.concatenate([qbuf[slot, a, pl.ds(r0, qr), lanes] for a in range(nparts)], axis=0).astype(bf16)
                kcat = jnp.concatenate([kbuf[slot, a, pl.ds(r0, 2 * qr), lanes] for a in range(nparts)], axis=0).astype(bf16)
                vcat = jnp.concatenate([vbuf[slot, a, pl.ds(r0, 2 * qr), lanes] for a in range(nparts)], axis=0).astype(bf16)
                o, lse = _attn_block_fwd(qb, kcat, vcat, mask, lo)
                for a in range(nparts):
                    obuf[slot, a, pl.ds(r0, qr), lanes] = o[a * qr:(a + 1) * qr]
                    lbuf[slot, a, pl.ds(r0, qr), lanes] = lse[a * qr:(a + 1) * qr]
            return carry

        lax.fori_loop(0, nb, blk, 0)

        for cp in out_copies(i, slot):
            cp.start()

        @pl.when(i == nitems - 1)
        def _():
            for cp in out_copies(i - 1, 1 - slot) + out_copies(i, slot):
                cp.wait()

    vshape = (S // 8, 8, A_WIDTH) if d == 4 else (S // 16, 2, 8, A_WIDTH)
    o, lse = pl.pallas_call(
        body, name=name, grid=(nitems,),
        in_specs=[_HBM_ANY, _HBM_ANY], out_specs=[_HBM_ANY, _HBM_ANY],
        out_shape=[jax.ShapeDtypeStruct(vshape, f32)] * 2,
        scratch_shapes=[pltpu.VMEM((2, nparts, rows, lw), f32), pltpu.VMEM((2, nparts, qr + rows, lw), f32),
                        pltpu.VMEM((2, nparts, qr + rows, lw), f32), pltpu.VMEM((2, nparts, rows, lw), f32),
                        pltpu.VMEM((2, nparts, rows, lw), f32),
                        pltpu.SemaphoreType.DMA((2, 3 * nparts)), pltpu.SemaphoreType.DMA((2, 2 * nparts))],
        compiler_params=_cp(("arbitrary",)),
    )(_stream_view(qk, d), _stream_view(proj, d))
    return o.reshape(S, A_WIDTH), lse.reshape(S, A_WIDTH)


def _attn_bwd_dil(qk, proj, do_a, lt, dsum, g, *, name):
    S = qk.shape[0]
    d = DILATIONS[g]
    nparts, rows, qr = _stream_geometry(S, d)
    nb = rows // qr
    lw = LANES if d == 4 else 4 * LANES
    nlg = A_WIDTH // lw
    nitems = d * nlg
    ins = ((0, E_Q + A_WIDTH * g, 0), (2, 0, 0), (3, 0, 0), (4, 0, 0), (0, E_K + A_WIDTH * g, qr), (1, E_V + A_WIDTH * g, qr))
    n_in = len(ins)

    def body(qk_hbm, pj_hbm, do_hbm, lt_hbm, ds_hbm, dq_hbm, dk_hbm, dv_hbm,
             qbuf, dobuf, ltbuf, dsbuf, kbuf, vbuf, dqbuf, dkbuf, dvbuf, in_sems, out_sems):
        i = pl.program_id(0)
        slot = i % 2
        hbm_in = (qk_hbm, pj_hbm, do_hbm, lt_hbm, ds_hbm)
        bufs_in = (qbuf, dobuf, ltbuf, dsbuf, kbuf, vbuf)

        def in_copies(item, sl):
            r, lg = item // nlg, item % nlg
            cps = []
            for a in range(nparts):
                for t, (src, col, pad) in enumerate(ins):
                    cps.append(pltpu.make_async_copy(
                        _stream_ref(hbm_in[src], d, r, a, pl.multiple_of(col + lw * lg, LANES), lw),
                        bufs_in[t].at[sl, a, pl.ds(pad, rows), :], in_sems.at[sl, n_in * a + t]))
            return cps

        def out_copies(item, sl):
            r, lg = item // nlg, item % nlg
            cps = []
            for a in range(nparts):
                for t, (buf, dst, pad) in enumerate(((dqbuf, dq_hbm, 0), (dkbuf, dk_hbm, qr), (dvbuf, dv_hbm, qr))):
                    cps.append(pltpu.make_async_copy(
                        buf.at[sl, a, pl.ds(pad, rows), :],
                        _stream_ref(dst, d, r, a, pl.multiple_of(lw * lg, LANES), lw), out_sems.at[sl, 3 * a + t]))
            return cps

        @pl.when(i == 0)
        def _():
            for sl in range(2):
                for a in range(nparts):
                    kbuf[sl, a, 0:qr, :] = jnp.zeros((qr, lw), f32)
                    vbuf[sl, a, 0:qr, :] = jnp.zeros((qr, lw), f32)
            for cp in in_copies(0, 0):
                cp.start()

        @pl.when(i + 1 < nitems)
        def _():
            for cp in in_copies(i + 1, 1 - slot):
                cp.start()

        for cp in in_copies(i, slot):
            cp.wait()

        @pl.when(i >= 2)
        def _():
            for cp in out_copies(i - 2, slot):
                cp.wait()

        for a in range(nparts):
            dkbuf[slot, a] = jnp.zeros((qr + rows, lw), f32)
            dvbuf[slot, a] = jnp.zeros((qr + rows, lw), f32)
        band, is_prev = _key_geometry(nparts)
        lo = lax.broadcasted_iota(jnp.int32, (QBLK, LANES), 1) < HEAD_DIM

        def blk(c, carry):
            r0 = pl.multiple_of(c * qr, qr)
            mask = band & (is_prev * jnp.where(c == 0, 1, 0) == 0)

            def rows_of(buf, n, lanes):
                return jnp.concatenate([buf[slot, a, pl.ds(r0, n), lanes] for a in range(nparts)], axis=0)

            for pp in range(lw // LANES):
                lanes = slice(pp * LANES, (pp + 1) * LANES)
                dq, dk, dv = _attn_block_bwd(
                    rows_of(qbuf, qr, lanes).astype(bf16), rows_of(dobuf, qr, lanes).astype(bf16),
                    rows_of(kbuf, 2 * qr, lanes).astype(bf16), rows_of(vbuf, 2 * qr, lanes).astype(bf16),
                    rows_of(ltbuf, qr, lanes), rows_of(dsbuf, qr, lanes), mask, lo)
                for a in range(nparts):
                    dqbuf[slot, a, pl.ds(r0, qr), lanes] = dq[a * qr:(a + 1) * qr]
                    dkbuf[slot, a, pl.ds(r0, 2 * qr), lanes] += dk[2 * a * qr:2 * (a + 1) * qr]
                    dvbuf[slot, a, pl.ds(r0, 2 * qr), lanes] += dv[2 * a * qr:2 * (a + 1) * qr]
            return carry

        lax.fori_loop(0, nb, blk, 0)

        for cp in out_copies(i, slot):
            cp.start()

        @pl.when(i == nitems - 1)
        def _():
            for cp in out_copies(i - 1, 1 - slot) + out_copies(i, slot):
                cp.wait()

    vshape = (S // 8, 8, A_WIDTH) if d == 4 else (S // 16, 2, 8, A_WIDTH)
    plain = pltpu.VMEM((2, nparts, rows, lw), f32)
    padded = pltpu.VMEM((2, nparts, qr + rows, lw), f32)
    outs = pl.pallas_call(
        body, name=name, grid=(nitems,),
        in_specs=[_HBM_ANY] * 5, out_specs=[_HBM_ANY] * 3,
        out_shape=[jax.ShapeDtypeStruct(vshape, f32)] * 3,
        scratch_shapes=[plain, plain, plain, plain, padded, padded, plain, padded, padded,
                        pltpu.SemaphoreType.DMA((2, n_in * nparts)), pltpu.SemaphoreType.DMA((2, 3 * nparts))],
        compiler_params=_cp(("arbitrary",)),
    )(*[_stream_view(a, d) for a in (qk, proj, do_a, lt, dsum)])
    return [o.reshape(S, A_WIDTH) for o in outs]


def _prev_halo(tm, h, col):
    return pl.BlockSpec((h, 512), lambda i: (jnp.maximum(i * (tm // h) - 1, 0), col))


def _next_halo(tm, h, col, S):
    return pl.BlockSpec((h, 512), lambda i: (jnp.minimum((i + 1) * (tm // h), S // h - 1), col))


def _mix0_fwd(o_g, lse_g, proj, conv_w):
    S = proj.shape[0]
    tm = _tile(S, 256)

    def body(o0, o1, o2, l0, l1, l2, bg_ref, cg_ref, hb_ref, z_ref, cgh_ref, hbh_ref, w_ref,
             u_ref, oa_ref, lt_ref, tbuf):
        i = pl.program_id(0)
        ls = [l0[...], l1[...], l2[...]]
        mx = jnp.maximum(jnp.maximum(ls[0], ls[1]), ls[2])
        es = [jnp.exp(l - mx) for l in ls]
        tot = es[0] + es[1] + es[2]
        lt_ref[...] = mx + jnp.log(tot)
        inv = 1.0 / tot
        z = z_ref[...]
        sz = z * _sigmoid(z)
        oa = (es[0] * inv) * o0[...] + (es[1] * inv) * o1[...] + (es[2] * inv) * o2[...]
        oa_ref[...] = oa
        u_ref[:, :A_WIDTH] = (oa * sz[:, :A_WIDTH]).astype(bf16)
        t = cg_ref[...] * hb_ref[...]
        tbuf[0:8, :] = jnp.where(i > 0, cgh_ref[...] * hbh_ref[...], 0.0)
        tbuf[8:, :] = t
        cv = w_ref[2:3, :] * t + w_ref[1:2, :] * tbuf[pl.ds(7, tm), :] + w_ref[0:1, :] * tbuf[pl.ds(6, tm), :]
        u_ref[:, A_WIDTH:] = (bg_ref[...] * cv * sz[:, A_WIDTH:]).astype(bf16)

    row = lambda w, c: pl.BlockSpec((tm, w), lambda i: (i, c))
    return pl.pallas_call(
        body, name="mix0_fwd", grid=(S // tm,),
        in_specs=[row(512, 0)] * 6
        + [row(512, E_BG // 512), row(512, E_CG // 512), row(512, E_HB // 512), row(1024, E_Z // 1024),
           _prev_halo(tm, 8, E_CG // 512), _prev_halo(tm, 8, E_HB // 512), pl.BlockSpec((SC_WIDTH, 512), lambda i: (0, 0))],
        out_specs=[row(1024, 0), row(512, 0), row(512, 0)],
        out_shape=[jax.ShapeDtypeStruct((S, D_MODEL), bf16), jax.ShapeDtypeStruct((S, A_WIDTH), f32),
                   jax.ShapeDtypeStruct((S, A_WIDTH), f32)],
        scratch_shapes=[pltpu.VMEM((tm + 8, 512), f32)],
        compiler_params=_cp(("parallel",)),
    )(*o_g, *lse_g, proj, proj, proj, proj, proj, proj, conv_w)


def _dsilu(z, sg):
    return sg * (1.0 + z * (1.0 - sg))


def _mix0_bwd_a(du, proj, o_a, conv_w):
    S = proj.shape[0]
    tm = _tile(S, 256)

    def body(du_ref, bg_ref, cg_ref, hb_ref, z_ref, cgh_ref, hbh_ref, oa_ref, w_ref,
             dz_ref, doa_ref, ds_ref, dbg_ref, dcv_ref, tbuf):
        i = pl.program_id(0)
        lo = lax.broadcasted_iota(jnp.int32, (tm, LANES), 1) < HEAD_DIM
        z = z_ref[...]
        sg = _sigmoid(z)
        sz = z * sg
        dsz = _dsilu(z, sg)
        du_v = du_ref[...]
        t = cg_ref[...] * hb_ref[...]
        tbuf[0:8, :] = jnp.where(i > 0, cgh_ref[...] * hbh_ref[...], 0.0)
        tbuf[8:, :] = t
        cv = w_ref[2:3, :] * t + w_ref[1:2, :] * tbuf[pl.ds(7, tm), :] + w_ref[0:1, :] * tbuf[pl.ds(6, tm), :]
        bg = bg_ref[...]
        oa = oa_ref[...]
        dz_ref[:, :A_WIDTH] = (du_v[:, :A_WIDTH] * oa * dsz[:, :A_WIDTH]).astype(bf16)
        dz_ref[:, A_WIDTH:] = (du_v[:, A_WIDTH:] * (bg * cv) * dsz[:, A_WIDTH:]).astype(bf16)
        doa = du_v[:, :A_WIDTH] * sz[:, :A_WIDTH]
        dyb = du_v[:, A_WIDTH:] * sz[:, A_WIDTH:]
        doa_ref[...] = doa
        dbg_ref[...] = (dyb * cv).astype(bf16)
        dcv_ref[...] = dyb * bg
        prod = doa * oa
        for p in range(4):
            pp = prod[:, p * LANES:(p + 1) * LANES]
            sa = jnp.sum(jnp.where(lo, pp, 0.0), axis=-1, keepdims=True)
            sb = jnp.sum(jnp.where(lo, 0.0, pp), axis=-1, keepdims=True)
            ds_ref[:, p * LANES:(p + 1) * LANES] = jnp.where(lo, sa, sb)

    row = lambda w, c: pl.BlockSpec((tm, w), lambda i: (i, c))
    return pl.pallas_call(
        body, name="mix0_bwd_a", grid=(S // tm,),
        in_specs=[row(1024, 0), row(512, E_BG // 512), row(512, E_CG // 512), row(512, E_HB // 512), row(1024, E_Z // 1024),
                  _prev_halo(tm, 8, E_CG // 512), _prev_halo(tm, 8, E_HB // 512), row(512, 0),
                  pl.BlockSpec((SC_WIDTH, 512), lambda i: (0, 0))],
        out_specs=[row(1024, 0), row(512, 0), row(512, 0), row(512, 0), row(512, 0)],
        out_shape=[jax.ShapeDtypeStruct((S, D_MODEL), bf16), jax.ShapeDtypeStruct((S, A_WIDTH), f32),
                   jax.ShapeDtypeStruct((S, A_WIDTH), f32), jax.ShapeDtypeStruct((S, 512), bf16),
                   jax.ShapeDtypeStruct((S, 512), f32)],
        scratch_shapes=[pltpu.VMEM((tm + 8, 512), f32)],
        compiler_params=_cp(("parallel",)),
    )(du, proj, proj, proj, proj, proj, proj, o_a, conv_w)


def _mix0_bwd_b(dcv, proj, conv_w):
    S = proj.shape[0]
    tm = _tile(S, 256)
    nt = S // tm

    def body(dcv_ref, dcvn_ref, cg_ref, hb_ref, cgh_ref, hbh_ref, w_ref, dcg_ref, dhb_ref, gw_ref, tbuf, dbuf):
        i = pl.program_id(0)
        cg = cg_ref[...]
        hb = hb_ref[...]
        t = cg * hb
        tbuf[0:8, :] = jnp.where(i > 0, cgh_ref[...] * hbh_ref[...], 0.0)
        tbuf[8:, :] = t
        dcv_v = dcv_ref[...]
        dbuf[0:tm, :] = dcv_v
        dbuf[tm:, :] = jnp.where(i < nt - 1, dcvn_ref[...], 0.0)
        dt = w_ref[2:3, :] * dcv_v + w_ref[1:2, :] * dbuf[pl.ds(1, tm), :] + w_ref[0:1, :] * dbuf[pl.ds(2, tm), :]
        dcg_ref[...] = (dt * hb).astype(bf16)
        dhb_ref[...] = (dt * cg).astype(bf16)
        g2 = jnp.sum(dcv_v * t, axis=0, keepdims=True)
        g1 = jnp.sum(dcv_v * tbuf[pl.ds(7, tm), :], axis=0, keepdims=True)
        g0 = jnp.sum(dcv_v * tbuf[pl.ds(6, tm), :], axis=0, keepdims=True)
        part = jnp.concatenate([g0, g1, g2, jnp.zeros((5, 512), f32)], axis=0)

        @pl.when(i == 0)
        def _():
            gw_ref[...] = part

        @pl.when(i > 0)
        def _():
            gw_ref[...] += part

    row = lambda w, c: pl.BlockSpec((tm, w), lambda i: (i, c))
    return pl.pallas_call(
        body, name="mix0_bwd_b", grid=(nt,),
        in_specs=[row(512, 0), _next_halo(tm, 8, 0, S), row(512, E_CG // 512), row(512, E_HB // 512),
                  _prev_halo(tm, 8, E_CG // 512), _prev_halo(tm, 8, E_HB // 512),
                  pl.BlockSpec((SC_WIDTH, 512), lambda i: (0, 0))],
        out_specs=[row(512, 0), row(512, 0), pl.BlockSpec((8, 512), lambda i: (0, 0))],
        out_shape=[jax.ShapeDtypeStruct((S, 512), bf16), jax.ShapeDtypeStruct((S, 512), bf16),
                   jax.ShapeDtypeStruct((8, 512), f32)],
        scratch_shapes=[pltpu.VMEM((tm + 8, 512), f32), pltpu.VMEM((tm + 8, 512), f32)],
        compiler_params=_cp(("arbitrary",)),
    )(dcv, dcv, proj, proj, proj, proj, conv_w)


def _qk_bwd(dq_g, dk_g, dv_g, proj, tabs, nw, hm, dbg, dcg, dhb, dz):
    S = proj.shape[0]
    tm = _tile(S, 256)

    def body(*refs):
        d_refs = refs[0:6]
        dv_refs = refs[6:9]
        x_ref, c_ref, s1_ref, s2_ref, nw_ref, m_ref, dbg_ref, dcg_ref, dhb_ref, dz_ref, o_ref, gw_ref = refs[9:]
        i = pl.program_id(0)
        c, s1, s2, m = c_ref[...], s1_ref[...], s2_ref[...], m_ref[...]
        accs = []
        for kind in range(2):
            w = nw_ref[kind:kind + 1, :]
            acc = jnp.zeros((1, LANES), f32)
            for gi in range(N_GROUPS):
                for p in range(4):
                    col = kind * 1536 + gi * 512 + p * LANES
                    dout = d_refs[kind * 3 + gi][:, p * LANES:(p + 1) * LANES]
                    t = x_ref[:, col:col + LANES]
                    dthat = (dout * c + pltpu.roll(dout * s1, LANES - ROT_HALF, axis=1)
                             + pltpu.roll(dout * s2, ROT_HALF, axis=1))
                    r = lax.rsqrt(_head_mean(t * t, m) + EPS)
                    tn = t * r
                    acc = acc + jnp.sum(dthat * tn, axis=0, keepdims=True)
                    dtn = dthat * w
                    o_ref[:, col:col + LANES] = (r * (dtn - tn * _head_mean(dtn * tn, m))).astype(bf16)
            accs.append(acc + pltpu.roll(acc, HEAD_DIM, axis=1))
        for gi in range(N_GROUPS):
            o_ref[:, E_V + gi * 512:E_V + (gi + 1) * 512] = dv_refs[gi][...].astype(bf16)
        o_ref[:, E_BG:E_CG] = dbg_ref[...]
        o_ref[:, E_CG:E_HB] = dcg_ref[...]
        o_ref[:, E_HB:E_Z] = dhb_ref[...]
        o_ref[:, E_Z:] = dz_ref[...]
        part = jnp.concatenate([accs[0], accs[1], jnp.zeros((6, LANES), f32)], axis=0)

        @pl.when(i == 0)
        def _():
            gw_ref[...] = part

        @pl.when(i > 0)
        def _():
            gw_ref[...] += part

    row = lambda w, c: pl.BlockSpec((tm, w), lambda i: (i, c))
    tab = row(LANES, 0)
    return pl.pallas_call(
        body, name="qk_bwd", grid=(S // tm,),
        in_specs=[row(512, 0)] * 9 + [row(3072, 0), tab, tab, tab, pl.BlockSpec((2, LANES), lambda i: (0, 0)),
                                      pl.BlockSpec((LANES, LANES), lambda i: (0, 0)),
                                      row(512, 0), row(512, 0), row(512, 0), row(1024, 0)],
        out_specs=[row(EVEN_IN, 0), pl.BlockSpec((8, LANES), lambda i: (0, 0))],
        out_shape=[jax.ShapeDtypeStruct((S, EVEN_IN), bf16), jax.ShapeDtypeStruct((8, LANES), f32)],
        compiler_params=_cp(("arbitrary",)),
    )(*dq_g, *dk_g, *dv_g, proj, *tabs, nw, hm, dbg, dcg, dhb, dz)


def _inv_count(i, tm, p):
    rowg = lax.broadcasted_iota(jnp.int32, (tm, 1), 0) + i * tm
    return 1.0 / jnp.minimum(rowg + 1, p).astype(f32)


def _layer_norm_stats(c):
    mu = jnp.mean(c, axis=-1, keepdims=True)
    cen = c - mu
    rstd = lax.rsqrt(jnp.mean(cen * cen, axis=-1, keepdims=True) + EPS)
    return cen * rstd, rstd


def _fill_pool_buf(i, ubuf, uc_ref, uch_ref):
    ubuf[0:16, :] = jnp.where(i > 0, uch_ref[...], 0.0)
    ubuf[16:, :] = uc_ref[...]


def _pooled(i, tm, ubuf, gi):
    p = POOL_SIZES[gi]
    cols = slice(gi * LANES, (gi + 1) * LANES)
    acc = ubuf[pl.ds(16, tm), cols]
    cur = acc
    for jj in range(1, p):
        acc = acc + ubuf[pl.ds(16 - jj, tm), cols]
    return acc * _inv_count(i, tm, p) - cur


def _fill_glu_buf(i, gbuf, da_ref, dg_ref, dah_ref, dgh_ref):
    gbuf[0:32, :] = jnp.where(i > 0, dah_ref[...] * _sigmoid(dgh_ref[...]), 0.0)
    gbuf[32:, :] = da_ref[...] * _sigmoid(dg_ref[...])


def _shift_copies(buf, sh, tm):
    for b in range(1, 8):
        sh[b - 1] = buf[pl.ds(b, tm + 24), :]


CONV_ROWS = 32


def _window(buf, sh, base, off, rows):
    b = off % 8
    if b == 0:
        return buf[pl.ds(base + off, rows), :]
    return sh[b - 1, pl.ds(base + (off - b), rows), :]


def _mix1_fwd(proj, pool_w, pool_scale, dconv_w, dconv_b, ln_w, ln_b):
    S = proj.shape[0]
    tm = _tile(S, 256)

    def body(uc_ref, uch_ref, da_ref, dg_ref, dah_ref, dgh_ref, za_ref, zb_ref, pw_ref, ps_ref, cw_ref, cb_ref,
             lw_ref, lb_ref, u_ref, c_ref, mc_ref, ubuf, gbuf, gsh):
        i = pl.program_id(0)
        _fill_pool_buf(i, ubuf, uc_ref, uch_ref)
        za = za_ref[...]
        for gi in range(4):
            cols = slice(gi * LANES, (gi + 1) * LANES)
            mc = jnp.dot(_pooled(i, tm, ubuf, gi).astype(bf16), pw_ref[gi], preferred_element_type=f32)
            mc_ref[:, cols] = mc
            zg = za[:, cols]
            u_ref[:, cols] = (mc * ps_ref[:, cols] * (zg * _sigmoid(zg))).astype(bf16)
        _fill_glu_buf(i, gbuf, da_ref, dg_ref, dah_ref, dgh_ref)
        _shift_copies(gbuf, gsh, tm)

        def conv_rows(ci, carry):
            base = pl.multiple_of(ci * CONV_ROWS, CONV_ROWS)
            acc = jnp.zeros((CONV_ROWS, 512), f32) + cb_ref[...]
            for k in range(D_CONV):
                acc = acc + cw_ref[k:k + 1, :] * _window(gbuf, gsh, base, 32 - (D_CONV - 1) + k, CONV_ROWS)
            c_ref[pl.ds(base, CONV_ROWS), :] = acc
            return carry

        lax.fori_loop(0, tm // CONV_ROWS, conv_rows, 0)
        c = c_ref[...]
        yhat, _ = _layer_norm_stats(c)
        l = yhat * lw_ref[...] + lb_ref[...]
        zb = zb_ref[...]
        u_ref[:, 512:] = (l * _sigmoid(l) * (zb * _sigmoid(zb))).astype(bf16)

    row = lambda w, c: pl.BlockSpec((tm, w), lambda i: (i, c))
    vec = pl.BlockSpec((1, 512), lambda i: (0, 0))
    return pl.pallas_call(
        body, name="mix1_fwd", grid=(S // tm,),
        in_specs=[row(512, 0), _prev_halo(tm, 16, 0), row(512, 1), row(512, 2), _prev_halo(tm, 32, 1), _prev_halo(tm, 32, 2),
                  row(512, 3), row(512, 4), pl.BlockSpec((4, LANES, LANES), lambda i: (0, 0, 0)), vec,
                  pl.BlockSpec((D_CONV, 512), lambda i: (0, 0)), vec, vec, vec],
        out_specs=[row(1024, 0), row(512, 0), row(512, 0)],
        out_shape=[jax.ShapeDtypeStruct((S, D_MODEL), bf16), jax.ShapeDtypeStruct((S, 512), f32),
                   jax.ShapeDtypeStruct((S, 512), f32)],
        scratch_shapes=[pltpu.VMEM((tm + 16, 512), f32), pltpu.VMEM((tm + 32, 512), f32),
                        pltpu.VMEM((7, tm + 24, 512), f32)],
        compiler_params=_cp(("parallel",)),
    )(proj, proj, proj, proj, proj, proj, proj, proj, pool_w, pool_scale, dconv_w, dconv_b, ln_w, ln_b)


def _mix1_bwd_a(du, proj, c, mc, pool_w, pool_scale, ln_w, ln_b):
    S = proj.shape[0]
    tm = _tile(S, 256)

    def body(du_ref, za_ref, zb_ref, c_ref, mc_ref, pw_ref, ps_ref, lw_ref, lb_ref,
             dz_ref, dc_ref, dpl_ref, dmc_ref, acc_ref):
        i = pl.program_id(0)
        du_v = du_ref[...]
        ps = ps_ref[...]
        za = za_ref[...]
        sga = _sigmoid(za)
        mcv = mc_ref[...]
        dz_ref[:, :512] = (du_v[:, :512] * (mcv * ps) * _dsilu(za, sga)).astype(bf16)
        dyc = du_v[:, :512] * (za * sga)
        g_ps = jnp.sum(dyc * mcv, axis=0, keepdims=True)
        dmc = (dyc * ps).astype(bf16)
        dmc_ref[...] = dmc
        for gi in range(4):
            cols = slice(gi * LANES, (gi + 1) * LANES)
            dpl_ref[:, cols] = lax.dot_general(dmc[:, cols], pw_ref[gi], (((1,), (1,)), ((), ())), preferred_element_type=f32)
        yhat, rstd = _layer_norm_stats(c_ref[...])
        lw = lw_ref[...]
        l = yhat * lw + lb_ref[...]
        sgl = _sigmoid(l)
        zb = zb_ref[...]
        sgb = _sigmoid(zb)
        dz_ref[:, 512:] = (du_v[:, 512:] * (l * sgl) * _dsilu(zb, sgb)).astype(bf16)
        dl = du_v[:, 512:] * (zb * sgb) * _dsilu(l, sgl)
        g_lb = jnp.sum(dl, axis=0, keepdims=True)
        g_lw = jnp.sum(dl * yhat, axis=0, keepdims=True)
        dyh = dl * lw
        dc = rstd * (dyh - jnp.mean(dyh, axis=-1, keepdims=True) - yhat * jnp.mean(dyh * yhat, axis=-1, keepdims=True))
        dc_ref[...] = dc
        g_db = jnp.sum(dc, axis=0, keepdims=True)
        part = jnp.concatenate([g_ps, g_lw, g_lb, g_db, jnp.zeros((4, 512), f32)], axis=0)

        @pl.when(i == 0)
        def _():
            acc_ref[...] = part

        @pl.when(i > 0)
        def _():
            acc_ref[...] += part

    row = lambda w, c_: pl.BlockSpec((tm, w), lambda i: (i, c_))
    vec = pl.BlockSpec((1, 512), lambda i: (0, 0))
    return pl.pallas_call(
        body, name="mix1_bwd_a", grid=(S // tm,),
        in_specs=[row(1024, 0), row(512, 3), row(512, 4), row(512, 0), row(512, 0),
                  pl.BlockSpec((4, LANES, LANES), lambda i: (0, 0, 0)), vec, vec, vec],
        out_specs=[row(1024, 0), row(512, 0), row(512, 0), row(512, 0), pl.BlockSpec((8, 512), lambda i: (0, 0))],
        out_shape=[jax.ShapeDtypeStruct((S, D_MODEL), bf16), jax.ShapeDtypeStruct((S, 512), f32),
                   jax.ShapeDtypeStruct((S, 512), f32), jax.ShapeDtypeStruct((S, 512), bf16),
                   jax.ShapeDtypeStruct((8, 512), f32)],
        compiler_params=_cp(("arbitrary",)),
    )(du, proj, proj, c, mc, pool_w, pool_scale, ln_w, ln_b)


def _mix1_bwd_b(dc, dpl, dmc, dz, proj, dconv_w):
    S = proj.shape[0]
    tm = _tile(S, 256)
    nt = S // tm

    def body(dc_ref, dcn_ref, dpl_ref, dpn_ref, dmc_ref, dz_ref, uc_ref, uch_ref, da_ref, dg_ref,
             cw_ref, o_ref, gcw_ref, gpw_ref, ubuf, dcbuf, dpbuf, dcsh, gacc):
        i = pl.program_id(0)
        last = i == nt - 1
        _fill_pool_buf(i, ubuf, uc_ref, uch_ref)
        dcbuf[0:tm, :] = dc_ref[...]
        dcbuf[tm:, :] = jnp.where(last, 0.0, dcn_ref[...])
        _shift_copies(dcbuf, dcsh, tm)
        dpl_v = dpl_ref[...]
        for gi in range(4):
            p = POOL_SIZES[gi]
            cols = slice(gi * LANES, (gi + 1) * LANES)
            dpbuf[0:tm, cols] = dpl_v[:, cols] * _inv_count(i, tm, p)
            dpbuf[tm:, cols] = jnp.where(last, 0.0, dpn_ref[:, cols] * (1.0 / p))
        gpw = []
        for gi in range(4):
            p = POOL_SIZES[gi]
            cols = slice(gi * LANES, (gi + 1) * LANES)
            acc = -dpl_v[:, cols]
            for jj in range(p):
                acc = acc + dpbuf[pl.ds(jj, tm), cols]
            o_ref[:, cols] = acc.astype(bf16)
            pooled = _pooled(i, tm, ubuf, gi).astype(bf16)
            gpw.append(lax.dot_general(pooled, dmc_ref[:, cols], (((0,), (0,)), ((), ())), preferred_element_type=f32))
        gacc[...] = jnp.zeros_like(gacc)

        def conv_rows(ci, carry):
            base = pl.multiple_of(ci * CONV_ROWS, CONV_ROWS)
            da = da_ref[pl.ds(base, CONV_ROWS), :]
            sg = _sigmoid(dg_ref[pl.ds(base, CONV_ROWS), :])
            gl = da * sg
            dgl = jnp.zeros((CONV_ROWS, 512), f32)
            for k in range(D_CONV):
                win = _window(dcbuf, dcsh, base, D_CONV - 1 - k, CONV_ROWS)
                dgl = dgl + cw_ref[k:k + 1, :] * win
                gacc[k] += jnp.sum((gl * win).reshape(CONV_ROWS // 8, 8, 512), axis=0)
            o_ref[pl.ds(base, CONV_ROWS), O_DA:O_DG] = (dgl * sg).astype(bf16)
            o_ref[pl.ds(base, CONV_ROWS), O_DG:O_Z] = (dgl * da * sg * (1.0 - sg)).astype(bf16)
            return carry

        lax.fori_loop(0, tm // CONV_ROWS, conv_rows, 0)
        o_ref[:, O_Z:] = dz_ref[...]
        gcw_part = jnp.concatenate(
            [jnp.sum(gacc[k], axis=0, keepdims=True) for k in range(D_CONV)] + [jnp.zeros((1, 512), f32)], axis=0)

        @pl.when(i == 0)
        def _():
            gcw_ref[...] = gcw_part
            for gi in range(4):
                gpw_ref[gi] = gpw[gi]

        @pl.when(i > 0)
        def _():
            gcw_ref[...] += gcw_part
            for gi in range(4):
                gpw_ref[gi] += gpw[gi]

    row = lambda w, c_: pl.BlockSpec((tm, w), lambda i: (i, c_))
    return pl.pallas_call(
        body, name="mix1_bwd_b", grid=(nt,),
        in_specs=[row(512, 0), _next_halo(tm, 32, 0, S), row(512, 0), _next_halo(tm, 16, 0, S), row(512, 0), row(1024, 0),
                  row(512, 0), _prev_halo(tm, 16, 0), row(512, 1), row(512, 2),
                  pl.BlockSpec((D_CONV, 512), lambda i: (0, 0))],
        out_specs=[row(ODD_IN, 0), pl.BlockSpec((32, 512), lambda i: (0, 0)),
                   pl.BlockSpec((4, LANES, LANES), lambda i: (0, 0, 0))],
        out_shape=[jax.ShapeDtypeStruct((S, ODD_IN), bf16), jax.ShapeDtypeStruct((32, 512), f32),
                   jax.ShapeDtypeStruct((4, LANES, LANES), f32)],
        scratch_shapes=[pltpu.VMEM((tm + 16, 512), f32), pltpu.VMEM((tm + 32, 512), f32),
                        pltpu.VMEM((tm + 16, 512), f32), pltpu.VMEM((7, tm + 24, 512), f32),
                        pltpu.VMEM((D_CONV, 8, 512), f32)],
        compiler_params=_cp(("arbitrary",)),
    )(dc, dc, dpl, dpl, dmc, dz, proj, proj, proj, proj, dconv_w)


_SMALL_LATE = ["e_q_norm_w", "e_k_norm_w", "e_conv_w", "o_norm_w", "o_pool_w", "o_pool_scale", "o_dconv_w", "o_dconv_b",
               "o_ln_w", "o_ln_b"]


def _local_step(x, pos_col, target, w, dist=None):
    hm = _head_mean_matrix()
    nw = jnp.concatenate([jnp.tile(w["e_q_norm_w"], (1, 2)), jnp.tile(w["e_k_norm_w"], (1, 2))], axis=0)
    tabs = _rope_tables(pos_col)
    pool_wb = w["o_pool_w"].astype(bf16)
    e_norm_w, e_w_in = w["e_norm_w"], w["e_w_in"]

    h0 = _rms_fwd(x, e_norm_w, name="rms0_fwd")
    if dist is None:
        proj0 = _mm_nn_resident(h0, e_w_in, name="in_proj0")
    else:
        proj0, gathered = _mm_nn_resident(h0, e_w_in, name="in_proj0", fuse=([], dist[0]))
        w = {**w, **dist[1](gathered)}
    e_conv_w, e_w_out, o_norm_w, o_w_in, o_w_out = w["e_conv_w"], w["e_w_out"], w["o_norm_w"], w["o_w_in"], w["o_w_out"]
    o_pool_scale, o_dconv_w, o_dconv_b, o_ln_w, o_ln_b = (w[k] for k in ("o_pool_scale", "o_dconv_w", "o_dconv_b", "o_ln_w", "o_ln_b"))
    qk = _qk_prep(proj0, tabs, nw, hm)
    o_g, lse_g = [], []
    for g in range(N_GROUPS):
        o, l = _attn_fwd_local(qk, proj0) if g == 0 else _attn_fwd_dil(qk, proj0, g, name=f"attn_fwd{g}")
        o_g.append(o)
        lse_g.append(l)
    u0, o_a, lt = _mix0_fwd(o_g, lse_g, proj0, e_conv_w)
    x1 = _mm_nn(u0, e_w_out, res=x, name="out_proj0")
    h1 = _rms_fwd(x1, o_norm_w, name="rms1_fwd")
    proj1 = _mm_nn(h1, o_w_in, name="in_proj1", tn=512)
    u1, c1, mc1 = _mix1_fwd(proj1, pool_wb, o_pool_scale, o_dconv_w, o_dconv_b, o_ln_w, o_ln_b)
    dy, dyb, loss = _mm_out_loss(u1, o_w_out, x1, target, name="out_proj1_loss")
    g_o_w_out = _mm_tn(u1, dyb, name="g_w_out1", out_dtype=bf16)
    du1 = _mm_nt(dyb, o_w_out, name="d_u1")
    dz1, dc1, dpl1, dmc1, sums1 = _mix1_bwd_a(du1, proj1, c1, mc1, pool_wb, o_pool_scale, o_ln_w, o_ln_b)
    dproj1, g_dconv_w, g_pool_w = _mix1_bwd_b(dc1, dpl1, dmc1, dz1, proj1, o_dconv_w)
    g_o_w_in = _mm_tn(h1, dproj1, name="g_w_in1", out_dtype=bf16)
    dh1 = _mm_nt(dproj1, o_w_in, name="d_h1")
    d1, d1b, g_o_norm = _rms_bwd(x1, o_norm_w, dh1, dy, name="rms1_bwd")
    g_e_w_out = _mm_tn(u0, d1b, name="g_w_out0", out_dtype=bf16)
    du0 = _mm_nt(d1b, e_w_out, name="d_u0")
    dz0, do_a, dsum, dbg, dcv = _mix0_bwd_a(du0, proj0, o_a, e_conv_w)
    dcg, dhb, g_conv_w = _mix0_bwd_b(dcv, proj0, e_conv_w)
    fuse_a = None if dist is None else (
        [g_e_w_out.reshape(N_DEV, D_MODEL // N_DEV, D_MODEL),
         jnp.moveaxis(g_o_w_in.reshape(D_MODEL, N_DEV, ODD_IN // N_DEV), 1, 0),
         g_o_w_out.reshape(N_DEV, D_MODEL // N_DEV, D_MODEL)], [])
    dq_g, dk_g, dv_g = [], [], []
    for g in range(N_GROUPS):
        if g == 0:
            dqkv = _attn_bwd_local(qk, proj0, do_a, lt, dsum, fuse=fuse_a)
            if dist is not None:
                dqkv, recv_a = dqkv
            dq, dk, dv = dqkv
        else:
            dq, dk, dv = _attn_bwd_dil(qk, proj0, do_a, lt, dsum, g, name=f"attn_bwd{g}")
        dq_g.append(dq)
        dk_g.append(dk)
        dv_g.append(dv)
    dproj0, g_qk_norm = _qk_bwd(dq_g, dk_g, dv_g, proj0, tabs, nw, hm, dbg, dcg, dhb, dz0)
    g_e_w_in = _mm_tn(h0, dproj0, name="g_w_in0", out_dtype=bf16, chunks=N_DEV)
    grads = dict(
        e_w_in=g_e_w_in, e_q_norm_w=g_qk_norm[0:1, :HEAD_DIM], e_k_norm_w=g_qk_norm[1:2, :HEAD_DIM],
        e_conv_w=g_conv_w[:SC_WIDTH], e_w_out=g_e_w_out,
        o_norm_w=g_o_norm, o_w_in=g_o_w_in, o_pool_w=g_pool_w,
        o_pool_scale=sums1[0:1], o_dconv_w=g_dconv_w[:D_CONV], o_dconv_b=sums1[3:4],
        o_ln_w=sums1[1:2], o_ln_b=sums1[2:3], o_w_out=g_o_w_out)
    if dist is None:
        dh0 = _mm_nt_resident(dproj0, e_w_in, name="d_h0")
        grad_x, _, grads["e_norm_w"] = _rms_bwd(x, e_norm_w, dh0, d1, name="rms0_bwd")
        return loss, grad_x, grads
    small_late, offs = _pack_rows([grads[n_] for n_ in _SMALL_LATE])
    dh0, recv_b = _mm_nt_resident(dproj0, e_w_in, name="d_h0", fuse=([g_e_w_in], [small_late]))
    grad_x, _, g_e_norm = _rms_bwd(x, e_norm_w, dh0, d1, name="rms0_bwd")
    recv_c = _exchange([], [g_e_norm.reshape(8, LANES)], name="exchange_e_norm")
    recv = dict(e_w_out=recv_a[0], o_w_in=recv_a[1], o_w_out=recv_a[2], e_w_in=recv_b[0], small_late=recv_b[1],
                e_norm_w=recv_c[0])
    return loss, grad_x, recv, {n_: (off, grads[n_].shape) for n_, off in zip(_SMALL_LATE, offs)}


_MESH_ID = pl.DeviceIdType.MESH
_HBM = pl.BlockSpec(memory_space=pl.ANY)


def _all_gather(arrs, *, name):
    n = len(arrs)

    def body(*refs):
        ins, outs = refs[:n], refs[n:2 * n]
        send_sems, recv_sems, local_sems = refs[2 * n:]
        x, y, c = _place()
        me, sibling = (x, y, c), (x, y, 1 - c)
        chips = [(1 - x, y), (x, 1 - y), (1 - x, 1 - y)]

        def slot(t, px, py, pc):
            return outs[t].at[4 * px + 2 * py + pc]

        def copy(t, k, block, to, src=None):
            dst = slot(t, *block)
            return pltpu.make_async_remote_copy(
                src_ref=dst if src is None else src, dst_ref=dst,
                send_sem=send_sems.at[7 * t + k], recv_sem=recv_sems.at[7 * t + k],
                device_id=to, device_id_type=_MESH_ID)

        mine = [pltpu.make_async_copy(ins[t], slot(t, *me), local_sems.at[t]) for t in range(n)]
        for cp in mine:
            cp.start()
        first = []
        for t in range(n):
            first.append(copy(t, 0, me, sibling, src=ins[t]))
            first += [copy(t, 1 + j, me, (*chip, c), src=ins[t]) for j, chip in enumerate(chips)]
        for cp in first:
            cp.start()
        passed = []
        for j, chip in enumerate(chips):
            for t in range(n):
                copy(t, 1 + j, (*chip, c), me).wait_recv()
                fwd = copy(t, 4 + j, (*chip, c), sibling)
                fwd.start()
                passed.append(fwd)
        for t in range(n):
            copy(t, 0, sibling, me).wait_recv()
            for j, chip in enumerate(chips):
                copy(t, 4 + j, (*chip, 1 - c), me).wait_recv()
        for cp in first + passed:
            cp.wait_send()
        for cp in mine:
            cp.wait()

    return pl.pallas_call(
        body, name=name,
        in_specs=[_HBM] * n, out_specs=[_HBM] * n,
        out_shape=[jax.ShapeDtypeStruct((N_DEV, *a.shape), a.dtype) for a in arrs],
        scratch_shapes=[pltpu.SemaphoreType.DMA((7 * n,)), pltpu.SemaphoreType.DMA((7 * n,)),
                        pltpu.SemaphoreType.DMA((n,))],
    )(*arrs)


def _exchange(chunked, whole, *, name):
    arrs = list(chunked) + list(whole)
    n = len(arrs)

    def body(*refs):
        start, wait = _exchange_plan(refs[:n], refs[n:2 * n], *refs[2 * n:], len(chunked))
        start()
        wait()

    return pl.pallas_call(
        body, name=name, in_specs=[_HBM] * n, out_specs=[_HBM] * n,
        out_shape=_exchange_out_shapes(chunked, whole), scratch_shapes=_exchange_sems(n),
    )(*arrs)


def _adamw(w, g, m, v):
    m2 = ADAM_B1 * m + (1.0 - ADAM_B1) * g
    v2 = ADAM_B2 * v + (1.0 - ADAM_B2) * (g * g)
    m_hat = m2 / (1.0 - ADAM_B1 ** ADAM_STEP)
    v_hat = v2 / (1.0 - ADAM_B2 ** ADAM_STEP)
    delta = -ADAM_LR * (m_hat / (jnp.sqrt(v_hat) + ADAM_EPS) + ADAM_WD * w)
    return delta, m2, v2


def _sum_adamw(parts, w, m, v, *, name):
    R, C = w.shape
    tr = _tile(R, 256)

    def body(p_ref, w_ref, m_ref, v_ref, g_ref, d_ref, nm_ref, nv_ref):
        g = p_ref[0].astype(f32)
        for i in range(1, N_DEV):
            g = g + p_ref[i].astype(f32)
        g_ref[...] = g
        d_ref[...], nm_ref[...], nv_ref[...] = _adamw(w_ref[...], g, m_ref[...], v_ref[...])

    spec = pl.BlockSpec((tr, C), lambda i: (i, 0))
    return pl.pallas_call(
        body, name=name, grid=(R // tr,),
        in_specs=[pl.BlockSpec((N_DEV, tr, C), lambda i: (0, i, 0)), spec, spec, spec],
        out_specs=[spec] * 4, out_shape=[jax.ShapeDtypeStruct((R, C), f32)] * 4,
        compiler_params=_cp(("parallel",)),
    )(parts, w, m, v)


def _sum_parts(parts, *, name):
    _, R, C = parts.shape

    def body(p_ref, o_ref):
        g = p_ref[0]
        for i in range(1, N_DEV):
            g = g + p_ref[i]
        o_ref[...] = g

    return pl.pallas_call(body, name=name, out_shape=jax.ShapeDtypeStruct((R, C), f32),
                          compiler_params=pltpu.CompilerParams(vmem_limit_bytes=VMEM_LIMIT))(parts)


def _adamw_small(ws, gs, ms, vs):
    n = len(ws)

    def body(*refs):
        w_r, g_r, m_r, v_r = refs[:n], refs[n:2 * n], refs[2 * n:3 * n], refs[3 * n:4 * n]
        d_r, nm_r, nv_r = refs[4 * n:5 * n], refs[5 * n:6 * n], refs[6 * n:7 * n]
        for t in range(n):
            d_r[t][...], nm_r[t][...], nv_r[t][...] = _adamw(w_r[t][...], g_r[t][...], m_r[t][...], v_r[t][...])

    shapes = [jax.ShapeDtypeStruct(w.shape, f32) for w in ws]
    outs = pl.pallas_call(body, name="adamw_small", out_shape=shapes * 3)(*ws, *gs, *ms, *vs)
    return outs[:n], outs[n:2 * n], outs[2 * n:]


_WEIGHTS = ["e_norm_w", "e_w_in", "e_q_norm_w", "e_k_norm_w", "e_conv_w", "e_w_out", "o_norm_w", "o_w_in", "o_pool_w",
            "o_pool_scale", "o_dconv_w", "o_dconv_b", "o_ln_w", "o_ln_b", "o_w_out"]
_BIG = ["e_w_in", "e_w_out", "o_w_in", "o_w_out"]
_SMALL_SHARDED = ["e_conv_w", "o_norm_w", "o_pool_scale", "o_dconv_w", "o_dconv_b", "o_ln_w", "o_ln_b"]
_SMALL_ALL = ["e_norm_w", "e_q_norm_w", "e_k_norm_w", "e_conv_w", "o_norm_w", "o_pool_w", "o_pool_scale", "o_dconv_w",
              "o_dconv_b", "o_ln_w", "o_ln_b"]


def _pack_rows(pieces):
    rows, offs, r0 = [], [], 0
    for p in pieces:
        flat = p.reshape(-1)
        nr = -(-flat.shape[0] // (8 * LANES)) * 8
        rows.append(jnp.pad(flat, (0, nr * LANES - flat.shape[0])).reshape(nr, LANES))
        offs.append((r0, nr))
        r0 += nr
    return jnp.concatenate(rows, axis=0), offs


def _unpack_rows(buf, off, shape):
    r0, nr = off
    size = int(np.prod(shape))
    return buf[..., r0:r0 + nr, :].reshape(*buf.shape[:-2], nr * LANES)[..., :size].reshape(*buf.shape[:-2], *shape)


def kernel(x, positions, e_norm_w, e_w_in, e_q_norm_w, e_k_norm_w, e_conv_w, e_w_out, o_norm_w, o_w_in, o_pool_w, o_pool_scale, o_dconv_w, o_dconv_b, o_ln_w, o_ln_b, o_w_out, loss_target, m_e_norm_w, m_e_w_in, m_e_q_norm_w, m_e_k_norm_w, m_e_conv_w, m_e_w_out, m_o_norm_w, m_o_w_in, m_o_pool_w, m_o_pool_scale, m_o_dconv_w, m_o_dconv_b, m_o_ln_w, m_o_ln_b, m_o_w_out, v_e_norm_w, v_e_w_in, v_e_q_norm_w, v_e_k_norm_w, v_e_conv_w, v_e_w_out, v_o_norm_w, v_o_w_in, v_o_pool_w, v_o_pool_scale, v_o_dconv_w, v_o_dconv_b, v_o_ln_w, v_o_ln_b, v_o_w_out):
    w = dict(e_norm_w=e_norm_w, e_w_in=e_w_in, e_q_norm_w=e_q_norm_w, e_k_norm_w=e_k_norm_w, e_conv_w=e_conv_w,
             e_w_out=e_w_out, o_norm_w=o_norm_w, o_w_in=o_w_in, o_pool_w=o_pool_w, o_pool_scale=o_pool_scale,
             o_dconv_w=o_dconv_w, o_dconv_b=o_dconv_b, o_ln_w=o_ln_w, o_ln_b=o_ln_b, o_w_out=o_w_out)
    m = dict(e_norm_w=m_e_norm_w, e_w_in=m_e_w_in, e_q_norm_w=m_e_q_norm_w, e_k_norm_w=m_e_k_norm_w, e_conv_w=m_e_conv_w,
             e_w_out=m_e_w_out, o_norm_w=m_o_norm_w, o_w_in=m_o_w_in, o_pool_w=m_o_pool_w, o_pool_scale=m_o_pool_scale,
             o_dconv_w=m_o_dconv_w, o_dconv_b=m_o_dconv_b, o_ln_w=m_o_ln_w, o_ln_b=m_o_ln_b, o_w_out=m_o_w_out)
    v = dict(e_norm_w=v_e_norm_w, e_w_in=v_e_w_in, e_q_norm_w=v_e_q_norm_w, e_k_norm_w=v_e_k_norm_w, e_conv_w=v_e_conv_w,
             e_w_out=v_e_w_out, o_norm_w=v_o_norm_w, o_w_in=v_o_w_in, o_pool_w=v_o_pool_w, o_pool_scale=v_o_pool_scale,
             o_dconv_w=v_o_dconv_w, o_dconv_b=v_o_dconv_b, o_ln_w=v_o_ln_w, o_ln_b=v_o_ln_b, o_w_out=v_o_w_out)
    S = x.shape[1]
    me = 4 * lax.axis_index("x") + 2 * lax.axis_index("y") + lax.axis_index("c")

    small_local, small_offs = _pack_rows([w[n_] for n_ in _SMALL_SHARDED])
    g_e_in, = _all_gather([w["e_w_in"][0].astype(bf16)], name="gather_e_w_in")
    rest_local = [w["e_w_out"][0].astype(bf16), w["o_w_in"][0].astype(bf16), w["o_w_out"][0].astype(bf16), small_local]

    def unpack_rest(gathered):
        g_e_out, g_o_in, g_o_out, g_small = gathered
        full = {}
        for n_, off in zip(_SMALL_SHARDED, small_offs):
            shard = _unpack_rows(g_small, off, w[n_].shape[1:])
            full[n_] = jnp.moveaxis(shard, 0, -2).reshape(*shard.shape[1:-1], N_DEV * shard.shape[-1])
        return dict(
            e_conv_w=full["e_conv_w"], e_w_out=g_e_out.reshape(D_MODEL, D_MODEL), o_norm_w=full["o_norm_w"].reshape(1, D_MODEL),
            o_w_in=jnp.moveaxis(g_o_in, 0, 1).reshape(D_MODEL, ODD_IN), o_pool_scale=full["o_pool_scale"].reshape(1, 512),
            o_dconv_w=full["o_dconv_w"], o_dconv_b=full["o_dconv_b"].reshape(1, 512), o_ln_w=full["o_ln_w"].reshape(1, 512),
            o_ln_b=full["o_ln_b"].reshape(1, 512), o_w_out=g_o_out.reshape(D_MODEL, D_MODEL))

    loss_blk, grad_x, recv, small_where = _local_step(
        x[0], positions.reshape(S, 1), loss_target[0],
        dict(e_norm_w=w["e_norm_w"], e_w_in=g_e_in, e_q_norm_w=w["e_q_norm_w"], e_k_norm_w=w["e_k_norm_w"],
             o_pool_w=w["o_pool_w"][0]),
        dist=(rest_local, unpack_rest))
    loss = lax.psum(loss_blk[0, 0], ("x", "y", "c"))

    out_g, out_d, out_m, out_v = {}, {}, {}, {}
    for n_ in _BIG:
        res = _sum_adamw(recv[n_], w[n_][0], m[n_][0], v[n_][0], name="adamw_" + n_)
        out_g[n_], out_d[n_], out_m[n_], out_v[n_] = [r[None] for r in res]
    small_sum = _sum_parts(recv["small_late"], name="sum_small_grads")
    e_norm_sum = _sum_parts(recv["e_norm_w"], name="sum_e_norm_grad")
    gs = []
    for n_ in _SMALL_ALL:
        if n_ == "e_norm_w":
            gs.append(e_norm_sum.reshape(w[n_].shape))
            continue
        off, shape = small_where[n_]
        gfull = _unpack_rows(small_sum, off, shape)
        if n_ in _SMALL_SHARDED:
            width = w[n_].shape[-1]
            gfull = lax.dynamic_slice_in_dim(gfull, me * width, width, axis=gfull.ndim - 1)
        gs.append(gfull.reshape(w[n_].shape))
    ds, nms, nvs = _adamw_small([w[n_] for n_ in _SMALL_ALL], gs, [m[n_] for n_ in _SMALL_ALL], [v[n_] for n_ in _SMALL_ALL])
    for n_, g_, d_, nm_, nv_ in zip(_SMALL_ALL, gs, ds, nms, nvs):
        out_g[n_], out_d[n_], out_m[n_], out_v[n_] = g_, d_, nm_, nv_

    return (loss, grad_x[None], *[out_g[n_] for n_ in _WEIGHTS], *[out_d[n_] for n_ in _WEIGHTS],
            *[out_m[n_] for n_ in _WEIGHTS], *[out_v[n_] for n_ in _WEIGHTS])
```

```python
import functools

import numpy as np
import jax
import jax.numpy as jnp
from jax import lax
from jax.experimental import pallas as pl
from jax.experimental.pallas import tpu as pltpu

f32 = jnp.float32
bf16 = jnp.bfloat16

D_MODEL = 1024
HEAD_DIM = 64
N_GROUPS = 3
DILATIONS = (1, 4, 16)
QBLK = 128
A_WIDTH = 512
EVEN_IN = 7168
ODD_IN = 2560
POOL_SIZES = (2, 4, 8, 16)
D_CONV = 31
SC_WIDTH = 3
ROT_HALF = 8
ROPE_THETA = 500000.0
EPS = 1e-6
NEG = -1e30
SCALE = HEAD_DIM ** -0.5
N_DEV = 8
LANES = 128
VMEM_LIMIT = 48 * 1024 * 1024

ADAM_LR = 0.001
ADAM_B1 = 0.9
ADAM_B2 = 0.999
ADAM_EPS = 1e-08
ADAM_WD = 0.01
ADAM_STEP = 10

E_Q, E_K, E_V, E_BG, E_CG, E_HB, E_Z = 0, 1536, 3072, 4608, 5120, 5632, 6144
O_UC, O_DA, O_DG, O_Z = 0, 512, 1024, 1536


def _cp(sem):
    return pltpu.CompilerParams(dimension_semantics=sem, vmem_limit_bytes=VMEM_LIMIT)


_HBM_ANY = pl.BlockSpec(memory_space=pl.ANY)


def _sigmoid(z):
    return 1.0 / (1.0 + jnp.exp(-z))


def _tile(n, pref):
    t = pref
    while n % t:
        t //= 2
    return t


def _place():
    return lax.axis_index("x"), lax.axis_index("y"), lax.axis_index("c")


def _exchange_plan(ins, outs, send_sems, recv_sems, local_sems, nc):
    n = len(ins)
    x, y, c = _place()
    me_i = 4 * x + 2 * y + c

    def src(t, dev_i):
        return ins[t].at[dev_i] if t < nc else ins[t]

    def copies(arriving):
        cps = []
        for m in range(1, N_DEV):
            px = 1 - x if m & 4 else x
            py = 1 - y if m & 2 else y
            pc = 1 - c if m & 1 else c
            peer_i = 4 * px + 2 * py + pc
            for t in range(n):
                cps.append(pltpu.make_async_remote_copy(
                    src_ref=src(t, peer_i), dst_ref=outs[t].at[peer_i if arriving else me_i],
                    send_sem=send_sems.at[7 * t + m - 1], recv_sem=recv_sems.at[7 * t + m - 1],
                    device_id=(x, y, c) if arriving else (px, py, pc), device_id_type=pl.DeviceIdType.MESH))
        return cps

    def mine():
        return [pltpu.make_async_copy(src(t, me_i), outs[t].at[me_i], local_sems.at[t]) for t in range(n)]

    def start():
        for cp in mine() + copies(False):
            cp.start()

    def wait():
        for cp in copies(True):
            cp.wait_recv()
        for cp in copies(False):
            cp.wait_send()
        for cp in mine():
            cp.wait()

    return start, wait


def _exchange_sems(n):
    return [pltpu.SemaphoreType.DMA((7 * n,)), pltpu.SemaphoreType.DMA((7 * n,)), pltpu.SemaphoreType.DMA((n,))]


def _exchange_out_shapes(chunked, whole):
    return ([jax.ShapeDtypeStruct(a.shape, a.dtype) for a in chunked]
            + [jax.ShapeDtypeStruct((N_DEV, *a.shape), a.dtype) for a in whole])


def _grid_call(body, *, name, grid, in_specs, out_specs, out_shape, scratch_shapes, sem, args, fuse=None):
    if fuse is None:
        return pl.pallas_call(body, name=name, grid=grid, in_specs=in_specs, out_specs=out_specs, out_shape=out_shape,
                              scratch_shapes=scratch_shapes, compiler_params=_cp(sem))(*args)
    chunked, whole = fuse
    ex = list(chunked) + list(whole)
    n, n_in, n_out, n_sc = len(ex), len(in_specs), len(out_specs), len(scratch_shapes)

    def fused(*refs):
        ins, ex_in = refs[:n_in], refs[n_in:n_in + n]
        outs, ex_out = refs[n_in + n:n_in + n + n_out], refs[n_in + n + n_out:n_in + 2 * n + n_out]
        scratch = refs[n_in + 2 * n + n_out:n_in + 2 * n + n_out + n_sc]
        start, wait = _exchange_plan(ex_in, ex_out, *refs[-3:], len(chunked))
        first = functools.reduce(jnp.logical_and, [pl.program_id(a) == 0 for a in range(len(grid))])
        last = functools.reduce(jnp.logical_and, [pl.program_id(a) == g - 1 for a, g in enumerate(grid)])
        pl.when(first)(start)
        body(*ins, *outs, *scratch)
        pl.when(last)(wait)

    res = pl.pallas_call(
        fused, name=name, grid=grid, in_specs=list(in_specs) + [_HBM_ANY] * n,
        out_specs=list(out_specs) + [_HBM_ANY] * n, out_shape=list(out_shape) + _exchange_out_shapes(chunked, whole),
        scratch_shapes=list(scratch_shapes) + _exchange_sems(n),
        compiler_params=_cp(("arbitrary",) * len(grid)))(*args, *ex)
    return res[:n_out], res[n_out:]


def _mm_nn(a, b, *, name, out_dtype=f32, res=None, tn=1024, fuse=None):
    M, K = a.shape
    tm = _tile(M, 1024)
    if b.ndim == 3:
        tn = b.shape[2]
        N = b.shape[0] * tn
        b_spec = pl.BlockSpec((None, K, tn), lambda i, j: (j, 0, 0))
    else:
        N = b.shape[1]
        tn = _tile(N, tn)
        b_spec = pl.BlockSpec((K, tn), lambda i, j: (0, j))

    def body(*refs):
        if res is None:
            a_ref, b_ref, o_ref = refs
        else:
            a_ref, b_ref, r_ref, o_ref = refs
        acc = jnp.dot(a_ref[...], b_ref[...], preferred_element_type=f32)
        if res is not None:
            acc = acc + r_ref[...]
        o_ref[...] = acc.astype(out_dtype)

    in_specs = [pl.BlockSpec((tm, K), lambda i, j: (i, 0)), b_spec]
    args = [a, b]
    if res is not None:
        in_specs.append(pl.BlockSpec((tm, tn), lambda i, j: (i, j)))
        args.append(res)
    out = _grid_call(
        body, name=name, grid=(M // tm, N // tn), in_specs=in_specs,
        out_specs=[pl.BlockSpec((tm, tn), lambda i, j: (i, j))],
        out_shape=[jax.ShapeDtypeStruct((M, N), out_dtype)], scratch_shapes=[],
        sem=("parallel", "parallel"), args=args, fuse=fuse)
    return out[0] if fuse is None else (out[0][0], out[1])


def _mm_nt(a, b, *, name, out_dtype=f32, fuse=None):
    M, K = a.shape
    tm = _tile(M, 1024)
    if b.ndim == 3:
        nk, N, tk = b.shape
        b_spec = pl.BlockSpec((None, N, tk), lambda i, k: (k, 0, 0))
    else:
        N = b.shape[0]
        tk = _tile(K, 1024) if K % 1024 == 0 else _tile(K, 512)
        nk = K // tk
        b_spec = pl.BlockSpec((N, tk), lambda i, k: (0, k))

    def body(a_ref, b_ref, o_ref, acc_ref):
        k = pl.program_id(1)
        part = lax.dot_general(a_ref[...], b_ref[...], (((1,), (1,)), ((), ())), preferred_element_type=f32)

        @pl.when(k == 0)
        def _():
            acc_ref[...] = part

        @pl.when(k > 0)
        def _():
            acc_ref[...] += part

        @pl.when(k == nk - 1)
        def _():
            o_ref[...] = acc_ref[...].astype(out_dtype)

    out = _grid_call(
        body, name=name, grid=(M // tm, nk),
        in_specs=[pl.BlockSpec((tm, tk), lambda i, k: (i, k)), b_spec],
        out_specs=[pl.BlockSpec((tm, N), lambda i, k: (i, 0))],
        out_shape=[jax.ShapeDtypeStruct((M, N), out_dtype)],
        scratch_shapes=[pltpu.VMEM((tm, N), f32)],
        sem=("parallel", "arbitrary"), args=[a, b], fuse=fuse)
    return out[0] if fuse is None else (out[0][0], out[1])


def _load_once(src_hbm, dst_vmem, sem):
    @pl.when(pl.program_id(0) == 0)
    def _():
        cp = pltpu.make_async_copy(src_hbm, dst_vmem, sem)
        cp.start()
        cp.wait()


def _mm_nn_resident(a, b, *, name, fuse=None):
    M, K = a.shape
    nch, _, tn = b.shape
    tm = _tile(M, 256)

    def body(a_ref, b_hbm, o_ref, bbuf, sem):
        _load_once(b_hbm, bbuf, sem)
        av = a_ref[...]
        for j in range(nch):
            o_ref[:, j * tn:(j + 1) * tn] = jnp.dot(av, bbuf[j], preferred_element_type=f32)

    out = _grid_call(
        body, name=name, grid=(M // tm,), in_specs=[pl.BlockSpec((tm, K), lambda i: (i, 0)), _HBM_ANY],
        out_specs=[pl.BlockSpec((tm, nch * tn), lambda i: (i, 0))],
        out_shape=[jax.ShapeDtypeStruct((M, nch * tn), f32)],
        scratch_shapes=[pltpu.VMEM(b.shape, b.dtype), pltpu.SemaphoreType.DMA],
        sem=("arbitrary",), args=[a, b], fuse=fuse)
    return out[0] if fuse is None else (out[0][0], out[1])


def _mm_nt_resident(a, b, *, name, fuse=None):
    M, K = a.shape
    nch, N, tk = b.shape
    tm = _tile(M, 512)

    def body(a_ref, b_hbm, o_ref, bbuf, sem):
        _load_once(b_hbm, bbuf, sem)
        acc = None
        for k in range(nch):
            part = lax.dot_general(a_ref[:, k * tk:(k + 1) * tk], bbuf[k], (((1,), (1,)), ((), ())),
                                   preferred_element_type=f32)
            acc = part if acc is None else acc + part
        o_ref[...] = acc

    out = _grid_call(
        body, name=name, grid=(M // tm,), in_specs=[pl.BlockSpec((tm, K), lambda i: (i, 0)), _HBM_ANY],
        out_specs=[pl.BlockSpec((tm, N), lambda i: (i, 0))],
        out_shape=[jax.ShapeDtypeStruct((M, N), f32)],
        scratch_shapes=[pltpu.VMEM(b.shape, b.dtype), pltpu.SemaphoreType.DMA],
        sem=("arbitrary",), args=[a, b], fuse=fuse)
    return out[0] if fuse is None else (out[0][0], out[1])


def _mm_tn(a, b, *, name, out_dtype=f32, tn=512, chunks=None, a_cols=None, fuse=None):
    S, Ka = a.shape
    a_blk = 0
    if a_cols is not None:
        a_blk, Ka = a_cols
    N = b.shape[1]
    ts = _tile(S, 2048)
    ns = S // ts
    if chunks:
        tn = N // chunks
        out_spec = pl.BlockSpec((None, Ka, tn), lambda j, s: (j, 0, 0))
        out_shape = jax.ShapeDtypeStruct((chunks, Ka, tn), out_dtype)
    else:
        tn = _tile(N, tn)
        out_spec = pl.BlockSpec((Ka, tn), lambda j, s: (0, j))
        out_shape = jax.ShapeDtypeStruct((Ka, N), out_dtype)

    def body(a_ref, b_ref, o_ref, acc_ref):
        s = pl.program_id(1)
        part = lax.dot_general(a_ref[...], b_ref[...], (((0,), (0,)), ((), ())), preferred_element_type=f32)

        @pl.when(s == 0)
        def _():
            acc_ref[...] = part

        @pl.when(s > 0)
        def _():
            acc_ref[...] += part

        @pl.when(s == ns - 1)
        def _():
            o_ref[...] = acc_ref[...].astype(out_dtype)

    out = _grid_call(
        body, name=name, grid=(N // tn, ns),
        in_specs=[pl.BlockSpec((ts, Ka), lambda j, s: (s, a_blk)), pl.BlockSpec((ts, tn), lambda j, s: (s, j))],
        out_specs=[out_spec], out_shape=[out_shape],
        scratch_shapes=[pltpu.VMEM((Ka, tn), f32)],
        sem=("parallel", "arbitrary"), args=[a, b], fuse=fuse)
    return out[0] if fuse is None else (out[0][0], out[1])


def _mm_out_loss(u, w, x_res, target, *, name):
    M, K = u.shape
    N = w.shape[1]
    tm = _tile(M, 512)
    nm = M // tm

    def body(u_ref, w_ref, x_ref, t_ref, dy_ref, dyb_ref, loss_ref, acc_ref):
        i = pl.program_id(0)
        y = jnp.dot(u_ref[...], w_ref[...], preferred_element_type=f32) + x_ref[...]
        err = y - t_ref[...]
        dy = err * (1.0 / N)
        dy_ref[...] = dy
        dyb_ref[...] = dy.astype(bf16)
        part = jnp.sum(err * err, axis=0, keepdims=True)

        @pl.when(i == 0)
        def _():
            acc_ref[...] = part

        @pl.when(i > 0)
        def _():
            acc_ref[...] += part

        @pl.when(i == nm - 1)
        def _():
            tot = jnp.sum(acc_ref[...], axis=1, keepdims=True)
            loss_ref[...] = jnp.broadcast_to(tot * (0.5 / N), (8, LANES))

    return pl.pallas_call(
        body, name=name, grid=(nm,),
        in_specs=[pl.BlockSpec((tm, K), lambda i: (i, 0)), pl.BlockSpec((K, N), lambda i: (0, 0)),
                  pl.BlockSpec((tm, N), lambda i: (i, 0)), pl.BlockSpec((tm, N), lambda i: (i, 0))],
        out_specs=[pl.BlockSpec((tm, N), lambda i: (i, 0)), pl.BlockSpec((tm, N), lambda i: (i, 0)),
                   pl.BlockSpec((8, LANES), lambda i: (0, 0))],
        out_shape=[jax.ShapeDtypeStruct((M, N), f32), jax.ShapeDtypeStruct((M, N), bf16),
                   jax.ShapeDtypeStruct((8, LANES), f32)],
        scratch_shapes=[pltpu.VMEM((1, N), f32)],
        compiler_params=_cp(("arbitrary",)),
    )(u, w, x_res, target)


def _rms_fwd(x, w, *, name):
    S, Dm = x.shape
    tm = _tile(S, 1024)

    def body(x_ref, w_ref, h_ref):
        xv = x_ref[...]
        r = lax.rsqrt(jnp.mean(xv * xv, axis=-1, keepdims=True) + EPS)
        h_ref[...] = (xv * r * w_ref[...]).astype(bf16)

    return pl.pallas_call(
        body, name=name, grid=(S // tm,),
        in_specs=[pl.BlockSpec((tm, Dm), lambda i: (i, 0)), pl.BlockSpec((1, Dm), lambda i: (0, 0))],
        out_specs=pl.BlockSpec((tm, Dm), lambda i: (i, 0)),
        out_shape=jax.ShapeDtypeStruct((S, Dm), bf16),
        compiler_params=_cp(("parallel",)),
    )(x, w)


def _rms_bwd(x, w, dh, res, *, name):
    S, Dm = x.shape
    tm = _tile(S, 512)

    def body(x_ref, w_ref, dh_ref, res_ref, dx_ref, dxb_ref, gw_ref):
        i = pl.program_id(0)
        xv = x_ref[...]
        r = lax.rsqrt(jnp.mean(xv * xv, axis=-1, keepdims=True) + EPS)
        xn = xv * r
        dh_v = dh_ref[...]
        dxn = dh_v * w_ref[...]
        dx = r * (dxn - xn * jnp.mean(dxn * xn, axis=-1, keepdims=True)) + res_ref[...]
        dx_ref[...] = dx
        dxb_ref[...] = dx.astype(bf16)
        part = jnp.sum(dh_v * xn, axis=0, keepdims=True)

        @pl.when(i == 0)
        def _():
            gw_ref[...] = part

        @pl.when(i > 0)
        def _():
            gw_ref[...] += part

    dx, dxb, gw = pl.pallas_call(
        body, name=name, grid=(S // tm,),
        in_specs=[pl.BlockSpec((tm, Dm), lambda i: (i, 0)), pl.BlockSpec((1, Dm), lambda i: (0, 0)),
                  pl.BlockSpec((tm, Dm), lambda i: (i, 0)), pl.BlockSpec((tm, Dm), lambda i: (i, 0))],
        out_specs=[pl.BlockSpec((tm, Dm), lambda i: (i, 0)), pl.BlockSpec((tm, Dm), lambda i: (i, 0)),
                   pl.BlockSpec((1, Dm), lambda i: (0, 0))],
        out_shape=[jax.ShapeDtypeStruct((S, Dm), f32), jax.ShapeDtypeStruct((S, Dm), bf16),
                   jax.ShapeDtypeStruct((1, Dm), f32)],
        compiler_params=_cp(("arbitrary",)),
    )(x, w, dh, res)
    return dx, dxb, gw


_INV_FREQ = [float(v) for v in (np.float32(ROPE_THETA) ** (-np.arange(ROT_HALF, dtype=np.float32) / np.float32(ROT_HALF))).astype(np.float32)]


def _rope_tables(pos_col):
    S = pos_col.shape[0]
    tm = _tile(S, 1024)

    def body(p_ref, c_ref, s1_ref, s2_ref):
        lane = lax.broadcasted_iota(jnp.int32, (tm, LANES), 1)
        lm = lane % HEAD_DIM
        fi = lm % ROT_HALF
        inv = jnp.zeros((tm, LANES), f32)
        for k in range(ROT_HALF):
            inv = jnp.where(fi == k, _INV_FREQ[k], inv)
        ang = p_ref[...].astype(f32) * inv
        cs = jnp.cos(ang)
        sn = jnp.sin(ang)
        c_ref[...] = jnp.where(lm < 2 * ROT_HALF, cs, 1.0)
        s1_ref[...] = jnp.where((lm >= ROT_HALF) & (lm < 2 * ROT_HALF), sn, 0.0)
        s2_ref[...] = jnp.where(lm < ROT_HALF, -sn, 0.0)

    spec = pl.BlockSpec((tm, LANES), lambda i: (i, 0))
    return pl.pallas_call(
        body, name="rope_tables", grid=(S // tm,),
        in_specs=[pl.BlockSpec((tm, 1), lambda i: (i, 0))],
        out_specs=[spec, spec, spec],
        out_shape=[jax.ShapeDtypeStruct((S, LANES), f32)] * 3,
        compiler_params=_cp(("parallel",)),
    )(pos_col)


def _head_mean(v, m):
    hi = v.astype(bf16)
    lo = (v - hi.astype(f32)).astype(bf16)
    return jnp.dot(hi, m, preferred_element_type=f32) + jnp.dot(lo, m, preferred_element_type=f32)


def _head_mean_matrix():
    i = np.arange(LANES)
    return jnp.asarray(((i[:, None] // HEAD_DIM) == (i[None, :] // HEAD_DIM)).astype(np.float32) / HEAD_DIM, dtype=bf16)


def _in_proj0(h, b, tabs, nw, hm, *, fuse=None):
    M, K = h.shape
    nch, _, tn = b.shape
    tm = _tile(M, 256)

    def body(a_ref, b_hbm, c_ref, s1_ref, s2_ref, nw_ref, m_ref, o_ref, qk_ref, bbuf, sem):
        _load_once(b_hbm, bbuf, sem)
        av = a_ref[...]
        c, s1, s2, m = c_ref[...], s1_ref[...], s2_ref[...], m_ref[...]
        for j in range(nch):
            res = jnp.dot(av, bbuf[j], preferred_element_type=f32)
            o_ref[:, j * tn:(j + 1) * tn] = res
            for p in range(tn // LANES):
                col = j * tn + p * LANES
                if col >= E_V:
                    continue
                w = nw_ref[0:1, :] if col < E_K else nw_ref[1:2, :]
                t = res[:, p * LANES:(p + 1) * LANES]
                that = t * lax.rsqrt(_head_mean(t * t, m) + EPS) * w
                qk_ref[:, col:col + LANES] = (
                    that * c + pltpu.roll(that, ROT_HALF, axis=1) * s1 + pltpu.roll(that, LANES - ROT_HALF, axis=1) * s2)

    tab = pl.BlockSpec((tm, LANES), lambda i: (i, 0))
    out = _grid_call(
        body, name="in_proj0", grid=(M // tm,),
        in_specs=[pl.BlockSpec((tm, K), lambda i: (i, 0)), _HBM_ANY, tab, tab, tab,
                  pl.BlockSpec((2, LANES), lambda i: (0, 0)), pl.BlockSpec((LANES, LANES), lambda i: (0, 0))],
        out_specs=[pl.BlockSpec((tm, nch * tn), lambda i: (i, 0)), pl.BlockSpec((tm, E_V), lambda i: (i, 0))],
        out_shape=[jax.ShapeDtypeStruct((M, nch * tn), f32), jax.ShapeDtypeStruct((M, E_V), f32)],
        scratch_shapes=[pltpu.VMEM(b.shape, b.dtype), pltpu.SemaphoreType.DMA],
        sem=("arbitrary",), args=[h, b, *tabs, nw, hm], fuse=fuse)
    return out if fuse is None else (*out[0], out[1])


def _key_geometry(nparts):
    qr = QBLK // nparts
    rho = lax.broadcasted_iota(jnp.int32, (2 * QBLK, 2 * QBLK), 0) % QBLK
    kap = lax.broadcasted_iota(jnp.int32, (2 * QBLK, 2 * QBLK), 1)
    n_q = QBLK + nparts * (rho % qr) + rho // qr
    tt = kap % (2 * qr)
    n_k = nparts * tt + kap // (2 * qr)
    dist = n_q - n_k
    return (dist >= 0) & (dist <= QBLK), (tt < qr).astype(jnp.int32)


def _stack_heads(t, lo):
    zero = jnp.zeros_like(t)
    return jnp.concatenate([jnp.where(lo, t, zero), jnp.where(lo, zero, t)], axis=0)


def _attn_block_fwd(qb, kcat, vcat, mask, lo):
    s = lax.dot_general(_stack_heads(qb, lo), kcat, (((1,), (1,)), ((), ())), preferred_element_type=f32) * SCALE
    s = jnp.where(mask, s, NEG)
    mx = jnp.max(s, axis=-1, keepdims=True)
    pexp = jnp.exp(s - mx)
    den = jnp.sum(pexp, axis=-1, keepdims=True)
    pn = (pexp * (1.0 / den)).astype(bf16)
    o2 = jnp.dot(pn, vcat, preferred_element_type=f32)
    lse2 = jnp.broadcast_to(mx + jnp.log(den), (2 * QBLK, LANES))
    return jnp.where(lo, o2[:QBLK], o2[QBLK:]), jnp.where(lo, lse2[:QBLK], lse2[QBLK:])


def _attn_block_bwd(qb, dob, kcat, vcat, lt, ds, mask, lo):
    lt_sw = pltpu.roll(lt, HEAD_DIM, axis=1)
    ds_sw = pltpu.roll(ds, HEAD_DIM, axis=1)
    lt2 = jnp.concatenate([jnp.where(lo, lt, lt_sw), jnp.where(lo, lt_sw, lt)], axis=0)
    ds2 = jnp.concatenate([jnp.where(lo, ds, ds_sw), jnp.where(lo, ds_sw, ds)], axis=0)
    q2 = _stack_heads(qb, lo)
    do2 = _stack_heads(dob, lo)
    s = lax.dot_general(q2, kcat, (((1,), (1,)), ((), ())), preferred_element_type=f32) * SCALE
    s = jnp.where(mask, s, NEG)
    prob = jnp.exp(s - jnp.concatenate([lt2, lt2], axis=1))
    dp = lax.dot_general(do2, vcat, (((1,), (1,)), ((), ())), preferred_element_type=f32)
    dsb = (prob * (dp - jnp.concatenate([ds2, ds2], axis=1)) * SCALE).astype(bf16)
    dq2 = jnp.dot(dsb, kcat, preferred_element_type=f32)
    dk = lax.dot_general(dsb, q2, (((0,), (0,)), ((), ())), preferred_element_type=f32)
    dv = lax.dot_general(prob.astype(bf16), do2, (((0,), (0,)), ((), ())), preferred_element_type=f32)
    return jnp.where(lo, dq2[:QBLK], dq2[QBLK:]), dk, dv


ATT_ROWS = 1024


def _attn_fwd_local(qk, proj):
    S = qk.shape[0]
    tr = _tile(S, ATT_ROWS)
    lw = 4 * LANES
    nb = tr // QBLK

    def body(q_ref, k_ref, kh_ref, v_ref, vh_ref, o_ref, lse_ref, kbuf, vbuf):
        j = pl.program_id(0)
        kbuf[0:QBLK, :] = jnp.where(j > 0, kh_ref[...], 0.0)
        kbuf[QBLK:, :] = k_ref[...]
        vbuf[0:QBLK, :] = jnp.where(j > 0, vh_ref[...], 0.0)
        vbuf[QBLK:, :] = v_ref[...]
        band, is_prev = _key_geometry(1)
        lo = lax.broadcasted_iota(jnp.int32, (QBLK, LANES), 1) < HEAD_DIM

        def blk(c, carry):
            r0 = pl.multiple_of(c * QBLK, QBLK)
            first = jnp.where((c == 0) & (j == 0), 1, 0)
            mask = band & (is_prev * first == 0)
            for pp in range(lw // LANES):
                lanes = slice(pp * LANES, (pp + 1) * LANES)
                o, lse = _attn_block_fwd(q_ref[pl.ds(r0, QBLK), lanes].astype(bf16),
                                         kbuf[pl.ds(r0, 2 * QBLK), lanes].astype(bf16),
                                         vbuf[pl.ds(r0, 2 * QBLK), lanes].astype(bf16), mask, lo)
                o_ref[pl.ds(r0, QBLK), lanes] = o
                lse_ref[pl.ds(r0, QBLK), lanes] = lse
            return carry

        lax.fori_loop(0, nb, blk, 0)

    def halo(col):
        return pl.BlockSpec((QBLK, lw), lambda j, l: (jnp.maximum(j * nb - 1, 0), col + l))

    def tile(col):
        return pl.BlockSpec((tr, lw), lambda j, l: (j, col + l))

    return pl.pallas_call(
        body, name="attn_fwd0", grid=(S // tr, A_WIDTH // lw),
        in_specs=[tile(E_Q // lw), tile(E_K // lw), halo(E_K // lw), tile(E_V // lw), halo(E_V // lw)],
        out_specs=[tile(0), tile(0)],
        out_shape=[jax.ShapeDtypeStruct((S, A_WIDTH), f32)] * 2,
        scratch_shapes=[pltpu.VMEM((QBLK + tr, lw), f32)] * 2,
        compiler_params=_cp(("parallel", "parallel")),
    )(qk, qk, qk, proj, proj)


def _attn_bwd_local(qk, proj, do_a, lt, dsum, fuse=None):
    S = qk.shape[0]
    tr = _tile(S, ATT_ROWS)
    lw = 2 * LANES
    nb = tr // QBLK
    nt = S // tr

    def body(q_ref, qn_ref, do_ref, don_ref, lt_ref, ltn_ref, ds_ref, dsn_ref, k_ref, kh_ref, v_ref, vh_ref,
             dq_ref, dk_ref, dv_ref, kbuf, vbuf, dkbuf, dvbuf):
        j = pl.program_id(0)
        zeros = jnp.zeros((QBLK, lw), f32)
        kbuf[0:QBLK, :] = jnp.where(j > 0, kh_ref[...], 0.0)
        kbuf[pl.ds(QBLK, tr), :] = k_ref[...]
        kbuf[pl.ds(QBLK + tr, QBLK), :] = zeros
        vbuf[0:QBLK, :] = jnp.where(j > 0, vh_ref[...], 0.0)
        vbuf[pl.ds(QBLK, tr), :] = v_ref[...]
        vbuf[pl.ds(QBLK + tr, QBLK), :] = zeros
        dkbuf[...] = jnp.zeros_like(dkbuf)
        dvbuf[...] = jnp.zeros_like(dvbuf)
        band, is_prev = _key_geometry(1)
        lo = lax.broadcasted_iota(jnp.int32, (QBLK, LANES), 1) < HEAD_DIM

        def blk(c, carry):
            r0 = pl.multiple_of(c * QBLK, QBLK)
            first = jnp.where((c == 0) & (j == 0), 1, 0)
            mask = band & (is_prev * first == 0)
            for pp in range(lw // LANES):
                lanes = slice(pp * LANES, (pp + 1) * LANES)
                dq, dk, dv = _attn_block_bwd(
                    q_ref[pl.ds(r0, QBLK), lanes].astype(bf16), do_ref[pl.ds(r0, QBLK), lanes].astype(bf16),
                    kbuf[pl.ds(r0, 2 * QBLK), lanes].astype(bf16), vbuf[pl.ds(r0, 2 * QBLK), lanes].astype(bf16),
                    lt_ref[pl.ds(r0, QBLK), lanes], ds_ref[pl.ds(r0, QBLK), lanes], mask, lo)
                dq_ref[pl.ds(r0, QBLK), lanes] = dq
                dkbuf[pl.ds(r0, 2 * QBLK), lanes] += dk
                dvbuf[pl.ds(r0, 2 * QBLK), lanes] += dv
            return carry

        lax.fori_loop(0, nb, blk, 0)

        @pl.when(j < nt - 1)
        def _():
            mask = band & (is_prev == 1)
            for pp in range(lw // LANES):
                lanes = slice(pp * LANES, (pp + 1) * LANES)
                _, dk, dv = _attn_block_bwd(
                    qn_ref[:, lanes].astype(bf16), don_ref[:, lanes].astype(bf16),
                    kbuf[pl.ds(tr, 2 * QBLK), lanes].astype(bf16), vbuf[pl.ds(tr, 2 * QBLK), lanes].astype(bf16),
                    ltn_ref[:, lanes], dsn_ref[:, lanes], mask, lo)
                dkbuf[pl.ds(tr, 2 * QBLK), lanes] += dk
                dvbuf[pl.ds(tr, 2 * QBLK), lanes] += dv

        dk_ref[...] = dkbuf[pl.ds(QBLK, tr), :]
        dv_ref[...] = dvbuf[pl.ds(QBLK, tr), :]

    def prev_halo(col):
        return pl.BlockSpec((QBLK, lw), lambda j, l: (jnp.maximum(j * nb - 1, 0), col + l))

    def next_halo(col):
        return pl.BlockSpec((QBLK, lw), lambda j, l: (jnp.minimum((j + 1) * nb, S // QBLK - 1), col + l))

    def tile(col):
        return pl.BlockSpec((tr, lw), lambda j, l: (j, col + l))

    return _grid_call(
        body, name="attn_bwd0", grid=(nt, A_WIDTH // lw),
        in_specs=[tile(E_Q // lw), next_halo(E_Q // lw), tile(0), next_halo(0), tile(0), next_halo(0), tile(0), next_halo(0),
                  tile(E_K // lw), prev_halo(E_K // lw), tile(E_V // lw), prev_halo(E_V // lw)],
        out_specs=[tile(0)] * 3,
        out_shape=[jax.ShapeDtypeStruct((S, A_WIDTH), f32)] * 3,
        scratch_shapes=[pltpu.VMEM((tr + 2 * QBLK, lw), f32)] * 4,
        sem=("parallel", "parallel"), args=[qk, qk, do_a, do_a, lt, lt, dsum, dsum, qk, qk, proj, proj], fuse=fuse)


def _stream_view(a, d):
    S, W = a.shape
    return a.reshape(S // 8, 8, W) if d == 4 else a.reshape(S // 16, 2, 8, W)


def _stream_ref(ref, d, r, part, col, lw):
    n = ref.shape[0]
    if d == 4:
        return ref.at[pl.ds(0, n), r + 4 * part, pl.ds(col, lw)]
    return ref.at[pl.ds(0, n), r // 8, r % 8, pl.ds(col, lw)]


def _stream_geometry(S, d):
    nparts = 2 if d == 4 else 1
    rows = S // (d * nparts)
    return nparts, rows, QBLK // nparts


def _attn_fwd_dil(qk, proj, g, *, name):
    S = qk.shape[0]
    d = DILATIONS[g]
    nparts, rows, qr = _stream_geometry(S, d)
    nb = rows // qr
    lw = 2 * LANES if d == 4 else 4 * LANES
    nlg = A_WIDTH // lw
    nitems = d * nlg
    ins = ((0, E_Q + A_WIDTH * g, 0), (0, E_K + A_WIDTH * g, qr), (1, E_V + A_WIDTH * g, qr))

    def body(qk_hbm, pj_hbm, o_hbm, l_hbm, qbuf, kbuf, vbuf, obuf, lbuf, in_sems, out_sems):
        i = pl.program_id(0)
        slot = i % 2
        hbm_in = (qk_hbm, pj_hbm)
        bufs_in = (qbuf, kbuf, vbuf)

        def in_copies(item, sl):
            r, lg = item // nlg, item % nlg
            cps = []
            for a in range(nparts):
                for t, (src, col, pad) in enumerate(ins):
                    cps.append(pltpu.make_async_copy(
                        _stream_ref(hbm_in[src], d, r, a, pl.multiple_of(col + lw * lg, LANES), lw),
                        bufs_in[t].at[sl, a, pl.ds(pad, rows), :], in_sems.at[sl, 3 * a + t]))
            return cps

        def out_copies(item, sl):
            r, lg = item // nlg, item % nlg
            cps = []
            for a in range(nparts):
                for t, (buf, dst) in enumerate(((obuf, o_hbm), (lbuf, l_hbm))):
                    cps.append(pltpu.make_async_copy(
                        buf.at[sl, a], _stream_ref(dst, d, r, a, pl.multiple_of(lw * lg, LANES), lw),
                        out_sems.at[sl, 2 * a + t]))
            return cps

        @pl.when(i == 0)
        def _():
            for sl in range(2):
                for a in range(nparts):
                    kbuf[sl, a, 0:qr, :] = jnp.zeros((qr, lw), f32)
                    vbuf[sl, a, 0:qr, :] = jnp.zeros((qr, lw), f32)
            for cp in in_copies(0, 0):
                cp.start()

        @pl.when(i + 1 < nitems)
        def _():
            for cp in in_copies(i + 1, 1 - slot):
                cp.start()

        for cp in in_copies(i, slot):
            cp.wait()

        @pl.when(i >= 2)
        def _():
            for cp in out_copies(i - 2, slot):
                cp.wait()

        band, is_prev = _key_geometry(nparts)
        lo = lax.broadcasted_iota(jnp.int32, (QBLK, LANES), 1) < HEAD_DIM

        def blk(c, carry):
            r0 = pl.multiple_of(c * qr, qr)
            mask = band & (is_prev * jnp.where(c == 0, 1, 0) == 0)
            for pp in range(lw // LANES):
                lanes = slice(pp * LANES, (pp + 1) * LANES)
                qb = jnp.concatenate([qbuf[slot, a, pl.ds(r0, qr), lanes] for a in range(nparts)], axis=0).astype(bf16)
                kcat = jnp.concatenate([kbuf[slot, a, pl.ds(r0, 2 * qr), lanes] for a in range(nparts)], axis=0).astype(bf16)
                vcat = jnp.concatenate([vbuf[slot, a, pl.ds(r0, 2 * qr), lanes] for a in range(nparts)], axis=0).astype(bf16)
                o, lse = _attn_block_fwd(qb, kcat, vcat, mask, lo)
                for a in range(nparts):
                    obuf[slot, a, pl.ds(r0, qr), lanes] = o[a * qr:(a + 1) * qr]
                    lbuf[slot, a, pl.ds(r0, qr), lanes] = lse[a * qr:(a + 1) * qr]
            return carry

        lax.fori_loop(0, nb, blk, 0)

        for cp in out_copies(i, slot):
            cp.start()

        @pl.when(i == nitems - 1)
        def _():
            for cp in out_copies(i - 1, 1 - slot) + out_copies(i, slot):
                cp.wait()

    vshape = (S // 8, 8, A_WIDTH) if d == 4 else (S // 16, 2, 8, A_WIDTH)
    o, lse = pl.pallas_call(
        body, name=name, grid=(nitems,),
        in_specs=[_HBM_ANY, _HBM_ANY], out_specs=[_HBM_ANY, _HBM_ANY],
        out_shape=[jax.ShapeDtypeStruct(vshape, f32)] * 2,
        scratch_shapes=[pltpu.VMEM((2, nparts, rows, lw), f32), pltpu.VMEM((2, nparts, qr + rows, lw), f32),
                        pltpu.VMEM((2, nparts, qr + rows, lw), f32), pltpu.VMEM((2, nparts, rows, lw), f32),
                        pltpu.VMEM((2, nparts, rows, lw), f32),
                        pltpu.SemaphoreType.DMA((2, 3 * nparts)), pltpu.SemaphoreType.DMA((2, 2 * nparts))],
        compiler_params=_cp(("arbitrary",)),
    )(_stream_view(qk, d), _stream_view(proj, d))
    return o.reshape(S, A_WIDTH), lse.reshape(S, A_WIDTH)


def _attn_bwd_dil(qk, proj, do_a, lt, dsum, g, *, name):
    S = qk.shape[0]
    d = DILATIONS[g]
    nparts, rows, qr = _stream_geometry(S, d)
    nb = rows // qr
    lw = LANES if d == 4 else 4 * LANES
    nlg = A_WIDTH // lw
    nitems = d * nlg
    ins = ((0, E_Q + A_WIDTH * g, 0), (2, 0, 0), (3, 0, 0), (4, 0, 0), (0, E_K + A_WIDTH * g, qr), (1, E_V + A_WIDTH * g, qr))
    n_in = len(ins)

    def body(qk_hbm, pj_hbm, do_hbm, lt_hbm, ds_hbm, dq_hbm, dk_hbm, dv_hbm,
             qbuf, dobuf, ltbuf, dsbuf, kbuf, vbuf, dqbuf, dkbuf, dvbuf, in_sems, out_sems):
        i = pl.program_id(0)
        slot = i % 2
        hbm_in = (qk_hbm, pj_hbm, do_hbm, lt_hbm, ds_hbm)
        bufs_in = (qbuf, dobuf, ltbuf, dsbuf, kbuf, vbuf)

        def in_copies(item, sl):
            r, lg = item // nlg, item % nlg
            cps = []
            for a in range(nparts):
                for t, (src, col, pad) in enumerate(ins):
                    cps.append(pltpu.make_async_copy(
                        _stream_ref(hbm_in[src], d, r, a, pl.multiple_of(col + lw * lg, LANES), lw),
                        bufs_in[t].at[sl, a, pl.ds(pad, rows), :], in_sems.at[sl, n_in * a + t]))
            return cps

        def out_copies(item, sl):
            r, lg = item // nlg, item % nlg
            cps = []
            for a in range(nparts):
                for t, (buf, dst, pad) in enumerate(((dqbuf, dq_hbm, 0), (dkbuf, dk_hbm, qr), (dvbuf, dv_hbm, qr))):
                    cps.append(pltpu.make_async_copy(
                        buf.at[sl, a, pl.ds(pad, rows), :],
                        _stream_ref(dst, d, r, a, pl.multiple_of(lw * lg, LANES), lw), out_sems.at[sl, 3 * a + t]))
            return cps

        @pl.when(i == 0)
        def _():
            for sl in range(2):
                for a in range(nparts):
                    kbuf[sl, a, 0:qr, :] = jnp.zeros((qr, lw), f32)
                    vbuf[sl, a, 0:qr, :] = jnp.zeros((qr, lw), f32)
            for cp in in_copies(0, 0):
                cp.start()

        @pl.when(i + 1 < nitems)
        def _():
            for cp in in_copies(i + 1, 1 - slot):
                cp.start()

        for cp in in_copies(i, slot):
            cp.wait()

        @pl.when(i >= 2)
        def _():
            for cp in out_copies(i - 2, slot):
                cp.wait()

        for a in range(nparts):
            dkbuf[slot, a] = jnp.zeros((qr + rows, lw), f32)
            dvbuf[slot, a] = jnp.zeros((qr + rows, lw), f32)
        band, is_prev = _key_geometry(nparts)
        lo = lax.broadcasted_iota(jnp.int32, (QBLK, LANES), 1) < HEAD_DIM

        def blk(c, carry):
            r0 = pl.multiple_of(c * qr, qr)
            mask = band & (is_prev * jnp.where(c == 0, 1, 0) == 0)

            def rows_of(buf, n, lanes):
                return jnp.concatenate([buf[slot, a, pl.ds(r0, n), lanes] for a in range(nparts)], axis=0)

            for pp in range(lw // LANES):
                lanes = slice(pp * LANES, (pp + 1) * LANES)
                dq, dk, dv = _attn_block_bwd(
                    rows_of(qbuf, qr, lanes).astype(bf16), rows_of(dobuf, qr, lanes).astype(bf16),
                    rows_of(kbuf, 2 * qr, lanes).astype(bf16), rows_of(vbuf, 2 * qr, lanes).astype(bf16),
                    rows_of(ltbuf, qr, lanes), rows_of(dsbuf, qr, lanes), mask, lo)
                for a in range(nparts):
                    dqbuf[slot, a, pl.ds(r0, qr), lanes] = dq[a * qr:(a + 1) * qr]
                    dkbuf[slot, a, pl.ds(r0, 2 * qr), lanes] += dk[2 * a * qr:2 * (a + 1) * qr]
                    dvbuf[slot, a, pl.ds(r0, 2 * qr), lanes] += dv[2 * a * qr:2 * (a + 1) * qr]
            return carry

        lax.fori_loop(0, nb, blk, 0)

        for cp in out_copies(i, slot):
            cp.start()

        @pl.when(i == nitems - 1)
        def _():
            for cp in out_copies(i - 1, 1 - slot) + out_copies(i, slot):
                cp.wait()

    vshape = (S // 8, 8, A_WIDTH) if d == 4 else (S // 16, 2, 8, A_WIDTH)
    plain = pltpu.VMEM((2, nparts, rows, lw), f32)
    padded = pltpu.VMEM((2, nparts, qr + rows, lw), f32)
    outs = pl.pallas_call(
        body, name=name, grid=(nitems,),
        in_specs=[_HBM_ANY] * 5, out_specs=[_HBM_ANY] * 3,
        out_shape=[jax.ShapeDtypeStruct(vshape, f32)] * 3,
        scratch_shapes=[plain, plain, plain, plain, padded, padded, plain, padded, padded,
                        pltpu.SemaphoreType.DMA((2, n_in * nparts)), pltpu.SemaphoreType.DMA((2, 3 * nparts))],
        compiler_params=_cp(("arbitrary",)),
    )(*[_stream_view(a, d) for a in (qk, proj, do_a, lt, dsum)])
    return [o.reshape(S, A_WIDTH) for o in outs]


def _prev_halo(tm, h, col):
    return pl.BlockSpec((h, 512), lambda i: (jnp.maximum(i * (tm // h) - 1, 0), col))


def _next_halo(tm, h, col, S):
    return pl.BlockSpec((h, 512), lambda i: (jnp.minimum((i + 1) * (tm // h), S // h - 1), col))


def _mix0_fwd(o_g, lse_g, proj, conv_w):
    S = proj.shape[0]
    tm = _tile(S, 256)

    def body(o0, o1, o2, l0, l1, l2, bg_ref, cg_ref, hb_ref, z_ref, cgh_ref, hbh_ref, w_ref,
             u_ref, oa_ref, lt_ref, tbuf):
        i = pl.program_id(0)
        ls = [l0[...], l1[...], l2[...]]
        mx = jnp.maximum(jnp.maximum(ls[0], ls[1]), ls[2])
        es = [jnp.exp(l - mx) for l in ls]
        tot = es[0] + es[1] + es[2]
        lt_ref[...] = mx + jnp.log(tot)
        inv = 1.0 / tot
        z = z_ref[...]
        sz = z * _sigmoid(z)
        oa = (es[0] * inv) * o0[...] + (es[1] * inv) * o1[...] + (es[2] * inv) * o2[...]
        oa_ref[...] = oa
        u_ref[:, :A_WIDTH] = (oa * sz[:, :A_WIDTH]).astype(bf16)
        t = cg_ref[...] * hb_ref[...]
        tbuf[0:8, :] = jnp.where(i > 0, cgh_ref[...] * hbh_ref[...], 0.0)
        tbuf[8:, :] = t
        cv = w_ref[2:3, :] * t + w_ref[1:2, :] * tbuf[pl.ds(7, tm), :] + w_ref[0:1, :] * tbuf[pl.ds(6, tm), :]
        u_ref[:, A_WIDTH:] = (bg_ref[...] * cv * sz[:, A_WIDTH:]).astype(bf16)

    row = lambda w, c: pl.BlockSpec((tm, w), lambda i: (i, c))
    return pl.pallas_call(
        body, name="mix0_fwd", grid=(S // tm,),
        in_specs=[row(512, 0)] * 6
        + [row(512, E_BG // 512), row(512, E_CG // 512), row(512, E_HB // 512), row(1024, E_Z // 1024),
           _prev_halo(tm, 8, E_CG // 512), _prev_halo(tm, 8, E_HB // 512), pl.BlockSpec((SC_WIDTH, 512), lambda i: (0, 0))],
        out_specs=[row(1024, 0), row(512, 0), row(512, 0)],
        out_shape=[jax.ShapeDtypeStruct((S, D_MODEL), bf16), jax.ShapeDtypeStruct((S, A_WIDTH), f32),
                   jax.ShapeDtypeStruct((S, A_WIDTH), f32)],
        scratch_shapes=[pltpu.VMEM((tm + 8, 512), f32)],
        compiler_params=_cp(("parallel",)),
    )(*o_g, *lse_g, proj, proj, proj, proj, proj, proj, conv_w)


def _dsilu(z, sg):
    return sg * (1.0 + z * (1.0 - sg))


def _mix0_bwd_a(du, proj, o_a, conv_w):
    S = proj.shape[0]
    tm = _tile(S, 256)

    def body(du_ref, bg_ref, cg_ref, hb_ref, z_ref, cgh_ref, hbh_ref, oa_ref, w_ref,
             dz_ref, doa_ref, ds_ref, dbg_ref, dcv_ref, tbuf):
        i = pl.program_id(0)
        lo = lax.broadcasted_iota(jnp.int32, (tm, LANES), 1) < HEAD_DIM
        z = z_ref[...]
        sg = _sigmoid(z)
        sz = z * sg
        dsz = _dsilu(z, sg)
        du_v = du_ref[...]
        t = cg_ref[...] * hb_ref[...]
        tbuf[0:8, :] = jnp.where(i > 0, cgh_ref[...] * hbh_ref[...], 0.0)
        tbuf[8:, :] = t
        cv = w_ref[2:3, :] * t + w_ref[1:2, :] * tbuf[pl.ds(7, tm), :] + w_ref[0:1, :] * tbuf[pl.ds(6, tm), :]
        bg = bg_ref[...]
        oa = oa_ref[...]
        dz_ref[:, :A_WIDTH] = (du_v[:, :A_WIDTH] * oa * dsz[:, :A_WIDTH]).astype(bf16)
        dz_ref[:, A_WIDTH:] = (du_v[:, A_WIDTH:] * (bg * cv) * dsz[:, A_WIDTH:]).astype(bf16)
        doa = du_v[:, :A_WIDTH] * sz[:, :A_WIDTH]
        dyb = du_v[:, A_WIDTH:] * sz[:, A_WIDTH:]
        doa_ref[...] = doa
        dbg_ref[...] = (dyb * cv).astype(bf16)
        dcv_ref[...] = dyb * bg
        prod = doa * oa
        for p in range(4):
            pp = prod[:, p * LANES:(p + 1) * LANES]
            sa = jnp.sum(jnp.where(lo, pp, 0.0), axis=-1, keepdims=True)
            sb = jnp.sum(jnp.where(lo, 0.0, pp), axis=-1, keepdims=True)
            ds_ref[:, p * LANES:(p + 1) * LANES] = jnp.where(lo, sa, sb)

    row = lambda w, c: pl.BlockSpec((tm, w), lambda i: (i, c))
    return pl.pallas_call(
        body, name="mix0_bwd_a", grid=(S // tm,),
        in_specs=[row(1024, 0), row(512, E_BG // 512), row(512, E_CG // 512), row(512, E_HB // 512), row(1024, E_Z // 1024),
                  _prev_halo(tm, 8, E_CG // 512), _prev_halo(tm, 8, E_HB // 512), row(512, 0),
                  pl.BlockSpec((SC_WIDTH, 512), lambda i: (0, 0))],
        out_specs=[row(1024, 0), row(512, 0), row(512, 0), row(512, 0), row(512, 0)],
        out_shape=[jax.ShapeDtypeStruct((S, D_MODEL), bf16), jax.ShapeDtypeStruct((S, A_WIDTH), f32),
                   jax.ShapeDtypeStruct((S, A_WIDTH), f32), jax.ShapeDtypeStruct((S, 512), bf16),
                   jax.ShapeDtypeStruct((S, 512), f32)],
        scratch_shapes=[pltpu.VMEM((tm + 8, 512), f32)],
        compiler_params=_cp(("parallel",)),
    )(du, proj, proj, proj, proj, proj, proj, o_a, conv_w)


def _mix0_bwd_b(dcv, proj, conv_w):
    S = proj.shape[0]
    tm = _tile(S, 256)
    nt = S // tm

    def body(dcv_ref, dcvn_ref, cg_ref, hb_ref, cgh_ref, hbh_ref, w_ref, dcg_ref, dhb_ref, gw_ref, tbuf, dbuf):
        i = pl.program_id(0)
        cg = cg_ref[...]
        hb = hb_ref[...]
        t = cg * hb
        tbuf[0:8, :] = jnp.where(i > 0, cgh_ref[...] * hbh_ref[...], 0.0)
        tbuf[8:, :] = t
        dcv_v = dcv_ref[...]
        dbuf[0:tm, :] = dcv_v
        dbuf[tm:, :] = jnp.where(i < nt - 1, dcvn_ref[...], 0.0)
        dt = w_ref[2:3, :] * dcv_v + w_ref[1:2, :] * dbuf[pl.ds(1, tm), :] + w_ref[0:1, :] * dbuf[pl.ds(2, tm), :]
        dcg_ref[...] = (dt * hb).astype(bf16)
        dhb_ref[...] = (dt * cg).astype(bf16)
        g2 = jnp.sum(dcv_v * t, axis=0, keepdims=True)
        g1 = jnp.sum(dcv_v * tbuf[pl.ds(7, tm), :], axis=0, keepdims=True)
        g0 = jnp.sum(dcv_v * tbuf[pl.ds(6, tm), :], axis=0, keepdims=True)
        part = jnp.concatenate([g0, g1, g2, jnp.zeros((5, 512), f32)], axis=0)

        @pl.when(i == 0)
        def _():
            gw_ref[...] = part

        @pl.when(i > 0)
        def _():
            gw_ref[...] += part

    row = lambda w, c: pl.BlockSpec((tm, w), lambda i: (i, c))
    return pl.pallas_call(
        body, name="mix0_bwd_b", grid=(nt,),
        in_specs=[row(512, 0), _next_halo(tm, 8, 0, S), row(512, E_CG // 512), row(512, E_HB // 512),
                  _prev_halo(tm, 8, E_CG // 512), _prev_halo(tm, 8, E_HB // 512),
                  pl.BlockSpec((SC_WIDTH, 512), lambda i: (0, 0))],
        out_specs=[row(512, 0), row(512, 0), pl.BlockSpec((8, 512), lambda i: (0, 0))],
        out_shape=[jax.ShapeDtypeStruct((S, 512), bf16), jax.ShapeDtypeStruct((S, 512), bf16),
                   jax.ShapeDtypeStruct((8, 512), f32)],
        scratch_shapes=[pltpu.VMEM((tm + 8, 512), f32), pltpu.VMEM((tm + 8, 512), f32)],
        compiler_params=_cp(("arbitrary",)),
    )(dcv, dcv, proj, proj, proj, proj, conv_w)


def _qk_bwd(dq_g, dk_g, dv_g, proj, tabs, nw, hm, dbg, dcg, dhb, dz):
    S = proj.shape[0]
    tm = _tile(S, 256)

    def body(*refs):
        d_refs = refs[0:6]
        dv_refs = refs[6:9]
        x_ref, c_ref, s1_ref, s2_ref, nw_ref, m_ref, dbg_ref, dcg_ref, dhb_ref, dz_ref, o_ref, gw_ref = refs[9:]
        i = pl.program_id(0)
        c, s1, s2, m = c_ref[...], s1_ref[...], s2_ref[...], m_ref[...]
        accs = []
        for kind in range(2):
            w = nw_ref[kind:kind + 1, :]
            acc = jnp.zeros((1, LANES), f32)
            for gi in range(N_GROUPS):
                for p in range(4):
                    col = kind * 1536 + gi * 512 + p * LANES
                    dout = d_refs[kind * 3 + gi][:, p * LANES:(p + 1) * LANES]
                    t = x_ref[:, col:col + LANES]
                    dthat = (dout * c + pltpu.roll(dout * s1, LANES - ROT_HALF, axis=1)
                             + pltpu.roll(dout * s2, ROT_HALF, axis=1))
                    r = lax.rsqrt(_head_mean(t * t, m) + EPS)
                    tn = t * r
                    acc = acc + jnp.sum(dthat * tn, axis=0, keepdims=True)
                    dtn = dthat * w
                    o_ref[:, col:col + LANES] = (r * (dtn - tn * _head_mean(dtn * tn, m))).astype(bf16)
            accs.append(acc + pltpu.roll(acc, HEAD_DIM, axis=1))
        for gi in range(N_GROUPS):
            o_ref[:, E_V + gi * 512:E_V + (gi + 1) * 512] = dv_refs[gi][...].astype(bf16)
        o_ref[:, E_BG:E_CG] = dbg_ref[...]
        o_ref[:, E_CG:E_HB] = dcg_ref[...]
        o_ref[:, E_HB:E_Z] = dhb_ref[...]
        o_ref[:, E_Z:] = dz_ref[...]
        part = jnp.concatenate([accs[0], accs[1], jnp.zeros((6, LANES), f32)], axis=0)

        @pl.when(i == 0)
        def _():
            gw_ref[...] = part

        @pl.when(i > 0)
        def _():
            gw_ref[...] += part

    row = lambda w, c: pl.BlockSpec((tm, w), lambda i: (i, c))
    tab = row(LANES, 0)
    return pl.pallas_call(
        body, name="qk_bwd", grid=(S // tm,),
        in_specs=[row(512, 0)] * 9 + [row(3072, 0), tab, tab, tab, pl.BlockSpec((2, LANES), lambda i: (0, 0)),
                                      pl.BlockSpec((LANES, LANES), lambda i: (0, 0)),
                                      row(512, 0), row(512, 0), row(512, 0), row(1024, 0)],
        out_specs=[row(EVEN_IN, 0), pl.BlockSpec((8, LANES), lambda i: (0, 0))],
        out_shape=[jax.ShapeDtypeStruct((S, EVEN_IN), bf16), jax.ShapeDtypeStruct((8, LANES), f32)],
        compiler_params=_cp(("arbitrary",)),
    )(*dq_g, *dk_g, *dv_g, proj, *tabs, nw, hm, dbg, dcg, dhb, dz)


def _inv_count(i, tm, p):
    rowg = lax.broadcasted_iota(jnp.int32, (tm, 1), 0) + i * tm
    return 1.0 / jnp.minimum(rowg + 1, p).astype(f32)


def _layer_norm_stats(c):
    mu = jnp.mean(c, axis=-1, keepdims=True)
    cen = c - mu
    rstd = lax.rsqrt(jnp.mean(cen * cen, axis=-1, keepdims=True) + EPS)
    return cen * rstd, rstd


def _fill_pool_buf(i, ubuf, uc_ref, uch_ref):
    ubuf[0:16, :] = jnp.where(i > 0, uch_ref[...], 0.0)
    ubuf[16:, :] = uc_ref[...]


def _pooled(i, tm, ubuf, gi):
    p = POOL_SIZES[gi]
    cols = slice(gi * LANES, (gi + 1) * LANES)
    acc = ubuf[pl.ds(16, tm), cols]
    cur = acc
    for jj in range(1, p):
        acc = acc + ubuf[pl.ds(16 - jj, tm), cols]
    return acc * _inv_count(i, tm, p) - cur


def _fill_glu_buf(i, gbuf, da_ref, dg_ref, dah_ref, dgh_ref):
    gbuf[0:32, :] = jnp.where(i > 0, dah_ref[...] * _sigmoid(dgh_ref[...]), 0.0)
    gbuf[32:, :] = da_ref[...] * _sigmoid(dg_ref[...])


def _shift_copies(buf, sh, tm):
    for b in range(1, 8):
        sh[b - 1] = buf[pl.ds(b, tm + 24), :]


CONV_ROWS = 32


def _window(buf, sh, base, off, rows):
    b = off % 8
    if b == 0:
        return buf[pl.ds(base + off, rows), :]
    return sh[b - 1, pl.ds(base + (off - b), rows), :]


def _mix1_fwd(proj, pool_w, pool_scale, dconv_w, dconv_b, ln_w, ln_b):
    S = proj.shape[0]
    tm = _tile(S, 256)

    def body(uc_ref, uch_ref, da_ref, dg_ref, dah_ref, dgh_ref, za_ref, zb_ref, pw_ref, ps_ref, cw_ref, cb_ref,
             lw_ref, lb_ref, u_ref, c_ref, mc_ref, ubuf, gbuf, gsh):
        i = pl.program_id(0)
        _fill_pool_buf(i, ubuf, uc_ref, uch_ref)
        za = za_ref[...]
        for gi in range(4):
            cols = slice(gi * LANES, (gi + 1) * LANES)
            mc = jnp.dot(_pooled(i, tm, ubuf, gi).astype(bf16), pw_ref[gi], preferred_element_type=f32)
            mc_ref[:, cols] = mc
            zg = za[:, cols]
            u_ref[:, cols] = (mc * ps_ref[:, cols] * (zg * _sigmoid(zg))).astype(bf16)
        _fill_glu_buf(i, gbuf, da_ref, dg_ref, dah_ref, dgh_ref)
        _shift_copies(gbuf, gsh, tm)
        c = jnp.zeros((tm, 512), f32) + cb_ref[...]
        for k in range(D_CONV):
            c = c + cw_ref[k:k + 1, :] * _window(gbuf, gsh, 0, 32 - (D_CONV - 1) + k, tm)
        c_ref[...] = c
        yhat, _ = _layer_norm_stats(c)
        l = yhat * lw_ref[...] + lb_ref[...]
        zb = zb_ref[...]
        u_ref[:, 512:] = (l * _sigmoid(l) * (zb * _sigmoid(zb))).astype(bf16)

    row = lambda w, c: pl.BlockSpec((tm, w), lambda i: (i, c))
    vec = pl.BlockSpec((1, 512), lambda i: (0, 0))
    return pl.pallas_call(
        body, name="mix1_fwd", grid=(S // tm,),
        in_specs=[row(512, 0), _prev_halo(tm, 16, 0), row(512, 1), row(512, 2), _prev_halo(tm, 32, 1), _prev_halo(tm, 32, 2),
                  row(512, 3), row(512, 4), pl.BlockSpec((4, LANES, LANES), lambda i: (0, 0, 0)), vec,
                  pl.BlockSpec((D_CONV, 512), lambda i: (0, 0)), vec, vec, vec],
        out_specs=[row(1024, 0), row(512, 0), row(512, 0)],
        out_shape=[jax.ShapeDtypeStruct((S, D_MODEL), bf16), jax.ShapeDtypeStruct((S, 512), f32),
                   jax.ShapeDtypeStruct((S, 512), f32)],
        scratch_shapes=[pltpu.VMEM((tm + 16, 512), f32), pltpu.VMEM((tm + 32, 512), f32),
                        pltpu.VMEM((7, tm + 24, 512), f32)],
        compiler_params=_cp(("parallel",)),
    )(proj, proj, proj, proj, proj, proj, proj, proj, pool_w, pool_scale, dconv_w, dconv_b, ln_w, ln_b)


def _mix1_bwd_a(du, proj, c, mc, pool_w, pool_scale, ln_w, ln_b):
    S = proj.shape[0]
    tm = _tile(S, 256)

    def body(du_ref, za_ref, zb_ref, c_ref, mc_ref, pw_ref, ps_ref, lw_ref, lb_ref,
             dz_ref, dc_ref, dpl_ref, dmc_ref, acc_ref):
        i = pl.program_id(0)
        du_v = du_ref[...]
        ps = ps_ref[...]
        za = za_ref[...]
        sga = _sigmoid(za)
        mcv = mc_ref[...]
        dz_ref[:, :512] = (du_v[:, :512] * (mcv * ps) * _dsilu(za, sga)).astype(bf16)
        dyc = du_v[:, :512] * (za * sga)
        g_ps = jnp.sum(dyc * mcv, axis=0, keepdims=True)
        dmc = (dyc * ps).astype(bf16)
        dmc_ref[...] = dmc
        for gi in range(4):
            cols = slice(gi * LANES, (gi + 1) * LANES)
            dpl_ref[:, cols] = lax.dot_general(dmc[:, cols], pw_ref[gi], (((1,), (1,)), ((), ())), preferred_element_type=f32)
        yhat, rstd = _layer_norm_stats(c_ref[...])
        lw = lw_ref[...]
        l = yhat * lw + lb_ref[...]
        sgl = _sigmoid(l)
        zb = zb_ref[...]
        sgb = _sigmoid(zb)
        dz_ref[:, 512:] = (du_v[:, 512:] * (l * sgl) * _dsilu(zb, sgb)).astype(bf16)
        dl = du_v[:, 512:] * (zb * sgb) * _dsilu(l, sgl)
        g_lb = jnp.sum(dl, axis=0, keepdims=True)
        g_lw = jnp.sum(dl * yhat, axis=0, keepdims=True)
        dyh = dl * lw
        dc = rstd * (dyh - jnp.mean(dyh, axis=-1, keepdims=True) - yhat * jnp.mean(dyh * yhat, axis=-1, keepdims=True))
        dc_ref[...] = dc
        g_db = jnp.sum(dc, axis=0, keepdims=True)
        part = jnp.concatenate([g_ps, g_lw, g_lb, g_db, jnp.zeros((4, 512), f32)], axis=0)

        @pl.when(i == 0)
        def _():
            acc_ref[...] = part

        @pl.when(i > 0)
        def _():
            acc_ref[...] += part

    row = lambda w, c_: pl.BlockSpec((tm, w), lambda i: (i, c_))
    vec = pl.BlockSpec((1, 512), lambda i: (0, 0))
    return pl.pallas_call(
        body, name="mix1_bwd_a", grid=(S // tm,),
        in_specs=[row(1024, 0), row(512, 3), row(512, 4), row(512, 0), row(512, 0),
                  pl.BlockSpec((4, LANES, LANES), lambda i: (0, 0, 0)), vec, vec, vec],
        out_specs=[row(1024, 0), row(512, 0), row(512, 0), row(512, 0), pl.BlockSpec((8, 512), lambda i: (0, 0))],
        out_shape=[jax.ShapeDtypeStruct((S, D_MODEL), bf16), jax.ShapeDtypeStruct((S, 512), f32),
                   jax.ShapeDtypeStruct((S, 512), f32), jax.ShapeDtypeStruct((S, 512), bf16),
                   jax.ShapeDtypeStruct((8, 512), f32)],
        compiler_params=_cp(("arbitrary",)),
    )(du, proj, proj, c, mc, pool_w, pool_scale, ln_w, ln_b)


def _mix1_bwd_b(dc, dpl, dmc, dz, proj, dconv_w):
    S = proj.shape[0]
    tm = _tile(S, 256)
    nt = S // tm

    def body(dc_ref, dcn_ref, dpl_ref, dpn_ref, dmc_ref, dz_ref, uc_ref, uch_ref, da_ref, dg_ref,
             cw_ref, o_ref, gcw_ref, gpw_ref, ubuf, dcbuf, dpbuf, dcsh, gacc):
        i = pl.program_id(0)
        last = i == nt - 1
        _fill_pool_buf(i, ubuf, uc_ref, uch_ref)
        dcbuf[0:tm, :] = dc_ref[...]
        dcbuf[tm:, :] = jnp.where(last, 0.0, dcn_ref[...])
        _shift_copies(dcbuf, dcsh, tm)
        dpl_v = dpl_ref[...]
        for gi in range(4):
            p = POOL_SIZES[gi]
            cols = slice(gi * LANES, (gi + 1) * LANES)
            dpbuf[0:tm, cols] = dpl_v[:, cols] * _inv_count(i, tm, p)
            dpbuf[tm:, cols] = jnp.where(last, 0.0, dpn_ref[:, cols] * (1.0 / p))
        gpw = []
        for gi in range(4):
            p = POOL_SIZES[gi]
            cols = slice(gi * LANES, (gi + 1) * LANES)
            acc = -dpl_v[:, cols]
            for jj in range(p):
                acc = acc + dpbuf[pl.ds(jj, tm), cols]
            o_ref[:, cols] = acc.astype(bf16)
            pooled = _pooled(i, tm, ubuf, gi).astype(bf16)
            gpw.append(lax.dot_general(pooled, dmc_ref[:, cols], (((0,), (0,)), ((), ())), preferred_element_type=f32))
        gacc[...] = jnp.zeros_like(gacc)

        def conv_rows(ci, carry):
            base = pl.multiple_of(ci * CONV_ROWS, CONV_ROWS)
            da = da_ref[pl.ds(base, CONV_ROWS), :]
            sg = _sigmoid(dg_ref[pl.ds(base, CONV_ROWS), :])
            gl = da * sg
            dgl = jnp.zeros((CONV_ROWS, 512), f32)
            for k in range(D_CONV):
                win = _window(dcbuf, dcsh, base, D_CONV - 1 - k, CONV_ROWS)
                dgl = dgl + cw_ref[k:k + 1, :] * win
                gacc[k] += jnp.sum((gl * win).reshape(CONV_ROWS // 8, 8, 512), axis=0)
            o_ref[pl.ds(base, CONV_ROWS), O_DA:O_DG] = (dgl * sg).astype(bf16)
            o_ref[pl.ds(base, CONV_ROWS), O_DG:O_Z] = (dgl * da * sg * (1.0 - sg)).astype(bf16)
            return carry

        lax.fori_loop(0, tm // CONV_ROWS, conv_rows, 0)
        o_ref[:, O_Z:] = dz_ref[...]
        gcw_part = jnp.concatenate(
            [jnp.sum(gacc[k], axis=0, keepdims=True) for k in range(D_CONV)] + [jnp.zeros((1, 512), f32)], axis=0)

        @pl.when(i == 0)
        def _():
            gcw_ref[...] = gcw_part
            for gi in range(4):
                gpw_ref[gi] = gpw[gi]

        @pl.when(i > 0)
        def _():
            gcw_ref[...] += gcw_part
            for gi in range(4):
                gpw_ref[gi] += gpw[gi]

    row = lambda w, c_: pl.BlockSpec((tm, w), lambda i: (i, c_))
    return pl.pallas_call(
        body, name="mix1_bwd_b", grid=(nt,),
        in_specs=[row(512, 0), _next_halo(tm, 32, 0, S), row(512, 0), _next_halo(tm, 16, 0, S), row(512, 0), row(1024, 0),
                  row(512, 0), _prev_halo(tm, 16, 0), row(512, 1), row(512, 2),
                  pl.BlockSpec((D_CONV, 512), lambda i: (0, 0))],
        out_specs=[row(ODD_IN, 0), pl.BlockSpec((32, 512), lambda i: (0, 0)),
                   pl.BlockSpec((4, LANES, LANES), lambda i: (0, 0, 0))],
        out_shape=[jax.ShapeDtypeStruct((S, ODD_IN), bf16), jax.ShapeDtypeStruct((32, 512), f32),
                   jax.ShapeDtypeStruct((4, LANES, LANES), f32)],
        scratch_shapes=[pltpu.VMEM((tm + 16, 512), f32), pltpu.VMEM((tm + 32, 512), f32),
                        pltpu.VMEM((tm + 16, 512), f32), pltpu.VMEM((7, tm + 24, 512), f32),
                        pltpu.VMEM((D_CONV, 8, 512), f32)],
        compiler_params=_cp(("arbitrary",)),
    )(dc, dc, dpl, dpl, dmc, dz, proj, proj, proj, proj, dconv_w)


_SMALL_LATE = ["e_q_norm_w", "e_k_norm_w", "e_conv_w", "o_norm_w", "o_pool_w", "o_pool_scale", "o_dconv_w", "o_dconv_b",
               "o_ln_w", "o_ln_b"]


def _local_step(x, pos_col, target, w, dist=None):
    hm = _head_mean_matrix()
    nw = jnp.concatenate([jnp.tile(w["e_q_norm_w"], (1, 2)), jnp.tile(w["e_k_norm_w"], (1, 2))], axis=0)
    tabs = _rope_tables(pos_col)
    pool_wb = w["o_pool_w"].astype(bf16)
    e_norm_w, e_w_in = w["e_norm_w"], w["e_w_in"]

    h0 = _rms_fwd(x, e_norm_w, name="rms0_fwd")
    if dist is None:
        proj0, qk = _in_proj0(h0, e_w_in, tabs, nw, hm)
    else:
        proj0, qk, gathered = _in_proj0(h0, e_w_in, tabs, nw, hm, fuse=([], dist[0]))
        w = {**w, **dist[1](gathered)}
    e_conv_w, e_w_out, o_norm_w, o_w_in, o_w_out = w["e_conv_w"], w["e_w_out"], w["o_norm_w"], w["o_w_in"], w["o_w_out"]
    o_pool_scale, o_dconv_w, o_dconv_b, o_ln_w, o_ln_b = (w[k] for k in ("o_pool_scale", "o_dconv_w", "o_dconv_b", "o_ln_w", "o_ln_b"))
    o_g, lse_g = [], []
    for g in range(N_GROUPS):
        o, l = _attn_fwd_local(qk, proj0) if g == 0 else _attn_fwd_dil(qk, proj0, g, name=f"attn_fwd{g}")
        o_g.append(o)
        lse_g.append(l)
    u0, o_a, lt = _mix0_fwd(o_g, lse_g, proj0, e_conv_w)
    x1 = _mm_nn(u0, e_w_out, res=x, name="out_proj0")
    h1 = _rms_fwd(x1, o_norm_w, name="rms1_fwd")
    proj1 = _mm_nn(h1, o_w_in, name="in_proj1", tn=512)
    u1, c1, mc1 = _mix1_fwd(proj1, pool_wb, o_pool_scale, o_dconv_w, o_dconv_b, o_ln_w, o_ln_b)
    dy, dyb, loss = _mm_out_loss(u1, o_w_out, x1, target, name="out_proj1_loss")
    g_o_w_out = _mm_tn(u1, dyb, name="g_w_out1", out_dtype=bf16)
    du1 = _mm_nt(dyb, o_w_out, name="d_u1")
    dz1, dc1, dpl1, dmc1, sums1 = _mix1_bwd_a(du1, proj1, c1, mc1, pool_wb, o_pool_scale, o_ln_w, o_ln_b)
    dproj1, g_dconv_w, g_pool_w = _mix1_bwd_b(dc1, dpl1, dmc1, dz1, proj1, o_dconv_w)
    g_o_w_in = _mm_tn(h1, dproj1, name="g_w_in1", out_dtype=bf16)
    dh1 = _mm_nt(dproj1, o_w_in, name="d_h1")
    d1, d1b, g_o_norm = _rms_bwd(x1, o_norm_w, dh1, dy, name="rms1_bwd")
    g_e_w_out = _mm_tn(u0, d1b, name="g_w_out0", out_dtype=bf16)
    du0 = _mm_nt(d1b, e_w_out, name="d_u0")
    dz0, do_a, dsum, dbg, dcv = _mix0_bwd_a(du0, proj0, o_a, e_conv_w)
    dcg, dhb, g_conv_w = _mix0_bwd_b(dcv, proj0, e_conv_w)
    fuse_a = None if dist is None else (
        [g_e_w_out.reshape(N_DEV, D_MODEL // N_DEV, D_MODEL),
         jnp.moveaxis(g_o_w_in.reshape(D_MODEL, N_DEV, ODD_IN // N_DEV), 1, 0),
         g_o_w_out.reshape(N_DEV, D_MODEL // N_DEV, D_MODEL)], [])
    dq_g, dk_g, dv_g = [], [], []
    for g in range(N_GROUPS):
        if g == 0:
            dqkv = _attn_bwd_local(qk, proj0, do_a, lt, dsum, fuse=fuse_a)
            if dist is not None:
                dqkv, recv_a = dqkv
            dq, dk, dv = dqkv
        else:
            dq, dk, dv = _attn_bwd_dil(qk, proj0, do_a, lt, dsum, g, name=f"attn_bwd{g}")
        dq_g.append(dq)
        dk_g.append(dk)
        dv_g.append(dv)
    dproj0, g_qk_norm = _qk_bwd(dq_g, dk_g, dv_g, proj0, tabs, nw, hm, dbg, dcg, dhb, dz0)
    half = D_MODEL // 2
    g_e_w_in_a = _mm_tn(h0, dproj0, name="g_w_in0a", out_dtype=bf16, chunks=N_DEV, a_cols=(0, half))
    if dist is None:
        g_e_w_in_b = _mm_tn(h0, dproj0, name="g_w_in0b", out_dtype=bf16, chunks=N_DEV, a_cols=(1, half))
    else:
        g_e_w_in_b, recv_b0 = _mm_tn(h0, dproj0, name="g_w_in0b", out_dtype=bf16, chunks=N_DEV, a_cols=(1, half),
                                     fuse=([g_e_w_in_a], []))
    grads = dict(
        e_q_norm_w=g_qk_norm[0:1, :HEAD_DIM], e_k_norm_w=g_qk_norm[1:2, :HEAD_DIM],
        e_conv_w=g_conv_w[:SC_WIDTH], e_w_out=g_e_w_out,
        o_norm_w=g_o_norm, o_w_in=g_o_w_in, o_pool_w=g_pool_w,
        o_pool_scale=sums1[0:1], o_dconv_w=g_dconv_w[:D_CONV], o_dconv_b=sums1[3:4],
        o_ln_w=sums1[1:2], o_ln_b=sums1[2:3], o_w_out=g_o_w_out)
    if dist is None:
        dh0 = _mm_nt_resident(dproj0, e_w_in, name="d_h0")
        grad_x, _, grads["e_norm_w"] = _rms_bwd(x, e_norm_w, dh0, d1, name="rms0_bwd")
        grads["e_w_in"] = jnp.concatenate([g_e_w_in_a, g_e_w_in_b], axis=1)
        return loss, grad_x, grads
    small_late, offs = _pack_rows([grads[n_] for n_ in _SMALL_LATE])
    dh0, recv_b = _mm_nt_resident(dproj0, e_w_in, name="d_h0", fuse=([g_e_w_in_b], [small_late]))
    grad_x, _, g_e_norm = _rms_bwd(x, e_norm_w, dh0, d1, name="rms0_bwd")
    recv_c = _exchange([], [g_e_norm.reshape(8, LANES)], name="exchange_e_norm")
    recv = dict(e_w_out=[recv_a[0]], o_w_in=[recv_a[1]], o_w_out=[recv_a[2]], e_w_in=[recv_b0[0], recv_b[0]],
                small_late=recv_b[1], e_norm_w=recv_c[0])
    return loss, grad_x, recv, {n_: (off, grads[n_].shape) for n_, off in zip(_SMALL_LATE, offs)}


_MESH_ID = pl.DeviceIdType.MESH
_HBM = pl.BlockSpec(memory_space=pl.ANY)


def _all_gather(arrs, *, name):
    n = len(arrs)

    def body(*refs):
        ins, outs = refs[:n], refs[n:2 * n]
        send_sems, recv_sems, local_sems = refs[2 * n:]
        x, y, c = _place()
        me, sibling = (x, y, c), (x, y, 1 - c)
        chips = [(1 - x, y), (x, 1 - y), (1 - x, 1 - y)]

        def slot(t, px, py, pc):
            return outs[t].at[4 * px + 2 * py + pc]

        def copy(t, k, block, to, src=None):
            dst = slot(t, *block)
            return pltpu.make_async_remote_copy(
                src_ref=dst if src is None else src, dst_ref=dst,
                send_sem=send_sems.at[7 * t + k], recv_sem=recv_sems.at[7 * t + k],
                device_id=to, device_id_type=_MESH_ID)

        mine = [pltpu.make_async_copy(ins[t], slot(t, *me), local_sems.at[t]) for t in range(n)]
        for cp in mine:
            cp.start()
        first = []
        for t in range(n):
            first.append(copy(t, 0, me, sibling, src=ins[t]))
            first += [copy(t, 1 + j, me, (*chip, c), src=ins[t]) for j, chip in enumerate(chips)]
        for cp in first:
            cp.start()
        passed = []
        for j, chip in enumerate(chips):
            for t in range(n):
                copy(t, 1 + j, (*chip, c), me).wait_recv()
                fwd = copy(t, 4 + j, (*chip, c), sibling)
                fwd.start()
                passed.append(fwd)
        for t in range(n):
            copy(t, 0, sibling, me).wait_recv()
            for j, chip in enumerate(chips):
                copy(t, 4 + j, (*chip, 1 - c), me).wait_recv()
        for cp in first + passed:
            cp.wait_send()
        for cp in mine:
            cp.wait()

    return pl.pallas_call(
        body, name=name,
        in_specs=[_HBM] * n, out_specs=[_HBM] * n,
        out_shape=[jax.ShapeDtypeStruct((N_DEV, *a.shape), a.dtype) for a in arrs],
        scratch_shapes=[pltpu.SemaphoreType.DMA((7 * n,)), pltpu.SemaphoreType.DMA((7 * n,)),
                        pltpu.SemaphoreType.DMA((n,))],
    )(*arrs)


def _exchange(chunked, whole, *, name):
    arrs = list(chunked) + list(whole)
    n = len(arrs)

    def body(*refs):
        start, wait = _exchange_plan(refs[:n], refs[n:2 * n], *refs[2 * n:], len(chunked))
        start()
        wait()

    return pl.pallas_call(
        body, name=name, in_specs=[_HBM] * n, out_specs=[_HBM] * n,
        out_shape=_exchange_out_shapes(chunked, whole), scratch_shapes=_exchange_sems(n),
    )(*arrs)


def _adamw(w, g, m, v):
    m2 = ADAM_B1 * m + (1.0 - ADAM_B1) * g
    v2 = ADAM_B2 * v + (1.0 - ADAM_B2) * (g * g)
    m_hat = m2 / (1.0 - ADAM_B1 ** ADAM_STEP)
    v_hat = v2 / (1.0 - ADAM_B2 ** ADAM_STEP)
    delta = -ADAM_LR * (m_hat / (jnp.sqrt(v_hat) + ADAM_EPS) + ADAM_WD * w)
    return delta, m2, v2


def _sum_adamw(parts, w, m, v, *, name):
    R, C = w.shape
    nsplit = len(parts)
    rp = R // nsplit
    tr = _tile(rp, 256)
    npt = rp // tr

    def body(*refs):
        p_refs = refs[:nsplit]
        w_ref, m_ref, v_ref, g_ref, d_ref, nm_ref, nv_ref = refs[nsplit:]
        h = pl.program_id(0)
        g = None
        for i in range(N_DEV):
            pi = p_refs[0][i]
            for q in range(1, nsplit):
                pi = jnp.where(h == q, p_refs[q][i], pi)
            g = pi.astype(f32) if g is None else g + pi.astype(f32)
        g_ref[...] = g
        d_ref[...], nm_ref[...], nv_ref[...] = _adamw(w_ref[...], g, m_ref[...], v_ref[...])

    def part_spec(q):
        return pl.BlockSpec((N_DEV, tr, C), lambda h, i: (0, jnp.where(h == q, i, 0), 0))

    spec = pl.BlockSpec((tr, C), lambda h, i: (h * npt + i, 0))
    return pl.pallas_call(
        body, name=name, grid=(nsplit, npt),
        in_specs=[part_spec(q) for q in range(nsplit)] + [spec, spec, spec],
        out_specs=[spec] * 4, out_shape=[jax.ShapeDtypeStruct((R, C), f32)] * 4,
        compiler_params=_cp(("parallel", "parallel")),
    )(*parts, w, m, v)


def _sum_parts(parts, *, name):
    _, R, C = parts.shape

    def body(p_ref, o_ref):
        g = p_ref[0]
        for i in range(1, N_DEV):
            g = g + p_ref[i]
        o_ref[...] = g

    return pl.pallas_call(body, name=name, out_shape=jax.ShapeDtypeStruct((R, C), f32),
                          compiler_params=pltpu.CompilerParams(vmem_limit_bytes=VMEM_LIMIT))(parts)


def _adamw_small(ws, gs, ms, vs):
    n = len(ws)

    def body(*refs):
        w_r, g_r, m_r, v_r = refs[:n], refs[n:2 * n], refs[2 * n:3 * n], refs[3 * n:4 * n]
        d_r, nm_r, nv_r = refs[4 * n:5 * n], refs[5 * n:6 * n], refs[6 * n:7 * n]
        for t in range(n):
            d_r[t][...], nm_r[t][...], nv_r[t][...] = _adamw(w_r[t][...], g_r[t][...], m_r[t][...], v_r[t][...])

    shapes = [jax.ShapeDtypeStruct(w.shape, f32) for w in ws]
    outs = pl.pallas_call(body, name="adamw_small", out_shape=shapes * 3)(*ws, *gs, *ms, *vs)
    return outs[:n], outs[n:2 * n], outs[2 * n:]


_WEIGHTS = ["e_norm_w", "e_w_in", "e_q_norm_w", "e_k_norm_w", "e_conv_w", "e_w_out", "o_norm_w", "o_w_in", "o_pool_w",
            "o_pool_scale", "o_dconv_w", "o_dconv_b", "o_ln_w", "o_ln_b", "o_w_out"]
_BIG = ["e_w_in", "e_w_out", "o_w_in", "o_w_out"]
_SMALL_SHARDED = ["e_conv_w", "o_norm_w", "o_pool_scale", "o_dconv_w", "o_dconv_b", "o_ln_w", "o_ln_b"]
_SMALL_ALL = ["e_norm_w", "e_q_norm_w", "e_k_norm_w", "e_conv_w", "o_norm_w", "o_pool_w", "o_pool_scale", "o_dconv_w",
              "o_dconv_b", "o_ln_w", "o_ln_b"]


def _pack_rows(pieces):
    rows, offs, r0 = [], [], 0
    for p in pieces:
        flat = p.reshape(-1)
        nr = -(-flat.shape[0] // (8 * LANES)) * 8
        rows.append(jnp.pad(flat, (0, nr * LANES - flat.shape[0])).reshape(nr, LANES))
        offs.append((r0, nr))
        r0 += nr
    return jnp.concatenate(rows, axis=0), offs


def _unpack_rows(buf, off, shape):
    r0, nr = off
    size = int(np.prod(shape))
    return buf[..., r0:r0 + nr, :].reshape(*buf.shape[:-2], nr * LANES)[..., :size].reshape(*buf.shape[:-2], *shape)


def kernel(x, positions, e_norm_w, e_w_in, e_q_norm_w, e_k_norm_w, e_conv_w, e_w_out, o_norm_w, o_w_in, o_pool_w, o_pool_scale, o_dconv_w, o_dconv_b, o_ln_w, o_ln_b, o_w_out, loss_target, m_e_norm_w, m_e_w_in, m_e_q_norm_w, m_e_k_norm_w, m_e_conv_w, m_e_w_out, m_o_norm_w, m_o_w_in, m_o_pool_w, m_o_pool_scale, m_o_dconv_w, m_o_dconv_b, m_o_ln_w, m_o_ln_b, m_o_w_out, v_e_norm_w, v_e_w_in, v_e_q_norm_w, v_e_k_norm_w, v_e_conv_w, v_e_w_out, v_o_norm_w, v_o_w_in, v_o_pool_w, v_o_pool_scale, v_o_dconv_w, v_o_dconv_b, v_o_ln_w, v_o_ln_b, v_o_w_out):
    w = dict(e_norm_w=e_norm_w, e_w_in=e_w_in, e_q_norm_w=e_q_norm_w, e_k_norm_w=e_k_norm_w, e_conv_w=e_conv_w,
             e_w_out=e_w_out, o_norm_w=o_norm_w, o_w_in=o_w_in, o_pool_w=o_pool_w, o_pool_scale=o_pool_scale,
             o_dconv_w=o_dconv_w, o_dconv_b=o_dconv_b, o_ln_w=o_ln_w, o_ln_b=o_ln_b, o_w_out=o_w_out)
    m = dict(e_norm_w=m_e_norm_w, e_w_in=m_e_w_in, e_q_norm_w=m_e_q_norm_w, e_k_norm_w=m_e_k_norm_w, e_conv_w=m_e_conv_w,
             e_w_out=m_e_w_out, o_norm_w=m_o_norm_w, o_w_in=m_o_w_in, o_pool_w=m_o_pool_w, o_pool_scale=m_o_pool_scale,
             o_dconv_w=m_o_dconv_w, o_dconv_b=m_o_dconv_b, o_ln_w=m_o_ln_w, o_ln_b=m_o_ln_b, o_w_out=m_o_w_out)
    v = dict(e_norm_w=v_e_norm_w, e_w_in=v_e_w_in, e_q_norm_w=v_e_q_norm_w, e_k_norm_w=v_e_k_norm_w, e_conv_w=v_e_conv_w,
             e_w_out=v_e_w_out, o_norm_w=v_o_norm_w, o_w_in=v_o_w_in, o_pool_w=v_o_pool_w, o_pool_scale=v_o_pool_scale,
             o_dconv_w=v_o_dconv_w, o_dconv_b=v_o_dconv_b, o_ln_w=v_o_ln_w, o_ln_b=v_o_ln_b, o_w_out=v_o_w_out)
    S = x.shape[1]
    me = 4 * lax.axis_index("x") + 2 * lax.axis_index("y") + lax.axis_index("c")

    small_local, small_offs = _pack_rows([w[n_] for n_ in _SMALL_SHARDED])
    g_e_in, = _all_gather([w["e_w_in"][0].astype(bf16)], name="gather_e_w_in")
    rest_local = [w["e_w_out"][0].astype(bf16), w["o_w_in"][0].astype(bf16), w["o_w_out"][0].astype(bf16), small_local]

    def unpack_rest(gathered):
        g_e_out, g_o_in, g_o_out, g_small = gathered
        full = {}
        for n_, off in zip(_SMALL_SHARDED, small_offs):
            shard = _unpack_rows(g_small, off, w[n_].shape[1:])
            full[n_] = jnp.moveaxis(shard, 0, -2).reshape(*shard.shape[1:-1], N_DEV * shard.shape[-1])
        return dict(
            e_conv_w=full["e_conv_w"], e_w_out=g_e_out.reshape(D_MODEL, D_MODEL), o_norm_w=full["o_norm_w"].reshape(1, D_MODEL),
            o_w_in=jnp.moveaxis(g_o_in, 0, 1).reshape(D_MODEL, ODD_IN), o_pool_scale=full["o_pool_scale"].reshape(1, 512),
            o_dconv_w=full["o_dconv_w"], o_dconv_b=full["o_dconv_b"].reshape(1, 512), o_ln_w=full["o_ln_w"].reshape(1, 512),
            o_ln_b=full["o_ln_b"].reshape(1, 512), o_w_out=g_o_out.reshape(D_MODEL, D_MODEL))

    loss_blk, grad_x, recv, small_where = _local_step(
        x[0], positions.reshape(S, 1), loss_target[0],
        dict(e_norm_w=w["e_norm_w"], e_w_in=g_e_in, e_q_norm_w=w["e_q_norm_w"], e_k_norm_w=w["e_k_norm_w"],
             o_pool_w=w["o_pool_w"][0]),
        dist=(rest_local, unpack_rest))
    loss = lax.psum(loss_blk[0, 0], ("x", "y", "c"))

    out_g, out_d, out_m, out_v = {}, {}, {}, {}
    for n_ in _BIG:
        res = _sum_adamw(recv[n_], w[n_][0], m[n_][0], v[n_][0], name="adamw_" + n_)
        out_g[n_], out_d[n_], out_m[n_], out_v[n_] = [r[None] for r in res]
    small_sum = _sum_parts(recv["small_late"], name="sum_small_grads")
    e_norm_sum = _sum_parts(recv["e_norm_w"], name="sum_e_norm_grad")
    gs = []
    for n_ in _SMALL_ALL:
        if n_ == "e_norm_w":
            gs.append(e_norm_sum.reshape(w[n_].shape))
            continue
        off, shape = small_where[n_]
        gfull = _unpack_rows(small_sum, off, shape)
        if n_ in _SMALL_SHARDED:
            width = w[n_].shape[-1]
            gfull = lax.dynamic_slice_in_dim(gfull, me * width, width, axis=gfull.ndim - 1)
        gs.append(gfull.reshape(w[n_].shape))
    ds, nms, nvs = _adamw_small([w[n_] for n_ in _SMALL_ALL], gs, [m[n_] for n_ in _SMALL_ALL], [v[n_] for n_ in _SMALL_ALL])
    for n_, g_, d_, nm_, nv_ in zip(_SMALL_ALL, gs, ds, nms, nvs):
        out_g[n_], out_d[n_], out_m[n_], out_v[n_] = g_, d_, nm_, nv_

    return (loss, grad_x[None], *[out_g[n_] for n_ in _WEIGHTS], *[out_d[n_] for n_ in _WEIGHTS],
            *[out_m[n_] for n_ in _WEIGHTS], *[out_v[n_] for n_ in _WEIGHTS])
```

```python
import functools

import numpy as np
import jax
import jax.numpy as jnp
from jax import lax
from jax.experimental import pallas as pl
from jax.experimental.pallas import tpu as pltpu

f32 = jnp.float32
bf16 = jnp.bfloat16

D_MODEL = 1024
HEAD_DIM = 64
N_GROUPS = 3
DILATIONS = (1, 4, 16)
QBLK = 128
A_WIDTH = 512
EVEN_IN = 7168
ODD_IN = 2560
POOL_SIZES = (2, 4, 8, 16)
D_CONV = 31
SC_WIDTH = 3
ROT_HALF = 8
ROPE_THETA = 500000.0
EPS = 1e-6
NEG = -1e30
SCALE = HEAD_DIM ** -0.5
N_DEV = 8
LANES = 128
VMEM_LIMIT = 48 * 1024 * 1024

ADAM_LR = 0.001
ADAM_B1 = 0.9
ADAM_B2 = 0.999
ADAM_EPS = 1e-08
ADAM_WD = 0.01
ADAM_STEP = 10

E_Q, E_K, E_V, E_BG, E_CG, E_HB, E_Z = 0, 1536, 3072, 4608, 5120, 5632, 6144
O_UC, O_DA, O_DG, O_Z = 0, 512, 1024, 1536


def _cp(sem):
    return pltpu.CompilerParams(dimension_semantics=sem, vmem_limit_bytes=VMEM_LIMIT)


_HBM_ANY = pl.BlockSpec(memory_space=pl.ANY)


def _sigmoid(z):
    return 1.0 / (1.0 + jnp.exp(-z))


def _tile(n, pref):
    t = pref
    while n % t:
        t //= 2
    return t


def _place():
    return lax.axis_index("x"), lax.axis_index("y"), lax.axis_index("c")


def _exchange_plan(ins, outs, send_sems, recv_sems, local_sems, nc):
    n = len(ins)
    x, y, c = _place()
    me_i = 4 * x + 2 * y + c

    def src(t, dev_i):
        return ins[t].at[dev_i] if t < nc else ins[t]

    def copies(arriving):
        cps = []
        for m in range(1, N_DEV):
            px = 1 - x if m & 4 else x
            py = 1 - y if m & 2 else y
            pc = 1 - c if m & 1 else c
            peer_i = 4 * px + 2 * py + pc
            for t in range(n):
                cps.append(pltpu.make_async_remote_copy(
                    src_ref=src(t, peer_i), dst_ref=outs[t].at[peer_i if arriving else me_i],
                    send_sem=send_sems.at[7 * t + m - 1], recv_sem=recv_sems.at[7 * t + m - 1],
                    device_id=(x, y, c) if arriving else (px, py, pc), device_id_type=pl.DeviceIdType.MESH))
        return cps

    def mine():
        return [pltpu.make_async_copy(src(t, me_i), outs[t].at[me_i], local_sems.at[t]) for t in range(n)]

    def start():
        for cp in mine() + copies(False):
            cp.start()

    def wait():
        for cp in copies(True):
            cp.wait_recv()
        for cp in copies(False):
            cp.wait_send()
        for cp in mine():
            cp.wait()

    return start, wait


def _exchange_sems(n):
    return [pltpu.SemaphoreType.DMA((7 * n,)), pltpu.SemaphoreType.DMA((7 * n,)), pltpu.SemaphoreType.DMA((n,))]


def _exchange_out_shapes(chunked, whole):
    return ([jax.ShapeDtypeStruct(a.shape, a.dtype) for a in chunked]
            + [jax.ShapeDtypeStruct((N_DEV, *a.shape), a.dtype) for a in whole])


def _grid_call(body, *, name, grid, in_specs, out_specs, out_shape, scratch_shapes, sem, args, fuse=None):
    if fuse is None:
        return pl.pallas_call(body, name=name, grid=grid, in_specs=in_specs, out_specs=out_specs, out_shape=out_shape,
                              scratch_shapes=scratch_shapes, compiler_params=_cp(sem))(*args)
    chunked, whole = fuse
    ex = list(chunked) + list(whole)
    n, n_in, n_out, n_sc = len(ex), len(in_specs), len(out_specs), len(scratch_shapes)

    def fused(*refs):
        ins, ex_in = refs[:n_in], refs[n_in:n_in + n]
        outs, ex_out = refs[n_in + n:n_in + n + n_out], refs[n_in + n + n_out:n_in + 2 * n + n_out]
        scratch = refs[n_in + 2 * n + n_out:n_in + 2 * n + n_out + n_sc]
        start, wait = _exchange_plan(ex_in, ex_out, *refs[-3:], len(chunked))
        first = functools.reduce(jnp.logical_and, [pl.program_id(a) == 0 for a in range(len(grid))])
        last = functools.reduce(jnp.logical_and, [pl.program_id(a) == g - 1 for a, g in enumerate(grid)])
        pl.when(first)(start)
        body(*ins, *outs, *scratch)
        pl.when(last)(wait)

    res = pl.pallas_call(
        fused, name=name, grid=grid, in_specs=list(in_specs) + [_HBM_ANY] * n,
        out_specs=list(out_specs) + [_HBM_ANY] * n, out_shape=list(out_shape) + _exchange_out_shapes(chunked, whole),
        scratch_shapes=list(scratch_shapes) + _exchange_sems(n),
        compiler_params=_cp(("arbitrary",) * len(grid)))(*args, *ex)
    return res[:n_out], res[n_out:]


def _mm_nn(a, b, *, name, out_dtype=f32, res=None, tn=1024, fuse=None):
    M, K = a.shape
    tm = _tile(M, 1024)
    if b.ndim == 3:
        tn = b.shape[2]
        N = b.shape[0] * tn
        b_spec = pl.BlockSpec((None, K, tn), lambda i, j: (j, 0, 0))
    else:
        N = b.shape[1]
        tn = _tile(N, tn)
        b_spec = pl.BlockSpec((K, tn), lambda i, j: (0, j))

    def body(*refs):
        if res is None:
            a_ref, b_ref, o_ref = refs
        else:
            a_ref, b_ref, r_ref, o_ref = refs
        acc = jnp.dot(a_ref[...], b_ref[...], preferred_element_type=f32)
        if res is not None:
            acc = acc + r_ref[...]
        o_ref[...] = acc.astype(out_dtype)

    in_specs = [pl.BlockSpec((tm, K), lambda i, j: (i, 0)), b_spec]
    args = [a, b]
    if res is not None:
        in_specs.append(pl.BlockSpec((tm, tn), lambda i, j: (i, j)))
        args.append(res)
    out = _grid_call(
        body, name=name, grid=(M // tm, N // tn), in_specs=in_specs,
        out_specs=[pl.BlockSpec((tm, tn), lambda i, j: (i, j))],
        out_shape=[jax.ShapeDtypeStruct((M, N), out_dtype)], scratch_shapes=[],
        sem=("parallel", "parallel"), args=args, fuse=fuse)
    return out[0] if fuse is None else (out[0][0], out[1])


def _mm_nt(a, b, *, name, out_dtype=f32, fuse=None):
    M, K = a.shape
    tm = _tile(M, 1024)
    if b.ndim == 3:
        nk, N, tk = b.shape
        b_spec = pl.BlockSpec((None, N, tk), lambda i, k: (k, 0, 0))
    else:
        N = b.shape[0]
        tk = _tile(K, 1024) if K % 1024 == 0 else _tile(K, 512)
        nk = K // tk
        b_spec = pl.BlockSpec((N, tk), lambda i, k: (0, k))

    def body(a_ref, b_ref, o_ref, acc_ref):
        k = pl.program_id(1)
        part = lax.dot_general(a_ref[...], b_ref[...], (((1,), (1,)), ((), ())), preferred_element_type=f32)

        @pl.when(k == 0)
        def _():
            acc_ref[...] = part

        @pl.when(k > 0)
        def _():
            acc_ref[...] += part

        @pl.when(k == nk - 1)
        def _():
            o_ref[...] = acc_ref[...].astype(out_dtype)

    out = _grid_call(
        body, name=name, grid=(M // tm, nk),
        in_specs=[pl.BlockSpec((tm, tk), lambda i, k: (i, k)), b_spec],
        out_specs=[pl.BlockSpec((tm, N), lambda i, k: (i, 0))],
        out_shape=[jax.ShapeDtypeStruct((M, N), out_dtype)],
        scratch_shapes=[pltpu.VMEM((tm, N), f32)],
        sem=("parallel", "arbitrary"), args=[a, b], fuse=fuse)
    return out[0] if fuse is None else (out[0][0], out[1])


def _load_once(src_hbm, dst_vmem, sem):
    @pl.when(pl.program_id(0) == 0)
    def _():
        cp = pltpu.make_async_copy(src_hbm, dst_vmem, sem)
        cp.start()
        cp.wait()


def _mm_nn_resident(a, b, *, name, tm=256, fuse=None):
    M, K = a.shape
    nch, _, tn = b.shape
    tm = _tile(M, tm)

    def body(a_ref, b_hbm, o_ref, bbuf, sem):
        _load_once(b_hbm, bbuf, sem)
        av = a_ref[...]
        for j in range(nch):
            o_ref[:, j * tn:(j + 1) * tn] = jnp.dot(av, bbuf[j], preferred_element_type=f32)

    out = _grid_call(
        body, name=name, grid=(M // tm,), in_specs=[pl.BlockSpec((tm, K), lambda i: (i, 0)), _HBM_ANY],
        out_specs=[pl.BlockSpec((tm, nch * tn), lambda i: (i, 0))],
        out_shape=[jax.ShapeDtypeStruct((M, nch * tn), f32)],
        scratch_shapes=[pltpu.VMEM(b.shape, b.dtype), pltpu.SemaphoreType.DMA],
        sem=("arbitrary",), args=[a, b], fuse=fuse)
    return out[0] if fuse is None else (out[0][0], out[1])


def _mm_nt_resident(a, b, *, name, fuse=None):
    M, K = a.shape
    nch, N, tk = b.shape
    tm = _tile(M, 512)

    def body(a_ref, b_hbm, o_ref, bbuf, sem):
        _load_once(b_hbm, bbuf, sem)
        acc = None
        for k in range(nch):
            part = lax.dot_general(a_ref[:, k * tk:(k + 1) * tk], bbuf[k], (((1,), (1,)), ((), ())),
                                   preferred_element_type=f32)
            acc = part if acc is None else acc + part
        o_ref[...] = acc

    out = _grid_call(
        body, name=name, grid=(M // tm,), in_specs=[pl.BlockSpec((tm, K), lambda i: (i, 0)), _HBM_ANY],
        out_specs=[pl.BlockSpec((tm, N), lambda i: (i, 0))],
        out_shape=[jax.ShapeDtypeStruct((M, N), f32)],
        scratch_shapes=[pltpu.VMEM(b.shape, b.dtype), pltpu.SemaphoreType.DMA],
        sem=("arbitrary",), args=[a, b], fuse=fuse)
    return out[0] if fuse is None else (out[0][0], out[1])


def _mm_tn(a, b, *, name, out_dtype=f32, tn=512, chunks=None, a_cols=None, fuse=None):
    S, Ka = a.shape
    a_blk = 0
    if a_cols is not None:
        a_blk, Ka = a_cols
    N = b.shape[1]
    ts = _tile(S, 2048)
    ns = S // ts
    if chunks:
        tn = N // chunks
        out_spec = pl.BlockSpec((None, Ka, tn), lambda j, s: (j, 0, 0))
        out_shape = jax.ShapeDtypeStruct((chunks, Ka, tn), out_dtype)
    else:
        tn = _tile(N, tn)
        out_spec = pl.BlockSpec((Ka, tn), lambda j, s: (0, j))
        out_shape = jax.ShapeDtypeStruct((Ka, N), out_dtype)

    def body(a_ref, b_ref, o_ref, acc_ref):
        s = pl.program_id(1)
        part = lax.dot_general(a_ref[...], b_ref[...], (((0,), (0,)), ((), ())), preferred_element_type=f32)

        @pl.when(s == 0)
        def _():
            acc_ref[...] = part

        @pl.when(s > 0)
        def _():
            acc_ref[...] += part

        @pl.when(s == ns - 1)
        def _():
            o_ref[...] = acc_ref[...].astype(out_dtype)

    out = _grid_call(
        body, name=name, grid=(N // tn, ns),
        in_specs=[pl.BlockSpec((ts, Ka), lambda j, s: (s, a_blk)), pl.BlockSpec((ts, tn), lambda j, s: (s, j))],
        out_specs=[out_spec], out_shape=[out_shape],
        scratch_shapes=[pltpu.VMEM((Ka, tn), f32)],
        sem=("parallel", "arbitrary"), args=[a, b], fuse=fuse)
    return out[0] if fuse is None else (out[0][0], out[1])


def _mm_out_loss(u, w, x_res, target, *, name):
    M, K = u.shape
    N = w.shape[1]
    tm = _tile(M, 512)
    nm = M // tm

    def body(u_ref, w_ref, x_ref, t_ref, dy_ref, dyb_ref, loss_ref, acc_ref):
        i = pl.program_id(0)
        y = jnp.dot(u_ref[...], w_ref[...], preferred_element_type=f32) + x_ref[...]
        err = y - t_ref[...]
        dy = err * (1.0 / N)
        dy_ref[...] = dy
        dyb_ref[...] = dy.astype(bf16)
        part = jnp.sum(err * err, axis=0, keepdims=True)

        @pl.when(i == 0)
        def _():
            acc_ref[...] = part

        @pl.when(i > 0)
        def _():
            acc_ref[...] += part

        @pl.when(i == nm - 1)
        def _():
            tot = jnp.sum(acc_ref[...], axis=1, keepdims=True)
            loss_ref[...] = jnp.broadcast_to(tot * (0.5 / N), (8, LANES))

    return pl.pallas_call(
        body, name=name, grid=(nm,),
        in_specs=[pl.BlockSpec((tm, K), lambda i: (i, 0)), pl.BlockSpec((K, N), lambda i: (0, 0)),
                  pl.BlockSpec((tm, N), lambda i: (i, 0)), pl.BlockSpec((tm, N), lambda i: (i, 0))],
        out_specs=[pl.BlockSpec((tm, N), lambda i: (i, 0)), pl.BlockSpec((tm, N), lambda i: (i, 0)),
                   pl.BlockSpec((8, LANES), lambda i: (0, 0))],
        out_shape=[jax.ShapeDtypeStruct((M, N), f32), jax.ShapeDtypeStruct((M, N), bf16),
                   jax.ShapeDtypeStruct((8, LANES), f32)],
        scratch_shapes=[pltpu.VMEM((1, N), f32)],
        compiler_params=_cp(("arbitrary",)),
    )(u, w, x_res, target)


def _rms_fwd(x, w, *, name):
    S, Dm = x.shape
    tm = _tile(S, 1024)

    def body(x_ref, w_ref, h_ref):
        xv = x_ref[...]
        r = lax.rsqrt(jnp.mean(xv * xv, axis=-1, keepdims=True) + EPS)
        h_ref[...] = (xv * r * w_ref[...]).astype(bf16)

    return pl.pallas_call(
        body, name=name, grid=(S // tm,),
        in_specs=[pl.BlockSpec((tm, Dm), lambda i: (i, 0)), pl.BlockSpec((1, Dm), lambda i: (0, 0))],
        out_specs=pl.BlockSpec((tm, Dm), lambda i: (i, 0)),
        out_shape=jax.ShapeDtypeStruct((S, Dm), bf16),
        compiler_params=_cp(("parallel",)),
    )(x, w)


def _rms_bwd(x, w, dh, res, *, name):
    S, Dm = x.shape
    tm = _tile(S, 512)

    def body(x_ref, w_ref, dh_ref, res_ref, dx_ref, dxb_ref, gw_ref):
        i = pl.program_id(0)
        xv = x_ref[...]
        r = lax.rsqrt(jnp.mean(xv * xv, axis=-1, keepdims=True) + EPS)
        xn = xv * r
        dh_v = dh_ref[...]
        dxn = dh_v * w_ref[...]
        dx = r * (dxn - xn * jnp.mean(dxn * xn, axis=-1, keepdims=True)) + res_ref[...]
        dx_ref[...] = dx
        dxb_ref[...] = dx.astype(bf16)
        part = jnp.sum(dh_v * xn, axis=0, keepdims=True)

        @pl.when(i == 0)
        def _():
            gw_ref[...] = part

        @pl.when(i > 0)
        def _():
            gw_ref[...] += part

    dx, dxb, gw = pl.pallas_call(
        body, name=name, grid=(S // tm,),
        in_specs=[pl.BlockSpec((tm, Dm), lambda i: (i, 0)), pl.BlockSpec((1, Dm), lambda i: (0, 0)),
                  pl.BlockSpec((tm, Dm), lambda i: (i, 0)), pl.BlockSpec((tm, Dm), lambda i: (i, 0))],
        out_specs=[pl.BlockSpec((tm, Dm), lambda i: (i, 0)), pl.BlockSpec((tm, Dm), lambda i: (i, 0)),
                   pl.BlockSpec((1, Dm), lambda i: (0, 0))],
        out_shape=[jax.ShapeDtypeStruct((S, Dm), f32), jax.ShapeDtypeStruct((S, Dm), bf16),
                   jax.ShapeDtypeStruct((1, Dm), f32)],
        compiler_params=_cp(("arbitrary",)),
    )(x, w, dh, res)
    return dx, dxb, gw


_INV_FREQ = [float(v) for v in (np.float32(ROPE_THETA) ** (-np.arange(ROT_HALF, dtype=np.float32) / np.float32(ROT_HALF))).astype(np.float32)]


def _rope_tables(pos_col):
    S = pos_col.shape[0]
    tm = _tile(S, 1024)

    def body(p_ref, c_ref, s1_ref, s2_ref):
        lane = lax.broadcasted_iota(jnp.int32, (tm, LANES), 1)
        lm = lane % HEAD_DIM
        fi = lm % ROT_HALF
        inv = jnp.zeros((tm, LANES), f32)
        for k in range(ROT_HALF):
            inv = jnp.where(fi == k, _INV_FREQ[k], inv)
        ang = p_ref[...].astype(f32) * inv
        cs = jnp.cos(ang)
        sn = jnp.sin(ang)
        c_ref[...] = jnp.where(lm < 2 * ROT_HALF, cs, 1.0)
        s1_ref[...] = jnp.where((lm >= ROT_HALF) & (lm < 2 * ROT_HALF), sn, 0.0)
        s2_ref[...] = jnp.where(lm < ROT_HALF, -sn, 0.0)

    spec = pl.BlockSpec((tm, LANES), lambda i: (i, 0))
    return pl.pallas_call(
        body, name="rope_tables", grid=(S // tm,),
        in_specs=[pl.BlockSpec((tm, 1), lambda i: (i, 0))],
        out_specs=[spec, spec, spec],
        out_shape=[jax.ShapeDtypeStruct((S, LANES), f32)] * 3,
        compiler_params=_cp(("parallel",)),
    )(pos_col)


def _head_mean(v, m):
    hi = v.astype(bf16)
    lo = (v - hi.astype(f32)).astype(bf16)
    return jnp.dot(hi, m, preferred_element_type=f32) + jnp.dot(lo, m, preferred_element_type=f32)


def _head_mean_matrix():
    i = np.arange(LANES)
    return jnp.asarray(((i[:, None] // HEAD_DIM) == (i[None, :] // HEAD_DIM)).astype(np.float32) / HEAD_DIM, dtype=bf16)


def _in_proj0(h, b, tabs, nw, hm, *, fuse=None):
    M, K = h.shape
    nch, _, tn = b.shape
    tm = _tile(M, 256)

    def body(a_ref, b_hbm, c_ref, s1_ref, s2_ref, nw_ref, m_ref, o_ref, qk_ref, bbuf, sem):
        _load_once(b_hbm, bbuf, sem)
        av = a_ref[...]
        c, s1, s2, m = c_ref[...], s1_ref[...], s2_ref[...], m_ref[...]
        for j in range(nch):
            res = jnp.dot(av, bbuf[j], preferred_element_type=f32)
            o_ref[:, j * tn:(j + 1) * tn] = res
            for p in range(tn // LANES):
                col = j * tn + p * LANES
                if col >= E_V:
                    continue
                w = nw_ref[0:1, :] if col < E_K else nw_ref[1:2, :]
                t = res[:, p * LANES:(p + 1) * LANES]
                that = t * lax.rsqrt(_head_mean(t * t, m) + EPS) * w
                qk_ref[:, col:col + LANES] = (
                    that * c + pltpu.roll(that, ROT_HALF, axis=1) * s1 + pltpu.roll(that, LANES - ROT_HALF, axis=1) * s2)

    tab = pl.BlockSpec((tm, LANES), lambda i: (i, 0))
    out = _grid_call(
        body, name="in_proj0", grid=(M // tm,),
        in_specs=[pl.BlockSpec((tm, K), lambda i: (i, 0)), _HBM_ANY, tab, tab, tab,
                  pl.BlockSpec((2, LANES), lambda i: (0, 0)), pl.BlockSpec((LANES, LANES), lambda i: (0, 0))],
        out_specs=[pl.BlockSpec((tm, nch * tn), lambda i: (i, 0)), pl.BlockSpec((tm, E_V), lambda i: (i, 0))],
        out_shape=[jax.ShapeDtypeStruct((M, nch * tn), f32), jax.ShapeDtypeStruct((M, E_V), f32)],
        scratch_shapes=[pltpu.VMEM(b.shape, b.dtype), pltpu.SemaphoreType.DMA],
        sem=("arbitrary",), args=[h, b, *tabs, nw, hm], fuse=fuse)
    return out if fuse is None else (*out[0], out[1])


def _key_geometry(nparts):
    qr = QBLK // nparts
    rho = lax.broadcasted_iota(jnp.int32, (2 * QBLK, 2 * QBLK), 0) % QBLK
    kap = lax.broadcasted_iota(jnp.int32, (2 * QBLK, 2 * QBLK), 1)
    n_q = QBLK + nparts * (rho % qr) + rho // qr
    tt = kap % (2 * qr)
    n_k = nparts * tt + kap // (2 * qr)
    dist = n_q - n_k
    return (dist >= 0) & (dist <= QBLK), (tt < qr).astype(jnp.int32)


def _stack_heads(t, lo):
    zero = jnp.zeros_like(t)
    return jnp.concatenate([jnp.where(lo, t, zero), jnp.where(lo, zero, t)], axis=0)


def _attn_block_fwd(qb, kcat, vcat, mask, lo):
    s = lax.dot_general(_stack_heads(qb, lo), kcat, (((1,), (1,)), ((), ())), preferred_element_type=f32) * SCALE
    s = jnp.where(mask, s, NEG)
    mx = jnp.max(s, axis=-1, keepdims=True)
    pexp = jnp.exp(s - mx)
    den = jnp.sum(pexp, axis=-1, keepdims=True)
    pn = (pexp * (1.0 / den)).astype(bf16)
    o2 = jnp.dot(pn, vcat, preferred_element_type=f32)
    lse2 = jnp.broadcast_to(mx + jnp.log(den), (2 * QBLK, LANES))
    return jnp.where(lo, o2[:QBLK], o2[QBLK:]), jnp.where(lo, lse2[:QBLK], lse2[QBLK:])


def _attn_block_bwd(qb, dob, kcat, vcat, lt, ds, mask, lo):
    lt_sw = pltpu.roll(lt, HEAD_DIM, axis=1)
    ds_sw = pltpu.roll(ds, HEAD_DIM, axis=1)
    lt2 = jnp.concatenate([jnp.where(lo, lt, lt_sw), jnp.where(lo, lt_sw, lt)], axis=0)
    ds2 = jnp.concatenate([jnp.where(lo, ds, ds_sw), jnp.where(lo, ds_sw, ds)], axis=0)
    q2 = _stack_heads(qb, lo)
    do2 = _stack_heads(dob, lo)
    s = lax.dot_general(q2, kcat, (((1,), (1,)), ((), ())), preferred_element_type=f32) * SCALE
    s = jnp.where(mask, s, NEG)
    prob = jnp.exp(s - jnp.concatenate([lt2, lt2], axis=1))
    dp = lax.dot_general(do2, vcat, (((1,), (1,)), ((), ())), preferred_element_type=f32)
    dsb = (prob * (dp - jnp.concatenate([ds2, ds2], axis=1)) * SCALE).astype(bf16)
    dq2 = jnp.dot(dsb, kcat, preferred_element_type=f32)
    dk = lax.dot_general(dsb, q2, (((0,), (0,)), ((), ())), preferred_element_type=f32)
    dv = lax.dot_general(prob.astype(bf16), do2, (((0,), (0,)), ((), ())), preferred_element_type=f32)
    return jnp.where(lo, dq2[:QBLK], dq2[QBLK:]), dk, dv


ATT_ROWS = 1024
ATT_UNROLL = 4


def _attn_fwd_local(qk, proj):
    S = qk.shape[0]
    tr = _tile(S, ATT_ROWS)
    lw = 4 * LANES
    nb = tr // QBLK

    def body(q_ref, k_ref, kh_ref, v_ref, vh_ref, o_ref, lse_ref, kbuf, vbuf):
        j = pl.program_id(0)
        kbuf[0:QBLK, :] = jnp.where(j > 0, kh_ref[...], 0.0)
        kbuf[QBLK:, :] = k_ref[...]
        vbuf[0:QBLK, :] = jnp.where(j > 0, vh_ref[...], 0.0)
        vbuf[QBLK:, :] = v_ref[...]
        band, is_prev = _key_geometry(1)
        lo = lax.broadcasted_iota(jnp.int32, (QBLK, LANES), 1) < HEAD_DIM

        def blk(c, carry):
            r0 = pl.multiple_of(c * QBLK, QBLK)
            first = jnp.where((c == 0) & (j == 0), 1, 0)
            mask = band & (is_prev * first == 0)
            for pp in range(lw // LANES):
                lanes = slice(pp * LANES, (pp + 1) * LANES)
                o, lse = _attn_block_fwd(q_ref[pl.ds(r0, QBLK), lanes].astype(bf16),
                                         kbuf[pl.ds(r0, 2 * QBLK), lanes].astype(bf16),
                                         vbuf[pl.ds(r0, 2 * QBLK), lanes].astype(bf16), mask, lo)
                o_ref[pl.ds(r0, QBLK), lanes] = o
                lse_ref[pl.ds(r0, QBLK), lanes] = lse
            return carry

        lax.fori_loop(0, nb, blk, 0, unroll=ATT_UNROLL)

    def halo(col):
        return pl.BlockSpec((QBLK, lw), lambda j, l: (jnp.maximum(j * nb - 1, 0), col + l))

    def tile(col):
        return pl.BlockSpec((tr, lw), lambda j, l: (j, col + l))

    return pl.pallas_call(
        body, name="attn_fwd0", grid=(S // tr, A_WIDTH // lw),
        in_specs=[tile(E_Q // lw), tile(E_K // lw), halo(E_K // lw), tile(E_V // lw), halo(E_V // lw)],
        out_specs=[tile(0), tile(0)],
        out_shape=[jax.ShapeDtypeStruct((S, A_WIDTH), f32)] * 2,
        scratch_shapes=[pltpu.VMEM((QBLK + tr, lw), f32)] * 2,
        compiler_params=_cp(("parallel", "parallel")),
    )(qk, qk, qk, proj, proj)


def _attn_bwd_local(qk, proj, do_a, lt, dsum, fuse=None):
    S = qk.shape[0]
    tr = _tile(S, ATT_ROWS)
    lw = 2 * LANES
    nb = tr // QBLK
    nt = S // tr

    def body(q_ref, qn_ref, do_ref, don_ref, lt_ref, ltn_ref, ds_ref, dsn_ref, k_ref, kh_ref, v_ref, vh_ref,
             dq_ref, dk_ref, dv_ref, kbuf, vbuf, dkbuf, dvbuf):
        j = pl.program_id(0)
        zeros = jnp.zeros((QBLK, lw), f32)
        kbuf[0:QBLK, :] = jnp.where(j > 0, kh_ref[...], 0.0)
        kbuf[pl.ds(QBLK, tr), :] = k_ref[...]
        kbuf[pl.ds(QBLK + tr, QBLK), :] = zeros
        vbuf[0:QBLK, :] = jnp.where(j > 0, vh_ref[...], 0.0)
        vbuf[pl.ds(QBLK, tr), :] = v_ref[...]
        vbuf[pl.ds(QBLK + tr, QBLK), :] = zeros
        dkbuf[...] = jnp.zeros_like(dkbuf)
        dvbuf[...] = jnp.zeros_like(dvbuf)
        band, is_prev = _key_geometry(1)
        lo = lax.broadcasted_iota(jnp.int32, (QBLK, LANES), 1) < HEAD_DIM

        def blk(c, carry):
            r0 = pl.multiple_of(c * QBLK, QBLK)
            first = jnp.where((c == 0) & (j == 0), 1, 0)
            mask = band & (is_prev * first == 0)
            for pp in range(lw // LANES):
                lanes = slice(pp * LANES, (pp + 1) * LANES)
                dq, dk, dv = _attn_block_bwd(
                    q_ref[pl.ds(r0, QBLK), lanes].astype(bf16), do_ref[pl.ds(r0, QBLK), lanes].astype(bf16),
                    kbuf[pl.ds(r0, 2 * QBLK), lanes].astype(bf16), vbuf[pl.ds(r0, 2 * QBLK), lanes].astype(bf16),
                    lt_ref[pl.ds(r0, QBLK), lanes], ds_ref[pl.ds(r0, QBLK), lanes], mask, lo)
                dq_ref[pl.ds(r0, QBLK), lanes] = dq
                dkbuf[pl.ds(r0, 2 * QBLK), lanes] += dk
                dvbuf[pl.ds(r0, 2 * QBLK), lanes] += dv
            return carry

        lax.fori_loop(0, nb, blk, 0, unroll=ATT_UNROLL)

        @pl.when(j < nt - 1)
        def _():
            mask = band & (is_prev == 1)
            for pp in range(lw // LANES):
                lanes = slice(pp * LANES, (pp + 1) * LANES)
                _, dk, dv = _attn_block_bwd(
                    qn_ref[:, lanes].astype(bf16), don_ref[:, lanes].astype(bf16),
                    kbuf[pl.ds(tr, 2 * QBLK), lanes].astype(bf16), vbuf[pl.ds(tr, 2 * QBLK), lanes].astype(bf16),
                    ltn_ref[:, lanes], dsn_ref[:, lanes], mask, lo)
                dkbuf[pl.ds(tr, 2 * QBLK), lanes] += dk
                dvbuf[pl.ds(tr, 2 * QBLK), lanes] += dv

        dk_ref[...] = dkbuf[pl.ds(QBLK, tr), :]
        dv_ref[...] = dvbuf[pl.ds(QBLK, tr), :]

    def prev_halo(col):
        return pl.BlockSpec((QBLK, lw), lambda j, l: (jnp.maximum(j * nb - 1, 0), col + l))

    def next_halo(col):
        return pl.BlockSpec((QBLK, lw), lambda j, l: (jnp.minimum((j + 1) * nb, S // QBLK - 1), col + l))

    def tile(col):
        return pl.BlockSpec((tr, lw), lambda j, l: (j, col + l))

    return _grid_call(
        body, name="attn_bwd0", grid=(nt, A_WIDTH // lw),
        in_specs=[tile(E_Q // lw), next_halo(E_Q // lw), tile(0), next_halo(0), tile(0), next_halo(0), tile(0), next_halo(0),
                  tile(E_K // lw), prev_halo(E_K // lw), tile(E_V // lw), prev_halo(E_V // lw)],
        out_specs=[tile(0)] * 3,
        out_shape=[jax.ShapeDtypeStruct((S, A_WIDTH), f32)] * 3,
        scratch_shapes=[pltpu.VMEM((tr + 2 * QBLK, lw), f32)] * 4,
        sem=("parallel", "parallel"), args=[qk, qk, do_a, do_a, lt, lt, dsum, dsum, qk, qk, proj, proj], fuse=fuse)


def _stream_view(a, d):
    S, W = a.shape
    return a.reshape(S // 8, 8, W) if d == 4 else a.reshape(S // 16, 2, 8, W)


def _stream_ref(ref, d, r, part, col, lw):
    n = ref.shape[0]
    if d == 4:
        return ref.at[pl.ds(0, n), r + 4 * part, pl.ds(col, lw)]
    return ref.at[pl.ds(0, n), r // 8, r % 8, pl.ds(col, lw)]


def _stream_geometry(S, d):
    nparts = 2 if d == 4 else 1
    rows = S // (d * nparts)
    return nparts, rows, QBLK // nparts


def _attn_fwd_dil(qk, proj, g, *, name):
    S = qk.shape[0]
    d = DILATIONS[g]
    nparts, rows, qr = _stream_geometry(S, d)
    nb = rows // qr
    lw = 2 * LANES if d == 4 else 4 * LANES
    nlg = A_WIDTH // lw
    nitems = d * nlg
    ins = ((0, E_Q + A_WIDTH * g, 0), (0, E_K + A_WIDTH * g, qr), (1, E_V + A_WIDTH * g, qr))

    def body(qk_hbm, pj_hbm, o_hbm, l_hbm, qbuf, kbuf, vbuf, obuf, lbuf, in_sems, out_sems):
        i = pl.program_id(0)
        slot = i % 2
        hbm_in = (qk_hbm, pj_hbm)
        bufs_in = (qbuf, kbuf, vbuf)

        def in_copies(item, sl):
            r, lg = item // nlg, item % nlg
            cps = []
            for a in range(nparts):
                for t, (src, col, pad) in enumerate(ins):
                    cps.append(pltpu.make_async_copy(
                        _stream_ref(hbm_in[src], d, r, a, pl.multiple_of(col + lw * lg, LANES), lw),
                        bufs_in[t].at[sl, a, pl.ds(pad, rows), :], in_sems.at[sl, 3 * a + t]))
            return cps

        def out_copies(item, sl):
            r, lg = item // nlg, item % nlg
            cps = []
            for a in range(nparts):
                for t, (buf, dst) in enumerate(((obuf, o_hbm), (lbuf, l_hbm))):
                    cps.append(pltpu.make_async_copy(
                        buf.at[sl, a], _stream_ref(dst, d, r, a, pl.multiple_of(lw * lg, LANES), lw),
                        out_sems.at[sl, 2 * a + t]))
            return cps

        @pl.when(i == 0)
        def _():
            for sl in range(2):
                for a in range(nparts):
                    kbuf[sl, a, 0:qr, :] = jnp.zeros((qr, lw), f32)
                    vbuf[sl, a, 0:qr, :] = jnp.zeros((qr, lw), f32)
            for cp in in_copies(0, 0):
                cp.start()

        @pl.when(i + 1 < nitems)
        def _():
            for cp in in_copies(i + 1, 1 - slot):
                cp.start()

        for cp in in_copies(i, slot):
            cp.wait()

        @pl.when(i >= 2)
        def _():
            for cp in out_copies(i - 2, slot):
                cp.wait()

        band, is_prev = _key_geometry(nparts)
        lo = lax.broadcasted_iota(jnp.int32, (QBLK, LANES), 1) < HEAD_DIM

        def blk(c, carry):
            r0 = pl.multiple_of(c * qr, qr)
            mask = band & (is_prev * jnp.where(c == 0, 1, 0) == 0)
            for pp in range(lw // LANES):
                lanes = slice(pp * LANES, (pp + 1) * LANES)
                qb = jnp.concatenate([qbuf[slot, a, pl.ds(r0, qr), lanes] for a in range(nparts)], axis=0).astype(bf16)
                kcat = jnp.concatenate([kbuf[slot, a, pl.ds(r0, 2 * qr), lanes] for a in range(nparts)], axis=0).astype(bf16)
                vcat = jnp.concatenate([vbuf[slot, a, pl.ds(r0, 2 * qr), lanes] for a in range(nparts)], axis=0).astype(bf16)
                o, lse = _attn_block_fwd(qb, kcat, vcat, mask, lo)
                for a in range(nparts):
                    obuf[slot, a, pl.ds(r0, qr), lanes] = o[a * qr:(a + 1) * qr]
                    lbuf[slot, a, pl.ds(r0, qr), lanes] = lse[a * qr:(a + 1) * qr]
            return carry

        lax.fori_loop(0, nb, blk, 0, unroll=ATT_UNROLL)

        for cp in out_copies(i, slot):
            cp.start()

        @pl.when(i == nitems - 1)
        def _():
            for cp in out_copies(i - 1, 1 - slot) + out_copies(i, slot):
                cp.wait()

    vshape = (S // 8, 8, A_WIDTH) if d == 4 else (S // 16, 2, 8, A_WIDTH)
    o, lse = pl.pallas_call(
        body, name=name, grid=(nitems,),
        in_specs=[_HBM_ANY, _HBM_ANY], out_specs=[_HBM_ANY, _HBM_ANY],
        out_shape=[jax.ShapeDtypeStruct(vshape, f32)] * 2,
        scratch_shapes=[pltpu.VMEM((2, nparts, rows, lw), f32), pltpu.VMEM((2, nparts, qr + rows, lw), f32),
                        pltpu.VMEM((2, nparts, qr + rows, lw), f32), pltpu.VMEM((2, nparts, rows, lw), f32),
                        pltpu.VMEM((2, nparts, rows, lw), f32),
                        pltpu.SemaphoreType.DMA((2, 3 * nparts)), pltpu.SemaphoreType.DMA((2, 2 * nparts))],
        compiler_params=_cp(("arbitrary",)),
    )(_stream_view(qk, d), _stream_view(proj, d))
    return o.reshape(S, A_WIDTH), lse.reshape(S, A_WIDTH)


def _attn_bwd_dil(qk, proj, do_a, lt, dsum, g, *, name):
    S = qk.shape[0]
    d = DILATIONS[g]
    nparts, rows, qr = _stream_geometry(S, d)
    nb = rows // qr
    lw = LANES if d == 4 else 4 * LANES
    nlg = A_WIDTH // lw
    nitems = d * nlg
    ins = ((0, E_Q + A_WIDTH * g, 0), (2, 0, 0), (3, 0, 0), (4, 0, 0), (0, E_K + A_WIDTH * g, qr), (1, E_V + A_WIDTH * g, qr))
    n_in = len(ins)

    def body(qk_hbm, pj_hbm, do_hbm, lt_hbm, ds_hbm, dq_hbm, dk_hbm, dv_hbm,
             qbuf, dobuf, ltbuf, dsbuf, kbuf, vbuf, dqbuf, dkbuf, dvbuf, in_sems, out_sems):
        i = pl.program_id(0)
        slot = i % 2
        hbm_in = (qk_hbm, pj_hbm, do_hbm, lt_hbm, ds_hbm)
        bufs_in = (qbuf, dobuf, ltbuf, dsbuf, kbuf, vbuf)

        def in_copies(item, sl):
            r, lg = item // nlg, item % nlg
            cps = []
            for a in range(nparts):
                for t, (src, col, pad) in enumerate(ins):
                    cps.append(pltpu.make_async_copy(
                        _stream_ref(hbm_in[src], d, r, a, pl.multiple_of(col + lw * lg, LANES), lw),
                        bufs_in[t].at[sl, a, pl.ds(pad, rows), :], in_sems.at[sl, n_in * a + t]))
            return cps

        def out_copies(item, sl):
            r, lg = item // nlg, item % nlg
            cps = []
            for a in range(nparts):
                for t, (buf, dst, pad) in enumerate(((dqbuf, dq_hbm, 0), (dkbuf, dk_hbm, qr), (dvbuf, dv_hbm, qr))):
                    cps.append(pltpu.make_async_copy(
                        buf.at[sl, a, pl.ds(pad, rows), :],
                        _stream_ref(dst, d, r, a, pl.multiple_of(lw * lg, LANES), lw), out_sems.at[sl, 3 * a + t]))
            return cps

        @pl.when(i == 0)
        def _():
            for sl in range(2):
                for a in range(nparts):
                    kbuf[sl, a, 0:qr, :] = jnp.zeros((qr, lw), f32)
                    vbuf[sl, a, 0:qr, :] = jnp.zeros((qr, lw), f32)
            for cp in in_copies(0, 0):
                cp.start()

        @pl.when(i + 1 < nitems)
        def _():
            for cp in in_copies(i + 1, 1 - slot):
                cp.start()

        for cp in in_copies(i, slot):
            cp.wait()

        @pl.when(i >= 2)
        def _():
            for cp in out_copies(i - 2, slot):
                cp.wait()

        for a in range(nparts):
            dkbuf[slot, a] = jnp.zeros((qr + rows, lw), f32)
            dvbuf[slot, a] = jnp.zeros((qr + rows, lw), f32)
        band, is_prev = _key_geometry(nparts)
        lo = lax.broadcasted_iota(jnp.int32, (QBLK, LANES), 1) < HEAD_DIM

        def blk(c, carry):
            r0 = pl.multiple_of(c * qr, qr)
            mask = band & (is_prev * jnp.where(c == 0, 1, 0) == 0)

            def rows_of(buf, n, lanes):
                return jnp.concatenate([buf[slot, a, pl.ds(r0, n), lanes] for a in range(nparts)], axis=0)

            for pp in range(lw // LANES):
                lanes = slice(pp * LANES, (pp + 1) * LANES)
                dq, dk, dv = _attn_block_bwd(
                    rows_of(qbuf, qr, lanes).astype(bf16), rows_of(dobuf, qr, lanes).astype(bf16),
                    rows_of(kbuf, 2 * qr, lanes).astype(bf16), rows_of(vbuf, 2 * qr, lanes).astype(bf16),
                    rows_of(ltbuf, qr, lanes), rows_of(dsbuf, qr, lanes), mask, lo)
                for a in range(nparts):
                    dqbuf[slot, a, pl.ds(r0, qr), lanes] = dq[a * qr:(a + 1) * qr]
                    dkbuf[slot, a, pl.ds(r0, 2 * qr), lanes] += dk[2 * a * qr:2 * (a + 1) * qr]
                    dvbuf[slot, a, pl.ds(r0, 2 * qr), lanes] += dv[2 * a * qr:2 * (a + 1) * qr]
            return carry

        lax.fori_loop(0, nb, blk, 0, unroll=ATT_UNROLL)

        for cp in out_copies(i, slot):
            cp.start()

        @pl.when(i == nitems - 1)
        def _():
            for cp in out_copies(i - 1, 1 - slot) + out_copies(i, slot):
                cp.wait()

    vshape = (S // 8, 8, A_WIDTH) if d == 4 else (S // 16, 2, 8, A_WIDTH)
    plain = pltpu.VMEM((2, nparts, rows, lw), f32)
    padded = pltpu.VMEM((2, nparts, qr + rows, lw), f32)
    outs = pl.pallas_call(
        body, name=name, grid=(nitems,),
        in_specs=[_HBM_ANY] * 5, out_specs=[_HBM_ANY] * 3,
        out_shape=[jax.ShapeDtypeStruct(vshape, f32)] * 3,
        scratch_shapes=[plain, plain, plain, plain, padded, padded, plain, padded, padded,
                        pltpu.SemaphoreType.DMA((2, n_in * nparts)), pltpu.SemaphoreType.DMA((2, 3 * nparts))],
        compiler_params=_cp(("arbitrary",)),
    )(*[_stream_view(a, d) for a in (qk, proj, do_a, lt, dsum)])
    return [o.reshape(S, A_WIDTH) for o in outs]


def _prev_halo(tm, h, col):
    return pl.BlockSpec((h, 512), lambda i: (jnp.maximum(i * (tm // h) - 1, 0), col))


def _next_halo(tm, h, col, S):
    return pl.BlockSpec((h, 512), lambda i: (jnp.minimum((i + 1) * (tm // h), S // h - 1), col))


def _mix0_fwd(o_g, lse_g, proj, conv_w):
    S = proj.shape[0]
    tm = _tile(S, 256)

    def body(o0, o1, o2, l0, l1, l2, bg_ref, cg_ref, hb_ref, z_ref, cgh_ref, hbh_ref, w_ref,
             u_ref, oa_ref, lt_ref, tbuf):
        i = pl.program_id(0)
        ls = [l0[...], l1[...], l2[...]]
        mx = jnp.maximum(jnp.maximum(ls[0], ls[1]), ls[2])
        es = [jnp.exp(l - mx) for l in ls]
        tot = es[0] + es[1] + es[2]
        lt_ref[...] = mx + jnp.log(tot)
        inv = 1.0 / tot
        z = z_ref[...]
        sz = z * _sigmoid(z)
        oa = (es[0] * inv) * o0[...] + (es[1] * inv) * o1[...] + (es[2] * inv) * o2[...]
        oa_ref[...] = oa
        u_ref[:, :A_WIDTH] = (oa * sz[:, :A_WIDTH]).astype(bf16)
        t = cg_ref[...] * hb_ref[...]
        tbuf[0:8, :] = jnp.where(i > 0, cgh_ref[...] * hbh_ref[...], 0.0)
        tbuf[8:, :] = t
        cv = w_ref[2:3, :] * t + w_ref[1:2, :] * tbuf[pl.ds(7, tm), :] + w_ref[0:1, :] * tbuf[pl.ds(6, tm), :]
        u_ref[:, A_WIDTH:] = (bg_ref[...] * cv * sz[:, A_WIDTH:]).astype(bf16)

    row = lambda w, c: pl.BlockSpec((tm, w), lambda i: (i, c))
    return pl.pallas_call(
        body, name="mix0_fwd", grid=(S // tm,),
        in_specs=[row(512, 0)] * 6
        + [row(512, E_BG // 512), row(512, E_CG // 512), row(512, E_HB // 512), row(1024, E_Z // 1024),
           _prev_halo(tm, 8, E_CG // 512), _prev_halo(tm, 8, E_HB // 512), pl.BlockSpec((SC_WIDTH, 512), lambda i: (0, 0))],
        out_specs=[row(1024, 0), row(512, 0), row(512, 0)],
        out_shape=[jax.ShapeDtypeStruct((S, D_MODEL), bf16), jax.ShapeDtypeStruct((S, A_WIDTH), f32),
                   jax.ShapeDtypeStruct((S, A_WIDTH), f32)],
        scratch_shapes=[pltpu.VMEM((tm + 8, 512), f32)],
        compiler_params=_cp(("parallel",)),
    )(*o_g, *lse_g, proj, proj, proj, proj, proj, proj, conv_w)


def _dsilu(z, sg):
    return sg * (1.0 + z * (1.0 - sg))


def _d_gate_in(dy_ref, wo_ref):
    return lax.dot_general(dy_ref[...], wo_ref[...], (((1,), (1,)), ((), ())), preferred_element_type=f32)


def _mix0_bwd_a(dy, w_out, proj, o_a, conv_w):
    S = proj.shape[0]
    tm = _tile(S, 256)

    def body(dy_ref, wo_ref, bg_ref, cg_ref, hb_ref, z_ref, cgh_ref, hbh_ref, oa_ref, w_ref,
             dz_ref, doa_ref, ds_ref, dbg_ref, dcv_ref, tbuf):
        i = pl.program_id(0)
        lo = lax.broadcasted_iota(jnp.int32, (tm, LANES), 1) < HEAD_DIM
        z = z_ref[...]
        sg = _sigmoid(z)
        sz = z * sg
        dsz = _dsilu(z, sg)
        du_v = _d_gate_in(dy_ref, wo_ref)
        t = cg_ref[...] * hb_ref[...]
        tbuf[0:8, :] = jnp.where(i > 0, cgh_ref[...] * hbh_ref[...], 0.0)
        tbuf[8:, :] = t
        cv = w_ref[2:3, :] * t + w_ref[1:2, :] * tbuf[pl.ds(7, tm), :] + w_ref[0:1, :] * tbuf[pl.ds(6, tm), :]
        bg = bg_ref[...]
        oa = oa_ref[...]
        dz_ref[:, :A_WIDTH] = (du_v[:, :A_WIDTH] * oa * dsz[:, :A_WIDTH]).astype(bf16)
        dz_ref[:, A_WIDTH:] = (du_v[:, A_WIDTH:] * (bg * cv) * dsz[:, A_WIDTH:]).astype(bf16)
        doa = du_v[:, :A_WIDTH] * sz[:, :A_WIDTH]
        dyb = du_v[:, A_WIDTH:] * sz[:, A_WIDTH:]
        doa_ref[...] = doa
        dbg_ref[...] = (dyb * cv).astype(bf16)
        dcv_ref[...] = dyb * bg
        prod = doa * oa
        for p in range(4):
            pp = prod[:, p * LANES:(p + 1) * LANES]
            sa = jnp.sum(jnp.where(lo, pp, 0.0), axis=-1, keepdims=True)
            sb = jnp.sum(jnp.where(lo, 0.0, pp), axis=-1, keepdims=True)
            ds_ref[:, p * LANES:(p + 1) * LANES] = jnp.where(lo, sa, sb)

    row = lambda w, c: pl.BlockSpec((tm, w), lambda i: (i, c))
    return pl.pallas_call(
        body, name="mix0_bwd_a", grid=(S // tm,),
        in_specs=[row(1024, 0), pl.BlockSpec((D_MODEL, D_MODEL), lambda i: (0, 0)),
                  row(512, E_BG // 512), row(512, E_CG // 512), row(512, E_HB // 512), row(1024, E_Z // 1024),
                  _prev_halo(tm, 8, E_CG // 512), _prev_halo(tm, 8, E_HB // 512), row(512, 0),
                  pl.BlockSpec((SC_WIDTH, 512), lambda i: (0, 0))],
        out_specs=[row(1024, 0), row(512, 0), row(512, 0), row(512, 0), row(512, 0)],
        out_shape=[jax.ShapeDtypeStruct((S, D_MODEL), bf16), jax.ShapeDtypeStruct((S, A_WIDTH), f32),
                   jax.ShapeDtypeStruct((S, A_WIDTH), f32), jax.ShapeDtypeStruct((S, 512), bf16),
                   jax.ShapeDtypeStruct((S, 512), f32)],
        scratch_shapes=[pltpu.VMEM((tm + 8, 512), f32)],
        compiler_params=_cp(("parallel",)),
    )(dy, w_out, proj, proj, proj, proj, proj, proj, o_a, conv_w)


def _mix0_bwd_b(dcv, proj, conv_w):
    S = proj.shape[0]
    tm = _tile(S, 256)
    nt = S // tm

    def body(dcv_ref, dcvn_ref, cg_ref, hb_ref, cgh_ref, hbh_ref, w_ref, dcg_ref, dhb_ref, gw_ref, tbuf, dbuf):
        i = pl.program_id(0)
        cg = cg_ref[...]
        hb = hb_ref[...]
        t = cg * hb
        tbuf[0:8, :] = jnp.where(i > 0, cgh_ref[...] * hbh_ref[...], 0.0)
        tbuf[8:, :] = t
        dcv_v = dcv_ref[...]
        dbuf[0:tm, :] = dcv_v
        dbuf[tm:, :] = jnp.where(i < nt - 1, dcvn_ref[...], 0.0)
        dt = w_ref[2:3, :] * dcv_v + w_ref[1:2, :] * dbuf[pl.ds(1, tm), :] + w_ref[0:1, :] * dbuf[pl.ds(2, tm), :]
        dcg_ref[...] = (dt * hb).astype(bf16)
        dhb_ref[...] = (dt * cg).astype(bf16)
        g2 = jnp.sum(dcv_v * t, axis=0, keepdims=True)
        g1 = jnp.sum(dcv_v * tbuf[pl.ds(7, tm), :], axis=0, keepdims=True)
        g0 = jnp.sum(dcv_v * tbuf[pl.ds(6, tm), :], axis=0, keepdims=True)
        part = jnp.concatenate([g0, g1, g2, jnp.zeros((5, 512), f32)], axis=0)

        @pl.when(i == 0)
        def _():
            gw_ref[...] = part

        @pl.when(i > 0)
        def _():
            gw_ref[...] += part

    row = lambda w, c: pl.BlockSpec((tm, w), lambda i: (i, c))
    return pl.pallas_call(
        body, name="mix0_bwd_b", grid=(nt,),
        in_specs=[row(512, 0), _next_halo(tm, 8, 0, S), row(512, E_CG // 512), row(512, E_HB // 512),
                  _prev_halo(tm, 8, E_CG // 512), _prev_halo(tm, 8, E_HB // 512),
                  pl.BlockSpec((SC_WIDTH, 512), lambda i: (0, 0))],
        out_specs=[row(512, 0), row(512, 0), pl.BlockSpec((8, 512), lambda i: (0, 0))],
        out_shape=[jax.ShapeDtypeStruct((S, 512), bf16), jax.ShapeDtypeStruct((S, 512), bf16),
                   jax.ShapeDtypeStruct((8, 512), f32)],
        scratch_shapes=[pltpu.VMEM((tm + 8, 512), f32), pltpu.VMEM((tm + 8, 512), f32)],
        compiler_params=_cp(("arbitrary",)),
    )(dcv, dcv, proj, proj, proj, proj, conv_w)


def _qk_bwd(dq_g, dk_g, dv_g, proj, tabs, nw, hm, dbg, dcg, dhb, dz):
    S = proj.shape[0]
    tm = _tile(S, 256)

    def body(*refs):
        d_refs = refs[0:6]
        dv_refs = refs[6:9]
        x_ref, c_ref, s1_ref, s2_ref, nw_ref, m_ref, dbg_ref, dcg_ref, dhb_ref, dz_ref, o_ref, gw_ref = refs[9:]
        i = pl.program_id(0)
        c, s1, s2, m = c_ref[...], s1_ref[...], s2_ref[...], m_ref[...]
        accs = []
        for kind in range(2):
            w = nw_ref[kind:kind + 1, :]
            acc = jnp.zeros((1, LANES), f32)
            for gi in range(N_GROUPS):
                for p in range(4):
                    col = kind * 1536 + gi * 512 + p * LANES
                    dout = d_refs[kind * 3 + gi][:, p * LANES:(p + 1) * LANES]
                    t = x_ref[:, col:col + LANES]
                    dthat = (dout * c + pltpu.roll(dout * s1, LANES - ROT_HALF, axis=1)
                             + pltpu.roll(dout * s2, ROT_HALF, axis=1))
                    r = lax.rsqrt(_head_mean(t * t, m) + EPS)
                    tn = t * r
                    acc = acc + jnp.sum(dthat * tn, axis=0, keepdims=True)
                    dtn = dthat * w
                    o_ref[:, col:col + LANES] = (r * (dtn - tn * _head_mean(dtn * tn, m))).astype(bf16)
            accs.append(acc + pltpu.roll(acc, HEAD_DIM, axis=1))
        for gi in range(N_GROUPS):
            o_ref[:, E_V + gi * 512:E_V + (gi + 1) * 512] = dv_refs[gi][...].astype(bf16)
        o_ref[:, E_BG:E_CG] = dbg_ref[...]
        o_ref[:, E_CG:E_HB] = dcg_ref[...]
        o_ref[:, E_HB:E_Z] = dhb_ref[...]
        o_ref[:, E_Z:] = dz_ref[...]
        part = jnp.concatenate([accs[0], accs[1], jnp.zeros((6, LANES), f32)], axis=0)

        @pl.when(i == 0)
        def _():
            gw_ref[...] = part

        @pl.when(i > 0)
        def _():
            gw_ref[...] += part

    row = lambda w, c: pl.BlockSpec((tm, w), lambda i: (i, c))
    tab = row(LANES, 0)
    return pl.pallas_call(
        body, name="qk_bwd", grid=(S // tm,),
        in_specs=[row(512, 0)] * 9 + [row(3072, 0), tab, tab, tab, pl.BlockSpec((2, LANES), lambda i: (0, 0)),
                                      pl.BlockSpec((LANES, LANES), lambda i: (0, 0)),
                                      row(512, 0), row(512, 0), row(512, 0), row(1024, 0)],
        out_specs=[row(EVEN_IN, 0), pl.BlockSpec((8, LANES), lambda i: (0, 0))],
        out_shape=[jax.ShapeDtypeStruct((S, EVEN_IN), bf16), jax.ShapeDtypeStruct((8, LANES), f32)],
        compiler_params=_cp(("arbitrary",)),
    )(*dq_g, *dk_g, *dv_g, proj, *tabs, nw, hm, dbg, dcg, dhb, dz)


def _inv_count(i, tm, p):
    rowg = lax.broadcasted_iota(jnp.int32, (tm, 1), 0) + i * tm
    return 1.0 / jnp.minimum(rowg + 1, p).astype(f32)


def _layer_norm_stats(c):
    mu = jnp.mean(c, axis=-1, keepdims=True)
    cen = c - mu
    rstd = lax.rsqrt(jnp.mean(cen * cen, axis=-1, keepdims=True) + EPS)
    return cen * rstd, rstd


def _fill_pool_buf(i, ubuf, uc_ref, uch_ref):
    ubuf[0:16, :] = jnp.where(i > 0, uch_ref[...], 0.0)
    ubuf[16:, :] = uc_ref[...]


def _pooled(i, tm, ubuf, gi):
    p = POOL_SIZES[gi]
    cols = slice(gi * LANES, (gi + 1) * LANES)
    acc = ubuf[pl.ds(16, tm), cols]
    cur = acc
    for jj in range(1, p):
        acc = acc + ubuf[pl.ds(16 - jj, tm), cols]
    return acc * _inv_count(i, tm, p) - cur


def _fill_glu_buf(i, gbuf, da_ref, dg_ref, dah_ref, dgh_ref):
    gbuf[0:32, :] = jnp.where(i > 0, dah_ref[...] * _sigmoid(dgh_ref[...]), 0.0)
    gbuf[32:, :] = da_ref[...] * _sigmoid(dg_ref[...])


def _shift_copies(buf, sh, tm):
    for b in range(1, 8):
        sh[b - 1] = buf[pl.ds(b, tm + 24), :]


CONV_ROWS = 32


def _window(buf, sh, base, off, rows):
    b = off % 8
    if b == 0:
        return buf[pl.ds(base + off, rows), :]
    return sh[b - 1, pl.ds(base + (off - b), rows), :]


def _mix1_fwd(proj, pool_w, pool_scale, dconv_w, dconv_b, ln_w, ln_b):
    S = proj.shape[0]
    tm = _tile(S, 256)

    def body(uc_ref, uch_ref, da_ref, dg_ref, dah_ref, dgh_ref, za_ref, zb_ref, pw_ref, ps_ref, cw_ref, cb_ref,
             lw_ref, lb_ref, u_ref, c_ref, mc_ref, ubuf, gbuf, gsh):
        i = pl.program_id(0)
        _fill_pool_buf(i, ubuf, uc_ref, uch_ref)
        za = za_ref[...]
        for gi in range(4):
            cols = slice(gi * LANES, (gi + 1) * LANES)
            mc = jnp.dot(_pooled(i, tm, ubuf, gi).astype(bf16), pw_ref[gi], preferred_element_type=f32)
            mc_ref[:, cols] = mc
            zg = za[:, cols]
            u_ref[:, cols] = (mc * ps_ref[:, cols] * (zg * _sigmoid(zg))).astype(bf16)
        _fill_glu_buf(i, gbuf, da_ref, dg_ref, dah_ref, dgh_ref)
        _shift_copies(gbuf, gsh, tm)
        c = jnp.zeros((tm, 512), f32) + cb_ref[...]
        for k in range(D_CONV):
            c = c + cw_ref[k:k + 1, :] * _window(gbuf, gsh, 0, 32 - (D_CONV - 1) + k, tm)
        c_ref[...] = c
        yhat, _ = _layer_norm_stats(c)
        l = yhat * lw_ref[...] + lb_ref[...]
        zb = zb_ref[...]
        u_ref[:, 512:] = (l * _sigmoid(l) * (zb * _sigmoid(zb))).astype(bf16)

    row = lambda w, c: pl.BlockSpec((tm, w), lambda i: (i, c))
    vec = pl.BlockSpec((1, 512), lambda i: (0, 0))
    return pl.pallas_call(
        body, name="mix1_fwd", grid=(S // tm,),
        in_specs=[row(512, 0), _prev_halo(tm, 16, 0), row(512, 1), row(512, 2), _prev_halo(tm, 32, 1), _prev_halo(tm, 32, 2),
                  row(512, 3), row(512, 4), pl.BlockSpec((4, LANES, LANES), lambda i: (0, 0, 0)), vec,
                  pl.BlockSpec((D_CONV, 512), lambda i: (0, 0)), vec, vec, vec],
        out_specs=[row(1024, 0), row(512, 0), row(512, 0)],
        out_shape=[jax.ShapeDtypeStruct((S, D_MODEL), bf16), jax.ShapeDtypeStruct((S, 512), f32),
                   jax.ShapeDtypeStruct((S, 512), f32)],
        scratch_shapes=[pltpu.VMEM((tm + 16, 512), f32), pltpu.VMEM((tm + 32, 512), f32),
                        pltpu.VMEM((7, tm + 24, 512), f32)],
        compiler_params=_cp(("parallel",)),
    )(proj, proj, proj, proj, proj, proj, proj, proj, pool_w, pool_scale, dconv_w, dconv_b, ln_w, ln_b)


def _mix1_bwd_a(dy, w_out, proj, c, mc, pool_w, pool_scale, ln_w, ln_b):
    S = proj.shape[0]
    tm = _tile(S, 256)

    def body(dy_ref, wo_ref, za_ref, zb_ref, c_ref, mc_ref, pw_ref, ps_ref, lw_ref, lb_ref,
             dz_ref, dc_ref, dpl_ref, dmc_ref, acc_ref):
        i = pl.program_id(0)
        du_v = _d_gate_in(dy_ref, wo_ref)
        ps = ps_ref[...]
        za = za_ref[...]
        sga = _sigmoid(za)
        mcv = mc_ref[...]
        dz_ref[:, :512] = (du_v[:, :512] * (mcv * ps) * _dsilu(za, sga)).astype(bf16)
        dyc = du_v[:, :512] * (za * sga)
        g_ps = jnp.sum(dyc * mcv, axis=0, keepdims=True)
        dmc = (dyc * ps).astype(bf16)
        dmc_ref[...] = dmc
        for gi in range(4):
            cols = slice(gi * LANES, (gi + 1) * LANES)
            dpl_ref[:, cols] = lax.dot_general(dmc[:, cols], pw_ref[gi], (((1,), (1,)), ((), ())), preferred_element_type=f32)
        yhat, rstd = _layer_norm_stats(c_ref[...])
        lw = lw_ref[...]
        l = yhat * lw + lb_ref[...]
        sgl = _sigmoid(l)
        zb = zb_ref[...]
        sgb = _sigmoid(zb)
        dz_ref[:, 512:] = (du_v[:, 512:] * (l * sgl) * _dsilu(zb, sgb)).astype(bf16)
        dl = du_v[:, 512:] * (zb * sgb) * _dsilu(l, sgl)
        g_lb = jnp.sum(dl, axis=0, keepdims=True)
        g_lw = jnp.sum(dl * yhat, axis=0, keepdims=True)
        dyh = dl * lw
        dc = rstd * (dyh - jnp.mean(dyh, axis=-1, keepdims=True) - yhat * jnp.mean(dyh * yhat, axis=-1, keepdims=True))
        dc_ref[...] = dc
        g_db = jnp.sum(dc, axis=0, keepdims=True)
        part = jnp.concatenate([g_ps, g_lw, g_lb, g_db, jnp.zeros((4, 512), f32)], axis=0)

        @pl.when(i == 0)
        def _():
            acc_ref[...] = part

        @pl.when(i > 0)
        def _():
            acc_ref[...] += part

    row = lambda w, c_: pl.BlockSpec((tm, w), lambda i: (i, c_))
    vec = pl.BlockSpec((1, 512), lambda i: (0, 0))
    return pl.pallas_call(
        body, name="mix1_bwd_a", grid=(S // tm,),
        in_specs=[row(1024, 0), pl.BlockSpec((D_MODEL, D_MODEL), lambda i: (0, 0)),
                  row(512, 3), row(512, 4), row(512, 0), row(512, 0),
                  pl.BlockSpec((4, LANES, LANES), lambda i: (0, 0, 0)), vec, vec, vec],
        out_specs=[row(1024, 0), row(512, 0), row(512, 0), row(512, 0), pl.BlockSpec((8, 512), lambda i: (0, 0))],
        out_shape=[jax.ShapeDtypeStruct((S, D_MODEL), bf16), jax.ShapeDtypeStruct((S, 512), f32),
                   jax.ShapeDtypeStruct((S, 512), f32), jax.ShapeDtypeStruct((S, 512), bf16),
                   jax.ShapeDtypeStruct((8, 512), f32)],
        compiler_params=_cp(("arbitrary",)),
    )(dy, w_out, proj, proj, c, mc, pool_w, pool_scale, ln_w, ln_b)


def _mix1_bwd_b(dc, dpl, dmc, dz, proj, dconv_w):
    S = proj.shape[0]
    tm = _tile(S, 256)
    nt = S // tm

    def body(dc_ref, dcn_ref, dpl_ref, dpn_ref, dmc_ref, dz_ref, uc_ref, uch_ref, da_ref, dg_ref,
             cw_ref, o_ref, gcw_ref, gpw_ref, ubuf, dcbuf, dpbuf, dcsh, gacc):
        i = pl.program_id(0)
        last = i == nt - 1
        _fill_pool_buf(i, ubuf, uc_ref, uch_ref)
        dcbuf[0:tm, :] = dc_ref[...]
        dcbuf[tm:, :] = jnp.where(last, 0.0, dcn_ref[...])
        _shift_copies(dcbuf, dcsh, tm)
        dpl_v = dpl_ref[...]
        for gi in range(4):
            p = POOL_SIZES[gi]
            cols = slice(gi * LANES, (gi + 1) * LANES)
            dpbuf[0:tm, cols] = dpl_v[:, cols] * _inv_count(i, tm, p)
            dpbuf[tm:, cols] = jnp.where(last, 0.0, dpn_ref[:, cols] * (1.0 / p))
        gpw = []
        for gi in range(4):
            p = POOL_SIZES[gi]
            cols = slice(gi * LANES, (gi + 1) * LANES)
            acc = -dpl_v[:, cols]
            for jj in range(p):
                acc = acc + dpbuf[pl.ds(jj, tm), cols]
            o_ref[:, cols] = acc.astype(bf16)
            pooled = _pooled(i, tm, ubuf, gi).astype(bf16)
            gpw.append(lax.dot_general(pooled, dmc_ref[:, cols], (((0,), (0,)), ((), ())), preferred_element_type=f32))
        gacc[...] = jnp.zeros_like(gacc)

        def conv_rows(ci, carry):
            base = pl.multiple_of(ci * CONV_ROWS, CONV_ROWS)
            da = da_ref[pl.ds(base, CONV_ROWS), :]
            sg = _sigmoid(dg_ref[pl.ds(base, CONV_ROWS), :])
            gl = da * sg
            dgl = jnp.zeros((CONV_ROWS, 512), f32)
            for k in range(D_CONV):
                win = _window(dcbuf, dcsh, base, D_CONV - 1 - k, CONV_ROWS)
                dgl = dgl + cw_ref[k:k + 1, :] * win
                gacc[k] += jnp.sum((gl * win).reshape(CONV_ROWS // 8, 8, 512), axis=0)
            o_ref[pl.ds(base, CONV_ROWS), O_DA:O_DG] = (dgl * sg).astype(bf16)
            o_ref[pl.ds(base, CONV_ROWS), O_DG:O_Z] = (dgl * da * sg * (1.0 - sg)).astype(bf16)
            return carry

        lax.fori_loop(0, tm // CONV_ROWS, conv_rows, 0)
        o_ref[:, O_Z:] = dz_ref[...]
        gcw_part = jnp.concatenate(
            [jnp.sum(gacc[k], axis=0, keepdims=True) for k in range(D_CONV)] + [jnp.zeros((1, 512), f32)], axis=0)

        @pl.when(i == 0)
        def _():
            gcw_ref[...] = gcw_part
            for gi in range(4):
                gpw_ref[gi] = gpw[gi]

        @pl.when(i > 0)
        def _():
            gcw_ref[...] += gcw_part
            for gi in range(4):
                gpw_ref[gi] += gpw[gi]

    row = lambda w, c_: pl.BlockSpec((tm, w), lambda i: (i, c_))
    return pl.pallas_call(
        body, name="mix1_bwd_b", grid=(nt,),
        in_specs=[row(512, 0), _next_halo(tm, 32, 0, S), row(512, 0), _next_halo(tm, 16, 0, S), row(512, 0), row(1024, 0),
                  row(512, 0), _prev_halo(tm, 16, 0), row(512, 1), row(512, 2),
                  pl.BlockSpec((D_CONV, 512), lambda i: (0, 0))],
        out_specs=[row(ODD_IN, 0), pl.BlockSpec((32, 512), lambda i: (0, 0)),
                   pl.BlockSpec((4, LANES, LANES), lambda i: (0, 0, 0))],
        out_shape=[jax.ShapeDtypeStruct((S, ODD_IN), bf16), jax.ShapeDtypeStruct((32, 512), f32),
                   jax.ShapeDtypeStruct((4, LANES, LANES), f32)],
        scratch_shapes=[pltpu.VMEM((tm + 16, 512), f32), pltpu.VMEM((tm + 32, 512), f32),
                        pltpu.VMEM((tm + 16, 512), f32), pltpu.VMEM((7, tm + 24, 512), f32),
                        pltpu.VMEM((D_CONV, 8, 512), f32)],
        compiler_params=_cp(("arbitrary",)),
    )(dc, dc, dpl, dpl, dmc, dz, proj, proj, proj, proj, dconv_w)


_SMALL_LATE = ["e_q_norm_w", "e_k_norm_w", "e_conv_w", "o_norm_w", "o_pool_w", "o_pool_scale", "o_dconv_w", "o_dconv_b",
               "o_ln_w", "o_ln_b"]


def _local_step(x, pos_col, target, w, dist=None):
    hm = _head_mean_matrix()
    nw = jnp.concatenate([jnp.tile(w["e_q_norm_w"], (1, 2)), jnp.tile(w["e_k_norm_w"], (1, 2))], axis=0)
    tabs = _rope_tables(pos_col)
    pool_wb = w["o_pool_w"].astype(bf16)
    e_norm_w, e_w_in = w["e_norm_w"], w["e_w_in"]

    h0 = _rms_fwd(x, e_norm_w, name="rms0_fwd")
    if dist is None:
        proj0, qk = _in_proj0(h0, e_w_in, tabs, nw, hm)
    else:
        proj0, qk, gathered = _in_proj0(h0, e_w_in, tabs, nw, hm, fuse=([], dist[0]))
        w = {**w, **dist[1](gathered)}
    e_conv_w, e_w_out, o_norm_w, o_w_in, o_w_out = w["e_conv_w"], w["e_w_out"], w["o_norm_w"], w["o_w_in"], w["o_w_out"]
    o_pool_scale, o_dconv_w, o_dconv_b, o_ln_w, o_ln_b = (w[k] for k in ("o_pool_scale", "o_dconv_w", "o_dconv_b", "o_ln_w", "o_ln_b"))
    o_g, lse_g = [], []
    for g in range(N_GROUPS):
        o, l = _attn_fwd_local(qk, proj0) if g == 0 else _attn_fwd_dil(qk, proj0, g, name=f"attn_fwd{g}")
        o_g.append(o)
        lse_g.append(l)
    u0, o_a, lt = _mix0_fwd(o_g, lse_g, proj0, e_conv_w)
    x1 = _mm_nn(u0, e_w_out, res=x, name="out_proj0")
    h1 = _rms_fwd(x1, o_norm_w, name="rms1_fwd")
    o_w_in3 = o_w_in.reshape(1, D_MODEL, ODD_IN)
    proj1 = _mm_nn_resident(h1, o_w_in3, name="in_proj1", tm=512)
    u1, c1, mc1 = _mix1_fwd(proj1, pool_wb, o_pool_scale, o_dconv_w, o_dconv_b, o_ln_w, o_ln_b)
    dy, dyb, loss = _mm_out_loss(u1, o_w_out, x1, target, name="out_proj1_loss")
    g_o_w_out = _mm_tn(u1, dyb, name="g_w_out1", out_dtype=bf16)
    dz1, dc1, dpl1, dmc1, sums1 = _mix1_bwd_a(dyb, o_w_out, proj1, c1, mc1, pool_wb, o_pool_scale, o_ln_w, o_ln_b)
    dproj1, g_dconv_w, g_pool_w = _mix1_bwd_b(dc1, dpl1, dmc1, dz1, proj1, o_dconv_w)
    g_o_w_in = _mm_tn(h1, dproj1, name="g_w_in1", out_dtype=bf16)
    dh1 = _mm_nt_resident(dproj1, o_w_in3, name="d_h1")
    d1, d1b, g_o_norm = _rms_bwd(x1, o_norm_w, dh1, dy, name="rms1_bwd")
    g_e_w_out = _mm_tn(u0, d1b, name="g_w_out0", out_dtype=bf16)
    dz0, do_a, dsum, dbg, dcv = _mix0_bwd_a(d1b, e_w_out, proj0, o_a, e_conv_w)
    dcg, dhb, g_conv_w = _mix0_bwd_b(dcv, proj0, e_conv_w)
    fuse_a = None if dist is None else (
        [g_e_w_out.reshape(N_DEV, D_MODEL // N_DEV, D_MODEL),
         jnp.moveaxis(g_o_w_in.reshape(D_MODEL, N_DEV, ODD_IN // N_DEV), 1, 0),
         g_o_w_out.reshape(N_DEV, D_MODEL // N_DEV, D_MODEL)], [])
    dq_g, dk_g, dv_g = [], [], []
    for g in range(N_GROUPS):
        if g == 0:
            dqkv = _attn_bwd_local(qk, proj0, do_a, lt, dsum, fuse=fuse_a)
            if dist is not None:
                dqkv, recv_a = dqkv
            dq, dk, dv = dqkv
        else:
            dq, dk, dv = _attn_bwd_dil(qk, proj0, do_a, lt, dsum, g, name=f"attn_bwd{g}")
        dq_g.append(dq)
        dk_g.append(dk)
        dv_g.append(dv)
    dproj0, g_qk_norm = _qk_bwd(dq_g, dk_g, dv_g, proj0, tabs, nw, hm, dbg, dcg, dhb, dz0)
    half = D_MODEL // 2
    g_e_w_in_a = _mm_tn(h0, dproj0, name="g_w_in0a", out_dtype=bf16, chunks=N_DEV, a_cols=(0, half))
    if dist is None:
        g_e_w_in_b = _mm_tn(h0, dproj0, name="g_w_in0b", out_dtype=bf16, chunks=N_DEV, a_cols=(1, half))
    else:
        g_e_w_in_b, recv_b0 = _mm_tn(h0, dproj0, name="g_w_in0b", out_dtype=bf16, chunks=N_DEV, a_cols=(1, half),
                                     fuse=([g_e_w_in_a], []))
    grads = dict(
        e_q_norm_w=g_qk_norm[0:1, :HEAD_DIM], e_k_norm_w=g_qk_norm[1:2, :HEAD_DIM],
        e_conv_w=g_conv_w[:SC_WIDTH], e_w_out=g_e_w_out,
        o_norm_w=g_o_norm, o_w_in=g_o_w_in, o_pool_w=g_pool_w,
        o_pool_scale=sums1[0:1], o_dconv_w=g_dconv_w[:D_CONV], o_dconv_b=sums1[3:4],
        o_ln_w=sums1[1:2], o_ln_b=sums1[2:3], o_w_out=g_o_w_out)
    if dist is None:
        dh0 = _mm_nt_resident(dproj0, e_w_in, name="d_h0")
        grad_x, _, grads["e_norm_w"] = _rms_bwd(x, e_norm_w, dh0, d1, name="rms0_bwd")
        grads["e_w_in"] = jnp.concatenate([g_e_w_in_a, g_e_w_in_b], axis=1)
        return loss, grad_x, grads
    small_late, offs = _pack_rows([grads[n_] for n_ in _SMALL_LATE])
    dh0, recv_b = _mm_nt_resident(dproj0, e_w_in, name="d_h0", fuse=([g_e_w_in_b], [small_late]))
    grad_x, _, g_e_norm = _rms_bwd(x, e_norm_w, dh0, d1, name="rms0_bwd")
    recv_c = _exchange([], [g_e_norm.reshape(8, LANES)], name="exchange_e_norm")
    recv = dict(e_w_out=[recv_a[0]], o_w_in=[recv_a[1]], o_w_out=[recv_a[2]], e_w_in=[recv_b0[0], recv_b[0]],
                small_late=recv_b[1], e_norm_w=recv_c[0])
    return loss, grad_x, recv, {n_: (off, grads[n_].shape) for n_, off in zip(_SMALL_LATE, offs)}


_MESH_ID = pl.DeviceIdType.MESH
_HBM = pl.BlockSpec(memory_space=pl.ANY)


def _all_gather(arrs, *, name):
    n = len(arrs)

    def body(*refs):
        ins, outs = refs[:n], refs[n:2 * n]
        send_sems, recv_sems, local_sems = refs[2 * n:]
        x, y, c = _place()
        me, sibling = (x, y, c), (x, y, 1 - c)
        chips = [(1 - x, y), (x, 1 - y), (1 - x, 1 - y)]

        def slot(t, px, py, pc):
            return outs[t].at[4 * px + 2 * py + pc]

        def copy(t, k, block, to, src=None):
            dst = slot(t, *block)
            return pltpu.make_async_remote_copy(
                src_ref=dst if src is None else src, dst_ref=dst,
                send_sem=send_sems.at[7 * t + k], recv_sem=recv_sems.at[7 * t + k],
                device_id=to, device_id_type=_MESH_ID)

        mine = [pltpu.make_async_copy(ins[t], slot(t, *me), local_sems.at[t]) for t in range(n)]
        for cp in mine:
            cp.start()
        first = []
        for t in range(n):
            first.append(copy(t, 0, me, sibling, src=ins[t]))
            first += [copy(t, 1 + j, me, (*chip, c), src=ins[t]) for j, chip in enumerate(chips)]
        for cp in first:
            cp.start()
        passed = []
        for j, chip in enumerate(chips):
            for t in range(n):
                copy(t, 1 + j, (*chip, c), me).wait_recv()
                fwd = copy(t, 4 + j, (*chip, c), sibling)
                fwd.start()
                passed.append(fwd)
        for t in range(n):
            copy(t, 0, sibling, me).wait_recv()
            for j, chip in enumerate(chips):
                copy(t, 4 + j, (*chip, 1 - c), me).wait_recv()
        for cp in first + passed:
            cp.wait_send()
        for cp in mine:
            cp.wait()

    return pl.pallas_call(
        body, name=name,
        in_specs=[_HBM] * n, out_specs=[_HBM] * n,
        out_shape=[jax.ShapeDtypeStruct((N_DEV, *a.shape), a.dtype) for a in arrs],
        scratch_shapes=[pltpu.SemaphoreType.DMA((7 * n,)), pltpu.SemaphoreType.DMA((7 * n,)),
                        pltpu.SemaphoreType.DMA((n,))],
    )(*arrs)


def _exchange(chunked, whole, *, name):
    arrs = list(chunked) + list(whole)
    n = len(arrs)

    def body(*refs):
        start, wait = _exchange_plan(refs[:n], refs[n:2 * n], *refs[2 * n:], len(chunked))
        start()
        wait()

    return pl.pallas_call(
        body, name=name, in_specs=[_HBM] * n, out_specs=[_HBM] * n,
        out_shape=_exchange_out_shapes(chunked, whole), scratch_shapes=_exchange_sems(n),
    )(*arrs)


def _adamw(w, g, m, v):
    m2 = ADAM_B1 * m + (1.0 - ADAM_B1) * g
    v2 = ADAM_B2 * v + (1.0 - ADAM_B2) * (g * g)
    m_hat = m2 / (1.0 - ADAM_B1 ** ADAM_STEP)
    v_hat = v2 / (1.0 - ADAM_B2 ** ADAM_STEP)
    delta = -ADAM_LR * (m_hat / (jnp.sqrt(v_hat) + ADAM_EPS) + ADAM_WD * w)
    return delta, m2, v2


def _sum_adamw(parts, w, m, v, *, name):
    R, C = w.shape
    nsplit = len(parts)
    rp = R // nsplit
    tr = _tile(rp, 256)
    npt = rp // tr

    def body(*refs):
        p_refs = refs[:nsplit]
        w_ref, m_ref, v_ref, g_ref, d_ref, nm_ref, nv_ref = refs[nsplit:]
        h = pl.program_id(0)
        g = None
        for i in range(N_DEV):
            pi = p_refs[0][i]
            for q in range(1, nsplit):
                pi = jnp.where(h == q, p_refs[q][i], pi)
            g = pi.astype(f32) if g is None else g + pi.astype(f32)
        g_ref[...] = g
        d_ref[...], nm_ref[...], nv_ref[...] = _adamw(w_ref[...], g, m_ref[...], v_ref[...])

    def part_spec(q):
        return pl.BlockSpec((N_DEV, tr, C), lambda h, i: (0, jnp.where(h == q, i, 0), 0))

    spec = pl.BlockSpec((tr, C), lambda h, i: (h * npt + i, 0))
    return pl.pallas_call(
        body, name=name, grid=(nsplit, npt),
        in_specs=[part_spec(q) for q in range(nsplit)] + [spec, spec, spec],
        out_specs=[spec] * 4, out_shape=[jax.ShapeDtypeStruct((R, C), f32)] * 4,
        compiler_params=_cp(("parallel", "parallel")),
    )(*parts, w, m, v)


def _sum_parts(parts, *, name):
    _, R, C = parts.shape

    def body(p_ref, o_ref):
        g = p_ref[0]
        for i in range(1, N_DEV):
            g = g + p_ref[i]
        o_ref[...] = g

    return pl.pallas_call(body, name=name, out_shape=jax.ShapeDtypeStruct((R, C), f32),
                          compiler_params=pltpu.CompilerParams(vmem_limit_bytes=VMEM_LIMIT))(parts)


def _adamw_small(ws, gs, ms, vs):
    n = len(ws)

    def body(*refs):
        w_r, g_r, m_r, v_r = refs[:n], refs[n:2 * n], refs[2 * n:3 * n], refs[3 * n:4 * n]
        d_r, nm_r, nv_r = refs[4 * n:5 * n], refs[5 * n:6 * n], refs[6 * n:7 * n]
        for t in range(n):
            d_r[t][...], nm_r[t][...], nv_r[t][...] = _adamw(w_r[t][...], g_r[t][...], m_r[t][...], v_r[t][...])

    shapes = [jax.ShapeDtypeStruct(w.shape, f32) for w in ws]
    outs = pl.pallas_call(body, name="adamw_small", out_shape=shapes * 3)(*ws, *gs, *ms, *vs)
    return outs[:n], outs[n:2 * n], outs[2 * n:]


_WEIGHTS = ["e_norm_w", "e_w_in", "e_q_norm_w", "e_k_norm_w", "e_conv_w", "e_w_out", "o_norm_w", "o_w_in", "o_pool_w",
            "o_pool_scale", "o_dconv_w", "o_dconv_b", "o_ln_w", "o_ln_b", "o_w_out"]
_BIG = ["e_w_in", "e_w_out", "o_w_in", "o_w_out"]
_SMALL_SHARDED = ["e_conv_w", "o_norm_w", "o_pool_scale", "o_dconv_w", "o_dconv_b", "o_ln_w", "o_ln_b"]
_SMALL_ALL = ["e_norm_w", "e_q_norm_w", "e_k_norm_w", "e_conv_w", "o_norm_w", "o_pool_w", "o_pool_scale", "o_dconv_w",
              "o_dconv_b", "o_ln_w", "o_ln_b"]


def _pack_rows(pieces):
    rows, offs, r0 = [], [], 0
    for p in pieces:
        flat = p.reshape(-1)
        nr = -(-flat.shape[0] // (8 * LANES)) * 8
        rows.append(jnp.pad(flat, (0, nr * LANES - flat.shape[0])).reshape(nr, LANES))
        offs.append((r0, nr))
        r0 += nr
    return jnp.concatenate(rows, axis=0), offs


def _unpack_rows(buf, off, shape):
    r0, nr = off
    size = int(np.prod(shape))
    return buf[..., r0:r0 + nr, :].reshape(*buf.shape[:-2], nr * LANES)[..., :size].reshape(*buf.shape[:-2], *shape)


def kernel(x, positions, e_norm_w, e_w_in, e_q_norm_w, e_k_norm_w, e_conv_w, e_w_out, o_norm_w, o_w_in, o_pool_w, o_pool_scale, o_dconv_w, o_dconv_b, o_ln_w, o_ln_b, o_w_out, loss_target, m_e_norm_w, m_e_w_in, m_e_q_norm_w, m_e_k_norm_w, m_e_conv_w, m_e_w_out, m_o_norm_w, m_o_w_in, m_o_pool_w, m_o_pool_scale, m_o_dconv_w, m_o_dconv_b, m_o_ln_w, m_o_ln_b, m_o_w_out, v_e_norm_w, v_e_w_in, v_e_q_norm_w, v_e_k_norm_w, v_e_conv_w, v_e_w_out, v_o_norm_w, v_o_w_in, v_o_pool_w, v_o_pool_scale, v_o_dconv_w, v_o_dconv_b, v_o_ln_w, v_o_ln_b, v_o_w_out):
    w = dict(e_norm_w=e_norm_w, e_w_in=e_w_in, e_q_norm_w=e_q_norm_w, e_k_norm_w=e_k_norm_w, e_conv_w=e_conv_w,
             e_w_out=e_w_out, o_norm_w=o_norm_w, o_w_in=o_w_in, o_pool_w=o_pool_w, o_pool_scale=o_pool_scale,
             o_dconv_w=o_dconv_w, o_dconv_b=o_dconv_b, o_ln_w=o_ln_w, o_ln_b=o_ln_b, o_w_out=o_w_out)
    m = dict(e_norm_w=m_e_norm_w, e_w_in=m_e_w_in, e_q_norm_w=m_e_q_norm_w, e_k_norm_w=m_e_k_norm_w, e_conv_w=m_e_conv_w,
             e_w_out=m_e_w_out, o_norm_w=m_o_norm_w, o_w_in=m_o_w_in, o_pool_w=m_o_pool_w, o_pool_scale=m_o_pool_scale,
             o_dconv_w=m_o_dconv_w, o_dconv_b=m_o_dconv_b, o_ln_w=m_o_ln_w, o_ln_b=m_o_ln_b, o_w_out=m_o_w_out)
    v = dict(e_norm_w=v_e_norm_w, e_w_in=v_e_w_in, e_q_norm_w=v_e_q_norm_w, e_k_norm_w=v_e_k_norm_w, e_conv_w=v_e_conv_w,
             e_w_out=v_e_w_out, o_norm_w=v_o_norm_w, o_w_in=v_o_w_in, o_pool_w=v_o_pool_w, o_pool_scale=v_o_pool_scale,
             o_dconv_w=v_o_dconv_w, o_dconv_b=v_o_dconv_b, o_ln_w=v_o_ln_w, o_ln_b=v_o_ln_b, o_w_out=v_o_w_out)
    S = x.shape[1]
    me = 4 * lax.axis_index("x") + 2 * lax.axis_index("y") + lax.axis_index("c")

    small_local, small_offs = _pack_rows([w[n_] for n_ in _SMALL_SHARDED])
    g_e_in, = _all_gather([w["e_w_in"][0].astype(bf16)], name="gather_e_w_in")
    rest_local = [w["e_w_out"][0].astype(bf16), w["o_w_in"][0].astype(bf16), w["o_w_out"][0].astype(bf16), small_local]

    def unpack_rest(gathered):
        g_e_out, g_o_in, g_o_out, g_small = gathered
        full = {}
        for n_, off in zip(_SMALL_SHARDED, small_offs):
            shard = _unpack_rows(g_small, off, w[n_].shape[1:])
            full[n_] = jnp.moveaxis(shard, 0, -2).reshape(*shard.shape[1:-1], N_DEV * shard.shape[-1])
        return dict(
            e_conv_w=full["e_conv_w"], e_w_out=g_e_out.reshape(D_MODEL, D_MODEL), o_norm_w=full["o_norm_w"].reshape(1, D_MODEL),
            o_w_in=jnp.moveaxis(g_o_in, 0, 1).reshape(D_MODEL, ODD_IN), o_pool_scale=full["o_pool_scale"].reshape(1, 512),
            o_dconv_w=full["o_dconv_w"], o_dconv_b=full["o_dconv_b"].reshape(1, 512), o_ln_w=full["o_ln_w"].reshape(1, 512),
            o_ln_b=full["o_ln_b"].reshape(1, 512), o_w_out=g_o_out.reshape(D_MODEL, D_MODEL))

    loss_blk, grad_x, recv, small_where = _local_step(
        x[0], positions.reshape(S, 1), loss_target[0],
        dict(e_norm_w=w["e_norm_w"], e_w_in=g_e_in, e_q_norm_w=w["e_q_norm_w"], e_k_norm_w=w["e_k_norm_w"],
             o_pool_w=w["o_pool_w"][0]),
        dist=(rest_local, unpack_rest))
    loss = lax.psum(loss_blk[0, 0], ("x", "y", "c"))

    out_g, out_d, out_m, out_v = {}, {}, {}, {}
    for n_ in _BIG:
        res = _sum_adamw(recv[n_], w[n_][0], m[n_][0], v[n_][0], name="adamw_" + n_)
        out_g[n_], out_d[n_], out_m[n_], out_v[n_] = [r[None] for r in res]
    small_sum = _sum_parts(recv["small_late"], name="sum_small_grads")
    e_norm_sum = _sum_parts(recv["e_norm_w"], name="sum_e_norm_grad")
    gs = []
    for n_ in _SMALL_ALL:
        if n_ == "e_norm_w":
            gs.append(e_norm_sum.reshape(w[n_].shape))
            continue
        off, shape = small_where[n_]
        gfull = _unpack_rows(small_sum, off, shape)
        if n_ in _SMALL_SHARDED:
            width = w[n_].shape[-1]
            gfull = lax.dynamic_slice_in_dim(gfull, me * width, width, axis=gfull.ndim - 1)
        gs.append(gfull.reshape(w[n_].shape))
    ds, nms, nvs = _adamw_small([w[n_] for n_ in _SMALL_ALL], gs, [m[n_] for n_ in _SMALL_ALL], [v[n_] for n_ in _SMALL_ALL])
    for n_, g_, d_, nm_, nv_ in zip(_SMALL_ALL, gs, ds, nms, nvs):
        out_g[n_], out_d[n_], out_m[n_], out_v[n_] = g_, d_, nm_, nv_

    return (loss, grad_x[None], *[out_g[n_] for n_ in _WEIGHTS], *[out_d[n_] for n_ in _WEIGHTS],
            *[out_m[n_] for n_ in _WEIGHTS], *[out_v[n_] for n_ in _WEIGHTS])
```

```python
import functools

import numpy as np
import jax
import jax.numpy as jnp
from jax import lax
from jax.experimental import pallas as pl
from jax.experimental.pallas import tpu as pltpu

f32 = jnp.float32
bf16 = jnp.bfloat16

D_MODEL = 1024
HEAD_DIM = 64
N_GROUPS = 3
DILATIONS = (1, 4, 16)
QBLK = 128
A_WIDTH = 512
EVEN_IN = 7168
ODD_IN = 2560
POOL_SIZES = (2, 4, 8, 16)
D_CONV = 31
SC_WIDTH = 3
ROT_HALF = 8
ROPE_THETA = 500000.0
EPS = 1e-6
NEG = -1e30
SCALE = HEAD_DIM ** -0.5
N_DEV = 8
LANES = 128
VMEM_LIMIT = 48 * 1024 * 1024

ADAM_LR = 0.001
ADAM_B1 = 0.9
ADAM_B2 = 0.999
ADAM_EPS = 1e-08
ADAM_WD = 0.01
ADAM_STEP = 10

E_Q, E_K, E_V, E_BG, E_CG, E_HB, E_Z = 0, 1536, 3072, 4608, 5120, 5632, 6144
O_UC, O_DA, O_DG, O_Z = 0, 512, 1024, 1536


def _cp(sem):
    return pltpu.CompilerParams(dimension_semantics=sem, vmem_limit_bytes=VMEM_LIMIT)


_HBM_ANY = pl.BlockSpec(memory_space=pl.ANY)


def _sigmoid(z):
    return 1.0 / (1.0 + jnp.exp(-z))


def _tile(n, pref):
    t = pref
    while n % t:
        t //= 2
    return t


def _place():
    return lax.axis_index("x"), lax.axis_index("y"), lax.axis_index("c")


def _exchange_plan(ins, outs, send_sems, recv_sems, local_sems, nc):
    n = len(ins)
    x, y, c = _place()
    me_i = 4 * x + 2 * y + c

    def src(t, dev_i):
        return ins[t].at[dev_i] if t < nc else ins[t]

    def copies(arriving):
        cps = []
        for m in range(1, N_DEV):
            px = 1 - x if m & 4 else x
            py = 1 - y if m & 2 else y
            pc = 1 - c if m & 1 else c
            peer_i = 4 * px + 2 * py + pc
            for t in range(n):
                cps.append(pltpu.make_async_remote_copy(
                    src_ref=src(t, peer_i), dst_ref=outs[t].at[peer_i if arriving else me_i],
                    send_sem=send_sems.at[7 * t + m - 1], recv_sem=recv_sems.at[7 * t + m - 1],
                    device_id=(x, y, c) if arriving else (px, py, pc), device_id_type=pl.DeviceIdType.MESH))
        return cps

    def mine():
        return [pltpu.make_async_copy(src(t, me_i), outs[t].at[me_i], local_sems.at[t]) for t in range(n)]

    def start():
        for cp in mine() + copies(False):
            cp.start()

    def wait():
        for cp in copies(True):
            cp.wait_recv()
        for cp in copies(False):
            cp.wait_send()
        for cp in mine():
            cp.wait()

    return start, wait


def _exchange_sems(n):
    return [pltpu.SemaphoreType.DMA((7 * n,)), pltpu.SemaphoreType.DMA((7 * n,)), pltpu.SemaphoreType.DMA((n,))]


def _exchange_out_shapes(chunked, whole):
    return ([jax.ShapeDtypeStruct(a.shape, a.dtype) for a in chunked]
            + [jax.ShapeDtypeStruct((N_DEV, *a.shape), a.dtype) for a in whole])


def _grid_call(body, *, name, grid, in_specs, out_specs, out_shape, scratch_shapes, sem, args, fuse=None):
    if fuse is None:
        return pl.pallas_call(body, name=name, grid=grid, in_specs=in_specs, out_specs=out_specs, out_shape=out_shape,
                              scratch_shapes=scratch_shapes, compiler_params=_cp(sem))(*args)
    chunked, whole = fuse
    ex = list(chunked) + list(whole)
    n, n_in, n_out, n_sc = len(ex), len(in_specs), len(out_specs), len(scratch_shapes)

    def fused(*refs):
        ins, ex_in = refs[:n_in], refs[n_in:n_in + n]
        outs, ex_out = refs[n_in + n:n_in + n + n_out], refs[n_in + n + n_out:n_in + 2 * n + n_out]
        scratch = refs[n_in + 2 * n + n_out:n_in + 2 * n + n_out + n_sc]
        start, wait = _exchange_plan(ex_in, ex_out, *refs[-3:], len(chunked))
        first = functools.reduce(jnp.logical_and, [pl.program_id(a) == 0 for a in range(len(grid))])
        last = functools.reduce(jnp.logical_and, [pl.program_id(a) == g - 1 for a, g in enumerate(grid)])
        pl.when(first)(start)
        body(*ins, *outs, *scratch)
        pl.when(last)(wait)

    res = pl.pallas_call(
        fused, name=name, grid=grid, in_specs=list(in_specs) + [_HBM_ANY] * n,
        out_specs=list(out_specs) + [_HBM_ANY] * n, out_shape=list(out_shape) + _exchange_out_shapes(chunked, whole),
        scratch_shapes=list(scratch_shapes) + _exchange_sems(n),
        compiler_params=_cp(("arbitrary",) * len(grid)))(*args, *ex)
    return res[:n_out], res[n_out:]


def _mm_nn(a, b, *, name, out_dtype=f32, res=None, tn=1024, fuse=None):
    M, K = a.shape
    tm = _tile(M, 1024)
    if b.ndim == 3:
        tn = b.shape[2]
        N = b.shape[0] * tn
        b_spec = pl.BlockSpec((None, K, tn), lambda i, j: (j, 0, 0))
    else:
        N = b.shape[1]
        tn = _tile(N, tn)
        b_spec = pl.BlockSpec((K, tn), lambda i, j: (0, j))

    def body(*refs):
        if res is None:
            a_ref, b_ref, o_ref = refs
        else:
            a_ref, b_ref, r_ref, o_ref = refs
        acc = jnp.dot(a_ref[...], b_ref[...], preferred_element_type=f32)
        if res is not None:
            acc = acc + r_ref[...]
        o_ref[...] = acc.astype(out_dtype)

    in_specs = [pl.BlockSpec((tm, K), lambda i, j: (i, 0)), b_spec]
    args = [a, b]
    if res is not None:
        in_specs.append(pl.BlockSpec((tm, tn), lambda i, j: (i, j)))
        args.append(res)
    out = _grid_call(
        body, name=name, grid=(M // tm, N // tn), in_specs=in_specs,
        out_specs=[pl.BlockSpec((tm, tn), lambda i, j: (i, j))],
        out_shape=[jax.ShapeDtypeStruct((M, N), out_dtype)], scratch_shapes=[],
        sem=("parallel", "parallel"), args=args, fuse=fuse)
    return out[0] if fuse is None else (out[0][0], out[1])


def _mm_nt(a, b, *, name, out_dtype=f32, fuse=None):
    M, K = a.shape
    tm = _tile(M, 1024)
    if b.ndim == 3:
        nk, N, tk = b.shape
        b_spec = pl.BlockSpec((None, N, tk), lambda i, k: (k, 0, 0))
    else:
        N = b.shape[0]
        tk = _tile(K, 1024) if K % 1024 == 0 else _tile(K, 512)
        nk = K // tk
        b_spec = pl.BlockSpec((N, tk), lambda i, k: (0, k))

    def body(a_ref, b_ref, o_ref, acc_ref):
        k = pl.program_id(1)
        part = lax.dot_general(a_ref[...], b_ref[...], (((1,), (1,)), ((), ())), preferred_element_type=f32)

        @pl.when(k == 0)
        def _():
            acc_ref[...] = part

        @pl.when(k > 0)
        def _():
            acc_ref[...] += part

        @pl.when(k == nk - 1)
        def _():
            o_ref[...] = acc_ref[...].astype(out_dtype)

    out = _grid_call(
        body, name=name, grid=(M // tm, nk),
        in_specs=[pl.BlockSpec((tm, tk), lambda i, k: (i, k)), b_spec],
        out_specs=[pl.BlockSpec((tm, N), lambda i, k: (i, 0))],
        out_shape=[jax.ShapeDtypeStruct((M, N), out_dtype)],
        scratch_shapes=[pltpu.VMEM((tm, N), f32)],
        sem=("parallel", "arbitrary"), args=[a, b], fuse=fuse)
    return out[0] if fuse is None else (out[0][0], out[1])


def _load_once(src_hbm, dst_vmem, sem):
    @pl.when(pl.program_id(0) == 0)
    def _():
        cp = pltpu.make_async_copy(src_hbm, dst_vmem, sem)
        cp.start()
        cp.wait()


def _mm_nn_resident(a, b, *, name, tm=256, fuse=None):
    M, K = a.shape
    nch, _, tn = b.shape
    tm = _tile(M, tm)

    def body(a_ref, b_hbm, o_ref, bbuf, sem):
        _load_once(b_hbm, bbuf, sem)
        av = a_ref[...]
        for j in range(nch):
            o_ref[:, j * tn:(j + 1) * tn] = jnp.dot(av, bbuf[j], preferred_element_type=f32)

    out = _grid_call(
        body, name=name, grid=(M // tm,), in_specs=[pl.BlockSpec((tm, K), lambda i: (i, 0)), _HBM_ANY],
        out_specs=[pl.BlockSpec((tm, nch * tn), lambda i: (i, 0))],
        out_shape=[jax.ShapeDtypeStruct((M, nch * tn), f32)],
        scratch_shapes=[pltpu.VMEM(b.shape, b.dtype), pltpu.SemaphoreType.DMA],
        sem=("arbitrary",), args=[a, b], fuse=fuse)
    return out[0] if fuse is None else (out[0][0], out[1])


def _mm_nt_resident(a, b, *, name, fuse=None):
    M, K = a.shape
    nch, N, tk = b.shape
    tm = _tile(M, 512)

    def body(a_ref, b_hbm, o_ref, bbuf, sem):
        _load_once(b_hbm, bbuf, sem)
        acc = None
        for k in range(nch):
            part = lax.dot_general(a_ref[:, k * tk:(k + 1) * tk], bbuf[k], (((1,), (1,)), ((), ())),
                                   preferred_element_type=f32)
            acc = part if acc is None else acc + part
        o_ref[...] = acc

    out = _grid_call(
        body, name=name, grid=(M // tm,), in_specs=[pl.BlockSpec((tm, K), lambda i: (i, 0)), _HBM_ANY],
        out_specs=[pl.BlockSpec((tm, N), lambda i: (i, 0))],
        out_shape=[jax.ShapeDtypeStruct((M, N), f32)],
        scratch_shapes=[pltpu.VMEM(b.shape, b.dtype), pltpu.SemaphoreType.DMA],
        sem=("arbitrary",), args=[a, b], fuse=fuse)
    return out[0] if fuse is None else (out[0][0], out[1])


def _mm_tn(a, b, *, name, out_dtype=f32, tn=512, chunks=None, a_cols=None, fuse=None):
    S, Ka = a.shape
    a_blk = 0
    if a_cols is not None:
        a_blk, Ka = a_cols
    N = b.shape[1]
    ts = _tile(S, 2048)
    ns = S // ts
    if chunks:
        tn = N // chunks
        out_spec = pl.BlockSpec((None, Ka, tn), lambda j, s: (j, 0, 0))
        out_shape = jax.ShapeDtypeStruct((chunks, Ka, tn), out_dtype)
    else:
        tn = _tile(N, tn)
        out_spec = pl.BlockSpec((Ka, tn), lambda j, s: (0, j))
        out_shape = jax.ShapeDtypeStruct((Ka, N), out_dtype)

    def body(a_ref, b_ref, o_ref, acc_ref):
        s = pl.program_id(1)
        part = lax.dot_general(a_ref[...], b_ref[...], (((0,), (0,)), ((), ())), preferred_element_type=f32)

        @pl.when(s == 0)
        def _():
            acc_ref[...] = part

        @pl.when(s > 0)
        def _():
            acc_ref[...] += part

        @pl.when(s == ns - 1)
        def _():
            o_ref[...] = acc_ref[...].astype(out_dtype)

    out = _grid_call(
        body, name=name, grid=(N // tn, ns),
        in_specs=[pl.BlockSpec((ts, Ka), lambda j, s: (s, a_blk)), pl.BlockSpec((ts, tn), lambda j, s: (s, j))],
        out_specs=[out_spec], out_shape=[out_shape],
        scratch_shapes=[pltpu.VMEM((Ka, tn), f32)],
        sem=("parallel", "arbitrary"), args=[a, b], fuse=fuse)
    return out[0] if fuse is None else (out[0][0], out[1])


def _mm_out_loss(u, w, x_res, target, *, name):
    M, K = u.shape
    N = w.shape[1]
    tm = _tile(M, 512)
    nm = M // tm

    def body(u_ref, w_ref, x_ref, t_ref, dy_ref, dyb_ref, loss_ref, acc_ref):
        i = pl.program_id(0)
        y = jnp.dot(u_ref[...], w_ref[...], preferred_element_type=f32) + x_ref[...]
        err = y - t_ref[...]
        dy = err * (1.0 / N)
        dy_ref[...] = dy
        dyb_ref[...] = dy.astype(bf16)
        part = jnp.sum(err * err, axis=0, keepdims=True)

        @pl.when(i == 0)
        def _():
            acc_ref[...] = part

        @pl.when(i > 0)
        def _():
            acc_ref[...] += part

        @pl.when(i == nm - 1)
        def _():
            tot = jnp.sum(acc_ref[...], axis=1, keepdims=True)
            loss_ref[...] = jnp.broadcast_to(tot * (0.5 / N), (8, LANES))

    return pl.pallas_call(
        body, name=name, grid=(nm,),
        in_specs=[pl.BlockSpec((tm, K), lambda i: (i, 0)), pl.BlockSpec((K, N), lambda i: (0, 0)),
                  pl.BlockSpec((tm, N), lambda i: (i, 0)), pl.BlockSpec((tm, N), lambda i: (i, 0))],
        out_specs=[pl.BlockSpec((tm, N), lambda i: (i, 0)), pl.BlockSpec((tm, N), lambda i: (i, 0)),
                   pl.BlockSpec((8, LANES), lambda i: (0, 0))],
        out_shape=[jax.ShapeDtypeStruct((M, N), f32), jax.ShapeDtypeStruct((M, N), bf16),
                   jax.ShapeDtypeStruct((8, LANES), f32)],
        scratch_shapes=[pltpu.VMEM((1, N), f32)],
        compiler_params=_cp(("arbitrary",)),
    )(u, w, x_res, target)


def _rms_fwd(x, w, *, name):
    S, Dm = x.shape
    tm = _tile(S, 1024)

    def body(x_ref, w_ref, h_ref):
        xv = x_ref[...]
        r = lax.rsqrt(jnp.mean(xv * xv, axis=-1, keepdims=True) + EPS)
        h_ref[...] = (xv * r * w_ref[...]).astype(bf16)

    return pl.pallas_call(
        body, name=name, grid=(S // tm,),
        in_specs=[pl.BlockSpec((tm, Dm), lambda i: (i, 0)), pl.BlockSpec((1, Dm), lambda i: (0, 0))],
        out_specs=pl.BlockSpec((tm, Dm), lambda i: (i, 0)),
        out_shape=jax.ShapeDtypeStruct((S, Dm), bf16),
        compiler_params=_cp(("parallel",)),
    )(x, w)


def _out_proj_rms(u, w, res, norm_w, *, name):
    M, K = u.shape
    N = w.shape[1]
    tm = _tile(M, 512)

    def body(u_ref, w_ref, r_ref, nw_ref, y_ref, h_ref):
        y = jnp.dot(u_ref[...], w_ref[...], preferred_element_type=f32) + r_ref[...]
        y_ref[...] = y
        h_ref[...] = (y * lax.rsqrt(jnp.mean(y * y, axis=-1, keepdims=True) + EPS) * nw_ref[...]).astype(bf16)

    row = lambda width: pl.BlockSpec((tm, width), lambda i: (i, 0))
    return pl.pallas_call(
        body, name=name, grid=(M // tm,),
        in_specs=[row(K), pl.BlockSpec((K, N), lambda i: (0, 0)), row(N), pl.BlockSpec((1, N), lambda i: (0, 0))],
        out_specs=[row(N), row(N)],
        out_shape=[jax.ShapeDtypeStruct((M, N), f32), jax.ShapeDtypeStruct((M, N), bf16)],
        compiler_params=_cp(("parallel",)),
    )(u, w, res, norm_w)


def _mm_nt_rms_bwd(a, b, x, w, res, *, name):
    M, K = a.shape
    _, N, _ = b.shape
    tm = _tile(M, 512)

    def body(a_ref, b_hbm, x_ref, w_ref, res_ref, dx_ref, dxb_ref, gw_ref, bbuf, sem):
        i = pl.program_id(0)
        _load_once(b_hbm, bbuf, sem)
        dh_v = lax.dot_general(a_ref[...], bbuf[0], (((1,), (1,)), ((), ())), preferred_element_type=f32)
        xv = x_ref[...]
        r = lax.rsqrt(jnp.mean(xv * xv, axis=-1, keepdims=True) + EPS)
        xn = xv * r
        dxn = dh_v * w_ref[...]
        dx = r * (dxn - xn * jnp.mean(dxn * xn, axis=-1, keepdims=True)) + res_ref[...]
        dx_ref[...] = dx
        dxb_ref[...] = dx.astype(bf16)
        part = jnp.sum(dh_v * xn, axis=0, keepdims=True)

        @pl.when(i == 0)
        def _():
            gw_ref[...] = part

        @pl.when(i > 0)
        def _():
            gw_ref[...] += part

    row = lambda width: pl.BlockSpec((tm, width), lambda i: (i, 0))
    vec = pl.BlockSpec((1, N), lambda i: (0, 0))
    return pl.pallas_call(
        body, name=name, grid=(M // tm,),
        in_specs=[row(K), _HBM_ANY, row(N), vec, row(N)],
        out_specs=[row(N), row(N), vec],
        out_shape=[jax.ShapeDtypeStruct((M, N), f32), jax.ShapeDtypeStruct((M, N), bf16), jax.ShapeDtypeStruct((1, N), f32)],
        scratch_shapes=[pltpu.VMEM(b.shape, b.dtype), pltpu.SemaphoreType.DMA],
        compiler_params=_cp(("arbitrary",)),
    )(a, b, x, w, res)


def _rms_bwd(x, w, dh, res, *, name):
    S, Dm = x.shape
    tm = _tile(S, 512)

    def body(x_ref, w_ref, dh_ref, res_ref, dx_ref, dxb_ref, gw_ref):
        i = pl.program_id(0)
        xv = x_ref[...]
        r = lax.rsqrt(jnp.mean(xv * xv, axis=-1, keepdims=True) + EPS)
        xn = xv * r
        dh_v = dh_ref[...]
        dxn = dh_v * w_ref[...]
        dx = r * (dxn - xn * jnp.mean(dxn * xn, axis=-1, keepdims=True)) + res_ref[...]
        dx_ref[...] = dx
        dxb_ref[...] = dx.astype(bf16)
        part = jnp.sum(dh_v * xn, axis=0, keepdims=True)

        @pl.when(i == 0)
        def _():
            gw_ref[...] = part

        @pl.when(i > 0)
        def _():
            gw_ref[...] += part

    dx, dxb, gw = pl.pallas_call(
        body, name=name, grid=(S // tm,),
        in_specs=[pl.BlockSpec((tm, Dm), lambda i: (i, 0)), pl.BlockSpec((1, Dm), lambda i: (0, 0)),
                  pl.BlockSpec((tm, Dm), lambda i: (i, 0)), pl.BlockSpec((tm, Dm), lambda i: (i, 0))],
        out_specs=[pl.BlockSpec((tm, Dm), lambda i: (i, 0)), pl.BlockSpec((tm, Dm), lambda i: (i, 0)),
                   pl.BlockSpec((1, Dm), lambda i: (0, 0))],
        out_shape=[jax.ShapeDtypeStruct((S, Dm), f32), jax.ShapeDtypeStruct((S, Dm), bf16),
                   jax.ShapeDtypeStruct((1, Dm), f32)],
        compiler_params=_cp(("arbitrary",)),
    )(x, w, dh, res)
    return dx, dxb, gw


_INV_FREQ = [float(v) for v in (np.float32(ROPE_THETA) ** (-np.arange(ROT_HALF, dtype=np.float32) / np.float32(ROT_HALF))).astype(np.float32)]


def _rope_tables(pos_col):
    S = pos_col.shape[0]
    tm = _tile(S, 1024)

    def body(p_ref, c_ref, s1_ref, s2_ref):
        lane = lax.broadcasted_iota(jnp.int32, (tm, LANES), 1)
        lm = lane % HEAD_DIM
        fi = lm % ROT_HALF
        inv = jnp.zeros((tm, LANES), f32)
        for k in range(ROT_HALF):
            inv = jnp.where(fi == k, _INV_FREQ[k], inv)
        ang = p_ref[...].astype(f32) * inv
        cs = jnp.cos(ang)
        sn = jnp.sin(ang)
        c_ref[...] = jnp.where(lm < 2 * ROT_HALF, cs, 1.0)
        s1_ref[...] = jnp.where((lm >= ROT_HALF) & (lm < 2 * ROT_HALF), sn, 0.0)
        s2_ref[...] = jnp.where(lm < ROT_HALF, -sn, 0.0)

    spec = pl.BlockSpec((tm, LANES), lambda i: (i, 0))
    return pl.pallas_call(
        body, name="rope_tables", grid=(S // tm,),
        in_specs=[pl.BlockSpec((tm, 1), lambda i: (i, 0))],
        out_specs=[spec, spec, spec],
        out_shape=[jax.ShapeDtypeStruct((S, LANES), f32)] * 3,
        compiler_params=_cp(("parallel",)),
    )(pos_col)


def _head_mean(v, m):
    hi = v.astype(bf16)
    lo = (v - hi.astype(f32)).astype(bf16)
    return jnp.dot(hi, m, preferred_element_type=f32) + jnp.dot(lo, m, preferred_element_type=f32)


def _head_mean_matrix():
    i = np.arange(LANES)
    return jnp.asarray(((i[:, None] // HEAD_DIM) == (i[None, :] // HEAD_DIM)).astype(np.float32) / HEAD_DIM, dtype=bf16)


def _in_proj0(x, norm_w, b, tabs, nw, hm, *, fuse=None):
    M, K = x.shape
    nch, _, tn = b.shape
    tm = _tile(M, 256)

    def body(x_ref, w_ref, b_hbm, c_ref, s1_ref, s2_ref, nw_ref, m_ref, o_ref, qk_ref, h_ref, bbuf, sem):
        _load_once(b_hbm, bbuf, sem)
        xv = x_ref[...]
        av = (xv * lax.rsqrt(jnp.mean(xv * xv, axis=-1, keepdims=True) + EPS) * w_ref[...]).astype(bf16)
        h_ref[...] = av
        c, s1, s2, m = c_ref[...], s1_ref[...], s2_ref[...], m_ref[...]
        for j in range(nch):
            res = jnp.dot(av, bbuf[j], preferred_element_type=f32)
            o_ref[:, j * tn:(j + 1) * tn] = res
            for p in range(tn // LANES):
                col = j * tn + p * LANES
                if col >= E_V:
                    continue
                w = nw_ref[0:1, :] if col < E_K else nw_ref[1:2, :]
                t = res[:, p * LANES:(p + 1) * LANES]
                that = t * lax.rsqrt(_head_mean(t * t, m) + EPS) * w
                qk_ref[:, col:col + LANES] = (
                    that * c + pltpu.roll(that, ROT_HALF, axis=1) * s1 + pltpu.roll(that, LANES - ROT_HALF, axis=1) * s2)

    tab = pl.BlockSpec((tm, LANES), lambda i: (i, 0))
    out = _grid_call(
        body, name="in_proj0", grid=(M // tm,),
        in_specs=[pl.BlockSpec((tm, K), lambda i: (i, 0)), pl.BlockSpec((1, K), lambda i: (0, 0)), _HBM_ANY, tab, tab, tab,
                  pl.BlockSpec((2, LANES), lambda i: (0, 0)), pl.BlockSpec((LANES, LANES), lambda i: (0, 0))],
        out_specs=[pl.BlockSpec((tm, nch * tn), lambda i: (i, 0)), pl.BlockSpec((tm, E_V), lambda i: (i, 0)),
                   pl.BlockSpec((tm, K), lambda i: (i, 0))],
        out_shape=[jax.ShapeDtypeStruct((M, nch * tn), f32), jax.ShapeDtypeStruct((M, E_V), f32),
                   jax.ShapeDtypeStruct((M, K), bf16)],
        scratch_shapes=[pltpu.VMEM(b.shape, b.dtype), pltpu.SemaphoreType.DMA],
        sem=("arbitrary",), args=[x, norm_w, b, *tabs, nw, hm], fuse=fuse)
    return out if fuse is None else (*out[0], out[1])


def _key_geometry(nparts):
    qr = QBLK // nparts
    rho = lax.broadcasted_iota(jnp.int32, (2 * QBLK, 2 * QBLK), 0) % QBLK
    kap = lax.broadcasted_iota(jnp.int32, (2 * QBLK, 2 * QBLK), 1)
    n_q = QBLK + nparts * (rho % qr) + rho // qr
    tt = kap % (2 * qr)
    n_k = nparts * tt + kap // (2 * qr)
    dist = n_q - n_k
    return (dist >= 0) & (dist <= QBLK), (tt < qr).astype(jnp.int32)


def _stack_heads(t, lo):
    zero = jnp.zeros_like(t)
    return jnp.concatenate([jnp.where(lo, t, zero), jnp.where(lo, zero, t)], axis=0)


def _attn_block_fwd(qb, kcat, vcat, mask, lo):
    s = lax.dot_general(_stack_heads(qb, lo), kcat, (((1,), (1,)), ((), ())), preferred_element_type=f32) * SCALE
    s = jnp.where(mask, s, NEG)
    mx = jnp.max(s, axis=-1, keepdims=True)
    pexp = jnp.exp(s - mx)
    den = jnp.sum(pexp, axis=-1, keepdims=True)
    pn = (pexp * (1.0 / den)).astype(bf16)
    o2 = jnp.dot(pn, vcat, preferred_element_type=f32)
    lse2 = jnp.broadcast_to(mx + jnp.log(den), (2 * QBLK, LANES))
    return jnp.where(lo, o2[:QBLK], o2[QBLK:]), jnp.where(lo, lse2[:QBLK], lse2[QBLK:])


def _attn_block_bwd(qb, dob, kcat, vcat, lt, ds, mask, lo):
    lt_sw = pltpu.roll(lt, HEAD_DIM, axis=1)
    ds_sw = pltpu.roll(ds, HEAD_DIM, axis=1)
    lt2 = jnp.concatenate([jnp.where(lo, lt, lt_sw), jnp.where(lo, lt_sw, lt)], axis=0)
    ds2 = jnp.concatenate([jnp.where(lo, ds, ds_sw), jnp.where(lo, ds_sw, ds)], axis=0)
    q2 = _stack_heads(qb, lo)
    do2 = _stack_heads(dob, lo)
    s = lax.dot_general(q2, kcat, (((1,), (1,)), ((), ())), preferred_element_type=f32) * SCALE
    s = jnp.where(mask, s, NEG)
    prob = jnp.exp(s - jnp.concatenate([lt2, lt2], axis=1))
    dp = lax.dot_general(do2, vcat, (((1,), (1,)), ((), ())), preferred_element_type=f32)
    dsb = (prob * (dp - jnp.concatenate([ds2, ds2], axis=1)) * SCALE).astype(bf16)
    dq2 = jnp.dot(dsb, kcat, preferred_element_type=f32)
    dk = lax.dot_general(dsb, q2, (((0,), (0,)), ((), ())), preferred_element_type=f32)
    dv = lax.dot_general(prob.astype(bf16), do2, (((0,), (0,)), ((), ())), preferred_element_type=f32)
    return jnp.where(lo, dq2[:QBLK], dq2[QBLK:]), dk, dv


ATT_ROWS = 1024
ATT_UNROLL = 4


def _attn_fwd_local(qk, proj):
    S = qk.shape[0]
    tr = _tile(S, ATT_ROWS)
    lw = 4 * LANES
    nb = tr // QBLK

    def body(q_ref, k_ref, kh_ref, v_ref, vh_ref, o_ref, lse_ref, kbuf, vbuf):
        j = pl.program_id(0)
        kbuf[0:QBLK, :] = jnp.where(j > 0, kh_ref[...], 0.0)
        kbuf[QBLK:, :] = k_ref[...]
        vbuf[0:QBLK, :] = jnp.where(j > 0, vh_ref[...], 0.0)
        vbuf[QBLK:, :] = v_ref[...]
        band, is_prev = _key_geometry(1)
        lo = lax.broadcasted_iota(jnp.int32, (QBLK, LANES), 1) < HEAD_DIM

        def blk(c, carry):
            r0 = pl.multiple_of(c * QBLK, QBLK)
            first = jnp.where((c == 0) & (j == 0), 1, 0)
            mask = band & (is_prev * first == 0)
            for pp in range(lw // LANES):
                lanes = slice(pp * LANES, (pp + 1) * LANES)
                o, lse = _attn_block_fwd(q_ref[pl.ds(r0, QBLK), lanes].astype(bf16),
                                         kbuf[pl.ds(r0, 2 * QBLK), lanes].astype(bf16),
                                         vbuf[pl.ds(r0, 2 * QBLK), lanes].astype(bf16), mask, lo)
                o_ref[pl.ds(r0, QBLK), lanes] = o
                lse_ref[pl.ds(r0, QBLK), lanes] = lse
            return carry

        lax.fori_loop(0, nb, blk, 0, unroll=ATT_UNROLL)

    def halo(col):
        return pl.BlockSpec((QBLK, lw), lambda j, l: (jnp.maximum(j * nb - 1, 0), col + l))

    def tile(col):
        return pl.BlockSpec((tr, lw), lambda j, l: (j, col + l))

    return pl.pallas_call(
        body, name="attn_fwd0", grid=(S // tr, A_WIDTH // lw),
        in_specs=[tile(E_Q // lw), tile(E_K // lw), halo(E_K // lw), tile(E_V // lw), halo(E_V // lw)],
        out_specs=[tile(0), tile(0)],
        out_shape=[jax.ShapeDtypeStruct((S, A_WIDTH), f32)] * 2,
        scratch_shapes=[pltpu.VMEM((QBLK + tr, lw), f32)] * 2,
        compiler_params=_cp(("parallel", "parallel")),
    )(qk, qk, qk, proj, proj)


def _attn_bwd_local(qk, proj, do_a, lt, dsum, fuse=None):
    S = qk.shape[0]
    tr = _tile(S, ATT_ROWS)
    lw = 2 * LANES
    nb = tr // QBLK
    nt = S // tr

    def body(q_ref, qn_ref, do_ref, don_ref, lt_ref, ltn_ref, ds_ref, dsn_ref, k_ref, kh_ref, v_ref, vh_ref,
             dq_ref, dk_ref, dv_ref, kbuf, vbuf, dkbuf, dvbuf):
        j = pl.program_id(0)
        zeros = jnp.zeros((QBLK, lw), f32)
        kbuf[0:QBLK, :] = jnp.where(j > 0, kh_ref[...], 0.0)
        kbuf[pl.ds(QBLK, tr), :] = k_ref[...]
        kbuf[pl.ds(QBLK + tr, QBLK), :] = zeros
        vbuf[0:QBLK, :] = jnp.where(j > 0, vh_ref[...], 0.0)
        vbuf[pl.ds(QBLK, tr), :] = v_ref[...]
        vbuf[pl.ds(QBLK + tr, QBLK), :] = zeros
        dkbuf[...] = jnp.zeros_like(dkbuf)
        dvbuf[...] = jnp.zeros_like(dvbuf)
        band, is_prev = _key_geometry(1)
        lo = lax.broadcasted_iota(jnp.int32, (QBLK, LANES), 1) < HEAD_DIM

        def blk(c, carry):
            r0 = pl.multiple_of(c * QBLK, QBLK)
            first = jnp.where((c == 0) & (j == 0), 1, 0)
            mask = band & (is_prev * first == 0)
            for pp in range(lw // LANES):
                lanes = slice(pp * LANES, (pp + 1) * LANES)
                dq, dk, dv = _attn_block_bwd(
                    q_ref[pl.ds(r0, QBLK), lanes].astype(bf16), do_ref[pl.ds(r0, QBLK), lanes].astype(bf16),
                    kbuf[pl.ds(r0, 2 * QBLK), lanes].astype(bf16), vbuf[pl.ds(r0, 2 * QBLK), lanes].astype(bf16),
                    lt_ref[pl.ds(r0, QBLK), lanes], ds_ref[pl.ds(r0, QBLK), lanes], mask, lo)
                dq_ref[pl.ds(r0, QBLK), lanes] = dq
                dkbuf[pl.ds(r0, 2 * QBLK), lanes] += dk
                dvbuf[pl.ds(r0, 2 * QBLK), lanes] += dv
            return carry

        lax.fori_loop(0, nb, blk, 0, unroll=ATT_UNROLL)

        @pl.when(j < nt - 1)
        def _():
            mask = band & (is_prev == 1)
            for pp in range(lw // LANES):
                lanes = slice(pp * LANES, (pp + 1) * LANES)
                _, dk, dv = _attn_block_bwd(
                    qn_ref[:, lanes].astype(bf16), don_ref[:, lanes].astype(bf16),
                    kbuf[pl.ds(tr, 2 * QBLK), lanes].astype(bf16), vbuf[pl.ds(tr, 2 * QBLK), lanes].astype(bf16),
                    ltn_ref[:, lanes], dsn_ref[:, lanes], mask, lo)
                dkbuf[pl.ds(tr, 2 * QBLK), lanes] += dk
                dvbuf[pl.ds(tr, 2 * QBLK), lanes] += dv

        dk_ref[...] = dkbuf[pl.ds(QBLK, tr), :]
        dv_ref[...] = dvbuf[pl.ds(QBLK, tr), :]

    def prev_halo(col):
        return pl.BlockSpec((QBLK, lw), lambda j, l: (jnp.maximum(j * nb - 1, 0), col + l))

    def next_halo(col):
        return pl.BlockSpec((QBLK, lw), lambda j, l: (jnp.minimum((j + 1) * nb, S // QBLK - 1), col + l))

    def tile(col):
        return pl.BlockSpec((tr, lw), lambda j, l: (j, col + l))

    return _grid_call(
        body, name="attn_bwd0", grid=(nt, A_WIDTH // lw),
        in_specs=[tile(E_Q // lw), next_halo(E_Q // lw), tile(0), next_halo(0), tile(0), next_halo(0), tile(0), next_halo(0),
                  tile(E_K // lw), prev_halo(E_K // lw), tile(E_V // lw), prev_halo(E_V // lw)],
        out_specs=[tile(0)] * 3,
        out_shape=[jax.ShapeDtypeStruct((S, A_WIDTH), f32)] * 3,
        scratch_shapes=[pltpu.VMEM((tr + 2 * QBLK, lw), f32)] * 4,
        sem=("parallel", "parallel"), args=[qk, qk, do_a, do_a, lt, lt, dsum, dsum, qk, qk, proj, proj], fuse=fuse)


def _stream_view(a, d):
    S, W = a.shape
    return a.reshape(S // 8, 8, W) if d == 4 else a.reshape(S // 16, 2, 8, W)


def _stream_ref(ref, d, r, part, col, lw):
    n = ref.shape[0]
    if d == 4:
        return ref.at[pl.ds(0, n), r + 4 * part, pl.ds(col, lw)]
    return ref.at[pl.ds(0, n), r // 8, r % 8, pl.ds(col, lw)]


def _stream_geometry(S, d):
    nparts = 2 if d == 4 else 1
    rows = S // (d * nparts)
    return nparts, rows, QBLK // nparts


def _attn_fwd_dil(qk, proj, g, *, name):
    S = qk.shape[0]
    d = DILATIONS[g]
    nparts, rows, qr = _stream_geometry(S, d)
    nb = rows // qr
    lw = 2 * LANES if d == 4 else 4 * LANES
    nlg = A_WIDTH // lw
    nitems = d * nlg
    ins = ((0, E_Q + A_WIDTH * g, 0), (0, E_K + A_WIDTH * g, qr), (1, E_V + A_WIDTH * g, qr))

    def body(qk_hbm, pj_hbm, o_hbm, l_hbm, qbuf, kbuf, vbuf, obuf, lbuf, in_sems, out_sems):
        i = pl.program_id(0)
        slot = i % 2
        hbm_in = (qk_hbm, pj_hbm)
        bufs_in = (qbuf, kbuf, vbuf)

        def in_copies(item, sl):
            r, lg = item // nlg, item % nlg
            cps = []
            for a in range(nparts):
                for t, (src, col, pad) in enumerate(ins):
                    cps.append(pltpu.make_async_copy(
                        _stream_ref(hbm_in[src], d, r, a, pl.multiple_of(col + lw * lg, LANES), lw),
                        bufs_in[t].at[sl, a, pl.ds(pad, rows), :], in_sems.at[sl, 3 * a + t]))
            return cps

        def out_copies(item, sl):
            r, lg = item // nlg, item % nlg
            cps = []
            for a in range(nparts):
                for t, (buf, dst) in enumerate(((obuf, o_hbm), (lbuf, l_hbm))):
                    cps.append(pltpu.make_async_copy(
                        buf.at[sl, a], _stream_ref(dst, d, r, a, pl.multiple_of(lw * lg, LANES), lw),
                        out_sems.at[sl, 2 * a + t]))
            return cps

        @pl.when(i == 0)
        def _():
            for sl in range(2):
                for a in range(nparts):
                    kbuf[sl, a, 0:qr, :] = jnp.zeros((qr, lw), f32)
                    vbuf[sl, a, 0:qr, :] = jnp.zeros((qr, lw), f32)
            for cp in in_copies(0, 0):
                cp.start()

        @pl.when(i + 1 < nitems)
        def _():
            for cp in in_copies(i + 1, 1 - slot):
                cp.start()

        for cp in in_copies(i, slot):
            cp.wait()

        @pl.when(i >= 2)
        def _():
            for cp in out_copies(i - 2, slot):
                cp.wait()

        band, is_prev = _key_geometry(nparts)
        lo = lax.broadcasted_iota(jnp.int32, (QBLK, LANES), 1) < HEAD_DIM

        def blk(c, carry):
            r0 = pl.multiple_of(c * qr, qr)
            mask = band & (is_prev * jnp.where(c == 0, 1, 0) == 0)
            for pp in range(lw // LANES):
                lanes = slice(pp * LANES, (pp + 1) * LANES)
                qb = jnp.concatenate([qbuf[slot, a, pl.ds(r0, qr), lanes] for a in range(nparts)], axis=0).astype(bf16)
                kcat = jnp.concatenate([kbuf[slot, a, pl.ds(r0, 2 * qr), lanes] for a in range(nparts)], axis=0).astype(bf16)
                vcat = jnp.concatenate([vbuf[slot, a, pl.ds(r0, 2 * qr), lanes] for a in range(nparts)], axis=0).astype(bf16)
                o, lse = _attn_block_fwd(qb, kcat, vcat, mask, lo)
                for a in range(nparts):
                    obuf[slot, a, pl.ds(r0, qr), lanes] = o[a * qr:(a + 1) * qr]
                    lbuf[slot, a, pl.ds(r0, qr), lanes] = lse[a * qr:(a + 1) * qr]
            return carry

        lax.fori_loop(0, nb, blk, 0, unroll=ATT_UNROLL)

        for cp in out_copies(i, slot):
            cp.start()

        @pl.when(i == nitems - 1)
        def _():
            for cp in out_copies(i - 1, 1 - slot) + out_copies(i, slot):
                cp.wait()

    vshape = (S // 8, 8, A_WIDTH) if d == 4 else (S // 16, 2, 8, A_WIDTH)
    o, lse = pl.pallas_call(
        body, name=name, grid=(nitems,),
        in_specs=[_HBM_ANY, _HBM_ANY], out_specs=[_HBM_ANY, _HBM_ANY],
        out_shape=[jax.ShapeDtypeStruct(vshape, f32)] * 2,
        scratch_shapes=[pltpu.VMEM((2, nparts, rows, lw), f32), pltpu.VMEM((2, nparts, qr + rows, lw), f32),
                        pltpu.VMEM((2, nparts, qr + rows, lw), f32), pltpu.VMEM((2, nparts, rows, lw), f32),
                        pltpu.VMEM((2, nparts, rows, lw), f32),
                        pltpu.SemaphoreType.DMA((2, 3 * nparts)), pltpu.SemaphoreType.DMA((2, 2 * nparts))],
        compiler_params=_cp(("arbitrary",)),
    )(_stream_view(qk, d), _stream_view(proj, d))
    return o.reshape(S, A_WIDTH), lse.reshape(S, A_WIDTH)


def _attn_bwd_dil(qk, proj, do_a, lt, dsum, g, *, name):
    S = qk.shape[0]
    d = DILATIONS[g]
    nparts, rows, qr = _stream_geometry(S, d)
    nb = rows // qr
    lw = LANES if d == 4 else 4 * LANES
    nlg = A_WIDTH // lw
    nitems = d * nlg
    ins = ((0, E_Q + A_WIDTH * g, 0), (2, 0, 0), (3, 0, 0), (4, 0, 0), (0, E_K + A_WIDTH * g, qr), (1, E_V + A_WIDTH * g, qr))
    n_in = len(ins)

    def body(qk_hbm, pj_hbm, do_hbm, lt_hbm, ds_hbm, dq_hbm, dk_hbm, dv_hbm,
             qbuf, dobuf, ltbuf, dsbuf, kbuf, vbuf, dqbuf, dkbuf, dvbuf, in_sems, out_sems):
        i = pl.program_id(0)
        slot = i % 2
        hbm_in = (qk_hbm, pj_hbm, do_hbm, lt_hbm, ds_hbm)
        bufs_in = (qbuf, dobuf, ltbuf, dsbuf, kbuf, vbuf)

        def in_copies(item, sl):
            r, lg = item // nlg, item % nlg
            cps = []
            for a in range(nparts):
                for t, (src, col, pad) in enumerate(ins):
                    cps.append(pltpu.make_async_copy(
                        _stream_ref(hbm_in[src], d, r, a, pl.multiple_of(col + lw * lg, LANES), lw),
                        bufs_in[t].at[sl, a, pl.ds(pad, rows), :], in_sems.at[sl, n_in * a + t]))
            return cps

        def out_copies(item, sl):
            r, lg = item // nlg, item % nlg
            cps = []
            for a in range(nparts):
                for t, (buf, dst, pad) in enumerate(((dqbuf, dq_hbm, 0), (dkbuf, dk_hbm, qr), (dvbuf, dv_hbm, qr))):
                    cps.append(pltpu.make_async_copy(
                        buf.at[sl, a, pl.ds(pad, rows), :],
                        _stream_ref(dst, d, r, a, pl.multiple_of(lw * lg, LANES), lw), out_sems.at[sl, 3 * a + t]))
            return cps

        @pl.when(i == 0)
        def _():
            for sl in range(2):
                for a in range(nparts):
                    kbuf[sl, a, 0:qr, :] = jnp.zeros((qr, lw), f32)
                    vbuf[sl, a, 0:qr, :] = jnp.zeros((qr, lw), f32)
            for cp in in_copies(0, 0):
                cp.start()

        @pl.when(i + 1 < nitems)
        def _():
            for cp in in_copies(i + 1, 1 - slot):
                cp.start()

        for cp in in_copies(i, slot):
            cp.wait()

        @pl.when(i >= 2)
        def _():
            for cp in out_copies(i - 2, slot):
                cp.wait()

        for a in range(nparts):
            dkbuf[slot, a] = jnp.zeros((qr + rows, lw), f32)
            dvbuf[slot, a] = jnp.zeros((qr + rows, lw), f32)
        band, is_prev = _key_geometry(nparts)
        lo = lax.broadcasted_iota(jnp.int32, (QBLK, LANES), 1) < HEAD_DIM

        def blk(c, carry):
            r0 = pl.multiple_of(c * qr, qr)
            mask = band & (is_prev * jnp.where(c == 0, 1, 0) == 0)

            def rows_of(buf, n, lanes):
                return jnp.concatenate([buf[slot, a, pl.ds(r0, n), lanes] for a in range(nparts)], axis=0)

            for pp in range(lw // LANES):
                lanes = slice(pp * LANES, (pp + 1) * LANES)
                dq, dk, dv = _attn_block_bwd(
                    rows_of(qbuf, qr, lanes).astype(bf16), rows_of(dobuf, qr, lanes).astype(bf16),
                    rows_of(kbuf, 2 * qr, lanes).astype(bf16), rows_of(vbuf, 2 * qr, lanes).astype(bf16),
                    rows_of(ltbuf, qr, lanes), rows_of(dsbuf, qr, lanes), mask, lo)
                for a in range(nparts):
                    dqbuf[slot, a, pl.ds(r0, qr), lanes] = dq[a * qr:(a + 1) * qr]
                    dkbuf[slot, a, pl.ds(r0, 2 * qr), lanes] += dk[2 * a * qr:2 * (a + 1) * qr]
                    dvbuf[slot, a, pl.ds(r0, 2 * qr), lanes] += dv[2 * a * qr:2 * (a + 1) * qr]
            return carry

        lax.fori_loop(0, nb, blk, 0, unroll=ATT_UNROLL)

        for cp in out_copies(i, slot):
            cp.start()

        @pl.when(i == nitems - 1)
        def _():
            for cp in out_copies(i - 1, 1 - slot) + out_copies(i, slot):
                cp.wait()

    vshape = (S // 8, 8, A_WIDTH) if d == 4 else (S // 16, 2, 8, A_WIDTH)
    plain = pltpu.VMEM((2, nparts, rows, lw), f32)
    padded = pltpu.VMEM((2, nparts, qr + rows, lw), f32)
    outs = pl.pallas_call(
        body, name=name, grid=(nitems,),
        in_specs=[_HBM_ANY] * 5, out_specs=[_HBM_ANY] * 3,
        out_shape=[jax.ShapeDtypeStruct(vshape, f32)] * 3,
        scratch_shapes=[plain, plain, plain, plain, padded, padded, plain, padded, padded,
                        pltpu.SemaphoreType.DMA((2, n_in * nparts)), pltpu.SemaphoreType.DMA((2, 3 * nparts))],
        compiler_params=_cp(("arbitrary",)),
    )(*[_stream_view(a, d) for a in (qk, proj, do_a, lt, dsum)])
    return [o.reshape(S, A_WIDTH) for o in outs]


def _prev_halo(tm, h, col):
    return pl.BlockSpec((h, 512), lambda i: (jnp.maximum(i * (tm // h) - 1, 0), col))


def _next_halo(tm, h, col, S):
    return pl.BlockSpec((h, 512), lambda i: (jnp.minimum((i + 1) * (tm // h), S // h - 1), col))


def _mix0_fwd(o_g, lse_g, proj, conv_w):
    S = proj.shape[0]
    tm = _tile(S, 256)

    def body(o0, o1, o2, l0, l1, l2, bg_ref, cg_ref, hb_ref, z_ref, cgh_ref, hbh_ref, w_ref,
             u_ref, oa_ref, lt_ref, tbuf):
        i = pl.program_id(0)
        ls = [l0[...], l1[...], l2[...]]
        mx = jnp.maximum(jnp.maximum(ls[0], ls[1]), ls[2])
        es = [jnp.exp(l - mx) for l in ls]
        tot = es[0] + es[1] + es[2]
        lt_ref[...] = mx + jnp.log(tot)
        inv = 1.0 / tot
        z = z_ref[...]
        sz = z * _sigmoid(z)
        oa = (es[0] * inv) * o0[...] + (es[1] * inv) * o1[...] + (es[2] * inv) * o2[...]
        oa_ref[...] = oa
        u_ref[:, :A_WIDTH] = (oa * sz[:, :A_WIDTH]).astype(bf16)
        t = cg_ref[...] * hb_ref[...]
        tbuf[0:8, :] = jnp.where(i > 0, cgh_ref[...] * hbh_ref[...], 0.0)
        tbuf[8:, :] = t
        cv = w_ref[2:3, :] * t + w_ref[1:2, :] * tbuf[pl.ds(7, tm), :] + w_ref[0:1, :] * tbuf[pl.ds(6, tm), :]
        u_ref[:, A_WIDTH:] = (bg_ref[...] * cv * sz[:, A_WIDTH:]).astype(bf16)

    row = lambda w, c: pl.BlockSpec((tm, w), lambda i: (i, c))
    return pl.pallas_call(
        body, name="mix0_fwd", grid=(S // tm,),
        in_specs=[row(512, 0)] * 6
        + [row(512, E_BG // 512), row(512, E_CG // 512), row(512, E_HB // 512), row(1024, E_Z // 1024),
           _prev_halo(tm, 8, E_CG // 512), _prev_halo(tm, 8, E_HB // 512), pl.BlockSpec((SC_WIDTH, 512), lambda i: (0, 0))],
        out_specs=[row(1024, 0), row(512, 0), row(512, 0)],
        out_shape=[jax.ShapeDtypeStruct((S, D_MODEL), bf16), jax.ShapeDtypeStruct((S, A_WIDTH), f32),
                   jax.ShapeDtypeStruct((S, A_WIDTH), f32)],
        scratch_shapes=[pltpu.VMEM((tm + 8, 512), f32)],
        compiler_params=_cp(("parallel",)),
    )(*o_g, *lse_g, proj, proj, proj, proj, proj, proj, conv_w)


def _dsilu(z, sg):
    return sg * (1.0 + z * (1.0 - sg))


def _d_gate_in(dy_ref, wo_ref):
    return lax.dot_general(dy_ref[...], wo_ref[...], (((1,), (1,)), ((), ())), preferred_element_type=f32)


def _mix0_bwd_a(dy, w_out, proj, o_a, conv_w):
    S = proj.shape[0]
    tm = _tile(S, 256)

    def body(dy_ref, wo_ref, bg_ref, cg_ref, hb_ref, z_ref, cgh_ref, hbh_ref, oa_ref, w_ref,
             dz_ref, doa_ref, ds_ref, dbg_ref, dcv_ref, tbuf):
        i = pl.program_id(0)
        lo = lax.broadcasted_iota(jnp.int32, (tm, LANES), 1) < HEAD_DIM
        z = z_ref[...]
        sg = _sigmoid(z)
        sz = z * sg
        dsz = _dsilu(z, sg)
        du_v = _d_gate_in(dy_ref, wo_ref)
        t = cg_ref[...] * hb_ref[...]
        tbuf[0:8, :] = jnp.where(i > 0, cgh_ref[...] * hbh_ref[...], 0.0)
        tbuf[8:, :] = t
        cv = w_ref[2:3, :] * t + w_ref[1:2, :] * tbuf[pl.ds(7, tm), :] + w_ref[0:1, :] * tbuf[pl.ds(6, tm), :]
        bg = bg_ref[...]
        oa = oa_ref[...]
        dz_ref[:, :A_WIDTH] = (du_v[:, :A_WIDTH] * oa * dsz[:, :A_WIDTH]).astype(bf16)
        dz_ref[:, A_WIDTH:] = (du_v[:, A_WIDTH:] * (bg * cv) * dsz[:, A_WIDTH:]).astype(bf16)
        doa = du_v[:, :A_WIDTH] * sz[:, :A_WIDTH]
        dyb = du_v[:, A_WIDTH:] * sz[:, A_WIDTH:]
        doa_ref[...] = doa
        dbg_ref[...] = (dyb * cv).astype(bf16)
        dcv_ref[...] = dyb * bg
        prod = doa * oa
        for p in range(4):
            pp = prod[:, p * LANES:(p + 1) * LANES]
            sa = jnp.sum(jnp.where(lo, pp, 0.0), axis=-1, keepdims=True)
            sb = jnp.sum(jnp.where(lo, 0.0, pp), axis=-1, keepdims=True)
            ds_ref[:, p * LANES:(p + 1) * LANES] = jnp.where(lo, sa, sb)

    row = lambda w, c: pl.BlockSpec((tm, w), lambda i: (i, c))
    return pl.pallas_call(
        body, name="mix0_bwd_a", grid=(S // tm,),
        in_specs=[row(1024, 0), pl.BlockSpec((D_MODEL, D_MODEL), lambda i: (0, 0)),
                  row(512, E_BG // 512), row(512, E_CG // 512), row(512, E_HB // 512), row(1024, E_Z // 1024),
                  _prev_halo(tm, 8, E_CG // 512), _prev_halo(tm, 8, E_HB // 512), row(512, 0),
                  pl.BlockSpec((SC_WIDTH, 512), lambda i: (0, 0))],
        out_specs=[row(1024, 0), row(512, 0), row(512, 0), row(512, 0), row(512, 0)],
        out_shape=[jax.ShapeDtypeStruct((S, D_MODEL), bf16), jax.ShapeDtypeStruct((S, A_WIDTH), f32),
                   jax.ShapeDtypeStruct((S, A_WIDTH), f32), jax.ShapeDtypeStruct((S, 512), bf16),
                   jax.ShapeDtypeStruct((S, 512), f32)],
        scratch_shapes=[pltpu.VMEM((tm + 8, 512), f32)],
        compiler_params=_cp(("parallel",)),
    )(dy, w_out, proj, proj, proj, proj, proj, proj, o_a, conv_w)


def _mix0_bwd_b(dcv, proj, conv_w):
    S = proj.shape[0]
    tm = _tile(S, 256)
    nt = S // tm

    def body(dcv_ref, dcvn_ref, cg_ref, hb_ref, cgh_ref, hbh_ref, w_ref, dcg_ref, dhb_ref, gw_ref, tbuf, dbuf):
        i = pl.program_id(0)
        cg = cg_ref[...]
        hb = hb_ref[...]
        t = cg * hb
        tbuf[0:8, :] = jnp.where(i > 0, cgh_ref[...] * hbh_ref[...], 0.0)
        tbuf[8:, :] = t
        dcv_v = dcv_ref[...]
        dbuf[0:tm, :] = dcv_v
        dbuf[tm:, :] = jnp.where(i < nt - 1, dcvn_ref[...], 0.0)
        dt = w_ref[2:3, :] * dcv_v + w_ref[1:2, :] * dbuf[pl.ds(1, tm), :] + w_ref[0:1, :] * dbuf[pl.ds(2, tm), :]
        dcg_ref[...] = (dt * hb).astype(bf16)
        dhb_ref[...] = (dt * cg).astype(bf16)
        g2 = jnp.sum(dcv_v * t, axis=0, keepdims=True)
        g1 = jnp.sum(dcv_v * tbuf[pl.ds(7, tm), :], axis=0, keepdims=True)
        g0 = jnp.sum(dcv_v * tbuf[pl.ds(6, tm), :], axis=0, keepdims=True)
        part = jnp.concatenate([g0, g1, g2, jnp.zeros((5, 512), f32)], axis=0)

        @pl.when(i == 0)
        def _():
            gw_ref[...] = part

        @pl.when(i > 0)
        def _():
            gw_ref[...] += part

    row = lambda w, c: pl.BlockSpec((tm, w), lambda i: (i, c))
    return pl.pallas_call(
        body, name="mix0_bwd_b", grid=(nt,),
        in_specs=[row(512, 0), _next_halo(tm, 8, 0, S), row(512, E_CG // 512), row(512, E_HB // 512),
                  _prev_halo(tm, 8, E_CG // 512), _prev_halo(tm, 8, E_HB // 512),
                  pl.BlockSpec((SC_WIDTH, 512), lambda i: (0, 0))],
        out_specs=[row(512, 0), row(512, 0), pl.BlockSpec((8, 512), lambda i: (0, 0))],
        out_shape=[jax.ShapeDtypeStruct((S, 512), bf16), jax.ShapeDtypeStruct((S, 512), bf16),
                   jax.ShapeDtypeStruct((8, 512), f32)],
        scratch_shapes=[pltpu.VMEM((tm + 8, 512), f32), pltpu.VMEM((tm + 8, 512), f32)],
        compiler_params=_cp(("arbitrary",)),
    )(dcv, dcv, proj, proj, proj, proj, conv_w)


def _qk_bwd(dq_g, dk_g, dv_g, proj, tabs, nw, hm, dbg, dcg, dhb, dz):
    S = proj.shape[0]
    tm = _tile(S, 256)

    def body(*refs):
        d_refs = refs[0:6]
        dv_refs = refs[6:9]
        x_ref, c_ref, s1_ref, s2_ref, nw_ref, m_ref, dbg_ref, dcg_ref, dhb_ref, dz_ref, o_ref, gw_ref = refs[9:]
        i = pl.program_id(0)
        c, s1, s2, m = c_ref[...], s1_ref[...], s2_ref[...], m_ref[...]
        accs = []
        for kind in range(2):
            w = nw_ref[kind:kind + 1, :]
            acc = jnp.zeros((1, LANES), f32)
            for gi in range(N_GROUPS):
                for p in range(4):
                    col = kind * 1536 + gi * 512 + p * LANES
                    dout = d_refs[kind * 3 + gi][:, p * LANES:(p + 1) * LANES]
                    t = x_ref[:, col:col + LANES]
                    dthat = (dout * c + pltpu.roll(dout * s1, LANES - ROT_HALF, axis=1)
                             + pltpu.roll(dout * s2, ROT_HALF, axis=1))
                    r = lax.rsqrt(_head_mean(t * t, m) + EPS)
                    tn = t * r
                    acc = acc + jnp.sum(dthat * tn, axis=0, keepdims=True)
                    dtn = dthat * w
                    o_ref[:, col:col + LANES] = (r * (dtn - tn * _head_mean(dtn * tn, m))).astype(bf16)
            accs.append(acc + pltpu.roll(acc, HEAD_DIM, axis=1))
        for gi in range(N_GROUPS):
            o_ref[:, E_V + gi * 512:E_V + (gi + 1) * 512] = dv_refs[gi][...].astype(bf16)
        o_ref[:, E_BG:E_CG] = dbg_ref[...]
        o_ref[:, E_CG:E_HB] = dcg_ref[...]
        o_ref[:, E_HB:E_Z] = dhb_ref[...]
        o_ref[:, E_Z:] = dz_ref[...]
        part = jnp.concatenate([accs[0], accs[1], jnp.zeros((6, LANES), f32)], axis=0)

        @pl.when(i == 0)
        def _():
            gw_ref[...] = part

        @pl.when(i > 0)
        def _():
            gw_ref[...] += part

    row = lambda w, c: pl.BlockSpec((tm, w), lambda i: (i, c))
    tab = row(LANES, 0)
    return pl.pallas_call(
        body, name="qk_bwd", grid=(S // tm,),
        in_specs=[row(512, 0)] * 9 + [row(3072, 0), tab, tab, tab, pl.BlockSpec((2, LANES), lambda i: (0, 0)),
                                      pl.BlockSpec((LANES, LANES), lambda i: (0, 0)),
                                      row(512, 0), row(512, 0), row(512, 0), row(1024, 0)],
        out_specs=[row(EVEN_IN, 0), pl.BlockSpec((8, LANES), lambda i: (0, 0))],
        out_shape=[jax.ShapeDtypeStruct((S, EVEN_IN), bf16), jax.ShapeDtypeStruct((8, LANES), f32)],
        compiler_params=_cp(("arbitrary",)),
    )(*dq_g, *dk_g, *dv_g, proj, *tabs, nw, hm, dbg, dcg, dhb, dz)


def _inv_count(i, tm, p):
    rowg = lax.broadcasted_iota(jnp.int32, (tm, 1), 0) + i * tm
    return 1.0 / jnp.minimum(rowg + 1, p).astype(f32)


def _layer_norm_stats(c):
    mu = jnp.mean(c, axis=-1, keepdims=True)
    cen = c - mu
    rstd = lax.rsqrt(jnp.mean(cen * cen, axis=-1, keepdims=True) + EPS)
    return cen * rstd, rstd


def _fill_pool_buf(i, ubuf, uc_ref, uch_ref):
    ubuf[0:16, :] = jnp.where(i > 0, uch_ref[...], 0.0)
    ubuf[16:, :] = uc_ref[...]


def _pooled(i, tm, ubuf, gi):
    p = POOL_SIZES[gi]
    cols = slice(gi * LANES, (gi + 1) * LANES)
    acc = ubuf[pl.ds(16, tm), cols]
    cur = acc
    for jj in range(1, p):
        acc = acc + ubuf[pl.ds(16 - jj, tm), cols]
    return acc * _inv_count(i, tm, p) - cur


def _fill_glu_buf(i, gbuf, da_ref, dg_ref, dah_ref, dgh_ref):
    gbuf[0:32, :] = jnp.where(i > 0, dah_ref[...] * _sigmoid(dgh_ref[...]), 0.0)
    gbuf[32:, :] = da_ref[...] * _sigmoid(dg_ref[...])


def _shift_copies(buf, sh, tm):
    for b in range(1, 8):
        sh[b - 1] = buf[pl.ds(b, tm + 24), :]


CONV_ROWS = 32


def _window(buf, sh, base, off, rows):
    b = off % 8
    if b == 0:
        return buf[pl.ds(base + off, rows), :]
    return sh[b - 1, pl.ds(base + (off - b), rows), :]


def _mix1_fwd(proj, pool_w, pool_scale, dconv_w, dconv_b, ln_w, ln_b):
    S = proj.shape[0]
    tm = _tile(S, 256)

    def body(uc_ref, uch_ref, da_ref, dg_ref, dah_ref, dgh_ref, za_ref, zb_ref, pw_ref, ps_ref, cw_ref, cb_ref,
             lw_ref, lb_ref, u_ref, c_ref, mc_ref, ubuf, gbuf, gsh):
        i = pl.program_id(0)
        _fill_pool_buf(i, ubuf, uc_ref, uch_ref)
        za = za_ref[...]
        for gi in range(4):
            cols = slice(gi * LANES, (gi + 1) * LANES)
            mc = jnp.dot(_pooled(i, tm, ubuf, gi).astype(bf16), pw_ref[gi], preferred_element_type=f32)
            mc_ref[:, cols] = mc
            zg = za[:, cols]
            u_ref[:, cols] = (mc * ps_ref[:, cols] * (zg * _sigmoid(zg))).astype(bf16)
        _fill_glu_buf(i, gbuf, da_ref, dg_ref, dah_ref, dgh_ref)
        _shift_copies(gbuf, gsh, tm)
        c = jnp.zeros((tm, 512), f32) + cb_ref[...]
        for k in range(D_CONV):
            c = c + cw_ref[k:k + 1, :] * _window(gbuf, gsh, 0, 32 - (D_CONV - 1) + k, tm)
        c_ref[...] = c
        yhat, _ = _layer_norm_stats(c)
        l = yhat * lw_ref[...] + lb_ref[...]
        zb = zb_ref[...]
        u_ref[:, 512:] = (l * _sigmoid(l) * (zb * _sigmoid(zb))).astype(bf16)

    row = lambda w, c: pl.BlockSpec((tm, w), lambda i: (i, c))
    vec = pl.BlockSpec((1, 512), lambda i: (0, 0))
    return pl.pallas_call(
        body, name="mix1_fwd", grid=(S // tm,),
        in_specs=[row(512, 0), _prev_halo(tm, 16, 0), row(512, 1), row(512, 2), _prev_halo(tm, 32, 1), _prev_halo(tm, 32, 2),
                  row(512, 3), row(512, 4), pl.BlockSpec((4, LANES, LANES), lambda i: (0, 0, 0)), vec,
                  pl.BlockSpec((D_CONV, 512), lambda i: (0, 0)), vec, vec, vec],
        out_specs=[row(1024, 0), row(512, 0), row(512, 0)],
        out_shape=[jax.ShapeDtypeStruct((S, D_MODEL), bf16), jax.ShapeDtypeStruct((S, 512), f32),
                   jax.ShapeDtypeStruct((S, 512), f32)],
        scratch_shapes=[pltpu.VMEM((tm + 16, 512), f32), pltpu.VMEM((tm + 32, 512), f32),
                        pltpu.VMEM((7, tm + 24, 512), f32)],
        compiler_params=_cp(("parallel",)),
    )(proj, proj, proj, proj, proj, proj, proj, proj, pool_w, pool_scale, dconv_w, dconv_b, ln_w, ln_b)


def _mix1_bwd_a(dy, w_out, proj, c, mc, pool_w, pool_scale, ln_w, ln_b):
    S = proj.shape[0]
    tm = _tile(S, 256)

    def body(dy_ref, wo_ref, za_ref, zb_ref, c_ref, mc_ref, pw_ref, ps_ref, lw_ref, lb_ref,
             dz_ref, dc_ref, dpl_ref, dmc_ref, acc_ref):
        i = pl.program_id(0)
        du_v = _d_gate_in(dy_ref, wo_ref)
        ps = ps_ref[...]
        za = za_ref[...]
        sga = _sigmoid(za)
        mcv = mc_ref[...]
        dz_ref[:, :512] = (du_v[:, :512] * (mcv * ps) * _dsilu(za, sga)).astype(bf16)
        dyc = du_v[:, :512] * (za * sga)
        g_ps = jnp.sum(dyc * mcv, axis=0, keepdims=True)
        dmc = (dyc * ps).astype(bf16)
        dmc_ref[...] = dmc
        for gi in range(4):
            cols = slice(gi * LANES, (gi + 1) * LANES)
            dpl_ref[:, cols] = lax.dot_general(dmc[:, cols], pw_ref[gi], (((1,), (1,)), ((), ())), preferred_element_type=f32)
        yhat, rstd = _layer_norm_stats(c_ref[...])
        lw = lw_ref[...]
        l = yhat * lw + lb_ref[...]
        sgl = _sigmoid(l)
        zb = zb_ref[...]
        sgb = _sigmoid(zb)
        dz_ref[:, 512:] = (du_v[:, 512:] * (l * sgl) * _dsilu(zb, sgb)).astype(bf16)
        dl = du_v[:, 512:] * (zb * sgb) * _dsilu(l, sgl)
        g_lb = jnp.sum(dl, axis=0, keepdims=True)
        g_lw = jnp.sum(dl * yhat, axis=0, keepdims=True)
        dyh = dl * lw
        dc = rstd * (dyh - jnp.mean(dyh, axis=-1, keepdims=True) - yhat * jnp.mean(dyh * yhat, axis=-1, keepdims=True))
        dc_ref[...] = dc
        g_db = jnp.sum(dc, axis=0, keepdims=True)
        part = jnp.concatenate([g_ps, g_lw, g_lb, g_db, jnp.zeros((4, 512), f32)], axis=0)

        @pl.when(i == 0)
        def _():
            acc_ref[...] = part

        @pl.when(i > 0)
        def _():
            acc_ref[...] += part

    row = lambda w, c_: pl.BlockSpec((tm, w), lambda i: (i, c_))
    vec = pl.BlockSpec((1, 512), lambda i: (0, 0))
    return pl.pallas_call(
        body, name="mix1_bwd_a", grid=(S // tm,),
        in_specs=[row(1024, 0), pl.BlockSpec((D_MODEL, D_MODEL), lambda i: (0, 0)),
                  row(512, 3), row(512, 4), row(512, 0), row(512, 0),
                  pl.BlockSpec((4, LANES, LANES), lambda i: (0, 0, 0)), vec, vec, vec],
        out_specs=[row(1024, 0), row(512, 0), row(512, 0), row(512, 0), pl.BlockSpec((8, 512), lambda i: (0, 0))],
        out_shape=[jax.ShapeDtypeStruct((S, D_MODEL), bf16), jax.ShapeDtypeStruct((S, 512), f32),
                   jax.ShapeDtypeStruct((S, 512), f32), jax.ShapeDtypeStruct((S, 512), bf16),
                   jax.ShapeDtypeStruct((8, 512), f32)],
        compiler_params=_cp(("arbitrary",)),
    )(dy, w_out, proj, proj, c, mc, pool_w, pool_scale, ln_w, ln_b)


def _mix1_bwd_b(dc, dpl, dmc, dz, proj, dconv_w):
    S = proj.shape[0]
    tm = _tile(S, 256)
    nt = S // tm

    def body(dc_ref, dcn_ref, dpl_ref, dpn_ref, dmc_ref, dz_ref, uc_ref, uch_ref, da_ref, dg_ref,
             cw_ref, o_ref, gcw_ref, gpw_ref, ubuf, dcbuf, dpbuf, dcsh, gacc):
        i = pl.program_id(0)
        last = i == nt - 1
        _fill_pool_buf(i, ubuf, uc_ref, uch_ref)
        dcbuf[0:tm, :] = dc_ref[...]
        dcbuf[tm:, :] = jnp.where(last, 0.0, dcn_ref[...])
        _shift_copies(dcbuf, dcsh, tm)
        dpl_v = dpl_ref[...]
        for gi in range(4):
            p = POOL_SIZES[gi]
            cols = slice(gi * LANES, (gi + 1) * LANES)
            dpbuf[0:tm, cols] = dpl_v[:, cols] * _inv_count(i, tm, p)
            dpbuf[tm:, cols] = jnp.where(last, 0.0, dpn_ref[:, cols] * (1.0 / p))
        gpw = []
        for gi in range(4):
            p = POOL_SIZES[gi]
            cols = slice(gi * LANES, (gi + 1) * LANES)
            acc = -dpl_v[:, cols]
            for jj in range(p):
                acc = acc + dpbuf[pl.ds(jj, tm), cols]
            o_ref[:, cols] = acc.astype(bf16)
            pooled = _pooled(i, tm, ubuf, gi).astype(bf16)
            gpw.append(lax.dot_general(pooled, dmc_ref[:, cols], (((0,), (0,)), ((), ())), preferred_element_type=f32))
        gacc[...] = jnp.zeros_like(gacc)

        def conv_rows(ci, carry):
            base = pl.multiple_of(ci * CONV_ROWS, CONV_ROWS)
            da = da_ref[pl.ds(base, CONV_ROWS), :]
            sg = _sigmoid(dg_ref[pl.ds(base, CONV_ROWS), :])
            gl = da * sg
            dgl = jnp.zeros((CONV_ROWS, 512), f32)
            for k in range(D_CONV):
                win = _window(dcbuf, dcsh, base, D_CONV - 1 - k, CONV_ROWS)
                dgl = dgl + cw_ref[k:k + 1, :] * win
                gacc[k] += jnp.sum((gl * win).reshape(CONV_ROWS // 8, 8, 512), axis=0)
            o_ref[pl.ds(base, CONV_ROWS), O_DA:O_DG] = (dgl * sg).astype(bf16)
            o_ref[pl.ds(base, CONV_ROWS), O_DG:O_Z] = (dgl * da * sg * (1.0 - sg)).astype(bf16)
            return carry

        lax.fori_loop(0, tm // CONV_ROWS, conv_rows, 0)
        o_ref[:, O_Z:] = dz_ref[...]
        gcw_part = jnp.concatenate(
            [jnp.sum(gacc[k], axis=0, keepdims=True) for k in range(D_CONV)] + [jnp.zeros((1, 512), f32)], axis=0)

        @pl.when(i == 0)
        def _():
            gcw_ref[...] = gcw_part
            for gi in range(4):
                gpw_ref[gi] = gpw[gi]

        @pl.when(i > 0)
        def _():
            gcw_ref[...] += gcw_part
            for gi in range(4):
                gpw_ref[gi] += gpw[gi]

    row = lambda w, c_: pl.BlockSpec((tm, w), lambda i: (i, c_))
    return pl.pallas_call(
        body, name="mix1_bwd_b", grid=(nt,),
        in_specs=[row(512, 0), _next_halo(tm, 32, 0, S), row(512, 0), _next_halo(tm, 16, 0, S), row(512, 0), row(1024, 0),
                  row(512, 0), _prev_halo(tm, 16, 0), row(512, 1), row(512, 2),
                  pl.BlockSpec((D_CONV, 512), lambda i: (0, 0))],
        out_specs=[row(ODD_IN, 0), pl.BlockSpec((32, 512), lambda i: (0, 0)),
                   pl.BlockSpec((4, LANES, LANES), lambda i: (0, 0, 0))],
        out_shape=[jax.ShapeDtypeStruct((S, ODD_IN), bf16), jax.ShapeDtypeStruct((32, 512), f32),
                   jax.ShapeDtypeStruct((4, LANES, LANES), f32)],
        scratch_shapes=[pltpu.VMEM((tm + 16, 512), f32), pltpu.VMEM((tm + 32, 512), f32),
                        pltpu.VMEM((tm + 16, 512), f32), pltpu.VMEM((7, tm + 24, 512), f32),
                        pltpu.VMEM((D_CONV, 8, 512), f32)],
        compiler_params=_cp(("arbitrary",)),
    )(dc, dc, dpl, dpl, dmc, dz, proj, proj, proj, proj, dconv_w)


_SMALL_LATE = ["e_q_norm_w", "e_k_norm_w", "e_conv_w", "o_norm_w", "o_pool_w", "o_pool_scale", "o_dconv_w", "o_dconv_b",
               "o_ln_w", "o_ln_b"]


def _local_step(x, pos_col, target, w, dist=None):
    hm = _head_mean_matrix()
    nw = jnp.concatenate([jnp.tile(w["e_q_norm_w"], (1, 2)), jnp.tile(w["e_k_norm_w"], (1, 2))], axis=0)
    tabs = _rope_tables(pos_col)
    pool_wb = w["o_pool_w"].astype(bf16)
    e_norm_w, e_w_in = w["e_norm_w"], w["e_w_in"]

    if dist is None:
        proj0, qk, h0 = _in_proj0(x, e_norm_w, e_w_in, tabs, nw, hm)
    else:
        proj0, qk, h0, gathered = _in_proj0(x, e_norm_w, e_w_in, tabs, nw, hm, fuse=([], dist[0]))
        w = {**w, **dist[1](gathered)}
    e_conv_w, e_w_out, o_norm_w, o_w_in, o_w_out = w["e_conv_w"], w["e_w_out"], w["o_norm_w"], w["o_w_in"], w["o_w_out"]
    o_pool_scale, o_dconv_w, o_dconv_b, o_ln_w, o_ln_b = (w[k] for k in ("o_pool_scale", "o_dconv_w", "o_dconv_b", "o_ln_w", "o_ln_b"))
    o_g, lse_g = [], []
    for g in range(N_GROUPS):
        o, l = _attn_fwd_local(qk, proj0) if g == 0 else _attn_fwd_dil(qk, proj0, g, name=f"attn_fwd{g}")
        o_g.append(o)
        lse_g.append(l)
    u0, o_a, lt = _mix0_fwd(o_g, lse_g, proj0, e_conv_w)
    x1, h1 = _out_proj_rms(u0, e_w_out, x, o_norm_w, name="out_proj0")
    o_w_in3 = o_w_in.reshape(1, D_MODEL, ODD_IN)
    proj1 = _mm_nn_resident(h1, o_w_in3, name="in_proj1", tm=512)
    u1, c1, mc1 = _mix1_fwd(proj1, pool_wb, o_pool_scale, o_dconv_w, o_dconv_b, o_ln_w, o_ln_b)
    dy, dyb, loss = _mm_out_loss(u1, o_w_out, x1, target, name="out_proj1_loss")
    g_o_w_out = _mm_tn(u1, dyb, name="g_w_out1", out_dtype=bf16)
    dz1, dc1, dpl1, dmc1, sums1 = _mix1_bwd_a(dyb, o_w_out, proj1, c1, mc1, pool_wb, o_pool_scale, o_ln_w, o_ln_b)
    dproj1, g_dconv_w, g_pool_w = _mix1_bwd_b(dc1, dpl1, dmc1, dz1, proj1, o_dconv_w)
    g_o_w_in = _mm_tn(h1, dproj1, name="g_w_in1", out_dtype=bf16)
    d1, d1b, g_o_norm = _mm_nt_rms_bwd(dproj1, o_w_in3, x1, o_norm_w, dy, name="d_h1")
    g_e_w_out = _mm_tn(u0, d1b, name="g_w_out0", out_dtype=bf16)
    dz0, do_a, dsum, dbg, dcv = _mix0_bwd_a(d1b, e_w_out, proj0, o_a, e_conv_w)
    dcg, dhb, g_conv_w = _mix0_bwd_b(dcv, proj0, e_conv_w)
    fuse_a = None if dist is None else (
        [g_e_w_out.reshape(N_DEV, D_MODEL // N_DEV, D_MODEL),
         jnp.moveaxis(g_o_w_in.reshape(D_MODEL, N_DEV, ODD_IN // N_DEV), 1, 0),
         g_o_w_out.reshape(N_DEV, D_MODEL // N_DEV, D_MODEL)], [])
    dq_g, dk_g, dv_g = [], [], []
    for g in range(N_GROUPS):
        if g == 0:
            dqkv = _attn_bwd_local(qk, proj0, do_a, lt, dsum, fuse=fuse_a)
            if dist is not None:
                dqkv, recv_a = dqkv
            dq, dk, dv = dqkv
        else:
            dq, dk, dv = _attn_bwd_dil(qk, proj0, do_a, lt, dsum, g, name=f"attn_bwd{g}")
        dq_g.append(dq)
        dk_g.append(dk)
        dv_g.append(dv)
    dproj0, g_qk_norm = _qk_bwd(dq_g, dk_g, dv_g, proj0, tabs, nw, hm, dbg, dcg, dhb, dz0)
    half = D_MODEL // 2
    g_e_w_in_a = _mm_tn(h0, dproj0, name="g_w_in0a", out_dtype=bf16, chunks=N_DEV, a_cols=(0, half))
    if dist is None:
        g_e_w_in_b = _mm_tn(h0, dproj0, name="g_w_in0b", out_dtype=bf16, chunks=N_DEV, a_cols=(1, half))
    else:
        g_e_w_in_b, recv_b0 = _mm_tn(h0, dproj0, name="g_w_in0b", out_dtype=bf16, chunks=N_DEV, a_cols=(1, half),
                                     fuse=([g_e_w_in_a], []))
    grads = dict(
        e_q_norm_w=g_qk_norm[0:1, :HEAD_DIM], e_k_norm_w=g_qk_norm[1:2, :HEAD_DIM],
        e_conv_w=g_conv_w[:SC_WIDTH], e_w_out=g_e_w_out,
        o_norm_w=g_o_norm, o_w_in=g_o_w_in, o_pool_w=g_pool_w,
        o_pool_scale=sums1[0:1], o_dconv_w=g_dconv_w[:D_CONV], o_dconv_b=sums1[3:4],
        o_ln_w=sums1[1:2], o_ln_b=sums1[2:3], o_w_out=g_o_w_out)
    if dist is None:
        dh0 = _mm_nt_resident(dproj0, e_w_in, name="d_h0")
        grad_x, _, grads["e_norm_w"] = _rms_bwd(x, e_norm_w, dh0, d1, name="rms0_bwd")
        grads["e_w_in"] = jnp.concatenate([g_e_w_in_a, g_e_w_in_b], axis=1)
        return loss, grad_x, grads
    small_late, offs = _pack_rows([grads[n_] for n_ in _SMALL_LATE])
    dh0, recv_b = _mm_nt_resident(dproj0, e_w_in, name="d_h0", fuse=([g_e_w_in_b], [small_late]))
    grad_x, _, g_e_norm = _rms_bwd(x, e_norm_w, dh0, d1, name="rms0_bwd")
    recv_c = _exchange([], [g_e_norm.reshape(8, LANES)], name="exchange_e_norm")
    recv = dict(e_w_out=[recv_a[0]], o_w_in=[recv_a[1]], o_w_out=[recv_a[2]], e_w_in=[recv_b0[0], recv_b[0]],
                small_late=recv_b[1], e_norm_w=recv_c[0])
    return loss, grad_x, recv, {n_: (off, grads[n_].shape) for n_, off in zip(_SMALL_LATE, offs)}


_MESH_ID = pl.DeviceIdType.MESH
_HBM = pl.BlockSpec(memory_space=pl.ANY)


def _all_gather(arrs, *, name):
    n = len(arrs)

    def body(*refs):
        ins, outs = refs[:n], refs[n:2 * n]
        send_sems, recv_sems, local_sems = refs[2 * n:]
        x, y, c = _place()
        me, sibling = (x, y, c), (x, y, 1 - c)
        chips = [(1 - x, y), (x, 1 - y), (1 - x, 1 - y)]

        def slot(t, px, py, pc):
            return outs[t].at[4 * px + 2 * py + pc]

        def copy(t, k, block, to, src=None):
            dst = slot(t, *block)
            return pltpu.make_async_remote_copy(
                src_ref=dst if src is None else src, dst_ref=dst,
                send_sem=send_sems.at[7 * t + k], recv_sem=recv_sems.at[7 * t + k],
                device_id=to, device_id_type=_MESH_ID)

        mine = [pltpu.make_async_copy(ins[t], slot(t, *me), local_sems.at[t]) for t in range(n)]
        for cp in mine:
            cp.start()
        first = []
        for t in range(n):
            first.append(copy(t, 0, me, sibling, src=ins[t]))
            first += [copy(t, 1 + j, me, (*chip, c), src=ins[t]) for j, chip in enumerate(chips)]
        for cp in first:
            cp.start()
        passed = []
        for j, chip in enumerate(chips):
            for t in range(n):
                copy(t, 1 + j, (*chip, c), me).wait_recv()
                fwd = copy(t, 4 + j, (*chip, c), sibling)
                fwd.start()
                passed.append(fwd)
        for t in range(n):
            copy(t, 0, sibling, me).wait_recv()
            for j, chip in enumerate(chips):
                copy(t, 4 + j, (*chip, 1 - c), me).wait_recv()
        for cp in first + passed:
            cp.wait_send()
        for cp in mine:
            cp.wait()

    return pl.pallas_call(
        body, name=name,
        in_specs=[_HBM] * n, out_specs=[_HBM] * n,
        out_shape=[jax.ShapeDtypeStruct((N_DEV, *a.shape), a.dtype) for a in arrs],
        scratch_shapes=[pltpu.SemaphoreType.DMA((7 * n,)), pltpu.SemaphoreType.DMA((7 * n,)),
                        pltpu.SemaphoreType.DMA((n,))],
    )(*arrs)


def _exchange(chunked, whole, *, name):
    arrs = list(chunked) + list(whole)
    n = len(arrs)

    def body(*refs):
        start, wait = _exchange_plan(refs[:n], refs[n:2 * n], *refs[2 * n:], len(chunked))
        start()
        wait()

    return pl.pallas_call(
        body, name=name, in_specs=[_HBM] * n, out_specs=[_HBM] * n,
        out_shape=_exchange_out_shapes(chunked, whole), scratch_shapes=_exchange_sems(n),
    )(*arrs)


def _adamw(w, g, m, v):
    m2 = ADAM_B1 * m + (1.0 - ADAM_B1) * g
    v2 = ADAM_B2 * v + (1.0 - ADAM_B2) * (g * g)
    m_hat = m2 / (1.0 - ADAM_B1 ** ADAM_STEP)
    v_hat = v2 / (1.0 - ADAM_B2 ** ADAM_STEP)
    delta = -ADAM_LR * (m_hat / (jnp.sqrt(v_hat) + ADAM_EPS) + ADAM_WD * w)
    return delta, m2, v2


def _sum_adamw(parts, w, m, v, *, name):
    R, C = w.shape
    nsplit = len(parts)
    rp = R // nsplit
    tr = _tile(rp, 256)
    npt = rp // tr

    def body(*refs):
        p_refs = refs[:nsplit]
        w_ref, m_ref, v_ref, g_ref, d_ref, nm_ref, nv_ref = refs[nsplit:]
        h = pl.program_id(0)
        g = None
        for i in range(N_DEV):
            pi = p_refs[0][i]
            for q in range(1, nsplit):
                pi = jnp.where(h == q, p_refs[q][i], pi)
            g = pi.astype(f32) if g is None else g + pi.astype(f32)
        g_ref[...] = g
        d_ref[...], nm_ref[...], nv_ref[...] = _adamw(w_ref[...], g, m_ref[...], v_ref[...])

    def part_spec(q):
        return pl.BlockSpec((N_DEV, tr, C), lambda h, i: (0, jnp.where(h == q, i, 0), 0))

    spec = pl.BlockSpec((tr, C), lambda h, i: (h * npt + i, 0))
    return pl.pallas_call(
        body, name=name, grid=(nsplit, npt),
        in_specs=[part_spec(q) for q in range(nsplit)] + [spec, spec, spec],
        out_specs=[spec] * 4, out_shape=[jax.ShapeDtypeStruct((R, C), f32)] * 4,
        compiler_params=_cp(("parallel", "parallel")),
    )(*parts, w, m, v)


def _sum_parts(parts, *, name):
    _, R, C = parts.shape

    def body(p_ref, o_ref):
        g = p_ref[0]
        for i in range(1, N_DEV):
            g = g + p_ref[i]
        o_ref[...] = g

    return pl.pallas_call(body, name=name, out_shape=jax.ShapeDtypeStruct((R, C), f32),
                          compiler_params=pltpu.CompilerParams(vmem_limit_bytes=VMEM_LIMIT))(parts)


def _adamw_small(ws, gs, ms, vs):
    n = len(ws)

    def body(*refs):
        w_r, g_r, m_r, v_r = refs[:n], refs[n:2 * n], refs[2 * n:3 * n], refs[3 * n:4 * n]
        d_r, nm_r, nv_r = refs[4 * n:5 * n], refs[5 * n:6 * n], refs[6 * n:7 * n]
        for t in range(n):
            d_r[t][...], nm_r[t][...], nv_r[t][...] = _adamw(w_r[t][...], g_r[t][...], m_r[t][...], v_r[t][...])

    shapes = [jax.ShapeDtypeStruct(w.shape, f32) for w in ws]
    outs = pl.pallas_call(body, name="adamw_small", out_shape=shapes * 3)(*ws, *gs, *ms, *vs)
    return outs[:n], outs[n:2 * n], outs[2 * n:]


_WEIGHTS = ["e_norm_w", "e_w_in", "e_q_norm_w", "e_k_norm_w", "e_conv_w", "e_w_out", "o_norm_w", "o_w_in", "o_pool_w",
            "o_pool_scale", "o_dconv_w", "o_dconv_b", "o_ln_w", "o_ln_b", "o_w_out"]
_BIG = ["e_w_in", "e_w_out", "o_w_in", "o_w_out"]
_SMALL_SHARDED = ["e_conv_w", "o_norm_w", "o_pool_scale", "o_dconv_w", "o_dconv_b", "o_ln_w", "o_ln_b"]
_SMALL_ALL = ["e_norm_w", "e_q_norm_w", "e_k_norm_w", "e_conv_w", "o_norm_w", "o_pool_w", "o_pool_scale", "o_dconv_w",
              "o_dconv_b", "o_ln_w", "o_ln_b"]


def _pack_rows(pieces):
    rows, offs, r0 = [], [], 0
    for p in pieces:
        flat = p.reshape(-1)
        nr = -(-flat.shape[0] // (8 * LANES)) * 8
        rows.append(jnp.pad(flat, (0, nr * LANES - flat.shape[0])).reshape(nr, LANES))
        offs.append((r0, nr))
        r0 += nr
    return jnp.concatenate(rows, axis=0), offs


def _unpack_rows(buf, off, shape):
    r0, nr = off
    size = int(np.prod(shape))
    return buf[..., r0:r0 + nr, :].reshape(*buf.shape[:-2], nr * LANES)[..., :size].reshape(*buf.shape[:-2], *shape)


def kernel(x, positions, e_norm_w, e_w_in, e_q_norm_w, e_k_norm_w, e_conv_w, e_w_out, o_norm_w, o_w_in, o_pool_w, o_pool_scale, o_dconv_w, o_dconv_b, o_ln_w, o_ln_b, o_w_out, loss_target, m_e_norm_w, m_e_w_in, m_e_q_norm_w, m_e_k_norm_w, m_e_conv_w, m_e_w_out, m_o_norm_w, m_o_w_in, m_o_pool_w, m_o_pool_scale, m_o_dconv_w, m_o_dconv_b, m_o_ln_w, m_o_ln_b, m_o_w_out, v_e_norm_w, v_e_w_in, v_e_q_norm_w, v_e_k_norm_w, v_e_conv_w, v_e_w_out, v_o_norm_w, v_o_w_in, v_o_pool_w, v_o_pool_scale, v_o_dconv_w, v_o_dconv_b, v_o_ln_w, v_o_ln_b, v_o_w_out):
    w = dict(e_norm_w=e_norm_w, e_w_in=e_w_in, e_q_norm_w=e_q_norm_w, e_k_norm_w=e_k_norm_w, e_conv_w=e_conv_w,
             e_w_out=e_w_out, o_norm_w=o_norm_w, o_w_in=o_w_in, o_pool_w=o_pool_w, o_pool_scale=o_pool_scale,
             o_dconv_w=o_dconv_w, o_dconv_b=o_dconv_b, o_ln_w=o_ln_w, o_ln_b=o_ln_b, o_w_out=o_w_out)
    m = dict(e_norm_w=m_e_norm_w, e_w_in=m_e_w_in, e_q_norm_w=m_e_q_norm_w, e_k_norm_w=m_e_k_norm_w, e_conv_w=m_e_conv_w,
             e_w_out=m_e_w_out, o_norm_w=m_o_norm_w, o_w_in=m_o_w_in, o_pool_w=m_o_pool_w, o_pool_scale=m_o_pool_scale,
             o_dconv_w=m_o_dconv_w, o_dconv_b=m_o_dconv_b, o_ln_w=m_o_ln_w, o_ln_b=m_o_ln_b, o_w_out=m_o_w_out)
    v = dict(e_norm_w=v_e_norm_w, e_w_in=v_e_w_in, e_q_norm_w=v_e_q_norm_w, e_k_norm_w=v_e_k_norm_w, e_conv_w=v_e_conv_w,
             e_w_out=v_e_w_out, o_norm_w=v_o_norm_w, o_w_in=v_o_w_in, o_pool_w=v_o_pool_w, o_pool_scale=v_o_pool_scale,
             o_dconv_w=v_o_dconv_w, o_dconv_b=v_o_dconv_b, o_ln_w=v_o_ln_w, o_ln_b=v_o_ln_b, o_w_out=v_o_w_out)
    S = x.shape[1]
    me = 4 * lax.axis_index("x") + 2 * lax.axis_index("y") + lax.axis_index("c")

    small_local, small_offs = _pack_rows([w[n_] for n_ in _SMALL_SHARDED])
    g_e_in, = _all_gather([w["e_w_in"][0].astype(bf16)], name="gather_e_w_in")
    rest_local = [w["e_w_out"][0].astype(bf16), w["o_w_in"][0].astype(bf16), w["o_w_out"][0].astype(bf16), small_local]

    def unpack_rest(gathered):
        g_e_out, g_o_in, g_o_out, g_small = gathered
        full = {}
        for n_, off in zip(_SMALL_SHARDED, small_offs):
            shard = _unpack_rows(g_small, off, w[n_].shape[1:])
            full[n_] = jnp.moveaxis(shard, 0, -2).reshape(*shard.shape[1:-1], N_DEV * shard.shape[-1])
        return dict(
            e_conv_w=full["e_conv_w"], e_w_out=g_e_out.reshape(D_MODEL, D_MODEL), o_norm_w=full["o_norm_w"].reshape(1, D_MODEL),
            o_w_in=jnp.moveaxis(g_o_in, 0, 1).reshape(D_MODEL, ODD_IN), o_pool_scale=full["o_pool_scale"].reshape(1, 512),
            o_dconv_w=full["o_dconv_w"], o_dconv_b=full["o_dconv_b"].reshape(1, 512), o_ln_w=full["o_ln_w"].reshape(1, 512),
            o_ln_b=full["o_ln_b"].reshape(1, 512), o_w_out=g_o_out.reshape(D_MODEL, D_MODEL))

    loss_blk, grad_x, recv, small_where = _local_step(
        x[0], positions.reshape(S, 1), loss_target[0],
        dict(e_norm_w=w["e_norm_w"], e_w_in=g_e_in, e_q_norm_w=w["e_q_norm_w"], e_k_norm_w=w["e_k_norm_w"],
             o_pool_w=w["o_pool_w"][0]),
        dist=(rest_local, unpack_rest))
    loss = lax.psum(loss_blk[0, 0], ("x", "y", "c"))

    out_g, out_d, out_m, out_v = {}, {}, {}, {}
    for n_ in _BIG:
        res = _sum_adamw(recv[n_], w[n_][0], m[n_][0], v[n_][0], name="adamw_" + n_)
        out_g[n_], out_d[n_], out_m[n_], out_v[n_] = [r[None] for r in res]
    small_sum = _sum_parts(recv["small_late"], name="sum_small_grads")
    e_norm_sum = _sum_parts(recv["e_norm_w"], name="sum_e_norm_grad")
    gs = []
    for n_ in _SMALL_ALL:
        if n_ == "e_norm_w":
            gs.append(e_norm_sum.reshape(w[n_].shape))
            continue
        off, shape = small_where[n_]
        gfull = _unpack_rows(small_sum, off, shape)
        if n_ in _SMALL_SHARDED:
            width = w[n_].shape[-1]
            gfull = lax.dynamic_slice_in_dim(gfull, me * width, width, axis=gfull.ndim - 1)
        gs.append(gfull.reshape(w[n_].shape))
    ds, nms, nvs = _adamw_small([w[n_] for n_ in _SMALL_ALL], gs, [m[n_] for n_ in _SMALL_ALL], [v[n_] for n_ in _SMALL_ALL])
    for n_, g_, d_, nm_, nv_ in zip(_SMALL_ALL, gs, ds, nms, nvs):
        out_g[n_], out_d[n_], out_m[n_], out_v[n_] = g_, d_, nm_, nv_

    return (loss, grad_x[None], *[out_g[n_] for n_ in _WEIGHTS], *[out_d[n_] for n_ in _WEIGHTS],
            *[out_m[n_] for n_ in _WEIGHTS], *[out_v[n_] for n_ in _WEIGHTS])
```

```python
import functools

import numpy as np
import jax
import jax.numpy as jnp
from jax import lax
from jax.experimental import pallas as pl
from jax.experimental.pallas import tpu as pltpu

f32 = jnp.float32
bf16 = jnp.bfloat16

D_MODEL = 1024
HEAD_DIM = 64
N_GROUPS = 3
DILATIONS = (1, 4, 16)
QBLK = 128
A_WIDTH = 512
EVEN_IN = 7168
ODD_IN = 2560
POOL_SIZES = (2, 4, 8, 16)
D_CONV = 31
SC_WIDTH = 3
ROT_HALF = 8
ROPE_THETA = 500000.0
EPS = 1e-6
NEG = -1e30
SCALE = HEAD_DIM ** -0.5
N_DEV = 8
LANES = 128
VMEM_LIMIT = 48 * 1024 * 1024

ADAM_LR = 0.001
ADAM_B1 = 0.9
ADAM_B2 = 0.999
ADAM_EPS = 1e-08
ADAM_WD = 0.01
ADAM_STEP = 10

E_Q, E_K, E_V, E_BG, E_CG, E_HB, E_Z = 0, 1536, 3072, 4608, 5120, 5632, 6144
O_UC, O_DA, O_DG, O_Z = 0, 512, 1024, 1536


def _cp(sem):
    return pltpu.CompilerParams(dimension_semantics=sem, vmem_limit_bytes=VMEM_LIMIT)


_HBM_ANY = pl.BlockSpec(memory_space=pl.ANY)


def _sigmoid(z):
    return 1.0 / (1.0 + jnp.exp(-z))


def _tile(n, pref):
    t = pref
    while n % t:
        t //= 2
    return t


def _place():
    return lax.axis_index("x"), lax.axis_index("y"), lax.axis_index("c")


def _exchange_plan(ins, outs, send_sems, recv_sems, local_sems, nc):
    n = len(ins)
    x, y, c = _place()
    me_i = 4 * x + 2 * y + c

    def src(t, dev_i):
        return ins[t].at[dev_i] if t < nc else ins[t]

    def copies(arriving):
        cps = []
        for m in range(1, N_DEV):
            px = 1 - x if m & 4 else x
            py = 1 - y if m & 2 else y
            pc = 1 - c if m & 1 else c
            peer_i = 4 * px + 2 * py + pc
            for t in range(n):
                cps.append(pltpu.make_async_remote_copy(
                    src_ref=src(t, peer_i), dst_ref=outs[t].at[peer_i if arriving else me_i],
                    send_sem=send_sems.at[7 * t + m - 1], recv_sem=recv_sems.at[7 * t + m - 1],
                    device_id=(x, y, c) if arriving else (px, py, pc), device_id_type=pl.DeviceIdType.MESH))
        return cps

    def mine():
        return [pltpu.make_async_copy(src(t, me_i), outs[t].at[me_i], local_sems.at[t]) for t in range(n)]

    def start():
        for cp in mine() + copies(False):
            cp.start()

    def wait():
        for cp in copies(True):
            cp.wait_recv()
        for cp in copies(False):
            cp.wait_send()
        for cp in mine():
            cp.wait()

    return start, wait


def _exchange_sems(n):
    return [pltpu.SemaphoreType.DMA((7 * n,)), pltpu.SemaphoreType.DMA((7 * n,)), pltpu.SemaphoreType.DMA((n,))]


def _exchange_out_shapes(chunked, whole):
    return ([jax.ShapeDtypeStruct(a.shape, a.dtype) for a in chunked]
            + [jax.ShapeDtypeStruct((N_DEV, *a.shape), a.dtype) for a in whole])


def _grid_call(body, *, name, grid, in_specs, out_specs, out_shape, scratch_shapes, sem, args, fuse=None):
    if fuse is None:
        return pl.pallas_call(body, name=name, grid=grid, in_specs=in_specs, out_specs=out_specs, out_shape=out_shape,
                              scratch_shapes=scratch_shapes, compiler_params=_cp(sem))(*args)
    chunked, whole = fuse
    ex = list(chunked) + list(whole)
    n, n_in, n_out, n_sc = len(ex), len(in_specs), len(out_specs), len(scratch_shapes)

    def fused(*refs):
        ins, ex_in = refs[:n_in], refs[n_in:n_in + n]
        outs, ex_out = refs[n_in + n:n_in + n + n_out], refs[n_in + n + n_out:n_in + 2 * n + n_out]
        scratch = refs[n_in + 2 * n + n_out:n_in + 2 * n + n_out + n_sc]
        start, wait = _exchange_plan(ex_in, ex_out, *refs[-3:], len(chunked))
        first = functools.reduce(jnp.logical_and, [pl.program_id(a) == 0 for a in range(len(grid))])
        last = functools.reduce(jnp.logical_and, [pl.program_id(a) == g - 1 for a, g in enumerate(grid)])
        pl.when(first)(start)
        body(*ins, *outs, *scratch)
        pl.when(last)(wait)

    res = pl.pallas_call(
        fused, name=name, grid=grid, in_specs=list(in_specs) + [_HBM_ANY] * n,
        out_specs=list(out_specs) + [_HBM_ANY] * n, out_shape=list(out_shape) + _exchange_out_shapes(chunked, whole),
        scratch_shapes=list(scratch_shapes) + _exchange_sems(n),
        compiler_params=_cp(("arbitrary",) * len(grid)))(*args, *ex)
    return res[:n_out], res[n_out:]


def _mm_nn(a, b, *, name, out_dtype=f32, res=None, tn=1024, fuse=None):
    M, K = a.shape
    tm = _tile(M, 1024)
    if b.ndim == 3:
        tn = b.shape[2]
        N = b.shape[0] * tn
        b_spec = pl.BlockSpec((None, K, tn), lambda i, j: (j, 0, 0))
    else:
        N = b.shape[1]
        tn = _tile(N, tn)
        b_spec = pl.BlockSpec((K, tn), lambda i, j: (0, j))

    def body(*refs):
        if res is None:
            a_ref, b_ref, o_ref = refs
        else:
            a_ref, b_ref, r_ref, o_ref = refs
        acc = jnp.dot(a_ref[...], b_ref[...], preferred_element_type=f32)
        if res is not None:
            acc = acc + r_ref[...]
        o_ref[...] = acc.astype(out_dtype)

    in_specs = [pl.BlockSpec((tm, K), lambda i, j: (i, 0)), b_spec]
    args = [a, b]
    if res is not None:
        in_specs.append(pl.BlockSpec((tm, tn), lambda i, j: (i, j)))
        args.append(res)
    out = _grid_call(
        body, name=name, grid=(M // tm, N // tn), in_specs=in_specs,
        out_specs=[pl.BlockSpec((tm, tn), lambda i, j: (i, j))],
        out_shape=[jax.ShapeDtypeStruct((M, N), out_dtype)], scratch_shapes=[],
        sem=("parallel", "parallel"), args=args, fuse=fuse)
    return out[0] if fuse is None else (out[0][0], out[1])


def _mm_nt(a, b, *, name, out_dtype=f32, fuse=None):
    M, K = a.shape
    tm = _tile(M, 1024)
    if b.ndim == 3:
        nk, N, tk = b.shape
        b_spec = pl.BlockSpec((None, N, tk), lambda i, k: (k, 0, 0))
    else:
        N = b.shape[0]
        tk = _tile(K, 1024) if K % 1024 == 0 else _tile(K, 512)
        nk = K // tk
        b_spec = pl.BlockSpec((N, tk), lambda i, k: (0, k))

    def body(a_ref, b_ref, o_ref, acc_ref):
        k = pl.program_id(1)
        part = lax.dot_general(a_ref[...], b_ref[...], (((1,), (1,)), ((), ())), preferred_element_type=f32)

        @pl.when(k == 0)
        def _():
            acc_ref[...] = part

        @pl.when(k > 0)
        def _():
            acc_ref[...] += part

        @pl.when(k == nk - 1)
        def _():
            o_ref[...] = acc_ref[...].astype(out_dtype)

    out = _grid_call(
        body, name=name, grid=(M // tm, nk),
        in_specs=[pl.BlockSpec((tm, tk), lambda i, k: (i, k)), b_spec],
        out_specs=[pl.BlockSpec((tm, N), lambda i, k: (i, 0))],
        out_shape=[jax.ShapeDtypeStruct((M, N), out_dtype)],
        scratch_shapes=[pltpu.VMEM((tm, N), f32)],
        sem=("parallel", "arbitrary"), args=[a, b], fuse=fuse)
    return out[0] if fuse is None else (out[0][0], out[1])


def _load_once(src_hbm, dst_vmem, sem):
    @pl.when(pl.program_id(0) == 0)
    def _():
        cp = pltpu.make_async_copy(src_hbm, dst_vmem, sem)
        cp.start()
        cp.wait()


def _mm_nn_resident(a, b, *, name, tm=256, fuse=None):
    M, K = a.shape
    nch, _, tn = b.shape
    tm = _tile(M, tm)

    def body(a_ref, b_hbm, o_ref, bbuf, sem):
        _load_once(b_hbm, bbuf, sem)
        av = a_ref[...]
        for j in range(nch):
            o_ref[:, j * tn:(j + 1) * tn] = jnp.dot(av, bbuf[j], preferred_element_type=f32)

    out = _grid_call(
        body, name=name, grid=(M // tm,), in_specs=[pl.BlockSpec((tm, K), lambda i: (i, 0)), _HBM_ANY],
        out_specs=[pl.BlockSpec((tm, nch * tn), lambda i: (i, 0))],
        out_shape=[jax.ShapeDtypeStruct((M, nch * tn), f32)],
        scratch_shapes=[pltpu.VMEM(b.shape, b.dtype), pltpu.SemaphoreType.DMA],
        sem=("arbitrary",), args=[a, b], fuse=fuse)
    return out[0] if fuse is None else (out[0][0], out[1])


def _mm_nt_resident(a, b, *, name, fuse=None):
    M, K = a.shape
    nch, N, tk = b.shape
    tm = _tile(M, 512)

    def body(a_ref, b_hbm, o_ref, bbuf, sem):
        _load_once(b_hbm, bbuf, sem)
        acc = None
        for k in range(nch):
            part = lax.dot_general(a_ref[:, k * tk:(k + 1) * tk], bbuf[k], (((1,), (1,)), ((), ())),
                                   preferred_element_type=f32)
            acc = part if acc is None else acc + part
        o_ref[...] = acc

    out = _grid_call(
        body, name=name, grid=(M // tm,), in_specs=[pl.BlockSpec((tm, K), lambda i: (i, 0)), _HBM_ANY],
        out_specs=[pl.BlockSpec((tm, N), lambda i: (i, 0))],
        out_shape=[jax.ShapeDtypeStruct((M, N), f32)],
        scratch_shapes=[pltpu.VMEM(b.shape, b.dtype), pltpu.SemaphoreType.DMA],
        sem=("arbitrary",), args=[a, b], fuse=fuse)
    return out[0] if fuse is None else (out[0][0], out[1])


def _mm_tn(a, b, *, name, out_dtype=f32, tn=512, chunks=None, a_cols=None, fuse=None):
    S, Ka = a.shape
    a_blk = 0
    if a_cols is not None:
        a_blk, Ka = a_cols
    N = b.shape[1]
    ts = _tile(S, 2048)
    ns = S // ts
    if chunks:
        tn = N // chunks
        out_spec = pl.BlockSpec((None, Ka, tn), lambda j, s: (j, 0, 0))
        out_shape = jax.ShapeDtypeStruct((chunks, Ka, tn), out_dtype)
    else:
        tn = _tile(N, tn)
        out_spec = pl.BlockSpec((Ka, tn), lambda j, s: (0, j))
        out_shape = jax.ShapeDtypeStruct((Ka, N), out_dtype)

    def body(a_ref, b_ref, o_ref, acc_ref):
        s = pl.program_id(1)
        part = lax.dot_general(a_ref[...], b_ref[...], (((0,), (0,)), ((), ())), preferred_element_type=f32)

        @pl.when(s == 0)
        def _():
            acc_ref[...] = part

        @pl.when(s > 0)
        def _():
            acc_ref[...] += part

        @pl.when(s == ns - 1)
        def _():
            o_ref[...] = acc_ref[...].astype(out_dtype)

    out = _grid_call(
        body, name=name, grid=(N // tn, ns),
        in_specs=[pl.BlockSpec((ts, Ka), lambda j, s: (s, a_blk)), pl.BlockSpec((ts, tn), lambda j, s: (s, j))],
        out_specs=[out_spec], out_shape=[out_shape],
        scratch_shapes=[pltpu.VMEM((Ka, tn), f32)],
        sem=("parallel", "arbitrary"), args=[a, b], fuse=fuse)
    return out[0] if fuse is None else (out[0][0], out[1])


def _mm_out_loss(u, w, x_res, target, *, name):
    M, K = u.shape
    N = w.shape[1]
    tm = _tile(M, 512)
    nm = M // tm

    def body(u_ref, w_ref, x_ref, t_ref, dy_ref, dyb_ref, loss_ref, acc_ref):
        i = pl.program_id(0)
        y = jnp.dot(u_ref[...], w_ref[...], preferred_element_type=f32) + x_ref[...]
        err = y - t_ref[...]
        dy = err * (1.0 / N)
        dy_ref[...] = dy
        dyb_ref[...] = dy.astype(bf16)
        part = jnp.sum(err * err, axis=0, keepdims=True)

        @pl.when(i == 0)
        def _():
            acc_ref[...] = part

        @pl.when(i > 0)
        def _():
            acc_ref[...] += part

        @pl.when(i == nm - 1)
        def _():
            tot = jnp.sum(acc_ref[...], axis=1, keepdims=True)
            loss_ref[...] = jnp.broadcast_to(tot * (0.5 / N), (8, LANES))

    return pl.pallas_call(
        body, name=name, grid=(nm,),
        in_specs=[pl.BlockSpec((tm, K), lambda i: (i, 0)), pl.BlockSpec((K, N), lambda i: (0, 0)),
                  pl.BlockSpec((tm, N), lambda i: (i, 0)), pl.BlockSpec((tm, N), lambda i: (i, 0))],
        out_specs=[pl.BlockSpec((tm, N), lambda i: (i, 0)), pl.BlockSpec((tm, N), lambda i: (i, 0)),
                   pl.BlockSpec((8, LANES), lambda i: (0, 0))],
        out_shape=[jax.ShapeDtypeStruct((M, N), f32), jax.ShapeDtypeStruct((M, N), bf16),
                   jax.ShapeDtypeStruct((8, LANES), f32)],
        scratch_shapes=[pltpu.VMEM((1, N), f32)],
        compiler_params=_cp(("arbitrary",)),
    )(u, w, x_res, target)


def _rms_fwd(x, w, *, name):
    S, Dm = x.shape
    tm = _tile(S, 1024)

    def body(x_ref, w_ref, h_ref):
        xv = x_ref[...]
        r = lax.rsqrt(jnp.mean(xv * xv, axis=-1, keepdims=True) + EPS)
        h_ref[...] = (xv * r * w_ref[...]).astype(bf16)

    return pl.pallas_call(
        body, name=name, grid=(S // tm,),
        in_specs=[pl.BlockSpec((tm, Dm), lambda i: (i, 0)), pl.BlockSpec((1, Dm), lambda i: (0, 0))],
        out_specs=pl.BlockSpec((tm, Dm), lambda i: (i, 0)),
        out_shape=jax.ShapeDtypeStruct((S, Dm), bf16),
        compiler_params=_cp(("parallel",)),
    )(x, w)


def _out_proj_rms(u, w, res, norm_w, *, name):
    M, K = u.shape
    N = w.shape[1]
    tm = _tile(M, 512)

    def body(u_ref, w_ref, r_ref, nw_ref, y_ref, h_ref):
        y = jnp.dot(u_ref[...], w_ref[...], preferred_element_type=f32) + r_ref[...]
        y_ref[...] = y
        h_ref[...] = (y * lax.rsqrt(jnp.mean(y * y, axis=-1, keepdims=True) + EPS) * nw_ref[...]).astype(bf16)

    row = lambda width: pl.BlockSpec((tm, width), lambda i: (i, 0))
    return pl.pallas_call(
        body, name=name, grid=(M // tm,),
        in_specs=[row(K), pl.BlockSpec((K, N), lambda i: (0, 0)), row(N), pl.BlockSpec((1, N), lambda i: (0, 0))],
        out_specs=[row(N), row(N)],
        out_shape=[jax.ShapeDtypeStruct((M, N), f32), jax.ShapeDtypeStruct((M, N), bf16)],
        compiler_params=_cp(("parallel",)),
    )(u, w, res, norm_w)


def _mm_nt_rms_bwd(a, b, x, w, res, *, name, tm=512, fuse=None):
    M, K = a.shape
    nch, N, tk = b.shape
    tm = _tile(M, tm)

    def body(a_ref, b_hbm, x_ref, w_ref, res_ref, dx_ref, dxb_ref, gw_ref, bbuf, sem):
        i = pl.program_id(0)
        _load_once(b_hbm, bbuf, sem)
        dh_v = None
        for k in range(nch):
            part = lax.dot_general(a_ref[:, k * tk:(k + 1) * tk], bbuf[k], (((1,), (1,)), ((), ())),
                                   preferred_element_type=f32)
            dh_v = part if dh_v is None else dh_v + part
        xv = x_ref[...]
        r = lax.rsqrt(jnp.mean(xv * xv, axis=-1, keepdims=True) + EPS)
        xn = xv * r
        dxn = dh_v * w_ref[...]
        dx = r * (dxn - xn * jnp.mean(dxn * xn, axis=-1, keepdims=True)) + res_ref[...]
        dx_ref[...] = dx
        dxb_ref[...] = dx.astype(bf16)
        part = jnp.sum(dh_v * xn, axis=0, keepdims=True)

        @pl.when(i == 0)
        def _():
            gw_ref[...] = part

        @pl.when(i > 0)
        def _():
            gw_ref[...] += part

    row = lambda width: pl.BlockSpec((tm, width), lambda i: (i, 0))
    vec = pl.BlockSpec((1, N), lambda i: (0, 0))
    out = _grid_call(
        body, name=name, grid=(M // tm,),
        in_specs=[row(K), _HBM_ANY, row(N), vec, row(N)],
        out_specs=[row(N), row(N), vec],
        out_shape=[jax.ShapeDtypeStruct((M, N), f32), jax.ShapeDtypeStruct((M, N), bf16), jax.ShapeDtypeStruct((1, N), f32)],
        scratch_shapes=[pltpu.VMEM(b.shape, b.dtype), pltpu.SemaphoreType.DMA],
        sem=("arbitrary",), args=[a, b, x, w, res], fuse=fuse)
    return out if fuse is None else (*out[0], out[1])


def _rms_bwd(x, w, dh, res, *, name):
    S, Dm = x.shape
    tm = _tile(S, 512)

    def body(x_ref, w_ref, dh_ref, res_ref, dx_ref, dxb_ref, gw_ref):
        i = pl.program_id(0)
        xv = x_ref[...]
        r = lax.rsqrt(jnp.mean(xv * xv, axis=-1, keepdims=True) + EPS)
        xn = xv * r
        dh_v = dh_ref[...]
        dxn = dh_v * w_ref[...]
        dx = r * (dxn - xn * jnp.mean(dxn * xn, axis=-1, keepdims=True)) + res_ref[...]
        dx_ref[...] = dx
        dxb_ref[...] = dx.astype(bf16)
        part = jnp.sum(dh_v * xn, axis=0, keepdims=True)

        @pl.when(i == 0)
        def _():
            gw_ref[...] = part

        @pl.when(i > 0)
        def _():
            gw_ref[...] += part

    dx, dxb, gw = pl.pallas_call(
        body, name=name, grid=(S // tm,),
        in_specs=[pl.BlockSpec((tm, Dm), lambda i: (i, 0)), pl.BlockSpec((1, Dm), lambda i: (0, 0)),
                  pl.BlockSpec((tm, Dm), lambda i: (i, 0)), pl.BlockSpec((tm, Dm), lambda i: (i, 0))],
        out_specs=[pl.BlockSpec((tm, Dm), lambda i: (i, 0)), pl.BlockSpec((tm, Dm), lambda i: (i, 0)),
                   pl.BlockSpec((1, Dm), lambda i: (0, 0))],
        out_shape=[jax.ShapeDtypeStruct((S, Dm), f32), jax.ShapeDtypeStruct((S, Dm), bf16),
                   jax.ShapeDtypeStruct((1, Dm), f32)],
        compiler_params=_cp(("arbitrary",)),
    )(x, w, dh, res)
    return dx, dxb, gw


_INV_FREQ = [float(v) for v in (np.float32(ROPE_THETA) ** (-np.arange(ROT_HALF, dtype=np.float32) / np.float32(ROT_HALF))).astype(np.float32)]


def _rope_tables(pos_col):
    S = pos_col.shape[0]
    tm = _tile(S, 1024)

    def body(p_ref, c_ref, s1_ref, s2_ref):
        lane = lax.broadcasted_iota(jnp.int32, (tm, LANES), 1)
        lm = lane % HEAD_DIM
        fi = lm % ROT_HALF
        inv = jnp.zeros((tm, LANES), f32)
        for k in range(ROT_HALF):
            inv = jnp.where(fi == k, _INV_FREQ[k], inv)
        ang = p_ref[...].astype(f32) * inv
        cs = jnp.cos(ang)
        sn = jnp.sin(ang)
        c_ref[...] = jnp.where(lm < 2 * ROT_HALF, cs, 1.0)
        s1_ref[...] = jnp.where((lm >= ROT_HALF) & (lm < 2 * ROT_HALF), sn, 0.0)
        s2_ref[...] = jnp.where(lm < ROT_HALF, -sn, 0.0)

    spec = pl.BlockSpec((tm, LANES), lambda i: (i, 0))
    return pl.pallas_call(
        body, name="rope_tables", grid=(S // tm,),
        in_specs=[pl.BlockSpec((tm, 1), lambda i: (i, 0))],
        out_specs=[spec, spec, spec],
        out_shape=[jax.ShapeDtypeStruct((S, LANES), f32)] * 3,
        compiler_params=_cp(("parallel",)),
    )(pos_col)


def _head_mean(v, m):
    hi = v.astype(bf16)
    lo = (v - hi.astype(f32)).astype(bf16)
    return jnp.dot(hi, m, preferred_element_type=f32) + jnp.dot(lo, m, preferred_element_type=f32)


def _head_mean_matrix():
    i = np.arange(LANES)
    return jnp.asarray(((i[:, None] // HEAD_DIM) == (i[None, :] // HEAD_DIM)).astype(np.float32) / HEAD_DIM, dtype=bf16)


def _in_proj0(x, norm_w, b, tabs, nw, hm, *, fuse=None):
    M, K = x.shape
    nch, _, tn = b.shape
    tm = _tile(M, 256)

    def body(x_ref, w_ref, b_hbm, c_ref, s1_ref, s2_ref, nw_ref, m_ref, o_ref, qk_ref, h_ref, bbuf, sem):
        _load_once(b_hbm, bbuf, sem)
        xv = x_ref[...]
        av = (xv * lax.rsqrt(jnp.mean(xv * xv, axis=-1, keepdims=True) + EPS) * w_ref[...]).astype(bf16)
        h_ref[...] = av
        c, s1, s2, m = c_ref[...], s1_ref[...], s2_ref[...], m_ref[...]
        for j in range(nch):
            res = jnp.dot(av, bbuf[j], preferred_element_type=f32)
            o_ref[:, j * tn:(j + 1) * tn] = res
            for p in range(tn // LANES):
                col = j * tn + p * LANES
                if col >= E_V:
                    continue
                w = nw_ref[0:1, :] if col < E_K else nw_ref[1:2, :]
                t = res[:, p * LANES:(p + 1) * LANES]
                that = t * lax.rsqrt(_head_mean(t * t, m) + EPS) * w
                qk_ref[:, col:col + LANES] = (
                    that * c + pltpu.roll(that, ROT_HALF, axis=1) * s1 + pltpu.roll(that, LANES - ROT_HALF, axis=1) * s2)

    tab = pl.BlockSpec((tm, LANES), lambda i: (i, 0))
    out = _grid_call(
        body, name="in_proj0", grid=(M // tm,),
        in_specs=[pl.BlockSpec((tm, K), lambda i: (i, 0)), pl.BlockSpec((1, K), lambda i: (0, 0)), _HBM_ANY, tab, tab, tab,
                  pl.BlockSpec((2, LANES), lambda i: (0, 0)), pl.BlockSpec((LANES, LANES), lambda i: (0, 0))],
        out_specs=[pl.BlockSpec((tm, nch * tn), lambda i: (i, 0)), pl.BlockSpec((tm, E_V), lambda i: (i, 0)),
                   pl.BlockSpec((tm, K), lambda i: (i, 0))],
        out_shape=[jax.ShapeDtypeStruct((M, nch * tn), f32), jax.ShapeDtypeStruct((M, E_V), f32),
                   jax.ShapeDtypeStruct((M, K), bf16)],
        scratch_shapes=[pltpu.VMEM(b.shape, b.dtype), pltpu.SemaphoreType.DMA],
        sem=("arbitrary",), args=[x, norm_w, b, *tabs, nw, hm], fuse=fuse)
    return out if fuse is None else (*out[0], out[1])


def _key_geometry(nparts):
    qr = QBLK // nparts
    rho = lax.broadcasted_iota(jnp.int32, (2 * QBLK, 2 * QBLK), 0) % QBLK
    kap = lax.broadcasted_iota(jnp.int32, (2 * QBLK, 2 * QBLK), 1)
    n_q = QBLK + nparts * (rho % qr) + rho // qr
    tt = kap % (2 * qr)
    n_k = nparts * tt + kap // (2 * qr)
    dist = n_q - n_k
    return (dist >= 0) & (dist <= QBLK), (tt < qr).astype(jnp.int32)


def _stack_heads(t, lo):
    zero = jnp.zeros_like(t)
    return jnp.concatenate([jnp.where(lo, t, zero), jnp.where(lo, zero, t)], axis=0)


def _attn_block_fwd(qb, kcat, vcat, mask, lo):
    s = lax.dot_general(_stack_heads(qb, lo), kcat, (((1,), (1,)), ((), ())), preferred_element_type=f32) * SCALE
    s = jnp.where(mask, s, NEG)
    mx = jnp.max(s, axis=-1, keepdims=True)
    pexp = jnp.exp(s - mx)
    den = jnp.sum(pexp, axis=-1, keepdims=True)
    pn = (pexp * (1.0 / den)).astype(bf16)
    o2 = jnp.dot(pn, vcat, preferred_element_type=f32)
    lse2 = jnp.broadcast_to(mx + jnp.log(den), (2 * QBLK, LANES))
    return jnp.where(lo, o2[:QBLK], o2[QBLK:]), jnp.where(lo, lse2[:QBLK], lse2[QBLK:])


def _attn_block_bwd(qb, dob, kcat, vcat, lt, ds, mask, lo):
    lt_sw = pltpu.roll(lt, HEAD_DIM, axis=1)
    ds_sw = pltpu.roll(ds, HEAD_DIM, axis=1)
    lt2 = jnp.concatenate([jnp.where(lo, lt, lt_sw), jnp.where(lo, lt_sw, lt)], axis=0)
    ds2 = jnp.concatenate([jnp.where(lo, ds, ds_sw), jnp.where(lo, ds_sw, ds)], axis=0)
    q2 = _stack_heads(qb, lo)
    do2 = _stack_heads(dob, lo)
    s = lax.dot_general(q2, kcat, (((1,), (1,)), ((), ())), preferred_element_type=f32) * SCALE
    s = jnp.where(mask, s, NEG)
    prob = jnp.exp(s - jnp.concatenate([lt2, lt2], axis=1))
    dp = lax.dot_general(do2, vcat, (((1,), (1,)), ((), ())), preferred_element_type=f32)
    dsb = (prob * (dp - jnp.concatenate([ds2, ds2], axis=1)) * SCALE).astype(bf16)
    dq2 = jnp.dot(dsb, kcat, preferred_element_type=f32)
    dk = lax.dot_general(dsb, q2, (((0,), (0,)), ((), ())), preferred_element_type=f32)
    dv = lax.dot_general(prob.astype(bf16), do2, (((0,), (0,)), ((), ())), preferred_element_type=f32)
    return jnp.where(lo, dq2[:QBLK], dq2[QBLK:]), dk, dv


ATT_ROWS = 1024
ATT_UNROLL = 4


def _attn_fwd_local(qk, proj):
    S = qk.shape[0]
    tr = _tile(S, ATT_ROWS)
    lw = 4 * LANES
    nb = tr // QBLK

    def body(q_ref, k_ref, kh_ref, v_ref, vh_ref, o_ref, lse_ref, kbuf, vbuf):
        j = pl.program_id(0)
        kbuf[0:QBLK, :] = jnp.where(j > 0, kh_ref[...], 0.0)
        kbuf[QBLK:, :] = k_ref[...]
        vbuf[0:QBLK, :] = jnp.where(j > 0, vh_ref[...], 0.0)
        vbuf[QBLK:, :] = v_ref[...]
        band, is_prev = _key_geometry(1)
        lo = lax.broadcasted_iota(jnp.int32, (QBLK, LANES), 1) < HEAD_DIM

        def blk(c, carry):
            r0 = pl.multiple_of(c * QBLK, QBLK)
            first = jnp.where((c == 0) & (j == 0), 1, 0)
            mask = band & (is_prev * first == 0)
            for pp in range(lw // LANES):
                lanes = slice(pp * LANES, (pp + 1) * LANES)
                o, lse = _attn_block_fwd(q_ref[pl.ds(r0, QBLK), lanes].astype(bf16),
                                         kbuf[pl.ds(r0, 2 * QBLK), lanes].astype(bf16),
                                         vbuf[pl.ds(r0, 2 * QBLK), lanes].astype(bf16), mask, lo)
                o_ref[pl.ds(r0, QBLK), lanes] = o
                lse_ref[pl.ds(r0, QBLK), lanes] = lse
            return carry

        lax.fori_loop(0, nb, blk, 0, unroll=ATT_UNROLL)

    def halo(col):
        return pl.BlockSpec((QBLK, lw), lambda j, l: (jnp.maximum(j * nb - 1, 0), col + l))

    def tile(col):
        return pl.BlockSpec((tr, lw), lambda j, l: (j, col + l))

    return pl.pallas_call(
        body, name="attn_fwd0", grid=(S // tr, A_WIDTH // lw),
        in_specs=[tile(E_Q // lw), tile(E_K // lw), halo(E_K // lw), tile(E_V // lw), halo(E_V // lw)],
        out_specs=[tile(0), tile(0)],
        out_shape=[jax.ShapeDtypeStruct((S, A_WIDTH), f32)] * 2,
        scratch_shapes=[pltpu.VMEM((QBLK + tr, lw), f32)] * 2,
        compiler_params=_cp(("parallel", "parallel")),
    )(qk, qk, qk, proj, proj)


def _attn_bwd_local(qk, proj, do_a, lt, dsum, fuse=None):
    S = qk.shape[0]
    tr = _tile(S, ATT_ROWS)
    lw = 2 * LANES
    nb = tr // QBLK
    nt = S // tr

    def body(q_ref, qn_ref, do_ref, don_ref, lt_ref, ltn_ref, ds_ref, dsn_ref, k_ref, kh_ref, v_ref, vh_ref,
             dq_ref, dk_ref, dv_ref, kbuf, vbuf, dkbuf, dvbuf):
        j = pl.program_id(0)
        zeros = jnp.zeros((QBLK, lw), f32)
        kbuf[0:QBLK, :] = jnp.where(j > 0, kh_ref[...], 0.0)
        kbuf[pl.ds(QBLK, tr), :] = k_ref[...]
        kbuf[pl.ds(QBLK + tr, QBLK), :] = zeros
        vbuf[0:QBLK, :] = jnp.where(j > 0, vh_ref[...], 0.0)
        vbuf[pl.ds(QBLK, tr), :] = v_ref[...]
        vbuf[pl.ds(QBLK + tr, QBLK), :] = zeros
        dkbuf[...] = jnp.zeros_like(dkbuf)
        dvbuf[...] = jnp.zeros_like(dvbuf)
        band, is_prev = _key_geometry(1)
        lo = lax.broadcasted_iota(jnp.int32, (QBLK, LANES), 1) < HEAD_DIM

        def blk(c, carry):
            r0 = pl.multiple_of(c * QBLK, QBLK)
            first = jnp.where((c == 0) & (j == 0), 1, 0)
            mask = band & (is_prev * first == 0)
            for pp in range(lw // LANES):
                lanes = slice(pp * LANES, (pp + 1) * LANES)
                dq, dk, dv = _attn_block_bwd(
                    q_ref[pl.ds(r0, QBLK), lanes].astype(bf16), do_ref[pl.ds(r0, QBLK), lanes].astype(bf16),
                    kbuf[pl.ds(r0, 2 * QBLK), lanes].astype(bf16), vbuf[pl.ds(r0, 2 * QBLK), lanes].astype(bf16),
                    lt_ref[pl.ds(r0, QBLK), lanes], ds_ref[pl.ds(r0, QBLK), lanes], mask, lo)
                dq_ref[pl.ds(r0, QBLK), lanes] = dq
                dkbuf[pl.ds(r0, 2 * QBLK), lanes] += dk
                dvbuf[pl.ds(r0, 2 * QBLK), lanes] += dv
            return carry

        lax.fori_loop(0, nb, blk, 0, unroll=ATT_UNROLL)

        @pl.when(j < nt - 1)
        def _():
            mask = band & (is_prev == 1)
            for pp in range(lw // LANES):
                lanes = slice(pp * LANES, (pp + 1) * LANES)
                _, dk, dv = _attn_block_bwd(
                    qn_ref[:, lanes].astype(bf16), don_ref[:, lanes].astype(bf16),
                    kbuf[pl.ds(tr, 2 * QBLK), lanes].astype(bf16), vbuf[pl.ds(tr, 2 * QBLK), lanes].astype(bf16),
                    ltn_ref[:, lanes], dsn_ref[:, lanes], mask, lo)
                dkbuf[pl.ds(tr, 2 * QBLK), lanes] += dk
                dvbuf[pl.ds(tr, 2 * QBLK), lanes] += dv

        dk_ref[...] = dkbuf[pl.ds(QBLK, tr), :]
        dv_ref[...] = dvbuf[pl.ds(QBLK, tr), :]

    def prev_halo(col):
        return pl.BlockSpec((QBLK, lw), lambda j, l: (jnp.maximum(j * nb - 1, 0), col + l))

    def next_halo(col):
        return pl.BlockSpec((QBLK, lw), lambda j, l: (jnp.minimum((j + 1) * nb, S // QBLK - 1), col + l))

    def tile(col):
        return pl.BlockSpec((tr, lw), lambda j, l: (j, col + l))

    return _grid_call(
        body, name="attn_bwd0", grid=(nt, A_WIDTH // lw),
        in_specs=[tile(E_Q // lw), next_halo(E_Q // lw), tile(0), next_halo(0), tile(0), next_halo(0), tile(0), next_halo(0),
                  tile(E_K // lw), prev_halo(E_K // lw), tile(E_V // lw), prev_halo(E_V // lw)],
        out_specs=[tile(0)] * 3,
        out_shape=[jax.ShapeDtypeStruct((S, A_WIDTH), f32)] * 3,
        scratch_shapes=[pltpu.VMEM((tr + 2 * QBLK, lw), f32)] * 4,
        sem=("parallel", "parallel"), args=[qk, qk, do_a, do_a, lt, lt, dsum, dsum, qk, qk, proj, proj], fuse=fuse)


def _stream_view(a, d):
    S, W = a.shape
    return a.reshape(S // 8, 8, W) if d == 4 else a.reshape(S // 16, 2, 8, W)


def _stream_ref(ref, d, r, part, col, lw):
    n = ref.shape[0]
    if d == 4:
        return ref.at[pl.ds(0, n), r + 4 * part, pl.ds(col, lw)]
    return ref.at[pl.ds(0, n), r // 8, r % 8, pl.ds(col, lw)]


def _stream_geometry(S, d):
    nparts = 2 if d == 4 else 1
    rows = S // (d * nparts)
    return nparts, rows, QBLK // nparts


def _attn_fwd_dil(qk, proj, g, *, name):
    S = qk.shape[0]
    d = DILATIONS[g]
    nparts, rows, qr = _stream_geometry(S, d)
    nb = rows // qr
    lw = 2 * LANES if d == 4 else 4 * LANES
    nlg = A_WIDTH // lw
    nitems = d * nlg
    ins = ((0, E_Q + A_WIDTH * g, 0), (0, E_K + A_WIDTH * g, qr), (1, E_V + A_WIDTH * g, qr))

    def body(qk_hbm, pj_hbm, o_hbm, l_hbm, qbuf, kbuf, vbuf, obuf, lbuf, in_sems, out_sems):
        i = pl.program_id(0)
        slot = i % 2
        hbm_in = (qk_hbm, pj_hbm)
        bufs_in = (qbuf, kbuf, vbuf)

        def in_copies(item, sl):
            r, lg = item // nlg, item % nlg
            cps = []
            for a in range(nparts):
                for t, (src, col, pad) in enumerate(ins):
                    cps.append(pltpu.make_async_copy(
                        _stream_ref(hbm_in[src], d, r, a, pl.multiple_of(col + lw * lg, LANES), lw),
                        bufs_in[t].at[sl, a, pl.ds(pad, rows), :], in_sems.at[sl, 3 * a + t]))
            return cps

        def out_copies(item, sl):
            r, lg = item // nlg, item % nlg
            cps = []
            for a in range(nparts):
                for t, (buf, dst) in enumerate(((obuf, o_hbm), (lbuf, l_hbm))):
                    cps.append(pltpu.make_async_copy(
                        buf.at[sl, a], _stream_ref(dst, d, r, a, pl.multiple_of(lw * lg, LANES), lw),
                        out_sems.at[sl, 2 * a + t]))
            return cps

        @pl.when(i == 0)
        def _():
            for sl in range(2):
                for a in range(nparts):
                    kbuf[sl, a, 0:qr, :] = jnp.zeros((qr, lw), f32)
                    vbuf[sl, a, 0:qr, :] = jnp.zeros((qr, lw), f32)
            for cp in in_copies(0, 0):
                cp.start()

        @pl.when(i + 1 < nitems)
        def _():
            for cp in in_copies(i + 1, 1 - slot):
                cp.start()

        for cp in in_copies(i, slot):
            cp.wait()

        @pl.when(i >= 2)
        def _():
            for cp in out_copies(i - 2, slot):
                cp.wait()

        band, is_prev = _key_geometry(nparts)
        lo = lax.broadcasted_iota(jnp.int32, (QBLK, LANES), 1) < HEAD_DIM

        def blk(c, carry):
            r0 = pl.multiple_of(c * qr, qr)
            mask = band & (is_prev * jnp.where(c == 0, 1, 0) == 0)
            for pp in range(lw // LANES):
                lanes = slice(pp * LANES, (pp + 1) * LANES)
                qb = jnp.concatenate([qbuf[slot, a, pl.ds(r0, qr), lanes] for a in range(nparts)], axis=0).astype(bf16)
                kcat = jnp.concatenate([kbuf[slot, a, pl.ds(r0, 2 * qr), lanes] for a in range(nparts)], axis=0).astype(bf16)
                vcat = jnp.concatenate([vbuf[slot, a, pl.ds(r0, 2 * qr), lanes] for a in range(nparts)], axis=0).astype(bf16)
                o, lse = _attn_block_fwd(qb, kcat, vcat, mask, lo)
                for a in range(nparts):
                    obuf[slot, a, pl.ds(r0, qr), lanes] = o[a * qr:(a + 1) * qr]
                    lbuf[slot, a, pl.ds(r0, qr), lanes] = lse[a * qr:(a + 1) * qr]
            return carry

        lax.fori_loop(0, nb, blk, 0, unroll=ATT_UNROLL)

        for cp in out_copies(i, slot):
            cp.start()

        @pl.when(i == nitems - 1)
        def _():
            for cp in out_copies(i - 1, 1 - slot) + out_copies(i, slot):
                cp.wait()

    vshape = (S // 8, 8, A_WIDTH) if d == 4 else (S // 16, 2, 8, A_WIDTH)
    o, lse = pl.pallas_call(
        body, name=name, grid=(nitems,),
        in_specs=[_HBM_ANY, _HBM_ANY], out_specs=[_HBM_ANY, _HBM_ANY],
        out_shape=[jax.ShapeDtypeStruct(vshape, f32)] * 2,
        scratch_shapes=[pltpu.VMEM((2, nparts, rows, lw), f32), pltpu.VMEM((2, nparts, qr + rows, lw), f32),
                        pltpu.VMEM((2, nparts, qr + rows, lw), f32), pltpu.VMEM((2, nparts, rows, lw), f32),
                        pltpu.VMEM((2, nparts, rows, lw), f32),
                        pltpu.SemaphoreType.DMA((2, 3 * nparts)), pltpu.SemaphoreType.DMA((2, 2 * nparts))],
        compiler_params=_cp(("arbitrary",)),
    )(_stream_view(qk, d), _stream_view(proj, d))
    return o.reshape(S, A_WIDTH), lse.reshape(S, A_WIDTH)


def _attn_bwd_dil(qk, proj, do_a, lt, dsum, g, *, name):
    S = qk.shape[0]
    d = DILATIONS[g]
    nparts, rows, qr = _stream_geometry(S, d)
    nb = rows // qr
    lw = LANES if d == 4 else 4 * LANES
    nlg = A_WIDTH // lw
    nitems = d * nlg
    ins = ((0, E_Q + A_WIDTH * g, 0), (2, 0, 0), (3, 0, 0), (4, 0, 0), (0, E_K + A_WIDTH * g, qr), (1, E_V + A_WIDTH * g, qr))
    n_in = len(ins)

    def body(qk_hbm, pj_hbm, do_hbm, lt_hbm, ds_hbm, dq_hbm, dk_hbm, dv_hbm,
             qbuf, dobuf, ltbuf, dsbuf, kbuf, vbuf, dqbuf, dkbuf, dvbuf, in_sems, out_sems):
        i = pl.program_id(0)
        slot = i % 2
        hbm_in = (qk_hbm, pj_hbm, do_hbm, lt_hbm, ds_hbm)
        bufs_in = (qbuf, dobuf, ltbuf, dsbuf, kbuf, vbuf)

        def in_copies(item, sl):
            r, lg = item // nlg, item % nlg
            cps = []
            for a in range(nparts):
                for t, (src, col, pad) in enumerate(ins):
                    cps.append(pltpu.make_async_copy(
                        _stream_ref(hbm_in[src], d, r, a, pl.multiple_of(col + lw * lg, LANES), lw),
                        bufs_in[t].at[sl, a, pl.ds(pad, rows), :], in_sems.at[sl, n_in * a + t]))
            return cps

        def out_copies(item, sl):
            r, lg = item // nlg, item % nlg
            cps = []
            for a in range(nparts):
                for t, (buf, dst, pad) in enumerate(((dqbuf, dq_hbm, 0), (dkbuf, dk_hbm, qr), (dvbuf, dv_hbm, qr))):
                    cps.append(pltpu.make_async_copy(
                        buf.at[sl, a, pl.ds(pad, rows), :],
                        _stream_ref(dst, d, r, a, pl.multiple_of(lw * lg, LANES), lw), out_sems.at[sl, 3 * a + t]))
            return cps

        @pl.when(i == 0)
        def _():
            for sl in range(2):
                for a in range(nparts):
                    kbuf[sl, a, 0:qr, :] = jnp.zeros((qr, lw), f32)
                    vbuf[sl, a, 0:qr, :] = jnp.zeros((qr, lw), f32)
            for cp in in_copies(0, 0):
                cp.start()

        @pl.when(i + 1 < nitems)
        def _():
            for cp in in_copies(i + 1, 1 - slot):
                cp.start()

        for cp in in_copies(i, slot):
            cp.wait()

        @pl.when(i >= 2)
        def _():
            for cp in out_copies(i - 2, slot):
                cp.wait()

        for a in range(nparts):
            dkbuf[slot, a] = jnp.zeros((qr + rows, lw), f32)
            dvbuf[slot, a] = jnp.zeros((qr + rows, lw), f32)
        band, is_prev = _key_geometry(nparts)
        lo = lax.broadcasted_iota(jnp.int32, (QBLK, LANES), 1) < HEAD_DIM

        def blk(c, carry):
            r0 = pl.multiple_of(c * qr, qr)
            mask = band & (is_prev * jnp.where(c == 0, 1, 0) == 0)

            def rows_of(buf, n, lanes):
                return jnp.concatenate([buf[slot, a, pl.ds(r0, n), lanes] for a in range(nparts)], axis=0)

            for pp in range(lw // LANES):
                lanes = slice(pp * LANES, (pp + 1) * LANES)
                dq, dk, dv = _attn_block_bwd(
                    rows_of(qbuf, qr, lanes).astype(bf16), rows_of(dobuf, qr, lanes).astype(bf16),
                    rows_of(kbuf, 2 * qr, lanes).astype(bf16), rows_of(vbuf, 2 * qr, lanes).astype(bf16),
                    rows_of(ltbuf, qr, lanes), rows_of(dsbuf, qr, lanes), mask, lo)
                for a in range(nparts):
                    dqbuf[slot, a, pl.ds(r0, qr), lanes] = dq[a * qr:(a + 1) * qr]
                    dkbuf[slot, a, pl.ds(r0, 2 * qr), lanes] += dk[2 * a * qr:2 * (a + 1) * qr]
                    dvbuf[slot, a, pl.ds(r0, 2 * qr), lanes] += dv[2 * a * qr:2 * (a + 1) * qr]
            return carry

        lax.fori_loop(0, nb, blk, 0, unroll=ATT_UNROLL)

        for cp in out_copies(i, slot):
            cp.start()

        @pl.when(i == nitems - 1)
        def _():
            for cp in out_copies(i - 1, 1 - slot) + out_copies(i, slot):
                cp.wait()

    vshape = (S // 8, 8, A_WIDTH) if d == 4 else (S // 16, 2, 8, A_WIDTH)
    plain = pltpu.VMEM((2, nparts, rows, lw), f32)
    padded = pltpu.VMEM((2, nparts, qr + rows, lw), f32)
    outs = pl.pallas_call(
        body, name=name, grid=(nitems,),
        in_specs=[_HBM_ANY] * 5, out_specs=[_HBM_ANY] * 3,
        out_shape=[jax.ShapeDtypeStruct(vshape, f32)] * 3,
        scratch_shapes=[plain, plain, plain, plain, padded, padded, plain, padded, padded,
                        pltpu.SemaphoreType.DMA((2, n_in * nparts)), pltpu.SemaphoreType.DMA((2, 3 * nparts))],
        compiler_params=_cp(("arbitrary",)),
    )(*[_stream_view(a, d) for a in (qk, proj, do_a, lt, dsum)])
    return [o.reshape(S, A_WIDTH) for o in outs]


def _prev_halo(tm, h, col):
    return pl.BlockSpec((h, 512), lambda i: (jnp.maximum(i * (tm // h) - 1, 0), col))


def _next_halo(tm, h, col, S):
    return pl.BlockSpec((h, 512), lambda i: (jnp.minimum((i + 1) * (tm // h), S // h - 1), col))


def _mix0_fwd(o_g, lse_g, proj, conv_w):
    S = proj.shape[0]
    tm = _tile(S, 256)

    def body(o0, o1, o2, l0, l1, l2, bg_ref, cg_ref, hb_ref, z_ref, cgh_ref, hbh_ref, w_ref,
             u_ref, oa_ref, lt_ref, tbuf):
        i = pl.program_id(0)
        ls = [l0[...], l1[...], l2[...]]
        mx = jnp.maximum(jnp.maximum(ls[0], ls[1]), ls[2])
        es = [jnp.exp(l - mx) for l in ls]
        tot = es[0] + es[1] + es[2]
        lt_ref[...] = mx + jnp.log(tot)
        inv = 1.0 / tot
        z = z_ref[...]
        sz = z * _sigmoid(z)
        oa = (es[0] * inv) * o0[...] + (es[1] * inv) * o1[...] + (es[2] * inv) * o2[...]
        oa_ref[...] = oa
        u_ref[:, :A_WIDTH] = (oa * sz[:, :A_WIDTH]).astype(bf16)
        t = cg_ref[...] * hb_ref[...]
        tbuf[0:8, :] = jnp.where(i > 0, cgh_ref[...] * hbh_ref[...], 0.0)
        tbuf[8:, :] = t
        cv = w_ref[2:3, :] * t + w_ref[1:2, :] * tbuf[pl.ds(7, tm), :] + w_ref[0:1, :] * tbuf[pl.ds(6, tm), :]
        u_ref[:, A_WIDTH:] = (bg_ref[...] * cv * sz[:, A_WIDTH:]).astype(bf16)

    row = lambda w, c: pl.BlockSpec((tm, w), lambda i: (i, c))
    return pl.pallas_call(
        body, name="mix0_fwd", grid=(S // tm,),
        in_specs=[row(512, 0)] * 6
        + [row(512, E_BG // 512), row(512, E_CG // 512), row(512, E_HB // 512), row(1024, E_Z // 1024),
           _prev_halo(tm, 8, E_CG // 512), _prev_halo(tm, 8, E_HB // 512), pl.BlockSpec((SC_WIDTH, 512), lambda i: (0, 0))],
        out_specs=[row(1024, 0), row(512, 0), row(512, 0)],
        out_shape=[jax.ShapeDtypeStruct((S, D_MODEL), bf16), jax.ShapeDtypeStruct((S, A_WIDTH), f32),
                   jax.ShapeDtypeStruct((S, A_WIDTH), f32)],
        scratch_shapes=[pltpu.VMEM((tm + 8, 512), f32)],
        compiler_params=_cp(("parallel",)),
    )(*o_g, *lse_g, proj, proj, proj, proj, proj, proj, conv_w)


def _dsilu(z, sg):
    return sg * (1.0 + z * (1.0 - sg))


def _d_gate_in(dy_ref, wo_ref):
    return lax.dot_general(dy_ref[...], wo_ref[...], (((1,), (1,)), ((), ())), preferred_element_type=f32)


def _mix0_bwd_a(dy, w_out, proj, o_a, conv_w):
    S = proj.shape[0]
    tm = _tile(S, 256)

    def body(dy_ref, wo_ref, bg_ref, cg_ref, hb_ref, z_ref, cgh_ref, hbh_ref, oa_ref, w_ref,
             dz_ref, doa_ref, ds_ref, dbg_ref, dcv_ref, tbuf):
        i = pl.program_id(0)
        lo = lax.broadcasted_iota(jnp.int32, (tm, LANES), 1) < HEAD_DIM
        z = z_ref[...]
        sg = _sigmoid(z)
        sz = z * sg
        dsz = _dsilu(z, sg)
        du_v = _d_gate_in(dy_ref, wo_ref)
        t = cg_ref[...] * hb_ref[...]
        tbuf[0:8, :] = jnp.where(i > 0, cgh_ref[...] * hbh_ref[...], 0.0)
        tbuf[8:, :] = t
        cv = w_ref[2:3, :] * t + w_ref[1:2, :] * tbuf[pl.ds(7, tm), :] + w_ref[0:1, :] * tbuf[pl.ds(6, tm), :]
        bg = bg_ref[...]
        oa = oa_ref[...]
        dz_ref[:, :A_WIDTH] = (du_v[:, :A_WIDTH] * oa * dsz[:, :A_WIDTH]).astype(bf16)
        dz_ref[:, A_WIDTH:] = (du_v[:, A_WIDTH:] * (bg * cv) * dsz[:, A_WIDTH:]).astype(bf16)
        doa = du_v[:, :A_WIDTH] * sz[:, :A_WIDTH]
        dyb = du_v[:, A_WIDTH:] * sz[:, A_WIDTH:]
        doa_ref[...] = doa
        dbg_ref[...] = (dyb * cv).astype(bf16)
        dcv_ref[...] = dyb * bg
        prod = doa * oa
        for p in range(4):
            pp = prod[:, p * LANES:(p + 1) * LANES]
            sa = jnp.sum(jnp.where(lo, pp, 0.0), axis=-1, keepdims=True)
            sb = jnp.sum(jnp.where(lo, 0.0, pp), axis=-1, keepdims=True)
            ds_ref[:, p * LANES:(p + 1) * LANES] = jnp.where(lo, sa, sb)

    row = lambda w, c: pl.BlockSpec((tm, w), lambda i: (i, c))
    return pl.pallas_call(
        body, name="mix0_bwd_a", grid=(S // tm,),
        in_specs=[row(1024, 0), pl.BlockSpec((D_MODEL, D_MODEL), lambda i: (0, 0)),
                  row(512, E_BG // 512), row(512, E_CG // 512), row(512, E_HB // 512), row(1024, E_Z // 1024),
                  _prev_halo(tm, 8, E_CG // 512), _prev_halo(tm, 8, E_HB // 512), row(512, 0),
                  pl.BlockSpec((SC_WIDTH, 512), lambda i: (0, 0))],
        out_specs=[row(1024, 0), row(512, 0), row(512, 0), row(512, 0), row(512, 0)],
        out_shape=[jax.ShapeDtypeStruct((S, D_MODEL), bf16), jax.ShapeDtypeStruct((S, A_WIDTH), f32),
                   jax.ShapeDtypeStruct((S, A_WIDTH), f32), jax.ShapeDtypeStruct((S, 512), bf16),
                   jax.ShapeDtypeStruct((S, 512), f32)],
        scratch_shapes=[pltpu.VMEM((tm + 8, 512), f32)],
        compiler_params=_cp(("parallel",)),
    )(dy, w_out, proj, proj, proj, proj, proj, proj, o_a, conv_w)


def _mix0_bwd_b(dcv, proj, conv_w):
    S = proj.shape[0]
    tm = _tile(S, 256)
    nt = S // tm

    def body(dcv_ref, dcvn_ref, cg_ref, hb_ref, cgh_ref, hbh_ref, w_ref, dcg_ref, dhb_ref, gw_ref, tbuf, dbuf):
        i = pl.program_id(0)
        cg = cg_ref[...]
        hb = hb_ref[...]
        t = cg * hb
        tbuf[0:8, :] = jnp.where(i > 0, cgh_ref[...] * hbh_ref[...], 0.0)
        tbuf[8:, :] = t
        dcv_v = dcv_ref[...]
        dbuf[0:tm, :] = dcv_v
        dbuf[tm:, :] = jnp.where(i < nt - 1, dcvn_ref[...], 0.0)
        dt = w_ref[2:3, :] * dcv_v + w_ref[1:2, :] * dbuf[pl.ds(1, tm), :] + w_ref[0:1, :] * dbuf[pl.ds(2, tm), :]
        dcg_ref[...] = (dt * hb).astype(bf16)
        dhb_ref[...] = (dt * cg).astype(bf16)
        g2 = jnp.sum(dcv_v * t, axis=0, keepdims=True)
        g1 = jnp.sum(dcv_v * tbuf[pl.ds(7, tm), :], axis=0, keepdims=True)
        g0 = jnp.sum(dcv_v * tbuf[pl.ds(6, tm), :], axis=0, keepdims=True)
        part = jnp.concatenate([g0, g1, g2, jnp.zeros((5, 512), f32)], axis=0)

        @pl.when(i == 0)
        def _():
            gw_ref[...] = part

        @pl.when(i > 0)
        def _():
            gw_ref[...] += part

    row = lambda w, c: pl.BlockSpec((tm, w), lambda i: (i, c))
    return pl.pallas_call(
        body, name="mix0_bwd_b", grid=(nt,),
        in_specs=[row(512, 0), _next_halo(tm, 8, 0, S), row(512, E_CG // 512), row(512, E_HB // 512),
                  _prev_halo(tm, 8, E_CG // 512), _prev_halo(tm, 8, E_HB // 512),
                  pl.BlockSpec((SC_WIDTH, 512), lambda i: (0, 0))],
        out_specs=[row(512, 0), row(512, 0), pl.BlockSpec((8, 512), lambda i: (0, 0))],
        out_shape=[jax.ShapeDtypeStruct((S, 512), bf16), jax.ShapeDtypeStruct((S, 512), bf16),
                   jax.ShapeDtypeStruct((8, 512), f32)],
        scratch_shapes=[pltpu.VMEM((tm + 8, 512), f32), pltpu.VMEM((tm + 8, 512), f32)],
        compiler_params=_cp(("arbitrary",)),
    )(dcv, dcv, proj, proj, proj, proj, conv_w)


def _qk_bwd(dq_g, dk_g, dv_g, proj, tabs, nw, hm, dbg, dcg, dhb, dz):
    S = proj.shape[0]
    tm = _tile(S, 256)

    def body(*refs):
        d_refs = refs[0:6]
        dv_refs = refs[6:9]
        x_ref, c_ref, s1_ref, s2_ref, nw_ref, m_ref, dbg_ref, dcg_ref, dhb_ref, dz_ref, o_ref, gw_ref = refs[9:]
        i = pl.program_id(0)
        c, s1, s2, m = c_ref[...], s1_ref[...], s2_ref[...], m_ref[...]
        accs = []
        for kind in range(2):
            w = nw_ref[kind:kind + 1, :]
            acc = jnp.zeros((1, LANES), f32)
            for gi in range(N_GROUPS):
                for p in range(4):
                    col = kind * 1536 + gi * 512 + p * LANES
                    dout = d_refs[kind * 3 + gi][:, p * LANES:(p + 1) * LANES]
                    t = x_ref[:, col:col + LANES]
                    dthat = (dout * c + pltpu.roll(dout * s1, LANES - ROT_HALF, axis=1)
                             + pltpu.roll(dout * s2, ROT_HALF, axis=1))
                    r = lax.rsqrt(_head_mean(t * t, m) + EPS)
                    tn = t * r
                    acc = acc + jnp.sum(dthat * tn, axis=0, keepdims=True)
                    dtn = dthat * w
                    o_ref[:, col:col + LANES] = (r * (dtn - tn * _head_mean(dtn * tn, m))).astype(bf16)
            accs.append(acc + pltpu.roll(acc, HEAD_DIM, axis=1))
        for gi in range(N_GROUPS):
            o_ref[:, E_V + gi * 512:E_V + (gi + 1) * 512] = dv_refs[gi][...].astype(bf16)
        o_ref[:, E_BG:E_CG] = dbg_ref[...]
        o_ref[:, E_CG:E_HB] = dcg_ref[...]
        o_ref[:, E_HB:E_Z] = dhb_ref[...]
        o_ref[:, E_Z:] = dz_ref[...]
        part = jnp.concatenate([accs[0], accs[1], jnp.zeros((6, LANES), f32)], axis=0)

        @pl.when(i == 0)
        def _():
            gw_ref[...] = part

        @pl.when(i > 0)
        def _():
            gw_ref[...] += part

    row = lambda w, c: pl.BlockSpec((tm, w), lambda i: (i, c))
    tab = row(LANES, 0)
    return pl.pallas_call(
        body, name="qk_bwd", grid=(S // tm,),
        in_specs=[row(512, 0)] * 9 + [row(3072, 0), tab, tab, tab, pl.BlockSpec((2, LANES), lambda i: (0, 0)),
                                      pl.BlockSpec((LANES, LANES), lambda i: (0, 0)),
                                      row(512, 0), row(512, 0), row(512, 0), row(1024, 0)],
        out_specs=[row(EVEN_IN, 0), pl.BlockSpec((8, LANES), lambda i: (0, 0))],
        out_shape=[jax.ShapeDtypeStruct((S, EVEN_IN), bf16), jax.ShapeDtypeStruct((8, LANES), f32)],
        compiler_params=_cp(("arbitrary",)),
    )(*dq_g, *dk_g, *dv_g, proj, *tabs, nw, hm, dbg, dcg, dhb, dz)


def _inv_count(i, tm, p):
    rowg = lax.broadcasted_iota(jnp.int32, (tm, 1), 0) + i * tm
    return 1.0 / jnp.minimum(rowg + 1, p).astype(f32)


def _layer_norm_stats(c):
    mu = jnp.mean(c, axis=-1, keepdims=True)
    cen = c - mu
    rstd = lax.rsqrt(jnp.mean(cen * cen, axis=-1, keepdims=True) + EPS)
    return cen * rstd, rstd


def _fill_pool_buf(i, ubuf, uc_ref, uch_ref):
    ubuf[0:16, :] = jnp.where(i > 0, uch_ref[...], 0.0)
    ubuf[16:, :] = uc_ref[...]


def _pooled(i, tm, ubuf, gi):
    p = POOL_SIZES[gi]
    cols = slice(gi * LANES, (gi + 1) * LANES)
    acc = ubuf[pl.ds(16, tm), cols]
    cur = acc
    for jj in range(1, p):
        acc = acc + ubuf[pl.ds(16 - jj, tm), cols]
    return acc * _inv_count(i, tm, p) - cur


def _fill_glu_buf(i, gbuf, da_ref, dg_ref, dah_ref, dgh_ref):
    gbuf[0:32, :] = jnp.where(i > 0, dah_ref[...] * _sigmoid(dgh_ref[...]), 0.0)
    gbuf[32:, :] = da_ref[...] * _sigmoid(dg_ref[...])


def _shift_copies(buf, sh, tm):
    for b in range(1, 8):
        sh[b - 1] = buf[pl.ds(b, tm + 24), :]


CONV_ROWS = 32


def _window(buf, sh, base, off, rows):
    b = off % 8
    if b == 0:
        return buf[pl.ds(base + off, rows), :]
    return sh[b - 1, pl.ds(base + (off - b), rows), :]


def _mix1_fwd(proj, pool_w, pool_scale, dconv_w, dconv_b, ln_w, ln_b):
    S = proj.shape[0]
    tm = _tile(S, 256)

    def body(uc_ref, uch_ref, da_ref, dg_ref, dah_ref, dgh_ref, za_ref, zb_ref, pw_ref, ps_ref, cw_ref, cb_ref,
             lw_ref, lb_ref, u_ref, c_ref, mc_ref, ubuf, gbuf, gsh):
        i = pl.program_id(0)
        _fill_pool_buf(i, ubuf, uc_ref, uch_ref)
        za = za_ref[...]
        for gi in range(4):
            cols = slice(gi * LANES, (gi + 1) * LANES)
            mc = jnp.dot(_pooled(i, tm, ubuf, gi).astype(bf16), pw_ref[gi], preferred_element_type=f32)
            mc_ref[:, cols] = mc
            zg = za[:, cols]
            u_ref[:, cols] = (mc * ps_ref[:, cols] * (zg * _sigmoid(zg))).astype(bf16)
        _fill_glu_buf(i, gbuf, da_ref, dg_ref, dah_ref, dgh_ref)
        _shift_copies(gbuf, gsh, tm)
        c = jnp.zeros((tm, 512), f32) + cb_ref[...]
        for k in range(D_CONV):
            c = c + cw_ref[k:k + 1, :] * _window(gbuf, gsh, 0, 32 - (D_CONV - 1) + k, tm)
        c_ref[...] = c
        yhat, _ = _layer_norm_stats(c)
        l = yhat * lw_ref[...] + lb_ref[...]
        zb = zb_ref[...]
        u_ref[:, 512:] = (l * _sigmoid(l) * (zb * _sigmoid(zb))).astype(bf16)

    row = lambda w, c: pl.BlockSpec((tm, w), lambda i: (i, c))
    vec = pl.BlockSpec((1, 512), lambda i: (0, 0))
    return pl.pallas_call(
        body, name="mix1_fwd", grid=(S // tm,),
        in_specs=[row(512, 0), _prev_halo(tm, 16, 0), row(512, 1), row(512, 2), _prev_halo(tm, 32, 1), _prev_halo(tm, 32, 2),
                  row(512, 3), row(512, 4), pl.BlockSpec((4, LANES, LANES), lambda i: (0, 0, 0)), vec,
                  pl.BlockSpec((D_CONV, 512), lambda i: (0, 0)), vec, vec, vec],
        out_specs=[row(1024, 0), row(512, 0), row(512, 0)],
        out_shape=[jax.ShapeDtypeStruct((S, D_MODEL), bf16), jax.ShapeDtypeStruct((S, 512), f32),
                   jax.ShapeDtypeStruct((S, 512), f32)],
        scratch_shapes=[pltpu.VMEM((tm + 16, 512), f32), pltpu.VMEM((tm + 32, 512), f32),
                        pltpu.VMEM((7, tm + 24, 512), f32)],
        compiler_params=_cp(("parallel",)),
    )(proj, proj, proj, proj, proj, proj, proj, proj, pool_w, pool_scale, dconv_w, dconv_b, ln_w, ln_b)


def _mix1_bwd_a(dy, w_out, proj, c, mc, pool_w, pool_scale, ln_w, ln_b):
    S = proj.shape[0]
    tm = _tile(S, 256)

    def body(dy_ref, wo_ref, za_ref, zb_ref, c_ref, mc_ref, pw_ref, ps_ref, lw_ref, lb_ref,
             dz_ref, dc_ref, dpl_ref, dmc_ref, acc_ref):
        i = pl.program_id(0)
        du_v = _d_gate_in(dy_ref, wo_ref)
        ps = ps_ref[...]
        za = za_ref[...]
        sga = _sigmoid(za)
        mcv = mc_ref[...]
        dz_ref[:, :512] = (du_v[:, :512] * (mcv * ps) * _dsilu(za, sga)).astype(bf16)
        dyc = du_v[:, :512] * (za * sga)
        g_ps = jnp.sum(dyc * mcv, axis=0, keepdims=True)
        dmc = (dyc * ps).astype(bf16)
        dmc_ref[...] = dmc
        for gi in range(4):
            cols = slice(gi * LANES, (gi + 1) * LANES)
            dpl_ref[:, cols] = lax.dot_general(dmc[:, cols], pw_ref[gi], (((1,), (1,)), ((), ())), preferred_element_type=f32)
        yhat, rstd = _layer_norm_stats(c_ref[...])
        lw = lw_ref[...]
        l = yhat * lw + lb_ref[...]
        sgl = _sigmoid(l)
        zb = zb_ref[...]
        sgb = _sigmoid(zb)
        dz_ref[:, 512:] = (du_v[:, 512:] * (l * sgl) * _dsilu(zb, sgb)).astype(bf16)
        dl = du_v[:, 512:] * (zb * sgb) * _dsilu(l, sgl)
        g_lb = jnp.sum(dl, axis=0, keepdims=True)
        g_lw = jnp.sum(dl * yhat, axis=0, keepdims=True)
        dyh = dl * lw
        dc = rstd * (dyh - jnp.mean(dyh, axis=-1, keepdims=True) - yhat * jnp.mean(dyh * yhat, axis=-1, keepdims=True))
        dc_ref[...] = dc
        g_db = jnp.sum(dc, axis=0, keepdims=True)
        part = jnp.concatenate([g_ps, g_lw, g_lb, g_db, jnp.zeros((4, 512), f32)], axis=0)

        @pl.when(i == 0)
        def _():
            acc_ref[...] = part

        @pl.when(i > 0)
        def _():
            acc_ref[...] += part

    row = lambda w, c_: pl.BlockSpec((tm, w), lambda i: (i, c_))
    vec = pl.BlockSpec((1, 512), lambda i: (0, 0))
    return pl.pallas_call(
        body, name="mix1_bwd_a", grid=(S // tm,),
        in_specs=[row(1024, 0), pl.BlockSpec((D_MODEL, D_MODEL), lambda i: (0, 0)),
                  row(512, 3), row(512, 4), row(512, 0), row(512, 0),
                  pl.BlockSpec((4, LANES, LANES), lambda i: (0, 0, 0)), vec, vec, vec],
        out_specs=[row(1024, 0), row(512, 0), row(512, 0), row(512, 0), pl.BlockSpec((8, 512), lambda i: (0, 0))],
        out_shape=[jax.ShapeDtypeStruct((S, D_MODEL), bf16), jax.ShapeDtypeStruct((S, 512), f32),
                   jax.ShapeDtypeStruct((S, 512), f32), jax.ShapeDtypeStruct((S, 512), bf16),
                   jax.ShapeDtypeStruct((8, 512), f32)],
        compiler_params=_cp(("arbitrary",)),
    )(dy, w_out, proj, proj, c, mc, pool_w, pool_scale, ln_w, ln_b)


def _mix1_bwd_b(dc, dpl, dmc, dz, proj, dconv_w):
    S = proj.shape[0]
    tm = _tile(S, 256)
    nt = S // tm

    def body(dc_ref, dcn_ref, dpl_ref, dpn_ref, dmc_ref, dz_ref, uc_ref, uch_ref, da_ref, dg_ref,
             cw_ref, o_ref, gcw_ref, gpw_ref, ubuf, dcbuf, dpbuf, dcsh, gacc):
        i = pl.program_id(0)
        last = i == nt - 1
        _fill_pool_buf(i, ubuf, uc_ref, uch_ref)
        dcbuf[0:tm, :] = dc_ref[...]
        dcbuf[tm:, :] = jnp.where(last, 0.0, dcn_ref[...])
        _shift_copies(dcbuf, dcsh, tm)
        dpl_v = dpl_ref[...]
        for gi in range(4):
            p = POOL_SIZES[gi]
            cols = slice(gi * LANES, (gi + 1) * LANES)
            dpbuf[0:tm, cols] = dpl_v[:, cols] * _inv_count(i, tm, p)
            dpbuf[tm:, cols] = jnp.where(last, 0.0, dpn_ref[:, cols] * (1.0 / p))
        gpw = []
        for gi in range(4):
            p = POOL_SIZES[gi]
            cols = slice(gi * LANES, (gi + 1) * LANES)
            acc = -dpl_v[:, cols]
            for jj in range(p):
                acc = acc + dpbuf[pl.ds(jj, tm), cols]
            o_ref[:, cols] = acc.astype(bf16)
            pooled = _pooled(i, tm, ubuf, gi).astype(bf16)
            gpw.append(lax.dot_general(pooled, dmc_ref[:, cols], (((0,), (0,)), ((), ())), preferred_element_type=f32))
        gacc[...] = jnp.zeros_like(gacc)

        def conv_rows(ci, carry):
            base = pl.multiple_of(ci * CONV_ROWS, CONV_ROWS)
            da = da_ref[pl.ds(base, CONV_ROWS), :]
            sg = _sigmoid(dg_ref[pl.ds(base, CONV_ROWS), :])
            gl = da * sg
            dgl = jnp.zeros((CONV_ROWS, 512), f32)
            for k in range(D_CONV):
                win = _window(dcbuf, dcsh, base, D_CONV - 1 - k, CONV_ROWS)
                dgl = dgl + cw_ref[k:k + 1, :] * win
                gacc[k] += jnp.sum((gl * win).reshape(CONV_ROWS // 8, 8, 512), axis=0)
            o_ref[pl.ds(base, CONV_ROWS), O_DA:O_DG] = (dgl * sg).astype(bf16)
            o_ref[pl.ds(base, CONV_ROWS), O_DG:O_Z] = (dgl * da * sg * (1.0 - sg)).astype(bf16)
            return carry

        lax.fori_loop(0, tm // CONV_ROWS, conv_rows, 0)
        o_ref[:, O_Z:] = dz_ref[...]
        gcw_part = jnp.concatenate(
            [jnp.sum(gacc[k], axis=0, keepdims=True) for k in range(D_CONV)] + [jnp.zeros((1, 512), f32)], axis=0)

        @pl.when(i == 0)
        def _():
            gcw_ref[...] = gcw_part
            for gi in range(4):
                gpw_ref[gi] = gpw[gi]

        @pl.when(i > 0)
        def _():
            gcw_ref[...] += gcw_part
            for gi in range(4):
                gpw_ref[gi] += gpw[gi]

    row = lambda w, c_: pl.BlockSpec((tm, w), lambda i: (i, c_))
    return pl.pallas_call(
        body, name="mix1_bwd_b", grid=(nt,),
        in_specs=[row(512, 0), _next_halo(tm, 32, 0, S), row(512, 0), _next_halo(tm, 16, 0, S), row(512, 0), row(1024, 0),
                  row(512, 0), _prev_halo(tm, 16, 0), row(512, 1), row(512, 2),
                  pl.BlockSpec((D_CONV, 512), lambda i: (0, 0))],
        out_specs=[row(ODD_IN, 0), pl.BlockSpec((32, 512), lambda i: (0, 0)),
                   pl.BlockSpec((4, LANES, LANES), lambda i: (0, 0, 0))],
        out_shape=[jax.ShapeDtypeStruct((S, ODD_IN), bf16), jax.ShapeDtypeStruct((32, 512), f32),
                   jax.ShapeDtypeStruct((4, LANES, LANES), f32)],
        scratch_shapes=[pltpu.VMEM((tm + 16, 512), f32), pltpu.VMEM((tm + 32, 512), f32),
                        pltpu.VMEM((tm + 16, 512), f32), pltpu.VMEM((7, tm + 24, 512), f32),
                        pltpu.VMEM((D_CONV, 8, 512), f32)],
        compiler_params=_cp(("arbitrary",)),
    )(dc, dc, dpl, dpl, dmc, dz, proj, proj, proj, proj, dconv_w)


_SMALL_LATE = ["e_q_norm_w", "e_k_norm_w", "e_conv_w", "o_norm_w", "o_pool_w", "o_pool_scale", "o_dconv_w", "o_dconv_b",
               "o_ln_w", "o_ln_b"]


def _local_step(x, pos_col, target, w, dist=None):
    hm = _head_mean_matrix()
    nw = jnp.concatenate([jnp.tile(w["e_q_norm_w"], (1, 2)), jnp.tile(w["e_k_norm_w"], (1, 2))], axis=0)
    tabs = _rope_tables(pos_col)
    pool_wb = w["o_pool_w"].astype(bf16)
    e_norm_w, e_w_in = w["e_norm_w"], w["e_w_in"]

    if dist is None:
        proj0, qk, h0 = _in_proj0(x, e_norm_w, e_w_in, tabs, nw, hm)
    else:
        proj0, qk, h0, gathered = _in_proj0(x, e_norm_w, e_w_in, tabs, nw, hm, fuse=([], dist[0]))
        w = {**w, **dist[1](gathered)}
    e_conv_w, e_w_out, o_norm_w, o_w_in, o_w_out = w["e_conv_w"], w["e_w_out"], w["o_norm_w"], w["o_w_in"], w["o_w_out"]
    o_pool_scale, o_dconv_w, o_dconv_b, o_ln_w, o_ln_b = (w[k] for k in ("o_pool_scale", "o_dconv_w", "o_dconv_b", "o_ln_w", "o_ln_b"))
    o_g, lse_g = [], []
    for g in range(N_GROUPS):
        o, l = _attn_fwd_local(qk, proj0) if g == 0 else _attn_fwd_dil(qk, proj0, g, name=f"attn_fwd{g}")
        o_g.append(o)
        lse_g.append(l)
    u0, o_a, lt = _mix0_fwd(o_g, lse_g, proj0, e_conv_w)
    x1, h1 = _out_proj_rms(u0, e_w_out, x, o_norm_w, name="out_proj0")
    o_w_in3 = o_w_in.reshape(1, D_MODEL, ODD_IN)
    proj1 = _mm_nn_resident(h1, o_w_in3, name="in_proj1", tm=512)
    u1, c1, mc1 = _mix1_fwd(proj1, pool_wb, o_pool_scale, o_dconv_w, o_dconv_b, o_ln_w, o_ln_b)
    dy, dyb, loss = _mm_out_loss(u1, o_w_out, x1, target, name="out_proj1_loss")
    g_o_w_out = _mm_tn(u1, dyb, name="g_w_out1", out_dtype=bf16)
    dz1, dc1, dpl1, dmc1, sums1 = _mix1_bwd_a(dyb, o_w_out, proj1, c1, mc1, pool_wb, o_pool_scale, o_ln_w, o_ln_b)
    dproj1, g_dconv_w, g_pool_w = _mix1_bwd_b(dc1, dpl1, dmc1, dz1, proj1, o_dconv_w)
    g_o_w_in = _mm_tn(h1, dproj1, name="g_w_in1", out_dtype=bf16)
    d1, d1b, g_o_norm = _mm_nt_rms_bwd(dproj1, o_w_in3, x1, o_norm_w, dy, name="d_h1")
    g_e_w_out = _mm_tn(u0, d1b, name="g_w_out0", out_dtype=bf16)
    dz0, do_a, dsum, dbg, dcv = _mix0_bwd_a(d1b, e_w_out, proj0, o_a, e_conv_w)
    dcg, dhb, g_conv_w = _mix0_bwd_b(dcv, proj0, e_conv_w)
    fuse_a = None if dist is None else (
        [g_e_w_out.reshape(N_DEV, D_MODEL // N_DEV, D_MODEL),
         jnp.moveaxis(g_o_w_in.reshape(D_MODEL, N_DEV, ODD_IN // N_DEV), 1, 0),
         g_o_w_out.reshape(N_DEV, D_MODEL // N_DEV, D_MODEL)], [])
    dq_g, dk_g, dv_g = [], [], []
    for g in range(N_GROUPS):
        if g == 0:
            dqkv = _attn_bwd_local(qk, proj0, do_a, lt, dsum, fuse=fuse_a)
            if dist is not None:
                dqkv, recv_a = dqkv
            dq, dk, dv = dqkv
        else:
            dq, dk, dv = _attn_bwd_dil(qk, proj0, do_a, lt, dsum, g, name=f"attn_bwd{g}")
        dq_g.append(dq)
        dk_g.append(dk)
        dv_g.append(dv)
    dproj0, g_qk_norm = _qk_bwd(dq_g, dk_g, dv_g, proj0, tabs, nw, hm, dbg, dcg, dhb, dz0)
    half = D_MODEL // 2
    g_e_w_in_a = _mm_tn(h0, dproj0, name="g_w_in0a", out_dtype=bf16, chunks=N_DEV, a_cols=(0, half))
    if dist is None:
        g_e_w_in_b = _mm_tn(h0, dproj0, name="g_w_in0b", out_dtype=bf16, chunks=N_DEV, a_cols=(1, half))
    else:
        g_e_w_in_b, recv_b0 = _mm_tn(h0, dproj0, name="g_w_in0b", out_dtype=bf16, chunks=N_DEV, a_cols=(1, half),
                                     fuse=([g_e_w_in_a], []))
    grads = dict(
        e_q_norm_w=g_qk_norm[0:1, :HEAD_DIM], e_k_norm_w=g_qk_norm[1:2, :HEAD_DIM],
        e_conv_w=g_conv_w[:SC_WIDTH], e_w_out=g_e_w_out,
        o_norm_w=g_o_norm, o_w_in=g_o_w_in, o_pool_w=g_pool_w,
        o_pool_scale=sums1[0:1], o_dconv_w=g_dconv_w[:D_CONV], o_dconv_b=sums1[3:4],
        o_ln_w=sums1[1:2], o_ln_b=sums1[2:3], o_w_out=g_o_w_out)
    if dist is None:
        grad_x, _, grads["e_norm_w"] = _mm_nt_rms_bwd(dproj0, e_w_in, x, e_norm_w, d1, name="d_h0", tm=256)
        grads["e_w_in"] = jnp.concatenate([g_e_w_in_a, g_e_w_in_b], axis=1)
        return loss, grad_x, grads
    small_late, offs = _pack_rows([grads[n_] for n_ in _SMALL_LATE])
    grad_x, _, g_e_norm, recv_b = _mm_nt_rms_bwd(dproj0, e_w_in, x, e_norm_w, d1, name="d_h0", tm=256,
                                                 fuse=([g_e_w_in_b], [small_late]))
    recv_c = _exchange([], [jnp.concatenate([g_e_norm.reshape(8, LANES), loss], axis=0)], name="exchange_e_norm_loss")
    recv = dict(e_w_out=[recv_a[0]], o_w_in=[recv_a[1]], o_w_out=[recv_a[2]], e_w_in=[recv_b0[0], recv_b[0]],
                small_late=recv_b[1], e_norm_w=recv_c[0])
    return loss, grad_x, recv, {n_: (off, grads[n_].shape) for n_, off in zip(_SMALL_LATE, offs)}


_MESH_ID = pl.DeviceIdType.MESH
_HBM = pl.BlockSpec(memory_space=pl.ANY)


def _all_gather(arrs, *, name):
    n = len(arrs)

    def body(*refs):
        ins, outs = refs[:n], refs[n:2 * n]
        send_sems, recv_sems, local_sems = refs[2 * n:]
        x, y, c = _place()
        me, sibling = (x, y, c), (x, y, 1 - c)
        chips = [(1 - x, y), (x, 1 - y), (1 - x, 1 - y)]

        def slot(t, px, py, pc):
            return outs[t].at[4 * px + 2 * py + pc]

        def copy(t, k, block, to, src=None):
            dst = slot(t, *block)
            return pltpu.make_async_remote_copy(
                src_ref=dst if src is None else src, dst_ref=dst,
                send_sem=send_sems.at[7 * t + k], recv_sem=recv_sems.at[7 * t + k],
                device_id=to, device_id_type=_MESH_ID)

        mine = [pltpu.make_async_copy(ins[t], slot(t, *me), local_sems.at[t]) for t in range(n)]
        for cp in mine:
            cp.start()
        first = []
        for t in range(n):
            first.append(copy(t, 0, me, sibling, src=ins[t]))
            first += [copy(t, 1 + j, me, (*chip, c), src=ins[t]) for j, chip in enumerate(chips)]
        for cp in first:
            cp.start()
        passed = []
        for j, chip in enumerate(chips):
            for t in range(n):
                copy(t, 1 + j, (*chip, c), me).wait_recv()
                fwd = copy(t, 4 + j, (*chip, c), sibling)
                fwd.start()
                passed.append(fwd)
        for t in range(n):
            copy(t, 0, sibling, me).wait_recv()
            for j, chip in enumerate(chips):
                copy(t, 4 + j, (*chip, 1 - c), me).wait_recv()
        for cp in first + passed:
            cp.wait_send()
        for cp in mine:
            cp.wait()

    return pl.pallas_call(
        body, name=name,
        in_specs=[_HBM] * n, out_specs=[_HBM] * n,
        out_shape=[jax.ShapeDtypeStruct((N_DEV, *a.shape), a.dtype) for a in arrs],
        scratch_shapes=[pltpu.SemaphoreType.DMA((7 * n,)), pltpu.SemaphoreType.DMA((7 * n,)),
                        pltpu.SemaphoreType.DMA((n,))],
    )(*arrs)


def _exchange(chunked, whole, *, name):
    arrs = list(chunked) + list(whole)
    n = len(arrs)

    def body(*refs):
        start, wait = _exchange_plan(refs[:n], refs[n:2 * n], *refs[2 * n:], len(chunked))
        start()
        wait()

    return pl.pallas_call(
        body, name=name, in_specs=[_HBM] * n, out_specs=[_HBM] * n,
        out_shape=_exchange_out_shapes(chunked, whole), scratch_shapes=_exchange_sems(n),
    )(*arrs)


def _adamw(w, g, m, v):
    m2 = ADAM_B1 * m + (1.0 - ADAM_B1) * g
    v2 = ADAM_B2 * v + (1.0 - ADAM_B2) * (g * g)
    m_hat = m2 / (1.0 - ADAM_B1 ** ADAM_STEP)
    v_hat = v2 / (1.0 - ADAM_B2 ** ADAM_STEP)
    delta = -ADAM_LR * (m_hat / (jnp.sqrt(v_hat) + ADAM_EPS) + ADAM_WD * w)
    return delta, m2, v2


def _sum_adamw(parts, w, m, v, *, name):
    R, C = w.shape
    nsplit = len(parts)
    rp = R // nsplit
    tr = _tile(rp, 256)
    npt = rp // tr

    def body(*refs):
        p_refs = refs[:nsplit]
        w_ref, m_ref, v_ref, g_ref, d_ref, nm_ref, nv_ref = refs[nsplit:]
        h = pl.program_id(0)
        g = None
        for i in range(N_DEV):
            pi = p_refs[0][i]
            for q in range(1, nsplit):
                pi = jnp.where(h == q, p_refs[q][i], pi)
            g = pi.astype(f32) if g is None else g + pi.astype(f32)
        g_ref[...] = g
        d_ref[...], nm_ref[...], nv_ref[...] = _adamw(w_ref[...], g, m_ref[...], v_ref[...])

    def part_spec(q):
        return pl.BlockSpec((N_DEV, tr, C), lambda h, i: (0, jnp.where(h == q, i, 0), 0))

    spec = pl.BlockSpec((tr, C), lambda h, i: (h * npt + i, 0))
    return pl.pallas_call(
        body, name=name, grid=(nsplit, npt),
        in_specs=[part_spec(q) for q in range(nsplit)] + [spec, spec, spec],
        out_specs=[spec] * 4, out_shape=[jax.ShapeDtypeStruct((R, C), f32)] * 4,
        compiler_params=_cp(("parallel", "parallel")),
    )(*parts, w, m, v)


def _sum_parts(parts, *, name):
    _, R, C = parts.shape

    def body(p_ref, o_ref):
        g = p_ref[0]
        for i in range(1, N_DEV):
            g = g + p_ref[i]
        o_ref[...] = g

    return pl.pallas_call(body, name=name, out_shape=jax.ShapeDtypeStruct((R, C), f32),
                          compiler_params=pltpu.CompilerParams(vmem_limit_bytes=VMEM_LIMIT))(parts)


def _adamw_small(ws, gs, ms, vs):
    n = len(ws)

    def body(*refs):
        w_r, g_r, m_r, v_r = refs[:n], refs[n:2 * n], refs[2 * n:3 * n], refs[3 * n:4 * n]
        d_r, nm_r, nv_r = refs[4 * n:5 * n], refs[5 * n:6 * n], refs[6 * n:7 * n]
        for t in range(n):
            d_r[t][...], nm_r[t][...], nv_r[t][...] = _adamw(w_r[t][...], g_r[t][...], m_r[t][...], v_r[t][...])

    shapes = [jax.ShapeDtypeStruct(w.shape, f32) for w in ws]
    outs = pl.pallas_call(body, name="adamw_small", out_shape=shapes * 3)(*ws, *gs, *ms, *vs)
    return outs[:n], outs[n:2 * n], outs[2 * n:]


_WEIGHTS = ["e_norm_w", "e_w_in", "e_q_norm_w", "e_k_norm_w", "e_conv_w", "e_w_out", "o_norm_w", "o_w_in", "o_pool_w",
            "o_pool_scale", "o_dconv_w", "o_dconv_b", "o_ln_w", "o_ln_b", "o_w_out"]
_BIG = ["e_w_in", "e_w_out", "o_w_in", "o_w_out"]
_SMALL_SHARDED = ["e_conv_w", "o_norm_w", "o_pool_scale", "o_dconv_w", "o_dconv_b", "o_ln_w", "o_ln_b"]
_SMALL_ALL = ["e_norm_w", "e_q_norm_w", "e_k_norm_w", "e_conv_w", "o_norm_w", "o_pool_w", "o_pool_scale", "o_dconv_w",
              "o_dconv_b", "o_ln_w", "o_ln_b"]


def _pack_rows(pieces):
    rows, offs, r0 = [], [], 0
    for p in pieces:
        flat = p.reshape(-1)
        nr = -(-flat.shape[0] // (8 * LANES)) * 8
        rows.append(jnp.pad(flat, (0, nr * LANES - flat.shape[0])).reshape(nr, LANES))
        offs.append((r0, nr))
        r0 += nr
    return jnp.concatenate(rows, axis=0), offs


def _unpack_rows(buf, off, shape):
    r0, nr = off
    size = int(np.prod(shape))
    return buf[..., r0:r0 + nr, :].reshape(*buf.shape[:-2], nr * LANES)[..., :size].reshape(*buf.shape[:-2], *shape)


def kernel(x, positions, e_norm_w, e_w_in, e_q_norm_w, e_k_norm_w, e_conv_w, e_w_out, o_norm_w, o_w_in, o_pool_w, o_pool_scale, o_dconv_w, o_dconv_b, o_ln_w, o_ln_b, o_w_out, loss_target, m_e_norm_w, m_e_w_in, m_e_q_norm_w, m_e_k_norm_w, m_e_conv_w, m_e_w_out, m_o_norm_w, m_o_w_in, m_o_pool_w, m_o_pool_scale, m_o_dconv_w, m_o_dconv_b, m_o_ln_w, m_o_ln_b, m_o_w_out, v_e_norm_w, v_e_w_in, v_e_q_norm_w, v_e_k_norm_w, v_e_conv_w, v_e_w_out, v_o_norm_w, v_o_w_in, v_o_pool_w, v_o_pool_scale, v_o_dconv_w, v_o_dconv_b, v_o_ln_w, v_o_ln_b, v_o_w_out):
    w = dict(e_norm_w=e_norm_w, e_w_in=e_w_in, e_q_norm_w=e_q_norm_w, e_k_norm_w=e_k_norm_w, e_conv_w=e_conv_w,
             e_w_out=e_w_out, o_norm_w=o_norm_w, o_w_in=o_w_in, o_pool_w=o_pool_w, o_pool_scale=o_pool_scale,
             o_dconv_w=o_dconv_w, o_dconv_b=o_dconv_b, o_ln_w=o_ln_w, o_ln_b=o_ln_b, o_w_out=o_w_out)
    m = dict(e_norm_w=m_e_norm_w, e_w_in=m_e_w_in, e_q_norm_w=m_e_q_norm_w, e_k_norm_w=m_e_k_norm_w, e_conv_w=m_e_conv_w,
             e_w_out=m_e_w_out, o_norm_w=m_o_norm_w, o_w_in=m_o_w_in, o_pool_w=m_o_pool_w, o_pool_scale=m_o_pool_scale,
             o_dconv_w=m_o_dconv_w, o_dconv_b=m_o_dconv_b, o_ln_w=m_o_ln_w, o_ln_b=m_o_ln_b, o_w_out=m_o_w_out)
    v = dict(e_norm_w=v_e_norm_w, e_w_in=v_e_w_in, e_q_norm_w=v_e_q_norm_w, e_k_norm_w=v_e_k_norm_w, e_conv_w=v_e_conv_w,
             e_w_out=v_e_w_out, o_norm_w=v_o_norm_w, o_w_in=v_o_w_in, o_pool_w=v_o_pool_w, o_pool_scale=v_o_pool_scale,
             o_dconv_w=v_o_dconv_w, o_dconv_b=v_o_dconv_b, o_ln_w=v_o_ln_w, o_ln_b=v_o_ln_b, o_w_out=v_o_w_out)
    S = x.shape[1]
    me = 4 * lax.axis_index("x") + 2 * lax.axis_index("y") + lax.axis_index("c")

    small_local, small_offs = _pack_rows([w[n_] for n_ in _SMALL_SHARDED])
    g_e_in, = _all_gather([w["e_w_in"][0].astype(bf16)], name="gather_e_w_in")
    rest_local = [w["e_w_out"][0].astype(bf16), w["o_w_in"][0].astype(bf16), w["o_w_out"][0].astype(bf16), small_local]

    def unpack_rest(gathered):
        g_e_out, g_o_in, g_o_out, g_small = gathered
        full = {}
        for n_, off in zip(_SMALL_SHARDED, small_offs):
            shard = _unpack_rows(g_small, off, w[n_].shape[1:])
            full[n_] = jnp.moveaxis(shard, 0, -2).reshape(*shard.shape[1:-1], N_DEV * shard.shape[-1])
        return dict(
            e_conv_w=full["e_conv_w"], e_w_out=g_e_out.reshape(D_MODEL, D_MODEL), o_norm_w=full["o_norm_w"].reshape(1, D_MODEL),
            o_w_in=jnp.moveaxis(g_o_in, 0, 1).reshape(D_MODEL, ODD_IN), o_pool_scale=full["o_pool_scale"].reshape(1, 512),
            o_dconv_w=full["o_dconv_w"], o_dconv_b=full["o_dconv_b"].reshape(1, 512), o_ln_w=full["o_ln_w"].reshape(1, 512),
            o_ln_b=full["o_ln_b"].reshape(1, 512), o_w_out=g_o_out.reshape(D_MODEL, D_MODEL))

    loss_blk, grad_x, recv, small_where = _local_step(
        x[0], positions.reshape(S, 1), loss_target[0],
        dict(e_norm_w=w["e_norm_w"], e_w_in=g_e_in, e_q_norm_w=w["e_q_norm_w"], e_k_norm_w=w["e_k_norm_w"],
             o_pool_w=w["o_pool_w"][0]),
        dist=(rest_local, unpack_rest))

    out_g, out_d, out_m, out_v = {}, {}, {}, {}
    for n_ in _BIG:
        res = _sum_adamw(recv[n_], w[n_][0], m[n_][0], v[n_][0], name="adamw_" + n_)
        out_g[n_], out_d[n_], out_m[n_], out_v[n_] = [r[None] for r in res]
    small_sum = _sum_parts(recv["small_late"], name="sum_small_grads")
    last_sum = _sum_parts(recv["e_norm_w"], name="sum_e_norm_grad_loss")
    loss = last_sum[8, 0]
    gs = []
    for n_ in _SMALL_ALL:
        if n_ == "e_norm_w":
            gs.append(last_sum[:8].reshape(w[n_].shape))
            continue
        off, shape = small_where[n_]
        gfull = _unpack_rows(small_sum, off, shape)
        if n_ in _SMALL_SHARDED:
            width = w[n_].shape[-1]
            gfull = lax.dynamic_slice_in_dim(gfull, me * width, width, axis=gfull.ndim - 1)
        gs.append(gfull.reshape(w[n_].shape))
    ds, nms, nvs = _adamw_small([w[n_] for n_ in _SMALL_ALL], gs, [m[n_] for n_ in _SMALL_ALL], [v[n_] for n_ in _SMALL_ALL])
    for n_, g_, d_, nm_, nv_ in zip(_SMALL_ALL, gs, ds, nms, nvs):
        out_g[n_], out_d[n_], out_m[n_], out_v[n_] = g_, d_, nm_, nv_

    return (loss, grad_x[None], *[out_g[n_] for n_ in _WEIGHTS], *[out_d[n_] for n_ in _WEIGHTS],
            *[out_m[n_] for n_ in _WEIGHTS], *[out_v[n_] for n_ in _WEIGHTS])
```

```python
import functools

import numpy as np
import jax
import jax.numpy as jnp
from jax import lax
from jax.experimental import pallas as pl
from jax.experimental.pallas import tpu as pltpu

f32 = jnp.float32
bf16 = jnp.bfloat16

D_MODEL = 1024
HEAD_DIM = 64
N_GROUPS = 3
DILATIONS = (1, 4, 16)
QBLK = 128
A_WIDTH = 512
EVEN_IN = 7168
ODD_IN = 2560
POOL_SIZES = (2, 4, 8, 16)
D_CONV = 31
SC_WIDTH = 3
ROT_HALF = 8
ROPE_THETA = 500000.0
EPS = 1e-6
NEG = -1e30
SCALE = HEAD_DIM ** -0.5
N_DEV = 8
LANES = 128
VMEM_LIMIT = 48 * 1024 * 1024

ADAM_LR = 0.001
ADAM_B1 = 0.9
ADAM_B2 = 0.999
ADAM_EPS = 1e-08
ADAM_WD = 0.01
ADAM_STEP = 10

E_Q, E_K, E_V, E_BG, E_CG, E_HB, E_Z = 0, 1536, 3072, 4608, 5120, 5632, 6144
O_UC, O_DA, O_DG, O_Z = 0, 512, 1024, 1536


def _cp(sem):
    return pltpu.CompilerParams(dimension_semantics=sem, vmem_limit_bytes=VMEM_LIMIT)


_HBM_ANY = pl.BlockSpec(memory_space=pl.ANY)


def _sigmoid(z):
    return 1.0 / (1.0 + jnp.exp(-z))


def _tile(n, pref):
    t = pref
    while n % t:
        t //= 2
    return t


def _place():
    return lax.axis_index("x"), lax.axis_index("y"), lax.axis_index("c")


def _exchange_plan(ins, outs, send_sems, recv_sems, local_sems, nc):
    n = len(ins)
    x, y, c = _place()
    me_i = 4 * x + 2 * y + c

    def src(t, dev_i):
        return ins[t].at[dev_i] if t < nc else ins[t]

    def copies(arriving):
        cps = []
        for m in range(1, N_DEV):
            px = 1 - x if m & 4 else x
            py = 1 - y if m & 2 else y
            pc = 1 - c if m & 1 else c
            peer_i = 4 * px + 2 * py + pc
            for t in range(n):
                cps.append(pltpu.make_async_remote_copy(
                    src_ref=src(t, peer_i), dst_ref=outs[t].at[peer_i if arriving else me_i],
                    send_sem=send_sems.at[7 * t + m - 1], recv_sem=recv_sems.at[7 * t + m - 1],
                    device_id=(x, y, c) if arriving else (px, py, pc), device_id_type=pl.DeviceIdType.MESH))
        return cps

    def mine():
        return [pltpu.make_async_copy(src(t, me_i), outs[t].at[me_i], local_sems.at[t]) for t in range(n)]

    def start():
        for cp in mine() + copies(False):
            cp.start()

    def wait():
        for cp in copies(True):
            cp.wait_recv()
        for cp in copies(False):
            cp.wait_send()
        for cp in mine():
            cp.wait()

    return start, wait


def _exchange_sems(n):
    return [pltpu.SemaphoreType.DMA((7 * n,)), pltpu.SemaphoreType.DMA((7 * n,)), pltpu.SemaphoreType.DMA((n,))]


def _exchange_out_shapes(chunked, whole):
    return ([jax.ShapeDtypeStruct(a.shape, a.dtype) for a in chunked]
            + [jax.ShapeDtypeStruct((N_DEV, *a.shape), a.dtype) for a in whole])


def _grid_call(body, *, name, grid, in_specs, out_specs, out_shape, scratch_shapes, sem, args, fuse=None):
    if fuse is None:
        return pl.pallas_call(body, name=name, grid=grid, in_specs=in_specs, out_specs=out_specs, out_shape=out_shape,
                              scratch_shapes=scratch_shapes, compiler_params=_cp(sem))(*args)
    chunked, whole = fuse
    ex = list(chunked) + list(whole)
    n, n_in, n_out, n_sc = len(ex), len(in_specs), len(out_specs), len(scratch_shapes)

    def fused(*refs):
        ins, ex_in = refs[:n_in], refs[n_in:n_in + n]
        outs, ex_out = refs[n_in + n:n_in + n + n_out], refs[n_in + n + n_out:n_in + 2 * n + n_out]
        scratch = refs[n_in + 2 * n + n_out:n_in + 2 * n + n_out + n_sc]
        start, wait = _exchange_plan(ex_in, ex_out, *refs[-3:], len(chunked))
        first = functools.reduce(jnp.logical_and, [pl.program_id(a) == 0 for a in range(len(grid))])
        last = functools.reduce(jnp.logical_and, [pl.program_id(a) == g - 1 for a, g in enumerate(grid)])
        pl.when(first)(start)
        body(*ins, *outs, *scratch)
        pl.when(last)(wait)

    res = pl.pallas_call(
        fused, name=name, grid=grid, in_specs=list(in_specs) + [_HBM_ANY] * n,
        out_specs=list(out_specs) + [_HBM_ANY] * n, out_shape=list(out_shape) + _exchange_out_shapes(chunked, whole),
        scratch_shapes=list(scratch_shapes) + _exchange_sems(n),
        compiler_params=_cp(("arbitrary",) * len(grid)))(*args, *ex)
    return res[:n_out], res[n_out:]


def _load_once(src_hbm, dst_vmem, sem):
    @pl.when(pl.program_id(0) == 0)
    def _():
        cp = pltpu.make_async_copy(src_hbm, dst_vmem, sem)
        cp.start()
        cp.wait()


def _mm_nn_resident(a, b, *, name, tm=256, fuse=None):
    M, K = a.shape
    nch, _, tn = b.shape
    tm = _tile(M, tm)

    def body(a_ref, b_hbm, o_ref, bbuf, sem):
        _load_once(b_hbm, bbuf, sem)
        av = a_ref[...]
        for j in range(nch):
            o_ref[:, j * tn:(j + 1) * tn] = jnp.dot(av, bbuf[j], preferred_element_type=f32)

    out = _grid_call(
        body, name=name, grid=(M // tm,), in_specs=[pl.BlockSpec((tm, K), lambda i: (i, 0)), _HBM_ANY],
        out_specs=[pl.BlockSpec((tm, nch * tn), lambda i: (i, 0))],
        out_shape=[jax.ShapeDtypeStruct((M, nch * tn), f32)],
        scratch_shapes=[pltpu.VMEM(b.shape, b.dtype), pltpu.SemaphoreType.DMA],
        sem=("arbitrary",), args=[a, b], fuse=fuse)
    return out[0] if fuse is None else (out[0][0], out[1])


def _mm_tn(a, b, *, name, out_dtype=f32, tn=512, chunks=None, a_cols=None, fuse=None):
    S, Ka = a.shape
    a_blk = 0
    if a_cols is not None:
        a_blk, Ka = a_cols
    N = b.shape[1]
    ts = _tile(S, 2048)
    ns = S // ts
    if chunks:
        tn = N // chunks
        out_spec = pl.BlockSpec((None, Ka, tn), lambda j, s: (j, 0, 0))
        out_shape = jax.ShapeDtypeStruct((chunks, Ka, tn), out_dtype)
    else:
        tn = _tile(N, tn)
        out_spec = pl.BlockSpec((Ka, tn), lambda j, s: (0, j))
        out_shape = jax.ShapeDtypeStruct((Ka, N), out_dtype)

    def body(a_ref, b_ref, o_ref, acc_ref):
        s = pl.program_id(1)
        part = lax.dot_general(a_ref[...], b_ref[...], (((0,), (0,)), ((), ())), preferred_element_type=f32)

        @pl.when(s == 0)
        def _():
            acc_ref[...] = part

        @pl.when(s > 0)
        def _():
            acc_ref[...] += part

        @pl.when(s == ns - 1)
        def _():
            o_ref[...] = acc_ref[...].astype(out_dtype)

    out = _grid_call(
        body, name=name, grid=(N // tn, ns),
        in_specs=[pl.BlockSpec((ts, Ka), lambda j, s: (s, a_blk)), pl.BlockSpec((ts, tn), lambda j, s: (s, j))],
        out_specs=[out_spec], out_shape=[out_shape],
        scratch_shapes=[pltpu.VMEM((Ka, tn), f32)],
        sem=("parallel", "arbitrary"), args=[a, b], fuse=fuse)
    return out[0] if fuse is None else (out[0][0], out[1])


def _mm_out_loss(u, w, x_res, target, *, name):
    M, K = u.shape
    N = w.shape[1]
    tm = _tile(M, 512)
    nm = M // tm

    def body(u_ref, w_ref, x_ref, t_ref, dy_ref, dyb_ref, loss_ref, acc_ref):
        i = pl.program_id(0)
        y = jnp.dot(u_ref[...], w_ref[...], preferred_element_type=f32) + x_ref[...]
        err = y - t_ref[...]
        dy = err * (1.0 / N)
        dy_ref[...] = dy
        dyb_ref[...] = dy.astype(bf16)
        part = jnp.sum(err * err, axis=0, keepdims=True)

        @pl.when(i == 0)
        def _():
            acc_ref[...] = part

        @pl.when(i > 0)
        def _():
            acc_ref[...] += part

        @pl.when(i == nm - 1)
        def _():
            tot = jnp.sum(acc_ref[...], axis=1, keepdims=True)
            loss_ref[...] = jnp.broadcast_to(tot * (0.5 / N), (8, LANES))

    return pl.pallas_call(
        body, name=name, grid=(nm,),
        in_specs=[pl.BlockSpec((tm, K), lambda i: (i, 0)), pl.BlockSpec((K, N), lambda i: (0, 0)),
                  pl.BlockSpec((tm, N), lambda i: (i, 0)), pl.BlockSpec((tm, N), lambda i: (i, 0))],
        out_specs=[pl.BlockSpec((tm, N), lambda i: (i, 0)), pl.BlockSpec((tm, N), lambda i: (i, 0)),
                   pl.BlockSpec((8, LANES), lambda i: (0, 0))],
        out_shape=[jax.ShapeDtypeStruct((M, N), f32), jax.ShapeDtypeStruct((M, N), bf16),
                   jax.ShapeDtypeStruct((8, LANES), f32)],
        scratch_shapes=[pltpu.VMEM((1, N), f32)],
        compiler_params=_cp(("arbitrary",)),
    )(u, w, x_res, target)


def _out_proj_rms(u, w, res, norm_w, *, name):
    M, K = u.shape
    N = w.shape[1]
    tm = _tile(M, 512)

    def body(u_ref, w_ref, r_ref, nw_ref, y_ref, h_ref):
        y = jnp.dot(u_ref[...], w_ref[...], preferred_element_type=f32) + r_ref[...]
        y_ref[...] = y
        h_ref[...] = (y * lax.rsqrt(jnp.mean(y * y, axis=-1, keepdims=True) + EPS) * nw_ref[...]).astype(bf16)

    row = lambda width: pl.BlockSpec((tm, width), lambda i: (i, 0))
    return pl.pallas_call(
        body, name=name, grid=(M // tm,),
        in_specs=[row(K), pl.BlockSpec((K, N), lambda i: (0, 0)), row(N), pl.BlockSpec((1, N), lambda i: (0, 0))],
        out_specs=[row(N), row(N)],
        out_shape=[jax.ShapeDtypeStruct((M, N), f32), jax.ShapeDtypeStruct((M, N), bf16)],
        compiler_params=_cp(("parallel",)),
    )(u, w, res, norm_w)


def _mm_nt_rms_bwd(a, b, x, w, res, *, name, tm=512, fuse=None):
    M, K = a.shape
    nch, N, tk = b.shape
    tm = _tile(M, tm)

    def body(a_ref, b_hbm, x_ref, w_ref, res_ref, dx_ref, dxb_ref, gw_ref, bbuf, sem):
        i = pl.program_id(0)
        _load_once(b_hbm, bbuf, sem)
        dh_v = None
        for k in range(nch):
            part = lax.dot_general(a_ref[:, k * tk:(k + 1) * tk], bbuf[k], (((1,), (1,)), ((), ())),
                                   preferred_element_type=f32)
            dh_v = part if dh_v is None else dh_v + part
        xv = x_ref[...]
        r = lax.rsqrt(jnp.mean(xv * xv, axis=-1, keepdims=True) + EPS)
        xn = xv * r
        dxn = dh_v * w_ref[...]
        dx = r * (dxn - xn * jnp.mean(dxn * xn, axis=-1, keepdims=True)) + res_ref[...]
        dx_ref[...] = dx
        dxb_ref[...] = dx.astype(bf16)
        part = jnp.sum(dh_v * xn, axis=0, keepdims=True)

        @pl.when(i == 0)
        def _():
            gw_ref[...] = part

        @pl.when(i > 0)
        def _():
            gw_ref[...] += part

    row = lambda width: pl.BlockSpec((tm, width), lambda i: (i, 0))
    vec = pl.BlockSpec((1, N), lambda i: (0, 0))
    out = _grid_call(
        body, name=name, grid=(M // tm,),
        in_specs=[row(K), _HBM_ANY, row(N), vec, row(N)],
        out_specs=[row(N), row(N), vec],
        out_shape=[jax.ShapeDtypeStruct((M, N), f32), jax.ShapeDtypeStruct((M, N), bf16), jax.ShapeDtypeStruct((1, N), f32)],
        scratch_shapes=[pltpu.VMEM(b.shape, b.dtype), pltpu.SemaphoreType.DMA],
        sem=("arbitrary",), args=[a, b, x, w, res], fuse=fuse)
    return out if fuse is None else (*out[0], out[1])


_INV_FREQ = [float(v) for v in (np.float32(ROPE_THETA) ** (-np.arange(ROT_HALF, dtype=np.float32) / np.float32(ROT_HALF))).astype(np.float32)]


def _rope_tables(pos_col):
    S = pos_col.shape[0]
    tm = _tile(S, 1024)

    def body(p_ref, c_ref, s1_ref, s2_ref):
        lane = lax.broadcasted_iota(jnp.int32, (tm, LANES), 1)
        lm = lane % HEAD_DIM
        fi = lm % ROT_HALF
        inv = jnp.zeros((tm, LANES), f32)
        for k in range(ROT_HALF):
            inv = jnp.where(fi == k, _INV_FREQ[k], inv)
        ang = p_ref[...].astype(f32) * inv
        cs = jnp.cos(ang)
        sn = jnp.sin(ang)
        c_ref[...] = jnp.where(lm < 2 * ROT_HALF, cs, 1.0)
        s1_ref[...] = jnp.where((lm >= ROT_HALF) & (lm < 2 * ROT_HALF), sn, 0.0)
        s2_ref[...] = jnp.where(lm < ROT_HALF, -sn, 0.0)

    spec = pl.BlockSpec((tm, LANES), lambda i: (i, 0))
    return pl.pallas_call(
        body, name="rope_tables", grid=(S // tm,),
        in_specs=[pl.BlockSpec((tm, 1), lambda i: (i, 0))],
        out_specs=[spec, spec, spec],
        out_shape=[jax.ShapeDtypeStruct((S, LANES), f32)] * 3,
        compiler_params=_cp(("parallel",)),
    )(pos_col)


def _head_mean(v, m):
    hi = v.astype(bf16)
    lo = (v - hi.astype(f32)).astype(bf16)
    return jnp.dot(hi, m, preferred_element_type=f32) + jnp.dot(lo, m, preferred_element_type=f32)


def _head_mean_matrix():
    i = np.arange(LANES)
    return jnp.asarray(((i[:, None] // HEAD_DIM) == (i[None, :] // HEAD_DIM)).astype(np.float32) / HEAD_DIM, dtype=bf16)


def _in_proj0(x, norm_w, b, tabs, nw, hm, *, fuse=None):
    M, K = x.shape
    nch, _, tn = b.shape
    tm = _tile(M, 256)

    def body(x_ref, w_ref, b_hbm, c_ref, s1_ref, s2_ref, nw_ref, m_ref, o_ref, qk_ref, h_ref, bbuf, sem):
        _load_once(b_hbm, bbuf, sem)
        xv = x_ref[...]
        av = (xv * lax.rsqrt(jnp.mean(xv * xv, axis=-1, keepdims=True) + EPS) * w_ref[...]).astype(bf16)
        h_ref[...] = av
        c, s1, s2, m = c_ref[...], s1_ref[...], s2_ref[...], m_ref[...]
        for j in range(nch):
            res = jnp.dot(av, bbuf[j], preferred_element_type=f32)
            o_ref[:, j * tn:(j + 1) * tn] = res
            for p in range(tn // LANES):
                col = j * tn + p * LANES
                if col >= E_V:
                    continue
                w = nw_ref[0:1, :] if col < E_K else nw_ref[1:2, :]
                t = res[:, p * LANES:(p + 1) * LANES]
                that = t * lax.rsqrt(_head_mean(t * t, m) + EPS) * w
                qk_ref[:, col:col + LANES] = (
                    that * c + pltpu.roll(that, ROT_HALF, axis=1) * s1 + pltpu.roll(that, LANES - ROT_HALF, axis=1) * s2)

    tab = pl.BlockSpec((tm, LANES), lambda i: (i, 0))
    out = _grid_call(
        body, name="in_proj0", grid=(M // tm,),
        in_specs=[pl.BlockSpec((tm, K), lambda i: (i, 0)), pl.BlockSpec((1, K), lambda i: (0, 0)), _HBM_ANY, tab, tab, tab,
                  pl.BlockSpec((2, LANES), lambda i: (0, 0)), pl.BlockSpec((LANES, LANES), lambda i: (0, 0))],
        out_specs=[pl.BlockSpec((tm, nch * tn), lambda i: (i, 0)), pl.BlockSpec((tm, E_V), lambda i: (i, 0)),
                   pl.BlockSpec((tm, K), lambda i: (i, 0))],
        out_shape=[jax.ShapeDtypeStruct((M, nch * tn), f32), jax.ShapeDtypeStruct((M, E_V), f32),
                   jax.ShapeDtypeStruct((M, K), bf16)],
        scratch_shapes=[pltpu.VMEM(b.shape, b.dtype), pltpu.SemaphoreType.DMA],
        sem=("arbitrary",), args=[x, norm_w, b, *tabs, nw, hm], fuse=fuse)
    return out if fuse is None else (*out[0], out[1])


def _key_geometry(nparts):
    qr = QBLK // nparts
    rho = lax.broadcasted_iota(jnp.int32, (2 * QBLK, 2 * QBLK), 0) % QBLK
    kap = lax.broadcasted_iota(jnp.int32, (2 * QBLK, 2 * QBLK), 1)
    n_q = QBLK + nparts * (rho % qr) + rho // qr
    tt = kap % (2 * qr)
    n_k = nparts * tt + kap // (2 * qr)
    dist = n_q - n_k
    return (dist >= 0) & (dist <= QBLK), (tt < qr).astype(jnp.int32)


def _stack_heads(t, lo):
    zero = jnp.zeros_like(t)
    return jnp.concatenate([jnp.where(lo, t, zero), jnp.where(lo, zero, t)], axis=0)


def _attn_block_fwd(qb, kcat, vcat, mask, lo):
    s = lax.dot_general(_stack_heads(qb, lo), kcat, (((1,), (1,)), ((), ())), preferred_element_type=f32) * SCALE
    s = jnp.where(mask, s, NEG)
    mx = jnp.max(s, axis=-1, keepdims=True)
    pexp = jnp.exp(s - mx)
    den = jnp.sum(pexp, axis=-1, keepdims=True)
    pn = (pexp * (1.0 / den)).astype(bf16)
    o2 = jnp.dot(pn, vcat, preferred_element_type=f32)
    lse2 = jnp.broadcast_to(mx + jnp.log(den), (2 * QBLK, LANES))
    return jnp.where(lo, o2[:QBLK], o2[QBLK:]), jnp.where(lo, lse2[:QBLK], lse2[QBLK:])


def _attn_block_bwd(qb, dob, kcat, vcat, lt, ds, mask, lo):
    lt_sw = pltpu.roll(lt, HEAD_DIM, axis=1)
    ds_sw = pltpu.roll(ds, HEAD_DIM, axis=1)
    lt2 = jnp.concatenate([jnp.where(lo, lt, lt_sw), jnp.where(lo, lt_sw, lt)], axis=0)
    ds2 = jnp.concatenate([jnp.where(lo, ds, ds_sw), jnp.where(lo, ds_sw, ds)], axis=0)
    q2 = _stack_heads(qb, lo)
    do2 = _stack_heads(dob, lo)
    s = lax.dot_general(q2, kcat, (((1,), (1,)), ((), ())), preferred_element_type=f32) * SCALE
    s = jnp.where(mask, s, NEG)
    prob = jnp.exp(s - jnp.concatenate([lt2, lt2], axis=1))
    dp = lax.dot_general(do2, vcat, (((1,), (1,)), ((), ())), preferred_element_type=f32)
    dsb = (prob * (dp - jnp.concatenate([ds2, ds2], axis=1)) * SCALE).astype(bf16)
    dq2 = jnp.dot(dsb, kcat, preferred_element_type=f32)
    dk = lax.dot_general(dsb, q2, (((0,), (0,)), ((), ())), preferred_element_type=f32)
    dv = lax.dot_general(prob.astype(bf16), do2, (((0,), (0,)), ((), ())), preferred_element_type=f32)
    return jnp.where(lo, dq2[:QBLK], dq2[QBLK:]), dk, dv


ATT_ROWS = 1024
ATT_UNROLL = 4


def _attn_fwd_local(qk, proj):
    S = qk.shape[0]
    tr = _tile(S, ATT_ROWS)
    lw = 4 * LANES
    nb = tr // QBLK

    def body(q_ref, k_ref, kh_ref, v_ref, vh_ref, o_ref, lse_ref, kbuf, vbuf):
        j = pl.program_id(0)
        kbuf[0:QBLK, :] = jnp.where(j > 0, kh_ref[...], 0.0)
        kbuf[QBLK:, :] = k_ref[...]
        vbuf[0:QBLK, :] = jnp.where(j > 0, vh_ref[...], 0.0)
        vbuf[QBLK:, :] = v_ref[...]
        band, is_prev = _key_geometry(1)
        lo = lax.broadcasted_iota(jnp.int32, (QBLK, LANES), 1) < HEAD_DIM

        def blk(c, carry):
            r0 = pl.multiple_of(c * QBLK, QBLK)
            first = jnp.where((c == 0) & (j == 0), 1, 0)
            mask = band & (is_prev * first == 0)
            for pp in range(lw // LANES):
                lanes = slice(pp * LANES, (pp + 1) * LANES)
                o, lse = _attn_block_fwd(q_ref[pl.ds(r0, QBLK), lanes].astype(bf16),
                                         kbuf[pl.ds(r0, 2 * QBLK), lanes].astype(bf16),
                                         vbuf[pl.ds(r0, 2 * QBLK), lanes].astype(bf16), mask, lo)
                o_ref[pl.ds(r0, QBLK), lanes] = o
                lse_ref[pl.ds(r0, QBLK), lanes] = lse
            return carry

        lax.fori_loop(0, nb, blk, 0, unroll=ATT_UNROLL)

    def halo(col):
        return pl.BlockSpec((QBLK, lw), lambda j, l: (jnp.maximum(j * nb - 1, 0), col + l))

    def tile(col):
        return pl.BlockSpec((tr, lw), lambda j, l: (j, col + l))

    return pl.pallas_call(
        body, name="attn_fwd0", grid=(S // tr, A_WIDTH // lw),
        in_specs=[tile(E_Q // lw), tile(E_K // lw), halo(E_K // lw), tile(E_V // lw), halo(E_V // lw)],
        out_specs=[tile(0), tile(0)],
        out_shape=[jax.ShapeDtypeStruct((S, A_WIDTH), f32)] * 2,
        scratch_shapes=[pltpu.VMEM((QBLK + tr, lw), f32)] * 2,
        compiler_params=_cp(("parallel", "parallel")),
    )(qk, qk, qk, proj, proj)


def _attn_bwd_local(qk, proj, do_a, lt, dsum, fuse=None):
    S = qk.shape[0]
    tr = _tile(S, ATT_ROWS)
    lw = 2 * LANES
    nb = tr // QBLK
    nt = S // tr

    def body(q_ref, qn_ref, do_ref, don_ref, lt_ref, ltn_ref, ds_ref, dsn_ref, k_ref, kh_ref, v_ref, vh_ref,
             dq_ref, dk_ref, dv_ref, kbuf, vbuf, dkbuf, dvbuf):
        j = pl.program_id(0)
        zeros = jnp.zeros((QBLK, lw), f32)
        kbuf[0:QBLK, :] = jnp.where(j > 0, kh_ref[...], 0.0)
        kbuf[pl.ds(QBLK, tr), :] = k_ref[...]
        kbuf[pl.ds(QBLK + tr, QBLK), :] = zeros
        vbuf[0:QBLK, :] = jnp.where(j > 0, vh_ref[...], 0.0)
        vbuf[pl.ds(QBLK, tr), :] = v_ref[...]
        vbuf[pl.ds(QBLK + tr, QBLK), :] = zeros
        dkbuf[...] = jnp.zeros_like(dkbuf)
        dvbuf[...] = jnp.zeros_like(dvbuf)
        band, is_prev = _key_geometry(1)
        lo = lax.broadcasted_iota(jnp.int32, (QBLK, LANES), 1) < HEAD_DIM

        def blk(c, carry):
            r0 = pl.multiple_of(c * QBLK, QBLK)
            first = jnp.where((c == 0) & (j == 0), 1, 0)
            mask = band & (is_prev * first == 0)
            for pp in range(lw // LANES):
                lanes = slice(pp * LANES, (pp + 1) * LANES)
                dq, dk, dv = _attn_block_bwd(
                    q_ref[pl.ds(r0, QBLK), lanes].astype(bf16), do_ref[pl.ds(r0, QBLK), lanes].astype(bf16),
                    kbuf[pl.ds(r0, 2 * QBLK), lanes].astype(bf16), vbuf[pl.ds(r0, 2 * QBLK), lanes].astype(bf16),
                    lt_ref[pl.ds(r0, QBLK), lanes], ds_ref[pl.ds(r0, QBLK), lanes], mask, lo)
                dq_ref[pl.ds(r0, QBLK), lanes] = dq
                dkbuf[pl.ds(r0, 2 * QBLK), lanes] += dk
                dvbuf[pl.ds(r0, 2 * QBLK), lanes] += dv
            return carry

        lax.fori_loop(0, nb, blk, 0, unroll=ATT_UNROLL)

        @pl.when(j < nt - 1)
        def _():
            mask = band & (is_prev == 1)
            for pp in range(lw // LANES):
                lanes = slice(pp * LANES, (pp + 1) * LANES)
                _, dk, dv = _attn_block_bwd(
                    qn_ref[:, lanes].astype(bf16), don_ref[:, lanes].astype(bf16),
                    kbuf[pl.ds(tr, 2 * QBLK), lanes].astype(bf16), vbuf[pl.ds(tr, 2 * QBLK), lanes].astype(bf16),
                    ltn_ref[:, lanes], dsn_ref[:, lanes], mask, lo)
                dkbuf[pl.ds(tr, 2 * QBLK), lanes] += dk
                dvbuf[pl.ds(tr, 2 * QBLK), lanes] += dv

        dk_ref[...] = dkbuf[pl.ds(QBLK, tr), :]
        dv_ref[...] = dvbuf[pl.ds(QBLK, tr), :]

    def prev_halo(col):
        return pl.BlockSpec((QBLK, lw), lambda j, l: (jnp.maximum(j * nb - 1, 0), col + l))

    def next_halo(col):
        return pl.BlockSpec((QBLK, lw), lambda j, l: (jnp.minimum((j + 1) * nb, S // QBLK - 1), col + l))

    def tile(col):
        return pl.BlockSpec((tr, lw), lambda j, l: (j, col + l))

    return _grid_call(
        body, name="attn_bwd0", grid=(nt, A_WIDTH // lw),
        in_specs=[tile(E_Q // lw), next_halo(E_Q // lw), tile(0), next_halo(0), tile(0), next_halo(0), tile(0), next_halo(0),
                  tile(E_K // lw), prev_halo(E_K // lw), tile(E_V // lw), prev_halo(E_V // lw)],
        out_specs=[tile(0)] * 3,
        out_shape=[jax.ShapeDtypeStruct((S, A_WIDTH), f32)] * 3,
        scratch_shapes=[pltpu.VMEM((tr + 2 * QBLK, lw), f32)] * 4,
        sem=("parallel", "parallel"), args=[qk, qk, do_a, do_a, lt, lt, dsum, dsum, qk, qk, proj, proj], fuse=fuse)


def _stream_view(a, d):
    S, W = a.shape
    return a.reshape(S // 8, 8, W) if d == 4 else a.reshape(S // 16, 2, 8, W)


def _stream_ref(ref, d, r, part, col, lw):
    n = ref.shape[0]
    if d == 4:
        return ref.at[pl.ds(0, n), r + 4 * part, pl.ds(col, lw)]
    return ref.at[pl.ds(0, n), r // 8, r % 8, pl.ds(col, lw)]


def _stream_geometry(S, d):
    nparts = 2 if d == 4 else 1
    rows = S // (d * nparts)
    return nparts, rows, QBLK // nparts


def _attn_fwd_dil(qk, proj, g, *, name):
    S = qk.shape[0]
    d = DILATIONS[g]
    nparts, rows, qr = _stream_geometry(S, d)
    nb = rows // qr
    lw = 2 * LANES if d == 4 else 4 * LANES
    nlg = A_WIDTH // lw
    nitems = d * nlg
    ins = ((0, E_Q + A_WIDTH * g, 0), (0, E_K + A_WIDTH * g, qr), (1, E_V + A_WIDTH * g, qr))

    def body(qk_hbm, pj_hbm, o_hbm, l_hbm, qbuf, kbuf, vbuf, obuf, lbuf, in_sems, out_sems):
        i = pl.program_id(0)
        slot = i % 2
        hbm_in = (qk_hbm, pj_hbm)
        bufs_in = (qbuf, kbuf, vbuf)

        def in_copies(item, sl):
            r, lg = item // nlg, item % nlg
            cps = []
            for a in range(nparts):
                for t, (src, col, pad) in enumerate(ins):
                    cps.append(pltpu.make_async_copy(
                        _stream_ref(hbm_in[src], d, r, a, pl.multiple_of(col + lw * lg, LANES), lw),
                        bufs_in[t].at[sl, a, pl.ds(pad, rows), :], in_sems.at[sl, 3 * a + t]))
            return cps

        def out_copies(item, sl):
            r, lg = item // nlg, item % nlg
            cps = []
            for a in range(nparts):
                for t, (buf, dst) in enumerate(((obuf, o_hbm), (lbuf, l_hbm))):
                    cps.append(pltpu.make_async_copy(
                        buf.at[sl, a], _stream_ref(dst, d, r, a, pl.multiple_of(lw * lg, LANES), lw),
                        out_sems.at[sl, 2 * a + t]))
            return cps

        @pl.when(i == 0)
        def _():
            for sl in range(2):
                for a in range(nparts):
                    kbuf[sl, a, 0:qr, :] = jnp.zeros((qr, lw), f32)
                    vbuf[sl, a, 0:qr, :] = jnp.zeros((qr, lw), f32)
            for cp in in_copies(0, 0):
                cp.start()

        @pl.when(i + 1 < nitems)
        def _():
            for cp in in_copies(i + 1, 1 - slot):
                cp.start()

        for cp in in_copies(i, slot):
            cp.wait()

        @pl.when(i >= 2)
        def _():
            for cp in out_copies(i - 2, slot):
                cp.wait()

        band, is_prev = _key_geometry(nparts)
        lo = lax.broadcasted_iota(jnp.int32, (QBLK, LANES), 1) < HEAD_DIM

        def blk(c, carry):
            r0 = pl.multiple_of(c * qr, qr)
            mask = band & (is_prev * jnp.where(c == 0, 1, 0) == 0)
            for pp in range(lw // LANES):
                lanes = slice(pp * LANES, (pp + 1) * LANES)
                qb = jnp.concatenate([qbuf[slot, a, pl.ds(r0, qr), lanes] for a in range(nparts)], axis=0).astype(bf16)
                kcat = jnp.concatenate([kbuf[slot, a, pl.ds(r0, 2 * qr), lanes] for a in range(nparts)], axis=0).astype(bf16)
                vcat = jnp.concatenate([vbuf[slot, a, pl.ds(r0, 2 * qr), lanes] for a in range(nparts)], axis=0).astype(bf16)
                o, lse = _attn_block_fwd(qb, kcat, vcat, mask, lo)
                for a in range(nparts):
                    obuf[slot, a, pl.ds(r0, qr), lanes] = o[a * qr:(a + 1) * qr]
                    lbuf[slot, a, pl.ds(r0, qr), lanes] = lse[a * qr:(a + 1) * qr]
            return carry

        lax.fori_loop(0, nb, blk, 0, unroll=min(nb, 2 * ATT_UNROLL))

        for cp in out_copies(i, slot):
            cp.start()

        @pl.when(i == nitems - 1)
        def _():
            for cp in out_copies(i - 1, 1 - slot) + out_copies(i, slot):
                cp.wait()

    vshape = (S // 8, 8, A_WIDTH) if d == 4 else (S // 16, 2, 8, A_WIDTH)
    o, lse = pl.pallas_call(
        body, name=name, grid=(nitems,),
        in_specs=[_HBM_ANY, _HBM_ANY], out_specs=[_HBM_ANY, _HBM_ANY],
        out_shape=[jax.ShapeDtypeStruct(vshape, f32)] * 2,
        scratch_shapes=[pltpu.VMEM((2, nparts, rows, lw), f32), pltpu.VMEM((2, nparts, qr + rows, lw), f32),
                        pltpu.VMEM((2, nparts, qr + rows, lw), f32), pltpu.VMEM((2, nparts, rows, lw), f32),
                        pltpu.VMEM((2, nparts, rows, lw), f32),
                        pltpu.SemaphoreType.DMA((2, 3 * nparts)), pltpu.SemaphoreType.DMA((2, 2 * nparts))],
        compiler_params=_cp(("arbitrary",)),
    )(_stream_view(qk, d), _stream_view(proj, d))
    return o.reshape(S, A_WIDTH), lse.reshape(S, A_WIDTH)


def _attn_bwd_dil(qk, proj, do_a, lt, dsum, g, *, name):
    S = qk.shape[0]
    d = DILATIONS[g]
    nparts, rows, qr = _stream_geometry(S, d)
    nb = rows // qr
    lw = LANES if d == 4 else 4 * LANES
    nlg = A_WIDTH // lw
    nitems = d * nlg
    ins = ((0, E_Q + A_WIDTH * g, 0), (2, 0, 0), (3, 0, 0), (4, 0, 0), (0, E_K + A_WIDTH * g, qr), (1, E_V + A_WIDTH * g, qr))
    n_in = len(ins)

    def body(qk_hbm, pj_hbm, do_hbm, lt_hbm, ds_hbm, dq_hbm, dk_hbm, dv_hbm,
             qbuf, dobuf, ltbuf, dsbuf, kbuf, vbuf, dqbuf, dkbuf, dvbuf, in_sems, out_sems):
        i = pl.program_id(0)
        slot = i % 2
        hbm_in = (qk_hbm, pj_hbm, do_hbm, lt_hbm, ds_hbm)
        bufs_in = (qbuf, dobuf, ltbuf, dsbuf, kbuf, vbuf)

        def in_copies(item, sl):
            r, lg = item // nlg, item % nlg
            cps = []
            for a in range(nparts):
                for t, (src, col, pad) in enumerate(ins):
                    cps.append(pltpu.make_async_copy(
                        _stream_ref(hbm_in[src], d, r, a, pl.multiple_of(col + lw * lg, LANES), lw),
                        bufs_in[t].at[sl, a, pl.ds(pad, rows), :], in_sems.at[sl, n_in * a + t]))
            return cps

        def out_copies(item, sl):
            r, lg = item // nlg, item % nlg
            cps = []
            for a in range(nparts):
                for t, (buf, dst, pad) in enumerate(((dqbuf, dq_hbm, 0), (dkbuf, dk_hbm, qr), (dvbuf, dv_hbm, qr))):
                    cps.append(pltpu.make_async_copy(
                        buf.at[sl, a, pl.ds(pad, rows), :],
                        _stream_ref(dst, d, r, a, pl.multiple_of(lw * lg, LANES), lw), out_sems.at[sl, 3 * a + t]))
            return cps

        @pl.when(i == 0)
        def _():
            for sl in range(2):
                for a in range(nparts):
                    kbuf[sl, a, 0:qr, :] = jnp.zeros((qr, lw), f32)
                    vbuf[sl, a, 0:qr, :] = jnp.zeros((qr, lw), f32)
            for cp in in_copies(0, 0):
                cp.start()

        @pl.when(i + 1 < nitems)
        def _():
            for cp in in_copies(i + 1, 1 - slot):
                cp.start()

        for cp in in_copies(i, slot):
            cp.wait()

        @pl.when(i >= 2)
        def _():
            for cp in out_copies(i - 2, slot):
                cp.wait()

        for a in range(nparts):
            dkbuf[slot, a] = jnp.zeros((qr + rows, lw), f32)
            dvbuf[slot, a] = jnp.zeros((qr + rows, lw), f32)
        band, is_prev = _key_geometry(nparts)
        lo = lax.broadcasted_iota(jnp.int32, (QBLK, LANES), 1) < HEAD_DIM

        def blk(c, carry):
            r0 = pl.multiple_of(c * qr, qr)
            mask = band & (is_prev * jnp.where(c == 0, 1, 0) == 0)

            def rows_of(buf, n, lanes):
                return jnp.concatenate([buf[slot, a, pl.ds(r0, n), lanes] for a in range(nparts)], axis=0)

            for pp in range(lw // LANES):
                lanes = slice(pp * LANES, (pp + 1) * LANES)
                dq, dk, dv = _attn_block_bwd(
                    rows_of(qbuf, qr, lanes).astype(bf16), rows_of(dobuf, qr, lanes).astype(bf16),
                    rows_of(kbuf, 2 * qr, lanes).astype(bf16), rows_of(vbuf, 2 * qr, lanes).astype(bf16),
                    rows_of(ltbuf, qr, lanes), rows_of(dsbuf, qr, lanes), mask, lo)
                for a in range(nparts):
                    dqbuf[slot, a, pl.ds(r0, qr), lanes] = dq[a * qr:(a + 1) * qr]
                    dkbuf[slot, a, pl.ds(r0, 2 * qr), lanes] += dk[2 * a * qr:2 * (a + 1) * qr]
                    dvbuf[slot, a, pl.ds(r0, 2 * qr), lanes] += dv[2 * a * qr:2 * (a + 1) * qr]
            return carry

        lax.fori_loop(0, nb, blk, 0, unroll=ATT_UNROLL)

        for cp in out_copies(i, slot):
            cp.start()

        @pl.when(i == nitems - 1)
        def _():
            for cp in out_copies(i - 1, 1 - slot) + out_copies(i, slot):
                cp.wait()

    vshape = (S // 8, 8, A_WIDTH) if d == 4 else (S // 16, 2, 8, A_WIDTH)
    plain = pltpu.VMEM((2, nparts, rows, lw), f32)
    padded = pltpu.VMEM((2, nparts, qr + rows, lw), f32)
    outs = pl.pallas_call(
        body, name=name, grid=(nitems,),
        in_specs=[_HBM_ANY] * 5, out_specs=[_HBM_ANY] * 3,
        out_shape=[jax.ShapeDtypeStruct(vshape, f32)] * 3,
        scratch_shapes=[plain, plain, plain, plain, padded, padded, plain, padded, padded,
                        pltpu.SemaphoreType.DMA((2, n_in * nparts)), pltpu.SemaphoreType.DMA((2, 3 * nparts))],
        compiler_params=_cp(("arbitrary",)),
    )(*[_stream_view(a, d) for a in (qk, proj, do_a, lt, dsum)])
    return [o.reshape(S, A_WIDTH) for o in outs]


def _prev_halo(tm, h, col):
    return pl.BlockSpec((h, 512), lambda i: (jnp.maximum(i * (tm // h) - 1, 0), col))


def _next_halo(tm, h, col, S):
    return pl.BlockSpec((h, 512), lambda i: (jnp.minimum((i + 1) * (tm // h), S // h - 1), col))


def _mix0_fwd(o_g, lse_g, proj, conv_w):
    S = proj.shape[0]
    tm = _tile(S, 256)

    def body(o0, o1, o2, l0, l1, l2, bg_ref, cg_ref, hb_ref, z_ref, cgh_ref, hbh_ref, w_ref,
             u_ref, oa_ref, lt_ref, tbuf):
        i = pl.program_id(0)
        ls = [l0[...], l1[...], l2[...]]
        mx = jnp.maximum(jnp.maximum(ls[0], ls[1]), ls[2])
        es = [jnp.exp(l - mx) for l in ls]
        tot = es[0] + es[1] + es[2]
        lt_ref[...] = mx + jnp.log(tot)
        inv = 1.0 / tot
        z = z_ref[...]
        sz = z * _sigmoid(z)
        oa = (es[0] * inv) * o0[...] + (es[1] * inv) * o1[...] + (es[2] * inv) * o2[...]
        oa_ref[...] = oa
        u_ref[:, :A_WIDTH] = (oa * sz[:, :A_WIDTH]).astype(bf16)
        t = cg_ref[...] * hb_ref[...]
        tbuf[0:8, :] = jnp.where(i > 0, cgh_ref[...] * hbh_ref[...], 0.0)
        tbuf[8:, :] = t
        cv = w_ref[2:3, :] * t + w_ref[1:2, :] * tbuf[pl.ds(7, tm), :] + w_ref[0:1, :] * tbuf[pl.ds(6, tm), :]
        u_ref[:, A_WIDTH:] = (bg_ref[...] * cv * sz[:, A_WIDTH:]).astype(bf16)

    row = lambda w, c: pl.BlockSpec((tm, w), lambda i: (i, c))
    return pl.pallas_call(
        body, name="mix0_fwd", grid=(S // tm,),
        in_specs=[row(512, 0)] * 6
        + [row(512, E_BG // 512), row(512, E_CG // 512), row(512, E_HB // 512), row(1024, E_Z // 1024),
           _prev_halo(tm, 8, E_CG // 512), _prev_halo(tm, 8, E_HB // 512), pl.BlockSpec((SC_WIDTH, 512), lambda i: (0, 0))],
        out_specs=[row(1024, 0), row(512, 0), row(512, 0)],
        out_shape=[jax.ShapeDtypeStruct((S, D_MODEL), bf16), jax.ShapeDtypeStruct((S, A_WIDTH), f32),
                   jax.ShapeDtypeStruct((S, A_WIDTH), f32)],
        scratch_shapes=[pltpu.VMEM((tm + 8, 512), f32)],
        compiler_params=_cp(("parallel",)),
    )(*o_g, *lse_g, proj, proj, proj, proj, proj, proj, conv_w)


def _dsilu(z, sg):
    return sg * (1.0 + z * (1.0 - sg))


def _d_gate_in(dy_ref, wo_ref):
    return lax.dot_general(dy_ref[...], wo_ref[...], (((1,), (1,)), ((), ())), preferred_element_type=f32)


def _mix0_bwd_a(dy, w_out, proj, o_a, conv_w):
    S = proj.shape[0]
    tm = _tile(S, 256)

    def body(dy_ref, wo_ref, bg_ref, cg_ref, hb_ref, z_ref, cgh_ref, hbh_ref, oa_ref, w_ref,
             dz_ref, doa_ref, ds_ref, dbg_ref, dcv_ref, tbuf):
        i = pl.program_id(0)
        lo = lax.broadcasted_iota(jnp.int32, (tm, LANES), 1) < HEAD_DIM
        z = z_ref[...]
        sg = _sigmoid(z)
        sz = z * sg
        dsz = _dsilu(z, sg)
        du_v = _d_gate_in(dy_ref, wo_ref)
        t = cg_ref[...] * hb_ref[...]
        tbuf[0:8, :] = jnp.where(i > 0, cgh_ref[...] * hbh_ref[...], 0.0)
        tbuf[8:, :] = t
        cv = w_ref[2:3, :] * t + w_ref[1:2, :] * tbuf[pl.ds(7, tm), :] + w_ref[0:1, :] * tbuf[pl.ds(6, tm), :]
        bg = bg_ref[...]
        oa = oa_ref[...]
        dz_ref[:, :A_WIDTH] = (du_v[:, :A_WIDTH] * oa * dsz[:, :A_WIDTH]).astype(bf16)
        dz_ref[:, A_WIDTH:] = (du_v[:, A_WIDTH:] * (bg * cv) * dsz[:, A_WIDTH:]).astype(bf16)
        doa = du_v[:, :A_WIDTH] * sz[:, :A_WIDTH]
        dyb = du_v[:, A_WIDTH:] * sz[:, A_WIDTH:]
        doa_ref[...] = doa
        dbg_ref[...] = (dyb * cv).astype(bf16)
        dcv_ref[...] = dyb * bg
        prod = doa * oa
        for p in range(4):
            pp = prod[:, p * LANES:(p + 1) * LANES]
            sa = jnp.sum(jnp.where(lo, pp, 0.0), axis=-1, keepdims=True)
            sb = jnp.sum(jnp.where(lo, 0.0, pp), axis=-1, keepdims=True)
            ds_ref[:, p * LANES:(p + 1) * LANES] = jnp.where(lo, sa, sb)

    row = lambda w, c: pl.BlockSpec((tm, w), lambda i: (i, c))
    return pl.pallas_call(
        body, name="mix0_bwd_a", grid=(S // tm,),
        in_specs=[row(1024, 0), pl.BlockSpec((D_MODEL, D_MODEL), lambda i: (0, 0)),
                  row(512, E_BG // 512), row(512, E_CG // 512), row(512, E_HB // 512), row(1024, E_Z // 1024),
                  _prev_halo(tm, 8, E_CG // 512), _prev_halo(tm, 8, E_HB // 512), row(512, 0),
                  pl.BlockSpec((SC_WIDTH, 512), lambda i: (0, 0))],
        out_specs=[row(1024, 0), row(512, 0), row(512, 0), row(512, 0), row(512, 0)],
        out_shape=[jax.ShapeDtypeStruct((S, D_MODEL), bf16), jax.ShapeDtypeStruct((S, A_WIDTH), f32),
                   jax.ShapeDtypeStruct((S, A_WIDTH), f32), jax.ShapeDtypeStruct((S, 512), bf16),
                   jax.ShapeDtypeStruct((S, 512), f32)],
        scratch_shapes=[pltpu.VMEM((tm + 8, 512), f32)],
        compiler_params=_cp(("parallel",)),
    )(dy, w_out, proj, proj, proj, proj, proj, proj, o_a, conv_w)


def _mix0_bwd_b(dcv, proj, conv_w):
    S = proj.shape[0]
    tm = _tile(S, 256)
    nt = S // tm

    def body(dcv_ref, dcvn_ref, cg_ref, hb_ref, cgh_ref, hbh_ref, w_ref, dcg_ref, dhb_ref, gw_ref, tbuf, dbuf):
        i = pl.program_id(0)
        cg = cg_ref[...]
        hb = hb_ref[...]
        t = cg * hb
        tbuf[0:8, :] = jnp.where(i > 0, cgh_ref[...] * hbh_ref[...], 0.0)
        tbuf[8:, :] = t
        dcv_v = dcv_ref[...]
        dbuf[0:tm, :] = dcv_v
        dbuf[tm:, :] = jnp.where(i < nt - 1, dcvn_ref[...], 0.0)
        dt = w_ref[2:3, :] * dcv_v + w_ref[1:2, :] * dbuf[pl.ds(1, tm), :] + w_ref[0:1, :] * dbuf[pl.ds(2, tm), :]
        dcg_ref[...] = (dt * hb).astype(bf16)
        dhb_ref[...] = (dt * cg).astype(bf16)
        g2 = jnp.sum(dcv_v * t, axis=0, keepdims=True)
        g1 = jnp.sum(dcv_v * tbuf[pl.ds(7, tm), :], axis=0, keepdims=True)
        g0 = jnp.sum(dcv_v * tbuf[pl.ds(6, tm), :], axis=0, keepdims=True)
        part = jnp.concatenate([g0, g1, g2, jnp.zeros((5, 512), f32)], axis=0)

        @pl.when(i == 0)
        def _():
            gw_ref[...] = part

        @pl.when(i > 0)
        def _():
            gw_ref[...] += part

    row = lambda w, c: pl.BlockSpec((tm, w), lambda i: (i, c))
    return pl.pallas_call(
        body, name="mix0_bwd_b", grid=(nt,),
        in_specs=[row(512, 0), _next_halo(tm, 8, 0, S), row(512, E_CG // 512), row(512, E_HB // 512),
                  _prev_halo(tm, 8, E_CG // 512), _prev_halo(tm, 8, E_HB // 512),
                  pl.BlockSpec((SC_WIDTH, 512), lambda i: (0, 0))],
        out_specs=[row(512, 0), row(512, 0), pl.BlockSpec((8, 512), lambda i: (0, 0))],
        out_shape=[jax.ShapeDtypeStruct((S, 512), bf16), jax.ShapeDtypeStruct((S, 512), bf16),
                   jax.ShapeDtypeStruct((8, 512), f32)],
        scratch_shapes=[pltpu.VMEM((tm + 8, 512), f32), pltpu.VMEM((tm + 8, 512), f32)],
        compiler_params=_cp(("arbitrary",)),
    )(dcv, dcv, proj, proj, proj, proj, conv_w)


def _qk_bwd(dq_g, dk_g, dv_g, proj, tabs, nw, hm, dbg, dcg, dhb, dz):
    S = proj.shape[0]
    tm = _tile(S, 256)

    def body(*refs):
        d_refs = refs[0:6]
        dv_refs = refs[6:9]
        x_ref, c_ref, s1_ref, s2_ref, nw_ref, m_ref, dbg_ref, dcg_ref, dhb_ref, dz_ref, o_ref, gw_ref = refs[9:]
        i = pl.program_id(0)
        c, s1, s2, m = c_ref[...], s1_ref[...], s2_ref[...], m_ref[...]
        accs = []
        for kind in range(2):
            w = nw_ref[kind:kind + 1, :]
            acc = jnp.zeros((1, LANES), f32)
            for gi in range(N_GROUPS):
                for p in range(4):
                    col = kind * 1536 + gi * 512 + p * LANES
                    dout = d_refs[kind * 3 + gi][:, p * LANES:(p + 1) * LANES]
                    t = x_ref[:, col:col + LANES]
                    dthat = (dout * c + pltpu.roll(dout * s1, LANES - ROT_HALF, axis=1)
                             + pltpu.roll(dout * s2, ROT_HALF, axis=1))
                    r = lax.rsqrt(_head_mean(t * t, m) + EPS)
                    tn = t * r
                    acc = acc + jnp.sum(dthat * tn, axis=0, keepdims=True)
                    dtn = dthat * w
                    o_ref[:, col:col + LANES] = (r * (dtn - tn * _head_mean(dtn * tn, m))).astype(bf16)
            accs.append(acc + pltpu.roll(acc, HEAD_DIM, axis=1))
        for gi in range(N_GROUPS):
            o_ref[:, E_V + gi * 512:E_V + (gi + 1) * 512] = dv_refs[gi][...].astype(bf16)
        o_ref[:, E_BG:E_CG] = dbg_ref[...]
        o_ref[:, E_CG:E_HB] = dcg_ref[...]
        o_ref[:, E_HB:E_Z] = dhb_ref[...]
        o_ref[:, E_Z:] = dz_ref[...]
        part = jnp.concatenate([accs[0], accs[1], jnp.zeros((6, LANES), f32)], axis=0)

        @pl.when(i == 0)
        def _():
            gw_ref[...] = part

        @pl.when(i > 0)
        def _():
            gw_ref[...] += part

    row = lambda w, c: pl.BlockSpec((tm, w), lambda i: (i, c))
    tab = row(LANES, 0)
    return pl.pallas_call(
        body, name="qk_bwd", grid=(S // tm,),
        in_specs=[row(512, 0)] * 9 + [row(3072, 0), tab, tab, tab, pl.BlockSpec((2, LANES), lambda i: (0, 0)),
                                      pl.BlockSpec((LANES, LANES), lambda i: (0, 0)),
                                      row(512, 0), row(512, 0), row(512, 0), row(1024, 0)],
        out_specs=[row(EVEN_IN, 0), pl.BlockSpec((8, LANES), lambda i: (0, 0))],
        out_shape=[jax.ShapeDtypeStruct((S, EVEN_IN), bf16), jax.ShapeDtypeStruct((8, LANES), f32)],
        compiler_params=_cp(("arbitrary",)),
    )(*dq_g, *dk_g, *dv_g, proj, *tabs, nw, hm, dbg, dcg, dhb, dz)


def _inv_count(i, tm, p):
    rowg = lax.broadcasted_iota(jnp.int32, (tm, 1), 0) + i * tm
    return 1.0 / jnp.minimum(rowg + 1, p).astype(f32)


def _layer_norm_stats(c):
    mu = jnp.mean(c, axis=-1, keepdims=True)
    cen = c - mu
    rstd = lax.rsqrt(jnp.mean(cen * cen, axis=-1, keepdims=True) + EPS)
    return cen * rstd, rstd


def _fill_pool_buf(i, ubuf, uc_ref, uch_ref):
    ubuf[0:16, :] = jnp.where(i > 0, uch_ref[...], 0.0)
    ubuf[16:, :] = uc_ref[...]


def _pooled(i, tm, ubuf, gi):
    p = POOL_SIZES[gi]
    cols = slice(gi * LANES, (gi + 1) * LANES)
    acc = ubuf[pl.ds(16, tm), cols]
    cur = acc
    for jj in range(1, p):
        acc = acc + ubuf[pl.ds(16 - jj, tm), cols]
    return acc * _inv_count(i, tm, p) - cur


def _fill_glu_buf(i, gbuf, da_ref, dg_ref, dah_ref, dgh_ref):
    gbuf[0:32, :] = jnp.where(i > 0, dah_ref[...] * _sigmoid(dgh_ref[...]), 0.0)
    gbuf[32:, :] = da_ref[...] * _sigmoid(dg_ref[...])


def _shift_copies(buf, sh, tm):
    for b in range(1, 8):
        sh[b - 1] = buf[pl.ds(b, tm + 24), :]


CONV_ROWS = 32


def _window(buf, sh, base, off, rows):
    b = off % 8
    if b == 0:
        return buf[pl.ds(base + off, rows), :]
    return sh[b - 1, pl.ds(base + (off - b), rows), :]


def _mix1_fwd(proj, pool_w, pool_scale, dconv_w, dconv_b, ln_w, ln_b):
    S = proj.shape[0]
    tm = _tile(S, 256)

    def body(uc_ref, uch_ref, da_ref, dg_ref, dah_ref, dgh_ref, za_ref, zb_ref, pw_ref, ps_ref, cw_ref, cb_ref,
             lw_ref, lb_ref, u_ref, c_ref, mc_ref, ubuf, gbuf, gsh):
        i = pl.program_id(0)
        _fill_pool_buf(i, ubuf, uc_ref, uch_ref)
        za = za_ref[...]
        for gi in range(4):
            cols = slice(gi * LANES, (gi + 1) * LANES)
            mc = jnp.dot(_pooled(i, tm, ubuf, gi).astype(bf16), pw_ref[gi], preferred_element_type=f32)
            mc_ref[:, cols] = mc
            zg = za[:, cols]
            u_ref[:, cols] = (mc * ps_ref[:, cols] * (zg * _sigmoid(zg))).astype(bf16)
        _fill_glu_buf(i, gbuf, da_ref, dg_ref, dah_ref, dgh_ref)
        _shift_copies(gbuf, gsh, tm)
        c = jnp.zeros((tm, 512), f32) + cb_ref[...]
        for k in range(D_CONV):
            c = c + cw_ref[k:k + 1, :] * _window(gbuf, gsh, 0, 32 - (D_CONV - 1) + k, tm)
        c_ref[...] = c
        yhat, _ = _layer_norm_stats(c)
        l = yhat * lw_ref[...] + lb_ref[...]
        zb = zb_ref[...]
        u_ref[:, 512:] = (l * _sigmoid(l) * (zb * _sigmoid(zb))).astype(bf16)

    row = lambda w, c: pl.BlockSpec((tm, w), lambda i: (i, c))
    vec = pl.BlockSpec((1, 512), lambda i: (0, 0))
    return pl.pallas_call(
        body, name="mix1_fwd", grid=(S // tm,),
        in_specs=[row(512, 0), _prev_halo(tm, 16, 0), row(512, 1), row(512, 2), _prev_halo(tm, 32, 1), _prev_halo(tm, 32, 2),
                  row(512, 3), row(512, 4), pl.BlockSpec((4, LANES, LANES), lambda i: (0, 0, 0)), vec,
                  pl.BlockSpec((D_CONV, 512), lambda i: (0, 0)), vec, vec, vec],
        out_specs=[row(1024, 0), row(512, 0), row(512, 0)],
        out_shape=[jax.ShapeDtypeStruct((S, D_MODEL), bf16), jax.ShapeDtypeStruct((S, 512), f32),
                   jax.ShapeDtypeStruct((S, 512), f32)],
        scratch_shapes=[pltpu.VMEM((tm + 16, 512), f32), pltpu.VMEM((tm + 32, 512), f32),
                        pltpu.VMEM((7, tm + 24, 512), f32)],
        compiler_params=_cp(("parallel",)),
    )(proj, proj, proj, proj, proj, proj, proj, proj, pool_w, pool_scale, dconv_w, dconv_b, ln_w, ln_b)


def _mix1_bwd_a(dy, w_out, proj, c, mc, pool_w, pool_scale, ln_w, ln_b):
    S = proj.shape[0]
    tm = _tile(S, 256)

    def body(dy_ref, wo_ref, za_ref, zb_ref, c_ref, mc_ref, pw_ref, ps_ref, lw_ref, lb_ref,
             dz_ref, dc_ref, dpl_ref, dmc_ref, acc_ref):
        i = pl.program_id(0)
        du_v = _d_gate_in(dy_ref, wo_ref)
        ps = ps_ref[...]
        za = za_ref[...]
        sga = _sigmoid(za)
        mcv = mc_ref[...]
        dz_ref[:, :512] = (du_v[:, :512] * (mcv * ps) * _dsilu(za, sga)).astype(bf16)
        dyc = du_v[:, :512] * (za * sga)
        g_ps = jnp.sum(dyc * mcv, axis=0, keepdims=True)
        dmc = (dyc * ps).astype(bf16)
        dmc_ref[...] = dmc
        for gi in range(4):
            cols = slice(gi * LANES, (gi + 1) * LANES)
            dpl_ref[:, cols] = lax.dot_general(dmc[:, cols], pw_ref[gi], (((1,), (1,)), ((), ())), preferred_element_type=f32)
        yhat, rstd = _layer_norm_stats(c_ref[...])
        lw = lw_ref[...]
        l = yhat * lw + lb_ref[...]
        sgl = _sigmoid(l)
        zb = zb_ref[...]
        sgb = _sigmoid(zb)
        dz_ref[:, 512:] = (du_v[:, 512:] * (l * sgl) * _dsilu(zb, sgb)).astype(bf16)
        dl = du_v[:, 512:] * (zb * sgb) * _dsilu(l, sgl)
        g_lb = jnp.sum(dl, axis=0, keepdims=True)
        g_lw = jnp.sum(dl * yhat, axis=0, keepdims=True)
        dyh = dl * lw
        dc = rstd * (dyh - jnp.mean(dyh, axis=-1, keepdims=True) - yhat * jnp.mean(dyh * yhat, axis=-1, keepdims=True))
        dc_ref[...] = dc
        g_db = jnp.sum(dc, axis=0, keepdims=True)
        part = jnp.concatenate([g_ps, g_lw, g_lb, g_db, jnp.zeros((4, 512), f32)], axis=0)

        @pl.when(i == 0)
        def _():
            acc_ref[...] = part

        @pl.when(i > 0)
        def _():
            acc_ref[...] += part

    row = lambda w, c_: pl.BlockSpec((tm, w), lambda i: (i, c_))
    vec = pl.BlockSpec((1, 512), lambda i: (0, 0))
    return pl.pallas_call(
        body, name="mix1_bwd_a", grid=(S // tm,),
        in_specs=[row(1024, 0), pl.BlockSpec((D_MODEL, D_MODEL), lambda i: (0, 0)),
                  row(512, 3), row(512, 4), row(512, 0), row(512, 0),
                  pl.BlockSpec((4, LANES, LANES), lambda i: (0, 0, 0)), vec, vec, vec],
        out_specs=[row(1024, 0), row(512, 0), row(512, 0), row(512, 0), pl.BlockSpec((8, 512), lambda i: (0, 0))],
        out_shape=[jax.ShapeDtypeStruct((S, D_MODEL), bf16), jax.ShapeDtypeStruct((S, 512), f32),
                   jax.ShapeDtypeStruct((S, 512), f32), jax.ShapeDtypeStruct((S, 512), bf16),
                   jax.ShapeDtypeStruct((8, 512), f32)],
        compiler_params=_cp(("arbitrary",)),
    )(dy, w_out, proj, proj, c, mc, pool_w, pool_scale, ln_w, ln_b)


def _mix1_bwd_b(dc, dpl, dmc, dz, proj, dconv_w):
    S = proj.shape[0]
    tm = _tile(S, 256)
    nt = S // tm

    def body(dc_ref, dcn_ref, dpl_ref, dpn_ref, dmc_ref, dz_ref, uc_ref, uch_ref, da_ref, dg_ref,
             cw_ref, o_ref, gcw_ref, gpw_ref, ubuf, dcbuf, dpbuf, dcsh, gacc):
        i = pl.program_id(0)
        last = i == nt - 1
        _fill_pool_buf(i, ubuf, uc_ref, uch_ref)
        dcbuf[0:tm, :] = dc_ref[...]
        dcbuf[tm:, :] = jnp.where(last, 0.0, dcn_ref[...])
        _shift_copies(dcbuf, dcsh, tm)
        dpl_v = dpl_ref[...]
        for gi in range(4):
            p = POOL_SIZES[gi]
            cols = slice(gi * LANES, (gi + 1) * LANES)
            dpbuf[0:tm, cols] = dpl_v[:, cols] * _inv_count(i, tm, p)
            dpbuf[tm:, cols] = jnp.where(last, 0.0, dpn_ref[:, cols] * (1.0 / p))
        gpw = []
        for gi in range(4):
            p = POOL_SIZES[gi]
            cols = slice(gi * LANES, (gi + 1) * LANES)
            acc = -dpl_v[:, cols]
            for jj in range(p):
                acc = acc + dpbuf[pl.ds(jj, tm), cols]
            o_ref[:, cols] = acc.astype(bf16)
            pooled = _pooled(i, tm, ubuf, gi).astype(bf16)
            gpw.append(lax.dot_general(pooled, dmc_ref[:, cols], (((0,), (0,)), ((), ())), preferred_element_type=f32))
        gacc[...] = jnp.zeros_like(gacc)

        def conv_rows(ci, carry):
            base = pl.multiple_of(ci * CONV_ROWS, CONV_ROWS)
            da = da_ref[pl.ds(base, CONV_ROWS), :]
            sg = _sigmoid(dg_ref[pl.ds(base, CONV_ROWS), :])
            gl = da * sg
            dgl = jnp.zeros((CONV_ROWS, 512), f32)
            for k in range(D_CONV):
                win = _window(dcbuf, dcsh, base, D_CONV - 1 - k, CONV_ROWS)
                dgl = dgl + cw_ref[k:k + 1, :] * win
                gacc[k] += jnp.sum((gl * win).reshape(CONV_ROWS // 8, 8, 512), axis=0)
            o_ref[pl.ds(base, CONV_ROWS), O_DA:O_DG] = (dgl * sg).astype(bf16)
            o_ref[pl.ds(base, CONV_ROWS), O_DG:O_Z] = (dgl * da * sg * (1.0 - sg)).astype(bf16)
            return carry

        lax.fori_loop(0, tm // CONV_ROWS, conv_rows, 0)
        o_ref[:, O_Z:] = dz_ref[...]
        gcw_part = jnp.concatenate(
            [jnp.sum(gacc[k], axis=0, keepdims=True) for k in range(D_CONV)] + [jnp.zeros((1, 512), f32)], axis=0)

        @pl.when(i == 0)
        def _():
            gcw_ref[...] = gcw_part
            for gi in range(4):
                gpw_ref[gi] = gpw[gi]

        @pl.when(i > 0)
        def _():
            gcw_ref[...] += gcw_part
            for gi in range(4):
                gpw_ref[gi] += gpw[gi]

    row = lambda w, c_: pl.BlockSpec((tm, w), lambda i: (i, c_))
    return pl.pallas_call(
        body, name="mix1_bwd_b", grid=(nt,),
        in_specs=[row(512, 0), _next_halo(tm, 32, 0, S), row(512, 0), _next_halo(tm, 16, 0, S), row(512, 0), row(1024, 0),
                  row(512, 0), _prev_halo(tm, 16, 0), row(512, 1), row(512, 2),
                  pl.BlockSpec((D_CONV, 512), lambda i: (0, 0))],
        out_specs=[row(ODD_IN, 0), pl.BlockSpec((32, 512), lambda i: (0, 0)),
                   pl.BlockSpec((4, LANES, LANES), lambda i: (0, 0, 0))],
        out_shape=[jax.ShapeDtypeStruct((S, ODD_IN), bf16), jax.ShapeDtypeStruct((32, 512), f32),
                   jax.ShapeDtypeStruct((4, LANES, LANES), f32)],
        scratch_shapes=[pltpu.VMEM((tm + 16, 512), f32), pltpu.VMEM((tm + 32, 512), f32),
                        pltpu.VMEM((tm + 16, 512), f32), pltpu.VMEM((7, tm + 24, 512), f32),
                        pltpu.VMEM((D_CONV, 8, 512), f32)],
        compiler_params=_cp(("arbitrary",)),
    )(dc, dc, dpl, dpl, dmc, dz, proj, proj, proj, proj, dconv_w)


_SMALL_LATE = ["e_q_norm_w", "e_k_norm_w", "e_conv_w", "o_norm_w", "o_pool_w", "o_pool_scale", "o_dconv_w", "o_dconv_b",
               "o_ln_w", "o_ln_b"]


def _local_step(x, pos_col, target, w, dist=None):
    hm = _head_mean_matrix()
    nw = jnp.concatenate([jnp.tile(w["e_q_norm_w"], (1, 2)), jnp.tile(w["e_k_norm_w"], (1, 2))], axis=0)
    tabs = _rope_tables(pos_col)
    pool_wb = w["o_pool_w"].astype(bf16)
    e_norm_w, e_w_in = w["e_norm_w"], w["e_w_in"]

    if dist is None:
        proj0, qk, h0 = _in_proj0(x, e_norm_w, e_w_in, tabs, nw, hm)
    else:
        proj0, qk, h0, gathered = _in_proj0(x, e_norm_w, e_w_in, tabs, nw, hm, fuse=([], dist[0]))
        w = {**w, **dist[1](gathered)}
    e_conv_w, e_w_out, o_norm_w, o_w_in, o_w_out = w["e_conv_w"], w["e_w_out"], w["o_norm_w"], w["o_w_in"], w["o_w_out"]
    o_pool_scale, o_dconv_w, o_dconv_b, o_ln_w, o_ln_b = (w[k] for k in ("o_pool_scale", "o_dconv_w", "o_dconv_b", "o_ln_w", "o_ln_b"))
    o_g, lse_g = [], []
    for g in range(N_GROUPS):
        o, l = _attn_fwd_local(qk, proj0) if g == 0 else _attn_fwd_dil(qk, proj0, g, name=f"attn_fwd{g}")
        o_g.append(o)
        lse_g.append(l)
    u0, o_a, lt = _mix0_fwd(o_g, lse_g, proj0, e_conv_w)
    x1, h1 = _out_proj_rms(u0, e_w_out, x, o_norm_w, name="out_proj0")
    o_w_in3 = o_w_in.reshape(1, D_MODEL, ODD_IN)
    proj1 = _mm_nn_resident(h1, o_w_in3, name="in_proj1", tm=512)
    u1, c1, mc1 = _mix1_fwd(proj1, pool_wb, o_pool_scale, o_dconv_w, o_dconv_b, o_ln_w, o_ln_b)
    dy, dyb, loss = _mm_out_loss(u1, o_w_out, x1, target, name="out_proj1_loss")
    g_o_w_out = _mm_tn(u1, dyb, name="g_w_out1", out_dtype=bf16)
    dz1, dc1, dpl1, dmc1, sums1 = _mix1_bwd_a(dyb, o_w_out, proj1, c1, mc1, pool_wb, o_pool_scale, o_ln_w, o_ln_b)
    dproj1, g_dconv_w, g_pool_w = _mix1_bwd_b(dc1, dpl1, dmc1, dz1, proj1, o_dconv_w)
    g_o_w_in = _mm_tn(h1, dproj1, name="g_w_in1", out_dtype=bf16)
    d1, d1b, g_o_norm = _mm_nt_rms_bwd(dproj1, o_w_in3, x1, o_norm_w, dy, name="d_h1")
    g_e_w_out = _mm_tn(u0, d1b, name="g_w_out0", out_dtype=bf16)
    dz0, do_a, dsum, dbg, dcv = _mix0_bwd_a(d1b, e_w_out, proj0, o_a, e_conv_w)
    dcg, dhb, g_conv_w = _mix0_bwd_b(dcv, proj0, e_conv_w)
    fuse_a = None if dist is None else (
        [g_e_w_out.reshape(N_DEV, D_MODEL // N_DEV, D_MODEL),
         jnp.moveaxis(g_o_w_in.reshape(D_MODEL, N_DEV, ODD_IN // N_DEV), 1, 0),
         g_o_w_out.reshape(N_DEV, D_MODEL // N_DEV, D_MODEL)], [])
    dq_g, dk_g, dv_g = [], [], []
    for g in range(N_GROUPS):
        if g == 0:
            dqkv = _attn_bwd_local(qk, proj0, do_a, lt, dsum, fuse=fuse_a)
            if dist is not None:
                dqkv, recv_a = dqkv
            dq, dk, dv = dqkv
        else:
            dq, dk, dv = _attn_bwd_dil(qk, proj0, do_a, lt, dsum, g, name=f"attn_bwd{g}")
        dq_g.append(dq)
        dk_g.append(dk)
        dv_g.append(dv)
    dproj0, g_qk_norm = _qk_bwd(dq_g, dk_g, dv_g, proj0, tabs, nw, hm, dbg, dcg, dhb, dz0)
    half = D_MODEL // 2
    g_e_w_in_a = _mm_tn(h0, dproj0, name="g_w_in0a", out_dtype=bf16, chunks=N_DEV, a_cols=(0, half))
    if dist is None:
        g_e_w_in_b = _mm_tn(h0, dproj0, name="g_w_in0b", out_dtype=bf16, chunks=N_DEV, a_cols=(1, half))
    else:
        g_e_w_in_b, recv_b0 = _mm_tn(h0, dproj0, name="g_w_in0b", out_dtype=bf16, chunks=N_DEV, a_cols=(1, half),
                                     fuse=([g_e_w_in_a], []))
    grads = dict(
        e_q_norm_w=g_qk_norm[0:1, :HEAD_DIM], e_k_norm_w=g_qk_norm[1:2, :HEAD_DIM],
        e_conv_w=g_conv_w[:SC_WIDTH], e_w_out=g_e_w_out,
        o_norm_w=g_o_norm, o_w_in=g_o_w_in, o_pool_w=g_pool_w,
        o_pool_scale=sums1[0:1], o_dconv_w=g_dconv_w[:D_CONV], o_dconv_b=sums1[3:4],
        o_ln_w=sums1[1:2], o_ln_b=sums1[2:3], o_w_out=g_o_w_out)
    if dist is None:
        grad_x, _, grads["e_norm_w"] = _mm_nt_rms_bwd(dproj0, e_w_in, x, e_norm_w, d1, name="d_h0", tm=512)
        grads["e_w_in"] = jnp.concatenate([g_e_w_in_a, g_e_w_in_b], axis=1)
        return loss, grad_x, grads
    small_late, offs = _pack_rows([grads[n_] for n_ in _SMALL_LATE])
    grad_x, _, g_e_norm, recv_b = _mm_nt_rms_bwd(dproj0, e_w_in, x, e_norm_w, d1, name="d_h0", tm=512,
                                                 fuse=([g_e_w_in_b], [small_late]))
    recv_c = _exchange([], [jnp.concatenate([g_e_norm.reshape(8, LANES), loss], axis=0)], name="exchange_e_norm_loss")
    recv = dict(e_w_out=[recv_a[0]], o_w_in=[recv_a[1]], o_w_out=[recv_a[2]], e_w_in=[recv_b0[0], recv_b[0]],
                small_late=recv_b[1], e_norm_w=recv_c[0])
    return loss, grad_x, recv, {n_: (off, grads[n_].shape) for n_, off in zip(_SMALL_LATE, offs)}


_MESH_ID = pl.DeviceIdType.MESH
_HBM = pl.BlockSpec(memory_space=pl.ANY)


def _all_gather(arrs, *, name):
    n = len(arrs)

    def body(*refs):
        ins, outs = refs[:n], refs[n:2 * n]
        send_sems, recv_sems, local_sems = refs[2 * n:]
        x, y, c = _place()
        me, sibling = (x, y, c), (x, y, 1 - c)
        chips = [(1 - x, y), (x, 1 - y), (1 - x, 1 - y)]

        def slot(t, px, py, pc):
            return outs[t].at[4 * px + 2 * py + pc]

        def copy(t, k, block, to, src=None):
            dst = slot(t, *block)
            return pltpu.make_async_remote_copy(
                src_ref=dst if src is None else src, dst_ref=dst,
                send_sem=send_sems.at[7 * t + k], recv_sem=recv_sems.at[7 * t + k],
                device_id=to, device_id_type=_MESH_ID)

        mine = [pltpu.make_async_copy(ins[t], slot(t, *me), local_sems.at[t]) for t in range(n)]
        for cp in mine:
            cp.start()
        first = []
        for t in range(n):
            first.append(copy(t, 0, me, sibling, src=ins[t]))
            first += [copy(t, 1 + j, me, (*chip, c), src=ins[t]) for j, chip in enumerate(chips)]
        for cp in first:
            cp.start()
        passed = []
        for j, chip in enumerate(chips):
            for t in range(n):
                copy(t, 1 + j, (*chip, c), me).wait_recv()
                fwd = copy(t, 4 + j, (*chip, c), sibling)
                fwd.start()
                passed.append(fwd)
        for t in range(n):
            copy(t, 0, sibling, me).wait_recv()
            for j, chip in enumerate(chips):
                copy(t, 4 + j, (*chip, 1 - c), me).wait_recv()
        for cp in first + passed:
            cp.wait_send()
        for cp in mine:
            cp.wait()

    return pl.pallas_call(
        body, name=name,
        in_specs=[_HBM] * n, out_specs=[_HBM] * n,
        out_shape=[jax.ShapeDtypeStruct((N_DEV, *a.shape), a.dtype) for a in arrs],
        scratch_shapes=[pltpu.SemaphoreType.DMA((7 * n,)), pltpu.SemaphoreType.DMA((7 * n,)),
                        pltpu.SemaphoreType.DMA((n,))],
    )(*arrs)


def _exchange(chunked, whole, *, name):
    arrs = list(chunked) + list(whole)
    n = len(arrs)

    def body(*refs):
        start, wait = _exchange_plan(refs[:n], refs[n:2 * n], *refs[2 * n:], len(chunked))
        start()
        wait()

    return pl.pallas_call(
        body, name=name, in_specs=[_HBM] * n, out_specs=[_HBM] * n,
        out_shape=_exchange_out_shapes(chunked, whole), scratch_shapes=_exchange_sems(n),
    )(*arrs)


def _adamw(w, g, m, v):
    m2 = ADAM_B1 * m + (1.0 - ADAM_B1) * g
    v2 = ADAM_B2 * v + (1.0 - ADAM_B2) * (g * g)
    m_hat = m2 / (1.0 - ADAM_B1 ** ADAM_STEP)
    v_hat = v2 / (1.0 - ADAM_B2 ** ADAM_STEP)
    delta = -ADAM_LR * (m_hat / (jnp.sqrt(v_hat) + ADAM_EPS) + ADAM_WD * w)
    return delta, m2, v2


def _sum_adamw(parts, w, m, v, *, name):
    R, C = w.shape
    nsplit = len(parts)
    rp = R // nsplit
    tr = _tile(rp, 256)
    npt = rp // tr

    def body(*refs):
        p_refs = refs[:nsplit]
        w_ref, m_ref, v_ref, g_ref, d_ref, nm_ref, nv_ref = refs[nsplit:]
        h = pl.program_id(0)
        g = None
        for i in range(N_DEV):
            pi = p_refs[0][i]
            for q in range(1, nsplit):
                pi = jnp.where(h == q, p_refs[q][i], pi)
            g = pi.astype(f32) if g is None else g + pi.astype(f32)
        g_ref[...] = g
        d_ref[...], nm_ref[...], nv_ref[...] = _adamw(w_ref[...], g, m_ref[...], v_ref[...])

    def part_spec(q):
        return pl.BlockSpec((N_DEV, tr, C), lambda h, i: (0, jnp.where(h == q, i, 0), 0))

    spec = pl.BlockSpec((tr, C), lambda h, i: (h * npt + i, 0))
    return pl.pallas_call(
        body, name=name, grid=(nsplit, npt),
        in_specs=[part_spec(q) for q in range(nsplit)] + [spec, spec, spec],
        out_specs=[spec] * 4, out_shape=[jax.ShapeDtypeStruct((R, C), f32)] * 4,
        compiler_params=_cp(("parallel", "parallel")),
    )(*parts, w, m, v)


def _sum_parts(parts, *, name):
    _, R, C = parts.shape

    def body(p_ref, o_ref):
        g = p_ref[0]
        for i in range(1, N_DEV):
            g = g + p_ref[i]
        o_ref[...] = g

    return pl.pallas_call(body, name=name, out_shape=jax.ShapeDtypeStruct((R, C), f32),
                          compiler_params=pltpu.CompilerParams(vmem_limit_bytes=VMEM_LIMIT))(parts)


def _adamw_small(ws, gs, ms, vs):
    n = len(ws)

    def body(*refs):
        w_r, g_r, m_r, v_r = refs[:n], refs[n:2 * n], refs[2 * n:3 * n], refs[3 * n:4 * n]
        d_r, nm_r, nv_r = refs[4 * n:5 * n], refs[5 * n:6 * n], refs[6 * n:7 * n]
        for t in range(n):
            d_r[t][...], nm_r[t][...], nv_r[t][...] = _adamw(w_r[t][...], g_r[t][...], m_r[t][...], v_r[t][...])

    shapes = [jax.ShapeDtypeStruct(w.shape, f32) for w in ws]
    outs = pl.pallas_call(body, name="adamw_small", out_shape=shapes * 3)(*ws, *gs, *ms, *vs)
    return outs[:n], outs[n:2 * n], outs[2 * n:]


_WEIGHTS = ["e_norm_w", "e_w_in", "e_q_norm_w", "e_k_norm_w", "e_conv_w", "e_w_out", "o_norm_w", "o_w_in", "o_pool_w",
            "o_pool_scale", "o_dconv_w", "o_dconv_b", "o_ln_w", "o_ln_b", "o_w_out"]
_BIG = ["e_w_in", "e_w_out", "o_w_in", "o_w_out"]
_SMALL_SHARDED = ["e_conv_w", "o_norm_w", "o_pool_scale", "o_dconv_w", "o_dconv_b", "o_ln_w", "o_ln_b"]
_SMALL_ALL = ["e_norm_w", "e_q_norm_w", "e_k_norm_w", "e_conv_w", "o_norm_w", "o_pool_w", "o_pool_scale", "o_dconv_w",
              "o_dconv_b", "o_ln_w", "o_ln_b"]


def _pack_rows(pieces):
    rows, offs, r0 = [], [], 0
    for p in pieces:
        flat = p.reshape(-1)
        nr = -(-flat.shape[0] // (8 * LANES)) * 8
        rows.append(jnp.pad(flat, (0, nr * LANES - flat.shape[0])).reshape(nr, LANES))
        offs.append((r0, nr))
        r0 += nr
    return jnp.concatenate(rows, axis=0), offs


def _unpack_rows(buf, off, shape):
    r0, nr = off
    size = int(np.prod(shape))
    return buf[..., r0:r0 + nr, :].reshape(*buf.shape[:-2], nr * LANES)[..., :size].reshape(*buf.shape[:-2], *shape)


def kernel(x, positions, e_norm_w, e_w_in, e_q_norm_w, e_k_norm_w, e_conv_w, e_w_out, o_norm_w, o_w_in, o_pool_w, o_pool_scale, o_dconv_w, o_dconv_b, o_ln_w, o_ln_b, o_w_out, loss_target, m_e_norm_w, m_e_w_in, m_e_q_norm_w, m_e_k_norm_w, m_e_conv_w, m_e_w_out, m_o_norm_w, m_o_w_in, m_o_pool_w, m_o_pool_scale, m_o_dconv_w, m_o_dconv_b, m_o_ln_w, m_o_ln_b, m_o_w_out, v_e_norm_w, v_e_w_in, v_e_q_norm_w, v_e_k_norm_w, v_e_conv_w, v_e_w_out, v_o_norm_w, v_o_w_in, v_o_pool_w, v_o_pool_scale, v_o_dconv_w, v_o_dconv_b, v_o_ln_w, v_o_ln_b, v_o_w_out):
    w = dict(e_norm_w=e_norm_w, e_w_in=e_w_in, e_q_norm_w=e_q_norm_w, e_k_norm_w=e_k_norm_w, e_conv_w=e_conv_w,
             e_w_out=e_w_out, o_norm_w=o_norm_w, o_w_in=o_w_in, o_pool_w=o_pool_w, o_pool_scale=o_pool_scale,
             o_dconv_w=o_dconv_w, o_dconv_b=o_dconv_b, o_ln_w=o_ln_w, o_ln_b=o_ln_b, o_w_out=o_w_out)
    m = dict(e_norm_w=m_e_norm_w, e_w_in=m_e_w_in, e_q_norm_w=m_e_q_norm_w, e_k_norm_w=m_e_k_norm_w, e_conv_w=m_e_conv_w,
             e_w_out=m_e_w_out, o_norm_w=m_o_norm_w, o_w_in=m_o_w_in, o_pool_w=m_o_pool_w, o_pool_scale=m_o_pool_scale,
             o_dconv_w=m_o_dconv_w, o_dconv_b=m_o_dconv_b, o_ln_w=m_o_ln_w, o_ln_b=m_o_ln_b, o_w_out=m_o_w_out)
    v = dict(e_norm_w=v_e_norm_w, e_w_in=v_e_w_in, e_q_norm_w=v_e_q_norm_w, e_k_norm_w=v_e_k_norm_w, e_conv_w=v_e_conv_w,
             e_w_out=v_e_w_out, o_norm_w=v_o_norm_w, o_w_in=v_o_w_in, o_pool_w=v_o_pool_w, o_pool_scale=v_o_pool_scale,
             o_dconv_w=v_o_dconv_w, o_dconv_b=v_o_dconv_b, o_ln_w=v_o_ln_w, o_ln_b=v_o_ln_b, o_w_out=v_o_w_out)
    S = x.shape[1]
    me = 4 * lax.axis_index("x") + 2 * lax.axis_index("y") + lax.axis_index("c")

    small_local, small_offs = _pack_rows([w[n_] for n_ in _SMALL_SHARDED])
    g_e_in, = _all_gather([w["e_w_in"][0].astype(bf16)], name="gather_e_w_in")
    rest_local = [w["e_w_out"][0].astype(bf16), w["o_w_in"][0].astype(bf16), w["o_w_out"][0].astype(bf16), small_local]

    def unpack_rest(gathered):
        g_e_out, g_o_in, g_o_out, g_small = gathered
        full = {}
        for n_, off in zip(_SMALL_SHARDED, small_offs):
            shard = _unpack_rows(g_small, off, w[n_].shape[1:])
            full[n_] = jnp.moveaxis(shard, 0, -2).reshape(*shard.shape[1:-1], N_DEV * shard.shape[-1])
        return dict(
            e_conv_w=full["e_conv_w"], e_w_out=g_e_out.reshape(D_MODEL, D_MODEL), o_norm_w=full["o_norm_w"].reshape(1, D_MODEL),
            o_w_in=jnp.moveaxis(g_o_in, 0, 1).reshape(D_MODEL, ODD_IN), o_pool_scale=full["o_pool_scale"].reshape(1, 512),
            o_dconv_w=full["o_dconv_w"], o_dconv_b=full["o_dconv_b"].reshape(1, 512), o_ln_w=full["o_ln_w"].reshape(1, 512),
            o_ln_b=full["o_ln_b"].reshape(1, 512), o_w_out=g_o_out.reshape(D_MODEL, D_MODEL))

    loss_blk, grad_x, recv, small_where = _local_step(
        x[0], positions.reshape(S, 1), loss_target[0],
        dict(e_norm_w=w["e_norm_w"], e_w_in=g_e_in, e_q_norm_w=w["e_q_norm_w"], e_k_norm_w=w["e_k_norm_w"],
             o_pool_w=w["o_pool_w"][0]),
        dist=(rest_local, unpack_rest))

    out_g, out_d, out_m, out_v = {}, {}, {}, {}
    for n_ in _BIG:
        res = _sum_adamw(recv[n_], w[n_][0], m[n_][0], v[n_][0], name="adamw_" + n_)
        out_g[n_], out_d[n_], out_m[n_], out_v[n_] = [r[None] for r in res]
    small_sum = _sum_parts(recv["small_late"], name="sum_small_grads")
    last_sum = _sum_parts(recv["e_norm_w"], name="sum_e_norm_grad_loss")
    loss = last_sum[8, 0]
    gs = []
    for n_ in _SMALL_ALL:
        if n_ == "e_norm_w":
            gs.append(last_sum[:8].reshape(w[n_].shape))
            continue
        off, shape = small_where[n_]
        gfull = _unpack_rows(small_sum, off, shape)
        if n_ in _SMALL_SHARDED:
            width = w[n_].shape[-1]
            gfull = lax.dynamic_slice_in_dim(gfull, me * width, width, axis=gfull.ndim - 1)
        gs.append(gfull.reshape(w[n_].shape))
    ds, nms, nvs = _adamw_small([w[n_] for n_ in _SMALL_ALL], gs, [m[n_] for n_ in _SMALL_ALL], [v[n_] for n_ in _SMALL_ALL])
    for n_, g_, d_, nm_, nv_ in zip(_SMALL_ALL, gs, ds, nms, nvs):
        out_g[n_], out_d[n_], out_m[n_], out_v[n_] = g_, d_, nm_, nv_

    return (loss, grad_x[None], *[out_g[n_] for n_ in _WEIGHTS], *[out_d[n_] for n_ in _WEIGHTS],
            *[out_m[n_] for n_ in _WEIGHTS], *[out_v[n_] for n_ in _WEIGHTS])
```

```python
import functools

import numpy as np
import jax
import jax.numpy as jnp
from jax import lax
from jax.experimental import pallas as pl
from jax.experimental.pallas import tpu as pltpu

f32 = jnp.float32
bf16 = jnp.bfloat16

D_MODEL = 1024
HEAD_DIM = 64
N_GROUPS = 3
DILATIONS = (1, 4, 16)
QBLK = 128
A_WIDTH = 512
EVEN_IN = 7168
ODD_IN = 2560
POOL_SIZES = (2, 4, 8, 16)
D_CONV = 31
SC_WIDTH = 3
ROT_HALF = 8
ROPE_THETA = 500000.0
EPS = 1e-6
NEG = -1e30
SCALE = HEAD_DIM ** -0.5
N_DEV = 8
LANES = 128
VMEM_LIMIT = 48 * 1024 * 1024

ADAM_LR = 0.001
ADAM_B1 = 0.9
ADAM_B2 = 0.999
ADAM_EPS = 1e-08
ADAM_WD = 0.01
ADAM_STEP = 10

E_Q, E_K, E_V, E_BG, E_CG, E_HB, E_Z = 0, 1536, 3072, 4608, 5120, 5632, 6144
O_UC, O_DA, O_DG, O_Z = 0, 512, 1024, 1536


def _cp(sem):
    return pltpu.CompilerParams(dimension_semantics=sem, vmem_limit_bytes=VMEM_LIMIT)


_HBM_ANY = pl.BlockSpec(memory_space=pl.ANY)


def _sigmoid(z):
    return 1.0 / (1.0 + jnp.exp(-z))


def _tile(n, pref):
    t = pref
    while n % t:
        t //= 2
    return t


def _place():
    return lax.axis_index("x"), lax.axis_index("y"), lax.axis_index("c")


def _exchange_plan(ins, outs, send_sems, recv_sems, local_sems, nc):
    n = len(ins)
    x, y, c = _place()
    me_i = 4 * x + 2 * y + c

    def src(t, dev_i):
        return ins[t].at[dev_i] if t < nc else ins[t]

    def copies(arriving):
        cps = []
        for m in range(1, N_DEV):
            px = 1 - x if m & 4 else x
            py = 1 - y if m & 2 else y
            pc = 1 - c if m & 1 else c
            peer_i = 4 * px + 2 * py + pc
            for t in range(n):
                cps.append(pltpu.make_async_remote_copy(
                    src_ref=src(t, peer_i), dst_ref=outs[t].at[peer_i if arriving else me_i],
                    send_sem=send_sems.at[7 * t + m - 1], recv_sem=recv_sems.at[7 * t + m - 1],
                    device_id=(x, y, c) if arriving else (px, py, pc), device_id_type=pl.DeviceIdType.MESH))
        return cps

    def mine():
        return [pltpu.make_async_copy(src(t, me_i), outs[t].at[me_i], local_sems.at[t]) for t in range(n)]

    def start():
        for cp in mine() + copies(False):
            cp.start()

    def wait():
        for cp in copies(True):
            cp.wait_recv()
        for cp in copies(False):
            cp.wait_send()
        for cp in mine():
            cp.wait()

    return start, wait


def _exchange_sems(n):
    return [pltpu.SemaphoreType.DMA((7 * n,)), pltpu.SemaphoreType.DMA((7 * n,)), pltpu.SemaphoreType.DMA((n,))]


def _exchange_out_shapes(chunked, whole):
    return ([jax.ShapeDtypeStruct(a.shape, a.dtype) for a in chunked]
            + [jax.ShapeDtypeStruct((N_DEV, *a.shape), a.dtype) for a in whole])


def _grid_call(body, *, name, grid, in_specs, out_specs, out_shape, scratch_shapes, sem, args, fuse=None):
    if fuse is None:
        return pl.pallas_call(body, name=name, grid=grid, in_specs=in_specs, out_specs=out_specs, out_shape=out_shape,
                              scratch_shapes=scratch_shapes, compiler_params=_cp(sem))(*args)
    chunked, whole = fuse
    ex = list(chunked) + list(whole)
    n, n_in, n_out, n_sc = len(ex), len(in_specs), len(out_specs), len(scratch_shapes)

    def fused(*refs):
        ins, ex_in = refs[:n_in], refs[n_in:n_in + n]
        outs, ex_out = refs[n_in + n:n_in + n + n_out], refs[n_in + n + n_out:n_in + 2 * n + n_out]
        scratch = refs[n_in + 2 * n + n_out:n_in + 2 * n + n_out + n_sc]
        start, wait = _exchange_plan(ex_in, ex_out, *refs[-3:], len(chunked))
        first = functools.reduce(jnp.logical_and, [pl.program_id(a) == 0 for a in range(len(grid))])
        last = functools.reduce(jnp.logical_and, [pl.program_id(a) == g - 1 for a, g in enumerate(grid)])
        pl.when(first)(start)
        body(*ins, *outs, *scratch)
        pl.when(last)(wait)

    res = pl.pallas_call(
        fused, name=name, grid=grid, in_specs=list(in_specs) + [_HBM_ANY] * n,
        out_specs=list(out_specs) + [_HBM_ANY] * n, out_shape=list(out_shape) + _exchange_out_shapes(chunked, whole),
        scratch_shapes=list(scratch_shapes) + _exchange_sems(n),
        compiler_params=_cp(("arbitrary",) * len(grid)))(*args, *ex)
    return res[:n_out], res[n_out:]


def _load_once(src_hbm, dst_vmem, sem):
    @pl.when(pl.program_id(0) == 0)
    def _():
        cp = pltpu.make_async_copy(src_hbm, dst_vmem, sem)
        cp.start()
        cp.wait()


def _mm_nn_resident(a, b, *, name, tm=256, fuse=None):
    M, K = a.shape
    nch, _, tn = b.shape
    tm = _tile(M, tm)

    def body(a_ref, b_hbm, o_ref, bbuf, sem):
        _load_once(b_hbm, bbuf, sem)
        av = a_ref[...]
        for j in range(nch):
            o_ref[:, j * tn:(j + 1) * tn] = jnp.dot(av, bbuf[j], preferred_element_type=f32)

    out = _grid_call(
        body, name=name, grid=(M // tm,), in_specs=[pl.BlockSpec((tm, K), lambda i: (i, 0)), _HBM_ANY],
        out_specs=[pl.BlockSpec((tm, nch * tn), lambda i: (i, 0))],
        out_shape=[jax.ShapeDtypeStruct((M, nch * tn), f32)],
        scratch_shapes=[pltpu.VMEM(b.shape, b.dtype), pltpu.SemaphoreType.DMA],
        sem=("arbitrary",), args=[a, b], fuse=fuse)
    return out[0] if fuse is None else (out[0][0], out[1])


def _mm_tn(a, b, *, name, out_dtype=f32, tn=512, chunks=None, a_cols=None, fuse=None):
    S, Ka = a.shape
    a_blk = 0
    if a_cols is not None:
        a_blk, Ka = a_cols
    N = b.shape[1]
    ts = _tile(S, 2048)
    ns = S // ts
    if chunks:
        tn = N // chunks
        out_spec = pl.BlockSpec((None, Ka, tn), lambda j, s: (j, 0, 0))
        out_shape = jax.ShapeDtypeStruct((chunks, Ka, tn), out_dtype)
    else:
        tn = _tile(N, tn)
        out_spec = pl.BlockSpec((Ka, tn), lambda j, s: (0, j))
        out_shape = jax.ShapeDtypeStruct((Ka, N), out_dtype)

    def body(a_ref, b_ref, o_ref, acc_ref):
        s = pl.program_id(1)
        part = lax.dot_general(a_ref[...], b_ref[...], (((0,), (0,)), ((), ())), preferred_element_type=f32)

        @pl.when(s == 0)
        def _():
            acc_ref[...] = part

        @pl.when(s > 0)
        def _():
            acc_ref[...] += part

        @pl.when(s == ns - 1)
        def _():
            o_ref[...] = acc_ref[...].astype(out_dtype)

    out = _grid_call(
        body, name=name, grid=(N // tn, ns),
        in_specs=[pl.BlockSpec((ts, Ka), lambda j, s: (s, a_blk)), pl.BlockSpec((ts, tn), lambda j, s: (s, j))],
        out_specs=[out_spec], out_shape=[out_shape],
        scratch_shapes=[pltpu.VMEM((Ka, tn), f32)],
        sem=("parallel", "arbitrary"), args=[a, b], fuse=fuse)
    return out[0] if fuse is None else (out[0][0], out[1])


def _mm_out_loss(u, w, x_res, target, *, name):
    M, K = u.shape
    N = w.shape[1]
    tm = _tile(M, 512)
    nm = M // tm

    def body(u_ref, w_ref, x_ref, t_ref, dy_ref, dyb_ref, loss_ref, acc_ref):
        i = pl.program_id(0)
        y = jnp.dot(u_ref[...], w_ref[...], preferred_element_type=f32) + x_ref[...]
        err = y - t_ref[...]
        dy = err * (1.0 / N)
        dy_ref[...] = dy
        dyb_ref[...] = dy.astype(bf16)
        part = jnp.sum(err * err, axis=0, keepdims=True)

        @pl.when(i == 0)
        def _():
            acc_ref[...] = part

        @pl.when(i > 0)
        def _():
            acc_ref[...] += part

        @pl.when(i == nm - 1)
        def _():
            tot = jnp.sum(acc_ref[...], axis=1, keepdims=True)
            loss_ref[...] = jnp.broadcast_to(tot * (0.5 / N), (8, LANES))

    return pl.pallas_call(
        body, name=name, grid=(nm,),
        in_specs=[pl.BlockSpec((tm, K), lambda i: (i, 0)), pl.BlockSpec((K, N), lambda i: (0, 0)),
                  pl.BlockSpec((tm, N), lambda i: (i, 0)), pl.BlockSpec((tm, N), lambda i: (i, 0))],
        out_specs=[pl.BlockSpec((tm, N), lambda i: (i, 0)), pl.BlockSpec((tm, N), lambda i: (i, 0)),
                   pl.BlockSpec((8, LANES), lambda i: (0, 0))],
        out_shape=[jax.ShapeDtypeStruct((M, N), f32), jax.ShapeDtypeStruct((M, N), bf16),
                   jax.ShapeDtypeStruct((8, LANES), f32)],
        scratch_shapes=[pltpu.VMEM((1, N), f32)],
        compiler_params=_cp(("arbitrary",)),
    )(u, w, x_res, target)


def _out_proj_rms(u, w, res, norm_w, *, name):
    M, K = u.shape
    N = w.shape[1]
    tm = _tile(M, 512)

    def body(u_ref, w_ref, r_ref, nw_ref, y_ref, h_ref):
        y = jnp.dot(u_ref[...], w_ref[...], preferred_element_type=f32) + r_ref[...]
        y_ref[...] = y
        h_ref[...] = (y * lax.rsqrt(jnp.mean(y * y, axis=-1, keepdims=True) + EPS) * nw_ref[...]).astype(bf16)

    row = lambda width: pl.BlockSpec((tm, width), lambda i: (i, 0))
    return pl.pallas_call(
        body, name=name, grid=(M // tm,),
        in_specs=[row(K), pl.BlockSpec((K, N), lambda i: (0, 0)), row(N), pl.BlockSpec((1, N), lambda i: (0, 0))],
        out_specs=[row(N), row(N)],
        out_shape=[jax.ShapeDtypeStruct((M, N), f32), jax.ShapeDtypeStruct((M, N), bf16)],
        compiler_params=_cp(("parallel",)),
    )(u, w, res, norm_w)


def _mm_nt_rms_bwd(a, b, x, w, res, *, name, tm=512, fuse=None):
    M, K = a.shape
    nch, N, tk = b.shape
    tm = _tile(M, tm)

    def body(a_ref, b_hbm, x_ref, w_ref, res_ref, dx_ref, dxb_ref, gw_ref, bbuf, sem):
        i = pl.program_id(0)
        _load_once(b_hbm, bbuf, sem)
        dh_v = None
        for k in range(nch):
            part = lax.dot_general(a_ref[:, k * tk:(k + 1) * tk], bbuf[k], (((1,), (1,)), ((), ())),
                                   preferred_element_type=f32)
            dh_v = part if dh_v is None else dh_v + part
        xv = x_ref[...]
        r = lax.rsqrt(jnp.mean(xv * xv, axis=-1, keepdims=True) + EPS)
        xn = xv * r
        dxn = dh_v * w_ref[...]
        dx = r * (dxn - xn * jnp.mean(dxn * xn, axis=-1, keepdims=True)) + res_ref[...]
        dx_ref[...] = dx
        dxb_ref[...] = dx.astype(bf16)
        part = jnp.sum(dh_v * xn, axis=0, keepdims=True)

        @pl.when(i == 0)
        def _():
            gw_ref[...] = part

        @pl.when(i > 0)
        def _():
            gw_ref[...] += part

    row = lambda width: pl.BlockSpec((tm, width), lambda i: (i, 0))
    vec = pl.BlockSpec((1, N), lambda i: (0, 0))
    out = _grid_call(
        body, name=name, grid=(M // tm,),
        in_specs=[row(K), _HBM_ANY, row(N), vec, row(N)],
        out_specs=[row(N), row(N), vec],
        out_shape=[jax.ShapeDtypeStruct((M, N), f32), jax.ShapeDtypeStruct((M, N), bf16), jax.ShapeDtypeStruct((1, N), f32)],
        scratch_shapes=[pltpu.VMEM(b.shape, b.dtype), pltpu.SemaphoreType.DMA],
        sem=("arbitrary",), args=[a, b, x, w, res], fuse=fuse)
    return out if fuse is None else (*out[0], out[1])


_INV_FREQ = [float(v) for v in (np.float32(ROPE_THETA) ** (-np.arange(ROT_HALF, dtype=np.float32) / np.float32(ROT_HALF))).astype(np.float32)]


def _rope_tables(pos_col):
    S = pos_col.shape[0]
    tm = _tile(S, 1024)

    def body(p_ref, c_ref, s1_ref, s2_ref):
        lane = lax.broadcasted_iota(jnp.int32, (tm, LANES), 1)
        lm = lane % HEAD_DIM
        fi = lm % ROT_HALF
        inv = jnp.zeros((tm, LANES), f32)
        for k in range(ROT_HALF):
            inv = jnp.where(fi == k, _INV_FREQ[k], inv)
        ang = p_ref[...].astype(f32) * inv
        cs = jnp.cos(ang)
        sn = jnp.sin(ang)
        c_ref[...] = jnp.where(lm < 2 * ROT_HALF, cs, 1.0)
        s1_ref[...] = jnp.where((lm >= ROT_HALF) & (lm < 2 * ROT_HALF), sn, 0.0)
        s2_ref[...] = jnp.where(lm < ROT_HALF, -sn, 0.0)

    spec = pl.BlockSpec((tm, LANES), lambda i: (i, 0))
    return pl.pallas_call(
        body, name="rope_tables", grid=(S // tm,),
        in_specs=[pl.BlockSpec((tm, 1), lambda i: (i, 0))],
        out_specs=[spec, spec, spec],
        out_shape=[jax.ShapeDtypeStruct((S, LANES), f32)] * 3,
        compiler_params=_cp(("parallel",)),
    )(pos_col)


def _head_mean(v, m):
    hi = v.astype(bf16)
    lo = (v - hi.astype(f32)).astype(bf16)
    return jnp.dot(hi, m, preferred_element_type=f32) + jnp.dot(lo, m, preferred_element_type=f32)


def _head_mean_matrix():
    i = np.arange(LANES)
    return jnp.asarray(((i[:, None] // HEAD_DIM) == (i[None, :] // HEAD_DIM)).astype(np.float32) / HEAD_DIM, dtype=bf16)


def _in_proj0(x, norm_w, b, tabs, nw, hm, *, fuse=None):
    M, K = x.shape
    nch, _, tn = b.shape
    tm = _tile(M, 256)

    def body(x_ref, w_ref, b_hbm, c_ref, s1_ref, s2_ref, nw_ref, m_ref, o_ref, qk_ref, h_ref, bbuf, sem):
        _load_once(b_hbm, bbuf, sem)
        xv = x_ref[...]
        av = (xv * lax.rsqrt(jnp.mean(xv * xv, axis=-1, keepdims=True) + EPS) * w_ref[...]).astype(bf16)
        h_ref[...] = av
        c, s1, s2, m = c_ref[...], s1_ref[...], s2_ref[...], m_ref[...]
        for j in range(nch):
            res = jnp.dot(av, bbuf[j], preferred_element_type=f32)
            o_ref[:, j * tn:(j + 1) * tn] = res
            for p in range(tn // LANES):
                col = j * tn + p * LANES
                if col >= E_V:
                    continue
                w = nw_ref[0:1, :] if col < E_K else nw_ref[1:2, :]
                t = res[:, p * LANES:(p + 1) * LANES]
                that = t * lax.rsqrt(_head_mean(t * t, m) + EPS) * w
                qk_ref[:, col:col + LANES] = (
                    that * c + pltpu.roll(that, ROT_HALF, axis=1) * s1 + pltpu.roll(that, LANES - ROT_HALF, axis=1) * s2)

    tab = pl.BlockSpec((tm, LANES), lambda i: (i, 0))
    out = _grid_call(
        body, name="in_proj0", grid=(M // tm,),
        in_specs=[pl.BlockSpec((tm, K), lambda i: (i, 0)), pl.BlockSpec((1, K), lambda i: (0, 0)), _HBM_ANY, tab, tab, tab,
                  pl.BlockSpec((2, LANES), lambda i: (0, 0)), pl.BlockSpec((LANES, LANES), lambda i: (0, 0))],
        out_specs=[pl.BlockSpec((tm, nch * tn), lambda i: (i, 0)), pl.BlockSpec((tm, E_V), lambda i: (i, 0)),
                   pl.BlockSpec((tm, K), lambda i: (i, 0))],
        out_shape=[jax.ShapeDtypeStruct((M, nch * tn), f32), jax.ShapeDtypeStruct((M, E_V), f32),
                   jax.ShapeDtypeStruct((M, K), bf16)],
        scratch_shapes=[pltpu.VMEM(b.shape, b.dtype), pltpu.SemaphoreType.DMA],
        sem=("arbitrary",), args=[x, norm_w, b, *tabs, nw, hm], fuse=fuse)
    return out if fuse is None else (*out[0], out[1])


def _key_geometry(nparts):
    qr = QBLK // nparts
    rho = lax.broadcasted_iota(jnp.int32, (2 * QBLK, 2 * QBLK), 0) % QBLK
    kap = lax.broadcasted_iota(jnp.int32, (2 * QBLK, 2 * QBLK), 1)
    n_q = QBLK + nparts * (rho % qr) + rho // qr
    tt = kap % (2 * qr)
    n_k = nparts * tt + kap // (2 * qr)
    dist = n_q - n_k
    return (dist >= 0) & (dist <= QBLK), (tt < qr).astype(jnp.int32)


def _stack_heads(t, lo):
    zero = jnp.zeros_like(t)
    return jnp.concatenate([jnp.where(lo, t, zero), jnp.where(lo, zero, t)], axis=0)


def _attn_block_fwd(qb, kcat, vcat, mask, lo):
    s = lax.dot_general(_stack_heads(qb, lo), kcat, (((1,), (1,)), ((), ())), preferred_element_type=f32) * SCALE
    s = jnp.where(mask, s, NEG)
    mx = jnp.max(s, axis=-1, keepdims=True)
    pexp = jnp.exp(s - mx)
    den = jnp.sum(pexp, axis=-1, keepdims=True)
    pn = (pexp * (1.0 / den)).astype(bf16)
    o2 = jnp.dot(pn, vcat, preferred_element_type=f32)
    lse2 = jnp.broadcast_to(mx + jnp.log(den), (2 * QBLK, LANES))
    return jnp.where(lo, o2[:QBLK], o2[QBLK:]), jnp.where(lo, lse2[:QBLK], lse2[QBLK:])


def _attn_block_bwd(qb, dob, kcat, vcat, lt, ds, mask, lo):
    lt_sw = pltpu.roll(lt, HEAD_DIM, axis=1)
    ds_sw = pltpu.roll(ds, HEAD_DIM, axis=1)
    lt2 = jnp.concatenate([jnp.where(lo, lt, lt_sw), jnp.where(lo, lt_sw, lt)], axis=0)
    ds2 = jnp.concatenate([jnp.where(lo, ds, ds_sw), jnp.where(lo, ds_sw, ds)], axis=0)
    q2 = _stack_heads(qb, lo)
    do2 = _stack_heads(dob, lo)
    s = lax.dot_general(q2, kcat, (((1,), (1,)), ((), ())), preferred_element_type=f32) * SCALE
    s = jnp.where(mask, s, NEG)
    prob = jnp.exp(s - jnp.concatenate([lt2, lt2], axis=1))
    dp = lax.dot_general(do2, vcat, (((1,), (1,)), ((), ())), preferred_element_type=f32)
    dsb = (prob * (dp - jnp.concatenate([ds2, ds2], axis=1)) * SCALE).astype(bf16)
    dq2 = jnp.dot(dsb, kcat, preferred_element_type=f32)
    dk = lax.dot_general(dsb, q2, (((0,), (0,)), ((), ())), preferred_element_type=f32)
    dv = lax.dot_general(prob.astype(bf16), do2, (((0,), (0,)), ((), ())), preferred_element_type=f32)
    return jnp.where(lo, dq2[:QBLK], dq2[QBLK:]), dk, dv


ATT_ROWS = 1024
ATT_UNROLL = 4


def _attn_fwd_local(qk, proj):
    S = qk.shape[0]
    tr = _tile(S, ATT_ROWS)
    lw = 4 * LANES
    nb = tr // QBLK

    def body(q_ref, k_ref, kh_ref, v_ref, vh_ref, o_ref, lse_ref, kbuf, vbuf):
        j = pl.program_id(0)
        kbuf[0:QBLK, :] = jnp.where(j > 0, kh_ref[...], 0.0)
        kbuf[QBLK:, :] = k_ref[...]
        vbuf[0:QBLK, :] = jnp.where(j > 0, vh_ref[...], 0.0)
        vbuf[QBLK:, :] = v_ref[...]
        band, is_prev = _key_geometry(1)
        lo = lax.broadcasted_iota(jnp.int32, (QBLK, LANES), 1) < HEAD_DIM

        def blk(c, carry):
            r0 = pl.multiple_of(c * QBLK, QBLK)
            first = jnp.where((c == 0) & (j == 0), 1, 0)
            mask = band & (is_prev * first == 0)
            for pp in range(lw // LANES):
                lanes = slice(pp * LANES, (pp + 1) * LANES)
                o, lse = _attn_block_fwd(q_ref[pl.ds(r0, QBLK), lanes].astype(bf16),
                                         kbuf[pl.ds(r0, 2 * QBLK), lanes].astype(bf16),
                                         vbuf[pl.ds(r0, 2 * QBLK), lanes].astype(bf16), mask, lo)
                o_ref[pl.ds(r0, QBLK), lanes] = o
                lse_ref[pl.ds(r0, QBLK), lanes] = lse
            return carry

        lax.fori_loop(0, nb, blk, 0, unroll=ATT_UNROLL)

    def halo(col):
        return pl.BlockSpec((QBLK, lw), lambda j, l: (jnp.maximum(j * nb - 1, 0), col + l))

    def tile(col):
        return pl.BlockSpec((tr, lw), lambda j, l: (j, col + l))

    return pl.pallas_call(
        body, name="attn_fwd0", grid=(S // tr, A_WIDTH // lw),
        in_specs=[tile(E_Q // lw), tile(E_K // lw), halo(E_K // lw), tile(E_V // lw), halo(E_V // lw)],
        out_specs=[tile(0), tile(0)],
        out_shape=[jax.ShapeDtypeStruct((S, A_WIDTH), f32)] * 2,
        scratch_shapes=[pltpu.VMEM((QBLK + tr, lw), f32)] * 2,
        compiler_params=_cp(("parallel", "parallel")),
    )(qk, qk, qk, proj, proj)


def _attn_bwd_local(qk, proj, do_a, lt, dsum, fuse=None):
    S = qk.shape[0]
    tr = _tile(S, ATT_ROWS)
    lw = 2 * LANES
    nb = tr // QBLK
    nt = S // tr

    def body(q_ref, qn_ref, do_ref, don_ref, lt_ref, ltn_ref, ds_ref, dsn_ref, k_ref, kh_ref, v_ref, vh_ref,
             dq_ref, dk_ref, dv_ref, kbuf, vbuf, dkbuf, dvbuf):
        j = pl.program_id(0)
        zeros = jnp.zeros((QBLK, lw), f32)
        kbuf[0:QBLK, :] = jnp.where(j > 0, kh_ref[...], 0.0)
        kbuf[pl.ds(QBLK, tr), :] = k_ref[...]
        kbuf[pl.ds(QBLK + tr, QBLK), :] = zeros
        vbuf[0:QBLK, :] = jnp.where(j > 0, vh_ref[...], 0.0)
        vbuf[pl.ds(QBLK, tr), :] = v_ref[...]
        vbuf[pl.ds(QBLK + tr, QBLK), :] = zeros
        dkbuf[...] = jnp.zeros_like(dkbuf)
        dvbuf[...] = jnp.zeros_like(dvbuf)
        band, is_prev = _key_geometry(1)
        lo = lax.broadcasted_iota(jnp.int32, (QBLK, LANES), 1) < HEAD_DIM

        def blk(c, carry):
            r0 = pl.multiple_of(c * QBLK, QBLK)
            first = jnp.where((c == 0) & (j == 0), 1, 0)
            mask = band & (is_prev * first == 0)
            for pp in range(lw // LANES):
                lanes = slice(pp * LANES, (pp + 1) * LANES)
                dq, dk, dv = _attn_block_bwd(
                    q_ref[pl.ds(r0, QBLK), lanes].astype(bf16), do_ref[pl.ds(r0, QBLK), lanes].astype(bf16),
                    kbuf[pl.ds(r0, 2 * QBLK), lanes].astype(bf16), vbuf[pl.ds(r0, 2 * QBLK), lanes].astype(bf16),
                    lt_ref[pl.ds(r0, QBLK), lanes], ds_ref[pl.ds(r0, QBLK), lanes], mask, lo)
                dq_ref[pl.ds(r0, QBLK), lanes] = dq
                dkbuf[pl.ds(r0, 2 * QBLK), lanes] += dk
                dvbuf[pl.ds(r0, 2 * QBLK), lanes] += dv
            return carry

        lax.fori_loop(0, nb, blk, 0, unroll=min(nb, 2 * ATT_UNROLL))

        @pl.when(j < nt - 1)
        def _():
            mask = band & (is_prev == 1)
            for pp in range(lw // LANES):
                lanes = slice(pp * LANES, (pp + 1) * LANES)
                _, dk, dv = _attn_block_bwd(
                    qn_ref[:, lanes].astype(bf16), don_ref[:, lanes].astype(bf16),
                    kbuf[pl.ds(tr, 2 * QBLK), lanes].astype(bf16), vbuf[pl.ds(tr, 2 * QBLK), lanes].astype(bf16),
                    ltn_ref[:, lanes], dsn_ref[:, lanes], mask, lo)
                dkbuf[pl.ds(tr, 2 * QBLK), lanes] += dk
                dvbuf[pl.ds(tr, 2 * QBLK), lanes] += dv

        dk_ref[...] = dkbuf[pl.ds(QBLK, tr), :]
        dv_ref[...] = dvbuf[pl.ds(QBLK, tr), :]

    def prev_halo(col):
        return pl.BlockSpec((QBLK, lw), lambda j, l: (jnp.maximum(j * nb - 1, 0), col + l))

    def next_halo(col):
        return pl.BlockSpec((QBLK, lw), lambda j, l: (jnp.minimum((j + 1) * nb, S // QBLK - 1), col + l))

    def tile(col):
        return pl.BlockSpec((tr, lw), lambda j, l: (j, col + l))

    return _grid_call(
        body, name="attn_bwd0", grid=(nt, A_WIDTH // lw),
        in_specs=[tile(E_Q // lw), next_halo(E_Q // lw), tile(0), next_halo(0), tile(0), next_halo(0), tile(0), next_halo(0),
                  tile(E_K // lw), prev_halo(E_K // lw), tile(E_V // lw), prev_halo(E_V // lw)],
        out_specs=[tile(0)] * 3,
        out_shape=[jax.ShapeDtypeStruct((S, A_WIDTH), f32)] * 3,
        scratch_shapes=[pltpu.VMEM((tr + 2 * QBLK, lw), f32)] * 4,
        sem=("parallel", "parallel"), args=[qk, qk, do_a, do_a, lt, lt, dsum, dsum, qk, qk, proj, proj], fuse=fuse)


def _stream_view(a, d):
    S, W = a.shape
    return a.reshape(S // 8, 8, W) if d == 4 else a.reshape(S // 16, 2, 8, W)


def _stream_ref(ref, d, r, part, col, lw):
    n = ref.shape[0]
    if d == 4:
        return ref.at[pl.ds(0, n), r + 4 * part, pl.ds(col, lw)]
    return ref.at[pl.ds(0, n), r // 8, r % 8, pl.ds(col, lw)]


def _stream_geometry(S, d):
    nparts = 2 if d == 4 else 1
    rows = S // (d * nparts)
    return nparts, rows, QBLK // nparts


def _attn_fwd_dil(qk, proj, g, *, name):
    S = qk.shape[0]
    d = DILATIONS[g]
    nparts, rows, qr = _stream_geometry(S, d)
    nb = rows // qr
    lw = 2 * LANES if d == 4 else 4 * LANES
    nlg = A_WIDTH // lw
    nitems = d * nlg
    ins = ((0, E_Q + A_WIDTH * g, 0), (0, E_K + A_WIDTH * g, qr), (1, E_V + A_WIDTH * g, qr))

    def body(qk_hbm, pj_hbm, o_hbm, l_hbm, qbuf, kbuf, vbuf, obuf, lbuf, in_sems, out_sems):
        i = pl.program_id(0)
        slot = i % 2
        hbm_in = (qk_hbm, pj_hbm)
        bufs_in = (qbuf, kbuf, vbuf)

        def in_copies(item, sl):
            r, lg = item // nlg, item % nlg
            cps = []
            for a in range(nparts):
                for t, (src, col, pad) in enumerate(ins):
                    cps.append(pltpu.make_async_copy(
                        _stream_ref(hbm_in[src], d, r, a, pl.multiple_of(col + lw * lg, LANES), lw),
                        bufs_in[t].at[sl, a, pl.ds(pad, rows), :], in_sems.at[sl, 3 * a + t]))
            return cps

        def out_copies(item, sl):
            r, lg = item // nlg, item % nlg
            cps = []
            for a in range(nparts):
                for t, (buf, dst) in enumerate(((obuf, o_hbm), (lbuf, l_hbm))):
                    cps.append(pltpu.make_async_copy(
                        buf.at[sl, a], _stream_ref(dst, d, r, a, pl.multiple_of(lw * lg, LANES), lw),
                        out_sems.at[sl, 2 * a + t]))
            return cps

        @pl.when(i == 0)
        def _():
            for sl in range(2):
                for a in range(nparts):
                    kbuf[sl, a, 0:qr, :] = jnp.zeros((qr, lw), f32)
                    vbuf[sl, a, 0:qr, :] = jnp.zeros((qr, lw), f32)
            for cp in in_copies(0, 0):
                cp.start()

        @pl.when(i + 1 < nitems)
        def _():
            for cp in in_copies(i + 1, 1 - slot):
                cp.start()

        for cp in in_copies(i, slot):
            cp.wait()

        @pl.when(i >= 2)
        def _():
            for cp in out_copies(i - 2, slot):
                cp.wait()

        band, is_prev = _key_geometry(nparts)
        lo = lax.broadcasted_iota(jnp.int32, (QBLK, LANES), 1) < HEAD_DIM

        def blk(c, carry):
            r0 = pl.multiple_of(c * qr, qr)
            mask = band & (is_prev * jnp.where(c == 0, 1, 0) == 0)
            for pp in range(lw // LANES):
                lanes = slice(pp * LANES, (pp + 1) * LANES)
                qb = jnp.concatenate([qbuf[slot, a, pl.ds(r0, qr), lanes] for a in range(nparts)], axis=0).astype(bf16)
                kcat = jnp.concatenate([kbuf[slot, a, pl.ds(r0, 2 * qr), lanes] for a in range(nparts)], axis=0).astype(bf16)
                vcat = jnp.concatenate([vbuf[slot, a, pl.ds(r0, 2 * qr), lanes] for a in range(nparts)], axis=0).astype(bf16)
                o, lse = _attn_block_fwd(qb, kcat, vcat, mask, lo)
                for a in range(nparts):
                    obuf[slot, a, pl.ds(r0, qr), lanes] = o[a * qr:(a + 1) * qr]
                    lbuf[slot, a, pl.ds(r0, qr), lanes] = lse[a * qr:(a + 1) * qr]
            return carry

        lax.fori_loop(0, nb, blk, 0, unroll=min(nb, 2 * ATT_UNROLL))

        for cp in out_copies(i, slot):
            cp.start()

        @pl.when(i == nitems - 1)
        def _():
            for cp in out_copies(i - 1, 1 - slot) + out_copies(i, slot):
                cp.wait()

    vshape = (S // 8, 8, A_WIDTH) if d == 4 else (S // 16, 2, 8, A_WIDTH)
    o, lse = pl.pallas_call(
        body, name=name, grid=(nitems,),
        in_specs=[_HBM_ANY, _HBM_ANY], out_specs=[_HBM_ANY, _HBM_ANY],
        out_shape=[jax.ShapeDtypeStruct(vshape, f32)] * 2,
        scratch_shapes=[pltpu.VMEM((2, nparts, rows, lw), f32), pltpu.VMEM((2, nparts, qr + rows, lw), f32),
                        pltpu.VMEM((2, nparts, qr + rows, lw), f32), pltpu.VMEM((2, nparts, rows, lw), f32),
                        pltpu.VMEM((2, nparts, rows, lw), f32),
                        pltpu.SemaphoreType.DMA((2, 3 * nparts)), pltpu.SemaphoreType.DMA((2, 2 * nparts))],
        compiler_params=_cp(("arbitrary",)),
    )(_stream_view(qk, d), _stream_view(proj, d))
    return o.reshape(S, A_WIDTH), lse.reshape(S, A_WIDTH)


def _attn_bwd_dil(qk, proj, do_a, lt, dsum, g, *, name):
    S = qk.shape[0]
    d = DILATIONS[g]
    nparts, rows, qr = _stream_geometry(S, d)
    nb = rows // qr
    lw = LANES if d == 4 else 4 * LANES
    nlg = A_WIDTH // lw
    nitems = d * nlg
    ins = ((0, E_Q + A_WIDTH * g, 0), (2, 0, 0), (3, 0, 0), (4, 0, 0), (0, E_K + A_WIDTH * g, qr), (1, E_V + A_WIDTH * g, qr))
    n_in = len(ins)

    def body(qk_hbm, pj_hbm, do_hbm, lt_hbm, ds_hbm, dq_hbm, dk_hbm, dv_hbm,
             qbuf, dobuf, ltbuf, dsbuf, kbuf, vbuf, dqbuf, dkbuf, dvbuf, in_sems, out_sems):
        i = pl.program_id(0)
        slot = i % 2
        hbm_in = (qk_hbm, pj_hbm, do_hbm, lt_hbm, ds_hbm)
        bufs_in = (qbuf, dobuf, ltbuf, dsbuf, kbuf, vbuf)

        def in_copies(item, sl):
            r, lg = item // nlg, item % nlg
            cps = []
            for a in range(nparts):
                for t, (src, col, pad) in enumerate(ins):
                    cps.append(pltpu.make_async_copy(
                        _stream_ref(hbm_in[src], d, r, a, pl.multiple_of(col + lw * lg, LANES), lw),
                        bufs_in[t].at[sl, a, pl.ds(pad, rows), :], in_sems.at[sl, n_in * a + t]))
            return cps

        def out_copies(item, sl):
            r, lg = item // nlg, item % nlg
            cps = []
            for a in range(nparts):
                for t, (buf, dst, pad) in enumerate(((dqbuf, dq_hbm, 0), (dkbuf, dk_hbm, qr), (dvbuf, dv_hbm, qr))):
                    cps.append(pltpu.make_async_copy(
                        buf.at[sl, a, pl.ds(pad, rows), :],
                        _stream_ref(dst, d, r, a, pl.multiple_of(lw * lg, LANES), lw), out_sems.at[sl, 3 * a + t]))
            return cps

        @pl.when(i == 0)
        def _():
            for sl in range(2):
                for a in range(nparts):
                    kbuf[sl, a, 0:qr, :] = jnp.zeros((qr, lw), f32)
                    vbuf[sl, a, 0:qr, :] = jnp.zeros((qr, lw), f32)
            for cp in in_copies(0, 0):
                cp.start()

        @pl.when(i + 1 < nitems)
        def _():
            for cp in in_copies(i + 1, 1 - slot):
                cp.start()

        for cp in in_copies(i, slot):
            cp.wait()

        @pl.when(i >= 2)
        def _():
            for cp in out_copies(i - 2, slot):
                cp.wait()

        for a in range(nparts):
            dkbuf[slot, a] = jnp.zeros((qr + rows, lw), f32)
            dvbuf[slot, a] = jnp.zeros((qr + rows, lw), f32)
        band, is_prev = _key_geometry(nparts)
        lo = lax.broadcasted_iota(jnp.int32, (QBLK, LANES), 1) < HEAD_DIM

        def blk(c, carry):
            r0 = pl.multiple_of(c * qr, qr)
            mask = band & (is_prev * jnp.where(c == 0, 1, 0) == 0)

            def rows_of(buf, n, lanes):
                return jnp.concatenate([buf[slot, a, pl.ds(r0, n), lanes] for a in range(nparts)], axis=0)

            for pp in range(lw // LANES):
                lanes = slice(pp * LANES, (pp + 1) * LANES)
                dq, dk, dv = _attn_block_bwd(
                    rows_of(qbuf, qr, lanes).astype(bf16), rows_of(dobuf, qr, lanes).astype(bf16),
                    rows_of(kbuf, 2 * qr, lanes).astype(bf16), rows_of(vbuf, 2 * qr, lanes).astype(bf16),
                    rows_of(ltbuf, qr, lanes), rows_of(dsbuf, qr, lanes), mask, lo)
                for a in range(nparts):
                    dqbuf[slot, a, pl.ds(r0, qr), lanes] = dq[a * qr:(a + 1) * qr]
                    dkbuf[slot, a, pl.ds(r0, 2 * qr), lanes] += dk[2 * a * qr:2 * (a + 1) * qr]
                    dvbuf[slot, a, pl.ds(r0, 2 * qr), lanes] += dv[2 * a * qr:2 * (a + 1) * qr]
            return carry

        lax.fori_loop(0, nb, blk, 0, unroll=min(nb, 2 * ATT_UNROLL))

        for cp in out_copies(i, slot):
            cp.start()

        @pl.when(i == nitems - 1)
        def _():
            for cp in out_copies(i - 1, 1 - slot) + out_copies(i, slot):
                cp.wait()

    vshape = (S // 8, 8, A_WIDTH) if d == 4 else (S // 16, 2, 8, A_WIDTH)
    plain = pltpu.VMEM((2, nparts, rows, lw), f32)
    padded = pltpu.VMEM((2, nparts, qr + rows, lw), f32)
    outs = pl.pallas_call(
        body, name=name, grid=(nitems,),
        in_specs=[_HBM_ANY] * 5, out_specs=[_HBM_ANY] * 3,
        out_shape=[jax.ShapeDtypeStruct(vshape, f32)] * 3,
        scratch_shapes=[plain, plain, plain, plain, padded, padded, plain, padded, padded,
                        pltpu.SemaphoreType.DMA((2, n_in * nparts)), pltpu.SemaphoreType.DMA((2, 3 * nparts))],
        compiler_params=_cp(("arbitrary",)),
    )(*[_stream_view(a, d) for a in (qk, proj, do_a, lt, dsum)])
    return [o.reshape(S, A_WIDTH) for o in outs]


def _prev_halo(tm, h, col):
    return pl.BlockSpec((h, 512), lambda i: (jnp.maximum(i * (tm // h) - 1, 0), col))


def _next_halo(tm, h, col, S):
    return pl.BlockSpec((h, 512), lambda i: (jnp.minimum((i + 1) * (tm // h), S // h - 1), col))


def _mix0_fwd(o_g, lse_g, proj, conv_w):
    S = proj.shape[0]
    tm = _tile(S, 256)

    def body(o0, o1, o2, l0, l1, l2, bg_ref, cg_ref, hb_ref, z_ref, cgh_ref, hbh_ref, w_ref,
             u_ref, oa_ref, lt_ref, tbuf):
        i = pl.program_id(0)
        ls = [l0[...], l1[...], l2[...]]
        mx = jnp.maximum(jnp.maximum(ls[0], ls[1]), ls[2])
        es = [jnp.exp(l - mx) for l in ls]
        tot = es[0] + es[1] + es[2]
        lt_ref[...] = mx + jnp.log(tot)
        inv = 1.0 / tot
        z = z_ref[...]
        sz = z * _sigmoid(z)
        oa = (es[0] * inv) * o0[...] + (es[1] * inv) * o1[...] + (es[2] * inv) * o2[...]
        oa_ref[...] = oa
        u_ref[:, :A_WIDTH] = (oa * sz[:, :A_WIDTH]).astype(bf16)
        t = cg_ref[...] * hb_ref[...]
        tbuf[0:8, :] = jnp.where(i > 0, cgh_ref[...] * hbh_ref[...], 0.0)
        tbuf[8:, :] = t
        cv = w_ref[2:3, :] * t + w_ref[1:2, :] * tbuf[pl.ds(7, tm), :] + w_ref[0:1, :] * tbuf[pl.ds(6, tm), :]
        u_ref[:, A_WIDTH:] = (bg_ref[...] * cv * sz[:, A_WIDTH:]).astype(bf16)

    row = lambda w, c: pl.BlockSpec((tm, w), lambda i: (i, c))
    return pl.pallas_call(
        body, name="mix0_fwd", grid=(S // tm,),
        in_specs=[row(512, 0)] * 6
        + [row(512, E_BG // 512), row(512, E_CG // 512), row(512, E_HB // 512), row(1024, E_Z // 1024),
           _prev_halo(tm, 8, E_CG // 512), _prev_halo(tm, 8, E_HB // 512), pl.BlockSpec((SC_WIDTH, 512), lambda i: (0, 0))],
        out_specs=[row(1024, 0), row(512, 0), row(512, 0)],
        out_shape=[jax.ShapeDtypeStruct((S, D_MODEL), bf16), jax.ShapeDtypeStruct((S, A_WIDTH), f32),
                   jax.ShapeDtypeStruct((S, A_WIDTH), f32)],
        scratch_shapes=[pltpu.VMEM((tm + 8, 512), f32)],
        compiler_params=_cp(("parallel",)),
    )(*o_g, *lse_g, proj, proj, proj, proj, proj, proj, conv_w)


def _dsilu(z, sg):
    return sg * (1.0 + z * (1.0 - sg))


def _d_gate_in(dy_ref, wo_ref):
    return lax.dot_general(dy_ref[...], wo_ref[...], (((1,), (1,)), ((), ())), preferred_element_type=f32)


def _mix0_bwd_a(dy, w_out, proj, o_a, conv_w):
    S = proj.shape[0]
    tm = _tile(S, 256)

    def body(dy_ref, wo_ref, bg_ref, cg_ref, hb_ref, z_ref, cgh_ref, hbh_ref, oa_ref, w_ref,
             dz_ref, doa_ref, ds_ref, dbg_ref, dcv_ref, tbuf):
        i = pl.program_id(0)
        lo = lax.broadcasted_iota(jnp.int32, (tm, LANES), 1) < HEAD_DIM
        z = z_ref[...]
        sg = _sigmoid(z)
        sz = z * sg
        dsz = _dsilu(z, sg)
        du_v = _d_gate_in(dy_ref, wo_ref)
        t = cg_ref[...] * hb_ref[...]
        tbuf[0:8, :] = jnp.where(i > 0, cgh_ref[...] * hbh_ref[...], 0.0)
        tbuf[8:, :] = t
        cv = w_ref[2:3, :] * t + w_ref[1:2, :] * tbuf[pl.ds(7, tm), :] + w_ref[0:1, :] * tbuf[pl.ds(6, tm), :]
        bg = bg_ref[...]
        oa = oa_ref[...]
        dz_ref[:, :A_WIDTH] = (du_v[:, :A_WIDTH] * oa * dsz[:, :A_WIDTH]).astype(bf16)
        dz_ref[:, A_WIDTH:] = (du_v[:, A_WIDTH:] * (bg * cv) * dsz[:, A_WIDTH:]).astype(bf16)
        doa = du_v[:, :A_WIDTH] * sz[:, :A_WIDTH]
        dyb = du_v[:, A_WIDTH:] * sz[:, A_WIDTH:]
        doa_ref[...] = doa
        dbg_ref[...] = (dyb * cv).astype(bf16)
        dcv_ref[...] = dyb * bg
        prod = doa * oa
        for p in range(4):
            pp = prod[:, p * LANES:(p + 1) * LANES]
            sa = jnp.sum(jnp.where(lo, pp, 0.0), axis=-1, keepdims=True)
            sb = jnp.sum(jnp.where(lo, 0.0, pp), axis=-1, keepdims=True)
            ds_ref[:, p * LANES:(p + 1) * LANES] = jnp.where(lo, sa, sb)

    row = lambda w, c: pl.BlockSpec((tm, w), lambda i: (i, c))
    return pl.pallas_call(
        body, name="mix0_bwd_a", grid=(S // tm,),
        in_specs=[row(1024, 0), pl.BlockSpec((D_MODEL, D_MODEL), lambda i: (0, 0)),
                  row(512, E_BG // 512), row(512, E_CG // 512), row(512, E_HB // 512), row(1024, E_Z // 1024),
                  _prev_halo(tm, 8, E_CG // 512), _prev_halo(tm, 8, E_HB // 512), row(512, 0),
                  pl.BlockSpec((SC_WIDTH, 512), lambda i: (0, 0))],
        out_specs=[row(1024, 0), row(512, 0), row(512, 0), row(512, 0), row(512, 0)],
        out_shape=[jax.ShapeDtypeStruct((S, D_MODEL), bf16), jax.ShapeDtypeStruct((S, A_WIDTH), f32),
                   jax.ShapeDtypeStruct((S, A_WIDTH), f32), jax.ShapeDtypeStruct((S, 512), bf16),
                   jax.ShapeDtypeStruct((S, 512), f32)],
        scratch_shapes=[pltpu.VMEM((tm + 8, 512), f32)],
        compiler_params=_cp(("parallel",)),
    )(dy, w_out, proj, proj, proj, proj, proj, proj, o_a, conv_w)


def _mix0_bwd_b(dcv, proj, conv_w):
    S = proj.shape[0]
    tm = _tile(S, 256)
    nt = S // tm

    def body(dcv_ref, dcvn_ref, cg_ref, hb_ref, cgh_ref, hbh_ref, w_ref, dcg_ref, dhb_ref, gw_ref, tbuf, dbuf):
        i = pl.program_id(0)
        cg = cg_ref[...]
        hb = hb_ref[...]
        t = cg * hb
        tbuf[0:8, :] = jnp.where(i > 0, cgh_ref[...] * hbh_ref[...], 0.0)
        tbuf[8:, :] = t
        dcv_v = dcv_ref[...]
        dbuf[0:tm, :] = dcv_v
        dbuf[tm:, :] = jnp.where(i < nt - 1, dcvn_ref[...], 0.0)
        dt = w_ref[2:3, :] * dcv_v + w_ref[1:2, :] * dbuf[pl.ds(1, tm), :] + w_ref[0:1, :] * dbuf[pl.ds(2, tm), :]
        dcg_ref[...] = (dt * hb).astype(bf16)
        dhb_ref[...] = (dt * cg).astype(bf16)
        g2 = jnp.sum(dcv_v * t, axis=0, keepdims=True)
        g1 = jnp.sum(dcv_v * tbuf[pl.ds(7, tm), :], axis=0, keepdims=True)
        g0 = jnp.sum(dcv_v * tbuf[pl.ds(6, tm), :], axis=0, keepdims=True)
        part = jnp.concatenate([g0, g1, g2, jnp.zeros((5, 512), f32)], axis=0)

        @pl.when(i == 0)
        def _():
            gw_ref[...] = part

        @pl.when(i > 0)
        def _():
            gw_ref[...] += part

    row = lambda w, c: pl.BlockSpec((tm, w), lambda i: (i, c))
    return pl.pallas_call(
        body, name="mix0_bwd_b", grid=(nt,),
        in_specs=[row(512, 0), _next_halo(tm, 8, 0, S), row(512, E_CG // 512), row(512, E_HB // 512),
                  _prev_halo(tm, 8, E_CG // 512), _prev_halo(tm, 8, E_HB // 512),
                  pl.BlockSpec((SC_WIDTH, 512), lambda i: (0, 0))],
        out_specs=[row(512, 0), row(512, 0), pl.BlockSpec((8, 512), lambda i: (0, 0))],
        out_shape=[jax.ShapeDtypeStruct((S, 512), bf16), jax.ShapeDtypeStruct((S, 512), bf16),
                   jax.ShapeDtypeStruct((8, 512), f32)],
        scratch_shapes=[pltpu.VMEM((tm + 8, 512), f32), pltpu.VMEM((tm + 8, 512), f32)],
        compiler_params=_cp(("arbitrary",)),
    )(dcv, dcv, proj, proj, proj, proj, conv_w)


def _qk_bwd(dq_g, dk_g, dv_g, proj, tabs, nw, hm, dbg, dcg, dhb, dz):
    S = proj.shape[0]
    tm = _tile(S, 256)

    def body(*refs):
        d_refs = refs[0:6]
        dv_refs = refs[6:9]
        x_ref, c_ref, s1_ref, s2_ref, nw_ref, m_ref, dbg_ref, dcg_ref, dhb_ref, dz_ref, o_ref, gw_ref = refs[9:]
        i = pl.program_id(0)
        c, s1, s2, m = c_ref[...], s1_ref[...], s2_ref[...], m_ref[...]
        accs = []
        for kind in range(2):
            w = nw_ref[kind:kind + 1, :]
            acc = jnp.zeros((1, LANES), f32)
            for gi in range(N_GROUPS):
                for p in range(4):
                    col = kind * 1536 + gi * 512 + p * LANES
                    dout = d_refs[kind * 3 + gi][:, p * LANES:(p + 1) * LANES]
                    t = x_ref[:, col:col + LANES]
                    dthat = (dout * c + pltpu.roll(dout * s1, LANES - ROT_HALF, axis=1)
                             + pltpu.roll(dout * s2, ROT_HALF, axis=1))
                    r = lax.rsqrt(_head_mean(t * t, m) + EPS)
                    tn = t * r
                    acc = acc + jnp.sum(dthat * tn, axis=0, keepdims=True)
                    dtn = dthat * w
                    o_ref[:, col:col + LANES] = (r * (dtn - tn * _head_mean(dtn * tn, m))).astype(bf16)
            accs.append(acc + pltpu.roll(acc, HEAD_DIM, axis=1))
        for gi in range(N_GROUPS):
            o_ref[:, E_V + gi * 512:E_V + (gi + 1) * 512] = dv_refs[gi][...].astype(bf16)
        o_ref[:, E_BG:E_CG] = dbg_ref[...]
        o_ref[:, E_CG:E_HB] = dcg_ref[...]
        o_ref[:, E_HB:E_Z] = dhb_ref[...]
        o_ref[:, E_Z:] = dz_ref[...]
        part = jnp.concatenate([accs[0], accs[1], jnp.zeros((6, LANES), f32)], axis=0)

        @pl.when(i == 0)
        def _():
            gw_ref[...] = part

        @pl.when(i > 0)
        def _():
            gw_ref[...] += part

    row = lambda w, c: pl.BlockSpec((tm, w), lambda i: (i, c))
    tab = row(LANES, 0)
    return pl.pallas_call(
        body, name="qk_bwd", grid=(S // tm,),
        in_specs=[row(512, 0)] * 9 + [row(3072, 0), tab, tab, tab, pl.BlockSpec((2, LANES), lambda i: (0, 0)),
                                      pl.BlockSpec((LANES, LANES), lambda i: (0, 0)),
                                      row(512, 0), row(512, 0), row(512, 0), row(1024, 0)],
        out_specs=[row(EVEN_IN, 0), pl.BlockSpec((8, LANES), lambda i: (0, 0))],
        out_shape=[jax.ShapeDtypeStruct((S, EVEN_IN), bf16), jax.ShapeDtypeStruct((8, LANES), f32)],
        compiler_params=_cp(("arbitrary",)),
    )(*dq_g, *dk_g, *dv_g, proj, *tabs, nw, hm, dbg, dcg, dhb, dz)


def _inv_count(i, tm, p):
    rowg = lax.broadcasted_iota(jnp.int32, (tm, 1), 0) + i * tm
    return 1.0 / jnp.minimum(rowg + 1, p).astype(f32)


def _layer_norm_stats(c):
    mu = jnp.mean(c, axis=-1, keepdims=True)
    cen = c - mu
    rstd = lax.rsqrt(jnp.mean(cen * cen, axis=-1, keepdims=True) + EPS)
    return cen * rstd, rstd


def _fill_pool_buf(i, ubuf, uc_ref, uch_ref):
    ubuf[0:16, :] = jnp.where(i > 0, uch_ref[...], 0.0)
    ubuf[16:, :] = uc_ref[...]


def _pooled(i, tm, ubuf, gi):
    p = POOL_SIZES[gi]
    cols = slice(gi * LANES, (gi + 1) * LANES)
    acc = ubuf[pl.ds(16, tm), cols]
    cur = acc
    for jj in range(1, p):
        acc = acc + ubuf[pl.ds(16 - jj, tm), cols]
    return acc * _inv_count(i, tm, p) - cur


def _fill_glu_buf(i, gbuf, da_ref, dg_ref, dah_ref, dgh_ref):
    gbuf[0:32, :] = jnp.where(i > 0, dah_ref[...] * _sigmoid(dgh_ref[...]), 0.0)
    gbuf[32:, :] = da_ref[...] * _sigmoid(dg_ref[...])


def _shift_copies(buf, sh, tm):
    for b in range(1, 8):
        sh[b - 1] = buf[pl.ds(b, tm + 24), :]


CONV_ROWS = 32


def _window(buf, sh, base, off, rows):
    b = off % 8
    if b == 0:
        return buf[pl.ds(base + off, rows), :]
    return sh[b - 1, pl.ds(base + (off - b), rows), :]


def _mix1_fwd(proj, pool_w, pool_scale, dconv_w, dconv_b, ln_w, ln_b):
    S = proj.shape[0]
    tm = _tile(S, 256)

    def body(uc_ref, uch_ref, da_ref, dg_ref, dah_ref, dgh_ref, za_ref, zb_ref, pw_ref, ps_ref, cw_ref, cb_ref,
             lw_ref, lb_ref, u_ref, c_ref, mc_ref, ubuf, gbuf, gsh):
        i = pl.program_id(0)
        _fill_pool_buf(i, ubuf, uc_ref, uch_ref)
        za = za_ref[...]
        for gi in range(4):
            cols = slice(gi * LANES, (gi + 1) * LANES)
            mc = jnp.dot(_pooled(i, tm, ubuf, gi).astype(bf16), pw_ref[gi], preferred_element_type=f32)
            mc_ref[:, cols] = mc
            zg = za[:, cols]
            u_ref[:, cols] = (mc * ps_ref[:, cols] * (zg * _sigmoid(zg))).astype(bf16)
        _fill_glu_buf(i, gbuf, da_ref, dg_ref, dah_ref, dgh_ref)
        _shift_copies(gbuf, gsh, tm)
        c = jnp.zeros((tm, 512), f32) + cb_ref[...]
        for k in range(D_CONV):
            c = c + cw_ref[k:k + 1, :] * _window(gbuf, gsh, 0, 32 - (D_CONV - 1) + k, tm)
        c_ref[...] = c
        yhat, _ = _layer_norm_stats(c)
        l = yhat * lw_ref[...] + lb_ref[...]
        zb = zb_ref[...]
        u_ref[:, 512:] = (l * _sigmoid(l) * (zb * _sigmoid(zb))).astype(bf16)

    row = lambda w, c: pl.BlockSpec((tm, w), lambda i: (i, c))
    vec = pl.BlockSpec((1, 512), lambda i: (0, 0))
    return pl.pallas_call(
        body, name="mix1_fwd", grid=(S // tm,),
        in_specs=[row(512, 0), _prev_halo(tm, 16, 0), row(512, 1), row(512, 2), _prev_halo(tm, 32, 1), _prev_halo(tm, 32, 2),
                  row(512, 3), row(512, 4), pl.BlockSpec((4, LANES, LANES), lambda i: (0, 0, 0)), vec,
                  pl.BlockSpec((D_CONV, 512), lambda i: (0, 0)), vec, vec, vec],
        out_specs=[row(1024, 0), row(512, 0), row(512, 0)],
        out_shape=[jax.ShapeDtypeStruct((S, D_MODEL), bf16), jax.ShapeDtypeStruct((S, 512), f32),
                   jax.ShapeDtypeStruct((S, 512), f32)],
        scratch_shapes=[pltpu.VMEM((tm + 16, 512), f32), pltpu.VMEM((tm + 32, 512), f32),
                        pltpu.VMEM((7, tm + 24, 512), f32)],
        compiler_params=_cp(("parallel",)),
    )(proj, proj, proj, proj, proj, proj, proj, proj, pool_w, pool_scale, dconv_w, dconv_b, ln_w, ln_b)


def _mix1_bwd_a(dy, w_out, proj, c, mc, pool_w, pool_scale, ln_w, ln_b):
    S = proj.shape[0]
    tm = _tile(S, 256)

    def body(dy_ref, wo_ref, za_ref, zb_ref, c_ref, mc_ref, pw_ref, ps_ref, lw_ref, lb_ref,
             dz_ref, dc_ref, dpl_ref, dmc_ref, acc_ref):
        i = pl.program_id(0)
        du_v = _d_gate_in(dy_ref, wo_ref)
        ps = ps_ref[...]
        za = za_ref[...]
        sga = _sigmoid(za)
        mcv = mc_ref[...]
        dz_ref[:, :512] = (du_v[:, :512] * (mcv * ps) * _dsilu(za, sga)).astype(bf16)
        dyc = du_v[:, :512] * (za * sga)
        g_ps = jnp.sum(dyc * mcv, axis=0, keepdims=True)
        dmc = (dyc * ps).astype(bf16)
        dmc_ref[...] = dmc
        for gi in range(4):
            cols = slice(gi * LANES, (gi + 1) * LANES)
            dpl_ref[:, cols] = lax.dot_general(dmc[:, cols], pw_ref[gi], (((1,), (1,)), ((), ())), preferred_element_type=f32)
        yhat, rstd = _layer_norm_stats(c_ref[...])
        lw = lw_ref[...]
        l = yhat * lw + lb_ref[...]
        sgl = _sigmoid(l)
        zb = zb_ref[...]
        sgb = _sigmoid(zb)
        dz_ref[:, 512:] = (du_v[:, 512:] * (l * sgl) * _dsilu(zb, sgb)).astype(bf16)
        dl = du_v[:, 512:] * (zb * sgb) * _dsilu(l, sgl)
        g_lb = jnp.sum(dl, axis=0, keepdims=True)
        g_lw = jnp.sum(dl * yhat, axis=0, keepdims=True)
        dyh = dl * lw
        dc = rstd * (dyh - jnp.mean(dyh, axis=-1, keepdims=True) - yhat * jnp.mean(dyh * yhat, axis=-1, keepdims=True))
        dc_ref[...] = dc
        g_db = jnp.sum(dc, axis=0, keepdims=True)
        part = jnp.concatenate([g_ps, g_lw, g_lb, g_db, jnp.zeros((4, 512), f32)], axis=0)

        @pl.when(i == 0)
        def _():
            acc_ref[...] = part

        @pl.when(i > 0)
        def _():
            acc_ref[...] += part

    row = lambda w, c_: pl.BlockSpec((tm, w), lambda i: (i, c_))
    vec = pl.BlockSpec((1, 512), lambda i: (0, 0))
    return pl.pallas_call(
        body, name="mix1_bwd_a", grid=(S // tm,),
        in_specs=[row(1024, 0), pl.BlockSpec((D_MODEL, D_MODEL), lambda i: (0, 0)),
                  row(512, 3), row(512, 4), row(512, 0), row(512, 0),
                  pl.BlockSpec((4, LANES, LANES), lambda i: (0, 0, 0)), vec, vec, vec],
        out_specs=[row(1024, 0), row(512, 0), row(512, 0), row(512, 0), pl.BlockSpec((8, 512), lambda i: (0, 0))],
        out_shape=[jax.ShapeDtypeStruct((S, D_MODEL), bf16), jax.ShapeDtypeStruct((S, 512), f32),
                   jax.ShapeDtypeStruct((S, 512), f32), jax.ShapeDtypeStruct((S, 512), bf16),
                   jax.ShapeDtypeStruct((8, 512), f32)],
        compiler_params=_cp(("arbitrary",)),
    )(dy, w_out, proj, proj, c, mc, pool_w, pool_scale, ln_w, ln_b)


def _mix1_bwd_b(dc, dpl, dmc, dz, proj, dconv_w):
    S = proj.shape[0]
    tm = _tile(S, 256)
    nt = S // tm

    def body(dc_ref, dcn_ref, dpl_ref, dpn_ref, dmc_ref, dz_ref, uc_ref, uch_ref, da_ref, dg_ref,
             cw_ref, o_ref, gcw_ref, gpw_ref, ubuf, dcbuf, dpbuf, dcsh, gacc):
        i = pl.program_id(0)
        last = i == nt - 1
        _fill_pool_buf(i, ubuf, uc_ref, uch_ref)
        dcbuf[0:tm, :] = dc_ref[...]
        dcbuf[tm:, :] = jnp.where(last, 0.0, dcn_ref[...])
        _shift_copies(dcbuf, dcsh, tm)
        dpl_v = dpl_ref[...]
        for gi in range(4):
            p = POOL_SIZES[gi]
            cols = slice(gi * LANES, (gi + 1) * LANES)
            dpbuf[0:tm, cols] = dpl_v[:, cols] * _inv_count(i, tm, p)
            dpbuf[tm:, cols] = jnp.where(last, 0.0, dpn_ref[:, cols] * (1.0 / p))
        gpw = []
        for gi in range(4):
            p = POOL_SIZES[gi]
            cols = slice(gi * LANES, (gi + 1) * LANES)
            acc = -dpl_v[:, cols]
            for jj in range(p):
                acc = acc + dpbuf[pl.ds(jj, tm), cols]
            o_ref[:, cols] = acc.astype(bf16)
            pooled = _pooled(i, tm, ubuf, gi).astype(bf16)
            gpw.append(lax.dot_general(pooled, dmc_ref[:, cols], (((0,), (0,)), ((), ())), preferred_element_type=f32))
        gacc[...] = jnp.zeros_like(gacc)

        def conv_rows(ci, carry):
            base = pl.multiple_of(ci * CONV_ROWS, CONV_ROWS)
            da = da_ref[pl.ds(base, CONV_ROWS), :]
            sg = _sigmoid(dg_ref[pl.ds(base, CONV_ROWS), :])
            gl = da * sg
            dgl = jnp.zeros((CONV_ROWS, 512), f32)
            for k in range(D_CONV):
                win = _window(dcbuf, dcsh, base, D_CONV - 1 - k, CONV_ROWS)
                dgl = dgl + cw_ref[k:k + 1, :] * win
                gacc[k] += jnp.sum((gl * win).reshape(CONV_ROWS // 8, 8, 512), axis=0)
            o_ref[pl.ds(base, CONV_ROWS), O_DA:O_DG] = (dgl * sg).astype(bf16)
            o_ref[pl.ds(base, CONV_ROWS), O_DG:O_Z] = (dgl * da * sg * (1.0 - sg)).astype(bf16)
            return carry

        lax.fori_loop(0, tm // CONV_ROWS, conv_rows, 0)
        o_ref[:, O_Z:] = dz_ref[...]
        gcw_part = jnp.concatenate(
            [jnp.sum(gacc[k], axis=0, keepdims=True) for k in range(D_CONV)] + [jnp.zeros((1, 512), f32)], axis=0)

        @pl.when(i == 0)
        def _():
            gcw_ref[...] = gcw_part
            for gi in range(4):
                gpw_ref[gi] = gpw[gi]

        @pl.when(i > 0)
        def _():
            gcw_ref[...] += gcw_part
            for gi in range(4):
                gpw_ref[gi] += gpw[gi]

    row = lambda w, c_: pl.BlockSpec((tm, w), lambda i: (i, c_))
    return pl.pallas_call(
        body, name="mix1_bwd_b", grid=(nt,),
        in_specs=[row(512, 0), _next_halo(tm, 32, 0, S), row(512, 0), _next_halo(tm, 16, 0, S), row(512, 0), row(1024, 0),
                  row(512, 0), _prev_halo(tm, 16, 0), row(512, 1), row(512, 2),
                  pl.BlockSpec((D_CONV, 512), lambda i: (0, 0))],
        out_specs=[row(ODD_IN, 0), pl.BlockSpec((32, 512), lambda i: (0, 0)),
                   pl.BlockSpec((4, LANES, LANES), lambda i: (0, 0, 0))],
        out_shape=[jax.ShapeDtypeStruct((S, ODD_IN), bf16), jax.ShapeDtypeStruct((32, 512), f32),
                   jax.ShapeDtypeStruct((4, LANES, LANES), f32)],
        scratch_shapes=[pltpu.VMEM((tm + 16, 512), f32), pltpu.VMEM((tm + 32, 512), f32),
                        pltpu.VMEM((tm + 16, 512), f32), pltpu.VMEM((7, tm + 24, 512), f32),
                        pltpu.VMEM((D_CONV, 8, 512), f32)],
        compiler_params=_cp(("arbitrary",)),
    )(dc, dc, dpl, dpl, dmc, dz, proj, proj, proj, proj, dconv_w)


_SMALL_LATE = ["e_q_norm_w", "e_k_norm_w", "e_conv_w", "o_norm_w", "o_pool_w", "o_pool_scale", "o_dconv_w", "o_dconv_b",
               "o_ln_w", "o_ln_b"]


def _local_step(x, pos_col, target, w, dist=None):
    hm = _head_mean_matrix()
    nw = jnp.concatenate([jnp.tile(w["e_q_norm_w"], (1, 2)), jnp.tile(w["e_k_norm_w"], (1, 2))], axis=0)
    tabs = _rope_tables(pos_col)
    pool_wb = w["o_pool_w"].astype(bf16)
    e_norm_w, e_w_in = w["e_norm_w"], w["e_w_in"]

    if dist is None:
        proj0, qk, h0 = _in_proj0(x, e_norm_w, e_w_in, tabs, nw, hm)
    else:
        proj0, qk, h0, gathered = _in_proj0(x, e_norm_w, e_w_in, tabs, nw, hm, fuse=([], dist[0]))
        w = {**w, **dist[1](gathered)}
    e_conv_w, e_w_out, o_norm_w, o_w_in, o_w_out = w["e_conv_w"], w["e_w_out"], w["o_norm_w"], w["o_w_in"], w["o_w_out"]
    o_pool_scale, o_dconv_w, o_dconv_b, o_ln_w, o_ln_b = (w[k] for k in ("o_pool_scale", "o_dconv_w", "o_dconv_b", "o_ln_w", "o_ln_b"))
    o_g, lse_g = [], []
    for g in range(N_GROUPS):
        o, l = _attn_fwd_local(qk, proj0) if g == 0 else _attn_fwd_dil(qk, proj0, g, name=f"attn_fwd{g}")
        o_g.append(o)
        lse_g.append(l)
    u0, o_a, lt = _mix0_fwd(o_g, lse_g, proj0, e_conv_w)
    x1, h1 = _out_proj_rms(u0, e_w_out, x, o_norm_w, name="out_proj0")
    o_w_in3 = o_w_in.reshape(1, D_MODEL, ODD_IN)
    proj1 = _mm_nn_resident(h1, o_w_in3, name="in_proj1", tm=512)
    u1, c1, mc1 = _mix1_fwd(proj1, pool_wb, o_pool_scale, o_dconv_w, o_dconv_b, o_ln_w, o_ln_b)
    dy, dyb, loss = _mm_out_loss(u1, o_w_out, x1, target, name="out_proj1_loss")
    g_o_w_out = _mm_tn(u1, dyb, name="g_w_out1", out_dtype=bf16)
    dz1, dc1, dpl1, dmc1, sums1 = _mix1_bwd_a(dyb, o_w_out, proj1, c1, mc1, pool_wb, o_pool_scale, o_ln_w, o_ln_b)
    dproj1, g_dconv_w, g_pool_w = _mix1_bwd_b(dc1, dpl1, dmc1, dz1, proj1, o_dconv_w)
    g_o_w_in = _mm_tn(h1, dproj1, name="g_w_in1", out_dtype=bf16)
    d1, d1b, g_o_norm = _mm_nt_rms_bwd(dproj1, o_w_in3, x1, o_norm_w, dy, name="d_h1")
    g_e_w_out = _mm_tn(u0, d1b, name="g_w_out0", out_dtype=bf16)
    dz0, do_a, dsum, dbg, dcv = _mix0_bwd_a(d1b, e_w_out, proj0, o_a, e_conv_w)
    dcg, dhb, g_conv_w = _mix0_bwd_b(dcv, proj0, e_conv_w)
    fuse_a = None if dist is None else (
        [g_e_w_out.reshape(N_DEV, D_MODEL // N_DEV, D_MODEL),
         jnp.moveaxis(g_o_w_in.reshape(D_MODEL, N_DEV, ODD_IN // N_DEV), 1, 0),
         g_o_w_out.reshape(N_DEV, D_MODEL // N_DEV, D_MODEL)], [])
    dq_g, dk_g, dv_g = [], [], []
    for g in range(N_GROUPS):
        if g == 0:
            dqkv = _attn_bwd_local(qk, proj0, do_a, lt, dsum, fuse=fuse_a)
            if dist is not None:
                dqkv, recv_a = dqkv
            dq, dk, dv = dqkv
        else:
            dq, dk, dv = _attn_bwd_dil(qk, proj0, do_a, lt, dsum, g, name=f"attn_bwd{g}")
        dq_g.append(dq)
        dk_g.append(dk)
        dv_g.append(dv)
    dproj0, g_qk_norm = _qk_bwd(dq_g, dk_g, dv_g, proj0, tabs, nw, hm, dbg, dcg, dhb, dz0)
    half = D_MODEL // 2
    g_e_w_in_a = _mm_tn(h0, dproj0, name="g_w_in0a", out_dtype=bf16, chunks=N_DEV, a_cols=(0, half))
    if dist is None:
        g_e_w_in_b = _mm_tn(h0, dproj0, name="g_w_in0b", out_dtype=bf16, chunks=N_DEV, a_cols=(1, half))
    else:
        g_e_w_in_b, recv_b0 = _mm_tn(h0, dproj0, name="g_w_in0b", out_dtype=bf16, chunks=N_DEV, a_cols=(1, half),
                                     fuse=([g_e_w_in_a], []))
    grads = dict(
        e_q_norm_w=g_qk_norm[0:1, :HEAD_DIM], e_k_norm_w=g_qk_norm[1:2, :HEAD_DIM],
        e_conv_w=g_conv_w[:SC_WIDTH], e_w_out=g_e_w_out,
        o_norm_w=g_o_norm, o_w_in=g_o_w_in, o_pool_w=g_pool_w,
        o_pool_scale=sums1[0:1], o_dconv_w=g_dconv_w[:D_CONV], o_dconv_b=sums1[3:4],
        o_ln_w=sums1[1:2], o_ln_b=sums1[2:3], o_w_out=g_o_w_out)
    if dist is None:
        grad_x, _, grads["e_norm_w"] = _mm_nt_rms_bwd(dproj0, e_w_in, x, e_norm_w, d1, name="d_h0", tm=512)
        grads["e_w_in"] = jnp.concatenate([g_e_w_in_a, g_e_w_in_b], axis=1)
        return loss, grad_x, grads
    small_late, offs = _pack_rows([grads[n_] for n_ in _SMALL_LATE])
    grad_x, _, g_e_norm, recv_b = _mm_nt_rms_bwd(dproj0, e_w_in, x, e_norm_w, d1, name="d_h0", tm=512,
                                                 fuse=([g_e_w_in_b], [small_late]))
    recv_c = _exchange([], [jnp.concatenate([g_e_norm.reshape(8, LANES), loss], axis=0)], name="exchange_e_norm_loss")
    recv = dict(e_w_out=[recv_a[0]], o_w_in=[recv_a[1]], o_w_out=[recv_a[2]], e_w_in=[recv_b0[0], recv_b[0]],
                small_late=recv_b[1], e_norm_w=recv_c[0])
    return loss, grad_x, recv, {n_: (off, grads[n_].shape) for n_, off in zip(_SMALL_LATE, offs)}


_MESH_ID = pl.DeviceIdType.MESH
_HBM = pl.BlockSpec(memory_space=pl.ANY)


def _all_gather(arrs, *, name):
    n = len(arrs)

    def body(*refs):
        ins, outs = refs[:n], refs[n:2 * n]
        send_sems, recv_sems, local_sems = refs[2 * n:]
        x, y, c = _place()
        me, sibling = (x, y, c), (x, y, 1 - c)
        chips = [(1 - x, y), (x, 1 - y), (1 - x, 1 - y)]

        def slot(t, px, py, pc):
            return outs[t].at[4 * px + 2 * py + pc]

        def copy(t, k, block, to, src=None):
            dst = slot(t, *block)
            return pltpu.make_async_remote_copy(
                src_ref=dst if src is None else src, dst_ref=dst,
                send_sem=send_sems.at[7 * t + k], recv_sem=recv_sems.at[7 * t + k],
                device_id=to, device_id_type=_MESH_ID)

        mine = [pltpu.make_async_copy(ins[t], slot(t, *me), local_sems.at[t]) for t in range(n)]
        for cp in mine:
            cp.start()
        first = []
        for t in range(n):
            first.append(copy(t, 0, me, sibling, src=ins[t]))
            first += [copy(t, 1 + j, me, (*chip, c), src=ins[t]) for j, chip in enumerate(chips)]
        for cp in first:
            cp.start()
        passed = []
        for j, chip in enumerate(chips):
            for t in range(n):
                copy(t, 1 + j, (*chip, c), me).wait_recv()
                fwd = copy(t, 4 + j, (*chip, c), sibling)
                fwd.start()
                passed.append(fwd)
        for t in range(n):
            copy(t, 0, sibling, me).wait_recv()
            for j, chip in enumerate(chips):
                copy(t, 4 + j, (*chip, 1 - c), me).wait_recv()
        for cp in first + passed:
            cp.wait_send()
        for cp in mine:
            cp.wait()

    return pl.pallas_call(
        body, name=name,
        in_specs=[_HBM] * n, out_specs=[_HBM] * n,
        out_shape=[jax.ShapeDtypeStruct((N_DEV, *a.shape), a.dtype) for a in arrs],
        scratch_shapes=[pltpu.SemaphoreType.DMA((7 * n,)), pltpu.SemaphoreType.DMA((7 * n,)),
                        pltpu.SemaphoreType.DMA((n,))],
    )(*arrs)


def _exchange(chunked, whole, *, name):
    arrs = list(chunked) + list(whole)
    n = len(arrs)

    def body(*refs):
        start, wait = _exchange_plan(refs[:n], refs[n:2 * n], *refs[2 * n:], len(chunked))
        start()
        wait()

    return pl.pallas_call(
        body, name=name, in_specs=[_HBM] * n, out_specs=[_HBM] * n,
        out_shape=_exchange_out_shapes(chunked, whole), scratch_shapes=_exchange_sems(n),
    )(*arrs)


def _adamw(w, g, m, v):
    m2 = ADAM_B1 * m + (1.0 - ADAM_B1) * g
    v2 = ADAM_B2 * v + (1.0 - ADAM_B2) * (g * g)
    m_hat = m2 / (1.0 - ADAM_B1 ** ADAM_STEP)
    v_hat = v2 / (1.0 - ADAM_B2 ** ADAM_STEP)
    delta = -ADAM_LR * (m_hat / (jnp.sqrt(v_hat) + ADAM_EPS) + ADAM_WD * w)
    return delta, m2, v2


def _sum_adamw(parts, w, m, v, *, name):
    R, C = w.shape
    nsplit = len(parts)
    rp = R // nsplit
    tr = _tile(rp, 256)
    npt = rp // tr

    def body(*refs):
        p_refs = refs[:nsplit]
        w_ref, m_ref, v_ref, g_ref, d_ref, nm_ref, nv_ref = refs[nsplit:]
        h = pl.program_id(0)
        g = None
        for i in range(N_DEV):
            pi = p_refs[0][i]
            for q in range(1, nsplit):
                pi = jnp.where(h == q, p_refs[q][i], pi)
            g = pi.astype(f32) if g is None else g + pi.astype(f32)
        g_ref[...] = g
        d_ref[...], nm_ref[...], nv_ref[...] = _adamw(w_ref[...], g, m_ref[...], v_ref[...])

    def part_spec(q):
        return pl.BlockSpec((N_DEV, tr, C), lambda h, i: (0, jnp.where(h == q, i, 0), 0))

    spec = pl.BlockSpec((tr, C), lambda h, i: (h * npt + i, 0))
    return pl.pallas_call(
        body, name=name, grid=(nsplit, npt),
        in_specs=[part_spec(q) for q in range(nsplit)] + [spec, spec, spec],
        out_specs=[spec] * 4, out_shape=[jax.ShapeDtypeStruct((R, C), f32)] * 4,
        compiler_params=_cp(("parallel", "parallel")),
    )(*parts, w, m, v)


def _sum_parts(parts, *, name):
    _, R, C = parts.shape

    def body(p_ref, o_ref):
        g = p_ref[0]
        for i in range(1, N_DEV):
            g = g + p_ref[i]
        o_ref[...] = g

    return pl.pallas_call(body, name=name, out_shape=jax.ShapeDtypeStruct((R, C), f32),
                          compiler_params=pltpu.CompilerParams(vmem_limit_bytes=VMEM_LIMIT))(parts)


def _adamw_small(ws, gs, ms, vs):
    n = len(ws)

    def body(*refs):
        w_r, g_r, m_r, v_r = refs[:n], refs[n:2 * n], refs[2 * n:3 * n], refs[3 * n:4 * n]
        d_r, nm_r, nv_r = refs[4 * n:5 * n], refs[5 * n:6 * n], refs[6 * n:7 * n]
        for t in range(n):
            d_r[t][...], nm_r[t][...], nv_r[t][...] = _adamw(w_r[t][...], g_r[t][...], m_r[t][...], v_r[t][...])

    shapes = [jax.ShapeDtypeStruct(w.shape, f32) for w in ws]
    outs = pl.pallas_call(body, name="adamw_small", out_shape=shapes * 3)(*ws, *gs, *ms, *vs)
    return outs[:n], outs[n:2 * n], outs[2 * n:]


_WEIGHTS = ["e_norm_w", "e_w_in", "e_q_norm_w", "e_k_norm_w", "e_conv_w", "e_w_out", "o_norm_w", "o_w_in", "o_pool_w",
            "o_pool_scale", "o_dconv_w", "o_dconv_b", "o_ln_w", "o_ln_b", "o_w_out"]
_BIG = ["e_w_in", "e_w_out", "o_w_in", "o_w_out"]
_SMALL_SHARDED = ["e_conv_w", "o_norm_w", "o_pool_scale", "o_dconv_w", "o_dconv_b", "o_ln_w", "o_ln_b"]
_SMALL_ALL = ["e_norm_w", "e_q_norm_w", "e_k_norm_w", "e_conv_w", "o_norm_w", "o_pool_w", "o_pool_scale", "o_dconv_w",
              "o_dconv_b", "o_ln_w", "o_ln_b"]


def _pack_rows(pieces):
    rows, offs, r0 = [], [], 0
    for p in pieces:
        flat = p.reshape(-1)
        nr = -(-flat.shape[0] // (8 * LANES)) * 8
        rows.append(jnp.pad(flat, (0, nr * LANES - flat.shape[0])).reshape(nr, LANES))
        offs.append((r0, nr))
        r0 += nr
    return jnp.concatenate(rows, axis=0), offs


def _unpack_rows(buf, off, shape):
    r0, nr = off
    size = int(np.prod(shape))
    return buf[..., r0:r0 + nr, :].reshape(*buf.shape[:-2], nr * LANES)[..., :size].reshape(*buf.shape[:-2], *shape)


def kernel(x, positions, e_norm_w, e_w_in, e_q_norm_w, e_k_norm_w, e_conv_w, e_w_out, o_norm_w, o_w_in, o_pool_w, o_pool_scale, o_dconv_w, o_dconv_b, o_ln_w, o_ln_b, o_w_out, loss_target, m_e_norm_w, m_e_w_in, m_e_q_norm_w, m_e_k_norm_w, m_e_conv_w, m_e_w_out, m_o_norm_w, m_o_w_in, m_o_pool_w, m_o_pool_scale, m_o_dconv_w, m_o_dconv_b, m_o_ln_w, m_o_ln_b, m_o_w_out, v_e_norm_w, v_e_w_in, v_e_q_norm_w, v_e_k_norm_w, v_e_conv_w, v_e_w_out, v_o_norm_w, v_o_w_in, v_o_pool_w, v_o_pool_scale, v_o_dconv_w, v_o_dconv_b, v_o_ln_w, v_o_ln_b, v_o_w_out):
    w = dict(e_norm_w=e_norm_w, e_w_in=e_w_in, e_q_norm_w=e_q_norm_w, e_k_norm_w=e_k_norm_w, e_conv_w=e_conv_w,
             e_w_out=e_w_out, o_norm_w=o_norm_w, o_w_in=o_w_in, o_pool_w=o_pool_w, o_pool_scale=o_pool_scale,
             o_dconv_w=o_dconv_w, o_dconv_b=o_dconv_b, o_ln_w=o_ln_w, o_ln_b=o_ln_b, o_w_out=o_w_out)
    m = dict(e_norm_w=m_e_norm_w, e_w_in=m_e_w_in, e_q_norm_w=m_e_q_norm_w, e_k_norm_w=m_e_k_norm_w, e_conv_w=m_e_conv_w,
             e_w_out=m_e_w_out, o_norm_w=m_o_norm_w, o_w_in=m_o_w_in, o_pool_w=m_o_pool_w, o_pool_scale=m_o_pool_scale,
             o_dconv_w=m_o_dconv_w, o_dconv_b=m_o_dconv_b, o_ln_w=m_o_ln_w, o_ln_b=m_o_ln_b, o_w_out=m_o_w_out)
    v = dict(e_norm_w=v_e_norm_w, e_w_in=v_e_w_in, e_q_norm_w=v_e_q_norm_w, e_k_norm_w=v_e_k_norm_w, e_conv_w=v_e_conv_w,
             e_w_out=v_e_w_out, o_norm_w=v_o_norm_w, o_w_in=v_o_w_in, o_pool_w=v_o_pool_w, o_pool_scale=v_o_pool_scale,
             o_dconv_w=v_o_dconv_w, o_dconv_b=v_o_dconv_b, o_ln_w=v_o_ln_w, o_ln_b=v_o_ln_b, o_w_out=v_o_w_out)
    S = x.shape[1]
    me = 4 * lax.axis_index("x") + 2 * lax.axis_index("y") + lax.axis_index("c")

    small_local, small_offs = _pack_rows([w[n_] for n_ in _SMALL_SHARDED])
    g_e_in, = _all_gather([w["e_w_in"][0].astype(bf16)], name="gather_e_w_in")
    rest_local = [w["e_w_out"][0].astype(bf16), w["o_w_in"][0].astype(bf16), w["o_w_out"][0].astype(bf16), small_local]

    def unpack_rest(gathered):
        g_e_out, g_o_in, g_o_out, g_small = gathered
        full = {}
        for n_, off in zip(_SMALL_SHARDED, small_offs):
            shard = _unpack_rows(g_small, off, w[n_].shape[1:])
            full[n_] = jnp.moveaxis(shard, 0, -2).reshape(*shard.shape[1:-1], N_DEV * shard.shape[-1])
        return dict(
            e_conv_w=full["e_conv_w"], e_w_out=g_e_out.reshape(D_MODEL, D_MODEL), o_norm_w=full["o_norm_w"].reshape(1, D_MODEL),
            o_w_in=jnp.moveaxis(g_o_in, 0, 1).reshape(D_MODEL, ODD_IN), o_pool_scale=full["o_pool_scale"].reshape(1, 512),
            o_dconv_w=full["o_dconv_w"], o_dconv_b=full["o_dconv_b"].reshape(1, 512), o_ln_w=full["o_ln_w"].reshape(1, 512),
            o_ln_b=full["o_ln_b"].reshape(1, 512), o_w_out=g_o_out.reshape(D_MODEL, D_MODEL))

    loss_blk, grad_x, recv, small_where = _local_step(
        x[0], positions.reshape(S, 1), loss_target[0],
        dict(e_norm_w=w["e_norm_w"], e_w_in=g_e_in, e_q_norm_w=w["e_q_norm_w"], e_k_norm_w=w["e_k_norm_w"],
             o_pool_w=w["o_pool_w"][0]),
        dist=(rest_local, unpack_rest))

    out_g, out_d, out_m, out_v = {}, {}, {}, {}
    for n_ in _BIG:
        res = _sum_adamw(recv[n_], w[n_][0], m[n_][0], v[n_][0], name="adamw_" + n_)
        out_g[n_], out_d[n_], out_m[n_], out_v[n_] = [r[None] for r in res]
    small_sum = _sum_parts(recv["small_late"], name="sum_small_grads")
    last_sum = _sum_parts(recv["e_norm_w"], name="sum_e_norm_grad_loss")
    loss = last_sum[8, 0]
    gs = []
    for n_ in _SMALL_ALL:
        if n_ == "e_norm_w":
            gs.append(last_sum[:8].reshape(w[n_].shape))
            continue
        off, shape = small_where[n_]
        gfull = _unpack_rows(small_sum, off, shape)
        if n_ in _SMALL_SHARDED:
            width = w[n_].shape[-1]
            gfull = lax.dynamic_slice_in_dim(gfull, me * width, width, axis=gfull.ndim - 1)
        gs.append(gfull.reshape(w[n_].shape))
    ds, nms, nvs = _adamw_small([w[n_] for n_ in _SMALL_ALL], gs, [m[n_] for n_ in _SMALL_ALL], [v[n_] for n_ in _SMALL_ALL])
    for n_, g_, d_, nm_, nv_ in zip(_SMALL_ALL, gs, ds, nms, nvs):
        out_g[n_], out_d[n_], out_m[n_], out_v[n_] = g_, d_, nm_, nv_

    return (loss, grad_x[None], *[out_g[n_] for n_ in _WEIGHTS], *[out_d[n_] for n_ in _WEIGHTS],
            *[out_m[n_] for n_ in _WEIGHTS], *[out_v[n_] for n_ in _WEIGHTS])
```

```python
import functools

import numpy as np
import jax
import jax.numpy as jnp
from jax import lax
from jax.experimental import pallas as pl
from jax.experimental.pallas import tpu as pltpu

f32 = jnp.float32
bf16 = jnp.bfloat16

D_MODEL = 1024
HEAD_DIM = 64
N_GROUPS = 3
DILATIONS = (1, 4, 16)
QBLK = 128
A_WIDTH = 512
EVEN_IN = 7168
ODD_IN = 2560
POOL_SIZES = (2, 4, 8, 16)
D_CONV = 31
SC_WIDTH = 3
ROT_HALF = 8
ROPE_THETA = 500000.0
EPS = 1e-6
NEG = -1e30
SCALE = HEAD_DIM ** -0.5
N_DEV = 8
LANES = 128
VMEM_LIMIT = 48 * 1024 * 1024

ADAM_LR = 0.001
ADAM_B1 = 0.9
ADAM_B2 = 0.999
ADAM_EPS = 1e-08
ADAM_WD = 0.01
ADAM_STEP = 10

E_Q, E_K, E_V, E_BG, E_CG, E_HB, E_Z = 0, 1536, 3072, 4608, 5120, 5632, 6144
O_UC, O_DA, O_DG, O_Z = 0, 512, 1024, 1536


def _cp(sem):
    return pltpu.CompilerParams(dimension_semantics=sem, vmem_limit_bytes=VMEM_LIMIT)


_HBM_ANY = pl.BlockSpec(memory_space=pl.ANY)


def _sigmoid(z):
    return 1.0 / (1.0 + jnp.exp(-z))


def _tile(n, pref):
    t = pref
    while n % t:
        t //= 2
    return t


def _place():
    return lax.axis_index("x"), lax.axis_index("y"), lax.axis_index("c")


def _exchange_plan(ins, outs, send_sems, recv_sems, local_sems, nc):
    n = len(ins)
    x, y, c = _place()
    me_i = 4 * x + 2 * y + c

    def src(t, dev_i):
        return ins[t].at[dev_i] if t < nc else ins[t]

    def copies(arriving):
        cps = []
        for m in range(1, N_DEV):
            px = 1 - x if m & 4 else x
            py = 1 - y if m & 2 else y
            pc = 1 - c if m & 1 else c
            peer_i = 4 * px + 2 * py + pc
            for t in range(n):
                cps.append(pltpu.make_async_remote_copy(
                    src_ref=src(t, peer_i), dst_ref=outs[t].at[peer_i if arriving else me_i],
                    send_sem=send_sems.at[7 * t + m - 1], recv_sem=recv_sems.at[7 * t + m - 1],
                    device_id=(x, y, c) if arriving else (px, py, pc), device_id_type=pl.DeviceIdType.MESH))
        return cps

    def mine():
        return [pltpu.make_async_copy(src(t, me_i), outs[t].at[me_i], local_sems.at[t]) for t in range(n)]

    def start():
        for cp in mine() + copies(False):
            cp.start()

    def wait():
        for cp in copies(True):
            cp.wait_recv()
        for cp in copies(False):
            cp.wait_send()
        for cp in mine():
            cp.wait()

    return start, wait


def _exchange_sems(n):
    return [pltpu.SemaphoreType.DMA((7 * n,)), pltpu.SemaphoreType.DMA((7 * n,)), pltpu.SemaphoreType.DMA((n,))]


def _exchange_out_shapes(chunked, whole):
    return ([jax.ShapeDtypeStruct(a.shape, a.dtype) for a in chunked]
            + [jax.ShapeDtypeStruct((N_DEV, *a.shape), a.dtype) for a in whole])


def _grid_call(body, *, name, grid, in_specs, out_specs, out_shape, scratch_shapes, sem, args, fuse=None):
    if fuse is None:
        return pl.pallas_call(body, name=name, grid=grid, in_specs=in_specs, out_specs=out_specs, out_shape=out_shape,
                              scratch_shapes=scratch_shapes, compiler_params=_cp(sem))(*args)
    chunked, whole = fuse
    ex = list(chunked) + list(whole)
    n, n_in, n_out, n_sc = len(ex), len(in_specs), len(out_specs), len(scratch_shapes)

    def fused(*refs):
        ins, ex_in = refs[:n_in], refs[n_in:n_in + n]
        outs, ex_out = refs[n_in + n:n_in + n + n_out], refs[n_in + n + n_out:n_in + 2 * n + n_out]
        scratch = refs[n_in + 2 * n + n_out:n_in + 2 * n + n_out + n_sc]
        start, wait = _exchange_plan(ex_in, ex_out, *refs[-3:], len(chunked))
        first = functools.reduce(jnp.logical_and, [pl.program_id(a) == 0 for a in range(len(grid))])
        last = functools.reduce(jnp.logical_and, [pl.program_id(a) == g - 1 for a, g in enumerate(grid)])
        pl.when(first)(start)
        body(*ins, *outs, *scratch)
        pl.when(last)(wait)

    res = pl.pallas_call(
        fused, name=name, grid=grid, in_specs=list(in_specs) + [_HBM_ANY] * n,
        out_specs=list(out_specs) + [_HBM_ANY] * n, out_shape=list(out_shape) + _exchange_out_shapes(chunked, whole),
        scratch_shapes=list(scratch_shapes) + _exchange_sems(n),
        compiler_params=_cp(("arbitrary",) * len(grid)))(*args, *ex)
    return res[:n_out], res[n_out:]


def _load_once(src_hbm, dst_vmem, sem):
    @pl.when(pl.program_id(0) == 0)
    def _():
        cp = pltpu.make_async_copy(src_hbm, dst_vmem, sem)
        cp.start()
        cp.wait()


def _mm_nn_resident(a, b, *, name, tm=256, fuse=None):
    M, K = a.shape
    nch, _, tn = b.shape
    tm = _tile(M, tm)

    def body(a_ref, b_hbm, o_ref, bbuf, sem):
        _load_once(b_hbm, bbuf, sem)
        av = a_ref[...]
        for j in range(nch):
            o_ref[:, j * tn:(j + 1) * tn] = jnp.dot(av, bbuf[j], preferred_element_type=f32)

    out = _grid_call(
        body, name=name, grid=(M // tm,), in_specs=[pl.BlockSpec((tm, K), lambda i: (i, 0)), _HBM_ANY],
        out_specs=[pl.BlockSpec((tm, nch * tn), lambda i: (i, 0))],
        out_shape=[jax.ShapeDtypeStruct((M, nch * tn), f32)],
        scratch_shapes=[pltpu.VMEM(b.shape, b.dtype), pltpu.SemaphoreType.DMA],
        sem=("arbitrary",), args=[a, b], fuse=fuse)
    return out[0] if fuse is None else (out[0][0], out[1])


def _mm_tn(a, b, *, name, out_dtype=f32, tn=512, chunks=None, a_cols=None, fuse=None):
    S, Ka = a.shape
    a_blk = 0
    if a_cols is not None:
        a_blk, Ka = a_cols
    N = b.shape[1]
    ts = _tile(S, 4096)
    ns = S // ts
    if chunks:
        tn = N // chunks
        out_spec = pl.BlockSpec((None, Ka, tn), lambda j, s: (j, 0, 0))
        out_shape = jax.ShapeDtypeStruct((chunks, Ka, tn), out_dtype)
    else:
        tn = _tile(N, tn)
        out_spec = pl.BlockSpec((Ka, tn), lambda j, s: (0, j))
        out_shape = jax.ShapeDtypeStruct((Ka, N), out_dtype)

    def body(a_ref, b_ref, o_ref, acc_ref):
        s = pl.program_id(1)
        part = lax.dot_general(a_ref[...], b_ref[...], (((0,), (0,)), ((), ())), preferred_element_type=f32)

        @pl.when(s == 0)
        def _():
            acc_ref[...] = part

        @pl.when(s > 0)
        def _():
            acc_ref[...] += part

        @pl.when(s == ns - 1)
        def _():
            o_ref[...] = acc_ref[...].astype(out_dtype)

    out = _grid_call(
        body, name=name, grid=(N // tn, ns),
        in_specs=[pl.BlockSpec((ts, Ka), lambda j, s: (s, a_blk)), pl.BlockSpec((ts, tn), lambda j, s: (s, j))],
        out_specs=[out_spec], out_shape=[out_shape],
        scratch_shapes=[pltpu.VMEM((Ka, tn), f32)],
        sem=("parallel", "arbitrary"), args=[a, b], fuse=fuse)
    return out[0] if fuse is None else (out[0][0], out[1])


def _mm_out_loss(u, w, x_res, target, *, name):
    M, K = u.shape
    N = w.shape[1]
    tm = _tile(M, 512)
    nm = M // tm

    def body(u_ref, w_ref, x_ref, t_ref, dy_ref, dyb_ref, loss_ref, acc_ref):
        i = pl.program_id(0)
        y = jnp.dot(u_ref[...], w_ref[...], preferred_element_type=f32) + x_ref[...]
        err = y - t_ref[...]
        dy = err * (1.0 / N)
        dy_ref[...] = dy
        dyb_ref[...] = dy.astype(bf16)
        part = jnp.sum(err * err, axis=0, keepdims=True)

        @pl.when(i == 0)
        def _():
            acc_ref[...] = part

        @pl.when(i > 0)
        def _():
            acc_ref[...] += part

        @pl.when(i == nm - 1)
        def _():
            tot = jnp.sum(acc_ref[...], axis=1, keepdims=True)
            loss_ref[...] = jnp.broadcast_to(tot * (0.5 / N), (8, LANES))

    return pl.pallas_call(
        body, name=name, grid=(nm,),
        in_specs=[pl.BlockSpec((tm, K), lambda i: (i, 0)), pl.BlockSpec((K, N), lambda i: (0, 0)),
                  pl.BlockSpec((tm, N), lambda i: (i, 0)), pl.BlockSpec((tm, N), lambda i: (i, 0))],
        out_specs=[pl.BlockSpec((tm, N), lambda i: (i, 0)), pl.BlockSpec((tm, N), lambda i: (i, 0)),
                   pl.BlockSpec((8, LANES), lambda i: (0, 0))],
        out_shape=[jax.ShapeDtypeStruct((M, N), f32), jax.ShapeDtypeStruct((M, N), bf16),
                   jax.ShapeDtypeStruct((8, LANES), f32)],
        scratch_shapes=[pltpu.VMEM((1, N), f32)],
        compiler_params=_cp(("arbitrary",)),
    )(u, w, x_res, target)


def _out_proj_rms(u, w, res, norm_w, *, name):
    M, K = u.shape
    N = w.shape[1]
    tm = _tile(M, 512)

    def body(u_ref, w_ref, r_ref, nw_ref, y_ref, h_ref):
        y = jnp.dot(u_ref[...], w_ref[...], preferred_element_type=f32) + r_ref[...]
        y_ref[...] = y
        h_ref[...] = (y * lax.rsqrt(jnp.mean(y * y, axis=-1, keepdims=True) + EPS) * nw_ref[...]).astype(bf16)

    row = lambda width: pl.BlockSpec((tm, width), lambda i: (i, 0))
    return pl.pallas_call(
        body, name=name, grid=(M // tm,),
        in_specs=[row(K), pl.BlockSpec((K, N), lambda i: (0, 0)), row(N), pl.BlockSpec((1, N), lambda i: (0, 0))],
        out_specs=[row(N), row(N)],
        out_shape=[jax.ShapeDtypeStruct((M, N), f32), jax.ShapeDtypeStruct((M, N), bf16)],
        compiler_params=_cp(("parallel",)),
    )(u, w, res, norm_w)


def _mm_nt_rms_bwd(a, b, x, w, res, *, name, tm=512, fuse=None):
    M, K = a.shape
    nch, N, tk = b.shape
    tm = _tile(M, tm)

    def body(a_ref, b_hbm, x_ref, w_ref, res_ref, dx_ref, dxb_ref, gw_ref, bbuf, sem):
        i = pl.program_id(0)
        _load_once(b_hbm, bbuf, sem)
        dh_v = None
        for k in range(nch):
            part = lax.dot_general(a_ref[:, k * tk:(k + 1) * tk], bbuf[k], (((1,), (1,)), ((), ())),
                                   preferred_element_type=f32)
            dh_v = part if dh_v is None else dh_v + part
        xv = x_ref[...]
        r = lax.rsqrt(jnp.mean(xv * xv, axis=-1, keepdims=True) + EPS)
        xn = xv * r
        dxn = dh_v * w_ref[...]
        dx = r * (dxn - xn * jnp.mean(dxn * xn, axis=-1, keepdims=True)) + res_ref[...]
        dx_ref[...] = dx
        dxb_ref[...] = dx.astype(bf16)
        part = jnp.sum(dh_v * xn, axis=0, keepdims=True)

        @pl.when(i == 0)
        def _():
            gw_ref[...] = part

        @pl.when(i > 0)
        def _():
            gw_ref[...] += part

    row = lambda width: pl.BlockSpec((tm, width), lambda i: (i, 0))
    vec = pl.BlockSpec((1, N), lambda i: (0, 0))
    out = _grid_call(
        body, name=name, grid=(M // tm,),
        in_specs=[row(K), _HBM_ANY, row(N), vec, row(N)],
        out_specs=[row(N), row(N), vec],
        out_shape=[jax.ShapeDtypeStruct((M, N), f32), jax.ShapeDtypeStruct((M, N), bf16), jax.ShapeDtypeStruct((1, N), f32)],
        scratch_shapes=[pltpu.VMEM(b.shape, b.dtype), pltpu.SemaphoreType.DMA],
        sem=("arbitrary",), args=[a, b, x, w, res], fuse=fuse)
    return out if fuse is None else (*out[0], out[1])


_INV_FREQ = [float(v) for v in (np.float32(ROPE_THETA) ** (-np.arange(ROT_HALF, dtype=np.float32) / np.float32(ROT_HALF))).astype(np.float32)]


def _rope_tables(pos_col):
    S = pos_col.shape[0]
    tm = _tile(S, 1024)

    def body(p_ref, c_ref, s1_ref, s2_ref):
        lane = lax.broadcasted_iota(jnp.int32, (tm, LANES), 1)
        lm = lane % HEAD_DIM
        fi = lm % ROT_HALF
        inv = jnp.zeros((tm, LANES), f32)
        for k in range(ROT_HALF):
            inv = jnp.where(fi == k, _INV_FREQ[k], inv)
        ang = p_ref[...].astype(f32) * inv
        cs = jnp.cos(ang)
        sn = jnp.sin(ang)
        c_ref[...] = jnp.where(lm < 2 * ROT_HALF, cs, 1.0)
        s1_ref[...] = jnp.where((lm >= ROT_HALF) & (lm < 2 * ROT_HALF), sn, 0.0)
        s2_ref[...] = jnp.where(lm < ROT_HALF, -sn, 0.0)

    spec = pl.BlockSpec((tm, LANES), lambda i: (i, 0))
    return pl.pallas_call(
        body, name="rope_tables", grid=(S // tm,),
        in_specs=[pl.BlockSpec((tm, 1), lambda i: (i, 0))],
        out_specs=[spec, spec, spec],
        out_shape=[jax.ShapeDtypeStruct((S, LANES), f32)] * 3,
        compiler_params=_cp(("parallel",)),
    )(pos_col)


def _head_mean(v, m):
    hi = v.astype(bf16)
    lo = (v - hi.astype(f32)).astype(bf16)
    return jnp.dot(hi, m, preferred_element_type=f32) + jnp.dot(lo, m, preferred_element_type=f32)


def _head_mean_matrix():
    i = np.arange(LANES)
    return jnp.asarray(((i[:, None] // HEAD_DIM) == (i[None, :] // HEAD_DIM)).astype(np.float32) / HEAD_DIM, dtype=bf16)


def _in_proj0(x, norm_w, b, tabs, nw, hm, *, fuse=None):
    M, K = x.shape
    nch, _, tn = b.shape
    tm = _tile(M, 256)

    def body(x_ref, w_ref, b_hbm, c_ref, s1_ref, s2_ref, nw_ref, m_ref, o_ref, qk_ref, h_ref, bbuf, sem):
        _load_once(b_hbm, bbuf, sem)
        xv = x_ref[...]
        av = (xv * lax.rsqrt(jnp.mean(xv * xv, axis=-1, keepdims=True) + EPS) * w_ref[...]).astype(bf16)
        h_ref[...] = av
        c, s1, s2, m = c_ref[...], s1_ref[...], s2_ref[...], m_ref[...]
        for j in range(nch):
            res = jnp.dot(av, bbuf[j], preferred_element_type=f32)
            o_ref[:, j * tn:(j + 1) * tn] = res
            for p in range(tn // LANES):
                col = j * tn + p * LANES
                if col >= E_V:
                    continue
                w = nw_ref[0:1, :] if col < E_K else nw_ref[1:2, :]
                t = res[:, p * LANES:(p + 1) * LANES]
                that = t * lax.rsqrt(_head_mean(t * t, m) + EPS) * w
                qk_ref[:, col:col + LANES] = (
                    that * c + pltpu.roll(that, ROT_HALF, axis=1) * s1 + pltpu.roll(that, LANES - ROT_HALF, axis=1) * s2)

    tab = pl.BlockSpec((tm, LANES), lambda i: (i, 0))
    out = _grid_call(
        body, name="in_proj0", grid=(M // tm,),
        in_specs=[pl.BlockSpec((tm, K), lambda i: (i, 0)), pl.BlockSpec((1, K), lambda i: (0, 0)), _HBM_ANY, tab, tab, tab,
                  pl.BlockSpec((2, LANES), lambda i: (0, 0)), pl.BlockSpec((LANES, LANES), lambda i: (0, 0))],
        out_specs=[pl.BlockSpec((tm, nch * tn), lambda i: (i, 0)), pl.BlockSpec((tm, E_V), lambda i: (i, 0)),
                   pl.BlockSpec((tm, K), lambda i: (i, 0))],
        out_shape=[jax.ShapeDtypeStruct((M, nch * tn), f32), jax.ShapeDtypeStruct((M, E_V), f32),
                   jax.ShapeDtypeStruct((M, K), bf16)],
        scratch_shapes=[pltpu.VMEM(b.shape, b.dtype), pltpu.SemaphoreType.DMA],
        sem=("arbitrary",), args=[x, norm_w, b, *tabs, nw, hm], fuse=fuse)
    return out if fuse is None else (*out[0], out[1])


def _key_geometry(nparts):
    qr = QBLK // nparts
    rho = lax.broadcasted_iota(jnp.int32, (2 * QBLK, 2 * QBLK), 0) % QBLK
    kap = lax.broadcasted_iota(jnp.int32, (2 * QBLK, 2 * QBLK), 1)
    n_q = QBLK + nparts * (rho % qr) + rho // qr
    tt = kap % (2 * qr)
    n_k = nparts * tt + kap // (2 * qr)
    dist = n_q - n_k
    return (dist >= 0) & (dist <= QBLK), (tt < qr).astype(jnp.int32)


def _stack_heads(t, lo):
    zero = jnp.zeros_like(t)
    return jnp.concatenate([jnp.where(lo, t, zero), jnp.where(lo, zero, t)], axis=0)


def _attn_block_fwd(qb, kcat, vcat, mask, lo):
    s = lax.dot_general(_stack_heads(qb, lo), kcat, (((1,), (1,)), ((), ())), preferred_element_type=f32) * SCALE
    s = jnp.where(mask, s, NEG)
    mx = jnp.max(s, axis=-1, keepdims=True)
    pexp = jnp.exp(s - mx)
    den = jnp.sum(pexp, axis=-1, keepdims=True)
    pn = (pexp * (1.0 / den)).astype(bf16)
    o2 = jnp.dot(pn, vcat, preferred_element_type=f32)
    lse2 = jnp.broadcast_to(mx + jnp.log(den), (2 * QBLK, LANES))
    return jnp.where(lo, o2[:QBLK], o2[QBLK:]), jnp.where(lo, lse2[:QBLK], lse2[QBLK:])


def _attn_block_bwd(qb, dob, kcat, vcat, lt, ds, mask, lo):
    lt_sw = pltpu.roll(lt, HEAD_DIM, axis=1)
    ds_sw = pltpu.roll(ds, HEAD_DIM, axis=1)
    lt2 = jnp.concatenate([jnp.where(lo, lt, lt_sw), jnp.where(lo, lt_sw, lt)], axis=0)
    ds2 = jnp.concatenate([jnp.where(lo, ds, ds_sw), jnp.where(lo, ds_sw, ds)], axis=0)
    q2 = _stack_heads(qb, lo)
    do2 = _stack_heads(dob, lo)
    s = lax.dot_general(q2, kcat, (((1,), (1,)), ((), ())), preferred_element_type=f32) * SCALE
    s = jnp.where(mask, s, NEG)
    prob = jnp.exp(s - jnp.concatenate([lt2, lt2], axis=1))
    dp = lax.dot_general(do2, vcat, (((1,), (1,)), ((), ())), preferred_element_type=f32)
    dsb = (prob * (dp - jnp.concatenate([ds2, ds2], axis=1)) * SCALE).astype(bf16)
    dq2 = jnp.dot(dsb, kcat, preferred_element_type=f32)
    dk = lax.dot_general(dsb, q2, (((0,), (0,)), ((), ())), preferred_element_type=f32)
    dv = lax.dot_general(prob.astype(bf16), do2, (((0,), (0,)), ((), ())), preferred_element_type=f32)
    return jnp.where(lo, dq2[:QBLK], dq2[QBLK:]), dk, dv


ATT_ROWS = 1024
ATT_UNROLL = 4


def _attn_fwd_local(qk, proj):
    S = qk.shape[0]
    tr = _tile(S, ATT_ROWS)
    lw = 4 * LANES
    nb = tr // QBLK

    def body(q_ref, k_ref, kh_ref, v_ref, vh_ref, o_ref, lse_ref, kbuf, vbuf):
        j = pl.program_id(0)
        kbuf[0:QBLK, :] = jnp.where(j > 0, kh_ref[...], 0.0)
        kbuf[QBLK:, :] = k_ref[...]
        vbuf[0:QBLK, :] = jnp.where(j > 0, vh_ref[...], 0.0)
        vbuf[QBLK:, :] = v_ref[...]
        band, is_prev = _key_geometry(1)
        lo = lax.broadcasted_iota(jnp.int32, (QBLK, LANES), 1) < HEAD_DIM

        def blk(c, carry):
            r0 = pl.multiple_of(c * QBLK, QBLK)
            first = jnp.where((c == 0) & (j == 0), 1, 0)
            mask = band & (is_prev * first == 0)
            for pp in range(lw // LANES):
                lanes = slice(pp * LANES, (pp + 1) * LANES)
                o, lse = _attn_block_fwd(q_ref[pl.ds(r0, QBLK), lanes].astype(bf16),
                                         kbuf[pl.ds(r0, 2 * QBLK), lanes].astype(bf16),
                                         vbuf[pl.ds(r0, 2 * QBLK), lanes].astype(bf16), mask, lo)
                o_ref[pl.ds(r0, QBLK), lanes] = o
                lse_ref[pl.ds(r0, QBLK), lanes] = lse
            return carry

        lax.fori_loop(0, nb, blk, 0, unroll=ATT_UNROLL)

    def halo(col):
        return pl.BlockSpec((QBLK, lw), lambda j, l: (jnp.maximum(j * nb - 1, 0), col + l))

    def tile(col):
        return pl.BlockSpec((tr, lw), lambda j, l: (j, col + l))

    return pl.pallas_call(
        body, name="attn_fwd0", grid=(S // tr, A_WIDTH // lw),
        in_specs=[tile(E_Q // lw), tile(E_K // lw), halo(E_K // lw), tile(E_V // lw), halo(E_V // lw)],
        out_specs=[tile(0), tile(0)],
        out_shape=[jax.ShapeDtypeStruct((S, A_WIDTH), f32)] * 2,
        scratch_shapes=[pltpu.VMEM((QBLK + tr, lw), f32)] * 2,
        compiler_params=_cp(("parallel", "parallel")),
    )(qk, qk, qk, proj, proj)


def _attn_bwd_local(qk, proj, do_a, lt, dsum, fuse=None):
    S = qk.shape[0]
    tr = _tile(S, ATT_ROWS)
    lw = 2 * LANES
    nb = tr // QBLK
    nt = S // tr

    def body(q_ref, qn_ref, do_ref, don_ref, lt_ref, ltn_ref, ds_ref, dsn_ref, k_ref, kh_ref, v_ref, vh_ref,
             dq_ref, dk_ref, dv_ref, kbuf, vbuf, dkbuf, dvbuf):
        j = pl.program_id(0)
        zeros = jnp.zeros((QBLK, lw), f32)
        kbuf[0:QBLK, :] = jnp.where(j > 0, kh_ref[...], 0.0)
        kbuf[pl.ds(QBLK, tr), :] = k_ref[...]
        kbuf[pl.ds(QBLK + tr, QBLK), :] = zeros
        vbuf[0:QBLK, :] = jnp.where(j > 0, vh_ref[...], 0.0)
        vbuf[pl.ds(QBLK, tr), :] = v_ref[...]
        vbuf[pl.ds(QBLK + tr, QBLK), :] = zeros
        dkbuf[...] = jnp.zeros_like(dkbuf)
        dvbuf[...] = jnp.zeros_like(dvbuf)
        band, is_prev = _key_geometry(1)
        lo = lax.broadcasted_iota(jnp.int32, (QBLK, LANES), 1) < HEAD_DIM

        def blk(c, carry):
            r0 = pl.multiple_of(c * QBLK, QBLK)
            first = jnp.where((c == 0) & (j == 0), 1, 0)
            mask = band & (is_prev * first == 0)
            for pp in range(lw // LANES):
                lanes = slice(pp * LANES, (pp + 1) * LANES)
                dq, dk, dv = _attn_block_bwd(
                    q_ref[pl.ds(r0, QBLK), lanes].astype(bf16), do_ref[pl.ds(r0, QBLK), lanes].astype(bf16),
                    kbuf[pl.ds(r0, 2 * QBLK), lanes].astype(bf16), vbuf[pl.ds(r0, 2 * QBLK), lanes].astype(bf16),
                    lt_ref[pl.ds(r0, QBLK), lanes], ds_ref[pl.ds(r0, QBLK), lanes], mask, lo)
                dq_ref[pl.ds(r0, QBLK), lanes] = dq
                dkbuf[pl.ds(r0, 2 * QBLK), lanes] += dk
                dvbuf[pl.ds(r0, 2 * QBLK), lanes] += dv
            return carry

        lax.fori_loop(0, nb, blk, 0, unroll=min(nb, 2 * ATT_UNROLL))

        @pl.when(j < nt - 1)
        def _():
            mask = band & (is_prev == 1)
            for pp in range(lw // LANES):
                lanes = slice(pp * LANES, (pp + 1) * LANES)
                _, dk, dv = _attn_block_bwd(
                    qn_ref[:, lanes].astype(bf16), don_ref[:, lanes].astype(bf16),
                    kbuf[pl.ds(tr, 2 * QBLK), lanes].astype(bf16), vbuf[pl.ds(tr, 2 * QBLK), lanes].astype(bf16),
                    ltn_ref[:, lanes], dsn_ref[:, lanes], mask, lo)
                dkbuf[pl.ds(tr, 2 * QBLK), lanes] += dk
                dvbuf[pl.ds(tr, 2 * QBLK), lanes] += dv

        dk_ref[...] = dkbuf[pl.ds(QBLK, tr), :]
        dv_ref[...] = dvbuf[pl.ds(QBLK, tr), :]

    def prev_halo(col):
        return pl.BlockSpec((QBLK, lw), lambda j, l: (jnp.maximum(j * nb - 1, 0), col + l))

    def next_halo(col):
        return pl.BlockSpec((QBLK, lw), lambda j, l: (jnp.minimum((j + 1) * nb, S // QBLK - 1), col + l))

    def tile(col):
        return pl.BlockSpec((tr, lw), lambda j, l: (j, col + l))

    return _grid_call(
        body, name="attn_bwd0", grid=(nt, A_WIDTH // lw),
        in_specs=[tile(E_Q // lw), next_halo(E_Q // lw), tile(0), next_halo(0), tile(0), next_halo(0), tile(0), next_halo(0),
                  tile(E_K // lw), prev_halo(E_K // lw), tile(E_V // lw), prev_halo(E_V // lw)],
        out_specs=[tile(0)] * 3,
        out_shape=[jax.ShapeDtypeStruct((S, A_WIDTH), f32)] * 3,
        scratch_shapes=[pltpu.VMEM((tr + 2 * QBLK, lw), f32)] * 4,
        sem=("parallel", "parallel"), args=[qk, qk, do_a, do_a, lt, lt, dsum, dsum, qk, qk, proj, proj], fuse=fuse)


def _stream_view(a, d):
    S, W = a.shape
    return a.reshape(S // 8, 8, W) if d == 4 else a.reshape(S // 16, 2, 8, W)


def _stream_ref(ref, d, r, part, col, lw):
    n = ref.shape[0]
    if d == 4:
        return ref.at[pl.ds(0, n), r + 4 * part, pl.ds(col, lw)]
    return ref.at[pl.ds(0, n), r // 8, r % 8, pl.ds(col, lw)]


def _stream_geometry(S, d):
    nparts = 2 if d == 4 else 1
    rows = S // (d * nparts)
    return nparts, rows, QBLK // nparts


def _attn_fwd_dil(qk, proj, g, *, name):
    S = qk.shape[0]
    d = DILATIONS[g]
    nparts, rows, qr = _stream_geometry(S, d)
    nb = rows // qr
    lw = 2 * LANES if d == 4 else 4 * LANES
    nlg = A_WIDTH // lw
    nitems = d * nlg
    ins = ((0, E_Q + A_WIDTH * g, 0), (0, E_K + A_WIDTH * g, qr), (1, E_V + A_WIDTH * g, qr))

    def body(qk_hbm, pj_hbm, o_hbm, l_hbm, qbuf, kbuf, vbuf, obuf, lbuf, in_sems, out_sems):
        i = pl.program_id(0)
        slot = i % 2
        hbm_in = (qk_hbm, pj_hbm)
        bufs_in = (qbuf, kbuf, vbuf)

        def in_copies(item, sl):
            r, lg = item // nlg, item % nlg
            cps = []
            for a in range(nparts):
                for t, (src, col, pad) in enumerate(ins):
                    cps.append(pltpu.make_async_copy(
                        _stream_ref(hbm_in[src], d, r, a, pl.multiple_of(col + lw * lg, LANES), lw),
                        bufs_in[t].at[sl, a, pl.ds(pad, rows), :], in_sems.at[sl, 3 * a + t]))
            return cps

        def out_copies(item, sl):
            r, lg = item // nlg, item % nlg
            cps = []
            for a in range(nparts):
                for t, (buf, dst) in enumerate(((obuf, o_hbm), (lbuf, l_hbm))):
                    cps.append(pltpu.make_async_copy(
                        buf.at[sl, a], _stream_ref(dst, d, r, a, pl.multiple_of(lw * lg, LANES), lw),
                        out_sems.at[sl, 2 * a + t]))
            return cps

        @pl.when(i == 0)
        def _():
            for sl in range(2):
                for a in range(nparts):
                    kbuf[sl, a, 0:qr, :] = jnp.zeros((qr, lw), f32)
                    vbuf[sl, a, 0:qr, :] = jnp.zeros((qr, lw), f32)
            for cp in in_copies(0, 0):
                cp.start()

        @pl.when(i + 1 < nitems)
        def _():
            for cp in in_copies(i + 1, 1 - slot):
                cp.start()

        for cp in in_copies(i, slot):
            cp.wait()

        @pl.when(i >= 2)
        def _():
            for cp in out_copies(i - 2, slot):
                cp.wait()

        band, is_prev = _key_geometry(nparts)
        lo = lax.broadcasted_iota(jnp.int32, (QBLK, LANES), 1) < HEAD_DIM

        def blk(c, carry):
            r0 = pl.multiple_of(c * qr, qr)
            mask = band & (is_prev * jnp.where(c == 0, 1, 0) == 0)
            for pp in range(lw // LANES):
                lanes = slice(pp * LANES, (pp + 1) * LANES)
                qb = jnp.concatenate([qbuf[slot, a, pl.ds(r0, qr), lanes] for a in range(nparts)], axis=0).astype(bf16)
                kcat = jnp.concatenate([kbuf[slot, a, pl.ds(r0, 2 * qr), lanes] for a in range(nparts)], axis=0).astype(bf16)
                vcat = jnp.concatenate([vbuf[slot, a, pl.ds(r0, 2 * qr), lanes] for a in range(nparts)], axis=0).astype(bf16)
                o, lse = _attn_block_fwd(qb, kcat, vcat, mask, lo)
                for a in range(nparts):
                    obuf[slot, a, pl.ds(r0, qr), lanes] = o[a * qr:(a + 1) * qr]
                    lbuf[slot, a, pl.ds(r0, qr), lanes] = lse[a * qr:(a + 1) * qr]
            return carry

        lax.fori_loop(0, nb, blk, 0, unroll=min(nb, 2 * ATT_UNROLL))

        for cp in out_copies(i, slot):
            cp.start()

        @pl.when(i == nitems - 1)
        def _():
            for cp in out_copies(i - 1, 1 - slot) + out_copies(i, slot):
                cp.wait()

    vshape = (S // 8, 8, A_WIDTH) if d == 4 else (S // 16, 2, 8, A_WIDTH)
    o, lse = pl.pallas_call(
        body, name=name, grid=(nitems,),
        in_specs=[_HBM_ANY, _HBM_ANY], out_specs=[_HBM_ANY, _HBM_ANY],
        out_shape=[jax.ShapeDtypeStruct(vshape, f32)] * 2,
        scratch_shapes=[pltpu.VMEM((2, nparts, rows, lw), f32), pltpu.VMEM((2, nparts, qr + rows, lw), f32),
                        pltpu.VMEM((2, nparts, qr + rows, lw), f32), pltpu.VMEM((2, nparts, rows, lw), f32),
                        pltpu.VMEM((2, nparts, rows, lw), f32),
                        pltpu.SemaphoreType.DMA((2, 3 * nparts)), pltpu.SemaphoreType.DMA((2, 2 * nparts))],
        compiler_params=_cp(("arbitrary",)),
    )(_stream_view(qk, d), _stream_view(proj, d))
    return o.reshape(S, A_WIDTH), lse.reshape(S, A_WIDTH)


def _attn_bwd_dil(qk, proj, do_a, lt, dsum, g, *, name):
    S = qk.shape[0]
    d = DILATIONS[g]
    nparts, rows, qr = _stream_geometry(S, d)
    nb = rows // qr
    lw = LANES if d == 4 else 4 * LANES
    nlg = A_WIDTH // lw
    nitems = d * nlg
    ins = ((0, E_Q + A_WIDTH * g, 0), (2, 0, 0), (3, 0, 0), (4, 0, 0), (0, E_K + A_WIDTH * g, qr), (1, E_V + A_WIDTH * g, qr))
    n_in = len(ins)

    def body(qk_hbm, pj_hbm, do_hbm, lt_hbm, ds_hbm, dq_hbm, dk_hbm, dv_hbm,
             qbuf, dobuf, ltbuf, dsbuf, kbuf, vbuf, dqbuf, dkbuf, dvbuf, in_sems, out_sems):
        i = pl.program_id(0)
        slot = i % 2
        hbm_in = (qk_hbm, pj_hbm, do_hbm, lt_hbm, ds_hbm)
        bufs_in = (qbuf, dobuf, ltbuf, dsbuf, kbuf, vbuf)

        def in_copies(item, sl):
            r, lg = item // nlg, item % nlg
            cps = []
            for a in range(nparts):
                for t, (src, col, pad) in enumerate(ins):
                    cps.append(pltpu.make_async_copy(
                        _stream_ref(hbm_in[src], d, r, a, pl.multiple_of(col + lw * lg, LANES), lw),
                        bufs_in[t].at[sl, a, pl.ds(pad, rows), :], in_sems.at[sl, n_in * a + t]))
            return cps

        def out_copies(item, sl):
            r, lg = item // nlg, item % nlg
            cps = []
            for a in range(nparts):
                for t, (buf, dst, pad) in enumerate(((dqbuf, dq_hbm, 0), (dkbuf, dk_hbm, qr), (dvbuf, dv_hbm, qr))):
                    cps.append(pltpu.make_async_copy(
                        buf.at[sl, a, pl.ds(pad, rows), :],
                        _stream_ref(dst, d, r, a, pl.multiple_of(lw * lg, LANES), lw), out_sems.at[sl, 3 * a + t]))
            return cps

        @pl.when(i == 0)
        def _():
            for sl in range(2):
                for a in range(nparts):
                    kbuf[sl, a, 0:qr, :] = jnp.zeros((qr, lw), f32)
                    vbuf[sl, a, 0:qr, :] = jnp.zeros((qr, lw), f32)
            for cp in in_copies(0, 0):
                cp.start()

        @pl.when(i + 1 < nitems)
        def _():
            for cp in in_copies(i + 1, 1 - slot):
                cp.start()

        for cp in in_copies(i, slot):
            cp.wait()

        @pl.when(i >= 2)
        def _():
            for cp in out_copies(i - 2, slot):
                cp.wait()

        for a in range(nparts):
            dkbuf[slot, a] = jnp.zeros((qr + rows, lw), f32)
            dvbuf[slot, a] = jnp.zeros((qr + rows, lw), f32)
        band, is_prev = _key_geometry(nparts)
        lo = lax.broadcasted_iota(jnp.int32, (QBLK, LANES), 1) < HEAD_DIM

        def blk(c, carry):
            r0 = pl.multiple_of(c * qr, qr)
            mask = band & (is_prev * jnp.where(c == 0, 1, 0) == 0)

            def rows_of(buf, n, lanes):
                return jnp.concatenate([buf[slot, a, pl.ds(r0, n), lanes] for a in range(nparts)], axis=0)

            for pp in range(lw // LANES):
                lanes = slice(pp * LANES, (pp + 1) * LANES)
                dq, dk, dv = _attn_block_bwd(
                    rows_of(qbuf, qr, lanes).astype(bf16), rows_of(dobuf, qr, lanes).astype(bf16),
                    rows_of(kbuf, 2 * qr, lanes).astype(bf16), rows_of(vbuf, 2 * qr, lanes).astype(bf16),
                    rows_of(ltbuf, qr, lanes), rows_of(dsbuf, qr, lanes), mask, lo)
                for a in range(nparts):
                    dqbuf[slot, a, pl.ds(r0, qr), lanes] = dq[a * qr:(a + 1) * qr]
                    dkbuf[slot, a, pl.ds(r0, 2 * qr), lanes] += dk[2 * a * qr:2 * (a + 1) * qr]
                    dvbuf[slot, a, pl.ds(r0, 2 * qr), lanes] += dv[2 * a * qr:2 * (a + 1) * qr]
            return carry

        lax.fori_loop(0, nb, blk, 0, unroll=min(nb, 2 * ATT_UNROLL))

        for cp in out_copies(i, slot):
            cp.start()

        @pl.when(i == nitems - 1)
        def _():
            for cp in out_copies(i - 1, 1 - slot) + out_copies(i, slot):
                cp.wait()

    vshape = (S // 8, 8, A_WIDTH) if d == 4 else (S // 16, 2, 8, A_WIDTH)
    plain = pltpu.VMEM((2, nparts, rows, lw), f32)
    padded = pltpu.VMEM((2, nparts, qr + rows, lw), f32)
    outs = pl.pallas_call(
        body, name=name, grid=(nitems,),
        in_specs=[_HBM_ANY] * 5, out_specs=[_HBM_ANY] * 3,
        out_shape=[jax.ShapeDtypeStruct(vshape, f32)] * 3,
        scratch_shapes=[plain, plain, plain, plain, padded, padded, plain, padded, padded,
                        pltpu.SemaphoreType.DMA((2, n_in * nparts)), pltpu.SemaphoreType.DMA((2, 3 * nparts))],
        compiler_params=_cp(("arbitrary",)),
    )(*[_stream_view(a, d) for a in (qk, proj, do_a, lt, dsum)])
    return [o.reshape(S, A_WIDTH) for o in outs]


def _prev_halo(tm, h, col):
    return pl.BlockSpec((h, 512), lambda i: (jnp.maximum(i * (tm // h) - 1, 0), col))


def _next_halo(tm, h, col, S):
    return pl.BlockSpec((h, 512), lambda i: (jnp.minimum((i + 1) * (tm // h), S // h - 1), col))


def _mix0_fwd(o_g, lse_g, proj, conv_w):
    S = proj.shape[0]
    tm = _tile(S, 256)

    def body(o0, o1, o2, l0, l1, l2, bg_ref, cg_ref, hb_ref, z_ref, cgh_ref, hbh_ref, w_ref,
             u_ref, oa_ref, lt_ref, tbuf):
        i = pl.program_id(0)
        ls = [l0[...], l1[...], l2[...]]
        mx = jnp.maximum(jnp.maximum(ls[0], ls[1]), ls[2])
        es = [jnp.exp(l - mx) for l in ls]
        tot = es[0] + es[1] + es[2]
        lt_ref[...] = mx + jnp.log(tot)
        inv = 1.0 / tot
        z = z_ref[...]
        sz = z * _sigmoid(z)
        oa = (es[0] * inv) * o0[...] + (es[1] * inv) * o1[...] + (es[2] * inv) * o2[...]
        oa_ref[...] = oa
        u_ref[:, :A_WIDTH] = (oa * sz[:, :A_WIDTH]).astype(bf16)
        t = cg_ref[...] * hb_ref[...]
        tbuf[0:8, :] = jnp.where(i > 0, cgh_ref[...] * hbh_ref[...], 0.0)
        tbuf[8:, :] = t
        cv = w_ref[2:3, :] * t + w_ref[1:2, :] * tbuf[pl.ds(7, tm), :] + w_ref[0:1, :] * tbuf[pl.ds(6, tm), :]
        u_ref[:, A_WIDTH:] = (bg_ref[...] * cv * sz[:, A_WIDTH:]).astype(bf16)

    row = lambda w, c: pl.BlockSpec((tm, w), lambda i: (i, c))
    return pl.pallas_call(
        body, name="mix0_fwd", grid=(S // tm,),
        in_specs=[row(512, 0)] * 6
        + [row(512, E_BG // 512), row(512, E_CG // 512), row(512, E_HB // 512), row(1024, E_Z // 1024),
           _prev_halo(tm, 8, E_CG // 512), _prev_halo(tm, 8, E_HB // 512), pl.BlockSpec((SC_WIDTH, 512), lambda i: (0, 0))],
        out_specs=[row(1024, 0), row(512, 0), row(512, 0)],
        out_shape=[jax.ShapeDtypeStruct((S, D_MODEL), bf16), jax.ShapeDtypeStruct((S, A_WIDTH), f32),
                   jax.ShapeDtypeStruct((S, A_WIDTH), f32)],
        scratch_shapes=[pltpu.VMEM((tm + 8, 512), f32)],
        compiler_params=_cp(("parallel",)),
    )(*o_g, *lse_g, proj, proj, proj, proj, proj, proj, conv_w)


def _dsilu(z, sg):
    return sg * (1.0 + z * (1.0 - sg))


def _d_gate_in(dy_ref, wo_ref):
    return lax.dot_general(dy_ref[...], wo_ref[...], (((1,), (1,)), ((), ())), preferred_element_type=f32)


def _mix0_bwd_a(dy, w_out, proj, o_a, conv_w):
    S = proj.shape[0]
    tm = _tile(S, 256)

    def body(dy_ref, wo_ref, bg_ref, cg_ref, hb_ref, z_ref, cgh_ref, hbh_ref, oa_ref, w_ref,
             dz_ref, doa_ref, ds_ref, dbg_ref, dcv_ref, tbuf):
        i = pl.program_id(0)
        lo = lax.broadcasted_iota(jnp.int32, (tm, LANES), 1) < HEAD_DIM
        z = z_ref[...]
        sg = _sigmoid(z)
        sz = z * sg
        dsz = _dsilu(z, sg)
        du_v = _d_gate_in(dy_ref, wo_ref)
        t = cg_ref[...] * hb_ref[...]
        tbuf[0:8, :] = jnp.where(i > 0, cgh_ref[...] * hbh_ref[...], 0.0)
        tbuf[8:, :] = t
        cv = w_ref[2:3, :] * t + w_ref[1:2, :] * tbuf[pl.ds(7, tm), :] + w_ref[0:1, :] * tbuf[pl.ds(6, tm), :]
        bg = bg_ref[...]
        oa = oa_ref[...]
        dz_ref[:, :A_WIDTH] = (du_v[:, :A_WIDTH] * oa * dsz[:, :A_WIDTH]).astype(bf16)
        dz_ref[:, A_WIDTH:] = (du_v[:, A_WIDTH:] * (bg * cv) * dsz[:, A_WIDTH:]).astype(bf16)
        doa = du_v[:, :A_WIDTH] * sz[:, :A_WIDTH]
        dyb = du_v[:, A_WIDTH:] * sz[:, A_WIDTH:]
        doa_ref[...] = doa
        dbg_ref[...] = (dyb * cv).astype(bf16)
        dcv_ref[...] = dyb * bg
        prod = doa * oa
        for p in range(4):
            pp = prod[:, p * LANES:(p + 1) * LANES]
            sa = jnp.sum(jnp.where(lo, pp, 0.0), axis=-1, keepdims=True)
            sb = jnp.sum(jnp.where(lo, 0.0, pp), axis=-1, keepdims=True)
            ds_ref[:, p * LANES:(p + 1) * LANES] = jnp.where(lo, sa, sb)

    row = lambda w, c: pl.BlockSpec((tm, w), lambda i: (i, c))
    return pl.pallas_call(
        body, name="mix0_bwd_a", grid=(S // tm,),
        in_specs=[row(1024, 0), pl.BlockSpec((D_MODEL, D_MODEL), lambda i: (0, 0)),
                  row(512, E_BG // 512), row(512, E_CG // 512), row(512, E_HB // 512), row(1024, E_Z // 1024),
                  _prev_halo(tm, 8, E_CG // 512), _prev_halo(tm, 8, E_HB // 512), row(512, 0),
                  pl.BlockSpec((SC_WIDTH, 512), lambda i: (0, 0))],
        out_specs=[row(1024, 0), row(512, 0), row(512, 0), row(512, 0), row(512, 0)],
        out_shape=[jax.ShapeDtypeStruct((S, D_MODEL), bf16), jax.ShapeDtypeStruct((S, A_WIDTH), f32),
                   jax.ShapeDtypeStruct((S, A_WIDTH), f32), jax.ShapeDtypeStruct((S, 512), bf16),
                   jax.ShapeDtypeStruct((S, 512), f32)],
        scratch_shapes=[pltpu.VMEM((tm + 8, 512), f32)],
        compiler_params=_cp(("parallel",)),
    )(dy, w_out, proj, proj, proj, proj, proj, proj, o_a, conv_w)


def _mix0_bwd_b(dcv, proj, conv_w):
    S = proj.shape[0]
    tm = _tile(S, 256)
    nt = S // tm

    def body(dcv_ref, dcvn_ref, cg_ref, hb_ref, cgh_ref, hbh_ref, w_ref, dcg_ref, dhb_ref, gw_ref, tbuf, dbuf):
        i = pl.program_id(0)
        cg = cg_ref[...]
        hb = hb_ref[...]
        t = cg * hb
        tbuf[0:8, :] = jnp.where(i > 0, cgh_ref[...] * hbh_ref[...], 0.0)
        tbuf[8:, :] = t
        dcv_v = dcv_ref[...]
        dbuf[0:tm, :] = dcv_v
        dbuf[tm:, :] = jnp.where(i < nt - 1, dcvn_ref[...], 0.0)
        dt = w_ref[2:3, :] * dcv_v + w_ref[1:2, :] * dbuf[pl.ds(1, tm), :] + w_ref[0:1, :] * dbuf[pl.ds(2, tm), :]
        dcg_ref[...] = (dt * hb).astype(bf16)
        dhb_ref[...] = (dt * cg).astype(bf16)
        g2 = jnp.sum(dcv_v * t, axis=0, keepdims=True)
        g1 = jnp.sum(dcv_v * tbuf[pl.ds(7, tm), :], axis=0, keepdims=True)
        g0 = jnp.sum(dcv_v * tbuf[pl.ds(6, tm), :], axis=0, keepdims=True)
        part = jnp.concatenate([g0, g1, g2, jnp.zeros((5, 512), f32)], axis=0)

        @pl.when(i == 0)
        def _():
            gw_ref[...] = part

        @pl.when(i > 0)
        def _():
            gw_ref[...] += part

    row = lambda w, c: pl.BlockSpec((tm, w), lambda i: (i, c))
    return pl.pallas_call(
        body, name="mix0_bwd_b", grid=(nt,),
        in_specs=[row(512, 0), _next_halo(tm, 8, 0, S), row(512, E_CG // 512), row(512, E_HB // 512),
                  _prev_halo(tm, 8, E_CG // 512), _prev_halo(tm, 8, E_HB // 512),
                  pl.BlockSpec((SC_WIDTH, 512), lambda i: (0, 0))],
        out_specs=[row(512, 0), row(512, 0), pl.BlockSpec((8, 512), lambda i: (0, 0))],
        out_shape=[jax.ShapeDtypeStruct((S, 512), bf16), jax.ShapeDtypeStruct((S, 512), bf16),
                   jax.ShapeDtypeStruct((8, 512), f32)],
        scratch_shapes=[pltpu.VMEM((tm + 8, 512), f32), pltpu.VMEM((tm + 8, 512), f32)],
        compiler_params=_cp(("arbitrary",)),
    )(dcv, dcv, proj, proj, proj, proj, conv_w)


def _qk_bwd(dq_g, dk_g, dv_g, proj, tabs, nw, hm, dbg, dcg, dhb, dz):
    S = proj.shape[0]
    tm = _tile(S, 256)

    def body(*refs):
        d_refs = refs[0:6]
        dv_refs = refs[6:9]
        x_ref, c_ref, s1_ref, s2_ref, nw_ref, m_ref, dbg_ref, dcg_ref, dhb_ref, dz_ref, o_ref, gw_ref = refs[9:]
        i = pl.program_id(0)
        c, s1, s2, m = c_ref[...], s1_ref[...], s2_ref[...], m_ref[...]
        accs = []
        for kind in range(2):
            w = nw_ref[kind:kind + 1, :]
            acc = jnp.zeros((1, LANES), f32)
            for gi in range(N_GROUPS):
                for p in range(4):
                    col = kind * 1536 + gi * 512 + p * LANES
                    dout = d_refs[kind * 3 + gi][:, p * LANES:(p + 1) * LANES]
                    t = x_ref[:, col:col + LANES]
                    dthat = (dout * c + pltpu.roll(dout * s1, LANES - ROT_HALF, axis=1)
                             + pltpu.roll(dout * s2, ROT_HALF, axis=1))
                    r = lax.rsqrt(_head_mean(t * t, m) + EPS)
                    tn = t * r
                    acc = acc + jnp.sum(dthat * tn, axis=0, keepdims=True)
                    dtn = dthat * w
                    o_ref[:, col:col + LANES] = (r * (dtn - tn * _head_mean(dtn * tn, m))).astype(bf16)
            accs.append(acc + pltpu.roll(acc, HEAD_DIM, axis=1))
        for gi in range(N_GROUPS):
            o_ref[:, E_V + gi * 512:E_V + (gi + 1) * 512] = dv_refs[gi][...].astype(bf16)
        o_ref[:, E_BG:E_CG] = dbg_ref[...]
        o_ref[:, E_CG:E_HB] = dcg_ref[...]
        o_ref[:, E_HB:E_Z] = dhb_ref[...]
        o_ref[:, E_Z:] = dz_ref[...]
        part = jnp.concatenate([accs[0], accs[1], jnp.zeros((6, LANES), f32)], axis=0)

        @pl.when(i == 0)
        def _():
            gw_ref[...] = part

        @pl.when(i > 0)
        def _():
            gw_ref[...] += part

    row = lambda w, c: pl.BlockSpec((tm, w), lambda i: (i, c))
    tab = row(LANES, 0)
    return pl.pallas_call(
        body, name="qk_bwd", grid=(S // tm,),
        in_specs=[row(512, 0)] * 9 + [row(3072, 0), tab, tab, tab, pl.BlockSpec((2, LANES), lambda i: (0, 0)),
                                      pl.BlockSpec((LANES, LANES), lambda i: (0, 0)),
                                      row(512, 0), row(512, 0), row(512, 0), row(1024, 0)],
        out_specs=[row(EVEN_IN, 0), pl.BlockSpec((8, LANES), lambda i: (0, 0))],
        out_shape=[jax.ShapeDtypeStruct((S, EVEN_IN), bf16), jax.ShapeDtypeStruct((8, LANES), f32)],
        compiler_params=_cp(("arbitrary",)),
    )(*dq_g, *dk_g, *dv_g, proj, *tabs, nw, hm, dbg, dcg, dhb, dz)


def _inv_count(i, tm, p):
    rowg = lax.broadcasted_iota(jnp.int32, (tm, 1), 0) + i * tm
    return 1.0 / jnp.minimum(rowg + 1, p).astype(f32)


def _layer_norm_stats(c):
    mu = jnp.mean(c, axis=-1, keepdims=True)
    cen = c - mu
    rstd = lax.rsqrt(jnp.mean(cen * cen, axis=-1, keepdims=True) + EPS)
    return cen * rstd, rstd


def _fill_pool_buf(i, ubuf, uc_ref, uch_ref):
    ubuf[0:16, :] = jnp.where(i > 0, uch_ref[...], 0.0)
    ubuf[16:, :] = uc_ref[...]


def _pooled(i, tm, ubuf, gi):
    p = POOL_SIZES[gi]
    cols = slice(gi * LANES, (gi + 1) * LANES)
    acc = ubuf[pl.ds(16, tm), cols]
    cur = acc
    for jj in range(1, p):
        acc = acc + ubuf[pl.ds(16 - jj, tm), cols]
    return acc * _inv_count(i, tm, p) - cur


def _fill_glu_buf(i, gbuf, da_ref, dg_ref, dah_ref, dgh_ref):
    gbuf[0:32, :] = jnp.where(i > 0, dah_ref[...] * _sigmoid(dgh_ref[...]), 0.0)
    gbuf[32:, :] = da_ref[...] * _sigmoid(dg_ref[...])


def _shift_copies(buf, sh, tm):
    for b in range(1, 8):
        sh[b - 1] = buf[pl.ds(b, tm + 24), :]


CONV_ROWS = 32


def _window(buf, sh, base, off, rows):
    b = off % 8
    if b == 0:
        return buf[pl.ds(base + off, rows), :]
    return sh[b - 1, pl.ds(base + (off - b), rows), :]


def _mix1_fwd(proj, pool_w, pool_scale, dconv_w, dconv_b, ln_w, ln_b):
    S = proj.shape[0]
    tm = _tile(S, 256)

    def body(uc_ref, uch_ref, da_ref, dg_ref, dah_ref, dgh_ref, za_ref, zb_ref, pw_ref, ps_ref, cw_ref, cb_ref,
             lw_ref, lb_ref, u_ref, c_ref, mc_ref, ubuf, gbuf, gsh):
        i = pl.program_id(0)
        _fill_pool_buf(i, ubuf, uc_ref, uch_ref)
        za = za_ref[...]
        for gi in range(4):
            cols = slice(gi * LANES, (gi + 1) * LANES)
            mc = jnp.dot(_pooled(i, tm, ubuf, gi).astype(bf16), pw_ref[gi], preferred_element_type=f32)
            mc_ref[:, cols] = mc
            zg = za[:, cols]
            u_ref[:, cols] = (mc * ps_ref[:, cols] * (zg * _sigmoid(zg))).astype(bf16)
        _fill_glu_buf(i, gbuf, da_ref, dg_ref, dah_ref, dgh_ref)
        _shift_copies(gbuf, gsh, tm)
        c = jnp.zeros((tm, 512), f32) + cb_ref[...]
        for k in range(D_CONV):
            c = c + cw_ref[k:k + 1, :] * _window(gbuf, gsh, 0, 32 - (D_CONV - 1) + k, tm)
        c_ref[...] = c
        yhat, _ = _layer_norm_stats(c)
        l = yhat * lw_ref[...] + lb_ref[...]
        zb = zb_ref[...]
        u_ref[:, 512:] = (l * _sigmoid(l) * (zb * _sigmoid(zb))).astype(bf16)

    row = lambda w, c: pl.BlockSpec((tm, w), lambda i: (i, c))
    vec = pl.BlockSpec((1, 512), lambda i: (0, 0))
    return pl.pallas_call(
        body, name="mix1_fwd", grid=(S // tm,),
        in_specs=[row(512, 0), _prev_halo(tm, 16, 0), row(512, 1), row(512, 2), _prev_halo(tm, 32, 1), _prev_halo(tm, 32, 2),
                  row(512, 3), row(512, 4), pl.BlockSpec((4, LANES, LANES), lambda i: (0, 0, 0)), vec,
                  pl.BlockSpec((D_CONV, 512), lambda i: (0, 0)), vec, vec, vec],
        out_specs=[row(1024, 0), row(512, 0), row(512, 0)],
        out_shape=[jax.ShapeDtypeStruct((S, D_MODEL), bf16), jax.ShapeDtypeStruct((S, 512), f32),
                   jax.ShapeDtypeStruct((S, 512), f32)],
        scratch_shapes=[pltpu.VMEM((tm + 16, 512), f32), pltpu.VMEM((tm + 32, 512), f32),
                        pltpu.VMEM((7, tm + 24, 512), f32)],
        compiler_params=_cp(("parallel",)),
    )(proj, proj, proj, proj, proj, proj, proj, proj, pool_w, pool_scale, dconv_w, dconv_b, ln_w, ln_b)


def _mix1_bwd_a(dy, w_out, proj, c, mc, pool_w, pool_scale, ln_w, ln_b):
    S = proj.shape[0]
    tm = _tile(S, 256)

    def body(dy_ref, wo_ref, za_ref, zb_ref, c_ref, mc_ref, pw_ref, ps_ref, lw_ref, lb_ref,
             dz_ref, dc_ref, dpl_ref, dmc_ref, acc_ref):
        i = pl.program_id(0)
        du_v = _d_gate_in(dy_ref, wo_ref)
        ps = ps_ref[...]
        za = za_ref[...]
        sga = _sigmoid(za)
        mcv = mc_ref[...]
        dz_ref[:, :512] = (du_v[:, :512] * (mcv * ps) * _dsilu(za, sga)).astype(bf16)
        dyc = du_v[:, :512] * (za * sga)
        g_ps = jnp.sum(dyc * mcv, axis=0, keepdims=True)
        dmc = (dyc * ps).astype(bf16)
        dmc_ref[...] = dmc
        for gi in range(4):
            cols = slice(gi * LANES, (gi + 1) * LANES)
            dpl_ref[:, cols] = lax.dot_general(dmc[:, cols], pw_ref[gi], (((1,), (1,)), ((), ())), preferred_element_type=f32)
        yhat, rstd = _layer_norm_stats(c_ref[...])
        lw = lw_ref[...]
        l = yhat * lw + lb_ref[...]
        sgl = _sigmoid(l)
        zb = zb_ref[...]
        sgb = _sigmoid(zb)
        dz_ref[:, 512:] = (du_v[:, 512:] * (l * sgl) * _dsilu(zb, sgb)).astype(bf16)
        dl = du_v[:, 512:] * (zb * sgb) * _dsilu(l, sgl)
        g_lb = jnp.sum(dl, axis=0, keepdims=True)
        g_lw = jnp.sum(dl * yhat, axis=0, keepdims=True)
        dyh = dl * lw
        dc = rstd * (dyh - jnp.mean(dyh, axis=-1, keepdims=True) - yhat * jnp.mean(dyh * yhat, axis=-1, keepdims=True))
        dc_ref[...] = dc
        g_db = jnp.sum(dc, axis=0, keepdims=True)
        part = jnp.concatenate([g_ps, g_lw, g_lb, g_db, jnp.zeros((4, 512), f32)], axis=0)

        @pl.when(i == 0)
        def _():
            acc_ref[...] = part

        @pl.when(i > 0)
        def _():
            acc_ref[...] += part

    row = lambda w, c_: pl.BlockSpec((tm, w), lambda i: (i, c_))
    vec = pl.BlockSpec((1, 512), lambda i: (0, 0))
    return pl.pallas_call(
        body, name="mix1_bwd_a", grid=(S // tm,),
        in_specs=[row(1024, 0), pl.BlockSpec((D_MODEL, D_MODEL), lambda i: (0, 0)),
                  row(512, 3), row(512, 4), row(512, 0), row(512, 0),
                  pl.BlockSpec((4, LANES, LANES), lambda i: (0, 0, 0)), vec, vec, vec],
        out_specs=[row(1024, 0), row(512, 0), row(512, 0), row(512, 0), pl.BlockSpec((8, 512), lambda i: (0, 0))],
        out_shape=[jax.ShapeDtypeStruct((S, D_MODEL), bf16), jax.ShapeDtypeStruct((S, 512), f32),
                   jax.ShapeDtypeStruct((S, 512), f32), jax.ShapeDtypeStruct((S, 512), bf16),
                   jax.ShapeDtypeStruct((8, 512), f32)],
        compiler_params=_cp(("arbitrary",)),
    )(dy, w_out, proj, proj, c, mc, pool_w, pool_scale, ln_w, ln_b)


def _mix1_bwd_b(dc, dpl, dmc, dz, proj, dconv_w):
    S = proj.shape[0]
    tm = _tile(S, 256)
    nt = S // tm

    def body(dc_ref, dcn_ref, dpl_ref, dpn_ref, dmc_ref, dz_ref, uc_ref, uch_ref, da_ref, dg_ref,
             cw_ref, o_ref, gcw_ref, gpw_ref, ubuf, dcbuf, dpbuf, dcsh, gacc):
        i = pl.program_id(0)
        last = i == nt - 1
        _fill_pool_buf(i, ubuf, uc_ref, uch_ref)
        dcbuf[0:tm, :] = dc_ref[...]
        dcbuf[tm:, :] = jnp.where(last, 0.0, dcn_ref[...])
        _shift_copies(dcbuf, dcsh, tm)
        dpl_v = dpl_ref[...]
        for gi in range(4):
            p = POOL_SIZES[gi]
            cols = slice(gi * LANES, (gi + 1) * LANES)
            dpbuf[0:tm, cols] = dpl_v[:, cols] * _inv_count(i, tm, p)
            dpbuf[tm:, cols] = jnp.where(last, 0.0, dpn_ref[:, cols] * (1.0 / p))
        gpw = []
        for gi in range(4):
            p = POOL_SIZES[gi]
            cols = slice(gi * LANES, (gi + 1) * LANES)
            acc = -dpl_v[:, cols]
            for jj in range(p):
                acc = acc + dpbuf[pl.ds(jj, tm), cols]
            o_ref[:, cols] = acc.astype(bf16)
            pooled = _pooled(i, tm, ubuf, gi).astype(bf16)
            gpw.append(lax.dot_general(pooled, dmc_ref[:, cols], (((0,), (0,)), ((), ())), preferred_element_type=f32))
        gacc[...] = jnp.zeros_like(gacc)

        def conv_rows(ci, carry):
            base = pl.multiple_of(ci * CONV_ROWS, CONV_ROWS)
            da = da_ref[pl.ds(base, CONV_ROWS), :]
            sg = _sigmoid(dg_ref[pl.ds(base, CONV_ROWS), :])
            gl = da * sg
            dgl = jnp.zeros((CONV_ROWS, 512), f32)
            for k in range(D_CONV):
                win = _window(dcbuf, dcsh, base, D_CONV - 1 - k, CONV_ROWS)
                dgl = dgl + cw_ref[k:k + 1, :] * win
                gacc[k] += jnp.sum((gl * win).reshape(CONV_ROWS // 8, 8, 512), axis=0)
            o_ref[pl.ds(base, CONV_ROWS), O_DA:O_DG] = (dgl * sg).astype(bf16)
            o_ref[pl.ds(base, CONV_ROWS), O_DG:O_Z] = (dgl * da * sg * (1.0 - sg)).astype(bf16)
            return carry

        lax.fori_loop(0, tm // CONV_ROWS, conv_rows, 0)
        o_ref[:, O_Z:] = dz_ref[...]
        gcw_part = jnp.concatenate(
            [jnp.sum(gacc[k], axis=0, keepdims=True) for k in range(D_CONV)] + [jnp.zeros((1, 512), f32)], axis=0)

        @pl.when(i == 0)
        def _():
            gcw_ref[...] = gcw_part
            for gi in range(4):
                gpw_ref[gi] = gpw[gi]

        @pl.when(i > 0)
        def _():
            gcw_ref[...] += gcw_part
            for gi in range(4):
                gpw_ref[gi] += gpw[gi]

    row = lambda w, c_: pl.BlockSpec((tm, w), lambda i: (i, c_))
    return pl.pallas_call(
        body, name="mix1_bwd_b", grid=(nt,),
        in_specs=[row(512, 0), _next_halo(tm, 32, 0, S), row(512, 0), _next_halo(tm, 16, 0, S), row(512, 0), row(1024, 0),
                  row(512, 0), _prev_halo(tm, 16, 0), row(512, 1), row(512, 2),
                  pl.BlockSpec((D_CONV, 512), lambda i: (0, 0))],
        out_specs=[row(ODD_IN, 0), pl.BlockSpec((32, 512), lambda i: (0, 0)),
                   pl.BlockSpec((4, LANES, LANES), lambda i: (0, 0, 0))],
        out_shape=[jax.ShapeDtypeStruct((S, ODD_IN), bf16), jax.ShapeDtypeStruct((32, 512), f32),
                   jax.ShapeDtypeStruct((4, LANES, LANES), f32)],
        scratch_shapes=[pltpu.VMEM((tm + 16, 512), f32), pltpu.VMEM((tm + 32, 512), f32),
                        pltpu.VMEM((tm + 16, 512), f32), pltpu.VMEM((7, tm + 24, 512), f32),
                        pltpu.VMEM((D_CONV, 8, 512), f32)],
        compiler_params=_cp(("arbitrary",)),
    )(dc, dc, dpl, dpl, dmc, dz, proj, proj, proj, proj, dconv_w)


_SMALL_LATE = ["e_q_norm_w", "e_k_norm_w", "e_conv_w", "o_norm_w", "o_pool_w", "o_pool_scale", "o_dconv_w", "o_dconv_b",
               "o_ln_w", "o_ln_b"]


def _local_step(x, pos_col, target, w, dist=None):
    hm = _head_mean_matrix()
    nw = jnp.concatenate([jnp.tile(w["e_q_norm_w"], (1, 2)), jnp.tile(w["e_k_norm_w"], (1, 2))], axis=0)
    tabs = _rope_tables(pos_col)
    pool_wb = w["o_pool_w"].astype(bf16)
    e_norm_w, e_w_in = w["e_norm_w"], w["e_w_in"]

    if dist is None:
        proj0, qk, h0 = _in_proj0(x, e_norm_w, e_w_in, tabs, nw, hm)
    else:
        proj0, qk, h0, gathered = _in_proj0(x, e_norm_w, e_w_in, tabs, nw, hm, fuse=([], dist[0]))
        w = {**w, **dist[1](gathered)}
    e_conv_w, e_w_out, o_norm_w, o_w_in, o_w_out = w["e_conv_w"], w["e_w_out"], w["o_norm_w"], w["o_w_in"], w["o_w_out"]
    o_pool_scale, o_dconv_w, o_dconv_b, o_ln_w, o_ln_b = (w[k] for k in ("o_pool_scale", "o_dconv_w", "o_dconv_b", "o_ln_w", "o_ln_b"))
    o_g, lse_g = [], []
    for g in range(N_GROUPS):
        o, l = _attn_fwd_local(qk, proj0) if g == 0 else _attn_fwd_dil(qk, proj0, g, name=f"attn_fwd{g}")
        o_g.append(o)
        lse_g.append(l)
    u0, o_a, lt = _mix0_fwd(o_g, lse_g, proj0, e_conv_w)
    x1, h1 = _out_proj_rms(u0, e_w_out, x, o_norm_w, name="out_proj0")
    o_w_in3 = o_w_in.reshape(1, D_MODEL, ODD_IN)
    proj1 = _mm_nn_resident(h1, o_w_in3, name="in_proj1", tm=512)
    u1, c1, mc1 = _mix1_fwd(proj1, pool_wb, o_pool_scale, o_dconv_w, o_dconv_b, o_ln_w, o_ln_b)
    dy, dyb, loss = _mm_out_loss(u1, o_w_out, x1, target, name="out_proj1_loss")
    g_o_w_out = _mm_tn(u1, dyb, name="g_w_out1", out_dtype=bf16)
    dz1, dc1, dpl1, dmc1, sums1 = _mix1_bwd_a(dyb, o_w_out, proj1, c1, mc1, pool_wb, o_pool_scale, o_ln_w, o_ln_b)
    dproj1, g_dconv_w, g_pool_w = _mix1_bwd_b(dc1, dpl1, dmc1, dz1, proj1, o_dconv_w)
    g_o_w_in = _mm_tn(h1, dproj1, name="g_w_in1", out_dtype=bf16)
    d1, d1b, g_o_norm = _mm_nt_rms_bwd(dproj1, o_w_in3, x1, o_norm_w, dy, name="d_h1")
    g_e_w_out = _mm_tn(u0, d1b, name="g_w_out0", out_dtype=bf16)
    dz0, do_a, dsum, dbg, dcv = _mix0_bwd_a(d1b, e_w_out, proj0, o_a, e_conv_w)
    dcg, dhb, g_conv_w = _mix0_bwd_b(dcv, proj0, e_conv_w)
    fuse_a = None if dist is None else (
        [g_e_w_out.reshape(N_DEV, D_MODEL // N_DEV, D_MODEL),
         jnp.moveaxis(g_o_w_in.reshape(D_MODEL, N_DEV, ODD_IN // N_DEV), 1, 0),
         g_o_w_out.reshape(N_DEV, D_MODEL // N_DEV, D_MODEL)], [])
    dq_g, dk_g, dv_g = [], [], []
    for g in range(N_GROUPS):
        if g == 0:
            dqkv = _attn_bwd_local(qk, proj0, do_a, lt, dsum, fuse=fuse_a)
            if dist is not None:
                dqkv, recv_a = dqkv
            dq, dk, dv = dqkv
        else:
            dq, dk, dv = _attn_bwd_dil(qk, proj0, do_a, lt, dsum, g, name=f"attn_bwd{g}")
        dq_g.append(dq)
        dk_g.append(dk)
        dv_g.append(dv)
    dproj0, g_qk_norm = _qk_bwd(dq_g, dk_g, dv_g, proj0, tabs, nw, hm, dbg, dcg, dhb, dz0)
    half = D_MODEL // 2
    g_e_w_in_a = _mm_tn(h0, dproj0, name="g_w_in0a", out_dtype=bf16, chunks=N_DEV, a_cols=(0, half))
    if dist is None:
        g_e_w_in_b = _mm_tn(h0, dproj0, name="g_w_in0b", out_dtype=bf16, chunks=N_DEV, a_cols=(1, half))
    else:
        g_e_w_in_b, recv_b0 = _mm_tn(h0, dproj0, name="g_w_in0b", out_dtype=bf16, chunks=N_DEV, a_cols=(1, half),
                                     fuse=([g_e_w_in_a], []))
    grads = dict(
        e_q_norm_w=g_qk_norm[0:1, :HEAD_DIM], e_k_norm_w=g_qk_norm[1:2, :HEAD_DIM],
        e_conv_w=g_conv_w[:SC_WIDTH], e_w_out=g_e_w_out,
        o_norm_w=g_o_norm, o_w_in=g_o_w_in, o_pool_w=g_pool_w,
        o_pool_scale=sums1[0:1], o_dconv_w=g_dconv_w[:D_CONV], o_dconv_b=sums1[3:4],
        o_ln_w=sums1[1:2], o_ln_b=sums1[2:3], o_w_out=g_o_w_out)
    if dist is None:
        grad_x, _, grads["e_norm_w"] = _mm_nt_rms_bwd(dproj0, e_w_in, x, e_norm_w, d1, name="d_h0", tm=512)
        grads["e_w_in"] = jnp.concatenate([g_e_w_in_a, g_e_w_in_b], axis=1)
        return loss, grad_x, grads
    small_late, offs = _pack_rows([grads[n_] for n_ in _SMALL_LATE])
    grad_x, _, g_e_norm, recv_b = _mm_nt_rms_bwd(dproj0, e_w_in, x, e_norm_w, d1, name="d_h0", tm=512,
                                                 fuse=([g_e_w_in_b], [small_late]))
    recv_c = _exchange([], [jnp.concatenate([g_e_norm.reshape(8, LANES), loss], axis=0)], name="exchange_e_norm_loss")
    recv = dict(e_w_out=[recv_a[0]], o_w_in=[recv_a[1]], o_w_out=[recv_a[2]], e_w_in=[recv_b0[0], recv_b[0]],
                small_late=recv_b[1], e_norm_w=recv_c[0])
    return loss, grad_x, recv, {n_: (off, grads[n_].shape) for n_, off in zip(_SMALL_LATE, offs)}


_MESH_ID = pl.DeviceIdType.MESH
_HBM = pl.BlockSpec(memory_space=pl.ANY)


def _all_gather(arrs, *, name):
    n = len(arrs)

    def body(*refs):
        ins, outs = refs[:n], refs[n:2 * n]
        send_sems, recv_sems, local_sems = refs[2 * n:]
        x, y, c = _place()
        me, sibling = (x, y, c), (x, y, 1 - c)
        chips = [(1 - x, y), (x, 1 - y), (1 - x, 1 - y)]

        def slot(t, px, py, pc):
            return outs[t].at[4 * px + 2 * py + pc]

        def copy(t, k, block, to, src=None):
            dst = slot(t, *block)
            return pltpu.make_async_remote_copy(
                src_ref=dst if src is None else src, dst_ref=dst,
                send_sem=send_sems.at[7 * t + k], recv_sem=recv_sems.at[7 * t + k],
                device_id=to, device_id_type=_MESH_ID)

        mine = [pltpu.make_async_copy(ins[t], slot(t, *me), local_sems.at[t]) for t in range(n)]
        for cp in mine:
            cp.start()
        first = []
        for t in range(n):
            first.append(copy(t, 0, me, sibling, src=ins[t]))
            first += [copy(t, 1 + j, me, (*chip, c), src=ins[t]) for j, chip in enumerate(chips)]
        for cp in first:
            cp.start()
        passed = []
        for j, chip in enumerate(chips):
            for t in range(n):
                copy(t, 1 + j, (*chip, c), me).wait_recv()
                fwd = copy(t, 4 + j, (*chip, c), sibling)
                fwd.start()
                passed.append(fwd)
        for t in range(n):
            copy(t, 0, sibling, me).wait_recv()
            for j, chip in enumerate(chips):
                copy(t, 4 + j, (*chip, 1 - c), me).wait_recv()
        for cp in first + passed:
            cp.wait_send()
        for cp in mine:
            cp.wait()

    return pl.pallas_call(
        body, name=name,
        in_specs=[_HBM] * n, out_specs=[_HBM] * n,
        out_shape=[jax.ShapeDtypeStruct((N_DEV, *a.shape), a.dtype) for a in arrs],
        scratch_shapes=[pltpu.SemaphoreType.DMA((7 * n,)), pltpu.SemaphoreType.DMA((7 * n,)),
                        pltpu.SemaphoreType.DMA((n,))],
    )(*arrs)


def _exchange(chunked, whole, *, name):
    arrs = list(chunked) + list(whole)
    n = len(arrs)

    def body(*refs):
        start, wait = _exchange_plan(refs[:n], refs[n:2 * n], *refs[2 * n:], len(chunked))
        start()
        wait()

    return pl.pallas_call(
        body, name=name, in_specs=[_HBM] * n, out_specs=[_HBM] * n,
        out_shape=_exchange_out_shapes(chunked, whole), scratch_shapes=_exchange_sems(n),
    )(*arrs)


def _adamw(w, g, m, v):
    m2 = ADAM_B1 * m + (1.0 - ADAM_B1) * g
    v2 = ADAM_B2 * v + (1.0 - ADAM_B2) * (g * g)
    m_hat = m2 / (1.0 - ADAM_B1 ** ADAM_STEP)
    v_hat = v2 / (1.0 - ADAM_B2 ** ADAM_STEP)
    delta = -ADAM_LR * (m_hat / (jnp.sqrt(v_hat) + ADAM_EPS) + ADAM_WD * w)
    return delta, m2, v2


def _sum_adamw(parts, w, m, v, *, name):
    R, C = w.shape
    nsplit = len(parts)
    rp = R // nsplit
    tr = _tile(rp, 256)
    npt = rp // tr

    def body(*refs):
        p_refs = refs[:nsplit]
        w_ref, m_ref, v_ref, g_ref, d_ref, nm_ref, nv_ref = refs[nsplit:]
        h = pl.program_id(0)
        g = None
        for i in range(N_DEV):
            pi = p_refs[0][i]
            for q in range(1, nsplit):
                pi = jnp.where(h == q, p_refs[q][i], pi)
            g = pi.astype(f32) if g is None else g + pi.astype(f32)
        g_ref[...] = g
        d_ref[...], nm_ref[...], nv_ref[...] = _adamw(w_ref[...], g, m_ref[...], v_ref[...])

    def part_spec(q):
        return pl.BlockSpec((N_DEV, tr, C), lambda h, i: (0, jnp.where(h == q, i, 0), 0))

    spec = pl.BlockSpec((tr, C), lambda h, i: (h * npt + i, 0))
    return pl.pallas_call(
        body, name=name, grid=(nsplit, npt),
        in_specs=[part_spec(q) for q in range(nsplit)] + [spec, spec, spec],
        out_specs=[spec] * 4, out_shape=[jax.ShapeDtypeStruct((R, C), f32)] * 4,
        compiler_params=_cp(("parallel", "parallel")),
    )(*parts, w, m, v)


def _sum_parts(parts, *, name):
    _, R, C = parts.shape

    def body(p_ref, o_ref):
        g = p_ref[0]
        for i in range(1, N_DEV):
            g = g + p_ref[i]
        o_ref[...] = g

    return pl.pallas_call(body, name=name, out_shape=jax.ShapeDtypeStruct((R, C), f32),
                          compiler_params=pltpu.CompilerParams(vmem_limit_bytes=VMEM_LIMIT))(parts)


def _adamw_small(ws, gs, ms, vs):
    n = len(ws)

    def body(*refs):
        w_r, g_r, m_r, v_r = refs[:n], refs[n:2 * n], refs[2 * n:3 * n], refs[3 * n:4 * n]
        d_r, nm_r, nv_r = refs[4 * n:5 * n], refs[5 * n:6 * n], refs[6 * n:7 * n]
        for t in range(n):
            d_r[t][...], nm_r[t][...], nv_r[t][...] = _adamw(w_r[t][...], g_r[t][...], m_r[t][...], v_r[t][...])

    shapes = [jax.ShapeDtypeStruct(w.shape, f32) for w in ws]
    outs = pl.pallas_call(body, name="adamw_small", out_shape=shapes * 3)(*ws, *gs, *ms, *vs)
    return outs[:n], outs[n:2 * n], outs[2 * n:]


_WEIGHTS = ["e_norm_w", "e_w_in", "e_q_norm_w", "e_k_norm_w", "e_conv_w", "e_w_out", "o_norm_w", "o_w_in", "o_pool_w",
            "o_pool_scale", "o_dconv_w", "o_dconv_b", "o_ln_w", "o_ln_b", "o_w_out"]
_BIG = ["e_w_in", "e_w_out", "o_w_in", "o_w_out"]
_SMALL_SHARDED = ["e_conv_w", "o_norm_w", "o_pool_scale", "o_dconv_w", "o_dconv_b", "o_ln_w", "o_ln_b"]
_SMALL_ALL = ["e_norm_w", "e_q_norm_w", "e_k_norm_w", "e_conv_w", "o_norm_w", "o_pool_w", "o_pool_scale", "o_dconv_w",
              "o_dconv_b", "o_ln_w", "o_ln_b"]


def _pack_rows(pieces):
    rows, offs, r0 = [], [], 0
    for p in pieces:
        flat = p.reshape(-1)
        nr = -(-flat.shape[0] // (8 * LANES)) * 8
        rows.append(jnp.pad(flat, (0, nr * LANES - flat.shape[0])).reshape(nr, LANES))
        offs.append((r0, nr))
        r0 += nr
    return jnp.concatenate(rows, axis=0), offs


def _unpack_rows(buf, off, shape):
    r0, nr = off
    size = int(np.prod(shape))
    return buf[..., r0:r0 + nr, :].reshape(*buf.shape[:-2], nr * LANES)[..., :size].reshape(*buf.shape[:-2], *shape)


def kernel(x, positions, e_norm_w, e_w_in, e_q_norm_w, e_k_norm_w, e_conv_w, e_w_out, o_norm_w, o_w_in, o_pool_w, o_pool_scale, o_dconv_w, o_dconv_b, o_ln_w, o_ln_b, o_w_out, loss_target, m_e_norm_w, m_e_w_in, m_e_q_norm_w, m_e_k_norm_w, m_e_conv_w, m_e_w_out, m_o_norm_w, m_o_w_in, m_o_pool_w, m_o_pool_scale, m_o_dconv_w, m_o_dconv_b, m_o_ln_w, m_o_ln_b, m_o_w_out, v_e_norm_w, v_e_w_in, v_e_q_norm_w, v_e_k_norm_w, v_e_conv_w, v_e_w_out, v_o_norm_w, v_o_w_in, v_o_pool_w, v_o_pool_scale, v_o_dconv_w, v_o_dconv_b, v_o_ln_w, v_o_ln_b, v_o_w_out):
    w = dict(e_norm_w=e_norm_w, e_w_in=e_w_in, e_q_norm_w=e_q_norm_w, e_k_norm_w=e_k_norm_w, e_conv_w=e_conv_w,
             e_w_out=e_w_out, o_norm_w=o_norm_w, o_w_in=o_w_in, o_pool_w=o_pool_w, o_pool_scale=o_pool_scale,
             o_dconv_w=o_dconv_w, o_dconv_b=o_dconv_b, o_ln_w=o_ln_w, o_ln_b=o_ln_b, o_w_out=o_w_out)
    m = dict(e_norm_w=m_e_norm_w, e_w_in=m_e_w_in, e_q_norm_w=m_e_q_norm_w, e_k_norm_w=m_e_k_norm_w, e_conv_w=m_e_conv_w,
             e_w_out=m_e_w_out, o_norm_w=m_o_norm_w, o_w_in=m_o_w_in, o_pool_w=m_o_pool_w, o_pool_scale=m_o_pool_scale,
             o_dconv_w=m_o_dconv_w, o_dconv_b=m_o_dconv_b, o_ln_w=m_o_ln_w, o_ln_b=m_o_ln_b, o_w_out=m_o_w_out)
    v = dict(e_norm_w=v_e_norm_w, e_w_in=v_e_w_in, e_q_norm_w=v_e_q_norm_w, e_k_norm_w=v_e_k_norm_w, e_conv_w=v_e_conv_w,
             e_w_out=v_e_w_out, o_norm_w=v_o_norm_w, o_w_in=v_o_w_in, o_pool_w=v_o_pool_w, o_pool_scale=v_o_pool_scale,
             o_dconv_w=v_o_dconv_w, o_dconv_b=v_o_dconv_b, o_ln_w=v_o_ln_w, o_ln_b=v_o_ln_b, o_w_out=v_o_w_out)
    S = x.shape[1]
    me = 4 * lax.axis_index("x") + 2 * lax.axis_index("y") + lax.axis_index("c")

    small_local, small_offs = _pack_rows([w[n_] for n_ in _SMALL_SHARDED])
    g_e_in, = _all_gather([w["e_w_in"][0].astype(bf16)], name="gather_e_w_in")
    rest_local = [w["e_w_out"][0].astype(bf16), w["o_w_in"][0].astype(bf16), w["o_w_out"][0].astype(bf16), small_local]

    def unpack_rest(gathered):
        g_e_out, g_o_in, g_o_out, g_small = gathered
        full = {}
        for n_, off in zip(_SMALL_SHARDED, small_offs):
            shard = _unpack_rows(g_small, off, w[n_].shape[1:])
            full[n_] = jnp.moveaxis(shard, 0, -2).reshape(*shard.shape[1:-1], N_DEV * shard.shape[-1])
        return dict(
            e_conv_w=full["e_conv_w"], e_w_out=g_e_out.reshape(D_MODEL, D_MODEL), o_norm_w=full["o_norm_w"].reshape(1, D_MODEL),
            o_w_in=jnp.moveaxis(g_o_in, 0, 1).reshape(D_MODEL, ODD_IN), o_pool_scale=full["o_pool_scale"].reshape(1, 512),
            o_dconv_w=full["o_dconv_w"], o_dconv_b=full["o_dconv_b"].reshape(1, 512), o_ln_w=full["o_ln_w"].reshape(1, 512),
            o_ln_b=full["o_ln_b"].reshape(1, 512), o_w_out=g_o_out.reshape(D_MODEL, D_MODEL))

    loss_blk, grad_x, recv, small_where = _local_step(
        x[0], positions.reshape(S, 1), loss_target[0],
        dict(e_norm_w=w["e_norm_w"], e_w_in=g_e_in, e_q_norm_w=w["e_q_norm_w"], e_k_norm_w=w["e_k_norm_w"],
             o_pool_w=w["o_pool_w"][0]),
        dist=(rest_local, unpack_rest))

    out_g, out_d, out_m, out_v = {}, {}, {}, {}
    for n_ in _BIG:
        res = _sum_adamw(recv[n_], w[n_][0], m[n_][0], v[n_][0], name="adamw_" + n_)
        out_g[n_], out_d[n_], out_m[n_], out_v[n_] = [r[None] for r in res]
    small_sum = _sum_parts(recv["small_late"], name="sum_small_grads")
    last_sum = _sum_parts(recv["e_norm_w"], name="sum_e_norm_grad_loss")
    loss = last_sum[8, 0]
    gs = []
    for n_ in _SMALL_ALL:
        if n_ == "e_norm_w":
            gs.append(last_sum[:8].reshape(w[n_].shape))
            continue
        off, shape = small_where[n_]
        gfull = _unpack_rows(small_sum, off, shape)
        if n_ in _SMALL_SHARDED:
            width = w[n_].shape[-1]
            gfull = lax.dynamic_slice_in_dim(gfull, me * width, width, axis=gfull.ndim - 1)
        gs.append(gfull.reshape(w[n_].shape))
    ds, nms, nvs = _adamw_small([w[n_] for n_ in _SMALL_ALL], gs, [m[n_] for n_ in _SMALL_ALL], [v[n_] for n_ in _SMALL_ALL])
    for n_, g_, d_, nm_, nv_ in zip(_SMALL_ALL, gs, ds, nms, nvs):
        out_g[n_], out_d[n_], out_m[n_], out_v[n_] = g_, d_, nm_, nv_

    return (loss, grad_x[None], *[out_g[n_] for n_ in _WEIGHTS], *[out_d[n_] for n_ in _WEIGHTS],
            *[out_m[n_] for n_ in _WEIGHTS], *[out_v[n_] for n_ in _WEIGHTS])
```

```python
import functools

import numpy as np
import jax
import jax.numpy as jnp
from jax import lax
from jax.experimental import pallas as pl
from jax.experimental.pallas import tpu as pltpu

f32 = jnp.float32
bf16 = jnp.bfloat16

D_MODEL = 1024
HEAD_DIM = 64
N_GROUPS = 3
DILATIONS = (1, 4, 16)
QBLK = 128
A_WIDTH = 512
EVEN_IN = 7168
ODD_IN = 2560
POOL_SIZES = (2, 4, 8, 16)
D_CONV = 31
SC_WIDTH = 3
ROT_HALF = 8
ROPE_THETA = 500000.0
EPS = 1e-6
NEG = -1e30
SCALE = HEAD_DIM ** -0.5
N_DEV = 8
LANES = 128
VMEM_LIMIT = 48 * 1024 * 1024

ADAM_LR = 0.001
ADAM_B1 = 0.9
ADAM_B2 = 0.999
ADAM_EPS = 1e-08
ADAM_WD = 0.01
ADAM_STEP = 10

E_Q, E_K, E_V, E_BG, E_CG, E_HB, E_Z = 0, 1536, 3072, 4608, 5120, 5632, 6144
O_UC, O_DA, O_DG, O_Z = 0, 512, 1024, 1536


def _cp(sem):
    return pltpu.CompilerParams(dimension_semantics=sem, vmem_limit_bytes=VMEM_LIMIT)


_HBM_ANY = pl.BlockSpec(memory_space=pl.ANY)


def _sigmoid(z):
    return 1.0 / (1.0 + jnp.exp(-z))


def _tile(n, pref):
    t = pref
    while n % t:
        t //= 2
    return t


def _place():
    return lax.axis_index("x"), lax.axis_index("y"), lax.axis_index("c")


def _exchange_plan(ins, outs, send_sems, recv_sems, local_sems, nc):
    n = len(ins)
    x, y, c = _place()
    me_i = 4 * x + 2 * y + c

    def src(t, dev_i):
        return ins[t].at[dev_i] if t < nc else ins[t]

    def copies(arriving):
        cps = []
        for m in range(1, N_DEV):
            px = 1 - x if m & 4 else x
            py = 1 - y if m & 2 else y
            pc = 1 - c if m & 1 else c
            peer_i = 4 * px + 2 * py + pc
            for t in range(n):
                cps.append(pltpu.make_async_remote_copy(
                    src_ref=src(t, peer_i), dst_ref=outs[t].at[peer_i if arriving else me_i],
                    send_sem=send_sems.at[7 * t + m - 1], recv_sem=recv_sems.at[7 * t + m - 1],
                    device_id=(x, y, c) if arriving else (px, py, pc), device_id_type=pl.DeviceIdType.MESH))
        return cps

    def mine():
        return [pltpu.make_async_copy(src(t, me_i), outs[t].at[me_i], local_sems.at[t]) for t in range(n)]

    def start():
        for cp in mine() + copies(False):
            cp.start()

    def wait():
        for cp in copies(True):
            cp.wait_recv()
        for cp in copies(False):
            cp.wait_send()
        for cp in mine():
            cp.wait()

    return start, wait


def _exchange_sems(n):
    return [pltpu.SemaphoreType.DMA((7 * n,)), pltpu.SemaphoreType.DMA((7 * n,)), pltpu.SemaphoreType.DMA((n,))]


def _exchange_out_shapes(chunked, whole):
    return ([jax.ShapeDtypeStruct(a.shape, a.dtype) for a in chunked]
            + [jax.ShapeDtypeStruct((N_DEV, *a.shape), a.dtype) for a in whole])


def _grid_call(body, *, name, grid, in_specs, out_specs, out_shape, scratch_shapes, sem, args, fuse=None):
    if fuse is None:
        return pl.pallas_call(body, name=name, grid=grid, in_specs=in_specs, out_specs=out_specs, out_shape=out_shape,
                              scratch_shapes=scratch_shapes, compiler_params=_cp(sem))(*args)
    chunked, whole = fuse
    ex = list(chunked) + list(whole)
    n, n_in, n_out, n_sc = len(ex), len(in_specs), len(out_specs), len(scratch_shapes)

    def fused(*refs):
        ins, ex_in = refs[:n_in], refs[n_in:n_in + n]
        outs, ex_out = refs[n_in + n:n_in + n + n_out], refs[n_in + n + n_out:n_in + 2 * n + n_out]
        scratch = refs[n_in + 2 * n + n_out:n_in + 2 * n + n_out + n_sc]
        start, wait = _exchange_plan(ex_in, ex_out, *refs[-3:], len(chunked))
        first = functools.reduce(jnp.logical_and, [pl.program_id(a) == 0 for a in range(len(grid))])
        last = functools.reduce(jnp.logical_and, [pl.program_id(a) == g - 1 for a, g in enumerate(grid)])
        pl.when(first)(start)
        body(*ins, *outs, *scratch)
        pl.when(last)(wait)

    res = pl.pallas_call(
        fused, name=name, grid=grid, in_specs=list(in_specs) + [_HBM_ANY] * n,
        out_specs=list(out_specs) + [_HBM_ANY] * n, out_shape=list(out_shape) + _exchange_out_shapes(chunked, whole),
        scratch_shapes=list(scratch_shapes) + _exchange_sems(n),
        compiler_params=_cp(("arbitrary",) * len(grid)))(*args, *ex)
    return res[:n_out], res[n_out:]


def _load_once(src_hbm, dst_vmem, sem):
    @pl.when(pl.program_id(0) == 0)
    def _():
        cp = pltpu.make_async_copy(src_hbm, dst_vmem, sem)
        cp.start()
        cp.wait()


def _mm_nn_resident(a, b, *, name, tm=256, fuse=None):
    M, K = a.shape
    nch, _, tn = b.shape
    tm = _tile(M, tm)

    def body(a_ref, b_hbm, o_ref, bbuf, sem):
        _load_once(b_hbm, bbuf, sem)
        av = a_ref[...]
        for j in range(nch):
            o_ref[:, j * tn:(j + 1) * tn] = jnp.dot(av, bbuf[j], preferred_element_type=f32)

    out = _grid_call(
        body, name=name, grid=(M // tm,), in_specs=[pl.BlockSpec((tm, K), lambda i: (i, 0)), _HBM_ANY],
        out_specs=[pl.BlockSpec((tm, nch * tn), lambda i: (i, 0))],
        out_shape=[jax.ShapeDtypeStruct((M, nch * tn), f32)],
        scratch_shapes=[pltpu.VMEM(b.shape, b.dtype), pltpu.SemaphoreType.DMA],
        sem=("arbitrary",), args=[a, b], fuse=fuse)
    return out[0] if fuse is None else (out[0][0], out[1])


def _mm_tn(a, b, *, name, out_dtype=f32, tn=512, chunks=None, a_cols=None, fuse=None):
    S, Ka = a.shape
    a_blk = 0
    if a_cols is not None:
        a_blk, Ka = a_cols
    N = b.shape[1]
    ts = _tile(S, 4096)
    ns = S // ts
    if chunks:
        tn = N // chunks
        out_spec = pl.BlockSpec((None, Ka, tn), lambda j, s: (j, 0, 0))
        out_shape = jax.ShapeDtypeStruct((chunks, Ka, tn), out_dtype)
    else:
        tn = _tile(N, tn)
        out_spec = pl.BlockSpec((Ka, tn), lambda j, s: (0, j))
        out_shape = jax.ShapeDtypeStruct((Ka, N), out_dtype)

    def body(a_ref, b_ref, o_ref, acc_ref):
        s = pl.program_id(1)
        part = lax.dot_general(a_ref[...], b_ref[...], (((0,), (0,)), ((), ())), preferred_element_type=f32)

        @pl.when(s == 0)
        def _():
            acc_ref[...] = part

        @pl.when(s > 0)
        def _():
            acc_ref[...] += part

        @pl.when(s == ns - 1)
        def _():
            o_ref[...] = acc_ref[...].astype(out_dtype)

    out = _grid_call(
        body, name=name, grid=(N // tn, ns),
        in_specs=[pl.BlockSpec((ts, Ka), lambda j, s: (s, a_blk)), pl.BlockSpec((ts, tn), lambda j, s: (s, j))],
        out_specs=[out_spec], out_shape=[out_shape],
        scratch_shapes=[pltpu.VMEM((Ka, tn), f32)],
        sem=("parallel", "arbitrary"), args=[a, b], fuse=fuse)
    return out[0] if fuse is None else (out[0][0], out[1])


def _mm_out_loss(u, w, x_res, target, *, name):
    M, K = u.shape
    N = w.shape[1]
    tm = _tile(M, 512)
    nm = M // tm

    def body(u_ref, w_ref, x_ref, t_ref, dy_ref, dyb_ref, loss_ref, acc_ref):
        i = pl.program_id(0)
        y = jnp.dot(u_ref[...], w_ref[...], preferred_element_type=f32) + x_ref[...]
        err = y - t_ref[...]
        dy = err * (1.0 / N)
        dy_ref[...] = dy
        dyb_ref[...] = dy.astype(bf16)
        part = jnp.sum(err * err, axis=0, keepdims=True)

        @pl.when(i == 0)
        def _():
            acc_ref[...] = part

        @pl.when(i > 0)
        def _():
            acc_ref[...] += part

        @pl.when(i == nm - 1)
        def _():
            tot = jnp.sum(acc_ref[...], axis=1, keepdims=True)
            loss_ref[...] = jnp.broadcast_to(tot * (0.5 / N), (8, LANES))

    return pl.pallas_call(
        body, name=name, grid=(nm,),
        in_specs=[pl.BlockSpec((tm, K), lambda i: (i, 0)), pl.BlockSpec((K, N), lambda i: (0, 0)),
                  pl.BlockSpec((tm, N), lambda i: (i, 0)), pl.BlockSpec((tm, N), lambda i: (i, 0))],
        out_specs=[pl.BlockSpec((tm, N), lambda i: (i, 0)), pl.BlockSpec((tm, N), lambda i: (i, 0)),
                   pl.BlockSpec((8, LANES), lambda i: (0, 0))],
        out_shape=[jax.ShapeDtypeStruct((M, N), f32), jax.ShapeDtypeStruct((M, N), bf16),
                   jax.ShapeDtypeStruct((8, LANES), f32)],
        scratch_shapes=[pltpu.VMEM((1, N), f32)],
        compiler_params=_cp(("arbitrary",)),
    )(u, w, x_res, target)


def _out_proj_rms(u, w, res, norm_w, *, name):
    M, K = u.shape
    N = w.shape[1]
    tm = _tile(M, 512)

    def body(u_ref, w_ref, r_ref, nw_ref, y_ref, h_ref):
        y = jnp.dot(u_ref[...], w_ref[...], preferred_element_type=f32) + r_ref[...]
        y_ref[...] = y
        h_ref[...] = (y * lax.rsqrt(jnp.mean(y * y, axis=-1, keepdims=True) + EPS) * nw_ref[...]).astype(bf16)

    row = lambda width: pl.BlockSpec((tm, width), lambda i: (i, 0))
    return pl.pallas_call(
        body, name=name, grid=(M // tm,),
        in_specs=[row(K), pl.BlockSpec((K, N), lambda i: (0, 0)), row(N), pl.BlockSpec((1, N), lambda i: (0, 0))],
        out_specs=[row(N), row(N)],
        out_shape=[jax.ShapeDtypeStruct((M, N), f32), jax.ShapeDtypeStruct((M, N), bf16)],
        compiler_params=_cp(("parallel",)),
    )(u, w, res, norm_w)


def _mm_nt_rms_bwd(a, b, x, w, res, *, name, tm=512, fuse=None):
    M, K = a.shape
    nch, N, tk = b.shape
    tm = _tile(M, tm)

    def body(a_ref, b_hbm, x_ref, w_ref, res_ref, dx_ref, dxb_ref, gw_ref, bbuf, sem):
        i = pl.program_id(0)
        _load_once(b_hbm, bbuf, sem)
        dh_v = None
        for k in range(nch):
            part = lax.dot_general(a_ref[:, k * tk:(k + 1) * tk], bbuf[k], (((1,), (1,)), ((), ())),
                                   preferred_element_type=f32)
            dh_v = part if dh_v is None else dh_v + part
        xv = x_ref[...]
        r = lax.rsqrt(jnp.mean(xv * xv, axis=-1, keepdims=True) + EPS)
        xn = xv * r
        dxn = dh_v * w_ref[...]
        dx = r * (dxn - xn * jnp.mean(dxn * xn, axis=-1, keepdims=True)) + res_ref[...]
        dx_ref[...] = dx
        dxb_ref[...] = dx.astype(bf16)
        part = jnp.sum(dh_v * xn, axis=0, keepdims=True)

        @pl.when(i == 0)
        def _():
            gw_ref[...] = part

        @pl.when(i > 0)
        def _():
            gw_ref[...] += part

    row = lambda width: pl.BlockSpec((tm, width), lambda i: (i, 0))
    vec = pl.BlockSpec((1, N), lambda i: (0, 0))
    out = _grid_call(
        body, name=name, grid=(M // tm,),
        in_specs=[row(K), _HBM_ANY, row(N), vec, row(N)],
        out_specs=[row(N), row(N), vec],
        out_shape=[jax.ShapeDtypeStruct((M, N), f32), jax.ShapeDtypeStruct((M, N), bf16), jax.ShapeDtypeStruct((1, N), f32)],
        scratch_shapes=[pltpu.VMEM(b.shape, b.dtype), pltpu.SemaphoreType.DMA],
        sem=("arbitrary",), args=[a, b, x, w, res], fuse=fuse)
    return out if fuse is None else (*out[0], out[1])


_INV_FREQ = [float(v) for v in (np.float32(ROPE_THETA) ** (-np.arange(ROT_HALF, dtype=np.float32) / np.float32(ROT_HALF))).astype(np.float32)]


def _rope_rows(pos):
    rows = pos.shape[0]
    lane = lax.broadcasted_iota(jnp.int32, (rows, LANES), 1)
    lm = lane % HEAD_DIM
    fi = lm % ROT_HALF
    inv = jnp.zeros((rows, LANES), f32)
    for k in range(ROT_HALF):
        inv = jnp.where(fi == k, _INV_FREQ[k], inv)
    ang = pos.astype(f32) * inv
    cs = jnp.cos(ang)
    sn = jnp.sin(ang)
    return (jnp.where(lm < 2 * ROT_HALF, cs, 1.0), jnp.where((lm >= ROT_HALF) & (lm < 2 * ROT_HALF), sn, 0.0),
            jnp.where(lm < ROT_HALF, -sn, 0.0))


def _rope_tables(pos_col):
    S = pos_col.shape[0]
    tm = _tile(S, 1024)

    def body(p_ref, c_ref, s1_ref, s2_ref):
        c_ref[...], s1_ref[...], s2_ref[...] = _rope_rows(p_ref[...])

    spec = pl.BlockSpec((tm, LANES), lambda i: (i, 0))
    return pl.pallas_call(
        body, name="rope_tables", grid=(S // tm,),
        in_specs=[pl.BlockSpec((tm, 1), lambda i: (i, 0))],
        out_specs=[spec, spec, spec],
        out_shape=[jax.ShapeDtypeStruct((S, LANES), f32)] * 3,
        compiler_params=_cp(("parallel",)),
    )(pos_col)


def _head_mean(v, m):
    hi = v.astype(bf16)
    lo = (v - hi.astype(f32)).astype(bf16)
    return jnp.dot(hi, m, preferred_element_type=f32) + jnp.dot(lo, m, preferred_element_type=f32)


def _head_mean_matrix():
    i = np.arange(LANES)
    return jnp.asarray(((i[:, None] // HEAD_DIM) == (i[None, :] // HEAD_DIM)).astype(np.float32) / HEAD_DIM, dtype=bf16)


def _in_proj0(x, norm_w, b, tabs, nw, hm, *, fuse=None):
    M, K = x.shape
    nch, _, tn = b.shape
    tm = _tile(M, 256)

    def body(x_ref, w_ref, b_hbm, c_ref, s1_ref, s2_ref, nw_ref, m_ref, o_ref, qk_ref, h_ref, bbuf, sem):
        _load_once(b_hbm, bbuf, sem)
        xv = x_ref[...]
        av = (xv * lax.rsqrt(jnp.mean(xv * xv, axis=-1, keepdims=True) + EPS) * w_ref[...]).astype(bf16)
        h_ref[...] = av
        c, s1, s2, m = c_ref[...], s1_ref[...], s2_ref[...], m_ref[...]
        for j in range(nch):
            res = jnp.dot(av, bbuf[j], preferred_element_type=f32)
            o_ref[:, j * tn:(j + 1) * tn] = res
            for p in range(tn // LANES):
                col = j * tn + p * LANES
                if col >= E_V:
                    continue
                w = nw_ref[0:1, :] if col < E_K else nw_ref[1:2, :]
                t = res[:, p * LANES:(p + 1) * LANES]
                that = t * lax.rsqrt(_head_mean(t * t, m) + EPS) * w
                qk_ref[:, col:col + LANES] = (
                    that * c + pltpu.roll(that, ROT_HALF, axis=1) * s1 + pltpu.roll(that, LANES - ROT_HALF, axis=1) * s2)

    tab = pl.BlockSpec((tm, LANES), lambda i: (i, 0))
    out = _grid_call(
        body, name="in_proj0", grid=(M // tm,),
        in_specs=[pl.BlockSpec((tm, K), lambda i: (i, 0)), pl.BlockSpec((1, K), lambda i: (0, 0)), _HBM_ANY, tab, tab, tab,
                  pl.BlockSpec((2, LANES), lambda i: (0, 0)), pl.BlockSpec((LANES, LANES), lambda i: (0, 0))],
        out_specs=[pl.BlockSpec((tm, nch * tn), lambda i: (i, 0)), pl.BlockSpec((tm, E_V), lambda i: (i, 0)),
                   pl.BlockSpec((tm, K), lambda i: (i, 0))],
        out_shape=[jax.ShapeDtypeStruct((M, nch * tn), f32), jax.ShapeDtypeStruct((M, E_V), f32),
                   jax.ShapeDtypeStruct((M, K), bf16)],
        scratch_shapes=[pltpu.VMEM(b.shape, b.dtype), pltpu.SemaphoreType.DMA],
        sem=("arbitrary",), args=[x, norm_w, b, *tabs, nw, hm], fuse=fuse)
    return out if fuse is None else (*out[0], out[1])


def _key_geometry(nparts):
    qr = QBLK // nparts
    rho = lax.broadcasted_iota(jnp.int32, (2 * QBLK, 2 * QBLK), 0) % QBLK
    kap = lax.broadcasted_iota(jnp.int32, (2 * QBLK, 2 * QBLK), 1)
    n_q = QBLK + nparts * (rho % qr) + rho // qr
    tt = kap % (2 * qr)
    n_k = nparts * tt + kap // (2 * qr)
    dist = n_q - n_k
    return (dist >= 0) & (dist <= QBLK), (tt < qr).astype(jnp.int32)


def _stack_heads(t, lo):
    zero = jnp.zeros_like(t)
    return jnp.concatenate([jnp.where(lo, t, zero), jnp.where(lo, zero, t)], axis=0)


def _attn_block_fwd(qb, kcat, vcat, mask, lo):
    s = lax.dot_general(_stack_heads(qb, lo), kcat, (((1,), (1,)), ((), ())), preferred_element_type=f32) * SCALE
    s = jnp.where(mask, s, NEG)
    mx = jnp.max(s, axis=-1, keepdims=True)
    pexp = jnp.exp(s - mx)
    den = jnp.sum(pexp, axis=-1, keepdims=True)
    pn = (pexp * (1.0 / den)).astype(bf16)
    o2 = jnp.dot(pn, vcat, preferred_element_type=f32)
    lse2 = jnp.broadcast_to(mx + jnp.log(den), (2 * QBLK, LANES))
    return jnp.where(lo, o2[:QBLK], o2[QBLK:]), jnp.where(lo, lse2[:QBLK], lse2[QBLK:])


def _attn_block_bwd(qb, dob, kcat, vcat, lt, ds, mask, lo):
    lt_sw = pltpu.roll(lt, HEAD_DIM, axis=1)
    ds_sw = pltpu.roll(ds, HEAD_DIM, axis=1)
    lt2 = jnp.concatenate([jnp.where(lo, lt, lt_sw), jnp.where(lo, lt_sw, lt)], axis=0)
    ds2 = jnp.concatenate([jnp.where(lo, ds, ds_sw), jnp.where(lo, ds_sw, ds)], axis=0)
    q2 = _stack_heads(qb, lo)
    do2 = _stack_heads(dob, lo)
    s = lax.dot_general(q2, kcat, (((1,), (1,)), ((), ())), preferred_element_type=f32) * SCALE
    s = jnp.where(mask, s, NEG)
    prob = jnp.exp(s - jnp.concatenate([lt2, lt2], axis=1))
    dp = lax.dot_general(do2, vcat, (((1,), (1,)), ((), ())), preferred_element_type=f32)
    dsb = (prob * (dp - jnp.concatenate([ds2, ds2], axis=1)) * SCALE).astype(bf16)
    dq2 = jnp.dot(dsb, kcat, preferred_element_type=f32)
    dk = lax.dot_general(dsb, q2, (((0,), (0,)), ((), ())), preferred_element_type=f32)
    dv = lax.dot_general(prob.astype(bf16), do2, (((0,), (0,)), ((), ())), preferred_element_type=f32)
    return jnp.where(lo, dq2[:QBLK], dq2[QBLK:]), dk, dv


ATT_ROWS = 1024
ATT_UNROLL = 4


def _attn_fwd_local(qk, proj):
    S = qk.shape[0]
    tr = _tile(S, ATT_ROWS)
    lw = 4 * LANES
    nb = tr // QBLK

    def body(q_ref, k_ref, kh_ref, v_ref, vh_ref, o_ref, lse_ref, kbuf, vbuf):
        j = pl.program_id(0)
        kbuf[0:QBLK, :] = jnp.where(j > 0, kh_ref[...], 0.0)
        kbuf[QBLK:, :] = k_ref[...]
        vbuf[0:QBLK, :] = jnp.where(j > 0, vh_ref[...], 0.0)
        vbuf[QBLK:, :] = v_ref[...]
        band, is_prev = _key_geometry(1)
        lo = lax.broadcasted_iota(jnp.int32, (QBLK, LANES), 1) < HEAD_DIM

        def blk(c, carry):
            r0 = pl.multiple_of(c * QBLK, QBLK)
            first = jnp.where((c == 0) & (j == 0), 1, 0)
            mask = band & (is_prev * first == 0)
            for pp in range(lw // LANES):
                lanes = slice(pp * LANES, (pp + 1) * LANES)
                o, lse = _attn_block_fwd(q_ref[pl.ds(r0, QBLK), lanes].astype(bf16),
                                         kbuf[pl.ds(r0, 2 * QBLK), lanes].astype(bf16),
                                         vbuf[pl.ds(r0, 2 * QBLK), lanes].astype(bf16), mask, lo)
                o_ref[pl.ds(r0, QBLK), lanes] = o
                lse_ref[pl.ds(r0, QBLK), lanes] = lse
            return carry

        lax.fori_loop(0, nb, blk, 0, unroll=ATT_UNROLL)

    def halo(col):
        return pl.BlockSpec((QBLK, lw), lambda j, l: (jnp.maximum(j * nb - 1, 0), col + l))

    def tile(col):
        return pl.BlockSpec((tr, lw), lambda j, l: (j, col + l))

    return pl.pallas_call(
        body, name="attn_fwd0", grid=(S // tr, A_WIDTH // lw),
        in_specs=[tile(E_Q // lw), tile(E_K // lw), halo(E_K // lw), tile(E_V // lw), halo(E_V // lw)],
        out_specs=[tile(0), tile(0)],
        out_shape=[jax.ShapeDtypeStruct((S, A_WIDTH), f32)] * 2,
        scratch_shapes=[pltpu.VMEM((QBLK + tr, lw), f32)] * 2,
        compiler_params=_cp(("parallel", "parallel")),
    )(qk, qk, qk, proj, proj)


def _attn_bwd_local(qk, proj, do_a, lt, dsum, fuse=None):
    S = qk.shape[0]
    tr = _tile(S, ATT_ROWS)
    lw = 2 * LANES
    nb = tr // QBLK
    nt = S // tr

    def body(q_ref, qn_ref, do_ref, don_ref, lt_ref, ltn_ref, ds_ref, dsn_ref, k_ref, kh_ref, v_ref, vh_ref,
             dq_ref, dk_ref, dv_ref, kbuf, vbuf, dkbuf, dvbuf):
        j = pl.program_id(0)
        zeros = jnp.zeros((QBLK, lw), f32)
        kbuf[0:QBLK, :] = jnp.where(j > 0, kh_ref[...], 0.0)
        kbuf[pl.ds(QBLK, tr), :] = k_ref[...]
        kbuf[pl.ds(QBLK + tr, QBLK), :] = zeros
        vbuf[0:QBLK, :] = jnp.where(j > 0, vh_ref[...], 0.0)
        vbuf[pl.ds(QBLK, tr), :] = v_ref[...]
        vbuf[pl.ds(QBLK + tr, QBLK), :] = zeros
        dkbuf[...] = jnp.zeros_like(dkbuf)
        dvbuf[...] = jnp.zeros_like(dvbuf)
        band, is_prev = _key_geometry(1)
        lo = lax.broadcasted_iota(jnp.int32, (QBLK, LANES), 1) < HEAD_DIM

        def blk(c, carry):
            r0 = pl.multiple_of(c * QBLK, QBLK)
            first = jnp.where((c == 0) & (j == 0), 1, 0)
            mask = band & (is_prev * first == 0)
            for pp in range(lw // LANES):
                lanes = slice(pp * LANES, (pp + 1) * LANES)
                dq, dk, dv = _attn_block_bwd(
                    q_ref[pl.ds(r0, QBLK), lanes].astype(bf16), do_ref[pl.ds(r0, QBLK), lanes].astype(bf16),
                    kbuf[pl.ds(r0, 2 * QBLK), lanes].astype(bf16), vbuf[pl.ds(r0, 2 * QBLK), lanes].astype(bf16),
                    lt_ref[pl.ds(r0, QBLK), lanes], ds_ref[pl.ds(r0, QBLK), lanes], mask, lo)
                dq_ref[pl.ds(r0, QBLK), lanes] = dq
                dkbuf[pl.ds(r0, 2 * QBLK), lanes] += dk
                dvbuf[pl.ds(r0, 2 * QBLK), lanes] += dv
            return carry

        lax.fori_loop(0, nb, blk, 0, unroll=min(nb, 2 * ATT_UNROLL))

        @pl.when(j < nt - 1)
        def _():
            mask = band & (is_prev == 1)
            for pp in range(lw // LANES):
                lanes = slice(pp * LANES, (pp + 1) * LANES)
                _, dk, dv = _attn_block_bwd(
                    qn_ref[:, lanes].astype(bf16), don_ref[:, lanes].astype(bf16),
                    kbuf[pl.ds(tr, 2 * QBLK), lanes].astype(bf16), vbuf[pl.ds(tr, 2 * QBLK), lanes].astype(bf16),
                    ltn_ref[:, lanes], dsn_ref[:, lanes], mask, lo)
                dkbuf[pl.ds(tr, 2 * QBLK), lanes] += dk
                dvbuf[pl.ds(tr, 2 * QBLK), lanes] += dv

        dk_ref[...] = dkbuf[pl.ds(QBLK, tr), :]
        dv_ref[...] = dvbuf[pl.ds(QBLK, tr), :]

    def prev_halo(col):
        return pl.BlockSpec((QBLK, lw), lambda j, l: (jnp.maximum(j * nb - 1, 0), col + l))

    def next_halo(col):
        return pl.BlockSpec((QBLK, lw), lambda j, l: (jnp.minimum((j + 1) * nb, S // QBLK - 1), col + l))

    def tile(col):
        return pl.BlockSpec((tr, lw), lambda j, l: (j, col + l))

    return _grid_call(
        body, name="attn_bwd0", grid=(nt, A_WIDTH // lw),
        in_specs=[tile(E_Q // lw), next_halo(E_Q // lw), tile(0), next_halo(0), tile(0), next_halo(0), tile(0), next_halo(0),
                  tile(E_K // lw), prev_halo(E_K // lw), tile(E_V // lw), prev_halo(E_V // lw)],
        out_specs=[tile(0)] * 3,
        out_shape=[jax.ShapeDtypeStruct((S, A_WIDTH), f32)] * 3,
        scratch_shapes=[pltpu.VMEM((tr + 2 * QBLK, lw), f32)] * 4,
        sem=("parallel", "parallel"), args=[qk, qk, do_a, do_a, lt, lt, dsum, dsum, qk, qk, proj, proj], fuse=fuse)


def _stream_view(a, d):
    S, W = a.shape
    return a.reshape(S // 8, 8, W) if d == 4 else a.reshape(S // 16, 2, 8, W)


def _stream_ref(ref, d, r, part, col, lw):
    n = ref.shape[0]
    if d == 4:
        return ref.at[pl.ds(0, n), r + 4 * part, pl.ds(col, lw)]
    return ref.at[pl.ds(0, n), r // 8, r % 8, pl.ds(col, lw)]


def _stream_geometry(S, d):
    nparts = 2 if d == 4 else 1
    rows = S // (d * nparts)
    return nparts, rows, QBLK // nparts


def _attn_fwd_dil(qk, proj, g, *, name):
    S = qk.shape[0]
    d = DILATIONS[g]
    nparts, rows, qr = _stream_geometry(S, d)
    nb = rows // qr
    lw = 2 * LANES if d == 4 else 4 * LANES
    nlg = A_WIDTH // lw
    nitems = d * nlg
    ins = ((0, E_Q + A_WIDTH * g, 0), (0, E_K + A_WIDTH * g, qr), (1, E_V + A_WIDTH * g, qr))

    def body(qk_hbm, pj_hbm, o_hbm, l_hbm, qbuf, kbuf, vbuf, obuf, lbuf, in_sems, out_sems):
        i = pl.program_id(0)
        slot = i % 2
        hbm_in = (qk_hbm, pj_hbm)
        bufs_in = (qbuf, kbuf, vbuf)

        def in_copies(item, sl):
            r, lg = item // nlg, item % nlg
            cps = []
            for a in range(nparts):
                for t, (src, col, pad) in enumerate(ins):
                    cps.append(pltpu.make_async_copy(
                        _stream_ref(hbm_in[src], d, r, a, pl.multiple_of(col + lw * lg, LANES), lw),
                        bufs_in[t].at[sl, a, pl.ds(pad, rows), :], in_sems.at[sl, 3 * a + t]))
            return cps

        def out_copies(item, sl):
            r, lg = item // nlg, item % nlg
            cps = []
            for a in range(nparts):
                for t, (buf, dst) in enumerate(((obuf, o_hbm), (lbuf, l_hbm))):
                    cps.append(pltpu.make_async_copy(
                        buf.at[sl, a], _stream_ref(dst, d, r, a, pl.multiple_of(lw * lg, LANES), lw),
                        out_sems.at[sl, 2 * a + t]))
            return cps

        @pl.when(i == 0)
        def _():
            for sl in range(2):
                for a in range(nparts):
                    kbuf[sl, a, 0:qr, :] = jnp.zeros((qr, lw), f32)
                    vbuf[sl, a, 0:qr, :] = jnp.zeros((qr, lw), f32)
            for cp in in_copies(0, 0):
                cp.start()

        @pl.when(i + 1 < nitems)
        def _():
            for cp in in_copies(i + 1, 1 - slot):
                cp.start()

        for cp in in_copies(i, slot):
            cp.wait()

        @pl.when(i >= 2)
        def _():
            for cp in out_copies(i - 2, slot):
                cp.wait()

        band, is_prev = _key_geometry(nparts)
        lo = lax.broadcasted_iota(jnp.int32, (QBLK, LANES), 1) < HEAD_DIM

        def blk(c, carry):
            r0 = pl.multiple_of(c * qr, qr)
            mask = band & (is_prev * jnp.where(c == 0, 1, 0) == 0)
            for pp in range(lw // LANES):
                lanes = slice(pp * LANES, (pp + 1) * LANES)
                qb = jnp.concatenate([qbuf[slot, a, pl.ds(r0, qr), lanes] for a in range(nparts)], axis=0).astype(bf16)
                kcat = jnp.concatenate([kbuf[slot, a, pl.ds(r0, 2 * qr), lanes] for a in range(nparts)], axis=0).astype(bf16)
                vcat = jnp.concatenate([vbuf[slot, a, pl.ds(r0, 2 * qr), lanes] for a in range(nparts)], axis=0).astype(bf16)
                o, lse = _attn_block_fwd(qb, kcat, vcat, mask, lo)
                for a in range(nparts):
                    obuf[slot, a, pl.ds(r0, qr), lanes] = o[a * qr:(a + 1) * qr]
                    lbuf[slot, a, pl.ds(r0, qr), lanes] = lse[a * qr:(a + 1) * qr]
            return carry

        lax.fori_loop(0, nb, blk, 0, unroll=min(nb, 2 * ATT_UNROLL))

        for cp in out_copies(i, slot):
            cp.start()

        @pl.when(i == nitems - 1)
        def _():
            for cp in out_copies(i - 1, 1 - slot) + out_copies(i, slot):
                cp.wait()

    vshape = (S // 8, 8, A_WIDTH) if d == 4 else (S // 16, 2, 8, A_WIDTH)
    o, lse = pl.pallas_call(
        body, name=name, grid=(nitems,),
        in_specs=[_HBM_ANY, _HBM_ANY], out_specs=[_HBM_ANY, _HBM_ANY],
        out_shape=[jax.ShapeDtypeStruct(vshape, f32)] * 2,
        scratch_shapes=[pltpu.VMEM((2, nparts, rows, lw), f32), pltpu.VMEM((2, nparts, qr + rows, lw), f32),
                        pltpu.VMEM((2, nparts, qr + rows, lw), f32), pltpu.VMEM((2, nparts, rows, lw), f32),
                        pltpu.VMEM((2, nparts, rows, lw), f32),
                        pltpu.SemaphoreType.DMA((2, 3 * nparts)), pltpu.SemaphoreType.DMA((2, 2 * nparts))],
        compiler_params=_cp(("arbitrary",)),
    )(_stream_view(qk, d), _stream_view(proj, d))
    return o.reshape(S, A_WIDTH), lse.reshape(S, A_WIDTH)


def _attn_bwd_dil(qk, proj, do_a, lt, dsum, g, *, name):
    S = qk.shape[0]
    d = DILATIONS[g]
    nparts, rows, qr = _stream_geometry(S, d)
    nb = rows // qr
    lw = LANES if d == 4 else 4 * LANES
    nlg = A_WIDTH // lw
    nitems = d * nlg
    ins = ((0, E_Q + A_WIDTH * g, 0), (2, 0, 0), (3, 0, 0), (4, 0, 0), (0, E_K + A_WIDTH * g, qr), (1, E_V + A_WIDTH * g, qr))
    n_in = len(ins)

    def body(qk_hbm, pj_hbm, do_hbm, lt_hbm, ds_hbm, dq_hbm, dk_hbm, dv_hbm,
             qbuf, dobuf, ltbuf, dsbuf, kbuf, vbuf, dqbuf, dkbuf, dvbuf, in_sems, out_sems):
        i = pl.program_id(0)
        slot = i % 2
        hbm_in = (qk_hbm, pj_hbm, do_hbm, lt_hbm, ds_hbm)
        bufs_in = (qbuf, dobuf, ltbuf, dsbuf, kbuf, vbuf)

        def in_copies(item, sl):
            r, lg = item // nlg, item % nlg
            cps = []
            for a in range(nparts):
                for t, (src, col, pad) in enumerate(ins):
                    cps.append(pltpu.make_async_copy(
                        _stream_ref(hbm_in[src], d, r, a, pl.multiple_of(col + lw * lg, LANES), lw),
                        bufs_in[t].at[sl, a, pl.ds(pad, rows), :], in_sems.at[sl, n_in * a + t]))
            return cps

        def out_copies(item, sl):
            r, lg = item // nlg, item % nlg
            cps = []
            for a in range(nparts):
                for t, (buf, dst, pad) in enumerate(((dqbuf, dq_hbm, 0), (dkbuf, dk_hbm, qr), (dvbuf, dv_hbm, qr))):
                    cps.append(pltpu.make_async_copy(
                        buf.at[sl, a, pl.ds(pad, rows), :],
                        _stream_ref(dst, d, r, a, pl.multiple_of(lw * lg, LANES), lw), out_sems.at[sl, 3 * a + t]))
            return cps

        @pl.when(i == 0)
        def _():
            for sl in range(2):
                for a in range(nparts):
                    kbuf[sl, a, 0:qr, :] = jnp.zeros((qr, lw), f32)
                    vbuf[sl, a, 0:qr, :] = jnp.zeros((qr, lw), f32)
            for cp in in_copies(0, 0):
                cp.start()

        @pl.when(i + 1 < nitems)
        def _():
            for cp in in_copies(i + 1, 1 - slot):
                cp.start()

        for cp in in_copies(i, slot):
            cp.wait()

        @pl.when(i >= 2)
        def _():
            for cp in out_copies(i - 2, slot):
                cp.wait()

        for a in range(nparts):
            dkbuf[slot, a] = jnp.zeros((qr + rows, lw), f32)
            dvbuf[slot, a] = jnp.zeros((qr + rows, lw), f32)
        band, is_prev = _key_geometry(nparts)
        lo = lax.broadcasted_iota(jnp.int32, (QBLK, LANES), 1) < HEAD_DIM

        def blk(c, carry):
            r0 = pl.multiple_of(c * qr, qr)
            mask = band & (is_prev * jnp.where(c == 0, 1, 0) == 0)

            def rows_of(buf, n, lanes):
                return jnp.concatenate([buf[slot, a, pl.ds(r0, n), lanes] for a in range(nparts)], axis=0)

            for pp in range(lw // LANES):
                lanes = slice(pp * LANES, (pp + 1) * LANES)
                dq, dk, dv = _attn_block_bwd(
                    rows_of(qbuf, qr, lanes).astype(bf16), rows_of(dobuf, qr, lanes).astype(bf16),
                    rows_of(kbuf, 2 * qr, lanes).astype(bf16), rows_of(vbuf, 2 * qr, lanes).astype(bf16),
                    rows_of(ltbuf, qr, lanes), rows_of(dsbuf, qr, lanes), mask, lo)
                for a in range(nparts):
                    dqbuf[slot, a, pl.ds(r0, qr), lanes] = dq[a * qr:(a + 1) * qr]
                    dkbuf[slot, a, pl.ds(r0, 2 * qr), lanes] += dk[2 * a * qr:2 * (a + 1) * qr]
                    dvbuf[slot, a, pl.ds(r0, 2 * qr), lanes] += dv[2 * a * qr:2 * (a + 1) * qr]
            return carry

        lax.fori_loop(0, nb, blk, 0, unroll=min(nb, 2 * ATT_UNROLL))

        for cp in out_copies(i, slot):
            cp.start()

        @pl.when(i == nitems - 1)
        def _():
            for cp in out_copies(i - 1, 1 - slot) + out_copies(i, slot):
                cp.wait()

    vshape = (S // 8, 8, A_WIDTH) if d == 4 else (S // 16, 2, 8, A_WIDTH)
    plain = pltpu.VMEM((2, nparts, rows, lw), f32)
    padded = pltpu.VMEM((2, nparts, qr + rows, lw), f32)
    outs = pl.pallas_call(
        body, name=name, grid=(nitems,),
        in_specs=[_HBM_ANY] * 5, out_specs=[_HBM_ANY] * 3,
        out_shape=[jax.ShapeDtypeStruct(vshape, f32)] * 3,
        scratch_shapes=[plain, plain, plain, plain, padded, padded, plain, padded, padded,
                        pltpu.SemaphoreType.DMA((2, n_in * nparts)), pltpu.SemaphoreType.DMA((2, 3 * nparts))],
        compiler_params=_cp(("arbitrary",)),
    )(*[_stream_view(a, d) for a in (qk, proj, do_a, lt, dsum)])
    return [o.reshape(S, A_WIDTH) for o in outs]


def _prev_halo(tm, h, col):
    return pl.BlockSpec((h, 512), lambda i: (jnp.maximum(i * (tm // h) - 1, 0), col))


def _next_halo(tm, h, col, S):
    return pl.BlockSpec((h, 512), lambda i: (jnp.minimum((i + 1) * (tm // h), S // h - 1), col))


def _mix0_fwd(o_g, lse_g, proj, conv_w):
    S = proj.shape[0]
    tm = _tile(S, 256)

    def body(o0, o1, o2, l0, l1, l2, bg_ref, cg_ref, hb_ref, z_ref, cgh_ref, hbh_ref, w_ref,
             u_ref, oa_ref, lt_ref, tbuf):
        i = pl.program_id(0)
        ls = [l0[...], l1[...], l2[...]]
        mx = jnp.maximum(jnp.maximum(ls[0], ls[1]), ls[2])
        es = [jnp.exp(l - mx) for l in ls]
        tot = es[0] + es[1] + es[2]
        lt_ref[...] = mx + jnp.log(tot)
        inv = 1.0 / tot
        z = z_ref[...]
        sz = z * _sigmoid(z)
        oa = (es[0] * inv) * o0[...] + (es[1] * inv) * o1[...] + (es[2] * inv) * o2[...]
        oa_ref[...] = oa
        u_ref[:, :A_WIDTH] = (oa * sz[:, :A_WIDTH]).astype(bf16)
        t = cg_ref[...] * hb_ref[...]
        tbuf[0:8, :] = jnp.where(i > 0, cgh_ref[...] * hbh_ref[...], 0.0)
        tbuf[8:, :] = t
        cv = w_ref[2:3, :] * t + w_ref[1:2, :] * tbuf[pl.ds(7, tm), :] + w_ref[0:1, :] * tbuf[pl.ds(6, tm), :]
        u_ref[:, A_WIDTH:] = (bg_ref[...] * cv * sz[:, A_WIDTH:]).astype(bf16)

    row = lambda w, c: pl.BlockSpec((tm, w), lambda i: (i, c))
    return pl.pallas_call(
        body, name="mix0_fwd", grid=(S // tm,),
        in_specs=[row(512, 0)] * 6
        + [row(512, E_BG // 512), row(512, E_CG // 512), row(512, E_HB // 512), row(1024, E_Z // 1024),
           _prev_halo(tm, 8, E_CG // 512), _prev_halo(tm, 8, E_HB // 512), pl.BlockSpec((SC_WIDTH, 512), lambda i: (0, 0))],
        out_specs=[row(1024, 0), row(512, 0), row(512, 0)],
        out_shape=[jax.ShapeDtypeStruct((S, D_MODEL), bf16), jax.ShapeDtypeStruct((S, A_WIDTH), f32),
                   jax.ShapeDtypeStruct((S, A_WIDTH), f32)],
        scratch_shapes=[pltpu.VMEM((tm + 8, 512), f32)],
        compiler_params=_cp(("parallel",)),
    )(*o_g, *lse_g, proj, proj, proj, proj, proj, proj, conv_w)


def _dsilu(z, sg):
    return sg * (1.0 + z * (1.0 - sg))


def _d_gate_in(dy_ref, wo_ref):
    return lax.dot_general(dy_ref[...], wo_ref[...], (((1,), (1,)), ((), ())), preferred_element_type=f32)


def _mix0_bwd_a(dy, w_out, proj, o_a, conv_w):
    S = proj.shape[0]
    tm = _tile(S, 256)

    def body(dy_ref, wo_ref, bg_ref, cg_ref, hb_ref, z_ref, cgh_ref, hbh_ref, oa_ref, w_ref,
             dz_ref, doa_ref, ds_ref, dbg_ref, dcv_ref, tbuf):
        i = pl.program_id(0)
        lo = lax.broadcasted_iota(jnp.int32, (tm, LANES), 1) < HEAD_DIM
        z = z_ref[...]
        sg = _sigmoid(z)
        sz = z * sg
        dsz = _dsilu(z, sg)
        du_v = _d_gate_in(dy_ref, wo_ref)
        t = cg_ref[...] * hb_ref[...]
        tbuf[0:8, :] = jnp.where(i > 0, cgh_ref[...] * hbh_ref[...], 0.0)
        tbuf[8:, :] = t
        cv = w_ref[2:3, :] * t + w_ref[1:2, :] * tbuf[pl.ds(7, tm), :] + w_ref[0:1, :] * tbuf[pl.ds(6, tm), :]
        bg = bg_ref[...]
        oa = oa_ref[...]
        dz_ref[:, :A_WIDTH] = (du_v[:, :A_WIDTH] * oa * dsz[:, :A_WIDTH]).astype(bf16)
        dz_ref[:, A_WIDTH:] = (du_v[:, A_WIDTH:] * (bg * cv) * dsz[:, A_WIDTH:]).astype(bf16)
        doa = du_v[:, :A_WIDTH] * sz[:, :A_WIDTH]
        dyb = du_v[:, A_WIDTH:] * sz[:, A_WIDTH:]
        doa_ref[...] = doa
        dbg_ref[...] = (dyb * cv).astype(bf16)
        dcv_ref[...] = dyb * bg
        prod = doa * oa
        for p in range(4):
            pp = prod[:, p * LANES:(p + 1) * LANES]
            sa = jnp.sum(jnp.where(lo, pp, 0.0), axis=-1, keepdims=True)
            sb = jnp.sum(jnp.where(lo, 0.0, pp), axis=-1, keepdims=True)
            ds_ref[:, p * LANES:(p + 1) * LANES] = jnp.where(lo, sa, sb)

    row = lambda w, c: pl.BlockSpec((tm, w), lambda i: (i, c))
    return pl.pallas_call(
        body, name="mix0_bwd_a", grid=(S // tm,),
        in_specs=[row(1024, 0), pl.BlockSpec((D_MODEL, D_MODEL), lambda i: (0, 0)),
                  row(512, E_BG // 512), row(512, E_CG // 512), row(512, E_HB // 512), row(1024, E_Z // 1024),
                  _prev_halo(tm, 8, E_CG // 512), _prev_halo(tm, 8, E_HB // 512), row(512, 0),
                  pl.BlockSpec((SC_WIDTH, 512), lambda i: (0, 0))],
        out_specs=[row(1024, 0), row(512, 0), row(512, 0), row(512, 0), row(512, 0)],
        out_shape=[jax.ShapeDtypeStruct((S, D_MODEL), bf16), jax.ShapeDtypeStruct((S, A_WIDTH), f32),
                   jax.ShapeDtypeStruct((S, A_WIDTH), f32), jax.ShapeDtypeStruct((S, 512), bf16),
                   jax.ShapeDtypeStruct((S, 512), f32)],
        scratch_shapes=[pltpu.VMEM((tm + 8, 512), f32)],
        compiler_params=_cp(("parallel",)),
    )(dy, w_out, proj, proj, proj, proj, proj, proj, o_a, conv_w)


def _mix0_bwd_b(dcv, proj, conv_w):
    S = proj.shape[0]
    tm = _tile(S, 256)
    nt = S // tm

    def body(dcv_ref, dcvn_ref, cg_ref, hb_ref, cgh_ref, hbh_ref, w_ref, dcg_ref, dhb_ref, gw_ref, tbuf, dbuf):
        i = pl.program_id(0)
        cg = cg_ref[...]
        hb = hb_ref[...]
        t = cg * hb
        tbuf[0:8, :] = jnp.where(i > 0, cgh_ref[...] * hbh_ref[...], 0.0)
        tbuf[8:, :] = t
        dcv_v = dcv_ref[...]
        dbuf[0:tm, :] = dcv_v
        dbuf[tm:, :] = jnp.where(i < nt - 1, dcvn_ref[...], 0.0)
        dt = w_ref[2:3, :] * dcv_v + w_ref[1:2, :] * dbuf[pl.ds(1, tm), :] + w_ref[0:1, :] * dbuf[pl.ds(2, tm), :]
        dcg_ref[...] = (dt * hb).astype(bf16)
        dhb_ref[...] = (dt * cg).astype(bf16)
        g2 = jnp.sum(dcv_v * t, axis=0, keepdims=True)
        g1 = jnp.sum(dcv_v * tbuf[pl.ds(7, tm), :], axis=0, keepdims=True)
        g0 = jnp.sum(dcv_v * tbuf[pl.ds(6, tm), :], axis=0, keepdims=True)
        part = jnp.concatenate([g0, g1, g2, jnp.zeros((5, 512), f32)], axis=0)

        @pl.when(i == 0)
        def _():
            gw_ref[...] = part

        @pl.when(i > 0)
        def _():
            gw_ref[...] += part

    row = lambda w, c: pl.BlockSpec((tm, w), lambda i: (i, c))
    return pl.pallas_call(
        body, name="mix0_bwd_b", grid=(nt,),
        in_specs=[row(512, 0), _next_halo(tm, 8, 0, S), row(512, E_CG // 512), row(512, E_HB // 512),
                  _prev_halo(tm, 8, E_CG // 512), _prev_halo(tm, 8, E_HB // 512),
                  pl.BlockSpec((SC_WIDTH, 512), lambda i: (0, 0))],
        out_specs=[row(512, 0), row(512, 0), pl.BlockSpec((8, 512), lambda i: (0, 0))],
        out_shape=[jax.ShapeDtypeStruct((S, 512), bf16), jax.ShapeDtypeStruct((S, 512), bf16),
                   jax.ShapeDtypeStruct((8, 512), f32)],
        scratch_shapes=[pltpu.VMEM((tm + 8, 512), f32), pltpu.VMEM((tm + 8, 512), f32)],
        compiler_params=_cp(("arbitrary",)),
    )(dcv, dcv, proj, proj, proj, proj, conv_w)


def _qk_bwd(dq_g, dk_g, dv_g, proj, tabs, nw, hm, dbg, dcg, dhb, dz):
    S = proj.shape[0]
    tm = _tile(S, 256)

    def body(*refs):
        d_refs = refs[0:6]
        dv_refs = refs[6:9]
        x_ref, c_ref, s1_ref, s2_ref, nw_ref, m_ref, dbg_ref, dcg_ref, dhb_ref, dz_ref, o_ref, gw_ref = refs[9:]
        i = pl.program_id(0)
        c, s1, s2, m = c_ref[...], s1_ref[...], s2_ref[...], m_ref[...]
        accs = []
        for kind in range(2):
            w = nw_ref[kind:kind + 1, :]
            acc = jnp.zeros((1, LANES), f32)
            for gi in range(N_GROUPS):
                for p in range(4):
                    col = kind * 1536 + gi * 512 + p * LANES
                    dout = d_refs[kind * 3 + gi][:, p * LANES:(p + 1) * LANES]
                    t = x_ref[:, col:col + LANES]
                    dthat = (dout * c + pltpu.roll(dout * s1, LANES - ROT_HALF, axis=1)
                             + pltpu.roll(dout * s2, ROT_HALF, axis=1))
                    r = lax.rsqrt(_head_mean(t * t, m) + EPS)
                    tn = t * r
                    acc = acc + jnp.sum(dthat * tn, axis=0, keepdims=True)
                    dtn = dthat * w
                    o_ref[:, col:col + LANES] = (r * (dtn - tn * _head_mean(dtn * tn, m))).astype(bf16)
            accs.append(acc + pltpu.roll(acc, HEAD_DIM, axis=1))
        for gi in range(N_GROUPS):
            o_ref[:, E_V + gi * 512:E_V + (gi + 1) * 512] = dv_refs[gi][...].astype(bf16)
        o_ref[:, E_BG:E_CG] = dbg_ref[...]
        o_ref[:, E_CG:E_HB] = dcg_ref[...]
        o_ref[:, E_HB:E_Z] = dhb_ref[...]
        o_ref[:, E_Z:] = dz_ref[...]
        part = jnp.concatenate([accs[0], accs[1], jnp.zeros((6, LANES), f32)], axis=0)

        @pl.when(i == 0)
        def _():
            gw_ref[...] = part

        @pl.when(i > 0)
        def _():
            gw_ref[...] += part

    row = lambda w, c: pl.BlockSpec((tm, w), lambda i: (i, c))
    tab = row(LANES, 0)
    return pl.pallas_call(
        body, name="qk_bwd", grid=(S // tm,),
        in_specs=[row(512, 0)] * 9 + [row(3072, 0), tab, tab, tab, pl.BlockSpec((2, LANES), lambda i: (0, 0)),
                                      pl.BlockSpec((LANES, LANES), lambda i: (0, 0)),
                                      row(512, 0), row(512, 0), row(512, 0), row(1024, 0)],
        out_specs=[row(EVEN_IN, 0), pl.BlockSpec((8, LANES), lambda i: (0, 0))],
        out_shape=[jax.ShapeDtypeStruct((S, EVEN_IN), bf16), jax.ShapeDtypeStruct((8, LANES), f32)],
        compiler_params=_cp(("arbitrary",)),
    )(*dq_g, *dk_g, *dv_g, proj, *tabs, nw, hm, dbg, dcg, dhb, dz)


def _inv_count(i, tm, p):
    rowg = lax.broadcasted_iota(jnp.int32, (tm, 1), 0) + i * tm
    return 1.0 / jnp.minimum(rowg + 1, p).astype(f32)


def _layer_norm_stats(c):
    mu = jnp.mean(c, axis=-1, keepdims=True)
    cen = c - mu
    rstd = lax.rsqrt(jnp.mean(cen * cen, axis=-1, keepdims=True) + EPS)
    return cen * rstd, rstd


def _fill_pool_buf(i, ubuf, uc_ref, uch_ref):
    ubuf[0:16, :] = jnp.where(i > 0, uch_ref[...], 0.0)
    ubuf[16:, :] = uc_ref[...]


def _pooled(i, tm, ubuf, gi):
    p = POOL_SIZES[gi]
    cols = slice(gi * LANES, (gi + 1) * LANES)
    acc = ubuf[pl.ds(16, tm), cols]
    cur = acc
    for jj in range(1, p):
        acc = acc + ubuf[pl.ds(16 - jj, tm), cols]
    return acc * _inv_count(i, tm, p) - cur


def _fill_glu_buf(i, gbuf, da_ref, dg_ref, dah_ref, dgh_ref):
    gbuf[0:32, :] = jnp.where(i > 0, dah_ref[...] * _sigmoid(dgh_ref[...]), 0.0)
    gbuf[32:, :] = da_ref[...] * _sigmoid(dg_ref[...])


def _shift_copies(buf, sh, tm):
    for b in range(1, 8):
        sh[b - 1] = buf[pl.ds(b, tm + 24), :]


CONV_ROWS = 32


def _window(buf, sh, base, off, rows):
    b = off % 8
    if b == 0:
        return buf[pl.ds(base + off, rows), :]
    return sh[b - 1, pl.ds(base + (off - b), rows), :]


def _mix1_fwd(proj, pool_w, pool_scale, dconv_w, dconv_b, ln_w, ln_b):
    S = proj.shape[0]
    tm = _tile(S, 256)

    def body(uc_ref, uch_ref, da_ref, dg_ref, dah_ref, dgh_ref, za_ref, zb_ref, pw_ref, ps_ref, cw_ref, cb_ref,
             lw_ref, lb_ref, u_ref, c_ref, mc_ref, ubuf, gbuf, gsh):
        i = pl.program_id(0)
        _fill_pool_buf(i, ubuf, uc_ref, uch_ref)
        za = za_ref[...]
        for gi in range(4):
            cols = slice(gi * LANES, (gi + 1) * LANES)
            mc = jnp.dot(_pooled(i, tm, ubuf, gi).astype(bf16), pw_ref[gi], preferred_element_type=f32)
            mc_ref[:, cols] = mc
            zg = za[:, cols]
            u_ref[:, cols] = (mc * ps_ref[:, cols] * (zg * _sigmoid(zg))).astype(bf16)
        _fill_glu_buf(i, gbuf, da_ref, dg_ref, dah_ref, dgh_ref)
        _shift_copies(gbuf, gsh, tm)
        c = jnp.zeros((tm, 512), f32) + cb_ref[...]
        for k in range(D_CONV):
            c = c + cw_ref[k:k + 1, :] * _window(gbuf, gsh, 0, 32 - (D_CONV - 1) + k, tm)
        c_ref[...] = c
        yhat, _ = _layer_norm_stats(c)
        l = yhat * lw_ref[...] + lb_ref[...]
        zb = zb_ref[...]
        u_ref[:, 512:] = (l * _sigmoid(l) * (zb * _sigmoid(zb))).astype(bf16)

    row = lambda w, c: pl.BlockSpec((tm, w), lambda i: (i, c))
    vec = pl.BlockSpec((1, 512), lambda i: (0, 0))
    return pl.pallas_call(
        body, name="mix1_fwd", grid=(S // tm,),
        in_specs=[row(512, 0), _prev_halo(tm, 16, 0), row(512, 1), row(512, 2), _prev_halo(tm, 32, 1), _prev_halo(tm, 32, 2),
                  row(512, 3), row(512, 4), pl.BlockSpec((4, LANES, LANES), lambda i: (0, 0, 0)), vec,
                  pl.BlockSpec((D_CONV, 512), lambda i: (0, 0)), vec, vec, vec],
        out_specs=[row(1024, 0), row(512, 0), row(512, 0)],
        out_shape=[jax.ShapeDtypeStruct((S, D_MODEL), bf16), jax.ShapeDtypeStruct((S, 512), f32),
                   jax.ShapeDtypeStruct((S, 512), f32)],
        scratch_shapes=[pltpu.VMEM((tm + 16, 512), f32), pltpu.VMEM((tm + 32, 512), f32),
                        pltpu.VMEM((7, tm + 24, 512), f32)],
        compiler_params=_cp(("parallel",)),
    )(proj, proj, proj, proj, proj, proj, proj, proj, pool_w, pool_scale, dconv_w, dconv_b, ln_w, ln_b)


def _mix1_bwd_a(dy, w_out, proj, c, mc, pool_w, pool_scale, ln_w, ln_b):
    S = proj.shape[0]
    tm = _tile(S, 256)

    def body(dy_ref, wo_ref, za_ref, zb_ref, c_ref, mc_ref, pw_ref, ps_ref, lw_ref, lb_ref,
             dz_ref, dc_ref, dpl_ref, dmc_ref, acc_ref):
        i = pl.program_id(0)
        du_v = _d_gate_in(dy_ref, wo_ref)
        ps = ps_ref[...]
        za = za_ref[...]
        sga = _sigmoid(za)
        mcv = mc_ref[...]
        dz_ref[:, :512] = (du_v[:, :512] * (mcv * ps) * _dsilu(za, sga)).astype(bf16)
        dyc = du_v[:, :512] * (za * sga)
        g_ps = jnp.sum(dyc * mcv, axis=0, keepdims=True)
        dmc = (dyc * ps).astype(bf16)
        dmc_ref[...] = dmc
        for gi in range(4):
            cols = slice(gi * LANES, (gi + 1) * LANES)
            dpl_ref[:, cols] = lax.dot_general(dmc[:, cols], pw_ref[gi], (((1,), (1,)), ((), ())), preferred_element_type=f32)
        yhat, rstd = _layer_norm_stats(c_ref[...])
        lw = lw_ref[...]
        l = yhat * lw + lb_ref[...]
        sgl = _sigmoid(l)
        zb = zb_ref[...]
        sgb = _sigmoid(zb)
        dz_ref[:, 512:] = (du_v[:, 512:] * (l * sgl) * _dsilu(zb, sgb)).astype(bf16)
        dl = du_v[:, 512:] * (zb * sgb) * _dsilu(l, sgl)
        g_lb = jnp.sum(dl, axis=0, keepdims=True)
        g_lw = jnp.sum(dl * yhat, axis=0, keepdims=True)
        dyh = dl * lw
        dc = rstd * (dyh - jnp.mean(dyh, axis=-1, keepdims=True) - yhat * jnp.mean(dyh * yhat, axis=-1, keepdims=True))
        dc_ref[...] = dc
        g_db = jnp.sum(dc, axis=0, keepdims=True)
        part = jnp.concatenate([g_ps, g_lw, g_lb, g_db, jnp.zeros((4, 512), f32)], axis=0)

        @pl.when(i == 0)
        def _():
            acc_ref[...] = part

        @pl.when(i > 0)
        def _():
            acc_ref[...] += part

    row = lambda w, c_: pl.BlockSpec((tm, w), lambda i: (i, c_))
    vec = pl.BlockSpec((1, 512), lambda i: (0, 0))
    return pl.pallas_call(
        body, name="mix1_bwd_a", grid=(S // tm,),
        in_specs=[row(1024, 0), pl.BlockSpec((D_MODEL, D_MODEL), lambda i: (0, 0)),
                  row(512, 3), row(512, 4), row(512, 0), row(512, 0),
                  pl.BlockSpec((4, LANES, LANES), lambda i: (0, 0, 0)), vec, vec, vec],
        out_specs=[row(1024, 0), row(512, 0), row(512, 0), row(512, 0), pl.BlockSpec((8, 512), lambda i: (0, 0))],
        out_shape=[jax.ShapeDtypeStruct((S, D_MODEL), bf16), jax.ShapeDtypeStruct((S, 512), f32),
                   jax.ShapeDtypeStruct((S, 512), f32), jax.ShapeDtypeStruct((S, 512), bf16),
                   jax.ShapeDtypeStruct((8, 512), f32)],
        compiler_params=_cp(("arbitrary",)),
    )(dy, w_out, proj, proj, c, mc, pool_w, pool_scale, ln_w, ln_b)


def _mix1_bwd_b(dc, dpl, dmc, dz, proj, dconv_w):
    S = proj.shape[0]
    tm = _tile(S, 256)
    nt = S // tm

    def body(dc_ref, dcn_ref, dpl_ref, dpn_ref, dmc_ref, dz_ref, uc_ref, uch_ref, da_ref, dg_ref,
             cw_ref, o_ref, gcw_ref, gpw_ref, ubuf, dcbuf, dpbuf, dcsh, gacc):
        i = pl.program_id(0)
        last = i == nt - 1
        _fill_pool_buf(i, ubuf, uc_ref, uch_ref)
        dcbuf[0:tm, :] = dc_ref[...]
        dcbuf[tm:, :] = jnp.where(last, 0.0, dcn_ref[...])
        _shift_copies(dcbuf, dcsh, tm)
        dpl_v = dpl_ref[...]
        for gi in range(4):
            p = POOL_SIZES[gi]
            cols = slice(gi * LANES, (gi + 1) * LANES)
            dpbuf[0:tm, cols] = dpl_v[:, cols] * _inv_count(i, tm, p)
            dpbuf[tm:, cols] = jnp.where(last, 0.0, dpn_ref[:, cols] * (1.0 / p))
        gpw = []
        for gi in range(4):
            p = POOL_SIZES[gi]
            cols = slice(gi * LANES, (gi + 1) * LANES)
            acc = -dpl_v[:, cols]
            for jj in range(p):
                acc = acc + dpbuf[pl.ds(jj, tm), cols]
            o_ref[:, cols] = acc.astype(bf16)
            pooled = _pooled(i, tm, ubuf, gi).astype(bf16)
            gpw.append(lax.dot_general(pooled, dmc_ref[:, cols], (((0,), (0,)), ((), ())), preferred_element_type=f32))
        gacc[...] = jnp.zeros_like(gacc)

        def conv_rows(ci, carry):
            base = pl.multiple_of(ci * CONV_ROWS, CONV_ROWS)
            da = da_ref[pl.ds(base, CONV_ROWS), :]
            sg = _sigmoid(dg_ref[pl.ds(base, CONV_ROWS), :])
            gl = da * sg
            dgl = jnp.zeros((CONV_ROWS, 512), f32)
            for k in range(D_CONV):
                win = _window(dcbuf, dcsh, base, D_CONV - 1 - k, CONV_ROWS)
                dgl = dgl + cw_ref[k:k + 1, :] * win
                gacc[k] += jnp.sum((gl * win).reshape(CONV_ROWS // 8, 8, 512), axis=0)
            o_ref[pl.ds(base, CONV_ROWS), O_DA:O_DG] = (dgl * sg).astype(bf16)
            o_ref[pl.ds(base, CONV_ROWS), O_DG:O_Z] = (dgl * da * sg * (1.0 - sg)).astype(bf16)
            return carry

        lax.fori_loop(0, tm // CONV_ROWS, conv_rows, 0)
        o_ref[:, O_Z:] = dz_ref[...]
        gcw_part = jnp.concatenate(
            [jnp.sum(gacc[k], axis=0, keepdims=True) for k in range(D_CONV)] + [jnp.zeros((1, 512), f32)], axis=0)

        @pl.when(i == 0)
        def _():
            gcw_ref[...] = gcw_part
            for gi in range(4):
                gpw_ref[gi] = gpw[gi]

        @pl.when(i > 0)
        def _():
            gcw_ref[...] += gcw_part
            for gi in range(4):
                gpw_ref[gi] += gpw[gi]

    row = lambda w, c_: pl.BlockSpec((tm, w), lambda i: (i, c_))
    return pl.pallas_call(
        body, name="mix1_bwd_b", grid=(nt,),
        in_specs=[row(512, 0), _next_halo(tm, 32, 0, S), row(512, 0), _next_halo(tm, 16, 0, S), row(512, 0), row(1024, 0),
                  row(512, 0), _prev_halo(tm, 16, 0), row(512, 1), row(512, 2),
                  pl.BlockSpec((D_CONV, 512), lambda i: (0, 0))],
        out_specs=[row(ODD_IN, 0), pl.BlockSpec((32, 512), lambda i: (0, 0)),
                   pl.BlockSpec((4, LANES, LANES), lambda i: (0, 0, 0))],
        out_shape=[jax.ShapeDtypeStruct((S, ODD_IN), bf16), jax.ShapeDtypeStruct((32, 512), f32),
                   jax.ShapeDtypeStruct((4, LANES, LANES), f32)],
        scratch_shapes=[pltpu.VMEM((tm + 16, 512), f32), pltpu.VMEM((tm + 32, 512), f32),
                        pltpu.VMEM((tm + 16, 512), f32), pltpu.VMEM((7, tm + 24, 512), f32),
                        pltpu.VMEM((D_CONV, 8, 512), f32)],
        compiler_params=_cp(("arbitrary",)),
    )(dc, dc, dpl, dpl, dmc, dz, proj, proj, proj, proj, dconv_w)


_SMALL_LATE = ["e_q_norm_w", "e_k_norm_w", "e_conv_w", "o_norm_w", "o_pool_w", "o_pool_scale", "o_dconv_w", "o_dconv_b",
               "o_ln_w", "o_ln_b"]


def _local_step(x, pos_col, target, w, dist=None):
    hm = _head_mean_matrix()
    nw = jnp.concatenate([jnp.tile(w["e_q_norm_w"], (1, 2)), jnp.tile(w["e_k_norm_w"], (1, 2))], axis=0)
    tabs = _rope_tables(pos_col) if dist is None else dist[2]
    pool_wb = w["o_pool_w"].astype(bf16)
    e_norm_w, e_w_in = w["e_norm_w"], w["e_w_in"]

    if dist is None:
        proj0, qk, h0 = _in_proj0(x, e_norm_w, e_w_in, tabs, nw, hm)
    else:
        proj0, qk, h0, gathered = _in_proj0(x, e_norm_w, e_w_in, tabs, nw, hm, fuse=([], dist[0]))
        w = {**w, **dist[1](gathered)}
    e_conv_w, e_w_out, o_norm_w, o_w_in, o_w_out = w["e_conv_w"], w["e_w_out"], w["o_norm_w"], w["o_w_in"], w["o_w_out"]
    o_pool_scale, o_dconv_w, o_dconv_b, o_ln_w, o_ln_b = (w[k] for k in ("o_pool_scale", "o_dconv_w", "o_dconv_b", "o_ln_w", "o_ln_b"))
    o_g, lse_g = [], []
    for g in range(N_GROUPS):
        o, l = _attn_fwd_local(qk, proj0) if g == 0 else _attn_fwd_dil(qk, proj0, g, name=f"attn_fwd{g}")
        o_g.append(o)
        lse_g.append(l)
    u0, o_a, lt = _mix0_fwd(o_g, lse_g, proj0, e_conv_w)
    x1, h1 = _out_proj_rms(u0, e_w_out, x, o_norm_w, name="out_proj0")
    o_w_in3 = o_w_in.reshape(1, D_MODEL, ODD_IN)
    proj1 = _mm_nn_resident(h1, o_w_in3, name="in_proj1", tm=512)
    u1, c1, mc1 = _mix1_fwd(proj1, pool_wb, o_pool_scale, o_dconv_w, o_dconv_b, o_ln_w, o_ln_b)
    dy, dyb, loss = _mm_out_loss(u1, o_w_out, x1, target, name="out_proj1_loss")
    g_o_w_out = _mm_tn(u1, dyb, name="g_w_out1", out_dtype=bf16)
    dz1, dc1, dpl1, dmc1, sums1 = _mix1_bwd_a(dyb, o_w_out, proj1, c1, mc1, pool_wb, o_pool_scale, o_ln_w, o_ln_b)
    dproj1, g_dconv_w, g_pool_w = _mix1_bwd_b(dc1, dpl1, dmc1, dz1, proj1, o_dconv_w)
    g_o_w_in = _mm_tn(h1, dproj1, name="g_w_in1", out_dtype=bf16)
    d1, d1b, g_o_norm = _mm_nt_rms_bwd(dproj1, o_w_in3, x1, o_norm_w, dy, name="d_h1")
    g_e_w_out = _mm_tn(u0, d1b, name="g_w_out0", out_dtype=bf16)
    dz0, do_a, dsum, dbg, dcv = _mix0_bwd_a(d1b, e_w_out, proj0, o_a, e_conv_w)
    dcg, dhb, g_conv_w = _mix0_bwd_b(dcv, proj0, e_conv_w)
    fuse_a = None if dist is None else (
        [g_e_w_out.reshape(N_DEV, D_MODEL // N_DEV, D_MODEL),
         jnp.moveaxis(g_o_w_in.reshape(D_MODEL, N_DEV, ODD_IN // N_DEV), 1, 0),
         g_o_w_out.reshape(N_DEV, D_MODEL // N_DEV, D_MODEL)], [])
    dq_g, dk_g, dv_g = [], [], []
    for g in range(N_GROUPS):
        if g == 0:
            dqkv = _attn_bwd_local(qk, proj0, do_a, lt, dsum, fuse=fuse_a)
            if dist is not None:
                dqkv, recv_a = dqkv
            dq, dk, dv = dqkv
        else:
            dq, dk, dv = _attn_bwd_dil(qk, proj0, do_a, lt, dsum, g, name=f"attn_bwd{g}")
        dq_g.append(dq)
        dk_g.append(dk)
        dv_g.append(dv)
    dproj0, g_qk_norm = _qk_bwd(dq_g, dk_g, dv_g, proj0, tabs, nw, hm, dbg, dcg, dhb, dz0)
    half = D_MODEL // 2
    g_e_w_in_a = _mm_tn(h0, dproj0, name="g_w_in0a", out_dtype=bf16, chunks=N_DEV, a_cols=(0, half))
    if dist is None:
        g_e_w_in_b = _mm_tn(h0, dproj0, name="g_w_in0b", out_dtype=bf16, chunks=N_DEV, a_cols=(1, half))
    else:
        g_e_w_in_b, recv_b0 = _mm_tn(h0, dproj0, name="g_w_in0b", out_dtype=bf16, chunks=N_DEV, a_cols=(1, half),
                                     fuse=([g_e_w_in_a], []))
    grads = dict(
        e_q_norm_w=g_qk_norm[0:1, :HEAD_DIM], e_k_norm_w=g_qk_norm[1:2, :HEAD_DIM],
        e_conv_w=g_conv_w[:SC_WIDTH], e_w_out=g_e_w_out,
        o_norm_w=g_o_norm, o_w_in=g_o_w_in, o_pool_w=g_pool_w,
        o_pool_scale=sums1[0:1], o_dconv_w=g_dconv_w[:D_CONV], o_dconv_b=sums1[3:4],
        o_ln_w=sums1[1:2], o_ln_b=sums1[2:3], o_w_out=g_o_w_out)
    if dist is None:
        grad_x, _, grads["e_norm_w"] = _mm_nt_rms_bwd(dproj0, e_w_in, x, e_norm_w, d1, name="d_h0", tm=512)
        grads["e_w_in"] = jnp.concatenate([g_e_w_in_a, g_e_w_in_b], axis=1)
        return loss, grad_x, grads
    small_late, offs = _pack_rows([grads[n_] for n_ in _SMALL_LATE])
    grad_x, _, g_e_norm, recv_b = _mm_nt_rms_bwd(dproj0, e_w_in, x, e_norm_w, d1, name="d_h0", tm=512,
                                                 fuse=([g_e_w_in_b], [small_late]))
    recv_c = _exchange([], [jnp.concatenate([g_e_norm.reshape(8, LANES), loss], axis=0)], name="exchange_e_norm_loss")
    recv = dict(e_w_out=[recv_a[0]], o_w_in=[recv_a[1]], o_w_out=[recv_a[2]], e_w_in=[recv_b0[0], recv_b[0]],
                small_late=recv_b[1], e_norm_w=recv_c[0])
    return loss, grad_x, recv, {n_: (off, grads[n_].shape) for n_, off in zip(_SMALL_LATE, offs)}


_MESH_ID = pl.DeviceIdType.MESH
_HBM = pl.BlockSpec(memory_space=pl.ANY)


def _all_gather(arrs, *, name, rope_pos=None):
    n = len(arrs)
    rope_rows = 1024

    def body(*refs):
        if rope_pos is None:
            ins, outs = refs[:n], refs[n:2 * n]
        else:
            ins, pos_ref, outs, tab_refs = refs[:n], refs[n], refs[n + 1:2 * n + 1], refs[2 * n + 1:2 * n + 4]
        send_sems, recv_sems, local_sems = refs[-3:]
        x, y, c = _place()
        me, sibling = (x, y, c), (x, y, 1 - c)
        chips = [(1 - x, y), (x, 1 - y), (1 - x, 1 - y)]

        def slot(t, px, py, pc):
            return outs[t].at[4 * px + 2 * py + pc]

        def copy(t, k, block, to, src=None):
            dst = slot(t, *block)
            return pltpu.make_async_remote_copy(
                src_ref=dst if src is None else src, dst_ref=dst,
                send_sem=send_sems.at[7 * t + k], recv_sem=recv_sems.at[7 * t + k],
                device_id=to, device_id_type=_MESH_ID)

        mine = [pltpu.make_async_copy(ins[t], slot(t, *me), local_sems.at[t]) for t in range(n)]
        for cp in mine:
            cp.start()
        first = []
        for t in range(n):
            first.append(copy(t, 0, me, sibling, src=ins[t]))
            first += [copy(t, 1 + j, me, (*chip, c), src=ins[t]) for j, chip in enumerate(chips)]
        for cp in first:
            cp.start()
        if rope_pos is not None:
            def rope_chunk(ci, carry):
                rows = pl.ds(pl.multiple_of(ci * rope_rows, rope_rows), rope_rows)
                tabs = _rope_rows(pos_ref[rows, :])
                for ref, tab in zip(tab_refs, tabs):
                    ref[rows, :] = tab
                return carry
            lax.fori_loop(0, rope_pos.shape[0] // rope_rows, rope_chunk, 0)
        passed = []
        for j, chip in enumerate(chips):
            for t in range(n):
                copy(t, 1 + j, (*chip, c), me).wait_recv()
                fwd = copy(t, 4 + j, (*chip, c), sibling)
                fwd.start()
                passed.append(fwd)
        for t in range(n):
            copy(t, 0, sibling, me).wait_recv()
            for j, chip in enumerate(chips):
                copy(t, 4 + j, (*chip, 1 - c), me).wait_recv()
        for cp in first + passed:
            cp.wait_send()
        for cp in mine:
            cp.wait()

    in_specs, out_specs, args = [_HBM] * n, [_HBM] * n, list(arrs)
    out_shape = [jax.ShapeDtypeStruct((N_DEV, *a.shape), a.dtype) for a in arrs]
    if rope_pos is not None:
        vmem = pl.BlockSpec(memory_space=pltpu.VMEM)
        in_specs, out_specs, args = in_specs + [vmem], out_specs + [vmem] * 3, args + [rope_pos]
        out_shape = out_shape + [jax.ShapeDtypeStruct((rope_pos.shape[0], LANES), f32)] * 3
    return pl.pallas_call(
        body, name=name, in_specs=in_specs, out_specs=out_specs, out_shape=out_shape,
        scratch_shapes=[pltpu.SemaphoreType.DMA((7 * n,)), pltpu.SemaphoreType.DMA((7 * n,)),
                        pltpu.SemaphoreType.DMA((n,))],
        compiler_params=pltpu.CompilerParams(vmem_limit_bytes=VMEM_LIMIT),
    )(*args)


def _exchange(chunked, whole, *, name):
    arrs = list(chunked) + list(whole)
    n = len(arrs)

    def body(*refs):
        start, wait = _exchange_plan(refs[:n], refs[n:2 * n], *refs[2 * n:], len(chunked))
        start()
        wait()

    return pl.pallas_call(
        body, name=name, in_specs=[_HBM] * n, out_specs=[_HBM] * n,
        out_shape=_exchange_out_shapes(chunked, whole), scratch_shapes=_exchange_sems(n),
    )(*arrs)


def _adamw(w, g, m, v):
    m2 = ADAM_B1 * m + (1.0 - ADAM_B1) * g
    v2 = ADAM_B2 * v + (1.0 - ADAM_B2) * (g * g)
    m_hat = m2 / (1.0 - ADAM_B1 ** ADAM_STEP)
    v_hat = v2 / (1.0 - ADAM_B2 ** ADAM_STEP)
    delta = -ADAM_LR * (m_hat / (jnp.sqrt(v_hat) + ADAM_EPS) + ADAM_WD * w)
    return delta, m2, v2


def _sum_adamw(parts, w, m, v, *, name):
    R, C = w.shape
    nsplit = len(parts)
    rp = R // nsplit
    tr = _tile(rp, 256)
    npt = rp // tr

    def body(*refs):
        p_refs = refs[:nsplit]
        w_ref, m_ref, v_ref, g_ref, d_ref, nm_ref, nv_ref = refs[nsplit:]
        h = pl.program_id(0)
        g = None
        for i in range(N_DEV):
            pi = p_refs[0][i]
            for q in range(1, nsplit):
                pi = jnp.where(h == q, p_refs[q][i], pi)
            g = pi.astype(f32) if g is None else g + pi.astype(f32)
        g_ref[...] = g
        d_ref[...], nm_ref[...], nv_ref[...] = _adamw(w_ref[...], g, m_ref[...], v_ref[...])

    def part_spec(q):
        return pl.BlockSpec((N_DEV, tr, C), lambda h, i: (0, jnp.where(h == q, i, 0), 0))

    spec = pl.BlockSpec((tr, C), lambda h, i: (h * npt + i, 0))
    return pl.pallas_call(
        body, name=name, grid=(nsplit, npt),
        in_specs=[part_spec(q) for q in range(nsplit)] + [spec, spec, spec],
        out_specs=[spec] * 4, out_shape=[jax.ShapeDtypeStruct((R, C), f32)] * 4,
        compiler_params=_cp(("parallel", "parallel")),
    )(*parts, w, m, v)


def _sum_parts(parts, *, name):
    _, R, C = parts.shape

    def body(p_ref, o_ref):
        g = p_ref[0]
        for i in range(1, N_DEV):
            g = g + p_ref[i]
        o_ref[...] = g

    return pl.pallas_call(body, name=name, out_shape=jax.ShapeDtypeStruct((R, C), f32),
                          compiler_params=pltpu.CompilerParams(vmem_limit_bytes=VMEM_LIMIT))(parts)


def _adamw_small(ws, gs, ms, vs):
    n = len(ws)

    def body(*refs):
        w_r, g_r, m_r, v_r = refs[:n], refs[n:2 * n], refs[2 * n:3 * n], refs[3 * n:4 * n]
        d_r, nm_r, nv_r = refs[4 * n:5 * n], refs[5 * n:6 * n], refs[6 * n:7 * n]
        for t in range(n):
            d_r[t][...], nm_r[t][...], nv_r[t][...] = _adamw(w_r[t][...], g_r[t][...], m_r[t][...], v_r[t][...])

    shapes = [jax.ShapeDtypeStruct(w.shape, f32) for w in ws]
    outs = pl.pallas_call(body, name="adamw_small", out_shape=shapes * 3)(*ws, *gs, *ms, *vs)
    return outs[:n], outs[n:2 * n], outs[2 * n:]


_WEIGHTS = ["e_norm_w", "e_w_in", "e_q_norm_w", "e_k_norm_w", "e_conv_w", "e_w_out", "o_norm_w", "o_w_in", "o_pool_w",
            "o_pool_scale", "o_dconv_w", "o_dconv_b", "o_ln_w", "o_ln_b", "o_w_out"]
_BIG = ["e_w_in", "e_w_out", "o_w_in", "o_w_out"]
_SMALL_SHARDED = ["e_conv_w", "o_norm_w", "o_pool_scale", "o_dconv_w", "o_dconv_b", "o_ln_w", "o_ln_b"]
_SMALL_ALL = ["e_norm_w", "e_q_norm_w", "e_k_norm_w", "e_conv_w", "o_norm_w", "o_pool_w", "o_pool_scale", "o_dconv_w",
              "o_dconv_b", "o_ln_w", "o_ln_b"]


def _pack_rows(pieces):
    rows, offs, r0 = [], [], 0
    for p in pieces:
        flat = p.reshape(-1)
        nr = -(-flat.shape[0] // (8 * LANES)) * 8
        rows.append(jnp.pad(flat, (0, nr * LANES - flat.shape[0])).reshape(nr, LANES))
        offs.append((r0, nr))
        r0 += nr
    return jnp.concatenate(rows, axis=0), offs


def _unpack_rows(buf, off, shape):
    r0, nr = off
    size = int(np.prod(shape))
    return buf[..., r0:r0 + nr, :].reshape(*buf.shape[:-2], nr * LANES)[..., :size].reshape(*buf.shape[:-2], *shape)


def kernel(x, positions, e_norm_w, e_w_in, e_q_norm_w, e_k_norm_w, e_conv_w, e_w_out, o_norm_w, o_w_in, o_pool_w, o_pool_scale, o_dconv_w, o_dconv_b, o_ln_w, o_ln_b, o_w_out, loss_target, m_e_norm_w, m_e_w_in, m_e_q_norm_w, m_e_k_norm_w, m_e_conv_w, m_e_w_out, m_o_norm_w, m_o_w_in, m_o_pool_w, m_o_pool_scale, m_o_dconv_w, m_o_dconv_b, m_o_ln_w, m_o_ln_b, m_o_w_out, v_e_norm_w, v_e_w_in, v_e_q_norm_w, v_e_k_norm_w, v_e_conv_w, v_e_w_out, v_o_norm_w, v_o_w_in, v_o_pool_w, v_o_pool_scale, v_o_dconv_w, v_o_dconv_b, v_o_ln_w, v_o_ln_b, v_o_w_out):
    w = dict(e_norm_w=e_norm_w, e_w_in=e_w_in, e_q_norm_w=e_q_norm_w, e_k_norm_w=e_k_norm_w, e_conv_w=e_conv_w,
             e_w_out=e_w_out, o_norm_w=o_norm_w, o_w_in=o_w_in, o_pool_w=o_pool_w, o_pool_scale=o_pool_scale,
             o_dconv_w=o_dconv_w, o_dconv_b=o_dconv_b, o_ln_w=o_ln_w, o_ln_b=o_ln_b, o_w_out=o_w_out)
    m = dict(e_norm_w=m_e_norm_w, e_w_in=m_e_w_in, e_q_norm_w=m_e_q_norm_w, e_k_norm_w=m_e_k_norm_w, e_conv_w=m_e_conv_w,
             e_w_out=m_e_w_out, o_norm_w=m_o_norm_w, o_w_in=m_o_w_in, o_pool_w=m_o_pool_w, o_pool_scale=m_o_pool_scale,
             o_dconv_w=m_o_dconv_w, o_dconv_b=m_o_dconv_b, o_ln_w=m_o_ln_w, o_ln_b=m_o_ln_b, o_w_out=m_o_w_out)
    v = dict(e_norm_w=v_e_norm_w, e_w_in=v_e_w_in, e_q_norm_w=v_e_q_norm_w, e_k_norm_w=v_e_k_norm_w, e_conv_w=v_e_conv_w,
             e_w_out=v_e_w_out, o_norm_w=v_o_norm_w, o_w_in=v_o_w_in, o_pool_w=v_o_pool_w, o_pool_scale=v_o_pool_scale,
             o_dconv_w=v_o_dconv_w, o_dconv_b=v_o_dconv_b, o_ln_w=v_o_ln_w, o_ln_b=v_o_ln_b, o_w_out=v_o_w_out)
    S = x.shape[1]
    me = 4 * lax.axis_index("x") + 2 * lax.axis_index("y") + lax.axis_index("c")

    small_local, small_offs = _pack_rows([w[n_] for n_ in _SMALL_SHARDED])
    pos_col = positions.reshape(S, 1)
    g_e_in, *tabs = _all_gather([w["e_w_in"][0].astype(bf16)], name="gather_e_w_in", rope_pos=pos_col)
    rest_local = [w["e_w_out"][0].astype(bf16), w["o_w_in"][0].astype(bf16), w["o_w_out"][0].astype(bf16), small_local]

    def unpack_rest(gathered):
        g_e_out, g_o_in, g_o_out, g_small = gathered
        full = {}
        for n_, off in zip(_SMALL_SHARDED, small_offs):
            shard = _unpack_rows(g_small, off, w[n_].shape[1:])
            full[n_] = jnp.moveaxis(shard, 0, -2).reshape(*shard.shape[1:-1], N_DEV * shard.shape[-1])
        return dict(
            e_conv_w=full["e_conv_w"], e_w_out=g_e_out.reshape(D_MODEL, D_MODEL), o_norm_w=full["o_norm_w"].reshape(1, D_MODEL),
            o_w_in=jnp.moveaxis(g_o_in, 0, 1).reshape(D_MODEL, ODD_IN), o_pool_scale=full["o_pool_scale"].reshape(1, 512),
            o_dconv_w=full["o_dconv_w"], o_dconv_b=full["o_dconv_b"].reshape(1, 512), o_ln_w=full["o_ln_w"].reshape(1, 512),
            o_ln_b=full["o_ln_b"].reshape(1, 512), o_w_out=g_o_out.reshape(D_MODEL, D_MODEL))

    loss_blk, grad_x, recv, small_where = _local_step(
        x[0], positions.reshape(S, 1), loss_target[0],
        dict(e_norm_w=w["e_norm_w"], e_w_in=g_e_in, e_q_norm_w=w["e_q_norm_w"], e_k_norm_w=w["e_k_norm_w"],
             o_pool_w=w["o_pool_w"][0]),
        dist=(rest_local, unpack_rest, tabs))

    out_g, out_d, out_m, out_v = {}, {}, {}, {}
    for n_ in _BIG:
        res = _sum_adamw(recv[n_], w[n_][0], m[n_][0], v[n_][0], name="adamw_" + n_)
        out_g[n_], out_d[n_], out_m[n_], out_v[n_] = [r[None] for r in res]
    small_sum = _sum_parts(recv["small_late"], name="sum_small_grads")
    last_sum = _sum_parts(recv["e_norm_w"], name="sum_e_norm_grad_loss")
    loss = last_sum[8, 0]
    gs = []
    for n_ in _SMALL_ALL:
        if n_ == "e_norm_w":
            gs.append(last_sum[:8].reshape(w[n_].shape))
            continue
        off, shape = small_where[n_]
        gfull = _unpack_rows(small_sum, off, shape)
        if n_ in _SMALL_SHARDED:
            width = w[n_].shape[-1]
            gfull = lax.dynamic_slice_in_dim(gfull, me * width, width, axis=gfull.ndim - 1)
        gs.append(gfull.reshape(w[n_].shape))
    ds, nms, nvs = _adamw_small([w[n_] for n_ in _SMALL_ALL], gs, [m[n_] for n_ in _SMALL_ALL], [v[n_] for n_ in _SMALL_ALL])
    for n_, g_, d_, nm_, nv_ in zip(_SMALL_ALL, gs, ds, nms, nvs):
        out_g[n_], out_d[n_], out_m[n_], out_v[n_] = g_, d_, nm_, nv_

    return (loss, grad_x[None], *[out_g[n_] for n_ in _WEIGHTS], *[out_d[n_] for n_ in _WEIGHTS],
            *[out_m[n_] for n_ in _WEIGHTS], *[out_v[n_] for n_ in _WEIGHTS])
```
